```python
import jax, jax.numpy as jnp
from jax import lax
import numpy as np

D_MODEL = 1024
BATCH = 8
SEQ = 8192
DEPTH = 4

CHUNK = 64
N_MIXERS = 2
EPS = 1e-6

SGU_BLOCK = 128
GMLP_HIDDEN = 2 * D_MODEL
SGU_GROUPS = 8
SGU_GROUP_DIM = GMLP_HIDDEN // SGU_GROUPS

N_HEADS = 16
HEAD_DIM = D_MODEL // N_HEADS
LEFT_CHUNKS = 8
BAND = (LEFT_CHUNKS + 1) * CHUNK
REL_MIN = -(CHUNK - 1)
REL_MAX = 128
N_REL = REL_MAX - REL_MIN + 1

D_FF = -(-(8 * D_MODEL) // (3 * 256)) * 256

N_A = (DEPTH + 1) // 2
N_B = DEPTH // 2

kernel_name = "hybrid_gmlp_chunkattn_trunk"


def rms_norm(x, g):
    xf = x.astype(jnp.float32)
    y = xf * lax.rsqrt(jnp.mean(xf * xf, axis=-1, keepdims=True) + EPS)
    return (y * g.astype(jnp.float32)).astype(x.dtype)


def gmlp_mixer(h, w_in, v_gain, w_s, b_s, w_out):
    bsz, seq, _ = h.shape
    uv = jax.nn.gelu(h @ w_in)
    u, v = jnp.split(uv, 2, axis=-1)
    v = rms_norm(v, v_gain)
    nb = seq // SGU_BLOCK
    v = v.reshape(bsz, nb, SGU_BLOCK, SGU_GROUPS, SGU_GROUP_DIM)
    pos = jnp.arange(SGU_BLOCK)
    mask = (pos[None, :] // CHUNK) <= (pos[:, None] // CHUNK)
    w = w_s * mask.astype(w_s.dtype)[None]
    v = jnp.einsum('gpq,bnqgc->bnpgc', w, v) + b_s.T[None, None, :, :, None]
    y = u * v.reshape(bsz, seq, GMLP_HIDDEN)
    return y @ w_out


def chunk_attention(h, w_qkv, rel_bias, w_out):
    bsz, seq, _ = h.shape
    nc = seq // CHUNK
    qkv = (h @ w_qkv).reshape(bsz, seq, 3, N_HEADS, HEAD_DIM)
    q, k, v = qkv[:, :, 0], qkv[:, :, 1], qkv[:, :, 2]
    pad = LEFT_CHUNKS * CHUNK
    kp = jnp.pad(k, ((0, 0), (pad, 0), (0, 0), (0, 0)))
    vp = jnp.pad(v, ((0, 0), (pad, 0), (0, 0), (0, 0)))
    qi = jnp.arange(CHUNK)[:, None]
    kj = jnp.arange(BAND)[None, :]
    rel_idx = jnp.clip(qi - (kj - pad), REL_MIN, REL_MAX) - REL_MIN
    bias = rel_bias[:, rel_idx].astype(jnp.float32)
    scale = HEAD_DIM ** -0.5

    def one_chunk(c):
        start = c * CHUNK
        qc = lax.dynamic_slice_in_dim(q, start, CHUNK, axis=1)
        kc = lax.dynamic_slice_in_dim(kp, start, BAND, axis=1)
        vc = lax.dynamic_slice_in_dim(vp, start, BAND, axis=1)
        s = jnp.einsum('bqhd,bkhd->bhqk', qc, kc).astype(jnp.float32) * scale + bias
        valid = kj >= pad - start
        s = jnp.where(valid[None, None], s, -jnp.inf)
        p = jax.nn.softmax(s, axis=-1).astype(vc.dtype)
        return jnp.einsum('bhqk,bkhd->bqhd', p, vc)

    o = lax.map(one_chunk, jnp.arange(nc))
    o = jnp.moveaxis(o, 0, 1).reshape(bsz, seq, N_HEADS * HEAD_DIM)
    return o @ w_out


def swiglu(h, w_gate, w_up, w_down):
    return (jax.nn.silu(h @ w_gate) * (h @ w_up)) @ w_down


def _fwd_setup_inputs(seed: int = 0) -> dict:
    key = jax.random.key(seed)
    ks = jax.random.split(key, 16)
    f32 = jnp.float32

    def nrm(k, shape, scale):
        return jax.random.normal(k, shape, f32) * scale

    return {
        "x": jax.random.normal(ks[0], (BATCH, SEQ, D_MODEL), f32),
        "norm_mix_g": 1.0 + nrm(ks[1], (DEPTH, D_MODEL), 0.05),
        "norm_ffn_g": 1.0 + nrm(ks[2], (DEPTH, D_MODEL), 0.05),
        "final_g": 1.0 + nrm(ks[3], (D_MODEL,), 0.05),
        "a_w_in": nrm(ks[4], (N_A, D_MODEL, 2 * GMLP_HIDDEN), D_MODEL ** -0.5),
        "a_v_gain": 1.0 + nrm(ks[5], (N_A, GMLP_HIDDEN), 0.05),
        "a_w_s": nrm(ks[6], (N_A, SGU_GROUPS, SGU_BLOCK, SGU_BLOCK), 0.5 * SGU_BLOCK ** -0.5),
        "a_b_s": 1.0 + nrm(ks[7], (N_A, SGU_GROUPS, SGU_BLOCK), 0.1),
        "a_w_out": nrm(ks[8], (N_A, GMLP_HIDDEN, D_MODEL), GMLP_HIDDEN ** -0.5),
        "b_w_qkv": nrm(ks[9], (N_B, D_MODEL, 3 * N_HEADS * HEAD_DIM), D_MODEL ** -0.5),
        "b_rel_bias": nrm(ks[10], (N_B, N_HEADS, N_REL), 0.5),
        "b_w_out": nrm(ks[11], (N_B, N_HEADS * HEAD_DIM, D_MODEL), (N_HEADS * HEAD_DIM) ** -0.5),
        "ffn_w_gate": nrm(ks[12], (DEPTH, D_MODEL, D_FF), D_MODEL ** -0.5),
        "ffn_w_up": nrm(ks[13], (DEPTH, D_MODEL, D_FF), D_MODEL ** -0.5),
        "ffn_w_down": nrm(ks[14], (DEPTH, D_FF, D_MODEL), D_FF ** -0.5),
    }


def _fwd_reference(x, norm_mix_g, norm_ffn_g, final_g, a_w_in, a_v_gain, a_w_s, a_b_s,
              a_w_out, b_w_qkv, b_rel_bias, b_w_out, ffn_w_gate, ffn_w_up, ffn_w_down):
    for i in range(DEPTH):
        j = i // N_MIXERS
        hn = rms_norm(x, norm_mix_g[i])
        if i % N_MIXERS == 0:
            x = x + gmlp_mixer(hn, a_w_in[j], a_v_gain[j], a_w_s[j], a_b_s[j], a_w_out[j])
        else:
            x = x + chunk_attention(hn, b_w_qkv[j], b_rel_bias[j], b_w_out[j])
        hn = rms_norm(x, norm_ffn_g[i])
        x = x + swiglu(hn, ffn_w_gate[i], ffn_w_up[i], ffn_w_down[i])
    return rms_norm(x, final_g)


import jax as _jax
import jax.numpy as _jnp

TWIN_FORMAT = 'train_step'
FWD_PARAMS = ['x', 'norm_mix_g', 'norm_ffn_g', 'final_g', 'a_w_in', 'a_v_gain', 'a_w_s', 'a_b_s', 'a_w_out', 'b_w_qkv', 'b_rel_bias', 'b_w_out', 'ffn_w_gate', 'ffn_w_up', 'ffn_w_down']
TWIN_WEIGHTS = ['norm_mix_g', 'norm_ffn_g', 'final_g', 'a_w_in', 'a_v_gain', 'a_w_s', 'a_b_s', 'a_w_out', 'b_w_qkv', 'b_rel_bias', 'b_w_out', 'ffn_w_gate', 'ffn_w_up', 'ffn_w_down']
TWIN_DIFF_INPUT = 'x'
TWIN_INPUTS = ['x', 'norm_mix_g', 'norm_ffn_g', 'final_g', 'a_w_in', 'a_v_gain', 'a_w_s', 'a_b_s', 'a_w_out', 'b_w_qkv', 'b_rel_bias', 'b_w_out', 'ffn_w_gate', 'ffn_w_up', 'ffn_w_down', 'loss_target', 'm_norm_mix_g', 'm_norm_ffn_g', 'm_final_g', 'm_a_w_in', 'm_a_v_gain', 'm_a_w_s', 'm_a_b_s', 'm_a_w_out', 'm_b_w_qkv', 'm_b_rel_bias', 'm_b_w_out', 'm_ffn_w_gate', 'm_ffn_w_up', 'm_ffn_w_down', 'v_norm_mix_g', 'v_norm_ffn_g', 'v_final_g', 'v_a_w_in', 'v_a_v_gain', 'v_a_w_s', 'v_a_b_s', 'v_a_w_out', 'v_b_w_qkv', 'v_b_rel_bias', 'v_b_w_out', 'v_ffn_w_gate', 'v_ffn_w_up', 'v_ffn_w_down']
TWIN_OUTPUTS = ['loss', 'grad_x', 'grad_norm_mix_g', 'grad_norm_ffn_g', 'grad_final_g', 'grad_a_w_in', 'grad_a_v_gain', 'grad_a_w_s', 'grad_a_b_s', 'grad_a_w_out', 'grad_b_w_qkv', 'grad_b_rel_bias', 'grad_b_w_out', 'grad_ffn_w_gate', 'grad_ffn_w_up', 'grad_ffn_w_down', 'delta_norm_mix_g', 'delta_norm_ffn_g', 'delta_final_g', 'delta_a_w_in', 'delta_a_v_gain', 'delta_a_w_s', 'delta_a_b_s', 'delta_a_w_out', 'delta_b_w_qkv', 'delta_b_rel_bias', 'delta_b_w_out', 'delta_ffn_w_gate', 'delta_ffn_w_up', 'delta_ffn_w_down', 'new_m_norm_mix_g', 'new_m_norm_ffn_g', 'new_m_final_g', 'new_m_a_w_in', 'new_m_a_v_gain', 'new_m_a_w_s', 'new_m_a_b_s', 'new_m_a_w_out', 'new_m_b_w_qkv', 'new_m_b_rel_bias', 'new_m_b_w_out', 'new_m_ffn_w_gate', 'new_m_ffn_w_up', 'new_m_ffn_w_down', 'new_v_norm_mix_g', 'new_v_norm_ffn_g', 'new_v_final_g', 'new_v_a_w_in', 'new_v_a_v_gain', 'new_v_a_w_s', 'new_v_a_b_s', 'new_v_a_w_out', 'new_v_b_w_qkv', 'new_v_b_rel_bias', 'new_v_b_w_out', 'new_v_ffn_w_gate', 'new_v_ffn_w_up', 'new_v_ffn_w_down']
TWIN_LEAF_KINDS = {'loss': 'loss', 'grad_x': 'grad_x', 'grad_norm_mix_g': 'grad_w', 'grad_norm_ffn_g': 'grad_w', 'grad_final_g': 'grad_w', 'grad_a_w_in': 'grad_w', 'grad_a_v_gain': 'grad_w', 'grad_a_w_s': 'grad_w', 'grad_a_b_s': 'grad_w', 'grad_a_w_out': 'grad_w', 'grad_b_w_qkv': 'grad_w', 'grad_b_rel_bias': 'grad_w', 'grad_b_w_out': 'grad_w', 'grad_ffn_w_gate': 'grad_w', 'grad_ffn_w_up': 'grad_w', 'grad_ffn_w_down': 'grad_w', 'delta_norm_mix_g': 'delta_w', 'delta_norm_ffn_g': 'delta_w', 'delta_final_g': 'delta_w', 'delta_a_w_in': 'delta_w', 'delta_a_v_gain': 'delta_w', 'delta_a_w_s': 'delta_w', 'delta_a_b_s': 'delta_w', 'delta_a_w_out': 'delta_w', 'delta_b_w_qkv': 'delta_w', 'delta_b_rel_bias': 'delta_w', 'delta_b_w_out': 'delta_w', 'delta_ffn_w_gate': 'delta_w', 'delta_ffn_w_up': 'delta_w', 'delta_ffn_w_down': 'delta_w', 'new_m_norm_mix_g': 'new_m', 'new_m_norm_ffn_g': 'new_m', 'new_m_final_g': 'new_m', 'new_m_a_w_in': 'new_m', 'new_m_a_v_gain': 'new_m', 'new_m_a_w_s': 'new_m', 'new_m_a_b_s': 'new_m', 'new_m_a_w_out': 'new_m', 'new_m_b_w_qkv': 'new_m', 'new_m_b_rel_bias': 'new_m', 'new_m_b_w_out': 'new_m', 'new_m_ffn_w_gate': 'new_m', 'new_m_ffn_w_up': 'new_m', 'new_m_ffn_w_down': 'new_m', 'new_v_norm_mix_g': 'new_v', 'new_v_norm_ffn_g': 'new_v', 'new_v_final_g': 'new_v', 'new_v_a_w_in': 'new_v', 'new_v_a_v_gain': 'new_v', 'new_v_a_w_s': 'new_v', 'new_v_a_b_s': 'new_v', 'new_v_a_w_out': 'new_v', 'new_v_b_w_qkv': 'new_v', 'new_v_b_rel_bias': 'new_v', 'new_v_b_w_out': 'new_v', 'new_v_ffn_w_gate': 'new_v', 'new_v_ffn_w_up': 'new_v', 'new_v_ffn_w_down': 'new_v'}


def _forward(args):
    return _fwd_reference(*[args[k] for k in FWD_PARAMS])


def _output_shape():
    def fwd():
        inp = _fwd_setup_inputs(0)
        return _fwd_reference(*[inp[k] for k in FWD_PARAMS])
    out = _jax.eval_shape(fwd)
    return out.shape, out.dtype

N_MICROBATCH = 1
ADAM_LR = 0.001
ADAM_B1 = 0.9
ADAM_B2 = 0.999
ADAM_EPS = 1e-08
ADAM_WD = 0.01
ADAM_STEP = 10
PER_EXAMPLE_BATCH_AXIS = {'x': 0, 'loss_target': 0}
SHARED_INPUTS = []
_WEIGHT_DTYPES = {'norm_mix_g': _jnp.float32, 'norm_ffn_g': _jnp.float32, 'final_g': _jnp.float32, 'a_w_in': _jnp.float32, 'a_v_gain': _jnp.float32, 'a_w_s': _jnp.float32, 'a_b_s': _jnp.float32, 'a_w_out': _jnp.float32, 'b_w_qkv': _jnp.float32, 'b_rel_bias': _jnp.float32, 'b_w_out': _jnp.float32, 'ffn_w_gate': _jnp.float32, 'ffn_w_up': _jnp.float32, 'ffn_w_down': _jnp.float32}
MOMENT_SCALE = {'norm_mix_g': 1.752919e-01, 'norm_ffn_g': 1.648327e-01, 'final_g': 6.444465e+01, 'a_w_in': 9.365522e-02, 'a_v_gain': 4.503176e-02, 'a_w_s': 1.277012e-01, 'a_b_s': 1.494761e-01, 'a_w_out': 2.956050e-01, 'b_w_qkv': 9.500880e-02, 'b_rel_bias': 1.538980e-02, 'b_w_out': 1.797995e-01, 'ffn_w_gate': 6.797901e-02, 'ffn_w_up': 6.954446e-02, 'ffn_w_down': 1.154753e-01}


def _to_microbatches(a, axis):
    t = _jnp.moveaxis(a, axis, 0)
    t = t.reshape((N_MICROBATCH, t.shape[0] // N_MICROBATCH) + t.shape[1:])
    return _jnp.moveaxis(t, 1, axis + 1)


def setup_inputs(seed: int = 0) -> dict:
    inp = _fwd_setup_inputs(seed)
    key = _jax.random.fold_in(_jax.random.key(seed), 7919)
    shape, _ = _output_shape()
    out = dict(inp)
    out["loss_target"] = _jax.random.normal(_jax.random.fold_in(key, 0), shape, _jnp.float32)
    for i, name in enumerate(TWIN_WEIGHTS):
        w = inp[name].astype(_jnp.float32)
        if MOMENT_SCALE is None:
            s = _jnp.sqrt(_jnp.mean(_jnp.square(w)) + 1e-30)
        else:
            s = MOMENT_SCALE[name]
        km, kv = _jax.random.split(_jax.random.fold_in(key, i + 1))
        out[name] = w
        out["m_" + name] = s * _jax.random.normal(km, w.shape, _jnp.float32)
        out["v_" + name] = (s * s) * _jax.random.uniform(kv, w.shape, _jnp.float32, 0.5, 1.5)
    if N_MICROBATCH > 1:
        for name, axis in PER_EXAMPLE_BATCH_AXIS.items():
            out[name] = _to_microbatches(out[name], axis)
    return {'x': out['x'], 'norm_mix_g': out['norm_mix_g'], 'norm_ffn_g': out['norm_ffn_g'], 'final_g': out['final_g'], 'a_w_in': out['a_w_in'], 'a_v_gain': out['a_v_gain'], 'a_w_s': out['a_w_s'], 'a_b_s': out['a_b_s'], 'a_w_out': out['a_w_out'], 'b_w_qkv': out['b_w_qkv'], 'b_rel_bias': out['b_rel_bias'], 'b_w_out': out['b_w_out'], 'ffn_w_gate': out['ffn_w_gate'], 'ffn_w_up': out['ffn_w_up'], 'ffn_w_down': out['ffn_w_down'], 'loss_target': out['loss_target'], 'm_norm_mix_g': out['m_norm_mix_g'], 'm_norm_ffn_g': out['m_norm_ffn_g'], 'm_final_g': out['m_final_g'], 'm_a_w_in': out['m_a_w_in'], 'm_a_v_gain': out['m_a_v_gain'], 'm_a_w_s': out['m_a_w_s'], 'm_a_b_s': out['m_a_b_s'], 'm_a_w_out': out['m_a_w_out'], 'm_b_w_qkv': out['m_b_w_qkv'], 'm_b_rel_bias': out['m_b_rel_bias'], 'm_b_w_out': out['m_b_w_out'], 'm_ffn_w_gate': out['m_ffn_w_gate'], 'm_ffn_w_up': out['m_ffn_w_up'], 'm_ffn_w_down': out['m_ffn_w_down'], 'v_norm_mix_g': out['v_norm_mix_g'], 'v_norm_ffn_g': out['v_norm_ffn_g'], 'v_final_g': out['v_final_g'], 'v_a_w_in': out['v_a_w_in'], 'v_a_v_gain': out['v_a_v_gain'], 'v_a_w_s': out['v_a_w_s'], 'v_a_b_s': out['v_a_b_s'], 'v_a_w_out': out['v_a_w_out'], 'v_b_w_qkv': out['v_b_w_qkv'], 'v_b_rel_bias': out['v_b_rel_bias'], 'v_b_w_out': out['v_b_w_out'], 'v_ffn_w_gate': out['v_ffn_w_gate'], 'v_ffn_w_up': out['v_ffn_w_up'], 'v_ffn_w_down': out['v_ffn_w_down']}


def _loss(weights, diff, rest, loss_target):
    with _jax.named_scope("forward"):
        args = {**rest, TWIN_DIFF_INPUT: diff, **{k: w.astype(_WEIGHT_DTYPES[k]) for k, w in weights.items()}}
        y = _forward(args)
    with _jax.named_scope("loss_head"):
        err = _jnp.square(y.astype(_jnp.float32) - loss_target)
        return 0.5 * _jnp.sum(_jnp.mean(err, axis=-1)) if err.ndim else 0.5 * err


def _adamw(w, g, m, v):
    m = ADAM_B1 * m + (1.0 - ADAM_B1) * g
    v = ADAM_B2 * v + (1.0 - ADAM_B2) * _jnp.square(g)
    m_hat = m / (1.0 - ADAM_B1 ** ADAM_STEP)
    v_hat = v / (1.0 - ADAM_B2 ** ADAM_STEP)
    delta = -ADAM_LR * (m_hat / (_jnp.sqrt(v_hat) + ADAM_EPS) + ADAM_WD * w)
    return delta, m, v


def reference(x, norm_mix_g, norm_ffn_g, final_g, a_w_in, a_v_gain, a_w_s, a_b_s, a_w_out, b_w_qkv, b_rel_bias, b_w_out, ffn_w_gate, ffn_w_up, ffn_w_down, loss_target, m_norm_mix_g, m_norm_ffn_g, m_final_g, m_a_w_in, m_a_v_gain, m_a_w_s, m_a_b_s, m_a_w_out, m_b_w_qkv, m_b_rel_bias, m_b_w_out, m_ffn_w_gate, m_ffn_w_up, m_ffn_w_down, v_norm_mix_g, v_norm_ffn_g, v_final_g, v_a_w_in, v_a_v_gain, v_a_w_s, v_a_b_s, v_a_w_out, v_b_w_qkv, v_b_rel_bias, v_b_w_out, v_ffn_w_gate, v_ffn_w_up, v_ffn_w_down):
    given = dict(x=x, norm_mix_g=norm_mix_g, norm_ffn_g=norm_ffn_g, final_g=final_g, a_w_in=a_w_in, a_v_gain=a_v_gain, a_w_s=a_w_s, a_b_s=a_b_s, a_w_out=a_w_out, b_w_qkv=b_w_qkv, b_rel_bias=b_rel_bias, b_w_out=b_w_out, ffn_w_gate=ffn_w_gate, ffn_w_up=ffn_w_up, ffn_w_down=ffn_w_down, loss_target=loss_target, m_norm_mix_g=m_norm_mix_g, m_norm_ffn_g=m_norm_ffn_g, m_final_g=m_final_g, m_a_w_in=m_a_w_in, m_a_v_gain=m_a_v_gain, m_a_w_s=m_a_w_s, m_a_b_s=m_a_b_s, m_a_w_out=m_a_w_out, m_b_w_qkv=m_b_w_qkv, m_b_rel_bias=m_b_rel_bias, m_b_w_out=m_b_w_out, m_ffn_w_gate=m_ffn_w_gate, m_ffn_w_up=m_ffn_w_up, m_ffn_w_down=m_ffn_w_down, v_norm_mix_g=v_norm_mix_g, v_norm_ffn_g=v_norm_ffn_g, v_final_g=v_final_g, v_a_w_in=v_a_w_in, v_a_v_gain=v_a_v_gain, v_a_w_s=v_a_w_s, v_a_b_s=v_a_b_s, v_a_w_out=v_a_w_out, v_b_w_qkv=v_b_w_qkv, v_b_rel_bias=v_b_rel_bias, v_b_w_out=v_b_w_out, v_ffn_w_gate=v_ffn_w_gate, v_ffn_w_up=v_ffn_w_up, v_ffn_w_down=v_ffn_w_down)
    weights = {n: given[n] for n in TWIN_WEIGHTS}
    shared = {n: given[n] for n in SHARED_INPUTS}
    per_example = {n: given[n] for n in ['x']}
    grad_fn = _jax.value_and_grad(_loss, argnums=(0, 1))

    def one_microbatch(ex, loss_target):
        ex = dict(ex)
        diff = ex.pop(TWIN_DIFF_INPUT)
        return grad_fn(weights, diff, {**shared, **ex}, loss_target)

    if N_MICROBATCH == 1:
        loss, (grad_w, grad_x) = one_microbatch(per_example, given["loss_target"])
    else:
        def body(carry, xs):
            loss_sum, grad_sum = carry
            l_k, (gw_k, gx_k) = one_microbatch(xs[0], xs[1])
            with _jax.named_scope("update"):
                return (loss_sum + l_k, _jax.tree.map(_jnp.add, grad_sum, gw_k)), gx_k

        init = (_jnp.zeros((), _jnp.float32), _jax.tree.map(_jnp.zeros_like, weights))
        (loss, grad_w), grad_x = _jax.lax.scan(body, init, (per_example, given["loss_target"]))
    with _jax.named_scope("update"):
        delta_w, new_m, new_v = {}, {}, {}
        for n in TWIN_WEIGHTS:
            delta_w[n], new_m[n], new_v[n] = _adamw(weights[n], grad_w[n], given["m_" + n], given["v_" + n])
    return (loss, grad_x, *[grad_w[n] for n in TWIN_WEIGHTS], *[delta_w[n] for n in TWIN_WEIGHTS],
            *[new_m[n] for n in TWIN_WEIGHTS], *[new_v[n] for n in TWIN_WEIGHTS])
```

```python
import functools

import jax
import jax.numpy as jnp
from jax import lax
from jax.experimental import pallas as pl
from jax.experimental.pallas import tpu as pltpu

F32 = jnp.float32
BF16 = jnp.bfloat16
MESH = pl.DeviceIdType.MESH

D = 1024
DEPTH = 4
EPS = 1e-6
SGU_BLOCK = 128
GH = 2048
SGU_G = 8
SGU_GD = GH // SGU_G
N_HEADS = 16
HEAD_DIM = 64
CHUNK = 64
PAD = 8 * CHUNK
QB = 128
KW = PAD + QB
N_REL = 192
REL_MIN = -(CHUNK - 1)
REL_MAX = 128
D_FF = 2816
FS = D_FF // 4
NEG = -1e30
SCALE = HEAD_DIM ** -0.5
N_CHIPS = 4

ADAM_LR = 0.001
ADAM_B1 = 0.9
ADAM_B2 = 0.999
ADAM_EPS = 1e-08
ADAM_WD = 0.01
ADAM_STEP = 10

VMEM_BIG = 56 * 1024 * 1024

NN = ((1,), (0,))
NT = ((1,), (1,))
TN = ((0,), (0,))


def _dot(a, b, dims):
    return lax.dot_general(a, b, (dims, ((), ())), preferred_element_type=F32)


def _pallas(body, **kw):
    return pl.pallas_call(body, **kw)


def _params(sem=None, vmem=None):
    return pltpu.CompilerParams(dimension_semantics=sem, vmem_limit_bytes=vmem)


def _sds(shape, dtype):
    return jax.ShapeDtypeStruct(tuple(shape), dtype)


_GELU_C = 0.7978845608028654


def _gelu(x):
    t = jnp.tanh(_GELU_C * (x + 0.044715 * (x * x * x)))
    return 0.5 * x * (1.0 + t)


def _gelu_and_grad(x):
    x2 = x * x
    t = jnp.tanh(_GELU_C * (x + 0.044715 * (x2 * x)))
    val = 0.5 * x * (1.0 + t)
    grad = 0.5 * (1.0 + t) + 0.5 * x * (1.0 - t * t) * (_GELU_C * (1.0 + 3.0 * 0.044715 * x2))
    return val, grad


def _sigmoid(x):
    return 1.0 / (1.0 + jnp.exp(-x))


def cast_bf16(w, name):
    L, R, C = w.shape

    def body(w_ref, o_ref):
        o_ref[...] = w_ref[...].astype(BF16)

    spec = pl.BlockSpec((None, R, C), lambda l: (l, 0, 0))
    return _pallas(body, name=name, grid=(L,), in_specs=[spec], out_specs=spec,
                   out_shape=_sds((L, R, C), BF16), compiler_params=_params(("parallel",)))(w)


def rms_fwd(x, g, name, tm=512):
    T = x.shape[0]

    def body(x_ref, g_ref, o_ref):
        xf = x_ref[...]
        r = lax.rsqrt(jnp.mean(xf * xf, axis=-1, keepdims=True) + EPS)
        o_ref[...] = ((xf * r) * g_ref[...]).astype(BF16)

    row = pl.BlockSpec((tm, D), lambda i: (i, 0))
    return _pallas(body, name=name, grid=(T // tm,),
                   in_specs=[row, pl.BlockSpec((1, D), lambda i: (0, 0))], out_specs=row,
                   out_shape=_sds((T, D), BF16), compiler_params=_params(("parallel",)))(x, g)


def rms_bwd(x, dh, g, dres, name, tm=256):
    T = x.shape[0]
    n = T // tm

    def body(x_ref, dh_ref, g_ref, dres_ref, dx_ref, dxb_ref, dg_ref, acc_ref):
        i = pl.program_id(0)
        xf = x_ref[...]
        r = lax.rsqrt(jnp.mean(xf * xf, axis=-1, keepdims=True) + EPS)
        xhat = xf * r
        dhf = dh_ref[...]
        part = (dhf * xhat).reshape(tm // 8, 8, D).sum(axis=0)

        @pl.when(i == 0)
        def _():
            acc_ref[...] = part

        @pl.when(i > 0)
        def _():
            acc_ref[...] += part

        dxhat = dhf * g_ref[...]
        dx = dres_ref[...] + r * (dxhat - xhat * jnp.mean(dxhat * xhat, axis=-1, keepdims=True))
        dx_ref[...] = dx
        dxb_ref[...] = dx.astype(BF16)

        @pl.when(i == n - 1)
        def _():
            dg_ref[...] = jnp.sum(acc_ref[...], axis=0, keepdims=True)

    row = pl.BlockSpec((tm, D), lambda i: (i, 0))
    vec = pl.BlockSpec((1, D), lambda i: (0, 0))
    return _pallas(body, name=name, grid=(n,), in_specs=[row, row, vec, row], out_specs=[row, row, vec],
                   out_shape=[_sds((T, D), F32), _sds((T, D), BF16), _sds((1, D), F32)],
                   scratch_shapes=[pltpu.VMEM((8, D), F32)],
                   compiler_params=_params(("arbitrary",)))(x, dh, g, dres)


def final_loss(x, g, tgt, name, tm=256):
    T = x.shape[0]
    n = T // tm

    def body(x_ref, g_ref, t_ref, loss_ref, dx_ref, dxb_ref, dg_ref, acc_ref, lacc_ref):
        i = pl.program_id(0)
        xf = x_ref[...]
        r = lax.rsqrt(jnp.mean(xf * xf, axis=-1, keepdims=True) + EPS)
        xhat = xf * r
        gg = g_ref[...]
        e = xhat * gg - t_ref[...]
        dy = e * (1.0 / D)
        part = (dy * xhat).reshape(tm // 8, 8, D).sum(axis=0)
        lpart = (e * e).reshape(tm // 8, 8, D).sum(axis=0)

        @pl.when(i == 0)
        def _():
            acc_ref[...] = part
            lacc_ref[...] = lpart

        @pl.when(i > 0)
        def _():
            acc_ref[...] += part
            lacc_ref[...] += lpart

        dxhat = dy * gg
        dx = r * (dxhat - xhat * jnp.mean(dxhat * xhat, axis=-1, keepdims=True))
        dx_ref[...] = dx
        dxb_ref[...] = dx.astype(BF16)

        @pl.when(i == n - 1)
        def _():
            dg_ref[...] = jnp.sum(acc_ref[...], axis=0, keepdims=True)
            total = jnp.sum(jnp.sum(lacc_ref[...], axis=0, keepdims=True), axis=1, keepdims=True)
            loss_ref[...] = jnp.broadcast_to(total * (0.5 / D), (1, 128))

    row = pl.BlockSpec((tm, D), lambda i: (i, 0))
    vec = pl.BlockSpec((1, D), lambda i: (0, 0))
    return _pallas(body, name=name, grid=(n,), in_specs=[row, vec, row],
                   out_specs=[pl.BlockSpec((1, 128), lambda i: (0, 0)), row, row, vec],
                   out_shape=[_sds((1, 128), F32), _sds((T, D), F32), _sds((T, D), BF16), _sds((1, D), F32)],
                   scratch_shapes=[pltpu.VMEM((8, D), F32), pltpu.VMEM((8, D), F32)],
                   compiler_params=_params(("arbitrary",)))(x, g, tgt)


def matmul(name, dims, a, a_spec, b, b_spec, out_shape, out_spec, grid, *, acc=False, res=None, res_spec=None):
    has_res = res is not None

    def body(*refs):
        a_ref, b_ref = refs[0], refs[1]
        r_ref = refs[2] if has_res else None
        o_ref = refs[-1]
        d = _dot(a_ref[...], b_ref[...], dims)
        if not acc:
            if has_res:
                d = d + r_ref[...]
            o_ref[...] = d.astype(o_ref.dtype)
        else:
            k = pl.program_id(len(grid) - 1)

            @pl.when(k == 0)
            def _():
                o_ref[...] = (d + r_ref[...]) if has_res else d

            @pl.when(k > 0)
            def _():
                o_ref[...] += d

    sem = ("parallel",) * (len(grid) - 1) + (("arbitrary",) if acc else ("parallel",))
    ins = [a, b] + ([res] if has_res else [])
    specs = [a_spec, b_spec] + ([res_spec] if has_res else [])
    return _pallas(body, name=name, grid=grid, in_specs=specs, out_specs=out_spec, out_shape=out_shape,
                   compiler_params=_params(sem, VMEM_BIG))(*ins)


def wgrad(name, a, a_spec, b, b_spec, out_shape, out_spec, J, T, tk=512):
    return matmul(name, TN, a, a_spec, b, b_spec, out_shape, out_spec, (J, T // tk), acc=True)


def _sgu_mask():
    p = lax.broadcasted_iota(jnp.int32, (SGU_BLOCK, SGU_BLOCK), 0)
    q = lax.broadcasted_iota(jnp.int32, (SGU_BLOCK, SGU_BLOCK), 1)
    return lax.shift_right_logical(q, 6) <= lax.shift_right_logical(p, 6)


def sgu_fwd(pre, gain, w_s, b_s, name):
    T = pre.shape[0]

    def body(pre_ref, gain_ref, ws_ref, bs_ref, y_ref):
        mask = _sgu_mask()
        u = _gelu(pre_ref[:, :GH].astype(F32))
        va = _gelu(pre_ref[:, GH:].astype(F32))
        r = lax.rsqrt(jnp.mean(va * va, axis=-1, keepdims=True) + EPS)
        vn = ((va * r) * gain_ref[...]).astype(BF16)
        for g in range(SGU_G):
            sl = slice(g * SGU_GD, (g + 1) * SGU_GD)
            wm = jnp.where(mask, ws_ref[g], 0.0).astype(BF16)
            vm = _dot(wm, vn[:, sl], NN) + bs_ref[g]
            y_ref[:, sl] = (u[:, sl] * vm).astype(BF16)

    return _pallas(
        body, name=name, grid=(T // SGU_BLOCK,),
        in_specs=[pl.BlockSpec((SGU_BLOCK, 2 * GH), lambda i: (i, 0)),
                  pl.BlockSpec((1, GH), lambda i: (0, 0)),
                  pl.BlockSpec((SGU_G, SGU_BLOCK, SGU_BLOCK), lambda i: (0, 0, 0)),
                  pl.BlockSpec((SGU_G, SGU_BLOCK, 1), lambda i: (0, 0, 0))],
        out_specs=pl.BlockSpec((SGU_BLOCK, GH), lambda i: (i, 0)),
        out_shape=_sds((T, GH), BF16), compiler_params=_params(("parallel",)))(pre, gain, w_s, b_s)


def sgu_bwd(pre, dy, gain, w_s, b_s, name):
    T = pre.shape[0]
    n = T // SGU_BLOCK

    def body(pre_ref, dy_ref, gain_ref, ws_ref, bs_ref, dpre_ref, dws_ref, dbs_ref, dgain_ref, gacc_ref):
        i = pl.program_id(0)

        @pl.when(i == 0)
        def _():
            dws_ref[...] = jnp.zeros_like(dws_ref)
            dbs_ref[...] = jnp.zeros_like(dbs_ref)
            gacc_ref[...] = jnp.zeros_like(gacc_ref)

        mask = _sgu_mask()
        u, du_dpre = _gelu_and_grad(pre_ref[:, :GH].astype(F32))
        va, dva_dpre = _gelu_and_grad(pre_ref[:, GH:].astype(F32))
        r = lax.rsqrt(jnp.mean(va * va, axis=-1, keepdims=True) + EPS)
        vhat = va * r
        gain_v = gain_ref[...]
        vn = (vhat * gain_v).astype(BF16)
        dyf = dy_ref[...].astype(F32)
        dvn_parts = []
        for g in range(SGU_G):
            sl = slice(g * SGU_GD, (g + 1) * SGU_GD)
            wm = jnp.where(mask, ws_ref[g], 0.0).astype(BF16)
            vm = _dot(wm, vn[:, sl], NN) + bs_ref[g]
            dpre_ref[:, sl] = ((dyf[:, sl] * vm) * du_dpre[:, sl]).astype(BF16)
            dvm = dyf[:, sl] * u[:, sl]
            dbs_ref[g] += jnp.sum(dvm, axis=-1, keepdims=True)
            dvm16 = dvm.astype(BF16)
            dws_ref[g] += jnp.where(mask, _dot(dvm16, vn[:, sl], NT), 0.0)
            dvn_parts.append(_dot(wm, dvm16, TN))
        dvn = jnp.concatenate(dvn_parts, axis=-1)
        gacc_ref[...] += (dvn * vhat).reshape(SGU_BLOCK // 8, 8, GH).sum(axis=0)
        dvhat = dvn * gain_v
        dva = r * (dvhat - vhat * jnp.mean(dvhat * vhat, axis=-1, keepdims=True))
        dpre_ref[:, GH:] = (dva * dva_dpre).astype(BF16)

        @pl.when(i == n - 1)
        def _():
            dgain_ref[...] = jnp.sum(gacc_ref[...], axis=0, keepdims=True)

    const3 = lambda i: (0, 0, 0)
    return _pallas(
        body, name=name, grid=(n,),
        in_specs=[pl.BlockSpec((SGU_BLOCK, 2 * GH), lambda i: (i, 0)),
                  pl.BlockSpec((SGU_BLOCK, GH), lambda i: (i, 0)),
                  pl.BlockSpec((1, GH), lambda i: (0, 0)),
                  pl.BlockSpec((SGU_G, SGU_BLOCK, SGU_BLOCK), const3),
                  pl.BlockSpec((SGU_G, SGU_BLOCK, 1), const3)],
        out_specs=[pl.BlockSpec((SGU_BLOCK, 2 * GH), lambda i: (i, 0)),
                   pl.BlockSpec((SGU_G, SGU_BLOCK, SGU_BLOCK), const3),
                   pl.BlockSpec((SGU_G, SGU_BLOCK, 1), const3),
                   pl.BlockSpec((1, GH), lambda i: (0, 0))],
        out_shape=[_sds((T, 2 * GH), BF16), _sds((SGU_G, SGU_BLOCK, SGU_BLOCK), F32),
                   _sds((SGU_G, SGU_BLOCK, 1), F32), _sds((1, GH), F32)],
        scratch_shapes=[pltpu.VMEM((8, GH), F32)],
        compiler_params=_params(("arbitrary",)))(pre, dy, gain, w_s, b_s)


def _rel_onehot(i):
    j = lax.broadcasted_iota(jnp.int32, (N_REL, KW), 1)
    r = lax.broadcasted_iota(jnp.int32, (N_REL, KW), 0)
    idx = jnp.clip(i - j + PAD, REL_MIN, REL_MAX) - REL_MIN
    return (idx == r).astype(BF16)


def _split3(v):
    hi = v.astype(BF16)
    r1 = v - hi.astype(F32)
    mid = r1.astype(BF16)
    lo = (r1 - mid.astype(F32)).astype(BF16)
    return hi, mid, lo


def bias_build(rel_bias, name):
    def body(rb_ref, o_ref):
        parts = _split3(rb_ref[...])

        def row(i, carry):
            oh = _rel_onehot(i)
            val = _dot(parts[0], oh, NN) + _dot(parts[1], oh, NN) + _dot(parts[2], oh, NN)
            j = lax.broadcasted_iota(jnp.int32, (1, KW), 1)
            rel = lax.shift_right_logical(i, 6) - lax.shift_right_logical(j, 6) + 8
            ok = (rel >= 0) & (rel <= 8)
            o_ref[i] = jnp.where(ok, val, NEG)
            return carry

        lax.fori_loop(0, QB, row, 0)

    return _pallas(body, name=name, out_shape=_sds((QB, N_HEADS, KW), F32),
                   in_specs=[pl.BlockSpec(memory_space=pltpu.VMEM)],
                   out_specs=pl.BlockSpec(memory_space=pltpu.VMEM))(rel_bias)


def bias_grad(dwb, name):
    def body(d_ref, o_ref):
        def row(i, acc):
            oh = _rel_onehot(i)
            hi, mid, lo = _split3(d_ref[i])
            return acc + (_dot(hi, oh, NT) + _dot(mid, oh, NT) + _dot(lo, oh, NT))

        o_ref[...] = lax.fori_loop(0, QB, row, jnp.zeros((N_HEADS, N_REL), F32))

    return _pallas(body, name=name, out_shape=_sds((N_HEADS, N_REL), F32),
                   in_specs=[pl.BlockSpec(memory_space=pltpu.VMEM)],
                   out_specs=pl.BlockSpec(memory_space=pltpu.VMEM))(dwb)


def _attn_probs(qkv_ref, w_ref, b, h):
    r0 = pl.multiple_of(b * QB, QB)
    q2 = qkv_ref[0, pl.ds(r0 + PAD, QB), :]
    k2 = qkv_ref[1, pl.ds(r0, KW), :]
    v2 = qkv_ref[2, pl.ds(r0, KW), :]
    lane = lax.broadcasted_iota(jnp.int32, (1, 2 * HEAD_DIM), 1)
    mh = (lane < HEAD_DIM) if h == 0 else (lane >= HEAD_DIM)
    qh = jnp.where(mh, q2, jnp.zeros_like(q2))
    col = lax.broadcasted_iota(jnp.int32, (1, KW), 1)
    s = _dot(qh, k2, NT) * SCALE + w_ref[h]
    s = jnp.where(col >= PAD - b * QB, s, NEG)
    e = jnp.exp(s - jnp.max(s, axis=-1, keepdims=True))
    p = e / jnp.sum(e, axis=-1, keepdims=True)
    return p, qh, k2, v2, mh


def attn_fwd(qkvp, wb, name):
    T = qkvp.shape[1] - PAD

    def body(qkv_ref, w_ref, o_ref):
        b = pl.program_id(1)
        outs = []
        for h in range(2):
            p, _, _, v2, _ = _attn_probs(qkv_ref, w_ref, b, h)
            outs.append(_dot(p.astype(BF16), v2, NN))
        lane = lax.broadcasted_iota(jnp.int32, (1, 2 * HEAD_DIM), 1)
        o_ref[...] = jnp.where(lane < HEAD_DIM, outs[0], outs[1]).astype(BF16)

    return _pallas(
        body, name=name, grid=(N_HEADS // 2, T // QB),
        in_specs=[pl.BlockSpec((3, PAD + T, 2 * HEAD_DIM), lambda hp, b: (0, 0, hp)),
                  pl.BlockSpec((2, QB, KW), lambda hp, b: (hp, 0, 0))],
        out_specs=pl.BlockSpec((QB, 2 * HEAD_DIM), lambda hp, b: (b, hp)),
        out_shape=_sds((T, D), BF16),
        compiler_params=_params(("parallel", "arbitrary"), VMEM_BIG))(qkvp, wb)


def attn_bwd(qkvp, do, wb, name):
    T = qkvp.shape[1] - PAD
    nb = T // QB

    def body(qkv_ref, do_ref, w_ref, dqkv_ref, dw_ref, dk_acc, dv_acc):
        b = pl.program_id(1)
        r0 = pl.multiple_of(b * QB, QB)

        @pl.when(b == 0)
        def _():
            dk_acc[...] = jnp.zeros_like(dk_acc)
            dv_acc[...] = jnp.zeros_like(dv_acc)
            dw_ref[...] = jnp.zeros_like(dw_ref)
            dqkv_ref[0, 0:PAD, :] = jnp.zeros((PAD, 2 * HEAD_DIM), BF16)

        do2 = do_ref[...]
        dq = None
        dk = None
        dv = None
        for h in range(2):
            p, qh, k2, v2, mh = _attn_probs(qkv_ref, w_ref, b, h)
            doh = jnp.where(mh, do2, jnp.zeros_like(do2))
            dp = _dot(doh, v2, NT)
            ds = p * (dp - jnp.sum(p * dp, axis=-1, keepdims=True))
            dw_ref[h] += ds
            ds16 = (ds * SCALE).astype(BF16)
            dqh = jnp.where(mh, _dot(ds16, k2, NN), 0.0)
            dkh = _dot(ds16, qh, TN)
            dvh = _dot(p.astype(BF16), doh, TN)
            dq = dqh if dq is None else dq + dqh
            dk = dkh if dk is None else dk + dkh
            dv = dvh if dv is None else dv + dvh
        dqkv_ref[0, pl.ds(r0 + PAD, QB), :] = dq.astype(BF16)
        dk_acc[pl.ds(r0, KW), :] += dk
        dv_acc[pl.ds(r0, KW), :] += dv

        @pl.when(b == nb - 1)
        def _():
            dqkv_ref[1] = dk_acc[...].astype(BF16)
            dqkv_ref[2] = dv_acc[...].astype(BF16)

    slab = pl.BlockSpec((3, PAD + T, 2 * HEAD_DIM), lambda hp, b: (0, 0, hp))
    wspec = pl.BlockSpec((2, QB, KW), lambda hp, b: (hp, 0, 0))
    return _pallas(
        body, name=name, grid=(N_HEADS // 2, nb),
        in_specs=[slab, pl.BlockSpec((QB, 2 * HEAD_DIM), lambda hp, b: (b, hp)), wspec],
        out_specs=[slab, wspec],
        out_shape=[_sds((3, PAD + T, D), BF16), _sds((N_HEADS, QB, KW), F32)],
        scratch_shapes=[pltpu.VMEM((PAD + T, 2 * HEAD_DIM), F32), pltpu.VMEM((PAD + T, 2 * HEAD_DIM), F32)],
        compiler_params=_params(("parallel", "arbitrary"), VMEM_BIG))(qkvp, do, wb)


def proj_qkv(hn, w, l, name, tm=512):
    T = hn.shape[0]
    pb = PAD // tm

    def body(a_ref, b_ref, o_ref):
        i = pl.program_id(1)

        @pl.when(i < pb)
        def _():
            o_ref[...] = jnp.zeros_like(o_ref)

        @pl.when(i >= pb)
        def _():
            o_ref[...] = _dot(a_ref[...], b_ref[...], NN).astype(BF16)

    return _pallas(
        body, name=name, grid=(3, pb + T // tm),
        in_specs=[pl.BlockSpec((tm, D), lambda p, i: (jnp.maximum(i - pb, 0), 0)),
                  pl.BlockSpec((None, D, D), lambda p, i: (l, 0, p))],
        out_specs=pl.BlockSpec((None, tm, D), lambda p, i: (p, i, 0)),
        out_shape=_sds((3, PAD + T, D), BF16),
        compiler_params=_params(("parallel", "parallel"), VMEM_BIG))(hn, w)


def ffn_up(hn, wg, wu, l, name, tm=1024):
    T = hn.shape[0]

    def body(a_ref, wg_ref, wu_ref, g_ref, u_ref, h_ref):
        a = a_ref[...]
        g = _dot(a, wg_ref[...], NN)
        u = _dot(a, wu_ref[...], NN)
        g_ref[...] = g.astype(BF16)
        u_ref[...] = u.astype(BF16)
        h_ref[...] = ((g * _sigmoid(g)) * u).astype(BF16)

    wspec = pl.BlockSpec((None, None, D, FS), lambda s, i: (l, s, 0, 0))
    ospec = pl.BlockSpec((None, tm, FS), lambda s, i: (s, i, 0))
    return _pallas(
        body, name=name, grid=(N_CHIPS, T // tm),
        in_specs=[pl.BlockSpec((tm, D), lambda s, i: (i, 0)), wspec, wspec],
        out_specs=[ospec, ospec, ospec],
        out_shape=[_sds((N_CHIPS, T, FS), BF16)] * 3,
        compiler_params=_params(("parallel", "parallel"), VMEM_BIG))(hn, wg, wu)


def ffn_bwd_dh(dxb, wd, g, u, l, name, tm=1024):
    T = dxb.shape[0]

    def body(a_ref, wd_ref, g_ref, u_ref, dg_ref, du_ref):
        dh = _dot(a_ref[...], wd_ref[...], NT)
        gf = g_ref[...].astype(F32)
        uf = u_ref[...].astype(F32)
        s = _sigmoid(gf)
        dg_ref[...] = (dh * uf * (s * (1.0 + gf * (1.0 - s)))).astype(BF16)
        du_ref[...] = (dh * (gf * s)).astype(BF16)

    aspec = pl.BlockSpec((None, tm, FS), lambda s, i: (s, i, 0))
    return _pallas(
        body, name=name, grid=(N_CHIPS, T // tm),
        in_specs=[pl.BlockSpec((tm, D), lambda s, i: (i, 0)),
                  pl.BlockSpec((None, None, FS, D), lambda s, i: (l, s, 0, 0)), aspec, aspec],
        out_specs=[aspec, aspec],
        out_shape=[_sds((N_CHIPS, T, FS), BF16)] * 2,
        compiler_params=_params(("parallel", "parallel"), VMEM_BIG))(dxb, wd, g, u)


def ffn_bwd_dhn(dg, du, wg, wu, l, name, tm=1024):
    T = dg.shape[1]

    def body(dg_ref, du_ref, wg_ref, wu_ref, o_ref):
        s = pl.program_id(1)
        d = _dot(dg_ref[...], wg_ref[...], NT) + _dot(du_ref[...], wu_ref[...], NT)

        @pl.when(s == 0)
        def _():
            o_ref[...] = d

        @pl.when(s > 0)
        def _():
            o_ref[...] += d

    aspec = pl.BlockSpec((None, tm, FS), lambda i, s: (s, i, 0))
    wspec = pl.BlockSpec((None, None, D, FS), lambda i, s: (l, s, 0, 0))
    return _pallas(
        body, name=name, grid=(T // tm, N_CHIPS), in_specs=[aspec, aspec, wspec, wspec],
        out_specs=pl.BlockSpec((tm, D), lambda i, s: (i, 0)), out_shape=_sds((T, D), F32),
        compiler_params=_params(("parallel", "arbitrary"), VMEM_BIG))(dg, du, wg, wu)


def adamw(w, g, m, v, name):
    L, R, C = w.shape

    def body(w_ref, g_ref, m_ref, v_ref, d_ref, nm_ref, nv_ref):
        gf = g_ref[...]
        nm = ADAM_B1 * m_ref[...] + (1.0 - ADAM_B1) * gf
        nv = ADAM_B2 * v_ref[...] + (1.0 - ADAM_B2) * (gf * gf)
        m_hat = nm / (1.0 - ADAM_B1 ** ADAM_STEP)
        v_hat = nv / (1.0 - ADAM_B2 ** ADAM_STEP)
        d_ref[...] = -ADAM_LR * (m_hat / (jnp.sqrt(v_hat) + ADAM_EPS) + ADAM_WD * w_ref[...])
        nm_ref[...] = nm
        nv_ref[...] = nv

    tr = R // 4 if R % 32 == 0 else R
    spec = pl.BlockSpec((None, tr, C), lambda l, r: (l, r, 0))
    return _pallas(body, name=name, grid=(L, R // tr), in_specs=[spec] * 4, out_specs=[spec] * 3,
                   out_shape=[_sds((L, R, C), F32)] * 3,
                   compiler_params=_params(("parallel", "parallel")))(w, g, m, v)


def _coords():
    return lax.axis_index("x"), lax.axis_index("y"), lax.axis_index("c")


def _other_chips(x, y):
    out = []
    for fx, fy in ((1, 0), (0, 1), (1, 1)):
        px = (1 - x) if fx else x
        py = (1 - y) if fy else y
        out.append((px, py))
    return out


def _flip_index(s, j):
    sx, sy = s // 2, s % 2
    fx, fy = ((1, 0), (0, 1), (1, 1))[j]
    return 2 * (sx ^ fx) + (sy ^ fy)


def _for_my_chip(sme, fn):
    for s in range(N_CHIPS):
        pl.when(sme == s)(functools.partial(fn, s))


ANY = pl.BlockSpec(memory_space=pl.ANY)

GATHER_KIND = {"a_w_in": "col", "b_w_qkv": "col", "a_w_out": "row", "b_w_out": "row",
               "ffn_w_gate": "row", "ffn_w_up": "row", "ffn_w_down": "row"}
BIG = tuple(GATHER_KIND)


def _gathered_shape(kind, shape):
    L, R, C = shape
    return (L, R, N_CHIPS * C) if kind == "col" else (L, N_CHIPS, R, C)


def _shard_rows(ref, kind, s, r0, rn, C):
    if kind == "col":
        return ref.at[:, pl.ds(r0, rn), s * C:(s + 1) * C]
    return ref.at[:, s, pl.ds(r0, rn), :]


def gather_weights(shards):
    names = list(shards)
    n = len(names)
    kinds = [GATHER_KIND[k] for k in names]
    shapes = [shards[k].shape for k in names]

    def body(*refs):
        ins, outs = refs[:n], refs[n:2 * n]
        lsem, ssem, rsem, fssem, frsem = refs[2 * n:]
        x, y, c = _coords()
        sme = 2 * x + y
        chips = _other_chips(x, y)

        def run(s):
            def half(t):
                R = shapes[t][1]
                return pl.multiple_of(c * (R // 2), 8), R // 2

            def local(t):
                L, R, C = shapes[t]
                return pltpu.make_async_copy(ins[t], _shard_rows(outs[t], kinds[t], s, 0, R, C), lsem.at[t])

            def send(t, j):
                C = shapes[t][2]
                r0, rn = half(t)
                return pltpu.make_async_remote_copy(
                    src_ref=ins[t].at[:, pl.ds(r0, rn), :], dst_ref=_shard_rows(outs[t], kinds[t], s, r0, rn, C),
                    send_sem=ssem.at[3 * t + j], recv_sem=rsem.at[3 * t + j],
                    device_id=(chips[j][0], chips[j][1], c), device_id_type=MESH)

            def landed(t, j):
                C = shapes[t][2]
                r0, rn = half(t)
                dst = _shard_rows(outs[t], kinds[t], _flip_index(s, j), r0, rn, C)
                return pltpu.make_async_remote_copy(
                    src_ref=dst, dst_ref=dst, send_sem=ssem.at[3 * t + j], recv_sem=rsem.at[3 * t + j],
                    device_id=(chips[j][0], chips[j][1], c), device_id_type=MESH)

            def forward(t, j, mine=True):
                C = shapes[t][2]
                R = shapes[t][1]
                cc = c if mine else 1 - c
                r0 = pl.multiple_of(cc * (R // 2), 8)
                blk = _shard_rows(outs[t], kinds[t], _flip_index(s, j), r0, R // 2, C)
                return pltpu.make_async_remote_copy(
                    src_ref=blk, dst_ref=blk, send_sem=fssem.at[3 * t + j], recv_sem=frsem.at[3 * t + j],
                    device_id=(x, y, 1 - c), device_id_type=MESH)

            for t in range(n):
                local(t).start()
                for j in range(3):
                    send(t, j).start()
            for t in range(n):
                for j in range(3):
                    landed(t, j).wait_recv()
                    forward(t, j).start()
            for t in range(n):
                for j in range(3):
                    forward(t, j, mine=False).wait_recv()
            for t in range(n):
                for j in range(3):
                    send(t, j).wait_send()
                    forward(t, j).wait_send()
                local(t).wait()

        _for_my_chip(sme, run)

    out_shape = [_sds(_gathered_shape(kinds[t], shapes[t]), BF16) for t in range(n)]
    outs = _pallas(body, name="gather_weights", in_specs=[ANY] * n, out_specs=[ANY] * n, out_shape=out_shape,
                   scratch_shapes=[pltpu.SemaphoreType.DMA((n,))] + [pltpu.SemaphoreType.DMA((3 * n,))] * 4,
                   compiler_params=pltpu.CompilerParams(has_side_effects=True))(*[shards[k] for k in names])
    return dict(zip(names, outs))


def _half_shape(kind, R, C):
    return (R // 2, N_CHIPS * C) if kind == "col" else (N_CHIPS, R // 2, C)


def exchange_halves(grads, metas):
    n = len(grads)

    def body(*refs):
        ins, outs = refs[:n], refs[n:2 * n]
        ssem, rsem = refs[2 * n:]
        x, y, c = _coords()
        copies = []
        for t, (kind, R, C) in enumerate(metas):
            r0 = pl.multiple_of((1 - c) * (R // 2), 8)
            src = ins[t].at[pl.ds(r0, R // 2), :] if kind == "col" else ins[t].at[:, pl.ds(r0, R // 2), :]
            copies.append(pltpu.make_async_remote_copy(
                src_ref=src, dst_ref=outs[t], send_sem=ssem.at[t], recv_sem=rsem.at[t],
                device_id=(x, y, 1 - c), device_id_type=MESH))
        for cp in copies:
            cp.start()
        for cp in copies:
            cp.wait()

    out_shape = [_sds(_half_shape(*m), F32) for m in metas]
    return _pallas(body, name="exchange_halves", in_specs=[ANY] * n, out_specs=[ANY] * n, out_shape=out_shape,
                   scratch_shapes=[pltpu.SemaphoreType.DMA((n,))] * 2,
                   compiler_params=pltpu.CompilerParams(has_side_effects=True))(*grads)


def pair_sum(me, g, sib, meta, name):
    kind, R, C = meta
    h = R // 2

    def body(me_ref, g_ref, sib_ref, p16_ref, own_ref):
        s = pl.program_id(0)
        v = g_ref[...] + sib_ref[...]
        p16_ref[...] = v.astype(BF16)

        @pl.when(s == me_ref[1])
        def _():
            own_ref[...] = v

    if kind == "col":
        gspec = pl.BlockSpec((h, C), lambda s, me_ref: (me_ref[0], s))
        sspec = pl.BlockSpec((h, C), lambda s, me_ref: (0, s))
    else:
        gspec = pl.BlockSpec((None, h, C), lambda s, me_ref: (s, me_ref[0], 0))
        sspec = pl.BlockSpec((None, h, C), lambda s, me_ref: (s, 0, 0))
    grid_spec = pltpu.PrefetchScalarGridSpec(
        num_scalar_prefetch=1, grid=(N_CHIPS,), in_specs=[gspec, sspec],
        out_specs=[sspec, pl.BlockSpec((h, C), lambda s, me_ref: (0, 0))])
    return _pallas(body, name=name, grid_spec=grid_spec,
                   out_shape=[_sds(_half_shape(*meta), BF16), _sds((h, C), F32)],
                   compiler_params=_params(("arbitrary",), VMEM_BIG))(me, g, sib)


def scatter_partials(p16s, metas):
    n = len(p16s)

    def body(*refs):
        ins, outs = refs[:n], refs[n:2 * n]
        ssem, rsem = refs[2 * n:]
        x, y, c = _coords()
        sme = 2 * x + y
        chips = _other_chips(x, y)

        def run(s):
            copies = []
            for t, (kind, R, C) in enumerate(metas):
                for j in range(3):
                    sj = _flip_index(s, j)
                    src = ins[t].at[:, sj * C:(sj + 1) * C] if kind == "col" else ins[t].at[sj]
                    copies.append(pltpu.make_async_remote_copy(
                        src_ref=src, dst_ref=outs[t].at[j], send_sem=ssem.at[3 * t + j], recv_sem=rsem.at[3 * t + j],
                        device_id=(chips[j][0], chips[j][1], c), device_id_type=MESH))
            for cp in copies:
                cp.start()
            for cp in copies:
                cp.wait()

        _for_my_chip(sme, run)

    out_shape = [_sds((3, R // 2, C), BF16) for (_, R, C) in metas]
    return _pallas(body, name="scatter_partials", in_specs=[ANY] * n, out_specs=[ANY] * n, out_shape=out_shape,
                   scratch_shapes=[pltpu.SemaphoreType.DMA((3 * n,))] * 2,
                   compiler_params=pltpu.CompilerParams(has_side_effects=True))(*p16s)


def final_sum(me, own, q, buf, l, meta, name):
    _, R, C = meta
    h = R // 2

    def body(me_ref, own_ref, q_ref, buf_ref, o_ref):
        del buf_ref
        o_ref[...] = ((own_ref[...] + q_ref[0].astype(F32)) + q_ref[1].astype(F32)) + q_ref[2].astype(F32)

    grid_spec = pltpu.PrefetchScalarGridSpec(
        num_scalar_prefetch=1, grid=(1,),
        in_specs=[pl.BlockSpec((h, C), lambda i, me_ref: (0, 0)),
                  pl.BlockSpec((3, h, C), lambda i, me_ref: (0, 0, 0)), ANY],
        out_specs=pl.BlockSpec((None, h, C), lambda i, me_ref: (l, me_ref[0], 0)))
    return _pallas(body, name=name, grid_spec=grid_spec, out_shape=_sds(buf.shape, F32),
                   input_output_aliases={3: 0},
                   compiler_params=_params(("arbitrary",), VMEM_BIG))(me, own, q, buf)


def share_final(bufs):
    n = len(bufs)

    def body(*refs):
        ins, outs = refs[:n], refs[n:2 * n]
        ssem, rsem = refs[2 * n:]
        del ins
        x, y, c = _coords()
        copies = []
        for t in range(n):
            R = bufs[t].shape[1]
            r0 = pl.multiple_of(c * (R // 2), 8)
            blk = outs[t].at[:, pl.ds(r0, R // 2), :]
            copies.append(pltpu.make_async_remote_copy(
                src_ref=blk, dst_ref=blk, send_sem=ssem.at[t], recv_sem=rsem.at[t],
                device_id=(x, y, 1 - c), device_id_type=MESH))
        for cp in copies:
            cp.start()
        for t in range(n):
            R = bufs[t].shape[1]
            r1 = pl.multiple_of((1 - c) * (R // 2), 8)
            other = outs[t].at[:, pl.ds(r1, R // 2), :]
            pltpu.make_async_remote_copy(
                src_ref=other, dst_ref=other, send_sem=ssem.at[t], recv_sem=rsem.at[t],
                device_id=(x, y, 1 - c), device_id_type=MESH).wait_recv()
        for cp in copies:
            cp.wait_send()

    out_shape = [_sds(b.shape, F32) for b in bufs]
    return _pallas(body, name="share_final", in_specs=[ANY] * n, out_specs=[ANY] * n, out_shape=out_shape,
                   input_output_aliases={t: t for t in range(n)},
                   scratch_shapes=[pltpu.SemaphoreType.DMA((n,))] * 2,
                   compiler_params=pltpu.CompilerParams(has_side_effects=True))(*bufs)


def allreduce_small(part):
    rows = part.shape[0]

    def body(p_ref, o_ref, buf, ssem, rsem, lsem):
        x, y, c = _coords()
        me = 4 * x + 2 * y + c
        mine = pltpu.make_async_copy(p_ref, buf.at[me], lsem)
        mine.start()
        copies = []
        for r in range(1, 8):
            fx, fy, fc = (r >> 2) & 1, (r >> 1) & 1, r & 1
            peer = ((1 - x) if fx else x, (1 - y) if fy else y, (1 - c) if fc else c)
            copies.append(pltpu.make_async_remote_copy(
                src_ref=p_ref, dst_ref=buf.at[me], send_sem=ssem.at[r - 1], recv_sem=rsem.at[r - 1],
                device_id=peer, device_id_type=MESH))
        for cp in copies:
            cp.start()
        for r in range(1, 8):
            fx, fy, fc = (r >> 2) & 1, (r >> 1) & 1, r & 1
            src = 4 * ((1 - x) if fx else x) + 2 * ((1 - y) if fy else y) + ((1 - c) if fc else c)
            pltpu.make_async_remote_copy(
                src_ref=buf.at[src], dst_ref=buf.at[src], send_sem=ssem.at[r - 1], recv_sem=rsem.at[r - 1],
                device_id=(x, y, c), device_id_type=MESH).wait_recv()
        for cp in copies:
            cp.wait_send()
        mine.wait()
        acc = buf[0]
        for d in range(1, 8):
            acc = acc + buf[d]
        o_ref[...] = acc

    return _pallas(body, name="allreduce_small",
                   in_specs=[pl.BlockSpec(memory_space=pltpu.VMEM)], out_specs=pl.BlockSpec(memory_space=pltpu.VMEM),
                   out_shape=_sds((rows, 128), F32),
                   scratch_shapes=[pltpu.VMEM((8, rows, 128), F32), pltpu.SemaphoreType.DMA((7,)),
                                   pltpu.SemaphoreType.DMA((7,)), pltpu.SemaphoreType.DMA],
                   compiler_params=pltpu.CompilerParams(has_side_effects=True, vmem_limit_bytes=VMEM_BIG))(part)


def _rows128(a):
    flat = a.reshape(-1)
    rows = -(-flat.shape[0] // 128)
    rows8 = -(-rows // 8) * 8
    flat = jnp.pad(flat, (0, rows8 * 128 - flat.shape[0]))
    return flat.reshape(rows8, 128)


def kernel(x, norm_mix_g, norm_ffn_g, final_g, a_w_in, a_v_gain, a_w_s, a_b_s, a_w_out, b_w_qkv, b_rel_bias, b_w_out, ffn_w_gate, ffn_w_up, ffn_w_down, loss_target, m_norm_mix_g, m_norm_ffn_g, m_final_g, m_a_w_in, m_a_v_gain, m_a_w_s, m_a_b_s, m_a_w_out, m_b_w_qkv, m_b_rel_bias, m_b_w_out, m_ffn_w_gate, m_ffn_w_up, m_ffn_w_down, v_norm_mix_g, v_norm_ffn_g, v_final_g, v_a_w_in, v_a_v_gain, v_a_w_s, v_a_b_s, v_a_w_out, v_b_w_qkv, v_b_rel_bias, v_b_w_out, v_ffn_w_gate, v_ffn_w_up, v_ffn_w_down):
    T = x.shape[1]
    weights = dict(norm_mix_g=norm_mix_g, norm_ffn_g=norm_ffn_g, final_g=final_g, a_w_in=a_w_in, a_v_gain=a_v_gain,
                   a_w_s=a_w_s, a_b_s=a_b_s, a_w_out=a_w_out, b_w_qkv=b_w_qkv, b_rel_bias=b_rel_bias,
                   b_w_out=b_w_out, ffn_w_gate=ffn_w_gate, ffn_w_up=ffn_w_up, ffn_w_down=ffn_w_down)
    mom_m = dict(norm_mix_g=m_norm_mix_g, norm_ffn_g=m_norm_ffn_g, final_g=m_final_g, a_w_in=m_a_w_in,
                 a_v_gain=m_a_v_gain, a_w_s=m_a_w_s, a_b_s=m_a_b_s, a_w_out=m_a_w_out, b_w_qkv=m_b_w_qkv,
                 b_rel_bias=m_b_rel_bias, b_w_out=m_b_w_out, ffn_w_gate=m_ffn_w_gate, ffn_w_up=m_ffn_w_up,
                 ffn_w_down=m_ffn_w_down)
    mom_v = dict(norm_mix_g=v_norm_mix_g, norm_ffn_g=v_norm_ffn_g, final_g=v_final_g, a_w_in=v_a_w_in,
                 a_v_gain=v_a_v_gain, a_w_s=v_a_w_s, a_b_s=v_a_b_s, a_w_out=v_a_w_out, b_w_qkv=v_b_w_qkv,
                 b_rel_bias=v_b_rel_bias, b_w_out=v_b_w_out, ffn_w_gate=v_ffn_w_gate, ffn_w_up=v_ffn_w_up,
                 ffn_w_down=v_ffn_w_down)
    order = list(weights)

    xi, yi, ci = _coords()
    me = jnp.stack([ci, 2 * xi + yi]).astype(jnp.int32)

    full = gather_weights({k: cast_bf16(weights[k], "cast_" + k) for k in BIG})
    w_in, w_qkv = full["a_w_in"], full["b_w_qkv"]
    w_aout = full["a_w_out"].reshape(2, GH, D)
    w_bout = full["b_w_out"].reshape(2, D, D)
    w_gate, w_up, w_down = full["ffn_w_gate"], full["ffn_w_up"], full["ffn_w_down"]

    row512 = lambda: pl.BlockSpec((512, D), lambda i, j: (i, 0))

    xcur = x.reshape(T, D)
    saved = []
    for i in range(DEPTH):
        j = i // 2
        tag = "_l%d" % i
        st = {"x_in": xcur}
        hn = rms_fwd(xcur, norm_mix_g[i][None], "rms_mix" + tag)
        st["hn"] = hn
        if i % 2 == 0:
            pre = matmul("a_in" + tag, NN, hn, pl.BlockSpec((1024, D), lambda i_, j_: (i_, 0)),
                         w_in, pl.BlockSpec((None, D, 1024), lambda i_, j_, l=j: (l, 0, j_)),
                         _sds((T, 2 * GH), BF16), pl.BlockSpec((1024, 1024), lambda i_, j_: (i_, j_)),
                         (T // 1024, 4))
            y = sgu_fwd(pre, a_v_gain[j][None], a_w_s[j], a_b_s[j][:, :, None], "sgu_fwd" + tag)
            xmid = matmul("a_out" + tag, NN, y, pl.BlockSpec((512, GH), lambda i_, j_: (i_, 0)),
                          w_aout, pl.BlockSpec((None, GH, D), lambda i_, j_, l=j: (l, 0, 0)),
                          _sds((T, D), F32), row512(), (T // 512, 1), res=xcur, res_spec=row512())
            st.update(pre=pre, y=y)
        else:
            qkvp = proj_qkv(hn, w_qkv, j, "b_qkv" + tag)
            wb = jnp.transpose(bias_build(b_rel_bias[j], "bias_build" + tag), (1, 0, 2))
            o = attn_fwd(qkvp, wb, "attn_fwd" + tag)
            xmid = matmul("b_out" + tag, NN, o, pl.BlockSpec((512, D), lambda i_, j_: (i_, 0)),
                          w_bout, pl.BlockSpec((None, D, D), lambda i_, j_, l=j: (l, 0, 0)),
                          _sds((T, D), F32), row512(), (T // 512, 1), res=xcur, res_spec=row512())
            st.update(qkvp=qkvp, wb=wb, o=o)
        hn2 = rms_fwd(xmid, norm_ffn_g[i][None], "rms_ffn" + tag)
        g, u, h = ffn_up(hn2, w_gate, w_up, i, "ffn_up" + tag)
        xcur = matmul("ffn_down" + tag, NN, h, pl.BlockSpec((None, 512, FS), lambda i_, s_: (s_, i_, 0)),
                      w_down, pl.BlockSpec((None, None, FS, D), lambda i_, s_, l=i: (l, s_, 0, 0)),
                      _sds((T, D), F32), pl.BlockSpec((512, D), lambda i_, s_: (i_, 0)), (T // 512, N_CHIPS),
                      acc=True, res=xmid, res_spec=pl.BlockSpec((512, D), lambda i_, s_: (i_, 0)))
        st.update(x_mid=xmid, hn2=hn2, g=g, u=u, h=h)
        saved.append(st)

    loss_part, dx, dxb, d_final = final_loss(xcur, final_g[None], loss_target.reshape(T, D), "final_loss")

    tk = 512
    big_grads = {k: [None] * weights[k].shape[0] for k in BIG}
    small = {"norm_mix_g": [None] * DEPTH, "norm_ffn_g": [None] * DEPTH, "a_v_gain": [None] * 2,
             "a_w_s": [None] * 2, "a_b_s": [None] * 2, "b_rel_bias": [None] * 2}
    tok = lambda width: pl.BlockSpec((tk, width), lambda j_, k_: (k_, 0))
    for i in reversed(range(DEPTH)):
        j = i // 2
        tag = "_l%d" % i
        st = saved[i]
        dg, du = ffn_bwd_dh(dxb, w_down, st["g"], st["u"], i, "ffn_bwd_dh" + tag)
        part = lambda: pl.BlockSpec((None, tk, FS), lambda j_, k_: (j_, k_, 0))
        big_grads["ffn_w_down"][i] = wgrad(
            "dw_down" + tag, st["h"], part(), dxb, tok(D), _sds((N_CHIPS, FS, D), F32),
            pl.BlockSpec((None, FS, D), lambda j_, k_: (j_, 0, 0)), N_CHIPS, T)
        dhn2 = ffn_bwd_dhn(dg, du, w_gate, w_up, i, "ffn_bwd_dhn" + tag)
        for nm, dz in (("ffn_w_gate", dg), ("ffn_w_up", du)):
            big_grads[nm][i] = wgrad(
                "d" + nm + tag, st["hn2"], tok(D), dz, part(), _sds((N_CHIPS, D, FS), F32),
                pl.BlockSpec((None, D, FS), lambda j_, k_: (j_, 0, 0)), N_CHIPS, T)
        dx, dxb, dgn = rms_bwd(st["x_mid"], dhn2, norm_ffn_g[i][None], dx, "rms_ffn_bwd" + tag)
        small["norm_ffn_g"][i] = dgn
        if i % 2 == 0:
            dy = matmul("a_out_bwd" + tag, NT, dxb, pl.BlockSpec((1024, D), lambda i_, j_: (i_, 0)),
                        w_aout, pl.BlockSpec((None, 1024, D), lambda i_, j_, l=j: (l, j_, 0)),
                        _sds((T, GH), BF16), pl.BlockSpec((1024, 1024), lambda i_, j_: (i_, j_)), (T // 1024, 2))
            big_grads["a_w_out"][j] = wgrad(
                "dw_aout" + tag, st["y"], pl.BlockSpec((tk, 512), lambda j_, k_: (k_, j_)), dxb, tok(D),
                _sds((N_CHIPS, 512, D), F32), pl.BlockSpec((None, 512, D), lambda j_, k_: (j_, 0, 0)), N_CHIPS, T)
            dpre, d_ws, d_bs, d_gain = sgu_bwd(st["pre"], dy, a_v_gain[j][None], a_w_s[j], a_b_s[j][:, :, None],
                                               "sgu_bwd" + tag)
            small["a_w_s"][j], small["a_b_s"][j], small["a_v_gain"][j] = d_ws, d_bs, d_gain
            dhn = matmul("a_in_bwd" + tag, NT, dpre, pl.BlockSpec((512, 2 * GH), lambda i_, j_: (i_, 0)),
                         w_in, pl.BlockSpec((None, D, 2 * GH), lambda i_, j_, l=j: (l, 0, 0)),
                         _sds((T, D), F32), row512(), (T // 512, 1))
            big_grads["a_w_in"][j] = wgrad(
                "dw_in" + tag, st["hn"], tok(D), dpre, pl.BlockSpec((tk, 1024), lambda j_, k_: (k_, j_)),
                _sds((D, 2 * GH), F32), pl.BlockSpec((D, 1024), lambda j_, k_: (0, j_)), 4, T)
        else:
            do = matmul("b_out_bwd" + tag, NT, dxb, pl.BlockSpec((1024, D), lambda i_, j_: (i_, 0)),
                        w_bout, pl.BlockSpec((None, D, D), lambda i_, j_, l=j: (l, 0, 0)),
                        _sds((T, D), BF16), pl.BlockSpec((1024, D), lambda i_, j_: (i_, 0)), (T // 1024, 1))
            big_grads["b_w_out"][j] = wgrad(
                "dw_bout" + tag, st["o"], pl.BlockSpec((tk, 256), lambda j_, k_: (k_, j_)), dxb, tok(D),
                _sds((N_CHIPS, 256, D), F32), pl.BlockSpec((None, 256, D), lambda j_, k_: (j_, 0, 0)), N_CHIPS, T)
            dqkvp, dwb = attn_bwd(st["qkvp"], do, st["wb"], "attn_bwd" + tag)
            small["b_rel_bias"][j] = bias_grad(jnp.transpose(dwb, (1, 0, 2)), "bias_grad" + tag)
            dhn = matmul("b_qkv_bwd" + tag, NT, dqkvp,
                         pl.BlockSpec((None, 512, D), lambda i_, p_: (p_, i_ + PAD // 512, 0)),
                         w_qkv, pl.BlockSpec((None, D, D), lambda i_, p_, l=j: (l, 0, p_)),
                         _sds((T, D), F32), pl.BlockSpec((512, D), lambda i_, p_: (i_, 0)), (T // 512, 3), acc=True)
            big_grads["b_w_qkv"][j] = wgrad(
                "dw_qkv" + tag, st["hn"], tok(D), dqkvp,
                pl.BlockSpec((None, tk, D), lambda j_, k_: (j_, k_ + PAD // tk, 0)),
                _sds((D, 3 * D), F32), pl.BlockSpec((D, D), lambda j_, k_: (0, j_)), 3, T)
        dx, dxb, dgn = rms_bwd(st["x_in"], dhn, norm_mix_g[i][None], dx, "rms_mix_bwd" + tag)
        small["norm_mix_g"][i] = dgn

    small_grads = {
        "norm_mix_g": jnp.concatenate(small["norm_mix_g"], axis=0),
        "norm_ffn_g": jnp.concatenate(small["norm_ffn_g"], axis=0),
        "final_g": d_final.reshape(D),
        "a_v_gain": jnp.concatenate(small["a_v_gain"], axis=0),
        "a_w_s": jnp.stack(small["a_w_s"]),
        "a_b_s": jnp.stack(small["a_b_s"]).reshape(2, SGU_G, SGU_BLOCK),
        "b_rel_bias": jnp.stack(small["b_rel_bias"]),
    }
    small_names = list(small_grads)
    packed = [_rows128(small_grads[k]) for k in small_names] + [_rows128(loss_part[:, :1])]
    offs = [0]
    for p in packed:
        offs.append(offs[-1] + p.shape[0])
    reduced = allreduce_small(jnp.concatenate(packed, axis=0))
    grads = {}
    for t, k in enumerate(small_names):
        nelem = small_grads[k].size
        grads[k] = reduced[offs[t]:offs[t + 1]].reshape(-1)[:nelem].reshape(weights[k].shape)
    loss = reduced[offs[len(small_names)], 0]

    metas, flat, where = [], [], []
    for k in BIG:
        L, R, C = weights[k].shape
        for l in range(L):
            metas.append((GATHER_KIND[k], R, C))
            flat.append(big_grads[k][l])
            where.append((k, l))
    sib = exchange_halves(flat, metas)
    p16, own = [], []
    for t, (k, l) in enumerate(where):
        a, b = pair_sum(me, flat[t], sib[t], metas[t], "pair_sum_%s_l%d" % (k, l))
        p16.append(a)
        own.append(b)
    q = scatter_partials(p16, metas)
    bufs = {k: jnp.zeros(weights[k].shape, F32) for k in BIG}
    for t, (k, l) in enumerate(where):
        bufs[k] = final_sum(me, own[t], q[t], bufs[k], l, metas[t], "final_sum_%s_l%d" % (k, l))
    shared = share_final([bufs[k] for k in BIG])
    for k, gfull in zip(BIG, shared):
        grads[k] = gfull

    delta, new_m, new_v = {}, {}, {}
    for k in order:
        shp = weights[k].shape
        if k in BIG:
            view = shp
        elif k == "a_w_s":
            view = (2, SGU_G * SGU_BLOCK, SGU_BLOCK)
        elif len(shp) == 1:
            view = (1, 1, shp[0])
        elif len(shp) == 2:
            view = (1,) + shp
        else:
            view = shp
        d_, m_, v_ = adamw(weights[k].reshape(view), grads[k].reshape(view), mom_m[k].reshape(view),
                           mom_v[k].reshape(view), "adamw_" + k)
        delta[k], new_m[k], new_v[k] = d_.reshape(shp), m_.reshape(shp), v_.reshape(shp)

    return (loss, dx.reshape(1, T, D), *[grads[k] for k in order], *[delta[k] for k in order],
            *[new_m[k] for k in order], *[new_v[k] for k in order])
```

```python
import functools

import jax
import jax.numpy as jnp
from jax import lax
from jax.experimental import pallas as pl
from jax.experimental.pallas import tpu as pltpu

F32 = jnp.float32
BF16 = jnp.bfloat16
MESH = pl.DeviceIdType.MESH

D = 1024
DEPTH = 4
EPS = 1e-6
SGU_BLOCK = 128
GH = 2048
SGU_G = 8
SGU_GD = GH // SGU_G
N_HEADS = 16
HEAD_DIM = 64
CHUNK = 64
PAD = 8 * CHUNK
FRONT = 2048
QB = 128
KW = PAD + QB
N_REL = 192
REL_MIN = -(CHUNK - 1)
REL_MAX = 128
D_FF = 2816
FS = D_FF // 4
NEG = -1e30
SCALE = HEAD_DIM ** -0.5
N_CHIPS = 4

ADAM_LR = 0.001
ADAM_B1 = 0.9
ADAM_B2 = 0.999
ADAM_EPS = 1e-08
ADAM_WD = 0.01
ADAM_STEP = 10

VMEM_BIG = 56 * 1024 * 1024

NN = ((1,), (0,))
NT = ((1,), (1,))
TN = ((0,), (0,))


def _dot(a, b, dims):
    return lax.dot_general(a, b, (dims, ((), ())), preferred_element_type=F32)


def _pallas(body, **kw):
    return pl.pallas_call(body, **kw)


def _params(sem=None, vmem=None):
    return pltpu.CompilerParams(dimension_semantics=sem, vmem_limit_bytes=vmem)


def _sds(shape, dtype):
    return jax.ShapeDtypeStruct(tuple(shape), dtype)


_GELU_C = 0.7978845608028654


def _gelu(x):
    t = jnp.tanh(_GELU_C * (x + 0.044715 * (x * x * x)))
    return 0.5 * x * (1.0 + t)


def _gelu_and_grad(x):
    x2 = x * x
    t = jnp.tanh(_GELU_C * (x + 0.044715 * (x2 * x)))
    val = 0.5 * x * (1.0 + t)
    grad = 0.5 * (1.0 + t) + 0.5 * x * (1.0 - t * t) * (_GELU_C * (1.0 + 3.0 * 0.044715 * x2))
    return val, grad


def _sigmoid(x):
    return 0.5 * (jnp.tanh(0.5 * x) + 1.0)


def cast_bf16(w, name):
    L, R, C = w.shape

    def body(w_ref, o_ref):
        o_ref[...] = w_ref[...].astype(BF16)

    spec = pl.BlockSpec((None, R, C), lambda l: (l, 0, 0))
    return _pallas(body, name=name, grid=(L,), in_specs=[spec], out_specs=spec,
                   out_shape=_sds((L, R, C), BF16), compiler_params=_params(("parallel",)))(w)


def rms_fwd(x, g, name, tm=512):
    T = x.shape[0]

    def body(x_ref, g_ref, o_ref):
        xf = x_ref[...]
        r = lax.rsqrt(jnp.mean(xf * xf, axis=-1, keepdims=True) + EPS)
        o_ref[...] = ((xf * r) * g_ref[...]).astype(BF16)

    row = pl.BlockSpec((tm, D), lambda i: (i, 0))
    return _pallas(body, name=name, grid=(T // tm,),
                   in_specs=[row, pl.BlockSpec((1, D), lambda i: (0, 0))], out_specs=row,
                   out_shape=_sds((T, D), BF16), compiler_params=_params(("parallel",)))(x, g)


def rms_bwd(x, dh, g, dres, name, tm=256):
    T = x.shape[0]
    n = T // tm

    def body(x_ref, dh_ref, g_ref, dres_ref, dx_ref, dxb_ref, dg_ref, acc_ref):
        i = pl.program_id(0)
        xf = x_ref[...]
        r = lax.rsqrt(jnp.mean(xf * xf, axis=-1, keepdims=True) + EPS)
        xhat = xf * r
        dhf = dh_ref[...]
        part = (dhf * xhat).reshape(tm // 8, 8, D).sum(axis=0)

        @pl.when(i == 0)
        def _():
            acc_ref[...] = part

        @pl.when(i > 0)
        def _():
            acc_ref[...] += part

        dxhat = dhf * g_ref[...]
        dx = dres_ref[...] + r * (dxhat - xhat * jnp.mean(dxhat * xhat, axis=-1, keepdims=True))
        dx_ref[...] = dx
        dxb_ref[...] = dx.astype(BF16)

        @pl.when(i == n - 1)
        def _():
            dg_ref[...] = jnp.sum(acc_ref[...], axis=0, keepdims=True)

    row = pl.BlockSpec((tm, D), lambda i: (i, 0))
    vec = pl.BlockSpec((1, D), lambda i: (0, 0))
    return _pallas(body, name=name, grid=(n,), in_specs=[row, row, vec, row], out_specs=[row, row, vec],
                   out_shape=[_sds((T, D), F32), _sds((T, D), BF16), _sds((1, D), F32)],
                   scratch_shapes=[pltpu.VMEM((8, D), F32)],
                   compiler_params=_params(("arbitrary",)))(x, dh, g, dres)


def final_loss(x, g, tgt, name, tm=256):
    T = x.shape[0]
    n = T // tm

    def body(x_ref, g_ref, t_ref, loss_ref, dx_ref, dxb_ref, dg_ref, acc_ref, lacc_ref):
        i = pl.program_id(0)
        xf = x_ref[...]
        r = lax.rsqrt(jnp.mean(xf * xf, axis=-1, keepdims=True) + EPS)
        xhat = xf * r
        gg = g_ref[...]
        e = xhat * gg - t_ref[...]
        dy = e * (1.0 / D)
        part = (dy * xhat).reshape(tm // 8, 8, D).sum(axis=0)
        lpart = (e * e).reshape(tm // 8, 8, D).sum(axis=0)

        @pl.when(i == 0)
        def _():
            acc_ref[...] = part
            lacc_ref[...] = lpart

        @pl.when(i > 0)
        def _():
            acc_ref[...] += part
            lacc_ref[...] += lpart

        dxhat = dy * gg
        dx = r * (dxhat - xhat * jnp.mean(dxhat * xhat, axis=-1, keepdims=True))
        dx_ref[...] = dx
        dxb_ref[...] = dx.astype(BF16)

        @pl.when(i == n - 1)
        def _():
            dg_ref[...] = jnp.sum(acc_ref[...], axis=0, keepdims=True)
            total = jnp.sum(jnp.sum(lacc_ref[...], axis=0, keepdims=True), axis=1, keepdims=True)
            loss_ref[...] = jnp.broadcast_to(total * (0.5 / D), (1, 128))

    row = pl.BlockSpec((tm, D), lambda i: (i, 0))
    vec = pl.BlockSpec((1, D), lambda i: (0, 0))
    return _pallas(body, name=name, grid=(n,), in_specs=[row, vec, row],
                   out_specs=[pl.BlockSpec((1, 128), lambda i: (0, 0)), row, row, vec],
                   out_shape=[_sds((1, 128), F32), _sds((T, D), F32), _sds((T, D), BF16), _sds((1, D), F32)],
                   scratch_shapes=[pltpu.VMEM((8, D), F32), pltpu.VMEM((8, D), F32)],
                   compiler_params=_params(("arbitrary",)))(x, g, tgt)


def matmul(name, dims, a, a_spec, b, b_spec, out_shape, out_spec, grid, *, acc=False, res=None, res_spec=None):
    has_res = res is not None

    def body(*refs):
        a_ref, b_ref = refs[0], refs[1]
        r_ref = refs[2] if has_res else None
        o_ref = refs[-1]
        d = _dot(a_ref[...], b_ref[...], dims)
        if not acc:
            if has_res:
                d = d + r_ref[...]
            o_ref[...] = d.astype(o_ref.dtype)
        else:
            k = pl.program_id(len(grid) - 1)

            @pl.when(k == 0)
            def _():
                o_ref[...] = (d + r_ref[...]) if has_res else d

            @pl.when(k > 0)
            def _():
                o_ref[...] += d

    sem = ("parallel",) * (len(grid) - 1) + (("arbitrary",) if acc else ("parallel",))
    ins = [a, b] + ([res] if has_res else [])
    specs = [a_spec, b_spec] + ([res_spec] if has_res else [])
    return _pallas(body, name=name, grid=grid, in_specs=specs, out_specs=out_spec, out_shape=out_shape,
                   compiler_params=_params(sem, VMEM_BIG))(*ins)


def wgrad(name, a, a_spec, b, b_spec, out_shape, out_spec, J, T, tk):
    return matmul(name, TN, a, a_spec, b, b_spec, out_shape, out_spec, (J, T // tk), acc=True)


def _sgu_mask():
    p = lax.broadcasted_iota(jnp.int32, (SGU_BLOCK, SGU_BLOCK), 0)
    q = lax.broadcasted_iota(jnp.int32, (SGU_BLOCK, SGU_BLOCK), 1)
    return lax.shift_right_logical(q, 6) <= lax.shift_right_logical(p, 6)


def sgu_fwd(pre, gain, w_s, b_s, name):
    T = pre.shape[0]

    def body(pre_ref, gain_ref, ws_ref, bs_ref, y_ref):
        mask = _sgu_mask()
        u = _gelu(pre_ref[:, :GH].astype(F32))
        va = _gelu(pre_ref[:, GH:].astype(F32))
        r = lax.rsqrt(jnp.mean(va * va, axis=-1, keepdims=True) + EPS)
        vn = ((va * r) * gain_ref[...]).astype(BF16)
        for g in range(SGU_G):
            sl = slice(g * SGU_GD, (g + 1) * SGU_GD)
            wm = jnp.where(mask, ws_ref[g], 0.0).astype(BF16)
            vm = _dot(wm, vn[:, sl], NN) + bs_ref[g]
            y_ref[:, sl] = (u[:, sl] * vm).astype(BF16)

    return _pallas(
        body, name=name, grid=(T // SGU_BLOCK,),
        in_specs=[pl.BlockSpec((SGU_BLOCK, 2 * GH), lambda i: (i, 0)),
                  pl.BlockSpec((1, GH), lambda i: (0, 0)),
                  pl.BlockSpec((SGU_G, SGU_BLOCK, SGU_BLOCK), lambda i: (0, 0, 0)),
                  pl.BlockSpec((SGU_G, SGU_BLOCK, 1), lambda i: (0, 0, 0))],
        out_specs=pl.BlockSpec((SGU_BLOCK, GH), lambda i: (i, 0)),
        out_shape=_sds((T, GH), BF16), compiler_params=_params(("parallel",)))(pre, gain, w_s, b_s)


def sgu_bwd(pre, dy, gain, w_s, b_s, name):
    T = pre.shape[0]
    n = T // SGU_BLOCK

    def body(pre_ref, dy_ref, gain_ref, ws_ref, bs_ref, dpre_ref, dws_ref, dbs_ref, dgain_ref, gacc_ref):
        i = pl.program_id(0)

        @pl.when(i == 0)
        def _():
            dws_ref[...] = jnp.zeros_like(dws_ref)
            dbs_ref[...] = jnp.zeros_like(dbs_ref)
            gacc_ref[...] = jnp.zeros_like(gacc_ref)

        mask = _sgu_mask()
        u, du_dpre = _gelu_and_grad(pre_ref[:, :GH].astype(F32))
        va, dva_dpre = _gelu_and_grad(pre_ref[:, GH:].astype(F32))
        r = lax.rsqrt(jnp.mean(va * va, axis=-1, keepdims=True) + EPS)
        vhat = va * r
        gain_v = gain_ref[...]
        vn = (vhat * gain_v).astype(BF16)
        dyf = dy_ref[...].astype(F32)
        dvn_parts = []
        for g in range(SGU_G):
            sl = slice(g * SGU_GD, (g + 1) * SGU_GD)
            wm = jnp.where(mask, ws_ref[g], 0.0).astype(BF16)
            vm = _dot(wm, vn[:, sl], NN) + bs_ref[g]
            dpre_ref[:, sl] = ((dyf[:, sl] * vm) * du_dpre[:, sl]).astype(BF16)
            dvm = dyf[:, sl] * u[:, sl]
            dbs_ref[g] += jnp.sum(dvm, axis=-1, keepdims=True)
            dvm16 = dvm.astype(BF16)
            dws_ref[g] += jnp.where(mask, _dot(dvm16, vn[:, sl], NT), 0.0)
            dvn_parts.append(_dot(wm, dvm16, TN))
        dvn = jnp.concatenate(dvn_parts, axis=-1)
        gacc_ref[...] += (dvn * vhat).reshape(SGU_BLOCK // 8, 8, GH).sum(axis=0)
        dvhat = dvn * gain_v
        dva = r * (dvhat - vhat * jnp.mean(dvhat * vhat, axis=-1, keepdims=True))
        dpre_ref[:, GH:] = (dva * dva_dpre).astype(BF16)

        @pl.when(i == n - 1)
        def _():
            dgain_ref[...] = jnp.sum(gacc_ref[...], axis=0, keepdims=True)

    const3 = lambda i: (0, 0, 0)
    return _pallas(
        body, name=name, grid=(n,),
        in_specs=[pl.BlockSpec((SGU_BLOCK, 2 * GH), lambda i: (i, 0)),
                  pl.BlockSpec((SGU_BLOCK, GH), lambda i: (i, 0)),
                  pl.BlockSpec((1, GH), lambda i: (0, 0)),
                  pl.BlockSpec((SGU_G, SGU_BLOCK, SGU_BLOCK), const3),
                  pl.BlockSpec((SGU_G, SGU_BLOCK, 1), const3)],
        out_specs=[pl.BlockSpec((SGU_BLOCK, 2 * GH), lambda i: (i, 0)),
                   pl.BlockSpec((SGU_G, SGU_BLOCK, SGU_BLOCK), const3),
                   pl.BlockSpec((SGU_G, SGU_BLOCK, 1), const3),
                   pl.BlockSpec((1, GH), lambda i: (0, 0))],
        out_shape=[_sds((T, 2 * GH), BF16), _sds((SGU_G, SGU_BLOCK, SGU_BLOCK), F32),
                   _sds((SGU_G, SGU_BLOCK, 1), F32), _sds((1, GH), F32)],
        scratch_shapes=[pltpu.VMEM((8, GH), F32)],
        compiler_params=_params(("arbitrary",)))(pre, dy, gain, w_s, b_s)


def _rel_onehot(i):
    j = lax.broadcasted_iota(jnp.int32, (N_REL, KW), 1)
    r = lax.broadcasted_iota(jnp.int32, (N_REL, KW), 0)
    idx = jnp.clip(i - j + PAD, REL_MIN, REL_MAX) - REL_MIN
    return (idx == r).astype(BF16)


def _split3(v):
    hi = v.astype(BF16)
    r1 = v - hi.astype(F32)
    mid = r1.astype(BF16)
    lo = (r1 - mid.astype(F32)).astype(BF16)
    return hi, mid, lo


def bias_build(rel_bias, name):
    def body(rb_ref, o_ref):
        parts = _split3(rb_ref[...])

        def row(i, carry):
            oh = _rel_onehot(i)
            val = _dot(parts[0], oh, NN) + _dot(parts[1], oh, NN) + _dot(parts[2], oh, NN)
            j = lax.broadcasted_iota(jnp.int32, (1, KW), 1)
            rel = lax.shift_right_logical(i, 6) - lax.shift_right_logical(j, 6) + 8
            ok = (rel >= 0) & (rel <= 8)
            o_ref[i] = jnp.where(ok, val, NEG)
            return carry

        lax.fori_loop(0, QB, row, 0)

    return _pallas(body, name=name, out_shape=_sds((QB, N_HEADS, KW), F32),
                   in_specs=[pl.BlockSpec(memory_space=pltpu.VMEM)],
                   out_specs=pl.BlockSpec(memory_space=pltpu.VMEM))(rel_bias)


def bias_grad(dwb, name):
    def body(d_ref, o_ref):
        def row(i, acc):
            oh = _rel_onehot(i)
            hi, mid, lo = _split3(d_ref[i])
            return acc + (_dot(hi, oh, NT) + _dot(mid, oh, NT) + _dot(lo, oh, NT))

        o_ref[...] = lax.fori_loop(0, QB, row, jnp.zeros((N_HEADS, N_REL), F32))

    return _pallas(body, name=name, out_shape=_sds((N_HEADS, N_REL), F32),
                   in_specs=[pl.BlockSpec(memory_space=pltpu.VMEM)],
                   out_specs=pl.BlockSpec(memory_space=pltpu.VMEM))(dwb)


def _attn_block(qkv_ref, blk):
    r0 = pl.multiple_of(blk * QB, QB)
    qs = qkv_ref[0, pl.ds(r0 + FRONT, QB), :] * SCALE
    k2 = qkv_ref[1, pl.ds(r0 + (FRONT - PAD), KW), :]
    v2 = qkv_ref[2, pl.ds(r0 + (FRONT - PAD), KW), :]
    col = lax.broadcasted_iota(jnp.int32, (1, KW), 1)
    return r0, qs, k2, v2, col >= PAD - blk * QB


def _head_mask(h):
    lane = lax.broadcasted_iota(jnp.int32, (1, 2 * HEAD_DIM), 1)
    return (lane < HEAD_DIM) if h == 0 else (lane >= HEAD_DIM)


def _attn_exp(qh, k2, bias, kvalid):
    s = jnp.where(kvalid, _dot(qh, k2, NT) + bias, NEG)
    e = jnp.exp(s - jnp.max(s, axis=-1, keepdims=True))
    return e, 1.0 / jnp.sum(e, axis=-1, keepdims=True)


ATTN_G = 2


def attn_fwd(qkvp, wb, name):
    T = qkvp.shape[1] - FRONT

    def body(qkv_ref, w_ref, o_ref):
        b = pl.program_id(1)
        for t in range(ATTN_G):
            _, qs, k2, v2, kvalid = _attn_block(qkv_ref, b * ATTN_G + t)
            outs = []
            for h in range(2):
                qh = jnp.where(_head_mask(h), qs, jnp.zeros_like(qs))
                e, inv = _attn_exp(qh, k2, w_ref[h], kvalid)
                outs.append(_dot(e.astype(BF16), v2, NN) * inv)
            o_ref[t * QB:(t + 1) * QB, :] = jnp.where(_head_mask(0), outs[0], outs[1]).astype(BF16)

    return _pallas(
        body, name=name, grid=(N_HEADS // 2, T // (QB * ATTN_G)),
        in_specs=[pl.BlockSpec((3, FRONT + T, 2 * HEAD_DIM), lambda hp, b: (0, 0, hp)),
                  pl.BlockSpec((2, QB, KW), lambda hp, b: (hp, 0, 0))],
        out_specs=pl.BlockSpec((QB * ATTN_G, 2 * HEAD_DIM), lambda hp, b: (b, hp)),
        out_shape=_sds((T, D), BF16),
        compiler_params=_params(("parallel", "arbitrary"), VMEM_BIG))(qkvp, wb)


def attn_bwd(qkvp, o, do, wb, name):
    T = qkvp.shape[1] - FRONT
    nb = T // (QB * ATTN_G)

    def body(qkv_ref, o_ref, do_ref, w_ref, dqkv_ref, dw_ref, dk_acc, dv_acc):
        b = pl.program_id(1)

        @pl.when(b == 0)
        def _():
            dk_acc[...] = jnp.zeros_like(dk_acc)
            dv_acc[...] = jnp.zeros_like(dv_acc)
            dw_ref[...] = jnp.zeros_like(dw_ref)
            dqkv_ref[0, 0:FRONT, :] = jnp.zeros((FRONT, 2 * HEAD_DIM), BF16)

        dws = [None, None]
        for t in range(ATTN_G):
            r0, qs, k2, v2, kvalid = _attn_block(qkv_ref, b * ATTN_G + t)
            do2 = do_ref[t * QB:(t + 1) * QB, :]
            prod = do2.astype(F32) * o_ref[t * QB:(t + 1) * QB, :].astype(F32)
            dq = []
            dk = None
            dv = None
            for h in range(2):
                mh = _head_mask(h)
                qh = jnp.where(mh, qs, jnp.zeros_like(qs))
                e, inv = _attn_exp(qh, k2, w_ref[h], kvalid)
                delta = jnp.sum(jnp.where(mh, prod, 0.0), axis=-1, keepdims=True)
                doh = jnp.where(mh, do2, jnp.zeros_like(do2))
                ds = e * ((_dot(doh, v2, NT) - delta) * inv)
                dws[h] = ds if dws[h] is None else dws[h] + ds
                ds16 = ds.astype(BF16)
                dq.append(_dot(ds16, k2, NN))
                dkh = _dot(ds16, qh, TN)
                dvh = _dot(e.astype(BF16), (doh.astype(F32) * inv).astype(BF16), TN)
                dk = dkh if dk is None else dk + dkh
                dv = dvh if dv is None else dv + dvh
            dqkv_ref[0, pl.ds(r0 + FRONT, QB), :] = (jnp.where(_head_mask(0), dq[0], dq[1]) * SCALE).astype(BF16)
            dk_acc[pl.ds(r0 + (FRONT - PAD), KW), :] += dk
            dv_acc[pl.ds(r0 + (FRONT - PAD), KW), :] += dv
        for h in range(2):
            dw_ref[h] += dws[h]

        @pl.when(b == nb - 1)
        def _():
            dqkv_ref[1] = dk_acc[...].astype(BF16)
            dqkv_ref[2] = dv_acc[...].astype(BF16)

    slab = pl.BlockSpec((3, FRONT + T, 2 * HEAD_DIM), lambda hp, b: (0, 0, hp))
    wspec = pl.BlockSpec((2, QB, KW), lambda hp, b: (hp, 0, 0))
    rows = pl.BlockSpec((QB * ATTN_G, 2 * HEAD_DIM), lambda hp, b: (b, hp))
    return _pallas(
        body, name=name, grid=(N_HEADS // 2, nb),
        in_specs=[slab, rows, rows, wspec],
        out_specs=[slab, wspec],
        out_shape=[_sds((3, FRONT + T, D), BF16), _sds((N_HEADS, QB, KW), F32)],
        scratch_shapes=[pltpu.VMEM((FRONT + T, 2 * HEAD_DIM), F32), pltpu.VMEM((FRONT + T, 2 * HEAD_DIM), F32)],
        compiler_params=_params(("parallel", "arbitrary"), VMEM_BIG))(qkvp, o, do, wb)


def proj_qkv(hn, w, l, name, tm=512):
    T = hn.shape[0]
    pb = FRONT // tm

    def body(a_ref, b_ref, o_ref):
        i = pl.program_id(1)

        @pl.when(i < pb)
        def _():
            o_ref[...] = jnp.zeros_like(o_ref)

        @pl.when(i >= pb)
        def _():
            o_ref[...] = _dot(a_ref[...], b_ref[...], NN).astype(BF16)

    return _pallas(
        body, name=name, grid=(3, pb + T // tm),
        in_specs=[pl.BlockSpec((tm, D), lambda p, i: (jnp.maximum(i - pb, 0), 0)),
                  pl.BlockSpec((None, D, D), lambda p, i: (l, 0, p))],
        out_specs=pl.BlockSpec((None, tm, D), lambda p, i: (p, i, 0)),
        out_shape=_sds((3, FRONT + T, D), BF16),
        compiler_params=_params(("parallel", "parallel"), VMEM_BIG))(hn, w)


def ffn_up(hn, wg, wu, l, name, tm=1024):
    T = hn.shape[0]

    def body(a_ref, wg_ref, wu_ref, g_ref, u_ref, h_ref):
        a = a_ref[...]
        g = _dot(a, wg_ref[...], NT)
        u = _dot(a, wu_ref[...], NT)
        g_ref[...] = g.astype(BF16)
        u_ref[...] = u.astype(BF16)
        h_ref[...] = ((g * _sigmoid(g)) * u).astype(BF16)

    wspec = pl.BlockSpec((None, None, FS, D), lambda s, i: (l, s, 0, 0))
    ospec = pl.BlockSpec((None, tm, FS), lambda s, i: (s, i, 0))
    return _pallas(
        body, name=name, grid=(N_CHIPS, T // tm),
        in_specs=[pl.BlockSpec((tm, D), lambda s, i: (i, 0)), wspec, wspec],
        out_specs=[ospec, ospec, ospec],
        out_shape=[_sds((N_CHIPS, T, FS), BF16)] * 3,
        compiler_params=_params(("parallel", "parallel"), VMEM_BIG))(hn, wg, wu)


def ffn_bwd_dh(dxb, wd, g, u, l, name, tm=1024):
    T = dxb.shape[0]

    def body(a_ref, wd_ref, g_ref, u_ref, dg_ref, du_ref):
        dh = _dot(a_ref[...], wd_ref[...], NT)
        gf = g_ref[...].astype(F32)
        uf = u_ref[...].astype(F32)
        s = _sigmoid(gf)
        dg_ref[...] = (dh * uf * (s * (1.0 + gf * (1.0 - s)))).astype(BF16)
        du_ref[...] = (dh * (gf * s)).astype(BF16)

    aspec = pl.BlockSpec((None, tm, FS), lambda s, i: (s, i, 0))
    return _pallas(
        body, name=name, grid=(N_CHIPS, T // tm),
        in_specs=[pl.BlockSpec((tm, D), lambda s, i: (i, 0)),
                  pl.BlockSpec((None, None, FS, D), lambda s, i: (l, s, 0, 0)), aspec, aspec],
        out_specs=[aspec, aspec],
        out_shape=[_sds((N_CHIPS, T, FS), BF16)] * 2,
        compiler_params=_params(("parallel", "parallel"), VMEM_BIG))(dxb, wd, g, u)


def ffn_bwd_dhn(dg, du, wg, wu, l, name, tm=512):
    T = dg.shape[1]

    def body(dg_ref, du_ref, wg_ref, wu_ref, o_ref):
        d = None
        for s in range(N_CHIPS):
            t = _dot(dg_ref[s], wg_ref[s], NN) + _dot(du_ref[s], wu_ref[s], NN)
            d = t if d is None else d + t
        o_ref[...] = d

    aspec = pl.BlockSpec((N_CHIPS, tm, FS), lambda i: (0, i, 0))
    wspec = pl.BlockSpec((None, N_CHIPS, FS, D), lambda i: (l, 0, 0, 0))
    return _pallas(
        body, name=name, grid=(T // tm,), in_specs=[aspec, aspec, wspec, wspec],
        out_specs=pl.BlockSpec((tm, D), lambda i: (i, 0)), out_shape=_sds((T, D), F32),
        compiler_params=_params(("parallel",), VMEM_BIG))(dg, du, wg, wu)


def ffn_down_res(h, wd, res, l, name, tm=512):
    T = h.shape[1]

    def body(h_ref, wd_ref, r_ref, o_ref):
        d = r_ref[...]
        for s in range(N_CHIPS):
            d = d + _dot(h_ref[s], wd_ref[s], NN)
        o_ref[...] = d

    row = pl.BlockSpec((tm, D), lambda i: (i, 0))
    return _pallas(
        body, name=name, grid=(T // tm,),
        in_specs=[pl.BlockSpec((N_CHIPS, tm, FS), lambda i: (0, i, 0)),
                  pl.BlockSpec((None, N_CHIPS, FS, D), lambda i: (l, 0, 0, 0)), row],
        out_specs=row, out_shape=_sds((T, D), F32),
        compiler_params=_params(("parallel",), VMEM_BIG))(h, wd, res)


def qkv_bwd(dqkvp, w, l, name, tm=512):
    T = dqkvp.shape[1] - FRONT

    def body(a_ref, w_ref, o_ref):
        d = None
        for p in range(3):
            t = _dot(a_ref[p], w_ref[:, p * D:(p + 1) * D], NT)
            d = t if d is None else d + t
        o_ref[...] = d

    return _pallas(
        body, name=name, grid=(T // tm,),
        in_specs=[pl.BlockSpec((3, tm, D), lambda i: (0, i + FRONT // tm, 0)),
                  pl.BlockSpec((None, D, 3 * D), lambda i: (l, 0, 0))],
        out_specs=pl.BlockSpec((tm, D), lambda i: (i, 0)), out_shape=_sds((T, D), F32),
        compiler_params=_params(("parallel",), VMEM_BIG))(dqkvp, w)


def adamw(w, g, m, v, name):
    L, R, C = w.shape

    def body(w_ref, g_ref, m_ref, v_ref, d_ref, nm_ref, nv_ref):
        gf = g_ref[...]
        nm = ADAM_B1 * m_ref[...] + (1.0 - ADAM_B1) * gf
        nv = ADAM_B2 * v_ref[...] + (1.0 - ADAM_B2) * (gf * gf)
        m_hat = nm / (1.0 - ADAM_B1 ** ADAM_STEP)
        v_hat = nv / (1.0 - ADAM_B2 ** ADAM_STEP)
        d_ref[...] = -ADAM_LR * (m_hat / (jnp.sqrt(v_hat) + ADAM_EPS) + ADAM_WD * w_ref[...])
        nm_ref[...] = nm
        nv_ref[...] = nv

    tr = R // 4 if R % 32 == 0 else R
    spec = pl.BlockSpec((None, tr, C), lambda l, r: (l, r, 0))
    return _pallas(body, name=name, grid=(L, R // tr), in_specs=[spec] * 4, out_specs=[spec] * 3,
                   out_shape=[_sds((L, R, C), F32)] * 3,
                   compiler_params=_params(("parallel", "parallel")))(w, g, m, v)


def _coords():
    return lax.axis_index("x"), lax.axis_index("y"), lax.axis_index("c")


def _other_chips(x, y):
    out = []
    for fx, fy in ((1, 0), (0, 1), (1, 1)):
        px = (1 - x) if fx else x
        py = (1 - y) if fy else y
        out.append((px, py))
    return out


def _flip_index(s, j):
    sx, sy = s // 2, s % 2
    fx, fy = ((1, 0), (0, 1), (1, 1))[j]
    return 2 * (sx ^ fx) + (sy ^ fy)


def _for_my_chip(sme, fn):
    for s in range(N_CHIPS):
        pl.when(sme == s)(functools.partial(fn, s))


ANY = pl.BlockSpec(memory_space=pl.ANY)

GATHER_KIND = {"a_w_in": "col", "b_w_qkv": "col", "a_w_out": "row", "b_w_out": "row",
               "ffn_w_gate": "row", "ffn_w_up": "row", "ffn_w_down": "row"}
BIG = tuple(GATHER_KIND)


def _gathered_shape(kind, shape):
    L, R, C = shape
    return (L, R, N_CHIPS * C) if kind == "col" else (L, N_CHIPS, R, C)


def _shard_rows(ref, kind, s, r0, rn, C):
    if kind == "col":
        return ref.at[:, pl.ds(r0, rn), s * C:(s + 1) * C]
    return ref.at[:, s, pl.ds(r0, rn), :]


def gather_weights(shards):
    names = list(shards)
    n = len(names)
    kinds = [GATHER_KIND[k] for k in names]
    shapes = [shards[k].shape for k in names]

    def body(*refs):
        ins, outs = refs[:n], refs[n:2 * n]
        lsem, ssem, rsem, fssem, frsem = refs[2 * n:]
        x, y, c = _coords()
        sme = 2 * x + y
        chips = _other_chips(x, y)

        def run(s):
            def half(t):
                R = shapes[t][1]
                return pl.multiple_of(c * (R // 2), 8), R // 2

            def local(t):
                L, R, C = shapes[t]
                return pltpu.make_async_copy(ins[t], _shard_rows(outs[t], kinds[t], s, 0, R, C), lsem.at[t])

            def send(t, j):
                C = shapes[t][2]
                r0, rn = half(t)
                return pltpu.make_async_remote_copy(
                    src_ref=ins[t].at[:, pl.ds(r0, rn), :], dst_ref=_shard_rows(outs[t], kinds[t], s, r0, rn, C),
                    send_sem=ssem.at[3 * t + j], recv_sem=rsem.at[3 * t + j],
                    device_id=(chips[j][0], chips[j][1], c), device_id_type=MESH)

            def landed(t, j):
                C = shapes[t][2]
                r0, rn = half(t)
                dst = _shard_rows(outs[t], kinds[t], _flip_index(s, j), r0, rn, C)
                return pltpu.make_async_remote_copy(
                    src_ref=dst, dst_ref=dst, send_sem=ssem.at[3 * t + j], recv_sem=rsem.at[3 * t + j],
                    device_id=(chips[j][0], chips[j][1], c), device_id_type=MESH)

            def forward(t, j, mine=True):
                C = shapes[t][2]
                R = shapes[t][1]
                cc = c if mine else 1 - c
                r0 = pl.multiple_of(cc * (R // 2), 8)
                blk = _shard_rows(outs[t], kinds[t], _flip_index(s, j), r0, R // 2, C)
                return pltpu.make_async_remote_copy(
                    src_ref=blk, dst_ref=blk, send_sem=fssem.at[3 * t + j], recv_sem=frsem.at[3 * t + j],
                    device_id=(x, y, 1 - c), device_id_type=MESH)

            for t in range(n):
                local(t).start()
                for j in range(3):
                    send(t, j).start()
            for t in range(n):
                for j in range(3):
                    landed(t, j).wait_recv()
                    forward(t, j).start()
            for t in range(n):
                for j in range(3):
                    forward(t, j, mine=False).wait_recv()
            for t in range(n):
                for j in range(3):
                    send(t, j).wait_send()
                    forward(t, j).wait_send()
                local(t).wait()

        _for_my_chip(sme, run)

    out_shape = [_sds(_gathered_shape(kinds[t], shapes[t]), BF16) for t in range(n)]
    outs = _pallas(body, name="gather_weights", in_specs=[ANY] * n, out_specs=[ANY] * n, out_shape=out_shape,
                   scratch_shapes=[pltpu.SemaphoreType.DMA((n,))] + [pltpu.SemaphoreType.DMA((3 * n,))] * 4,
                   compiler_params=pltpu.CompilerParams(has_side_effects=True))(*[shards[k] for k in names])
    return dict(zip(names, outs))


def _half_shape(kind, R, C):
    return (R // 2, N_CHIPS * C) if kind == "col" else (N_CHIPS, R // 2, C)


def exchange_halves(grads, metas):
    n = len(grads)

    def body(*refs):
        ins, outs = refs[:n], refs[n:2 * n]
        ssem, rsem = refs[2 * n:]
        x, y, c = _coords()
        copies = []
        for t, (kind, R, C) in enumerate(metas):
            r0 = pl.multiple_of((1 - c) * (R // 2), 8)
            src = ins[t].at[pl.ds(r0, R // 2), :] if kind == "col" else ins[t].at[:, pl.ds(r0, R // 2), :]
            copies.append(pltpu.make_async_remote_copy(
                src_ref=src, dst_ref=outs[t], send_sem=ssem.at[t], recv_sem=rsem.at[t],
                device_id=(x, y, 1 - c), device_id_type=MESH))
        for cp in copies:
            cp.start()
        for cp in copies:
            cp.wait()

    out_shape = [_sds(_half_shape(*m), F32) for m in metas]
    return _pallas(body, name="exchange_halves", in_specs=[ANY] * n, out_specs=[ANY] * n, out_shape=out_shape,
                   scratch_shapes=[pltpu.SemaphoreType.DMA((n,))] * 2,
                   compiler_params=pltpu.CompilerParams(has_side_effects=True))(*grads)


def pair_sum(me, g, sib, meta, name):
    kind, R, C = meta
    h = R // 2

    def body(me_ref, g_ref, sib_ref, p16_ref, own_ref):
        s = pl.program_id(0)
        v = g_ref[...] + sib_ref[...]
        p16_ref[...] = v.astype(BF16)

        @pl.when(s == me_ref[1])
        def _():
            own_ref[...] = v

    if kind == "col":
        gspec = pl.BlockSpec((h, C), lambda s, me_ref: (me_ref[0], s))
        sspec = pl.BlockSpec((h, C), lambda s, me_ref: (0, s))
    else:
        gspec = pl.BlockSpec((None, h, C), lambda s, me_ref: (s, me_ref[0], 0))
        sspec = pl.BlockSpec((None, h, C), lambda s, me_ref: (s, 0, 0))
    grid_spec = pltpu.PrefetchScalarGridSpec(
        num_scalar_prefetch=1, grid=(N_CHIPS,), in_specs=[gspec, sspec],
        out_specs=[sspec, pl.BlockSpec((h, C), lambda s, me_ref: (0, 0))])
    return _pallas(body, name=name, grid_spec=grid_spec,
                   out_shape=[_sds(_half_shape(*meta), BF16), _sds((h, C), F32)],
                   compiler_params=_params(("arbitrary",), VMEM_BIG))(me, g, sib)


def scatter_partials(p16s, metas):
    n = len(p16s)

    def body(*refs):
        ins, outs = refs[:n], refs[n:2 * n]
        ssem, rsem = refs[2 * n:]
        x, y, c = _coords()
        sme = 2 * x + y
        chips = _other_chips(x, y)

        def run(s):
            copies = []
            for t, (kind, R, C) in enumerate(metas):
                for j in range(3):
                    sj = _flip_index(s, j)
                    src = ins[t].at[:, sj * C:(sj + 1) * C] if kind == "col" else ins[t].at[sj]
                    copies.append(pltpu.make_async_remote_copy(
                        src_ref=src, dst_ref=outs[t].at[j], send_sem=ssem.at[3 * t + j], recv_sem=rsem.at[3 * t + j],
                        device_id=(chips[j][0], chips[j][1], c), device_id_type=MESH))
            for cp in copies:
                cp.start()
            for cp in copies:
                cp.wait()

        _for_my_chip(sme, run)

    out_shape = [_sds((3, R // 2, C), BF16) for (_, R, C) in metas]
    return _pallas(body, name="scatter_partials", in_specs=[ANY] * n, out_specs=[ANY] * n, out_shape=out_shape,
                   scratch_shapes=[pltpu.SemaphoreType.DMA((3 * n,))] * 2,
                   compiler_params=pltpu.CompilerParams(has_side_effects=True))(*p16s)


def final_sum(me, own, q, buf, l, meta, name):
    _, R, C = meta
    h = R // 2

    def body(me_ref, own_ref, q_ref, buf_ref, o_ref):
        del buf_ref
        o_ref[...] = ((own_ref[...] + q_ref[0].astype(F32)) + q_ref[1].astype(F32)) + q_ref[2].astype(F32)

    grid_spec = pltpu.PrefetchScalarGridSpec(
        num_scalar_prefetch=1, grid=(1,),
        in_specs=[pl.BlockSpec((h, C), lambda i, me_ref: (0, 0)),
                  pl.BlockSpec((3, h, C), lambda i, me_ref: (0, 0, 0)), ANY],
        out_specs=pl.BlockSpec((None, h, C), lambda i, me_ref: (l, me_ref[0], 0)))
    return _pallas(body, name=name, grid_spec=grid_spec, out_shape=_sds(buf.shape, F32),
                   input_output_aliases={3: 0},
                   compiler_params=_params(("arbitrary",), VMEM_BIG))(me, own, q, buf)


def share_final(bufs):
    n = len(bufs)

    def body(*refs):
        ins, outs = refs[:n], refs[n:2 * n]
        ssem, rsem = refs[2 * n:]
        del ins
        x, y, c = _coords()
        copies = []
        for t in range(n):
            R = bufs[t].shape[1]
            r0 = pl.multiple_of(c * (R // 2), 8)
            blk = outs[t].at[:, pl.ds(r0, R // 2), :]
            copies.append(pltpu.make_async_remote_copy(
                src_ref=blk, dst_ref=blk, send_sem=ssem.at[t], recv_sem=rsem.at[t],
                device_id=(x, y, 1 - c), device_id_type=MESH))
        for cp in copies:
            cp.start()
        for t in range(n):
            R = bufs[t].shape[1]
            r1 = pl.multiple_of((1 - c) * (R // 2), 8)
            other = outs[t].at[:, pl.ds(r1, R // 2), :]
            pltpu.make_async_remote_copy(
                src_ref=other, dst_ref=other, send_sem=ssem.at[t], recv_sem=rsem.at[t],
                device_id=(x, y, 1 - c), device_id_type=MESH).wait_recv()
        for cp in copies:
            cp.wait_send()

    out_shape = [_sds(b.shape, F32) for b in bufs]
    return _pallas(body, name="share_final", in_specs=[ANY] * n, out_specs=[ANY] * n, out_shape=out_shape,
                   input_output_aliases={t: t for t in range(n)},
                   scratch_shapes=[pltpu.SemaphoreType.DMA((n,))] * 2,
                   compiler_params=pltpu.CompilerParams(has_side_effects=True))(*bufs)


def allreduce_small(part):
    rows = part.shape[0]

    def body(p_ref, o_ref, buf, ssem, rsem, lsem):
        x, y, c = _coords()
        me = 4 * x + 2 * y + c
        mine = pltpu.make_async_copy(p_ref, buf.at[me], lsem)
        mine.start()
        copies = []
        for r in range(1, 8):
            fx, fy, fc = (r >> 2) & 1, (r >> 1) & 1, r & 1
            peer = ((1 - x) if fx else x, (1 - y) if fy else y, (1 - c) if fc else c)
            copies.append(pltpu.make_async_remote_copy(
                src_ref=p_ref, dst_ref=buf.at[me], send_sem=ssem.at[r - 1], recv_sem=rsem.at[r - 1],
                device_id=peer, device_id_type=MESH))
        for cp in copies:
            cp.start()
        for r in range(1, 8):
            fx, fy, fc = (r >> 2) & 1, (r >> 1) & 1, r & 1
            src = 4 * ((1 - x) if fx else x) + 2 * ((1 - y) if fy else y) + ((1 - c) if fc else c)
            pltpu.make_async_remote_copy(
                src_ref=buf.at[src], dst_ref=buf.at[src], send_sem=ssem.at[r - 1], recv_sem=rsem.at[r - 1],
                device_id=(x, y, c), device_id_type=MESH).wait_recv()
        for cp in copies:
            cp.wait_send()
        mine.wait()
        acc = buf[0]
        for d in range(1, 8):
            acc = acc + buf[d]
        o_ref[...] = acc

    return _pallas(body, name="allreduce_small",
                   in_specs=[pl.BlockSpec(memory_space=pltpu.VMEM)], out_specs=pl.BlockSpec(memory_space=pltpu.VMEM),
                   out_shape=_sds((rows, 128), F32),
                   scratch_shapes=[pltpu.VMEM((8, rows, 128), F32), pltpu.SemaphoreType.DMA((7,)),
                                   pltpu.SemaphoreType.DMA((7,)), pltpu.SemaphoreType.DMA],
                   compiler_params=pltpu.CompilerParams(has_side_effects=True, vmem_limit_bytes=VMEM_BIG))(part)


def _rows128(a):
    flat = a.reshape(-1)
    rows = -(-flat.shape[0] // 128)
    rows8 = -(-rows // 8) * 8
    flat = jnp.pad(flat, (0, rows8 * 128 - flat.shape[0]))
    return flat.reshape(rows8, 128)


def kernel(x, norm_mix_g, norm_ffn_g, final_g, a_w_in, a_v_gain, a_w_s, a_b_s, a_w_out, b_w_qkv, b_rel_bias, b_w_out, ffn_w_gate, ffn_w_up, ffn_w_down, loss_target, m_norm_mix_g, m_norm_ffn_g, m_final_g, m_a_w_in, m_a_v_gain, m_a_w_s, m_a_b_s, m_a_w_out, m_b_w_qkv, m_b_rel_bias, m_b_w_out, m_ffn_w_gate, m_ffn_w_up, m_ffn_w_down, v_norm_mix_g, v_norm_ffn_g, v_final_g, v_a_w_in, v_a_v_gain, v_a_w_s, v_a_b_s, v_a_w_out, v_b_w_qkv, v_b_rel_bias, v_b_w_out, v_ffn_w_gate, v_ffn_w_up, v_ffn_w_down):
    T = x.shape[1]
    weights = dict(norm_mix_g=norm_mix_g, norm_ffn_g=norm_ffn_g, final_g=final_g, a_w_in=a_w_in, a_v_gain=a_v_gain,
                   a_w_s=a_w_s, a_b_s=a_b_s, a_w_out=a_w_out, b_w_qkv=b_w_qkv, b_rel_bias=b_rel_bias,
                   b_w_out=b_w_out, ffn_w_gate=ffn_w_gate, ffn_w_up=ffn_w_up, ffn_w_down=ffn_w_down)
    mom_m = dict(norm_mix_g=m_norm_mix_g, norm_ffn_g=m_norm_ffn_g, final_g=m_final_g, a_w_in=m_a_w_in,
                 a_v_gain=m_a_v_gain, a_w_s=m_a_w_s, a_b_s=m_a_b_s, a_w_out=m_a_w_out, b_w_qkv=m_b_w_qkv,
                 b_rel_bias=m_b_rel_bias, b_w_out=m_b_w_out, ffn_w_gate=m_ffn_w_gate, ffn_w_up=m_ffn_w_up,
                 ffn_w_down=m_ffn_w_down)
    mom_v = dict(norm_mix_g=v_norm_mix_g, norm_ffn_g=v_norm_ffn_g, final_g=v_final_g, a_w_in=v_a_w_in,
                 a_v_gain=v_a_v_gain, a_w_s=v_a_w_s, a_b_s=v_a_b_s, a_w_out=v_a_w_out, b_w_qkv=v_b_w_qkv,
                 b_rel_bias=v_b_rel_bias, b_w_out=v_b_w_out, ffn_w_gate=v_ffn_w_gate, ffn_w_up=v_ffn_w_up,
                 ffn_w_down=v_ffn_w_down)
    order = list(weights)
    transposed = ("ffn_w_gate", "ffn_w_up")
    for k in transposed:
        weights[k], mom_m[k], mom_v[k] = (jnp.swapaxes(a, 1, 2) for a in (weights[k], mom_m[k], mom_v[k]))

    xi, yi, ci = _coords()
    me = jnp.stack([ci, 2 * xi + yi]).astype(jnp.int32)

    full = gather_weights({k: cast_bf16(weights[k], "cast_" + k) for k in BIG})
    w_in, w_qkv = full["a_w_in"], full["b_w_qkv"]
    w_aout = full["a_w_out"].reshape(2, GH, D)
    w_bout = full["b_w_out"].reshape(2, D, D)
    w_gate, w_up, w_down = full["ffn_w_gate"], full["ffn_w_up"], full["ffn_w_down"]

    row512 = lambda: pl.BlockSpec((512, D), lambda i, j: (i, 0))

    xcur = x.reshape(T, D)
    saved = []
    for i in range(DEPTH):
        j = i // 2
        tag = "_l%d" % i
        st = {"x_in": xcur}
        hn = rms_fwd(xcur, norm_mix_g[i][None], "rms_mix" + tag)
        st["hn"] = hn
        if i % 2 == 0:
            pre = matmul("a_in" + tag, NN, hn, pl.BlockSpec((1024, D), lambda i_, j_: (i_, 0)),
                         w_in, pl.BlockSpec((None, D, 1024), lambda i_, j_, l=j: (l, 0, j_)),
                         _sds((T, 2 * GH), BF16), pl.BlockSpec((1024, 1024), lambda i_, j_: (i_, j_)),
                         (T // 1024, 4))
            y = sgu_fwd(pre, a_v_gain[j][None], a_w_s[j], a_b_s[j][:, :, None], "sgu_fwd" + tag)
            xmid = matmul("a_out" + tag, NN, y, pl.BlockSpec((512, GH), lambda i_, j_: (i_, 0)),
                          w_aout, pl.BlockSpec((None, GH, D), lambda i_, j_, l=j: (l, 0, 0)),
                          _sds((T, D), F32), row512(), (T // 512, 1), res=xcur, res_spec=row512())
            st.update(pre=pre, y=y)
        else:
            qkvp = proj_qkv(hn, w_qkv, j, "b_qkv" + tag)
            wb = jnp.transpose(bias_build(b_rel_bias[j], "bias_build" + tag), (1, 0, 2))
            o = attn_fwd(qkvp, wb, "attn_fwd" + tag)
            xmid = matmul("b_out" + tag, NN, o, pl.BlockSpec((512, D), lambda i_, j_: (i_, 0)),
                          w_bout, pl.BlockSpec((None, D, D), lambda i_, j_, l=j: (l, 0, 0)),
                          _sds((T, D), F32), row512(), (T // 512, 1), res=xcur, res_spec=row512())
            st.update(qkvp=qkvp, wb=wb, o=o)
        hn2 = rms_fwd(xmid, norm_ffn_g[i][None], "rms_ffn" + tag)
        g, u, h = ffn_up(hn2, w_gate, w_up, i, "ffn_up" + tag)
        xcur = ffn_down_res(h, w_down, xmid, i, "ffn_down" + tag)
        st.update(x_mid=xmid, hn2=hn2, g=g, u=u, h=h)
        saved.append(st)

    loss_part, dx, dxb, d_final = final_loss(xcur, final_g[None], loss_target.reshape(T, D), "final_loss")

    tk = min(2048, T)
    big_grads = {k: [None] * weights[k].shape[0] for k in BIG}
    small = {"norm_mix_g": [None] * DEPTH, "norm_ffn_g": [None] * DEPTH, "a_v_gain": [None] * 2,
             "a_w_s": [None] * 2, "a_b_s": [None] * 2, "b_rel_bias": [None] * 2}
    tok = lambda width: pl.BlockSpec((tk, width), lambda j_, k_: (k_, 0))
    for i in reversed(range(DEPTH)):
        j = i // 2
        tag = "_l%d" % i
        st = saved[i]
        dg, du = ffn_bwd_dh(dxb, w_down, st["g"], st["u"], i, "ffn_bwd_dh" + tag)
        part = lambda: pl.BlockSpec((None, tk, FS), lambda j_, k_: (j_, k_, 0))
        big_grads["ffn_w_down"][i] = wgrad(
            "dw_down" + tag, st["h"], part(), dxb, tok(D), _sds((N_CHIPS, FS, D), F32),
            pl.BlockSpec((None, FS, D), lambda j_, k_: (j_, 0, 0)), N_CHIPS, T, tk)
        dhn2 = ffn_bwd_dhn(dg, du, w_gate, w_up, i, "ffn_bwd_dhn" + tag)
        for nm, dz in (("ffn_w_gate", dg), ("ffn_w_up", du)):
            big_grads[nm][i] = wgrad(
                "d" + nm + tag, dz, part(), st["hn2"], tok(D), _sds((N_CHIPS, FS, D), F32),
                pl.BlockSpec((None, FS, D), lambda j_, k_: (j_, 0, 0)), N_CHIPS, T, tk)
        dx, dxb, dgn = rms_bwd(st["x_mid"], dhn2, norm_ffn_g[i][None], dx, "rms_ffn_bwd" + tag)
        small["norm_ffn_g"][i] = dgn
        if i % 2 == 0:
            dy = matmul("a_out_bwd" + tag, NT, dxb, pl.BlockSpec((1024, D), lambda i_, j_: (i_, 0)),
                        w_aout, pl.BlockSpec((None, 1024, D), lambda i_, j_, l=j: (l, j_, 0)),
                        _sds((T, GH), BF16), pl.BlockSpec((1024, 1024), lambda i_, j_: (i_, j_)), (T // 1024, 2))
            big_grads["a_w_out"][j] = wgrad(
                "dw_aout" + tag, st["y"], pl.BlockSpec((tk, 1024), lambda j_, k_: (k_, j_)), dxb, tok(D),
                _sds((GH, D), F32), pl.BlockSpec((1024, D), lambda j_, k_: (j_, 0)), 2, T, tk
            ).reshape(N_CHIPS, GH // N_CHIPS, D)
            dpre, d_ws, d_bs, d_gain = sgu_bwd(st["pre"], dy, a_v_gain[j][None], a_w_s[j], a_b_s[j][:, :, None],
                                               "sgu_bwd" + tag)
            small["a_w_s"][j], small["a_b_s"][j], small["a_v_gain"][j] = d_ws, d_bs, d_gain
            dhn = matmul("a_in_bwd" + tag, NT, dpre, pl.BlockSpec((512, 2 * GH), lambda i_, j_: (i_, 0)),
                         w_in, pl.BlockSpec((None, D, 2 * GH), lambda i_, j_, l=j: (l, 0, 0)),
                         _sds((T, D), F32), row512(), (T // 512, 1))
            big_grads["a_w_in"][j] = wgrad(
                "dw_in" + tag, st["hn"], tok(D), dpre, pl.BlockSpec((tk, 1024), lambda j_, k_: (k_, j_)),
                _sds((D, 2 * GH), F32), pl.BlockSpec((D, 1024), lambda j_, k_: (0, j_)), 4, T, tk)
        else:
            do = matmul("b_out_bwd" + tag, NT, dxb, pl.BlockSpec((1024, D), lambda i_, j_: (i_, 0)),
                        w_bout, pl.BlockSpec((None, D, D), lambda i_, j_, l=j: (l, 0, 0)),
                        _sds((T, D), BF16), pl.BlockSpec((1024, D), lambda i_, j_: (i_, 0)), (T // 1024, 1))
            big_grads["b_w_out"][j] = wgrad(
                "dw_bout" + tag, st["o"], tok(D), dxb, tok(D),
                _sds((D, D), F32), pl.BlockSpec((D, D), lambda j_, k_: (0, 0)), 1, T, tk
            ).reshape(N_CHIPS, D // N_CHIPS, D)
            dqkvp, dwb = attn_bwd(st["qkvp"], st["o"], do, st["wb"], "attn_bwd" + tag)
            small["b_rel_bias"][j] = bias_grad(jnp.transpose(dwb, (1, 0, 2)), "bias_grad" + tag)
            dhn = qkv_bwd(dqkvp, w_qkv, j, "b_qkv_bwd" + tag)
            big_grads["b_w_qkv"][j] = wgrad(
                "dw_qkv" + tag, st["hn"], tok(D), dqkvp,
                pl.BlockSpec((None, tk, D), lambda j_, k_: (j_, k_ + FRONT // tk, 0)),
                _sds((D, 3 * D), F32), pl.BlockSpec((D, D), lambda j_, k_: (0, j_)), 3, T, tk)
        dx, dxb, dgn = rms_bwd(st["x_in"], dhn, norm_mix_g[i][None], dx, "rms_mix_bwd" + tag)
        small["norm_mix_g"][i] = dgn

    small_grads = {
        "norm_mix_g": jnp.concatenate(small["norm_mix_g"], axis=0),
        "norm_ffn_g": jnp.concatenate(small["norm_ffn_g"], axis=0),
        "final_g": d_final.reshape(D),
        "a_v_gain": jnp.concatenate(small["a_v_gain"], axis=0),
        "a_w_s": jnp.stack(small["a_w_s"]),
        "a_b_s": jnp.stack(small["a_b_s"]).reshape(2, SGU_G, SGU_BLOCK),
        "b_rel_bias": jnp.stack(small["b_rel_bias"]),
    }
    small_names = list(small_grads)
    packed = [_rows128(small_grads[k]) for k in small_names] + [_rows128(loss_part[:, :1])]
    offs = [0]
    for p in packed:
        offs.append(offs[-1] + p.shape[0])
    reduced = allreduce_small(jnp.concatenate(packed, axis=0))
    grads = {}
    for t, k in enumerate(small_names):
        nelem = small_grads[k].size
        grads[k] = reduced[offs[t]:offs[t + 1]].reshape(-1)[:nelem].reshape(weights[k].shape)
    loss = reduced[offs[len(small_names)], 0]

    metas, flat, where = [], [], []
    for k in BIG:
        L, R, C = weights[k].shape
        for l in range(L):
            metas.append((GATHER_KIND[k], R, C))
            flat.append(big_grads[k][l])
            where.append((k, l))
    sib = exchange_halves(flat, metas)
    p16, own = [], []
    for t, (k, l) in enumerate(where):
        a, b = pair_sum(me, flat[t], sib[t], metas[t], "pair_sum_%s_l%d" % (k, l))
        p16.append(a)
        own.append(b)
    q = scatter_partials(p16, metas)
    bufs = {k: jnp.zeros(weights[k].shape, F32) for k in BIG}
    for t, (k, l) in enumerate(where):
        bufs[k] = final_sum(me, own[t], q[t], bufs[k], l, metas[t], "final_sum_%s_l%d" % (k, l))
    shared = share_final([bufs[k] for k in BIG])
    for k, gfull in zip(BIG, shared):
        grads[k] = gfull

    delta, new_m, new_v = {}, {}, {}
    for k in order:
        shp = weights[k].shape
        if k in BIG:
            view = shp
        elif k == "a_w_s":
            view = (2, SGU_G * SGU_BLOCK, SGU_BLOCK)
        elif len(shp) == 1:
            view = (1, 1, shp[0])
        elif len(shp) == 2:
            view = (1,) + shp
        else:
            view = shp
        d_, m_, v_ = adamw(weights[k].reshape(view), grads[k].reshape(view), mom_m[k].reshape(view),
                           mom_v[k].reshape(view), "adamw_" + k)
        delta[k], new_m[k], new_v[k] = d_.reshape(shp), m_.reshape(shp), v_.reshape(shp)
    for k in transposed:
        for tree in (grads, delta, new_m, new_v):
            tree[k] = jnp.swapaxes(tree[k], 1, 2)

    return (loss, dx.reshape(1, T, D), *[grads[k] for k in order], *[delta[k] for k in order],
            *[new_m[k] for k in order], *[new_v[k] for k in order])
```

```python
import functools

import jax
import jax.numpy as jnp
from jax import lax
from jax.experimental import pallas as pl
from jax.experimental.pallas import tpu as pltpu

F32 = jnp.float32
BF16 = jnp.bfloat16
MESH = pl.DeviceIdType.MESH

D = 1024
DEPTH = 4
EPS = 1e-6
SGU_BLOCK = 128
GH = 2048
SGU_G = 8
SGU_GD = GH // SGU_G
N_HEADS = 16
HEAD_DIM = 64
CHUNK = 64
PAD = 8 * CHUNK
FRONT = 2048
QB = 128
KW = PAD + QB
N_REL = 192
REL_MIN = -(CHUNK - 1)
REL_MAX = 128
D_FF = 2816
FS = D_FF // 4
NEG = -1e30
SCALE = HEAD_DIM ** -0.5
N_CHIPS = 4

ADAM_LR = 0.001
ADAM_B1 = 0.9
ADAM_B2 = 0.999
ADAM_EPS = 1e-08
ADAM_WD = 0.01
ADAM_STEP = 10

VMEM_BIG = 56 * 1024 * 1024

NN = ((1,), (0,))
NT = ((1,), (1,))
TN = ((0,), (0,))


def _dot(a, b, dims):
    return lax.dot_general(a, b, (dims, ((), ())), preferred_element_type=F32)


def _pallas(body, **kw):
    return pl.pallas_call(body, **kw)


def _params(sem=None, vmem=None):
    return pltpu.CompilerParams(dimension_semantics=sem, vmem_limit_bytes=vmem)


def _sds(shape, dtype):
    return jax.ShapeDtypeStruct(tuple(shape), dtype)


_GELU_C = 0.7978845608028654


def _gelu(x):
    t = jnp.tanh(_GELU_C * (x + 0.044715 * (x * x * x)))
    return 0.5 * x * (1.0 + t)


def _gelu_and_grad(x):
    x2 = x * x
    t = jnp.tanh(_GELU_C * (x + 0.044715 * (x2 * x)))
    val = 0.5 * x * (1.0 + t)
    grad = 0.5 * (1.0 + t) + 0.5 * x * (1.0 - t * t) * (_GELU_C * (1.0 + 3.0 * 0.044715 * x2))
    return val, grad


def _sigmoid(x):
    return 0.5 * (jnp.tanh(0.5 * x) + 1.0)


def cast_bf16(w, name):
    L, R, C = w.shape

    def body(w_ref, o_ref):
        o_ref[...] = w_ref[...].astype(BF16)

    spec = pl.BlockSpec((None, R, C), lambda l: (l, 0, 0))
    return _pallas(body, name=name, grid=(L,), in_specs=[spec], out_specs=spec,
                   out_shape=_sds((L, R, C), BF16), compiler_params=_params(("parallel",)))(w)


def rms_fwd(x, g, name, tm=512):
    T = x.shape[0]

    def body(x_ref, g_ref, o_ref):
        xf = x_ref[...]
        r = lax.rsqrt(jnp.mean(xf * xf, axis=-1, keepdims=True) + EPS)
        o_ref[...] = ((xf * r) * g_ref[...]).astype(BF16)

    row = pl.BlockSpec((tm, D), lambda i: (i, 0))
    return _pallas(body, name=name, grid=(T // tm,),
                   in_specs=[row, pl.BlockSpec((1, D), lambda i: (0, 0))], out_specs=row,
                   out_shape=_sds((T, D), BF16), compiler_params=_params(("parallel",)))(x, g)


def rms_bwd(x, dh, g, dres, name, tm=256):
    T = x.shape[0]
    n = T // tm

    def body(x_ref, dh_ref, g_ref, dres_ref, dx_ref, dxb_ref, dg_ref, acc_ref):
        i = pl.program_id(0)
        xf = x_ref[...]
        r = lax.rsqrt(jnp.mean(xf * xf, axis=-1, keepdims=True) + EPS)
        xhat = xf * r
        dhf = dh_ref[...]
        part = (dhf * xhat).reshape(tm // 8, 8, D).sum(axis=0)

        @pl.when(i == 0)
        def _():
            acc_ref[...] = part

        @pl.when(i > 0)
        def _():
            acc_ref[...] += part

        dxhat = dhf * g_ref[...]
        dx = dres_ref[...] + r * (dxhat - xhat * jnp.mean(dxhat * xhat, axis=-1, keepdims=True))
        dx_ref[...] = dx
        dxb_ref[...] = dx.astype(BF16)

        @pl.when(i == n - 1)
        def _():
            dg_ref[...] = jnp.sum(acc_ref[...], axis=0, keepdims=True)

    row = pl.BlockSpec((tm, D), lambda i: (i, 0))
    vec = pl.BlockSpec((1, D), lambda i: (0, 0))
    return _pallas(body, name=name, grid=(n,), in_specs=[row, row, vec, row], out_specs=[row, row, vec],
                   out_shape=[_sds((T, D), F32), _sds((T, D), BF16), _sds((1, D), F32)],
                   scratch_shapes=[pltpu.VMEM((8, D), F32)],
                   compiler_params=_params(("arbitrary",)))(x, dh, g, dres)


def final_loss(x, g, tgt, name, tm=256):
    T = x.shape[0]
    n = T // tm

    def body(x_ref, g_ref, t_ref, loss_ref, dx_ref, dxb_ref, dg_ref, acc_ref, lacc_ref):
        i = pl.program_id(0)
        xf = x_ref[...]
        r = lax.rsqrt(jnp.mean(xf * xf, axis=-1, keepdims=True) + EPS)
        xhat = xf * r
        gg = g_ref[...]
        e = xhat * gg - t_ref[...]
        dy = e * (1.0 / D)
        part = (dy * xhat).reshape(tm // 8, 8, D).sum(axis=0)
        lpart = (e * e).reshape(tm // 8, 8, D).sum(axis=0)

        @pl.when(i == 0)
        def _():
            acc_ref[...] = part
            lacc_ref[...] = lpart

        @pl.when(i > 0)
        def _():
            acc_ref[...] += part
            lacc_ref[...] += lpart

        dxhat = dy * gg
        dx = r * (dxhat - xhat * jnp.mean(dxhat * xhat, axis=-1, keepdims=True))
        dx_ref[...] = dx
        dxb_ref[...] = dx.astype(BF16)

        @pl.when(i == n - 1)
        def _():
            dg_ref[...] = jnp.sum(acc_ref[...], axis=0, keepdims=True)
            total = jnp.sum(jnp.sum(lacc_ref[...], axis=0, keepdims=True), axis=1, keepdims=True)
            loss_ref[...] = jnp.broadcast_to(total * (0.5 / D), (1, 128))

    row = pl.BlockSpec((tm, D), lambda i: (i, 0))
    vec = pl.BlockSpec((1, D), lambda i: (0, 0))
    return _pallas(body, name=name, grid=(n,), in_specs=[row, vec, row],
                   out_specs=[pl.BlockSpec((1, 128), lambda i: (0, 0)), row, row, vec],
                   out_shape=[_sds((1, 128), F32), _sds((T, D), F32), _sds((T, D), BF16), _sds((1, D), F32)],
                   scratch_shapes=[pltpu.VMEM((8, D), F32), pltpu.VMEM((8, D), F32)],
                   compiler_params=_params(("arbitrary",)))(x, g, tgt)


def matmul(name, dims, a, a_spec, b, b_spec, out_shape, out_spec, grid, *, acc=False, res=None, res_spec=None):
    has_res = res is not None

    def body(*refs):
        a_ref, b_ref = refs[0], refs[1]
        r_ref = refs[2] if has_res else None
        o_ref = refs[-1]
        d = _dot(a_ref[...], b_ref[...], dims)
        if not acc:
            if has_res:
                d = d + r_ref[...]
            o_ref[...] = d.astype(o_ref.dtype)
        else:
            k = pl.program_id(len(grid) - 1)

            @pl.when(k == 0)
            def _():
                o_ref[...] = (d + r_ref[...]) if has_res else d

            @pl.when(k > 0)
            def _():
                o_ref[...] += d

    sem = ("parallel",) * (len(grid) - 1) + (("arbitrary",) if acc else ("parallel",))
    ins = [a, b] + ([res] if has_res else [])
    specs = [a_spec, b_spec] + ([res_spec] if has_res else [])
    return _pallas(body, name=name, grid=grid, in_specs=specs, out_specs=out_spec, out_shape=out_shape,
                   compiler_params=_params(sem, VMEM_BIG))(*ins)


def wgrad(name, a, a_spec, b, b_spec, out_shape, out_spec, J, T, tk):
    return matmul(name, TN, a, a_spec, b, b_spec, out_shape, out_spec, (J, T // tk), acc=True)


def _sgu_mask():
    p = lax.broadcasted_iota(jnp.int32, (SGU_BLOCK, SGU_BLOCK), 0)
    q = lax.broadcasted_iota(jnp.int32, (SGU_BLOCK, SGU_BLOCK), 1)
    return lax.shift_right_logical(q, 6) <= lax.shift_right_logical(p, 6)


def sgu_fwd(pre, gain, w_s, b_s, name):
    T = pre.shape[0]

    def body(pre_ref, gain_ref, ws_ref, bs_ref, y_ref):
        mask = _sgu_mask()
        u = _gelu(pre_ref[:, :GH].astype(F32))
        va = _gelu(pre_ref[:, GH:].astype(F32))
        r = lax.rsqrt(jnp.mean(va * va, axis=-1, keepdims=True) + EPS)
        vn = ((va * r) * gain_ref[...]).astype(BF16)
        for g in range(SGU_G):
            sl = slice(g * SGU_GD, (g + 1) * SGU_GD)
            wm = jnp.where(mask, ws_ref[g], 0.0).astype(BF16)
            vm = _dot(wm, vn[:, sl], NN) + bs_ref[g]
            y_ref[:, sl] = (u[:, sl] * vm).astype(BF16)

    return _pallas(
        body, name=name, grid=(T // SGU_BLOCK,),
        in_specs=[pl.BlockSpec((SGU_BLOCK, 2 * GH), lambda i: (i, 0)),
                  pl.BlockSpec((1, GH), lambda i: (0, 0)),
                  pl.BlockSpec((SGU_G, SGU_BLOCK, SGU_BLOCK), lambda i: (0, 0, 0)),
                  pl.BlockSpec((SGU_G, SGU_BLOCK, 1), lambda i: (0, 0, 0))],
        out_specs=pl.BlockSpec((SGU_BLOCK, GH), lambda i: (i, 0)),
        out_shape=_sds((T, GH), BF16), compiler_params=_params(("parallel",)))(pre, gain, w_s, b_s)


def sgu_bwd(pre, dy, gain, w_s, b_s, name):
    T = pre.shape[0]
    n = T // SGU_BLOCK

    def body(pre_ref, dy_ref, gain_ref, ws_ref, bs_ref, dpre_ref, dws_ref, dbs_ref, dgain_ref, gacc_ref):
        i = pl.program_id(0)

        @pl.when(i == 0)
        def _():
            dws_ref[...] = jnp.zeros_like(dws_ref)
            dbs_ref[...] = jnp.zeros_like(dbs_ref)
            gacc_ref[...] = jnp.zeros_like(gacc_ref)

        mask = _sgu_mask()
        u, du_dpre = _gelu_and_grad(pre_ref[:, :GH].astype(F32))
        va, dva_dpre = _gelu_and_grad(pre_ref[:, GH:].astype(F32))
        r = lax.rsqrt(jnp.mean(va * va, axis=-1, keepdims=True) + EPS)
        vhat = va * r
        gain_v = gain_ref[...]
        vn = (vhat * gain_v).astype(BF16)
        dyf = dy_ref[...].astype(F32)
        dvn_parts = []
        for g in range(SGU_G):
            sl = slice(g * SGU_GD, (g + 1) * SGU_GD)
            wm = jnp.where(mask, ws_ref[g], 0.0).astype(BF16)
            vm = _dot(wm, vn[:, sl], NN) + bs_ref[g]
            dpre_ref[:, sl] = ((dyf[:, sl] * vm) * du_dpre[:, sl]).astype(BF16)
            dvm = dyf[:, sl] * u[:, sl]
            dbs_ref[g] += jnp.sum(dvm, axis=-1, keepdims=True)
            dvm16 = dvm.astype(BF16)
            dws_ref[g] += jnp.where(mask, _dot(dvm16, vn[:, sl], NT), 0.0)
            dvn_parts.append(_dot(wm, dvm16, TN))
        dvn = jnp.concatenate(dvn_parts, axis=-1)
        gacc_ref[...] += (dvn * vhat).reshape(SGU_BLOCK // 8, 8, GH).sum(axis=0)
        dvhat = dvn * gain_v
        dva = r * (dvhat - vhat * jnp.mean(dvhat * vhat, axis=-1, keepdims=True))
        dpre_ref[:, GH:] = (dva * dva_dpre).astype(BF16)

        @pl.when(i == n - 1)
        def _():
            dgain_ref[...] = jnp.sum(gacc_ref[...], axis=0, keepdims=True)

    const3 = lambda i: (0, 0, 0)
    return _pallas(
        body, name=name, grid=(n,),
        in_specs=[pl.BlockSpec((SGU_BLOCK, 2 * GH), lambda i: (i, 0)),
                  pl.BlockSpec((SGU_BLOCK, GH), lambda i: (i, 0)),
                  pl.BlockSpec((1, GH), lambda i: (0, 0)),
                  pl.BlockSpec((SGU_G, SGU_BLOCK, SGU_BLOCK), const3),
                  pl.BlockSpec((SGU_G, SGU_BLOCK, 1), const3)],
        out_specs=[pl.BlockSpec((SGU_BLOCK, 2 * GH), lambda i: (i, 0)),
                   pl.BlockSpec((SGU_G, SGU_BLOCK, SGU_BLOCK), const3),
                   pl.BlockSpec((SGU_G, SGU_BLOCK, 1), const3),
                   pl.BlockSpec((1, GH), lambda i: (0, 0))],
        out_shape=[_sds((T, 2 * GH), BF16), _sds((SGU_G, SGU_BLOCK, SGU_BLOCK), F32),
                   _sds((SGU_G, SGU_BLOCK, 1), F32), _sds((1, GH), F32)],
        scratch_shapes=[pltpu.VMEM((8, GH), F32)],
        compiler_params=_params(("arbitrary",)))(pre, dy, gain, w_s, b_s)


def _rel_onehot(i):
    j = lax.broadcasted_iota(jnp.int32, (N_REL, KW), 1)
    r = lax.broadcasted_iota(jnp.int32, (N_REL, KW), 0)
    idx = jnp.clip(i - j + PAD, REL_MIN, REL_MAX) - REL_MIN
    return (idx == r).astype(BF16)


def _split3(v):
    hi = v.astype(BF16)
    r1 = v - hi.astype(F32)
    mid = r1.astype(BF16)
    lo = (r1 - mid.astype(F32)).astype(BF16)
    return hi, mid, lo


def bias_build(rel_bias, name):
    def body(rb_ref, o_ref):
        parts = _split3(rb_ref[...])

        def row(i, carry):
            oh = _rel_onehot(i)
            val = _dot(parts[0], oh, NN) + _dot(parts[1], oh, NN) + _dot(parts[2], oh, NN)
            j = lax.broadcasted_iota(jnp.int32, (1, KW), 1)
            rel = lax.shift_right_logical(i, 6) - lax.shift_right_logical(j, 6) + 8
            ok = (rel >= 0) & (rel <= 8)
            o_ref[i] = jnp.where(ok, val, NEG)
            return carry

        lax.fori_loop(0, QB, row, 0)

    return _pallas(body, name=name, out_shape=_sds((QB, N_HEADS, KW), F32),
                   in_specs=[pl.BlockSpec(memory_space=pltpu.VMEM)],
                   out_specs=pl.BlockSpec(memory_space=pltpu.VMEM))(rel_bias)


def bias_grad(dwb, name):
    def body(d_ref, o_ref):
        def row(i, acc):
            oh = _rel_onehot(i)
            hi, mid, lo = _split3(d_ref[i])
            return acc + (_dot(hi, oh, NT) + _dot(mid, oh, NT) + _dot(lo, oh, NT))

        o_ref[...] = lax.fori_loop(0, QB, row, jnp.zeros((N_HEADS, N_REL), F32))

    return _pallas(body, name=name, out_shape=_sds((N_HEADS, N_REL), F32),
                   in_specs=[pl.BlockSpec(memory_space=pltpu.VMEM)],
                   out_specs=pl.BlockSpec(memory_space=pltpu.VMEM))(dwb)


def _attn_block(qkv_ref, blk):
    r0 = pl.multiple_of(blk * QB, QB)
    qs = qkv_ref[0, pl.ds(r0 + FRONT, QB), :] * SCALE
    k2 = qkv_ref[1, pl.ds(r0 + (FRONT - PAD), KW), :]
    v2 = qkv_ref[2, pl.ds(r0 + (FRONT - PAD), KW), :]
    col = lax.broadcasted_iota(jnp.int32, (1, KW), 1)
    return r0, qs, k2, v2, col >= PAD - blk * QB


def _head_mask(h):
    lane = lax.broadcasted_iota(jnp.int32, (1, 2 * HEAD_DIM), 1)
    return (lane < HEAD_DIM) if h == 0 else (lane >= HEAD_DIM)


def _stack_heads(a):
    zero = jnp.zeros_like(a)
    return jnp.concatenate([jnp.where(_head_mask(0), a, zero), jnp.where(_head_mask(1), a, zero)], axis=0)


def _unstack_heads(a):
    return jnp.where(_head_mask(0), a[:QB], a[QB:])


def _attn_exp(qst, k2, w_ref, kvalid):
    s = jnp.where(kvalid, _dot(qst, k2, NT) + w_ref[...].reshape(2 * QB, KW), NEG)
    e = jnp.exp(s - jnp.max(s, axis=-1, keepdims=True))
    return e, 1.0 / jnp.sum(e, axis=-1, keepdims=True)


ATTN_G = 2


def attn_fwd(qkvp, wb, name):
    T = qkvp.shape[1] - FRONT

    def body(qkv_ref, w_ref, o_ref):
        b = pl.program_id(1)
        for t in range(ATTN_G):
            _, qs, k2, v2, kvalid = _attn_block(qkv_ref, b * ATTN_G + t)
            e, inv = _attn_exp(_stack_heads(qs), k2, w_ref, kvalid)
            o_ref[t * QB:(t + 1) * QB, :] = _unstack_heads(_dot(e.astype(BF16), v2, NN) * inv).astype(BF16)

    return _pallas(
        body, name=name, grid=(N_HEADS // 2, T // (QB * ATTN_G)),
        in_specs=[pl.BlockSpec((3, FRONT + T, 2 * HEAD_DIM), lambda hp, b: (0, 0, hp)),
                  pl.BlockSpec((2, QB, KW), lambda hp, b: (hp, 0, 0))],
        out_specs=pl.BlockSpec((QB * ATTN_G, 2 * HEAD_DIM), lambda hp, b: (b, hp)),
        out_shape=_sds((T, D), BF16),
        compiler_params=_params(("parallel", "arbitrary"), VMEM_BIG))(qkvp, wb)


def attn_bwd(qkvp, o, do, wb, name):
    T = qkvp.shape[1] - FRONT
    nb = T // (QB * ATTN_G)

    def body(qkv_ref, o_ref, do_ref, w_ref, dqkv_ref, dw_ref, dk_acc, dv_acc):
        b = pl.program_id(1)

        @pl.when(b == 0)
        def _():
            dk_acc[...] = jnp.zeros_like(dk_acc)
            dv_acc[...] = jnp.zeros_like(dv_acc)
            dw_ref[...] = jnp.zeros_like(dw_ref)
            dqkv_ref[0, 0:FRONT, :] = jnp.zeros((FRONT, 2 * HEAD_DIM), BF16)

        dws = None
        for t in range(ATTN_G):
            r0, qs, k2, v2, kvalid = _attn_block(qkv_ref, b * ATTN_G + t)
            qst = _stack_heads(qs)
            e, inv = _attn_exp(qst, k2, w_ref, kvalid)
            do2 = do_ref[t * QB:(t + 1) * QB, :]
            dost = _stack_heads(do2)
            prod = _stack_heads(do2.astype(F32) * o_ref[t * QB:(t + 1) * QB, :].astype(F32))
            delta = jnp.sum(prod, axis=-1, keepdims=True)
            ds = e * ((_dot(dost, v2, NT) - delta) * inv)
            dws = ds if dws is None else dws + ds
            ds16 = ds.astype(BF16)
            dq = _unstack_heads(_dot(ds16, k2, NN)) * SCALE
            dqkv_ref[0, pl.ds(r0 + FRONT, QB), :] = dq.astype(BF16)
            dk_acc[pl.ds(r0 + (FRONT - PAD), KW), :] += _dot(ds16, qst, TN)
            dv_acc[pl.ds(r0 + (FRONT - PAD), KW), :] += _dot(e.astype(BF16), (dost.astype(F32) * inv).astype(BF16), TN)
        dw_ref[...] += dws.reshape(2, QB, KW)

        @pl.when(b == nb - 1)
        def _():
            dqkv_ref[1] = dk_acc[...].astype(BF16)
            dqkv_ref[2] = dv_acc[...].astype(BF16)

    slab = pl.BlockSpec((3, FRONT + T, 2 * HEAD_DIM), lambda hp, b: (0, 0, hp))
    wspec = pl.BlockSpec((2, QB, KW), lambda hp, b: (hp, 0, 0))
    rows = pl.BlockSpec((QB * ATTN_G, 2 * HEAD_DIM), lambda hp, b: (b, hp))
    return _pallas(
        body, name=name, grid=(N_HEADS // 2, nb),
        in_specs=[slab, rows, rows, wspec],
        out_specs=[slab, wspec],
        out_shape=[_sds((3, FRONT + T, D), BF16), _sds((N_HEADS, QB, KW), F32)],
        scratch_shapes=[pltpu.VMEM((FRONT + T, 2 * HEAD_DIM), F32), pltpu.VMEM((FRONT + T, 2 * HEAD_DIM), F32)],
        compiler_params=_params(("parallel", "arbitrary"), VMEM_BIG))(qkvp, o, do, wb)


def proj_qkv(hn, w, l, name, tm=512):
    T = hn.shape[0]
    pb = FRONT // tm

    def body(a_ref, b_ref, o_ref):
        i = pl.program_id(1)

        @pl.when(i < pb)
        def _():
            o_ref[...] = jnp.zeros_like(o_ref)

        @pl.when(i >= pb)
        def _():
            o_ref[...] = _dot(a_ref[...], b_ref[...], NN).astype(BF16)

    return _pallas(
        body, name=name, grid=(3, pb + T // tm),
        in_specs=[pl.BlockSpec((tm, D), lambda p, i: (jnp.maximum(i - pb, 0), 0)),
                  pl.BlockSpec((None, D, D), lambda p, i: (l, 0, p))],
        out_specs=pl.BlockSpec((None, tm, D), lambda p, i: (p, i, 0)),
        out_shape=_sds((3, FRONT + T, D), BF16),
        compiler_params=_params(("parallel", "parallel"), VMEM_BIG))(hn, w)


def ffn_up(hn, wg, wu, l, name, tm=1024):
    T = hn.shape[0]

    def body(a_ref, wg_ref, wu_ref, g_ref, u_ref, h_ref):
        a = a_ref[...]
        g = _dot(a, wg_ref[...], NT)
        u = _dot(a, wu_ref[...], NT)
        g_ref[...] = g.astype(BF16)
        u_ref[...] = u.astype(BF16)
        h_ref[...] = ((g * _sigmoid(g)) * u).astype(BF16)

    wspec = pl.BlockSpec((None, None, FS, D), lambda s, i: (l, s, 0, 0))
    ospec = pl.BlockSpec((None, tm, FS), lambda s, i: (s, i, 0))
    return _pallas(
        body, name=name, grid=(N_CHIPS, T // tm),
        in_specs=[pl.BlockSpec((tm, D), lambda s, i: (i, 0)), wspec, wspec],
        out_specs=[ospec, ospec, ospec],
        out_shape=[_sds((N_CHIPS, T, FS), BF16)] * 3,
        compiler_params=_params(("parallel", "parallel"), VMEM_BIG))(hn, wg, wu)


def ffn_bwd_dh(dxb, wd, g, u, l, name, tm=1024):
    T = dxb.shape[0]

    def body(a_ref, wd_ref, g_ref, u_ref, dg_ref, du_ref):
        dh = _dot(a_ref[...], wd_ref[...], NT)
        gf = g_ref[...].astype(F32)
        uf = u_ref[...].astype(F32)
        s = _sigmoid(gf)
        dg_ref[...] = (dh * uf * (s * (1.0 + gf * (1.0 - s)))).astype(BF16)
        du_ref[...] = (dh * (gf * s)).astype(BF16)

    aspec = pl.BlockSpec((None, tm, FS), lambda s, i: (s, i, 0))
    return _pallas(
        body, name=name, grid=(N_CHIPS, T // tm),
        in_specs=[pl.BlockSpec((tm, D), lambda s, i: (i, 0)),
                  pl.BlockSpec((None, None, FS, D), lambda s, i: (l, s, 0, 0)), aspec, aspec],
        out_specs=[aspec, aspec],
        out_shape=[_sds((N_CHIPS, T, FS), BF16)] * 2,
        compiler_params=_params(("parallel", "parallel"), VMEM_BIG))(dxb, wd, g, u)


def ffn_bwd_dhn(dg, du, wg, wu, l, name, tm=512):
    T = dg.shape[1]

    def body(dg_ref, du_ref, wg_ref, wu_ref, o_ref):
        d = None
        for s in range(N_CHIPS):
            t = _dot(dg_ref[s], wg_ref[s], NN) + _dot(du_ref[s], wu_ref[s], NN)
            d = t if d is None else d + t
        o_ref[...] = d

    aspec = pl.BlockSpec((N_CHIPS, tm, FS), lambda i: (0, i, 0))
    wspec = pl.BlockSpec((None, N_CHIPS, FS, D), lambda i: (l, 0, 0, 0))
    return _pallas(
        body, name=name, grid=(T // tm,), in_specs=[aspec, aspec, wspec, wspec],
        out_specs=pl.BlockSpec((tm, D), lambda i: (i, 0)), out_shape=_sds((T, D), F32),
        compiler_params=_params(("parallel",), VMEM_BIG))(dg, du, wg, wu)


def ffn_down_res(h, wd, res, l, name, tm=512):
    T = h.shape[1]

    def body(h_ref, wd_ref, r_ref, o_ref):
        d = r_ref[...]
        for s in range(N_CHIPS):
            d = d + _dot(h_ref[s], wd_ref[s], NN)
        o_ref[...] = d

    row = pl.BlockSpec((tm, D), lambda i: (i, 0))
    return _pallas(
        body, name=name, grid=(T // tm,),
        in_specs=[pl.BlockSpec((N_CHIPS, tm, FS), lambda i: (0, i, 0)),
                  pl.BlockSpec((None, N_CHIPS, FS, D), lambda i: (l, 0, 0, 0)), row],
        out_specs=row, out_shape=_sds((T, D), F32),
        compiler_params=_params(("parallel",), VMEM_BIG))(h, wd, res)


def qkv_bwd(dqkvp, w, l, name, tm=512):
    T = dqkvp.shape[1] - FRONT

    def body(a_ref, w_ref, o_ref):
        d = None
        for p in range(3):
            t = _dot(a_ref[p], w_ref[:, p * D:(p + 1) * D], NT)
            d = t if d is None else d + t
        o_ref[...] = d

    return _pallas(
        body, name=name, grid=(T // tm,),
        in_specs=[pl.BlockSpec((3, tm, D), lambda i: (0, i + FRONT // tm, 0)),
                  pl.BlockSpec((None, D, 3 * D), lambda i: (l, 0, 0))],
        out_specs=pl.BlockSpec((tm, D), lambda i: (i, 0)), out_shape=_sds((T, D), F32),
        compiler_params=_params(("parallel",), VMEM_BIG))(dqkvp, w)


def adamw(w, g, m, v, name):
    L, R, C = w.shape

    def body(w_ref, g_ref, m_ref, v_ref, d_ref, nm_ref, nv_ref):
        gf = g_ref[...]
        nm = ADAM_B1 * m_ref[...] + (1.0 - ADAM_B1) * gf
        nv = ADAM_B2 * v_ref[...] + (1.0 - ADAM_B2) * (gf * gf)
        m_hat = nm / (1.0 - ADAM_B1 ** ADAM_STEP)
        v_hat = nv / (1.0 - ADAM_B2 ** ADAM_STEP)
        d_ref[...] = -ADAM_LR * (m_hat / (jnp.sqrt(v_hat) + ADAM_EPS) + ADAM_WD * w_ref[...])
        nm_ref[...] = nm
        nv_ref[...] = nv

    tr = R // 4 if R % 32 == 0 else R
    spec = pl.BlockSpec((None, tr, C), lambda l, r: (l, r, 0))
    return _pallas(body, name=name, grid=(L, R // tr), in_specs=[spec] * 4, out_specs=[spec] * 3,
                   out_shape=[_sds((L, R, C), F32)] * 3,
                   compiler_params=_params(("parallel", "parallel")))(w, g, m, v)


def _coords():
    return lax.axis_index("x"), lax.axis_index("y"), lax.axis_index("c")


def _other_chips(x, y):
    out = []
    for fx, fy in ((1, 0), (0, 1), (1, 1)):
        px = (1 - x) if fx else x
        py = (1 - y) if fy else y
        out.append((px, py))
    return out


def _flip_index(s, j):
    sx, sy = s // 2, s % 2
    fx, fy = ((1, 0), (0, 1), (1, 1))[j]
    return 2 * (sx ^ fx) + (sy ^ fy)


def _for_my_chip(sme, fn):
    for s in range(N_CHIPS):
        pl.when(sme == s)(functools.partial(fn, s))


ANY = pl.BlockSpec(memory_space=pl.ANY)

GATHER_KIND = {"a_w_in": "col", "b_w_qkv": "col", "a_w_out": "row", "b_w_out": "row",
               "ffn_w_gate": "row", "ffn_w_up": "row", "ffn_w_down": "row"}
BIG = tuple(GATHER_KIND)


def _gathered_shape(kind, shape):
    L, R, C = shape
    return (L, R, N_CHIPS * C) if kind == "col" else (L, N_CHIPS, R, C)


def _shard_rows(ref, kind, s, r0, rn, C):
    if kind == "col":
        return ref.at[:, pl.ds(r0, rn), s * C:(s + 1) * C]
    return ref.at[:, s, pl.ds(r0, rn), :]


def gather_weights(shards):
    names = list(shards)
    n = len(names)
    kinds = [GATHER_KIND[k] for k in names]
    shapes = [shards[k].shape for k in names]

    def body(*refs):
        ins, outs = refs[:n], refs[n:2 * n]
        lsem, ssem, rsem, fssem, frsem = refs[2 * n:]
        x, y, c = _coords()
        sme = 2 * x + y
        chips = _other_chips(x, y)

        def run(s):
            def half(t):
                R = shapes[t][1]
                return pl.multiple_of(c * (R // 2), 8), R // 2

            def local(t):
                L, R, C = shapes[t]
                return pltpu.make_async_copy(ins[t], _shard_rows(outs[t], kinds[t], s, 0, R, C), lsem.at[t])

            def send(t, j):
                C = shapes[t][2]
                r0, rn = half(t)
                return pltpu.make_async_remote_copy(
                    src_ref=ins[t].at[:, pl.ds(r0, rn), :], dst_ref=_shard_rows(outs[t], kinds[t], s, r0, rn, C),
                    send_sem=ssem.at[3 * t + j], recv_sem=rsem.at[3 * t + j],
                    device_id=(chips[j][0], chips[j][1], c), device_id_type=MESH)

            def landed(t, j):
                C = shapes[t][2]
                r0, rn = half(t)
                dst = _shard_rows(outs[t], kinds[t], _flip_index(s, j), r0, rn, C)
                return pltpu.make_async_remote_copy(
                    src_ref=dst, dst_ref=dst, send_sem=ssem.at[3 * t + j], recv_sem=rsem.at[3 * t + j],
                    device_id=(chips[j][0], chips[j][1], c), device_id_type=MESH)

            def forward(t, j, mine=True):
                C = shapes[t][2]
                R = shapes[t][1]
                cc = c if mine else 1 - c
                r0 = pl.multiple_of(cc * (R // 2), 8)
                blk = _shard_rows(outs[t], kinds[t], _flip_index(s, j), r0, R // 2, C)
                return pltpu.make_async_remote_copy(
                    src_ref=blk, dst_ref=blk, send_sem=fssem.at[3 * t + j], recv_sem=frsem.at[3 * t + j],
                    device_id=(x, y, 1 - c), device_id_type=MESH)

            for t in range(n):
                local(t).start()
                for j in range(3):
                    send(t, j).start()
            for t in range(n):
                for j in range(3):
                    landed(t, j).wait_recv()
                    forward(t, j).start()
            for t in range(n):
                for j in range(3):
                    forward(t, j, mine=False).wait_recv()
            for t in range(n):
                for j in range(3):
                    send(t, j).wait_send()
                    forward(t, j).wait_send()
                local(t).wait()

        _for_my_chip(sme, run)

    out_shape = [_sds(_gathered_shape(kinds[t], shapes[t]), BF16) for t in range(n)]
    outs = _pallas(body, name="gather_weights", in_specs=[ANY] * n, out_specs=[ANY] * n, out_shape=out_shape,
                   scratch_shapes=[pltpu.SemaphoreType.DMA((n,))] + [pltpu.SemaphoreType.DMA((3 * n,))] * 4,
                   compiler_params=pltpu.CompilerParams(has_side_effects=True))(*[shards[k] for k in names])
    return dict(zip(names, outs))


def _half_shape(kind, R, C):
    return (R // 2, N_CHIPS * C) if kind == "col" else (N_CHIPS, R // 2, C)


def exchange_halves(grads, metas):
    n = len(grads)

    def body(*refs):
        ins, outs = refs[:n], refs[n:2 * n]
        ssem, rsem = refs[2 * n:]
        x, y, c = _coords()
        copies = []
        for t, (kind, R, C) in enumerate(metas):
            r0 = pl.multiple_of((1 - c) * (R // 2), 8)
            src = ins[t].at[pl.ds(r0, R // 2), :] if kind == "col" else ins[t].at[:, pl.ds(r0, R // 2), :]
            copies.append(pltpu.make_async_remote_copy(
                src_ref=src, dst_ref=outs[t], send_sem=ssem.at[t], recv_sem=rsem.at[t],
                device_id=(x, y, 1 - c), device_id_type=MESH))
        for cp in copies:
            cp.start()
        for cp in copies:
            cp.wait()

    out_shape = [_sds(_half_shape(*m), F32) for m in metas]
    return _pallas(body, name="exchange_halves", in_specs=[ANY] * n, out_specs=[ANY] * n, out_shape=out_shape,
                   scratch_shapes=[pltpu.SemaphoreType.DMA((n,))] * 2,
                   compiler_params=pltpu.CompilerParams(has_side_effects=True))(*grads)


def pair_sum(me, g, sib, meta, name):
    kind, R, C = meta
    h = R // 2

    def body(me_ref, g_ref, sib_ref, p16_ref, own_ref):
        s = pl.program_id(0)
        v = g_ref[...] + sib_ref[...]
        p16_ref[...] = v.astype(BF16)

        @pl.when(s == me_ref[1])
        def _():
            own_ref[...] = v

    if kind == "col":
        gspec = pl.BlockSpec((h, C), lambda s, me_ref: (me_ref[0], s))
        sspec = pl.BlockSpec((h, C), lambda s, me_ref: (0, s))
    else:
        gspec = pl.BlockSpec((None, h, C), lambda s, me_ref: (s, me_ref[0], 0))
        sspec = pl.BlockSpec((None, h, C), lambda s, me_ref: (s, 0, 0))
    grid_spec = pltpu.PrefetchScalarGridSpec(
        num_scalar_prefetch=1, grid=(N_CHIPS,), in_specs=[gspec, sspec],
        out_specs=[sspec, pl.BlockSpec((h, C), lambda s, me_ref: (0, 0))])
    return _pallas(body, name=name, grid_spec=grid_spec,
                   out_shape=[_sds(_half_shape(*meta), BF16), _sds((h, C), F32)],
                   compiler_params=_params(("arbitrary",), VMEM_BIG))(me, g, sib)


def scatter_partials(p16s, metas):
    n = len(p16s)

    def body(*refs):
        ins, outs = refs[:n], refs[n:2 * n]
        ssem, rsem = refs[2 * n:]
        x, y, c = _coords()
        sme = 2 * x + y
        chips = _other_chips(x, y)

        def run(s):
            copies = []
            for t, (kind, R, C) in enumerate(metas):
                for j in range(3):
                    sj = _flip_index(s, j)
                    src = ins[t].at[:, sj * C:(sj + 1) * C] if kind == "col" else ins[t].at[sj]
                    copies.append(pltpu.make_async_remote_copy(
                        src_ref=src, dst_ref=outs[t].at[j], send_sem=ssem.at[3 * t + j], recv_sem=rsem.at[3 * t + j],
                        device_id=(chips[j][0], chips[j][1], c), device_id_type=MESH))
            for cp in copies:
                cp.start()
            for cp in copies:
                cp.wait()

        _for_my_chip(sme, run)

    out_shape = [_sds((3, R // 2, C), BF16) for (_, R, C) in metas]
    return _pallas(body, name="scatter_partials", in_specs=[ANY] * n, out_specs=[ANY] * n, out_shape=out_shape,
                   scratch_shapes=[pltpu.SemaphoreType.DMA((3 * n,))] * 2,
                   compiler_params=pltpu.CompilerParams(has_side_effects=True))(*p16s)


def final_sum(me, own, q, buf, l, meta, name):
    _, R, C = meta
    h = R // 2

    def body(me_ref, own_ref, q_ref, buf_ref, o_ref):
        del buf_ref
        o_ref[...] = ((own_ref[...] + q_ref[0].astype(F32)) + q_ref[1].astype(F32)) + q_ref[2].astype(F32)

    grid_spec = pltpu.PrefetchScalarGridSpec(
        num_scalar_prefetch=1, grid=(1,),
        in_specs=[pl.BlockSpec((h, C), lambda i, me_ref: (0, 0)),
                  pl.BlockSpec((3, h, C), lambda i, me_ref: (0, 0, 0)), ANY],
        out_specs=pl.BlockSpec((None, h, C), lambda i, me_ref: (l, me_ref[0], 0)))
    return _pallas(body, name=name, grid_spec=grid_spec, out_shape=_sds(buf.shape, F32),
                   input_output_aliases={3: 0},
                   compiler_params=_params(("arbitrary",), VMEM_BIG))(me, own, q, buf)


def share_final(bufs):
    n = len(bufs)

    def body(*refs):
        ins, outs = refs[:n], refs[n:2 * n]
        ssem, rsem = refs[2 * n:]
        del ins
        x, y, c = _coords()
        copies = []
        for t in range(n):
            R = bufs[t].shape[1]
            r0 = pl.multiple_of(c * (R // 2), 8)
            blk = outs[t].at[:, pl.ds(r0, R // 2), :]
            copies.append(pltpu.make_async_remote_copy(
                src_ref=blk, dst_ref=blk, send_sem=ssem.at[t], recv_sem=rsem.at[t],
                device_id=(x, y, 1 - c), device_id_type=MESH))
        for cp in copies:
            cp.start()
        for t in range(n):
            R = bufs[t].shape[1]
            r1 = pl.multiple_of((1 - c) * (R // 2), 8)
            other = outs[t].at[:, pl.ds(r1, R // 2), :]
            pltpu.make_async_remote_copy(
                src_ref=other, dst_ref=other, send_sem=ssem.at[t], recv_sem=rsem.at[t],
                device_id=(x, y, 1 - c), device_id_type=MESH).wait_recv()
        for cp in copies:
            cp.wait_send()

    out_shape = [_sds(b.shape, F32) for b in bufs]
    return _pallas(body, name="share_final", in_specs=[ANY] * n, out_specs=[ANY] * n, out_shape=out_shape,
                   input_output_aliases={t: t for t in range(n)},
                   scratch_shapes=[pltpu.SemaphoreType.DMA((n,))] * 2,
                   compiler_params=pltpu.CompilerParams(has_side_effects=True))(*bufs)


def allreduce_small(part):
    rows = part.shape[0]

    def body(p_ref, o_ref, buf, ssem, rsem, lsem):
        x, y, c = _coords()
        me = 4 * x + 2 * y + c
        mine = pltpu.make_async_copy(p_ref, buf.at[me], lsem)
        mine.start()
        copies = []
        for r in range(1, 8):
            fx, fy, fc = (r >> 2) & 1, (r >> 1) & 1, r & 1
            peer = ((1 - x) if fx else x, (1 - y) if fy else y, (1 - c) if fc else c)
            copies.append(pltpu.make_async_remote_copy(
                src_ref=p_ref, dst_ref=buf.at[me], send_sem=ssem.at[r - 1], recv_sem=rsem.at[r - 1],
                device_id=peer, device_id_type=MESH))
        for cp in copies:
            cp.start()
        for r in range(1, 8):
            fx, fy, fc = (r >> 2) & 1, (r >> 1) & 1, r & 1
            src = 4 * ((1 - x) if fx else x) + 2 * ((1 - y) if fy else y) + ((1 - c) if fc else c)
            pltpu.make_async_remote_copy(
                src_ref=buf.at[src], dst_ref=buf.at[src], send_sem=ssem.at[r - 1], recv_sem=rsem.at[r - 1],
                device_id=(x, y, c), device_id_type=MESH).wait_recv()
        for cp in copies:
            cp.wait_send()
        mine.wait()
        acc = buf[0]
        for d in range(1, 8):
            acc = acc + buf[d]
        o_ref[...] = acc

    return _pallas(body, name="allreduce_small",
                   in_specs=[pl.BlockSpec(memory_space=pltpu.VMEM)], out_specs=pl.BlockSpec(memory_space=pltpu.VMEM),
                   out_shape=_sds((rows, 128), F32),
                   scratch_shapes=[pltpu.VMEM((8, rows, 128), F32), pltpu.SemaphoreType.DMA((7,)),
                                   pltpu.SemaphoreType.DMA((7,)), pltpu.SemaphoreType.DMA],
                   compiler_params=pltpu.CompilerParams(has_side_effects=True, vmem_limit_bytes=VMEM_BIG))(part)


def _rows128(a):
    flat = a.reshape(-1)
    rows = -(-flat.shape[0] // 128)
    rows8 = -(-rows // 8) * 8
    flat = jnp.pad(flat, (0, rows8 * 128 - flat.shape[0]))
    return flat.reshape(rows8, 128)


def kernel(x, norm_mix_g, norm_ffn_g, final_g, a_w_in, a_v_gain, a_w_s, a_b_s, a_w_out, b_w_qkv, b_rel_bias, b_w_out, ffn_w_gate, ffn_w_up, ffn_w_down, loss_target, m_norm_mix_g, m_norm_ffn_g, m_final_g, m_a_w_in, m_a_v_gain, m_a_w_s, m_a_b_s, m_a_w_out, m_b_w_qkv, m_b_rel_bias, m_b_w_out, m_ffn_w_gate, m_ffn_w_up, m_ffn_w_down, v_norm_mix_g, v_norm_ffn_g, v_final_g, v_a_w_in, v_a_v_gain, v_a_w_s, v_a_b_s, v_a_w_out, v_b_w_qkv, v_b_rel_bias, v_b_w_out, v_ffn_w_gate, v_ffn_w_up, v_ffn_w_down):
    T = x.shape[1]
    weights = dict(norm_mix_g=norm_mix_g, norm_ffn_g=norm_ffn_g, final_g=final_g, a_w_in=a_w_in, a_v_gain=a_v_gain,
                   a_w_s=a_w_s, a_b_s=a_b_s, a_w_out=a_w_out, b_w_qkv=b_w_qkv, b_rel_bias=b_rel_bias,
                   b_w_out=b_w_out, ffn_w_gate=ffn_w_gate, ffn_w_up=ffn_w_up, ffn_w_down=ffn_w_down)
    mom_m = dict(norm_mix_g=m_norm_mix_g, norm_ffn_g=m_norm_ffn_g, final_g=m_final_g, a_w_in=m_a_w_in,
                 a_v_gain=m_a_v_gain, a_w_s=m_a_w_s, a_b_s=m_a_b_s, a_w_out=m_a_w_out, b_w_qkv=m_b_w_qkv,
                 b_rel_bias=m_b_rel_bias, b_w_out=m_b_w_out, ffn_w_gate=m_ffn_w_gate, ffn_w_up=m_ffn_w_up,
                 ffn_w_down=m_ffn_w_down)
    mom_v = dict(norm_mix_g=v_norm_mix_g, norm_ffn_g=v_norm_ffn_g, final_g=v_final_g, a_w_in=v_a_w_in,
                 a_v_gain=v_a_v_gain, a_w_s=v_a_w_s, a_b_s=v_a_b_s, a_w_out=v_a_w_out, b_w_qkv=v_b_w_qkv,
                 b_rel_bias=v_b_rel_bias, b_w_out=v_b_w_out, ffn_w_gate=v_ffn_w_gate, ffn_w_up=v_ffn_w_up,
                 ffn_w_down=v_ffn_w_down)
    order = list(weights)
    transposed = ("ffn_w_gate", "ffn_w_up")
    for k in transposed:
        weights[k], mom_m[k], mom_v[k] = (jnp.swapaxes(a, 1, 2) for a in (weights[k], mom_m[k], mom_v[k]))

    xi, yi, ci = _coords()
    me = jnp.stack([ci, 2 * xi + yi]).astype(jnp.int32)

    full = gather_weights({k: cast_bf16(weights[k], "cast_" + k) for k in BIG})
    w_in, w_qkv = full["a_w_in"], full["b_w_qkv"]
    w_aout = full["a_w_out"].reshape(2, GH, D)
    w_bout = full["b_w_out"].reshape(2, D, D)
    w_gate, w_up, w_down = full["ffn_w_gate"], full["ffn_w_up"], full["ffn_w_down"]

    row512 = lambda: pl.BlockSpec((512, D), lambda i, j: (i, 0))

    xcur = x.reshape(T, D)
    saved = []
    for i in range(DEPTH):
        j = i // 2
        tag = "_l%d" % i
        st = {"x_in": xcur}
        hn = rms_fwd(xcur, norm_mix_g[i][None], "rms_mix" + tag)
        st["hn"] = hn
        if i % 2 == 0:
            pre = matmul("a_in" + tag, NN, hn, pl.BlockSpec((1024, D), lambda i_, j_: (i_, 0)),
                         w_in, pl.BlockSpec((None, D, 1024), lambda i_, j_, l=j: (l, 0, j_)),
                         _sds((T, 2 * GH), BF16), pl.BlockSpec((1024, 1024), lambda i_, j_: (i_, j_)),
                         (T // 1024, 4))
            y = sgu_fwd(pre, a_v_gain[j][None], a_w_s[j], a_b_s[j][:, :, None], "sgu_fwd" + tag)
            xmid = matmul("a_out" + tag, NN, y, pl.BlockSpec((512, GH), lambda i_, j_: (i_, 0)),
                          w_aout, pl.BlockSpec((None, GH, D), lambda i_, j_, l=j: (l, 0, 0)),
                          _sds((T, D), F32), row512(), (T // 512, 1), res=xcur, res_spec=row512())
            st.update(pre=pre, y=y)
        else:
            qkvp = proj_qkv(hn, w_qkv, j, "b_qkv" + tag)
            wb = jnp.transpose(bias_build(b_rel_bias[j], "bias_build" + tag), (1, 0, 2))
            o = attn_fwd(qkvp, wb, "attn_fwd" + tag)
            xmid = matmul("b_out" + tag, NN, o, pl.BlockSpec((512, D), lambda i_, j_: (i_, 0)),
                          w_bout, pl.BlockSpec((None, D, D), lambda i_, j_, l=j: (l, 0, 0)),
                          _sds((T, D), F32), row512(), (T // 512, 1), res=xcur, res_spec=row512())
            st.update(qkvp=qkvp, wb=wb, o=o)
        hn2 = rms_fwd(xmid, norm_ffn_g[i][None], "rms_ffn" + tag)
        g, u, h = ffn_up(hn2, w_gate, w_up, i, "ffn_up" + tag)
        xcur = ffn_down_res(h, w_down, xmid, i, "ffn_down" + tag)
        st.update(x_mid=xmid, hn2=hn2, g=g, u=u, h=h)
        saved.append(st)

    loss_part, dx, dxb, d_final = final_loss(xcur, final_g[None], loss_target.reshape(T, D), "final_loss")

    tk = min(2048, T)
    big_grads = {k: [None] * weights[k].shape[0] for k in BIG}
    small = {"norm_mix_g": [None] * DEPTH, "norm_ffn_g": [None] * DEPTH, "a_v_gain": [None] * 2,
             "a_w_s": [None] * 2, "a_b_s": [None] * 2, "b_rel_bias": [None] * 2}
    tok = lambda width: pl.BlockSpec((tk, width), lambda j_, k_: (k_, 0))
    for i in reversed(range(DEPTH)):
        j = i // 2
        tag = "_l%d" % i
        st = saved[i]
        dg, du = ffn_bwd_dh(dxb, w_down, st["g"], st["u"], i, "ffn_bwd_dh" + tag)
        part = lambda: pl.BlockSpec((None, tk, FS), lambda j_, k_: (j_, k_, 0))
        big_grads["ffn_w_down"][i] = wgrad(
            "dw_down" + tag, st["h"], part(), dxb, tok(D), _sds((N_CHIPS, FS, D), F32),
            pl.BlockSpec((None, FS, D), lambda j_, k_: (j_, 0, 0)), N_CHIPS, T, tk)
        dhn2 = ffn_bwd_dhn(dg, du, w_gate, w_up, i, "ffn_bwd_dhn" + tag)
        for nm, dz in (("ffn_w_gate", dg), ("ffn_w_up", du)):
            big_grads[nm][i] = wgrad(
                "d" + nm + tag, dz, part(), st["hn2"], tok(D), _sds((N_CHIPS, FS, D), F32),
                pl.BlockSpec((None, FS, D), lambda j_, k_: (j_, 0, 0)), N_CHIPS, T, tk)
        dx, dxb, dgn = rms_bwd(st["x_mid"], dhn2, norm_ffn_g[i][None], dx, "rms_ffn_bwd" + tag)
        small["norm_ffn_g"][i] = dgn
        if i % 2 == 0:
            dy = matmul("a_out_bwd" + tag, NT, dxb, pl.BlockSpec((1024, D), lambda i_, j_: (i_, 0)),
                        w_aout, pl.BlockSpec((None, 1024, D), lambda i_, j_, l=j: (l, j_, 0)),
                        _sds((T, GH), BF16), pl.BlockSpec((1024, 1024), lambda i_, j_: (i_, j_)), (T // 1024, 2))
            big_grads["a_w_out"][j] = wgrad(
                "dw_aout" + tag, st["y"], pl.BlockSpec((tk, 1024), lambda j_, k_: (k_, j_)), dxb, tok(D),
                _sds((GH, D), F32), pl.BlockSpec((1024, D), lambda j_, k_: (j_, 0)), 2, T, tk
            ).reshape(N_CHIPS, GH // N_CHIPS, D)
            dpre, d_ws, d_bs, d_gain = sgu_bwd(st["pre"], dy, a_v_gain[j][None], a_w_s[j], a_b_s[j][:, :, None],
                                               "sgu_bwd" + tag)
            small["a_w_s"][j], small["a_b_s"][j], small["a_v_gain"][j] = d_ws, d_bs, d_gain
            dhn = matmul("a_in_bwd" + tag, NT, dpre, pl.BlockSpec((512, 2 * GH), lambda i_, j_: (i_, 0)),
                         w_in, pl.BlockSpec((None, D, 2 * GH), lambda i_, j_, l=j: (l, 0, 0)),
                         _sds((T, D), F32), row512(), (T // 512, 1))
            big_grads["a_w_in"][j] = wgrad(
                "dw_in" + tag, st["hn"], tok(D), dpre, pl.BlockSpec((tk, 1024), lambda j_, k_: (k_, j_)),
                _sds((D, 2 * GH), F32), pl.BlockSpec((D, 1024), lambda j_, k_: (0, j_)), 4, T, tk)
        else:
            do = matmul("b_out_bwd" + tag, NT, dxb, pl.BlockSpec((1024, D), lambda i_, j_: (i_, 0)),
                        w_bout, pl.BlockSpec((None, D, D), lambda i_, j_, l=j: (l, 0, 0)),
                        _sds((T, D), BF16), pl.BlockSpec((1024, D), lambda i_, j_: (i_, 0)), (T // 1024, 1))
            big_grads["b_w_out"][j] = wgrad(
                "dw_bout" + tag, st["o"], tok(D), dxb, tok(D),
                _sds((D, D), F32), pl.BlockSpec((D, D), lambda j_, k_: (0, 0)), 1, T, tk
            ).reshape(N_CHIPS, D // N_CHIPS, D)
            dqkvp, dwb = attn_bwd(st["qkvp"], st["o"], do, st["wb"], "attn_bwd" + tag)
            small["b_rel_bias"][j] = bias_grad(jnp.transpose(dwb, (1, 0, 2)), "bias_grad" + tag)
            dhn = qkv_bwd(dqkvp, w_qkv, j, "b_qkv_bwd" + tag)
            big_grads["b_w_qkv"][j] = wgrad(
                "dw_qkv" + tag, st["hn"], tok(D), dqkvp,
                pl.BlockSpec((None, tk, D), lambda j_, k_: (j_, k_ + FRONT // tk, 0)),
                _sds((D, 3 * D), F32), pl.BlockSpec((D, D), lambda j_, k_: (0, j_)), 3, T, tk)
        dx, dxb, dgn = rms_bwd(st["x_in"], dhn, norm_mix_g[i][None], dx, "rms_mix_bwd" + tag)
        small["norm_mix_g"][i] = dgn

    small_grads = {
        "norm_mix_g": jnp.concatenate(small["norm_mix_g"], axis=0),
        "norm_ffn_g": jnp.concatenate(small["norm_ffn_g"], axis=0),
        "final_g": d_final.reshape(D),
        "a_v_gain": jnp.concatenate(small["a_v_gain"], axis=0),
        "a_w_s": jnp.stack(small["a_w_s"]),
        "a_b_s": jnp.stack(small["a_b_s"]).reshape(2, SGU_G, SGU_BLOCK),
        "b_rel_bias": jnp.stack(small["b_rel_bias"]),
    }
    small_names = list(small_grads)
    packed = [_rows128(small_grads[k]) for k in small_names] + [_rows128(loss_part[:, :1])]
    offs = [0]
    for p in packed:
        offs.append(offs[-1] + p.shape[0])
    reduced = allreduce_small(jnp.concatenate(packed, axis=0))
    grads = {}
    for t, k in enumerate(small_names):
        nelem = small_grads[k].size
        grads[k] = reduced[offs[t]:offs[t + 1]].reshape(-1)[:nelem].reshape(weights[k].shape)
    loss = reduced[offs[len(small_names)], 0]

    metas, flat, where = [], [], []
    for k in BIG:
        L, R, C = weights[k].shape
        for l in range(L):
            metas.append((GATHER_KIND[k], R, C))
            flat.append(big_grads[k][l])
            where.append((k, l))
    sib = exchange_halves(flat, metas)
    p16, own = [], []
    for t, (k, l) in enumerate(where):
        a, b = pair_sum(me, flat[t], sib[t], metas[t], "pair_sum_%s_l%d" % (k, l))
        p16.append(a)
        own.append(b)
    q = scatter_partials(p16, metas)
    bufs = {k: jnp.zeros(weights[k].shape, F32) for k in BIG}
    for t, (k, l) in enumerate(where):
        bufs[k] = final_sum(me, own[t], q[t], bufs[k], l, metas[t], "final_sum_%s_l%d" % (k, l))
    shared = share_final([bufs[k] for k in BIG])
    for k, gfull in zip(BIG, shared):
        grads[k] = gfull

    delta, new_m, new_v = {}, {}, {}
    for k in order:
        shp = weights[k].shape
        if k in BIG:
            view = shp
        elif k == "a_w_s":
            view = (2, SGU_G * SGU_BLOCK, SGU_BLOCK)
        elif len(shp) == 1:
            view = (1, 1, shp[0])
        elif len(shp) == 2:
            view = (1,) + shp
        else:
            view = shp
        d_, m_, v_ = adamw(weights[k].reshape(view), grads[k].reshape(view), mom_m[k].reshape(view),
                           mom_v[k].reshape(view), "adamw_" + k)
        delta[k], new_m[k], new_v[k] = d_.reshape(shp), m_.reshape(shp), v_.reshape(shp)
    for k in transposed:
        for tree in (grads, delta, new_m, new_v):
            tree[k] = jnp.swapaxes(tree[k], 1, 2)

    return (loss, dx.reshape(1, T, D), *[grads[k] for k in order], *[delta[k] for k in order],
            *[new_m[k] for k in order], *[new_v[k] for k in order])
```

```python
import functools

import jax
import jax.numpy as jnp
from jax import lax
from jax.experimental import pallas as pl
from jax.experimental.pallas import tpu as pltpu

F32 = jnp.float32
BF16 = jnp.bfloat16
MESH = pl.DeviceIdType.MESH

D = 1024
DEPTH = 4
EPS = 1e-6
SGU_BLOCK = 128
GH = 2048
SGU_G = 8
SGU_GD = GH // SGU_G
N_HEADS = 16
HEAD_DIM = 64
CHUNK = 64
PAD = 8 * CHUNK
FRONT = 2048
QB = 128
KW = PAD + QB
N_REL = 192
REL_MIN = -(CHUNK - 1)
REL_MAX = 128
D_FF = 2816
FS = D_FF // 4
NEG = -1e30
SCALE = HEAD_DIM ** -0.5
N_CHIPS = 4

ADAM_LR = 0.001
ADAM_B1 = 0.9
ADAM_B2 = 0.999
ADAM_EPS = 1e-08
ADAM_WD = 0.01
ADAM_STEP = 10

VMEM_BIG = 56 * 1024 * 1024

NN = ((1,), (0,))
NT = ((1,), (1,))
TN = ((0,), (0,))


def _dot(a, b, dims):
    return lax.dot_general(a, b, (dims, ((), ())), preferred_element_type=F32)


class Comm:
    def __init__(self, ins, out_shapes, sems, start, wait, aliases=None):
        self.ins, self.out_shapes, self.sems = list(ins), list(out_shapes), list(sems)
        self.start, self.wait, self.aliases = start, wait, dict(aliases or {})


def _host(body, comm, kw):
    grid = tuple(kw["grid"])
    in_specs = list(kw["in_specs"])
    single = not isinstance(kw["out_specs"], (list, tuple))
    out_specs = [kw["out_specs"]] if single else list(kw["out_specs"])
    out_shape = [kw["out_shape"]] if single else list(kw["out_shape"])
    scratch = list(kw.get("scratch_shapes", ()))
    counts = (len(in_specs), len(comm.ins), len(out_specs), len(comm.out_shapes), len(scratch))

    def hosted(*refs):
        parts, p = [], 0
        for cnt in counts:
            parts.append(refs[p:p + cnt])
            p += cnt
        main_in, c_in, main_out, c_out, main_scr = parts
        sems = refs[p:]
        ids = [pl.program_id(a) for a in range(len(grid))]
        first = functools.reduce(jnp.logical_and, [i == 0 for i in ids])
        last = functools.reduce(jnp.logical_and, [i == n - 1 for i, n in zip(ids, grid)])
        pl.when(first)(lambda: comm.start(c_in, c_out, sems))
        body(*main_in, *main_out, *main_scr)
        pl.when(last)(lambda: comm.wait(c_in, c_out, sems))

    old = kw["compiler_params"]
    kw = dict(kw, in_specs=in_specs + [ANY] * len(comm.ins), out_specs=out_specs + [ANY] * len(comm.out_shapes),
              out_shape=out_shape + comm.out_shapes, scratch_shapes=scratch + comm.sems,
              compiler_params=pltpu.CompilerParams(dimension_semantics=("arbitrary",) * len(grid),
                                                   vmem_limit_bytes=old.vmem_limit_bytes, has_side_effects=True))
    if comm.aliases:
        kw["input_output_aliases"] = {counts[0] + i: counts[2] + o for i, o in comm.aliases.items()}
    return hosted, kw


def _pallas(body, comm=None, **kw):
    if comm is not None:
        body, kw = _host(body, comm, kw)
    return pl.pallas_call(body, **kw)


def _split_outs(outs, comm, n_main):
    outs = list(outs) if isinstance(outs, (list, tuple)) else [outs]
    main = outs[:n_main]
    return (main[0] if n_main == 1 else main), outs[n_main:]


def run_comm(comm, name):
    nci, nco = len(comm.ins), len(comm.out_shapes)

    def body(*refs):
        c_in, c_out, sems = refs[:nci], refs[nci:nci + nco], refs[nci + nco:]
        comm.start(c_in, c_out, sems)
        comm.wait(c_in, c_out, sems)

    kw = {}
    if comm.aliases:
        kw["input_output_aliases"] = dict(comm.aliases)
    return _pallas(body, name=name, in_specs=[ANY] * nci, out_specs=[ANY] * nco, out_shape=comm.out_shapes,
                   scratch_shapes=comm.sems, compiler_params=pltpu.CompilerParams(has_side_effects=True),
                   **kw)(*comm.ins)


def _call(body, comm, n_main, args, **kw):
    if comm is None:
        return _pallas(body, **kw)(*args)
    return _split_outs(_pallas(body, comm=comm, **kw)(*args, *comm.ins), comm, n_main)


def _params(sem=None, vmem=None):
    return pltpu.CompilerParams(dimension_semantics=sem, vmem_limit_bytes=vmem)


def _sds(shape, dtype):
    return jax.ShapeDtypeStruct(tuple(shape), dtype)


_GELU_C = 0.7978845608028654


def _gelu(x):
    t = jnp.tanh(_GELU_C * (x + 0.044715 * (x * x * x)))
    return 0.5 * x * (1.0 + t)


def _gelu_and_grad(x):
    x2 = x * x
    t = jnp.tanh(_GELU_C * (x + 0.044715 * (x2 * x)))
    val = 0.5 * x * (1.0 + t)
    grad = 0.5 * (1.0 + t) + 0.5 * x * (1.0 - t * t) * (_GELU_C * (1.0 + 3.0 * 0.044715 * x2))
    return val, grad


def _sigmoid(x):
    return 0.5 * (jnp.tanh(0.5 * x) + 1.0)


def cast_bf16(w, name):
    L, R, C = w.shape

    def body(w_ref, o_ref):
        o_ref[...] = w_ref[...].astype(BF16)

    spec = pl.BlockSpec((None, R, C), lambda l: (l, 0, 0))
    return _pallas(body, name=name, grid=(L,), in_specs=[spec], out_specs=spec,
                   out_shape=_sds((L, R, C), BF16), compiler_params=_params(("parallel",)))(w)


def rms_fwd(x, g, name, tm=512):
    T = x.shape[0]

    def body(x_ref, g_ref, o_ref):
        xf = x_ref[...]
        r = lax.rsqrt(jnp.mean(xf * xf, axis=-1, keepdims=True) + EPS)
        o_ref[...] = ((xf * r) * g_ref[...]).astype(BF16)

    row = pl.BlockSpec((tm, D), lambda i: (i, 0))
    return _pallas(body, name=name, grid=(T // tm,),
                   in_specs=[row, pl.BlockSpec((1, D), lambda i: (0, 0))], out_specs=row,
                   out_shape=_sds((T, D), BF16), compiler_params=_params(("parallel",)))(x, g)


def rms_bwd(x, dh, g, dres, name, tm=256):
    T = x.shape[0]
    n = T // tm

    def body(x_ref, dh_ref, g_ref, dres_ref, dx_ref, dxb_ref, dg_ref, acc_ref):
        i = pl.program_id(0)
        xf = x_ref[...]
        r = lax.rsqrt(jnp.mean(xf * xf, axis=-1, keepdims=True) + EPS)
        xhat = xf * r
        dhf = dh_ref[...]
        part = (dhf * xhat).reshape(tm // 8, 8, D).sum(axis=0)

        @pl.when(i == 0)
        def _():
            acc_ref[...] = part

        @pl.when(i > 0)
        def _():
            acc_ref[...] += part

        dxhat = dhf * g_ref[...]
        dx = dres_ref[...] + r * (dxhat - xhat * jnp.mean(dxhat * xhat, axis=-1, keepdims=True))
        dx_ref[...] = dx
        dxb_ref[...] = dx.astype(BF16)

        @pl.when(i == n - 1)
        def _():
            dg_ref[...] = jnp.sum(acc_ref[...], axis=0, keepdims=True)

    row = pl.BlockSpec((tm, D), lambda i: (i, 0))
    vec = pl.BlockSpec((1, D), lambda i: (0, 0))
    return _pallas(body, name=name, grid=(n,), in_specs=[row, row, vec, row], out_specs=[row, row, vec],
                   out_shape=[_sds((T, D), F32), _sds((T, D), BF16), _sds((1, D), F32)],
                   scratch_shapes=[pltpu.VMEM((8, D), F32)],
                   compiler_params=_params(("arbitrary",)))(x, dh, g, dres)


def final_loss(x, g, tgt, name, tm=256):
    T = x.shape[0]
    n = T // tm

    def body(x_ref, g_ref, t_ref, loss_ref, dx_ref, dxb_ref, dg_ref, acc_ref, lacc_ref):
        i = pl.program_id(0)
        xf = x_ref[...]
        r = lax.rsqrt(jnp.mean(xf * xf, axis=-1, keepdims=True) + EPS)
        xhat = xf * r
        gg = g_ref[...]
        e = xhat * gg - t_ref[...]
        dy = e * (1.0 / D)
        part = (dy * xhat).reshape(tm // 8, 8, D).sum(axis=0)
        lpart = (e * e).reshape(tm // 8, 8, D).sum(axis=0)

        @pl.when(i == 0)
        def _():
            acc_ref[...] = part
            lacc_ref[...] = lpart

        @pl.when(i > 0)
        def _():
            acc_ref[...] += part
            lacc_ref[...] += lpart

        dxhat = dy * gg
        dx = r * (dxhat - xhat * jnp.mean(dxhat * xhat, axis=-1, keepdims=True))
        dx_ref[...] = dx
        dxb_ref[...] = dx.astype(BF16)

        @pl.when(i == n - 1)
        def _():
            dg_ref[...] = jnp.sum(acc_ref[...], axis=0, keepdims=True)
            total = jnp.sum(jnp.sum(lacc_ref[...], axis=0, keepdims=True), axis=1, keepdims=True)
            loss_ref[...] = jnp.broadcast_to(total * (0.5 / D), (1, 128))

    row = pl.BlockSpec((tm, D), lambda i: (i, 0))
    vec = pl.BlockSpec((1, D), lambda i: (0, 0))
    return _pallas(body, name=name, grid=(n,), in_specs=[row, vec, row],
                   out_specs=[pl.BlockSpec((1, 128), lambda i: (0, 0)), row, row, vec],
                   out_shape=[_sds((1, 128), F32), _sds((T, D), F32), _sds((T, D), BF16), _sds((1, D), F32)],
                   scratch_shapes=[pltpu.VMEM((8, D), F32), pltpu.VMEM((8, D), F32)],
                   compiler_params=_params(("arbitrary",)))(x, g, tgt)


def matmul(name, dims, a, a_spec, b, b_spec, out_shape, out_spec, grid, *, acc=False, res=None, res_spec=None):
    has_res = res is not None

    def body(*refs):
        a_ref, b_ref = refs[0], refs[1]
        r_ref = refs[2] if has_res else None
        o_ref = refs[-1]
        d = _dot(a_ref[...], b_ref[...], dims)
        if not acc:
            if has_res:
                d = d + r_ref[...]
            o_ref[...] = d.astype(o_ref.dtype)
        else:
            k = pl.program_id(len(grid) - 1)

            @pl.when(k == 0)
            def _():
                o_ref[...] = (d + r_ref[...]) if has_res else d

            @pl.when(k > 0)
            def _():
                o_ref[...] += d

    sem = ("parallel",) * (len(grid) - 1) + (("arbitrary",) if acc else ("parallel",))
    ins = [a, b] + ([res] if has_res else [])
    specs = [a_spec, b_spec] + ([res_spec] if has_res else [])
    return _pallas(body, name=name, grid=grid, in_specs=specs, out_specs=out_spec, out_shape=out_shape,
                   compiler_params=_params(sem, VMEM_BIG))(*ins)


def wgrad(name, a, a_spec, b, b_spec, out_shape, out_spec, J, T, tk):
    return matmul(name, TN, a, a_spec, b, b_spec, out_shape, out_spec, (J, T // tk), acc=True)


def _sgu_mask():
    p = lax.broadcasted_iota(jnp.int32, (SGU_BLOCK, SGU_BLOCK), 0)
    q = lax.broadcasted_iota(jnp.int32, (SGU_BLOCK, SGU_BLOCK), 1)
    return lax.shift_right_logical(q, 6) <= lax.shift_right_logical(p, 6)


def sgu_fwd(pre, gain, w_s, b_s, name):
    T = pre.shape[0]

    def body(pre_ref, gain_ref, ws_ref, bs_ref, y_ref):
        mask = _sgu_mask()
        u = _gelu(pre_ref[:, :GH].astype(F32))
        va = _gelu(pre_ref[:, GH:].astype(F32))
        r = lax.rsqrt(jnp.mean(va * va, axis=-1, keepdims=True) + EPS)
        vn = ((va * r) * gain_ref[...]).astype(BF16)
        for g in range(SGU_G):
            sl = slice(g * SGU_GD, (g + 1) * SGU_GD)
            wm = jnp.where(mask, ws_ref[g], 0.0).astype(BF16)
            vm = _dot(wm, vn[:, sl], NN) + bs_ref[g]
            y_ref[:, sl] = (u[:, sl] * vm).astype(BF16)

    return _pallas(
        body, name=name, grid=(T // SGU_BLOCK,),
        in_specs=[pl.BlockSpec((SGU_BLOCK, 2 * GH), lambda i: (i, 0)),
                  pl.BlockSpec((1, GH), lambda i: (0, 0)),
                  pl.BlockSpec((SGU_G, SGU_BLOCK, SGU_BLOCK), lambda i: (0, 0, 0)),
                  pl.BlockSpec((SGU_G, SGU_BLOCK, 1), lambda i: (0, 0, 0))],
        out_specs=pl.BlockSpec((SGU_BLOCK, GH), lambda i: (i, 0)),
        out_shape=_sds((T, GH), BF16), compiler_params=_params(("parallel",)))(pre, gain, w_s, b_s)


def sgu_bwd(pre, dy, gain, w_s, b_s, name, comm=None):
    T = pre.shape[0]
    n = T // SGU_BLOCK

    def body(pre_ref, dy_ref, gain_ref, ws_ref, bs_ref, dpre_ref, dws_ref, dbs_ref, dgain_ref, gacc_ref):
        i = pl.program_id(0)

        @pl.when(i == 0)
        def _():
            dws_ref[...] = jnp.zeros_like(dws_ref)
            dbs_ref[...] = jnp.zeros_like(dbs_ref)
            gacc_ref[...] = jnp.zeros_like(gacc_ref)

        mask = _sgu_mask()
        u, du_dpre = _gelu_and_grad(pre_ref[:, :GH].astype(F32))
        va, dva_dpre = _gelu_and_grad(pre_ref[:, GH:].astype(F32))
        r = lax.rsqrt(jnp.mean(va * va, axis=-1, keepdims=True) + EPS)
        vhat = va * r
        gain_v = gain_ref[...]
        vn = (vhat * gain_v).astype(BF16)
        dyf = dy_ref[...].astype(F32)
        dvn_parts = []
        for g in range(SGU_G):
            sl = slice(g * SGU_GD, (g + 1) * SGU_GD)
            wm = jnp.where(mask, ws_ref[g], 0.0).astype(BF16)
            vm = _dot(wm, vn[:, sl], NN) + bs_ref[g]
            dpre_ref[:, sl] = ((dyf[:, sl] * vm) * du_dpre[:, sl]).astype(BF16)
            dvm = dyf[:, sl] * u[:, sl]
            dbs_ref[g] += jnp.sum(dvm, axis=-1, keepdims=True)
            dvm16 = dvm.astype(BF16)
            dws_ref[g] += jnp.where(mask, _dot(dvm16, vn[:, sl], NT), 0.0)
            dvn_parts.append(_dot(wm, dvm16, TN))
        dvn = jnp.concatenate(dvn_parts, axis=-1)
        gacc_ref[...] += (dvn * vhat).reshape(SGU_BLOCK // 8, 8, GH).sum(axis=0)
        dvhat = dvn * gain_v
        dva = r * (dvhat - vhat * jnp.mean(dvhat * vhat, axis=-1, keepdims=True))
        dpre_ref[:, GH:] = (dva * dva_dpre).astype(BF16)

        @pl.when(i == n - 1)
        def _():
            dgain_ref[...] = jnp.sum(gacc_ref[...], axis=0, keepdims=True)

    const3 = lambda i: (0, 0, 0)
    return _call(
        body, comm, 4, (pre, dy, gain, w_s, b_s), name=name, grid=(n,),
        in_specs=[pl.BlockSpec((SGU_BLOCK, 2 * GH), lambda i: (i, 0)),
                  pl.BlockSpec((SGU_BLOCK, GH), lambda i: (i, 0)),
                  pl.BlockSpec((1, GH), lambda i: (0, 0)),
                  pl.BlockSpec((SGU_G, SGU_BLOCK, SGU_BLOCK), const3),
                  pl.BlockSpec((SGU_G, SGU_BLOCK, 1), const3)],
        out_specs=[pl.BlockSpec((SGU_BLOCK, 2 * GH), lambda i: (i, 0)),
                   pl.BlockSpec((SGU_G, SGU_BLOCK, SGU_BLOCK), const3),
                   pl.BlockSpec((SGU_G, SGU_BLOCK, 1), const3),
                   pl.BlockSpec((1, GH), lambda i: (0, 0))],
        out_shape=[_sds((T, 2 * GH), BF16), _sds((SGU_G, SGU_BLOCK, SGU_BLOCK), F32),
                   _sds((SGU_G, SGU_BLOCK, 1), F32), _sds((1, GH), F32)],
        scratch_shapes=[pltpu.VMEM((8, GH), F32)],
        compiler_params=_params(("arbitrary",)))


def _rel_onehot(i):
    j = lax.broadcasted_iota(jnp.int32, (N_REL, KW), 1)
    r = lax.broadcasted_iota(jnp.int32, (N_REL, KW), 0)
    idx = jnp.clip(i - j + PAD, REL_MIN, REL_MAX) - REL_MIN
    return (idx == r).astype(BF16)


def _split3(v):
    hi = v.astype(BF16)
    r1 = v - hi.astype(F32)
    mid = r1.astype(BF16)
    lo = (r1 - mid.astype(F32)).astype(BF16)
    return hi, mid, lo


def bias_build(rel_bias, name):
    def body(rb_ref, o_ref):
        parts = _split3(rb_ref[...])

        def row(i, carry):
            oh = _rel_onehot(i)
            val = _dot(parts[0], oh, NN) + _dot(parts[1], oh, NN) + _dot(parts[2], oh, NN)
            j = lax.broadcasted_iota(jnp.int32, (1, KW), 1)
            rel = lax.shift_right_logical(i, 6) - lax.shift_right_logical(j, 6) + 8
            ok = (rel >= 0) & (rel <= 8)
            o_ref[i] = jnp.where(ok, val, NEG)
            return carry

        lax.fori_loop(0, QB, row, 0)

    return _pallas(body, name=name, out_shape=_sds((QB, N_HEADS, KW), F32),
                   in_specs=[pl.BlockSpec(memory_space=pltpu.VMEM)],
                   out_specs=pl.BlockSpec(memory_space=pltpu.VMEM))(rel_bias)


def bias_grad(dwb, name):
    def body(d_ref, o_ref):
        def row(i, acc):
            oh = _rel_onehot(i)
            hi, mid, lo = _split3(d_ref[i])
            return acc + (_dot(hi, oh, NT) + _dot(mid, oh, NT) + _dot(lo, oh, NT))

        o_ref[...] = lax.fori_loop(0, QB, row, jnp.zeros((N_HEADS, N_REL), F32))

    return _pallas(body, name=name, out_shape=_sds((N_HEADS, N_REL), F32),
                   in_specs=[pl.BlockSpec(memory_space=pltpu.VMEM)],
                   out_specs=pl.BlockSpec(memory_space=pltpu.VMEM))(dwb)


def _attn_block(qkv_ref, blk):
    r0 = pl.multiple_of(blk * QB, QB)
    qs = qkv_ref[0, pl.ds(r0 + FRONT, QB), :] * SCALE
    k2 = qkv_ref[1, pl.ds(r0 + (FRONT - PAD), KW), :]
    v2 = qkv_ref[2, pl.ds(r0 + (FRONT - PAD), KW), :]
    col = lax.broadcasted_iota(jnp.int32, (1, KW), 1)
    return r0, qs, k2, v2, col >= PAD - blk * QB


def _head_mask(h):
    lane = lax.broadcasted_iota(jnp.int32, (1, 2 * HEAD_DIM), 1)
    return (lane < HEAD_DIM) if h == 0 else (lane >= HEAD_DIM)


def _stack_heads(a):
    zero = jnp.zeros_like(a)
    return jnp.concatenate([jnp.where(_head_mask(0), a, zero), jnp.where(_head_mask(1), a, zero)], axis=0)


def _unstack_heads(a):
    return jnp.where(_head_mask(0), a[:QB], a[QB:])


def _attn_exp(qst, k2, w_ref, kvalid):
    s = jnp.where(kvalid, _dot(qst, k2, NT) + w_ref[...].reshape(2 * QB, KW), NEG)
    e = jnp.exp(s - jnp.max(s, axis=-1, keepdims=True))
    return e, 1.0 / jnp.sum(e, axis=-1, keepdims=True)


ATTN_G = 2


def attn_fwd(qkvp, wb, name, comm=None):
    T = qkvp.shape[1] - FRONT

    def body(qkv_ref, w_ref, o_ref):
        b = pl.program_id(1)
        for t in range(ATTN_G):
            _, qs, k2, v2, kvalid = _attn_block(qkv_ref, b * ATTN_G + t)
            e, inv = _attn_exp(_stack_heads(qs), k2, w_ref, kvalid)
            o_ref[t * QB:(t + 1) * QB, :] = _unstack_heads(_dot(e.astype(BF16), v2, NN) * inv).astype(BF16)

    return _call(
        body, comm, 1, (qkvp, wb), name=name, grid=(N_HEADS // 2, T // (QB * ATTN_G)),
        in_specs=[pl.BlockSpec((3, FRONT + T, 2 * HEAD_DIM), lambda hp, b: (0, 0, hp)),
                  pl.BlockSpec((2, QB, KW), lambda hp, b: (hp, 0, 0))],
        out_specs=pl.BlockSpec((QB * ATTN_G, 2 * HEAD_DIM), lambda hp, b: (b, hp)),
        out_shape=_sds((T, D), BF16),
        compiler_params=_params(("parallel", "arbitrary"), VMEM_BIG))


def attn_bwd(qkvp, o, do, wb, name, comm=None):
    T = qkvp.shape[1] - FRONT
    nb = T // (QB * ATTN_G)

    def body(qkv_ref, o_ref, do_ref, w_ref, dqkv_ref, dw_ref, dk_acc, dv_acc):
        b = pl.program_id(1)

        @pl.when(b == 0)
        def _():
            dk_acc[...] = jnp.zeros_like(dk_acc)
            dv_acc[...] = jnp.zeros_like(dv_acc)
            dw_ref[...] = jnp.zeros_like(dw_ref)
            dqkv_ref[0, 0:FRONT, :] = jnp.zeros((FRONT, 2 * HEAD_DIM), BF16)

        dws = None
        for t in range(ATTN_G):
            r0, qs, k2, v2, kvalid = _attn_block(qkv_ref, b * ATTN_G + t)
            qst = _stack_heads(qs)
            e, inv = _attn_exp(qst, k2, w_ref, kvalid)
            do2 = do_ref[t * QB:(t + 1) * QB, :]
            dost = _stack_heads(do2)
            prod = _stack_heads(do2.astype(F32) * o_ref[t * QB:(t + 1) * QB, :].astype(F32))
            delta = jnp.sum(prod, axis=-1, keepdims=True)
            ds = e * ((_dot(dost, v2, NT) - delta) * inv)
            dws = ds if dws is None else dws + ds
            ds16 = ds.astype(BF16)
            dq = _unstack_heads(_dot(ds16, k2, NN)) * SCALE
            dqkv_ref[0, pl.ds(r0 + FRONT, QB), :] = dq.astype(BF16)
            dk_acc[pl.ds(r0 + (FRONT - PAD), KW), :] += _dot(ds16, qst, TN)
            dv_acc[pl.ds(r0 + (FRONT - PAD), KW), :] += _dot(e.astype(BF16), (dost.astype(F32) * inv).astype(BF16), TN)
        dw_ref[...] += dws.reshape(2, QB, KW)

        @pl.when(b == nb - 1)
        def _():
            dqkv_ref[1] = dk_acc[...].astype(BF16)
            dqkv_ref[2] = dv_acc[...].astype(BF16)

    slab = pl.BlockSpec((3, FRONT + T, 2 * HEAD_DIM), lambda hp, b: (0, 0, hp))
    wspec = pl.BlockSpec((2, QB, KW), lambda hp, b: (hp, 0, 0))
    rows = pl.BlockSpec((QB * ATTN_G, 2 * HEAD_DIM), lambda hp, b: (b, hp))
    return _call(
        body, comm, 2, (qkvp, o, do, wb), name=name, grid=(N_HEADS // 2, nb),
        in_specs=[slab, rows, rows, wspec],
        out_specs=[slab, wspec],
        out_shape=[_sds((3, FRONT + T, D), BF16), _sds((N_HEADS, QB, KW), F32)],
        scratch_shapes=[pltpu.VMEM((FRONT + T, 2 * HEAD_DIM), F32), pltpu.VMEM((FRONT + T, 2 * HEAD_DIM), F32)],
        compiler_params=_params(("parallel", "arbitrary"), VMEM_BIG))


def proj_qkv(hn, w, l, name, tm=512):
    T = hn.shape[0]
    pb = FRONT // tm

    def body(a_ref, b_ref, o_ref):
        i = pl.program_id(1)

        @pl.when(i < pb)
        def _():
            o_ref[...] = jnp.zeros_like(o_ref)

        @pl.when(i >= pb)
        def _():
            o_ref[...] = _dot(a_ref[...], b_ref[...], NN).astype(BF16)

    return _pallas(
        body, name=name, grid=(3, pb + T // tm),
        in_specs=[pl.BlockSpec((tm, D), lambda p, i: (jnp.maximum(i - pb, 0), 0)),
                  pl.BlockSpec((None, D, D), lambda p, i: (l, 0, p))],
        out_specs=pl.BlockSpec((None, tm, D), lambda p, i: (p, i, 0)),
        out_shape=_sds((3, FRONT + T, D), BF16),
        compiler_params=_params(("parallel", "parallel"), VMEM_BIG))(hn, w)


def ffn_up(hn, wg, wu, l, name, tm=1024, comm=None):
    T = hn.shape[0]

    def body(a_ref, wg_ref, wu_ref, g_ref, u_ref, h_ref):
        a = a_ref[...]
        g = _dot(a, wg_ref[...], NT)
        u = _dot(a, wu_ref[...], NT)
        g_ref[...] = g.astype(BF16)
        u_ref[...] = u.astype(BF16)
        h_ref[...] = ((g * _sigmoid(g)) * u).astype(BF16)

    wspec = pl.BlockSpec((None, None, FS, D), lambda s, i: (l, s, 0, 0))
    ospec = pl.BlockSpec((None, tm, FS), lambda s, i: (s, i, 0))
    return _call(
        body, comm, 3, (hn, wg, wu), name=name, grid=(N_CHIPS, T // tm),
        in_specs=[pl.BlockSpec((tm, D), lambda s, i: (i, 0)), wspec, wspec],
        out_specs=[ospec, ospec, ospec],
        out_shape=[_sds((N_CHIPS, T, FS), BF16)] * 3,
        compiler_params=_params(("parallel", "parallel"), VMEM_BIG))


def ffn_bwd_dh(dxb, wd, g, u, l, name, tm=1024, comm=None):
    T = dxb.shape[0]

    def body(a_ref, wd_ref, g_ref, u_ref, dg_ref, du_ref):
        dh = _dot(a_ref[...], wd_ref[...], NT)
        gf = g_ref[...].astype(F32)
        uf = u_ref[...].astype(F32)
        s = _sigmoid(gf)
        dg_ref[...] = (dh * uf * (s * (1.0 + gf * (1.0 - s)))).astype(BF16)
        du_ref[...] = (dh * (gf * s)).astype(BF16)

    aspec = pl.BlockSpec((None, tm, FS), lambda s, i: (s, i, 0))
    return _call(
        body, comm, 2, (dxb, wd, g, u), name=name, grid=(N_CHIPS, T // tm),
        in_specs=[pl.BlockSpec((tm, D), lambda s, i: (i, 0)),
                  pl.BlockSpec((None, None, FS, D), lambda s, i: (l, s, 0, 0)), aspec, aspec],
        out_specs=[aspec, aspec],
        out_shape=[_sds((N_CHIPS, T, FS), BF16)] * 2,
        compiler_params=_params(("parallel", "parallel"), VMEM_BIG))


def ffn_bwd_dhn(dg, du, wg, wu, l, name, tm=512):
    T = dg.shape[1]

    def body(dg_ref, du_ref, wg_ref, wu_ref, o_ref):
        d = None
        for s in range(N_CHIPS):
            t = _dot(dg_ref[s], wg_ref[s], NN) + _dot(du_ref[s], wu_ref[s], NN)
            d = t if d is None else d + t
        o_ref[...] = d

    aspec = pl.BlockSpec((N_CHIPS, tm, FS), lambda i: (0, i, 0))
    wspec = pl.BlockSpec((None, N_CHIPS, FS, D), lambda i: (l, 0, 0, 0))
    return _pallas(
        body, name=name, grid=(T // tm,), in_specs=[aspec, aspec, wspec, wspec],
        out_specs=pl.BlockSpec((tm, D), lambda i: (i, 0)), out_shape=_sds((T, D), F32),
        compiler_params=_params(("parallel",), VMEM_BIG))(dg, du, wg, wu)


def ffn_down_res(h, wd, res, l, name, tm=512, comm=None):
    T = h.shape[1]

    def body(h_ref, wd_ref, r_ref, o_ref):
        d = r_ref[...]
        for s in range(N_CHIPS):
            d = d + _dot(h_ref[s], wd_ref[s], NN)
        o_ref[...] = d

    row = pl.BlockSpec((tm, D), lambda i: (i, 0))
    return _call(
        body, comm, 1, (h, wd, res), name=name, grid=(T // tm,),
        in_specs=[pl.BlockSpec((N_CHIPS, tm, FS), lambda i: (0, i, 0)),
                  pl.BlockSpec((None, N_CHIPS, FS, D), lambda i: (l, 0, 0, 0)), row],
        out_specs=row, out_shape=_sds((T, D), F32),
        compiler_params=_params(("parallel",), VMEM_BIG))


def qkv_bwd(dqkvp, w, l, name, tm=512):
    T = dqkvp.shape[1] - FRONT

    def body(a_ref, w_ref, o_ref):
        d = None
        for p in range(3):
            t = _dot(a_ref[p], w_ref[:, p * D:(p + 1) * D], NT)
            d = t if d is None else d + t
        o_ref[...] = d

    return _pallas(
        body, name=name, grid=(T // tm,),
        in_specs=[pl.BlockSpec((3, tm, D), lambda i: (0, i + FRONT // tm, 0)),
                  pl.BlockSpec((None, D, 3 * D), lambda i: (l, 0, 0))],
        out_specs=pl.BlockSpec((tm, D), lambda i: (i, 0)), out_shape=_sds((T, D), F32),
        compiler_params=_params(("parallel",), VMEM_BIG))(dqkvp, w)


def adamw(w, g, m, v, name):
    L, R, C = w.shape

    def body(w_ref, g_ref, m_ref, v_ref, d_ref, nm_ref, nv_ref):
        gf = g_ref[...]
        nm = ADAM_B1 * m_ref[...] + (1.0 - ADAM_B1) * gf
        nv = ADAM_B2 * v_ref[...] + (1.0 - ADAM_B2) * (gf * gf)
        m_hat = nm / (1.0 - ADAM_B1 ** ADAM_STEP)
        v_hat = nv / (1.0 - ADAM_B2 ** ADAM_STEP)
        d_ref[...] = -ADAM_LR * (m_hat / (jnp.sqrt(v_hat) + ADAM_EPS) + ADAM_WD * w_ref[...])
        nm_ref[...] = nm
        nv_ref[...] = nv

    tr = R // 4 if R % 32 == 0 else R
    spec = pl.BlockSpec((None, tr, C), lambda l, r: (l, r, 0))
    return _pallas(body, name=name, grid=(L, R // tr), in_specs=[spec] * 4, out_specs=[spec] * 3,
                   out_shape=[_sds((L, R, C), F32)] * 3,
                   compiler_params=_params(("parallel", "parallel")))(w, g, m, v)


def _coords():
    return lax.axis_index("x"), lax.axis_index("y"), lax.axis_index("c")


def _other_chips(x, y):
    out = []
    for fx, fy in ((1, 0), (0, 1), (1, 1)):
        px = (1 - x) if fx else x
        py = (1 - y) if fy else y
        out.append((px, py))
    return out


def _flip_index(s, j):
    sx, sy = s // 2, s % 2
    fx, fy = ((1, 0), (0, 1), (1, 1))[j]
    return 2 * (sx ^ fx) + (sy ^ fy)


def _for_my_chip(sme, fn):
    for s in range(N_CHIPS):
        pl.when(sme == s)(functools.partial(fn, s))


ANY = pl.BlockSpec(memory_space=pl.ANY)

GATHER_KIND = {"a_w_in": "col", "b_w_qkv": "col", "a_w_out": "row", "b_w_out": "row",
               "ffn_w_gate": "row", "ffn_w_up": "row", "ffn_w_down": "row"}
BIG = tuple(GATHER_KIND)


def _gathered_shape(kind, shape):
    L, R, C = shape
    return (L, R, N_CHIPS * C) if kind == "col" else (L, N_CHIPS, R, C)


def _shard_rows(ref, kind, s, r0, rn, C):
    if kind == "col":
        return ref.at[:, pl.ds(r0, rn), s * C:(s + 1) * C]
    return ref.at[:, s, pl.ds(r0, rn), :]


def gather_stage1(items):
    n = len(items)
    dims = [it[0].shape[1:] for it in items]

    def copies(ins, outs, sems, s):
        lsem, ssem, rsem = sems
        x, y, c = _coords()
        chips = _other_chips(x, y)
        local, send, landed = [], [], []
        for t, (_, li, kind) in enumerate(items):
            R, C = dims[t]
            r0 = pl.multiple_of(c * (R // 2), 8)
            local.append(pltpu.make_async_copy(ins[t].at[pl.ds(li, 1)], _shard_rows(outs[t], kind, s, 0, R, C),
                                               lsem.at[t]))
            for j in range(3):
                pair = dict(send_sem=ssem.at[3 * t + j], recv_sem=rsem.at[3 * t + j],
                            device_id=(chips[j][0], chips[j][1], c), device_id_type=MESH)
                send.append(pltpu.make_async_remote_copy(
                    src_ref=ins[t].at[pl.ds(li, 1), pl.ds(r0, R // 2), :],
                    dst_ref=_shard_rows(outs[t], kind, s, r0, R // 2, C), **pair))
                got = _shard_rows(outs[t], kind, _flip_index(s, j), r0, R // 2, C)
                landed.append(pltpu.make_async_remote_copy(src_ref=got, dst_ref=got, **pair))
        return local, send, landed

    def start(ins, outs, sems):
        def run(s):
            local, send, _ = copies(ins, outs, sems, s)
            for cp in local + send:
                cp.start()
        x, y, _ = _coords()
        _for_my_chip(2 * x + y, run)

    def wait(ins, outs, sems):
        def run(s):
            local, send, landed = copies(ins, outs, sems, s)
            for cp in landed:
                cp.wait_recv()
            for cp in send:
                cp.wait_send()
            for cp in local:
                cp.wait()
        x, y, _ = _coords()
        _for_my_chip(2 * x + y, run)

    out_shapes = [_sds(_gathered_shape(kind, (1,) + tuple(dims[t])), BF16) for t, (_, _, kind) in enumerate(items)]
    sems = [pltpu.SemaphoreType.DMA((n,)), pltpu.SemaphoreType.DMA((3 * n,)), pltpu.SemaphoreType.DMA((3 * n,))]
    return Comm([it[0] for it in items], out_shapes, sems, start, wait)


def gather_stage2(items, gathered):
    n = len(items)
    dims = [it[0].shape[1:] for it in items]

    def copies(outs, sems, s):
        ssem, rsem = sems
        x, y, c = _coords()
        send, landed = [], []
        for t, (_, _, kind) in enumerate(items):
            R, C = dims[t]
            for j in range(3):
                pair = dict(send_sem=ssem.at[3 * t + j], recv_sem=rsem.at[3 * t + j],
                            device_id=(x, y, 1 - c), device_id_type=MESH)
                mine = _shard_rows(outs[t], kind, _flip_index(s, j), pl.multiple_of(c * (R // 2), 8), R // 2, C)
                other = _shard_rows(outs[t], kind, _flip_index(s, j), pl.multiple_of((1 - c) * (R // 2), 8), R // 2, C)
                send.append(pltpu.make_async_remote_copy(src_ref=mine, dst_ref=mine, **pair))
                landed.append(pltpu.make_async_remote_copy(src_ref=other, dst_ref=other, **pair))
        return send, landed

    def start(ins, outs, sems):
        def run(s):
            for cp in copies(outs, sems, s)[0]:
                cp.start()
        x, y, _ = _coords()
        _for_my_chip(2 * x + y, run)

    def wait(ins, outs, sems):
        def run(s):
            send, landed = copies(outs, sems, s)
            for cp in landed:
                cp.wait_recv()
            for cp in send:
                cp.wait_send()
        x, y, _ = _coords()
        _for_my_chip(2 * x + y, run)

    out_shapes = [_sds(g.shape, BF16) for g in gathered]
    sems = [pltpu.SemaphoreType.DMA((3 * n,)), pltpu.SemaphoreType.DMA((3 * n,))]
    return Comm(gathered, out_shapes, sems, start, wait, aliases={t: t for t in range(n)})


def _half_shape(kind, R, C):
    return (R // 2, N_CHIPS * C) if kind == "col" else (N_CHIPS, R // 2, C)


def exchange_halves(grads, metas):
    n = len(grads)

    def copies(ins, outs, sems):
        ssem, rsem = sems
        x, y, c = _coords()
        out = []
        for t, (kind, R, C) in enumerate(metas):
            r0 = pl.multiple_of((1 - c) * (R // 2), 8)
            src = ins[t].at[pl.ds(r0, R // 2), :] if kind == "col" else ins[t].at[:, pl.ds(r0, R // 2), :]
            out.append(pltpu.make_async_remote_copy(
                src_ref=src, dst_ref=outs[t], send_sem=ssem.at[t], recv_sem=rsem.at[t],
                device_id=(x, y, 1 - c), device_id_type=MESH))
        return out

    def start(ins, outs, sems):
        for cp in copies(ins, outs, sems):
            cp.start()

    def wait(ins, outs, sems):
        for cp in copies(ins, outs, sems):
            cp.wait()

    return Comm(grads, [_sds(_half_shape(*m), F32) for m in metas], [pltpu.SemaphoreType.DMA((n,))] * 2, start, wait)


def pair_sum(me, g, sib, meta, name):
    kind, R, C = meta
    h = R // 2

    def body(me_ref, g_ref, sib_ref, p16_ref, own_ref):
        s = pl.program_id(0)
        v = g_ref[...] + sib_ref[...]
        p16_ref[...] = v.astype(BF16)

        @pl.when(s == me_ref[1])
        def _():
            own_ref[...] = v

    if kind == "col":
        gspec = pl.BlockSpec((h, C), lambda s, me_ref: (me_ref[0], s))
        sspec = pl.BlockSpec((h, C), lambda s, me_ref: (0, s))
    else:
        gspec = pl.BlockSpec((None, h, C), lambda s, me_ref: (s, me_ref[0], 0))
        sspec = pl.BlockSpec((None, h, C), lambda s, me_ref: (s, 0, 0))
    grid_spec = pltpu.PrefetchScalarGridSpec(
        num_scalar_prefetch=1, grid=(N_CHIPS,), in_specs=[gspec, sspec],
        out_specs=[sspec, pl.BlockSpec((h, C), lambda s, me_ref: (0, 0))])
    return _pallas(body, name=name, grid_spec=grid_spec,
                   out_shape=[_sds(_half_shape(*meta), BF16), _sds((h, C), F32)],
                   compiler_params=_params(("arbitrary",), VMEM_BIG))(me, g, sib)


def scatter_partials(p16s, metas):
    n = len(p16s)

    def copies(ins, outs, sems, s):
        ssem, rsem = sems
        x, y, c = _coords()
        chips = _other_chips(x, y)
        out = []
        for t, (kind, R, C) in enumerate(metas):
            for j in range(3):
                sj = _flip_index(s, j)
                src = ins[t].at[:, sj * C:(sj + 1) * C] if kind == "col" else ins[t].at[sj]
                out.append(pltpu.make_async_remote_copy(
                    src_ref=src, dst_ref=outs[t].at[j], send_sem=ssem.at[3 * t + j], recv_sem=rsem.at[3 * t + j],
                    device_id=(chips[j][0], chips[j][1], c), device_id_type=MESH))
        return out

    def start(ins, outs, sems):
        def run(s):
            for cp in copies(ins, outs, sems, s):
                cp.start()
        x, y, _ = _coords()
        _for_my_chip(2 * x + y, run)

    def wait(ins, outs, sems):
        def run(s):
            for cp in copies(ins, outs, sems, s):
                cp.wait()
        x, y, _ = _coords()
        _for_my_chip(2 * x + y, run)

    return Comm(p16s, [_sds((3, R // 2, C), BF16) for (_, R, C) in metas],
                [pltpu.SemaphoreType.DMA((3 * n,))] * 2, start, wait)


def final_sum(me, own, q, buf, l, meta, name):
    _, R, C = meta
    h = R // 2

    def body(me_ref, own_ref, q_ref, buf_ref, o_ref):
        del buf_ref
        o_ref[...] = ((own_ref[...] + q_ref[0].astype(F32)) + q_ref[1].astype(F32)) + q_ref[2].astype(F32)

    grid_spec = pltpu.PrefetchScalarGridSpec(
        num_scalar_prefetch=1, grid=(1,),
        in_specs=[pl.BlockSpec((h, C), lambda i, me_ref: (0, 0)),
                  pl.BlockSpec((3, h, C), lambda i, me_ref: (0, 0, 0)), ANY],
        out_specs=pl.BlockSpec((None, h, C), lambda i, me_ref: (l, me_ref[0], 0)))
    return _pallas(body, name=name, grid_spec=grid_spec, out_shape=_sds(buf.shape, F32),
                   input_output_aliases={3: 0},
                   compiler_params=_params(("arbitrary",), VMEM_BIG))(me, own, q, buf)


def share_final(bufs):
    n = len(bufs)

    def body(*refs):
        ins, outs = refs[:n], refs[n:2 * n]
        ssem, rsem = refs[2 * n:]
        del ins
        x, y, c = _coords()
        copies = []
        for t in range(n):
            R = bufs[t].shape[1]
            r0 = pl.multiple_of(c * (R // 2), 8)
            blk = outs[t].at[:, pl.ds(r0, R // 2), :]
            copies.append(pltpu.make_async_remote_copy(
                src_ref=blk, dst_ref=blk, send_sem=ssem.at[t], recv_sem=rsem.at[t],
                device_id=(x, y, 1 - c), device_id_type=MESH))
        for cp in copies:
            cp.start()
        for t in range(n):
            R = bufs[t].shape[1]
            r1 = pl.multiple_of((1 - c) * (R // 2), 8)
            other = outs[t].at[:, pl.ds(r1, R // 2), :]
            pltpu.make_async_remote_copy(
                src_ref=other, dst_ref=other, send_sem=ssem.at[t], recv_sem=rsem.at[t],
                device_id=(x, y, 1 - c), device_id_type=MESH).wait_recv()
        for cp in copies:
            cp.wait_send()

    out_shape = [_sds(b.shape, F32) for b in bufs]
    return _pallas(body, name="share_final", in_specs=[ANY] * n, out_specs=[ANY] * n, out_shape=out_shape,
                   input_output_aliases={t: t for t in range(n)},
                   scratch_shapes=[pltpu.SemaphoreType.DMA((n,))] * 2,
                   compiler_params=pltpu.CompilerParams(has_side_effects=True))(*bufs)


def allreduce_small(part):
    rows = part.shape[0]

    def body(p_ref, o_ref, buf, ssem, rsem, lsem):
        x, y, c = _coords()
        me = 4 * x + 2 * y + c
        mine = pltpu.make_async_copy(p_ref, buf.at[me], lsem)
        mine.start()
        copies = []
        for r in range(1, 8):
            fx, fy, fc = (r >> 2) & 1, (r >> 1) & 1, r & 1
            peer = ((1 - x) if fx else x, (1 - y) if fy else y, (1 - c) if fc else c)
            copies.append(pltpu.make_async_remote_copy(
                src_ref=p_ref, dst_ref=buf.at[me], send_sem=ssem.at[r - 1], recv_sem=rsem.at[r - 1],
                device_id=peer, device_id_type=MESH))
        for cp in copies:
            cp.start()
        for r in range(1, 8):
            fx, fy, fc = (r >> 2) & 1, (r >> 1) & 1, r & 1
            src = 4 * ((1 - x) if fx else x) + 2 * ((1 - y) if fy else y) + ((1 - c) if fc else c)
            pltpu.make_async_remote_copy(
                src_ref=buf.at[src], dst_ref=buf.at[src], send_sem=ssem.at[r - 1], recv_sem=rsem.at[r - 1],
                device_id=(x, y, c), device_id_type=MESH).wait_recv()
        for cp in copies:
            cp.wait_send()
        mine.wait()
        acc = buf[0]
        for d in range(1, 8):
            acc = acc + buf[d]
        o_ref[...] = acc

    return _pallas(body, name="allreduce_small",
                   in_specs=[pl.BlockSpec(memory_space=pltpu.VMEM)], out_specs=pl.BlockSpec(memory_space=pltpu.VMEM),
                   out_shape=_sds((rows, 128), F32),
                   scratch_shapes=[pltpu.VMEM((8, rows, 128), F32), pltpu.SemaphoreType.DMA((7,)),
                                   pltpu.SemaphoreType.DMA((7,)), pltpu.SemaphoreType.DMA],
                   compiler_params=pltpu.CompilerParams(has_side_effects=True, vmem_limit_bytes=VMEM_BIG))(part)


def _rows128(a):
    flat = a.reshape(-1)
    rows = -(-flat.shape[0] // 128)
    rows8 = -(-rows // 8) * 8
    flat = jnp.pad(flat, (0, rows8 * 128 - flat.shape[0]))
    return flat.reshape(rows8, 128)


def kernel(x, norm_mix_g, norm_ffn_g, final_g, a_w_in, a_v_gain, a_w_s, a_b_s, a_w_out, b_w_qkv, b_rel_bias, b_w_out, ffn_w_gate, ffn_w_up, ffn_w_down, loss_target, m_norm_mix_g, m_norm_ffn_g, m_final_g, m_a_w_in, m_a_v_gain, m_a_w_s, m_a_b_s, m_a_w_out, m_b_w_qkv, m_b_rel_bias, m_b_w_out, m_ffn_w_gate, m_ffn_w_up, m_ffn_w_down, v_norm_mix_g, v_norm_ffn_g, v_final_g, v_a_w_in, v_a_v_gain, v_a_w_s, v_a_b_s, v_a_w_out, v_b_w_qkv, v_b_rel_bias, v_b_w_out, v_ffn_w_gate, v_ffn_w_up, v_ffn_w_down):
    T = x.shape[1]
    weights = dict(norm_mix_g=norm_mix_g, norm_ffn_g=norm_ffn_g, final_g=final_g, a_w_in=a_w_in, a_v_gain=a_v_gain,
                   a_w_s=a_w_s, a_b_s=a_b_s, a_w_out=a_w_out, b_w_qkv=b_w_qkv, b_rel_bias=b_rel_bias,
                   b_w_out=b_w_out, ffn_w_gate=ffn_w_gate, ffn_w_up=ffn_w_up, ffn_w_down=ffn_w_down)
    mom_m = dict(norm_mix_g=m_norm_mix_g, norm_ffn_g=m_norm_ffn_g, final_g=m_final_g, a_w_in=m_a_w_in,
                 a_v_gain=m_a_v_gain, a_w_s=m_a_w_s, a_b_s=m_a_b_s, a_w_out=m_a_w_out, b_w_qkv=m_b_w_qkv,
                 b_rel_bias=m_b_rel_bias, b_w_out=m_b_w_out, ffn_w_gate=m_ffn_w_gate, ffn_w_up=m_ffn_w_up,
                 ffn_w_down=m_ffn_w_down)
    mom_v = dict(norm_mix_g=v_norm_mix_g, norm_ffn_g=v_norm_ffn_g, final_g=v_final_g, a_w_in=v_a_w_in,
                 a_v_gain=v_a_v_gain, a_w_s=v_a_w_s, a_b_s=v_a_b_s, a_w_out=v_a_w_out, b_w_qkv=v_b_w_qkv,
                 b_rel_bias=v_b_rel_bias, b_w_out=v_b_w_out, ffn_w_gate=v_ffn_w_gate, ffn_w_up=v_ffn_w_up,
                 ffn_w_down=v_ffn_w_down)
    order = list(weights)
    transposed = ("ffn_w_gate", "ffn_w_up")
    for k in transposed:
        weights[k], mom_m[k], mom_v[k] = (jnp.swapaxes(a, 1, 2) for a in (weights[k], mom_m[k], mom_v[k]))

    xi, yi, ci = _coords()
    me = jnp.stack([ci, 2 * xi + yi]).astype(jnp.int32)

    shard16 = {k: cast_bf16(weights[k], "cast_" + k) for k in BIG}

    def layer_tensors(i):
        mix = ("a_w_in", "a_w_out") if i % 2 == 0 else ("b_w_qkv", "b_w_out")
        return [(k, i // 2) for k in mix] + [(k, i) for k in ("ffn_w_gate", "ffn_w_up", "ffn_w_down")]

    def gather_items(layers):
        return [(shard16[k], l, GATHER_KIND[k]) for i in layers for (k, l) in layer_tensors(i)]

    W = [None] * DEPTH

    def set_weights(layers, gathered):
        for n_, i in enumerate(layers):
            a, b_, gate, up, down = gathered[5 * n_:5 * n_ + 5]
            if i % 2 == 0:
                W[i] = dict(w_in=a, w_out=b_.reshape(1, GH, D), gate=gate, up=up, down=down)
            else:
                W[i] = dict(w_qkv=a, w_out=b_.reshape(1, D, D), gate=gate, up=up, down=down)

    part0 = run_comm(gather_stage1(gather_items([0])), "gather_l0_ici")
    set_weights([0], run_comm(gather_stage2(gather_items([0]), part0), "gather_l0_d2d"))

    row512 = lambda: pl.BlockSpec((512, D), lambda i, j: (i, 0))

    xcur = x.reshape(T, D)
    saved = []
    for i in range(DEPTH):
        j = i // 2
        tag = "_l%d" % i
        st = {"x_in": xcur}
        hn = rms_fwd(xcur, norm_mix_g[i][None], "rms_mix" + tag)
        st["hn"] = hn
        Wi = W[i]
        if i % 2 == 0:
            pre = matmul("a_in" + tag, NN, hn, pl.BlockSpec((1024, D), lambda i_, j_: (i_, 0)),
                         Wi["w_in"], pl.BlockSpec((None, D, 1024), lambda i_, j_: (0, 0, j_)),
                         _sds((T, 2 * GH), BF16), pl.BlockSpec((1024, 1024), lambda i_, j_: (i_, j_)),
                         (T // 1024, 4))
            y = sgu_fwd(pre, a_v_gain[j][None], a_w_s[j], a_b_s[j][:, :, None], "sgu_fwd" + tag)
            xmid = matmul("a_out" + tag, NN, y, pl.BlockSpec((512, GH), lambda i_, j_: (i_, 0)),
                          Wi["w_out"], pl.BlockSpec((None, GH, D), lambda i_, j_: (0, 0, 0)),
                          _sds((T, D), F32), row512(), (T // 512, 1), res=xcur, res_spec=row512())
            st.update(pre=pre, y=y)
        else:
            qkvp = proj_qkv(hn, Wi["w_qkv"], 0, "b_qkv" + tag)
            wb = jnp.transpose(bias_build(b_rel_bias[j], "bias_build" + tag), (1, 0, 2))
            if i == 1:
                o, part23 = attn_fwd(qkvp, wb, "attn_fwd" + tag, comm=gather_stage1(gather_items([2, 3])))
            else:
                o = attn_fwd(qkvp, wb, "attn_fwd" + tag)
            xmid = matmul("b_out" + tag, NN, o, pl.BlockSpec((512, D), lambda i_, j_: (i_, 0)),
                          Wi["w_out"], pl.BlockSpec((None, D, D), lambda i_, j_: (0, 0, 0)),
                          _sds((T, D), F32), row512(), (T // 512, 1), res=xcur, res_spec=row512())
            st.update(qkvp=qkvp, wb=wb, o=o)
        hn2 = rms_fwd(xmid, norm_ffn_g[i][None], "rms_ffn" + tag)
        if i == 0:
            (g, u, h), part1 = ffn_up(hn2, Wi["gate"], Wi["up"], 0, "ffn_up" + tag,
                                      comm=gather_stage1(gather_items([1])))
            xcur, full1 = ffn_down_res(h, Wi["down"], xmid, 0, "ffn_down" + tag,
                                       comm=gather_stage2(gather_items([1]), part1))
            set_weights([1], full1)
        elif i == 1:
            (g, u, h), full23 = ffn_up(hn2, Wi["gate"], Wi["up"], 0, "ffn_up" + tag,
                                       comm=gather_stage2(gather_items([2, 3]), part23))
            set_weights([2, 3], full23)
            xcur = ffn_down_res(h, Wi["down"], xmid, 0, "ffn_down" + tag)
        else:
            g, u, h = ffn_up(hn2, Wi["gate"], Wi["up"], 0, "ffn_up" + tag)
            xcur = ffn_down_res(h, Wi["down"], xmid, 0, "ffn_down" + tag)
        st.update(x_mid=xmid, hn2=hn2, g=g, u=u, h=h)
        saved.append(st)

    loss_part, dx, dxb, d_final = final_loss(xcur, final_g[None], loss_target.reshape(T, D), "final_loss")

    tk = min(2048, T)
    big_grads = {k: [None] * weights[k].shape[0] for k in BIG}
    small = {"norm_mix_g": [None] * DEPTH, "norm_ffn_g": [None] * DEPTH, "a_v_gain": [None] * 2,
             "a_w_s": [None] * 2, "a_b_s": [None] * 2, "b_rel_bias": [None] * 2}
    tok = lambda width: pl.BlockSpec((tk, width), lambda j_, k_: (k_, 0))

    def layer_metas(i):
        return [(GATHER_KIND[k],) + tuple(weights[k].shape[1:]) for k, _ in layer_tensors(i)]

    def layer_grads(i):
        return [big_grads[k][l] for k, l in layer_tensors(i)]

    def pair_sums(i, sib):
        out = [pair_sum(me, g_, s_, m_, "pair_sum_%s_l%d" % kl)
               for kl, g_, s_, m_ in zip(layer_tensors(i), layer_grads(i), sib, layer_metas(i))]
        return [o_[0] for o_ in out], [o_[1] for o_ in out]

    own_parts, recv_parts = {}, {}
    for i in reversed(range(DEPTH)):
        j = i // 2
        tag = "_l%d" % i
        st = saved[i]
        Wi = W[i]
        w_down = Wi["down"]
        w_gate, w_up = Wi["gate"], Wi["up"]
        if i < DEPTH - 1:
            (dg, du), sib = ffn_bwd_dh(dxb, w_down, st["g"], st["u"], 0, "ffn_bwd_dh" + tag,
                                       comm=exchange_halves(layer_grads(i + 1), layer_metas(i + 1)))
            p16, own_parts[i + 1] = pair_sums(i + 1, sib)
            scatter = scatter_partials(p16, layer_metas(i + 1))
        else:
            dg, du = ffn_bwd_dh(dxb, w_down, st["g"], st["u"], 0, "ffn_bwd_dh" + tag)
            scatter = None
        part = lambda: pl.BlockSpec((None, tk, FS), lambda j_, k_: (j_, k_, 0))
        big_grads["ffn_w_down"][i] = wgrad(
            "dw_down" + tag, st["h"], part(), dxb, tok(D), _sds((N_CHIPS, FS, D), F32),
            pl.BlockSpec((None, FS, D), lambda j_, k_: (j_, 0, 0)), N_CHIPS, T, tk)
        dhn2 = ffn_bwd_dhn(dg, du, w_gate, w_up, 0, "ffn_bwd_dhn" + tag)
        for nm, dz in (("ffn_w_gate", dg), ("ffn_w_up", du)):
            big_grads[nm][i] = wgrad(
                "d" + nm + tag, dz, part(), st["hn2"], tok(D), _sds((N_CHIPS, FS, D), F32),
                pl.BlockSpec((None, FS, D), lambda j_, k_: (j_, 0, 0)), N_CHIPS, T, tk)
        dx, dxb, dgn = rms_bwd(st["x_mid"], dhn2, norm_ffn_g[i][None], dx, "rms_ffn_bwd" + tag)
        small["norm_ffn_g"][i] = dgn
        if i % 2 == 0:
            dy = matmul("a_out_bwd" + tag, NT, dxb, pl.BlockSpec((1024, D), lambda i_, j_: (i_, 0)),
                        Wi["w_out"], pl.BlockSpec((None, 1024, D), lambda i_, j_: (0, j_, 0)),
                        _sds((T, GH), BF16), pl.BlockSpec((1024, 1024), lambda i_, j_: (i_, j_)), (T // 1024, 2))
            big_grads["a_w_out"][j] = wgrad(
                "dw_aout" + tag, st["y"], pl.BlockSpec((tk, 1024), lambda j_, k_: (k_, j_)), dxb, tok(D),
                _sds((GH, D), F32), pl.BlockSpec((1024, D), lambda j_, k_: (j_, 0)), 2, T, tk
            ).reshape(N_CHIPS, GH // N_CHIPS, D)
            sgu_args = (st["pre"], dy, a_v_gain[j][None], a_w_s[j], a_b_s[j][:, :, None], "sgu_bwd" + tag)
            if scatter is not None:
                (dpre, d_ws, d_bs, d_gain), recv_parts[i + 1] = sgu_bwd(*sgu_args, comm=scatter)
            else:
                dpre, d_ws, d_bs, d_gain = sgu_bwd(*sgu_args)
            small["a_w_s"][j], small["a_b_s"][j], small["a_v_gain"][j] = d_ws, d_bs, d_gain
            dhn = matmul("a_in_bwd" + tag, NT, dpre, pl.BlockSpec((512, 2 * GH), lambda i_, j_: (i_, 0)),
                         Wi["w_in"], pl.BlockSpec((None, D, 2 * GH), lambda i_, j_: (0, 0, 0)),
                         _sds((T, D), F32), row512(), (T // 512, 1))
            big_grads["a_w_in"][j] = wgrad(
                "dw_in" + tag, st["hn"], tok(D), dpre, pl.BlockSpec((tk, 1024), lambda j_, k_: (k_, j_)),
                _sds((D, 2 * GH), F32), pl.BlockSpec((D, 1024), lambda j_, k_: (0, j_)), 4, T, tk)
        else:
            do = matmul("b_out_bwd" + tag, NT, dxb, pl.BlockSpec((1024, D), lambda i_, j_: (i_, 0)),
                        Wi["w_out"], pl.BlockSpec((None, D, D), lambda i_, j_: (0, 0, 0)),
                        _sds((T, D), BF16), pl.BlockSpec((1024, D), lambda i_, j_: (i_, 0)), (T // 1024, 1))
            big_grads["b_w_out"][j] = wgrad(
                "dw_bout" + tag, st["o"], tok(D), dxb, tok(D),
                _sds((D, D), F32), pl.BlockSpec((D, D), lambda j_, k_: (0, 0)), 1, T, tk
            ).reshape(N_CHIPS, D // N_CHIPS, D)
            if scatter is not None:
                (dqkvp, dwb), recv_parts[i + 1] = attn_bwd(st["qkvp"], st["o"], do, st["wb"], "attn_bwd" + tag,
                                                           comm=scatter)
            else:
                dqkvp, dwb = attn_bwd(st["qkvp"], st["o"], do, st["wb"], "attn_bwd" + tag)
            small["b_rel_bias"][j] = bias_grad(jnp.transpose(dwb, (1, 0, 2)), "bias_grad" + tag)
            dhn = qkv_bwd(dqkvp, Wi["w_qkv"], 0, "b_qkv_bwd" + tag)
            big_grads["b_w_qkv"][j] = wgrad(
                "dw_qkv" + tag, st["hn"], tok(D), dqkvp,
                pl.BlockSpec((None, tk, D), lambda j_, k_: (j_, k_ + FRONT // tk, 0)),
                _sds((D, 3 * D), F32), pl.BlockSpec((D, D), lambda j_, k_: (0, j_)), 3, T, tk)
        dx, dxb, dgn = rms_bwd(st["x_in"], dhn, norm_mix_g[i][None], dx, "rms_mix_bwd" + tag)
        small["norm_mix_g"][i] = dgn

    small_grads = {
        "norm_mix_g": jnp.concatenate(small["norm_mix_g"], axis=0),
        "norm_ffn_g": jnp.concatenate(small["norm_ffn_g"], axis=0),
        "final_g": d_final.reshape(D),
        "a_v_gain": jnp.concatenate(small["a_v_gain"], axis=0),
        "a_w_s": jnp.stack(small["a_w_s"]),
        "a_b_s": jnp.stack(small["a_b_s"]).reshape(2, SGU_G, SGU_BLOCK),
        "b_rel_bias": jnp.stack(small["b_rel_bias"]),
    }
    small_names = list(small_grads)
    packed = [_rows128(small_grads[k]) for k in small_names] + [_rows128(loss_part[:, :1])]
    offs = [0]
    for p in packed:
        offs.append(offs[-1] + p.shape[0])
    reduced = allreduce_small(jnp.concatenate(packed, axis=0))
    grads = {}
    for t, k in enumerate(small_names):
        nelem = small_grads[k].size
        grads[k] = reduced[offs[t]:offs[t + 1]].reshape(-1)[:nelem].reshape(weights[k].shape)
    loss = reduced[offs[len(small_names)], 0]

    sib = run_comm(exchange_halves(layer_grads(0), layer_metas(0)), "exchange_l0")
    p16, own_parts[0] = pair_sums(0, sib)
    recv_parts[0] = run_comm(scatter_partials(p16, layer_metas(0)), "scatter_l0")
    bufs = {k: jnp.zeros(weights[k].shape, F32) for k in BIG}
    for i in range(DEPTH):
        for (k, l), own_, q_, m_ in zip(layer_tensors(i), own_parts[i], recv_parts[i], layer_metas(i)):
            bufs[k] = final_sum(me, own_, q_, bufs[k], l, m_, "final_sum_%s_l%d" % (k, l))
    shared = share_final([bufs[k] for k in BIG])
    for k, gfull in zip(BIG, shared):
        grads[k] = gfull

    delta, new_m, new_v = {}, {}, {}
    for k in order:
        shp = weights[k].shape
        if k in BIG:
            view = shp
        elif k == "a_w_s":
            view = (2, SGU_G * SGU_BLOCK, SGU_BLOCK)
        elif len(shp) == 1:
            view = (1, 1, shp[0])
        elif len(shp) == 2:
            view = (1,) + shp
        else:
            view = shp
        d_, m_, v_ = adamw(weights[k].reshape(view), grads[k].reshape(view), mom_m[k].reshape(view),
                           mom_v[k].reshape(view), "adamw_" + k)
        delta[k], new_m[k], new_v[k] = d_.reshape(shp), m_.reshape(shp), v_.reshape(shp)
    for k in transposed:
        for tree in (grads, delta, new_m, new_v):
            tree[k] = jnp.swapaxes(tree[k], 1, 2)

    return (loss, dx.reshape(1, T, D), *[grads[k] for k in order], *[delta[k] for k in order],
            *[new_m[k] for k in order], *[new_v[k] for k in order])
```

```python
import functools

import jax
import jax.numpy as jnp
from jax import lax
from jax.experimental import pallas as pl
from jax.experimental.pallas import tpu as pltpu

F32 = jnp.float32
BF16 = jnp.bfloat16
MESH = pl.DeviceIdType.MESH

D = 1024
DEPTH = 4
EPS = 1e-6
SGU_BLOCK = 128
GH = 2048
SGU_G = 8
SGU_GD = GH // SGU_G
N_HEADS = 16
HEAD_DIM = 64
CHUNK = 64
PAD = 8 * CHUNK
FRONT = 2048
QB = 128
KW = PAD + QB
N_REL = 192
REL_MIN = -(CHUNK - 1)
REL_MAX = 128
D_FF = 2816
FS = D_FF // 4
NEG = -1e30
SCALE = HEAD_DIM ** -0.5
N_CHIPS = 4

ADAM_LR = 0.001
ADAM_B1 = 0.9
ADAM_B2 = 0.999
ADAM_EPS = 1e-08
ADAM_WD = 0.01
ADAM_STEP = 10

VMEM_BIG = 56 * 1024 * 1024

NN = ((1,), (0,))
NT = ((1,), (1,))
TN = ((0,), (0,))


def _dot(a, b, dims):
    return lax.dot_general(a, b, (dims, ((), ())), preferred_element_type=F32)


class Comm:
    def __init__(self, ins, out_shapes, sems, start, wait, aliases=None):
        self.ins, self.out_shapes, self.sems = list(ins), list(out_shapes), list(sems)
        self.start, self.wait, self.aliases = start, wait, dict(aliases or {})


def _host(body, comm, kw):
    grid = tuple(kw["grid"])
    in_specs = list(kw["in_specs"])
    single = not isinstance(kw["out_specs"], (list, tuple))
    out_specs = [kw["out_specs"]] if single else list(kw["out_specs"])
    out_shape = [kw["out_shape"]] if single else list(kw["out_shape"])
    scratch = list(kw.get("scratch_shapes", ()))
    counts = (len(in_specs), len(comm.ins), len(out_specs), len(comm.out_shapes), len(scratch))

    def hosted(*refs):
        parts, p = [], 0
        for cnt in counts:
            parts.append(refs[p:p + cnt])
            p += cnt
        main_in, c_in, main_out, c_out, main_scr = parts
        sems = refs[p:]
        ids = [pl.program_id(a) for a in range(len(grid))]
        first = functools.reduce(jnp.logical_and, [i == 0 for i in ids])
        last = functools.reduce(jnp.logical_and, [i == n - 1 for i, n in zip(ids, grid)])
        pl.when(first)(lambda: comm.start(c_in, c_out, sems))
        body(*main_in, *main_out, *main_scr)
        pl.when(last)(lambda: comm.wait(c_in, c_out, sems))

    old = kw["compiler_params"]
    kw = dict(kw, in_specs=in_specs + [ANY] * len(comm.ins), out_specs=out_specs + [ANY] * len(comm.out_shapes),
              out_shape=out_shape + comm.out_shapes, scratch_shapes=scratch + comm.sems,
              compiler_params=pltpu.CompilerParams(dimension_semantics=("arbitrary",) * len(grid),
                                                   vmem_limit_bytes=old.vmem_limit_bytes, has_side_effects=True))
    if comm.aliases:
        kw["input_output_aliases"] = {counts[0] + i: counts[2] + o for i, o in comm.aliases.items()}
    return hosted, kw


def _pallas(body, comm=None, **kw):
    if comm is not None:
        body, kw = _host(body, comm, kw)
    return pl.pallas_call(body, **kw)


def _split_outs(outs, comm, n_main):
    outs = list(outs) if isinstance(outs, (list, tuple)) else [outs]
    main = outs[:n_main]
    return (main[0] if n_main == 1 else main), outs[n_main:]


def run_comm(comm, name):
    nci, nco = len(comm.ins), len(comm.out_shapes)

    def body(*refs):
        c_in, c_out, sems = refs[:nci], refs[nci:nci + nco], refs[nci + nco:]
        comm.start(c_in, c_out, sems)
        comm.wait(c_in, c_out, sems)

    kw = {}
    if comm.aliases:
        kw["input_output_aliases"] = dict(comm.aliases)
    return _pallas(body, name=name, in_specs=[ANY] * nci, out_specs=[ANY] * nco, out_shape=comm.out_shapes,
                   scratch_shapes=comm.sems, compiler_params=pltpu.CompilerParams(has_side_effects=True),
                   **kw)(*comm.ins)


def combine(comms):
    if len(comms) == 1:
        return comms[0]
    spans, ni, no, ns = [], 0, 0, 0
    for c in comms:
        spans.append((slice(ni, ni + len(c.ins)), slice(no, no + len(c.out_shapes)), slice(ns, ns + len(c.sems))))
        ni, no, ns = ni + len(c.ins), no + len(c.out_shapes), ns + len(c.sems)

    def start(ins, outs, sems):
        for c, (si, so, ss) in zip(comms, spans):
            c.start(ins[si], outs[so], sems[ss])

    def wait(ins, outs, sems):
        for c, (si, so, ss) in zip(comms, spans):
            c.wait(ins[si], outs[so], sems[ss])

    aliases = {}
    for c, (si, so, _) in zip(comms, spans):
        aliases.update({si.start + i: so.start + o for i, o in c.aliases.items()})
    return Comm([a for c in comms for a in c.ins], [o for c in comms for o in c.out_shapes],
                [s for c in comms for s in c.sems], start, wait, aliases)


def _call(body, comm, n_main, args, **kw):
    if comm is None:
        return _pallas(body, **kw)(*args)
    return _split_outs(_pallas(body, comm=comm, **kw)(*args, *comm.ins), comm, n_main)


def _params(sem=None, vmem=None):
    return pltpu.CompilerParams(dimension_semantics=sem, vmem_limit_bytes=vmem)


def _sds(shape, dtype):
    return jax.ShapeDtypeStruct(tuple(shape), dtype)


_GELU_C = 0.7978845608028654


def _gelu(x):
    t = jnp.tanh(_GELU_C * (x + 0.044715 * (x * x * x)))
    return 0.5 * x * (1.0 + t)


def _gelu_and_grad(x):
    x2 = x * x
    t = jnp.tanh(_GELU_C * (x + 0.044715 * (x2 * x)))
    val = 0.5 * x * (1.0 + t)
    grad = 0.5 * (1.0 + t) + 0.5 * x * (1.0 - t * t) * (_GELU_C * (1.0 + 3.0 * 0.044715 * x2))
    return val, grad


def _sigmoid(x):
    return 0.5 * (jnp.tanh(0.5 * x) + 1.0)


def cast_bf16(w, name):
    L, R, C = w.shape

    def body(w_ref, o_ref):
        o_ref[...] = w_ref[...].astype(BF16)

    spec = pl.BlockSpec((None, R, C), lambda l: (l, 0, 0))
    return _pallas(body, name=name, grid=(L,), in_specs=[spec], out_specs=spec,
                   out_shape=_sds((L, R, C), BF16), compiler_params=_params(("parallel",)))(w)


def rms_fwd(x, g, name, tm=512):
    T = x.shape[0]

    def body(x_ref, g_ref, o_ref):
        xf = x_ref[...]
        r = lax.rsqrt(jnp.mean(xf * xf, axis=-1, keepdims=True) + EPS)
        o_ref[...] = ((xf * r) * g_ref[...]).astype(BF16)

    row = pl.BlockSpec((tm, D), lambda i: (i, 0))
    return _pallas(body, name=name, grid=(T // tm,),
                   in_specs=[row, pl.BlockSpec((1, D), lambda i: (0, 0))], out_specs=row,
                   out_shape=_sds((T, D), BF16), compiler_params=_params(("parallel",)))(x, g)


def rms_bwd(x, dh, g, dres, name, tm=256):
    T = x.shape[0]
    n = T // tm

    def body(x_ref, dh_ref, g_ref, dres_ref, dx_ref, dxb_ref, dg_ref, acc_ref):
        i = pl.program_id(0)
        xf = x_ref[...]
        r = lax.rsqrt(jnp.mean(xf * xf, axis=-1, keepdims=True) + EPS)
        xhat = xf * r
        dhf = dh_ref[...]
        part = (dhf * xhat).reshape(tm // 8, 8, D).sum(axis=0)

        @pl.when(i == 0)
        def _():
            acc_ref[...] = part

        @pl.when(i > 0)
        def _():
            acc_ref[...] += part

        dxhat = dhf * g_ref[...]
        dx = dres_ref[...] + r * (dxhat - xhat * jnp.mean(dxhat * xhat, axis=-1, keepdims=True))
        dx_ref[...] = dx
        dxb_ref[...] = dx.astype(BF16)

        @pl.when(i == n - 1)
        def _():
            dg_ref[...] = jnp.sum(acc_ref[...], axis=0, keepdims=True)

    row = pl.BlockSpec((tm, D), lambda i: (i, 0))
    vec = pl.BlockSpec((1, D), lambda i: (0, 0))
    return _pallas(body, name=name, grid=(n,), in_specs=[row, row, vec, row], out_specs=[row, row, vec],
                   out_shape=[_sds((T, D), F32), _sds((T, D), BF16), _sds((1, D), F32)],
                   scratch_shapes=[pltpu.VMEM((8, D), F32)],
                   compiler_params=_params(("arbitrary",)))(x, dh, g, dres)


def final_loss(x, g, tgt, name, tm=256):
    T = x.shape[0]
    n = T // tm

    def body(x_ref, g_ref, t_ref, loss_ref, dx_ref, dxb_ref, dg_ref, acc_ref, lacc_ref):
        i = pl.program_id(0)
        xf = x_ref[...]
        r = lax.rsqrt(jnp.mean(xf * xf, axis=-1, keepdims=True) + EPS)
        xhat = xf * r
        gg = g_ref[...]
        e = xhat * gg - t_ref[...]
        dy = e * (1.0 / D)
        part = (dy * xhat).reshape(tm // 8, 8, D).sum(axis=0)
        lpart = (e * e).reshape(tm // 8, 8, D).sum(axis=0)

        @pl.when(i == 0)
        def _():
            acc_ref[...] = part
            lacc_ref[...] = lpart

        @pl.when(i > 0)
        def _():
            acc_ref[...] += part
            lacc_ref[...] += lpart

        dxhat = dy * gg
        dx = r * (dxhat - xhat * jnp.mean(dxhat * xhat, axis=-1, keepdims=True))
        dx_ref[...] = dx
        dxb_ref[...] = dx.astype(BF16)

        @pl.when(i == n - 1)
        def _():
            dg_ref[...] = jnp.sum(acc_ref[...], axis=0, keepdims=True)
            total = jnp.sum(jnp.sum(lacc_ref[...], axis=0, keepdims=True), axis=1, keepdims=True)
            loss_ref[...] = jnp.broadcast_to(total * (0.5 / D), (1, 128))

    row = pl.BlockSpec((tm, D), lambda i: (i, 0))
    vec = pl.BlockSpec((1, D), lambda i: (0, 0))
    return _pallas(body, name=name, grid=(n,), in_specs=[row, vec, row],
                   out_specs=[pl.BlockSpec((1, 128), lambda i: (0, 0)), row, row, vec],
                   out_shape=[_sds((1, 128), F32), _sds((T, D), F32), _sds((T, D), BF16), _sds((1, D), F32)],
                   scratch_shapes=[pltpu.VMEM((8, D), F32), pltpu.VMEM((8, D), F32)],
                   compiler_params=_params(("arbitrary",)))(x, g, tgt)


def matmul(name, dims, a, a_spec, b, b_spec, out_shape, out_spec, grid, *, acc=False, res=None, res_spec=None,
           comm=None):
    has_res = res is not None

    def body(*refs):
        a_ref, b_ref = refs[0], refs[1]
        r_ref = refs[2] if has_res else None
        o_ref = refs[-1]
        d = _dot(a_ref[...], b_ref[...], dims)
        if not acc:
            if has_res:
                d = d + r_ref[...]
            o_ref[...] = d.astype(o_ref.dtype)
        else:
            k = pl.program_id(len(grid) - 1)

            @pl.when(k == 0)
            def _():
                o_ref[...] = (d + r_ref[...]) if has_res else d

            @pl.when(k > 0)
            def _():
                o_ref[...] += d

    sem = ("parallel",) * (len(grid) - 1) + (("arbitrary",) if acc else ("parallel",))
    ins = [a, b] + ([res] if has_res else [])
    specs = [a_spec, b_spec] + ([res_spec] if has_res else [])
    return _call(body, comm, 1, ins, name=name, grid=grid, in_specs=specs, out_specs=out_spec, out_shape=out_shape,
                 compiler_params=_params(sem, VMEM_BIG))


def wgrad(name, a, a_spec, b, b_spec, out_shape, out_spec, J, T, tk, comm=None):
    return matmul(name, TN, a, a_spec, b, b_spec, out_shape, out_spec, (J, T // tk), acc=True, comm=comm)


def _sgu_mask():
    p = lax.broadcasted_iota(jnp.int32, (SGU_BLOCK, SGU_BLOCK), 0)
    q = lax.broadcasted_iota(jnp.int32, (SGU_BLOCK, SGU_BLOCK), 1)
    return lax.shift_right_logical(q, 6) <= lax.shift_right_logical(p, 6)


def sgu_fwd(pre, gain, w_s, b_s, name, comm=None):
    T = pre.shape[0]

    def body(pre_ref, gain_ref, ws_ref, bs_ref, y_ref):
        mask = _sgu_mask()
        u = _gelu(pre_ref[:, :GH].astype(F32))
        va = _gelu(pre_ref[:, GH:].astype(F32))
        r = lax.rsqrt(jnp.mean(va * va, axis=-1, keepdims=True) + EPS)
        vn = ((va * r) * gain_ref[...]).astype(BF16)
        for g in range(SGU_G):
            sl = slice(g * SGU_GD, (g + 1) * SGU_GD)
            wm = jnp.where(mask, ws_ref[g], 0.0).astype(BF16)
            vm = _dot(wm, vn[:, sl], NN) + bs_ref[g]
            y_ref[:, sl] = (u[:, sl] * vm).astype(BF16)

    return _call(
        body, comm, 1, (pre, gain, w_s, b_s), name=name, grid=(T // SGU_BLOCK,),
        in_specs=[pl.BlockSpec((SGU_BLOCK, 2 * GH), lambda i: (i, 0)),
                  pl.BlockSpec((1, GH), lambda i: (0, 0)),
                  pl.BlockSpec((SGU_G, SGU_BLOCK, SGU_BLOCK), lambda i: (0, 0, 0)),
                  pl.BlockSpec((SGU_G, SGU_BLOCK, 1), lambda i: (0, 0, 0))],
        out_specs=pl.BlockSpec((SGU_BLOCK, GH), lambda i: (i, 0)),
        out_shape=_sds((T, GH), BF16), compiler_params=_params(("parallel",)))


def sgu_bwd(pre, dy, gain, w_s, b_s, name, comm=None):
    T = pre.shape[0]
    n = T // SGU_BLOCK

    def body(pre_ref, dy_ref, gain_ref, ws_ref, bs_ref, dpre_ref, dws_ref, dbs_ref, dgain_ref, gacc_ref):
        i = pl.program_id(0)

        @pl.when(i == 0)
        def _():
            dws_ref[...] = jnp.zeros_like(dws_ref)
            dbs_ref[...] = jnp.zeros_like(dbs_ref)
            gacc_ref[...] = jnp.zeros_like(gacc_ref)

        mask = _sgu_mask()
        u, du_dpre = _gelu_and_grad(pre_ref[:, :GH].astype(F32))
        va, dva_dpre = _gelu_and_grad(pre_ref[:, GH:].astype(F32))
        r = lax.rsqrt(jnp.mean(va * va, axis=-1, keepdims=True) + EPS)
        vhat = va * r
        gain_v = gain_ref[...]
        vn = (vhat * gain_v).astype(BF16)
        dyf = dy_ref[...].astype(F32)
        dvn_parts = []
        for g in range(SGU_G):
            sl = slice(g * SGU_GD, (g + 1) * SGU_GD)
            wm = jnp.where(mask, ws_ref[g], 0.0).astype(BF16)
            vm = _dot(wm, vn[:, sl], NN) + bs_ref[g]
            dpre_ref[:, sl] = ((dyf[:, sl] * vm) * du_dpre[:, sl]).astype(BF16)
            dvm = dyf[:, sl] * u[:, sl]
            dbs_ref[g] += jnp.sum(dvm, axis=-1, keepdims=True)
            dvm16 = dvm.astype(BF16)
            dws_ref[g] += jnp.where(mask, _dot(dvm16, vn[:, sl], NT), 0.0)
            dvn_parts.append(_dot(wm, dvm16, TN))
        dvn = jnp.concatenate(dvn_parts, axis=-1)
        gacc_ref[...] += (dvn * vhat).reshape(SGU_BLOCK // 8, 8, GH).sum(axis=0)
        dvhat = dvn * gain_v
        dva = r * (dvhat - vhat * jnp.mean(dvhat * vhat, axis=-1, keepdims=True))
        dpre_ref[:, GH:] = (dva * dva_dpre).astype(BF16)

        @pl.when(i == n - 1)
        def _():
            dgain_ref[...] = jnp.sum(gacc_ref[...], axis=0, keepdims=True)

    const3 = lambda i: (0, 0, 0)
    return _call(
        body, comm, 4, (pre, dy, gain, w_s, b_s), name=name, grid=(n,),
        in_specs=[pl.BlockSpec((SGU_BLOCK, 2 * GH), lambda i: (i, 0)),
                  pl.BlockSpec((SGU_BLOCK, GH), lambda i: (i, 0)),
                  pl.BlockSpec((1, GH), lambda i: (0, 0)),
                  pl.BlockSpec((SGU_G, SGU_BLOCK, SGU_BLOCK), const3),
                  pl.BlockSpec((SGU_G, SGU_BLOCK, 1), const3)],
        out_specs=[pl.BlockSpec((SGU_BLOCK, 2 * GH), lambda i: (i, 0)),
                   pl.BlockSpec((SGU_G, SGU_BLOCK, SGU_BLOCK), const3),
                   pl.BlockSpec((SGU_G, SGU_BLOCK, 1), const3),
                   pl.BlockSpec((1, GH), lambda i: (0, 0))],
        out_shape=[_sds((T, 2 * GH), BF16), _sds((SGU_G, SGU_BLOCK, SGU_BLOCK), F32),
                   _sds((SGU_G, SGU_BLOCK, 1), F32), _sds((1, GH), F32)],
        scratch_shapes=[pltpu.VMEM((8, GH), F32)],
        compiler_params=_params(("arbitrary",)))


def _rel_onehot(i):
    j = lax.broadcasted_iota(jnp.int32, (N_REL, KW), 1)
    r = lax.broadcasted_iota(jnp.int32, (N_REL, KW), 0)
    idx = jnp.clip(i - j + PAD, REL_MIN, REL_MAX) - REL_MIN
    return (idx == r).astype(BF16)


def _split3(v):
    hi = v.astype(BF16)
    r1 = v - hi.astype(F32)
    mid = r1.astype(BF16)
    lo = (r1 - mid.astype(F32)).astype(BF16)
    return hi, mid, lo


def bias_build(rel_bias, name):
    def body(rb_ref, o_ref):
        parts = _split3(rb_ref[...])

        def row(i, carry):
            oh = _rel_onehot(i)
            val = _dot(parts[0], oh, NN) + _dot(parts[1], oh, NN) + _dot(parts[2], oh, NN)
            j = lax.broadcasted_iota(jnp.int32, (1, KW), 1)
            rel = lax.shift_right_logical(i, 6) - lax.shift_right_logical(j, 6) + 8
            ok = (rel >= 0) & (rel <= 8)
            o_ref[i] = jnp.where(ok, val, NEG)
            return carry

        lax.fori_loop(0, QB, row, 0)

    return _pallas(body, name=name, out_shape=_sds((QB, N_HEADS, KW), F32),
                   in_specs=[pl.BlockSpec(memory_space=pltpu.VMEM)],
                   out_specs=pl.BlockSpec(memory_space=pltpu.VMEM))(rel_bias)


def bias_grad(dwb, name):
    def body(d_ref, o_ref):
        def row(i, acc):
            oh = _rel_onehot(i)
            hi, mid, lo = _split3(d_ref[i])
            return acc + (_dot(hi, oh, NT) + _dot(mid, oh, NT) + _dot(lo, oh, NT))

        o_ref[...] = lax.fori_loop(0, QB, row, jnp.zeros((N_HEADS, N_REL), F32))

    return _pallas(body, name=name, out_shape=_sds((N_HEADS, N_REL), F32),
                   in_specs=[pl.BlockSpec(memory_space=pltpu.VMEM)],
                   out_specs=pl.BlockSpec(memory_space=pltpu.VMEM))(dwb)


def _attn_block(qkv_ref, blk):
    r0 = pl.multiple_of(blk * QB, QB)
    qs = qkv_ref[0, pl.ds(r0 + FRONT, QB), :] * SCALE
    k2 = qkv_ref[1, pl.ds(r0 + (FRONT - PAD), KW), :]
    v2 = qkv_ref[2, pl.ds(r0 + (FRONT - PAD), KW), :]
    col = lax.broadcasted_iota(jnp.int32, (1, KW), 1)
    return r0, qs, k2, v2, col >= PAD - blk * QB


def _head_mask(h):
    lane = lax.broadcasted_iota(jnp.int32, (1, 2 * HEAD_DIM), 1)
    return (lane < HEAD_DIM) if h == 0 else (lane >= HEAD_DIM)


def _stack_heads(a):
    zero = jnp.zeros_like(a)
    return jnp.concatenate([jnp.where(_head_mask(0), a, zero), jnp.where(_head_mask(1), a, zero)], axis=0)


def _unstack_heads(a):
    return jnp.where(_head_mask(0), a[:QB], a[QB:])


def _attn_exp(qst, k2, w_ref, kvalid):
    s = jnp.where(kvalid, _dot(qst, k2, NT) + w_ref[...].reshape(2 * QB, KW), NEG)
    e = jnp.exp(s - jnp.max(s, axis=-1, keepdims=True))
    return e, 1.0 / jnp.sum(e, axis=-1, keepdims=True)


ATTN_G = 2


def attn_fwd(qkvp, wb, name, comm=None):
    T = qkvp.shape[1] - FRONT

    def body(qkv_ref, w_ref, o_ref):
        b = pl.program_id(1)
        for t in range(ATTN_G):
            _, qs, k2, v2, kvalid = _attn_block(qkv_ref, b * ATTN_G + t)
            e, inv = _attn_exp(_stack_heads(qs), k2, w_ref, kvalid)
            o_ref[t * QB:(t + 1) * QB, :] = _unstack_heads(_dot(e.astype(BF16), v2, NN) * inv).astype(BF16)

    return _call(
        body, comm, 1, (qkvp, wb), name=name, grid=(N_HEADS // 2, T // (QB * ATTN_G)),
        in_specs=[pl.BlockSpec((3, FRONT + T, 2 * HEAD_DIM), lambda hp, b: (0, 0, hp)),
                  pl.BlockSpec((2, QB, KW), lambda hp, b: (hp, 0, 0))],
        out_specs=pl.BlockSpec((QB * ATTN_G, 2 * HEAD_DIM), lambda hp, b: (b, hp)),
        out_shape=_sds((T, D), BF16),
        compiler_params=_params(("parallel", "arbitrary"), VMEM_BIG))


def attn_bwd(qkvp, o, do, wb, name, comm=None):
    T = qkvp.shape[1] - FRONT
    nb = T // (QB * ATTN_G)

    def body(qkv_ref, o_ref, do_ref, w_ref, dqkv_ref, dw_ref, dk_acc, dv_acc):
        b = pl.program_id(1)

        @pl.when(b == 0)
        def _():
            dk_acc[...] = jnp.zeros_like(dk_acc)
            dv_acc[...] = jnp.zeros_like(dv_acc)
            dw_ref[...] = jnp.zeros_like(dw_ref)
            dqkv_ref[0, 0:FRONT, :] = jnp.zeros((FRONT, 2 * HEAD_DIM), BF16)

        dws = None
        for t in range(ATTN_G):
            r0, qs, k2, v2, kvalid = _attn_block(qkv_ref, b * ATTN_G + t)
            qst = _stack_heads(qs)
            e, inv = _attn_exp(qst, k2, w_ref, kvalid)
            do2 = do_ref[t * QB:(t + 1) * QB, :]
            dost = _stack_heads(do2)
            prod = _stack_heads(do2.astype(F32) * o_ref[t * QB:(t + 1) * QB, :].astype(F32))
            delta = jnp.sum(prod, axis=-1, keepdims=True)
            ds = e * ((_dot(dost, v2, NT) - delta) * inv)
            dws = ds if dws is None else dws + ds
            ds16 = ds.astype(BF16)
            dq = _unstack_heads(_dot(ds16, k2, NN)) * SCALE
            dqkv_ref[0, pl.ds(r0 + FRONT, QB), :] = dq.astype(BF16)
            dk_acc[pl.ds(r0 + (FRONT - PAD), KW), :] += _dot(ds16, qst, TN)
            dv_acc[pl.ds(r0 + (FRONT - PAD), KW), :] += _dot(e.astype(BF16), (dost.astype(F32) * inv).astype(BF16), TN)
        dw_ref[...] += dws.reshape(2, QB, KW)

        @pl.when(b == nb - 1)
        def _():
            dqkv_ref[1] = dk_acc[...].astype(BF16)
            dqkv_ref[2] = dv_acc[...].astype(BF16)

    slab = pl.BlockSpec((3, FRONT + T, 2 * HEAD_DIM), lambda hp, b: (0, 0, hp))
    wspec = pl.BlockSpec((2, QB, KW), lambda hp, b: (hp, 0, 0))
    rows = pl.BlockSpec((QB * ATTN_G, 2 * HEAD_DIM), lambda hp, b: (b, hp))
    return _call(
        body, comm, 2, (qkvp, o, do, wb), name=name, grid=(N_HEADS // 2, nb),
        in_specs=[slab, rows, rows, wspec],
        out_specs=[slab, wspec],
        out_shape=[_sds((3, FRONT + T, D), BF16), _sds((N_HEADS, QB, KW), F32)],
        scratch_shapes=[pltpu.VMEM((FRONT + T, 2 * HEAD_DIM), F32), pltpu.VMEM((FRONT + T, 2 * HEAD_DIM), F32)],
        compiler_params=_params(("parallel", "arbitrary"), VMEM_BIG))


def proj_qkv(hn, w, l, name, tm=512, comm=None):
    T = hn.shape[0]
    pb = FRONT // tm

    def body(a_ref, b_ref, o_ref):
        i = pl.program_id(1)

        @pl.when(i < pb)
        def _():
            o_ref[...] = jnp.zeros_like(o_ref)

        @pl.when(i >= pb)
        def _():
            o_ref[...] = _dot(a_ref[...], b_ref[...], NN).astype(BF16)

    return _call(
        body, comm, 1, (hn, w), name=name, grid=(3, pb + T // tm),
        in_specs=[pl.BlockSpec((tm, D), lambda p, i: (jnp.maximum(i - pb, 0), 0)),
                  pl.BlockSpec((None, D, D), lambda p, i: (l, 0, p))],
        out_specs=pl.BlockSpec((None, tm, D), lambda p, i: (p, i, 0)),
        out_shape=_sds((3, FRONT + T, D), BF16),
        compiler_params=_params(("parallel", "parallel"), VMEM_BIG))


def ffn_up(hn, wg, wu, l, name, tm=1024, comm=None):
    T = hn.shape[0]

    def body(a_ref, wg_ref, wu_ref, g_ref, u_ref, h_ref):
        a = a_ref[...]
        g = _dot(a, wg_ref[...], NT)
        u = _dot(a, wu_ref[...], NT)
        g_ref[...] = g.astype(BF16)
        u_ref[...] = u.astype(BF16)
        h_ref[...] = ((g * _sigmoid(g)) * u).astype(BF16)

    wspec = pl.BlockSpec((None, None, FS, D), lambda s, i: (l, s, 0, 0))
    ospec = pl.BlockSpec((None, tm, FS), lambda s, i: (s, i, 0))
    return _call(
        body, comm, 3, (hn, wg, wu), name=name, grid=(N_CHIPS, T // tm),
        in_specs=[pl.BlockSpec((tm, D), lambda s, i: (i, 0)), wspec, wspec],
        out_specs=[ospec, ospec, ospec],
        out_shape=[_sds((N_CHIPS, T, FS), BF16)] * 3,
        compiler_params=_params(("parallel", "parallel"), VMEM_BIG))


def ffn_bwd_dh(dxb, wd, g, u, l, name, tm=1024, comm=None):
    T = dxb.shape[0]

    def body(a_ref, wd_ref, g_ref, u_ref, dg_ref, du_ref):
        dh = _dot(a_ref[...], wd_ref[...], NT)
        gf = g_ref[...].astype(F32)
        uf = u_ref[...].astype(F32)
        s = _sigmoid(gf)
        dg_ref[...] = (dh * uf * (s * (1.0 + gf * (1.0 - s)))).astype(BF16)
        du_ref[...] = (dh * (gf * s)).astype(BF16)

    aspec = pl.BlockSpec((None, tm, FS), lambda s, i: (s, i, 0))
    return _call(
        body, comm, 2, (dxb, wd, g, u), name=name, grid=(N_CHIPS, T // tm),
        in_specs=[pl.BlockSpec((tm, D), lambda s, i: (i, 0)),
                  pl.BlockSpec((None, None, FS, D), lambda s, i: (l, s, 0, 0)), aspec, aspec],
        out_specs=[aspec, aspec],
        out_shape=[_sds((N_CHIPS, T, FS), BF16)] * 2,
        compiler_params=_params(("parallel", "parallel"), VMEM_BIG))


def ffn_bwd_dhn(dg, du, wg, wu, l, name, tm=512, comm=None):
    T = dg.shape[1]

    def body(dg_ref, du_ref, wg_ref, wu_ref, o_ref):
        d = None
        for s in range(N_CHIPS):
            t = _dot(dg_ref[s], wg_ref[s], NN) + _dot(du_ref[s], wu_ref[s], NN)
            d = t if d is None else d + t
        o_ref[...] = d

    aspec = pl.BlockSpec((N_CHIPS, tm, FS), lambda i: (0, i, 0))
    wspec = pl.BlockSpec((None, N_CHIPS, FS, D), lambda i: (l, 0, 0, 0))
    return _call(
        body, comm, 1, (dg, du, wg, wu), name=name, grid=(T // tm,), in_specs=[aspec, aspec, wspec, wspec],
        out_specs=pl.BlockSpec((tm, D), lambda i: (i, 0)), out_shape=_sds((T, D), F32),
        compiler_params=_params(("parallel",), VMEM_BIG))


def ffn_down_res(h, wd, res, l, name, tm=512, comm=None):
    T = h.shape[1]

    def body(h_ref, wd_ref, r_ref, o_ref):
        d = r_ref[...]
        for s in range(N_CHIPS):
            d = d + _dot(h_ref[s], wd_ref[s], NN)
        o_ref[...] = d

    row = pl.BlockSpec((tm, D), lambda i: (i, 0))
    return _call(
        body, comm, 1, (h, wd, res), name=name, grid=(T // tm,),
        in_specs=[pl.BlockSpec((N_CHIPS, tm, FS), lambda i: (0, i, 0)),
                  pl.BlockSpec((None, N_CHIPS, FS, D), lambda i: (l, 0, 0, 0)), row],
        out_specs=row, out_shape=_sds((T, D), F32),
        compiler_params=_params(("parallel",), VMEM_BIG))


def qkv_bwd(dqkvp, w, l, name, tm=512):
    T = dqkvp.shape[1] - FRONT

    def body(a_ref, w_ref, o_ref):
        d = None
        for p in range(3):
            t = _dot(a_ref[p], w_ref[:, p * D:(p + 1) * D], NT)
            d = t if d is None else d + t
        o_ref[...] = d

    return _pallas(
        body, name=name, grid=(T // tm,),
        in_specs=[pl.BlockSpec((3, tm, D), lambda i: (0, i + FRONT // tm, 0)),
                  pl.BlockSpec((None, D, 3 * D), lambda i: (l, 0, 0))],
        out_specs=pl.BlockSpec((tm, D), lambda i: (i, 0)), out_shape=_sds((T, D), F32),
        compiler_params=_params(("parallel",), VMEM_BIG))(dqkvp, w)


def adamw(w, g, m, v, name):
    L, R, C = w.shape

    def body(w_ref, g_ref, m_ref, v_ref, d_ref, nm_ref, nv_ref):
        gf = g_ref[...]
        nm = ADAM_B1 * m_ref[...] + (1.0 - ADAM_B1) * gf
        nv = ADAM_B2 * v_ref[...] + (1.0 - ADAM_B2) * (gf * gf)
        m_hat = nm / (1.0 - ADAM_B1 ** ADAM_STEP)
        v_hat = nv / (1.0 - ADAM_B2 ** ADAM_STEP)
        d_ref[...] = -ADAM_LR * (m_hat / (jnp.sqrt(v_hat) + ADAM_EPS) + ADAM_WD * w_ref[...])
        nm_ref[...] = nm
        nv_ref[...] = nv

    tr = R // 4 if R % 32 == 0 else R
    spec = pl.BlockSpec((None, tr, C), lambda l, r: (l, r, 0))
    return _pallas(body, name=name, grid=(L, R // tr), in_specs=[spec] * 4, out_specs=[spec] * 3,
                   out_shape=[_sds((L, R, C), F32)] * 3,
                   compiler_params=_params(("parallel", "parallel")))(w, g, m, v)


def _coords():
    return lax.axis_index("x"), lax.axis_index("y"), lax.axis_index("c")


def _other_chips(x, y):
    out = []
    for fx, fy in ((1, 0), (0, 1), (1, 1)):
        px = (1 - x) if fx else x
        py = (1 - y) if fy else y
        out.append((px, py))
    return out


def _flip_index(s, j):
    sx, sy = s // 2, s % 2
    fx, fy = ((1, 0), (0, 1), (1, 1))[j]
    return 2 * (sx ^ fx) + (sy ^ fy)


def _for_my_chip(sme, fn):
    for s in range(N_CHIPS):
        pl.when(sme == s)(functools.partial(fn, s))


ANY = pl.BlockSpec(memory_space=pl.ANY)

GATHER_KIND = {"a_w_in": "col", "b_w_qkv": "col", "a_w_out": "row", "b_w_out": "row",
               "ffn_w_gate": "row", "ffn_w_up": "row", "ffn_w_down": "row"}
BIG = tuple(GATHER_KIND)


def _gathered_shape(kind, shape):
    L, R, C = shape
    return (L, R, N_CHIPS * C) if kind == "col" else (L, N_CHIPS, R, C)


def _shard_rows(ref, kind, s, r0, rn, C):
    if kind == "col":
        return ref.at[:, pl.ds(r0, rn), s * C:(s + 1) * C]
    return ref.at[:, s, pl.ds(r0, rn), :]


def gather_stage1(items):
    n = len(items)
    dims = [it[0].shape[1:] for it in items]

    def copies(ins, outs, sems, s, with_landed=True):
        lsem, ssem, rsem = sems
        x, y, c = _coords()
        chips = _other_chips(x, y)
        local, send, landed = [], [], []
        for t, (_, li, kind) in enumerate(items):
            R, C = dims[t]
            r0 = pl.multiple_of(c * (R // 2), 8)
            local.append(pltpu.make_async_copy(ins[t].at[pl.ds(li, 1)], _shard_rows(outs[t], kind, s, 0, R, C),
                                               lsem.at[t]))
            for j in range(3):
                pair = dict(send_sem=ssem.at[3 * t + j], recv_sem=rsem.at[3 * t + j],
                            device_id=(chips[j][0], chips[j][1], c), device_id_type=MESH)
                send.append(pltpu.make_async_remote_copy(
                    src_ref=ins[t].at[pl.ds(li, 1), pl.ds(r0, R // 2), :],
                    dst_ref=_shard_rows(outs[t], kind, s, r0, R // 2, C), **pair))
                if with_landed:
                    got = _shard_rows(outs[t], kind, _flip_index(s, j), r0, R // 2, C)
                    landed.append(pltpu.make_async_remote_copy(src_ref=got, dst_ref=got, **pair))
        return local, send, landed

    def start(ins, outs, sems):
        def run(s):
            local, send, _ = copies(ins, outs, sems, s, with_landed=False)
            for cp in local + send:
                cp.start()
        x, y, _ = _coords()
        _for_my_chip(2 * x + y, run)

    def wait(ins, outs, sems):
        def run(s):
            local, send, landed = copies(ins, outs, sems, s)
            for cp in landed:
                cp.wait_recv()
            for cp in send:
                cp.wait_send()
            for cp in local:
                cp.wait()
        x, y, _ = _coords()
        _for_my_chip(2 * x + y, run)

    out_shapes = [_sds(_gathered_shape(kind, (1,) + tuple(dims[t])), BF16) for t, (_, _, kind) in enumerate(items)]
    sems = [pltpu.SemaphoreType.DMA((n,)), pltpu.SemaphoreType.DMA((3 * n,)), pltpu.SemaphoreType.DMA((3 * n,))]
    return Comm([it[0] for it in items], out_shapes, sems, start, wait)


def gather_stage2(items, gathered):
    n = len(items)
    dims = [it[0].shape[1:] for it in items]

    def copies(outs, sems, s, with_landed=True):
        ssem, rsem = sems
        x, y, c = _coords()
        send, landed = [], []
        for t, (_, _, kind) in enumerate(items):
            R, C = dims[t]
            for j in range(3):
                pair = dict(send_sem=ssem.at[3 * t + j], recv_sem=rsem.at[3 * t + j],
                            device_id=(x, y, 1 - c), device_id_type=MESH)
                mine = _shard_rows(outs[t], kind, _flip_index(s, j), pl.multiple_of(c * (R // 2), 8), R // 2, C)
                send.append(pltpu.make_async_remote_copy(src_ref=mine, dst_ref=mine, **pair))
                if with_landed:
                    other = _shard_rows(outs[t], kind, _flip_index(s, j), pl.multiple_of((1 - c) * (R // 2), 8),
                                        R // 2, C)
                    landed.append(pltpu.make_async_remote_copy(src_ref=other, dst_ref=other, **pair))
        return send, landed

    def start(ins, outs, sems):
        def run(s):
            for cp in copies(outs, sems, s, with_landed=False)[0]:
                cp.start()
        x, y, _ = _coords()
        _for_my_chip(2 * x + y, run)

    def wait(ins, outs, sems):
        def run(s):
            send, landed = copies(outs, sems, s)
            for cp in landed:
                cp.wait_recv()
            for cp in send:
                cp.wait_send()
        x, y, _ = _coords()
        _for_my_chip(2 * x + y, run)

    out_shapes = [_sds(g.shape, BF16) for g in gathered]
    sems = [pltpu.SemaphoreType.DMA((3 * n,)), pltpu.SemaphoreType.DMA((3 * n,))]
    return Comm(gathered, out_shapes, sems, start, wait, aliases={t: t for t in range(n)})


def gather_both(items):
    s1 = gather_stage1(items)
    s2 = gather_stage2(items, s1.out_shapes)
    n1 = len(s1.sems)

    def wait(ins, outs, sems):
        s1.wait(ins, outs, sems[:n1])
        s2.start((), outs, sems[n1:])
        s2.wait((), outs, sems[n1:])

    return Comm(s1.ins, s1.out_shapes, s1.sems + s2.sems, lambda ins, outs, sems: s1.start(ins, outs, sems[:n1]), wait)


def _half_shape(kind, R, C):
    return (R // 2, N_CHIPS * C) if kind == "col" else (N_CHIPS, R // 2, C)


def exchange_halves(grads, metas):
    n = len(grads)

    def copies(ins, outs, sems):
        ssem, rsem = sems
        x, y, c = _coords()
        out = []
        for t, (kind, R, C) in enumerate(metas):
            r0 = pl.multiple_of((1 - c) * (R // 2), 8)
            src = ins[t].at[pl.ds(r0, R // 2), :] if kind == "col" else ins[t].at[:, pl.ds(r0, R // 2), :]
            out.append(pltpu.make_async_remote_copy(
                src_ref=src, dst_ref=outs[t], send_sem=ssem.at[t], recv_sem=rsem.at[t],
                device_id=(x, y, 1 - c), device_id_type=MESH))
        return out

    def start(ins, outs, sems):
        for cp in copies(ins, outs, sems):
            cp.start()

    def wait(ins, outs, sems):
        for cp in copies(ins, outs, sems):
            cp.wait()

    return Comm(grads, [_sds(_half_shape(*m), F32) for m in metas], [pltpu.SemaphoreType.DMA((n,))] * 2, start, wait)


def pair_sum(me, g, sib, meta, name):
    kind, R, C = meta
    h = R // 2

    def body(me_ref, g_ref, sib_ref, p16_ref, own_ref):
        s = pl.program_id(0)
        v = g_ref[...] + sib_ref[...]
        p16_ref[...] = v.astype(BF16)

        @pl.when(s == me_ref[1])
        def _():
            own_ref[...] = v

    if kind == "col":
        gspec = pl.BlockSpec((h, C), lambda s, me_ref: (me_ref[0], s))
        sspec = pl.BlockSpec((h, C), lambda s, me_ref: (0, s))
    else:
        gspec = pl.BlockSpec((None, h, C), lambda s, me_ref: (s, me_ref[0], 0))
        sspec = pl.BlockSpec((None, h, C), lambda s, me_ref: (s, 0, 0))
    grid_spec = pltpu.PrefetchScalarGridSpec(
        num_scalar_prefetch=1, grid=(N_CHIPS,), in_specs=[gspec, sspec],
        out_specs=[sspec, pl.BlockSpec((h, C), lambda s, me_ref: (0, 0))])
    return _pallas(body, name=name, grid_spec=grid_spec,
                   out_shape=[_sds(_half_shape(*meta), BF16), _sds((h, C), F32)],
                   compiler_params=_params(("arbitrary",), VMEM_BIG))(me, g, sib)


def scatter_partials(p16s, metas):
    n = len(p16s)

    def copies(ins, outs, sems, s):
        ssem, rsem = sems
        x, y, c = _coords()
        chips = _other_chips(x, y)
        out = []
        for t, (kind, R, C) in enumerate(metas):
            for j in range(3):
                sj = _flip_index(s, j)
                src = ins[t].at[:, sj * C:(sj + 1) * C] if kind == "col" else ins[t].at[sj]
                out.append(pltpu.make_async_remote_copy(
                    src_ref=src, dst_ref=outs[t].at[j], send_sem=ssem.at[3 * t + j], recv_sem=rsem.at[3 * t + j],
                    device_id=(chips[j][0], chips[j][1], c), device_id_type=MESH))
        return out

    def start(ins, outs, sems):
        def run(s):
            for cp in copies(ins, outs, sems, s):
                cp.start()
        x, y, _ = _coords()
        _for_my_chip(2 * x + y, run)

    def wait(ins, outs, sems):
        def run(s):
            for cp in copies(ins, outs, sems, s):
                cp.wait()
        x, y, _ = _coords()
        _for_my_chip(2 * x + y, run)

    return Comm(p16s, [_sds((3, R // 2, C), BF16) for (_, R, C) in metas],
                [pltpu.SemaphoreType.DMA((3 * n,))] * 2, start, wait)


def final_sum(me, own, q, buf, l, meta, name):
    _, R, C = meta
    h = R // 2

    def body(me_ref, own_ref, q_ref, buf_ref, o_ref):
        del buf_ref
        o_ref[...] = ((own_ref[...] + q_ref[0].astype(F32)) + q_ref[1].astype(F32)) + q_ref[2].astype(F32)

    grid_spec = pltpu.PrefetchScalarGridSpec(
        num_scalar_prefetch=1, grid=(1,),
        in_specs=[pl.BlockSpec((h, C), lambda i, me_ref: (0, 0)),
                  pl.BlockSpec((3, h, C), lambda i, me_ref: (0, 0, 0)), ANY],
        out_specs=pl.BlockSpec((None, h, C), lambda i, me_ref: (l, me_ref[0], 0)))
    return _pallas(body, name=name, grid_spec=grid_spec, out_shape=_sds(buf.shape, F32),
                   input_output_aliases={3: 0},
                   compiler_params=_params(("arbitrary",), VMEM_BIG))(me, own, q, buf)


def share_final(bufs):
    n = len(bufs)

    def body(*refs):
        ins, outs = refs[:n], refs[n:2 * n]
        ssem, rsem = refs[2 * n:]
        del ins
        x, y, c = _coords()
        copies = []
        for t in range(n):
            R = bufs[t].shape[1]
            r0 = pl.multiple_of(c * (R // 2), 8)
            blk = outs[t].at[:, pl.ds(r0, R // 2), :]
            copies.append(pltpu.make_async_remote_copy(
                src_ref=blk, dst_ref=blk, send_sem=ssem.at[t], recv_sem=rsem.at[t],
                device_id=(x, y, 1 - c), device_id_type=MESH))
        for cp in copies:
            cp.start()
        for t in range(n):
            R = bufs[t].shape[1]
            r1 = pl.multiple_of((1 - c) * (R // 2), 8)
            other = outs[t].at[:, pl.ds(r1, R // 2), :]
            pltpu.make_async_remote_copy(
                src_ref=other, dst_ref=other, send_sem=ssem.at[t], recv_sem=rsem.at[t],
                device_id=(x, y, 1 - c), device_id_type=MESH).wait_recv()
        for cp in copies:
            cp.wait_send()

    out_shape = [_sds(b.shape, F32) for b in bufs]
    return _pallas(body, name="share_final", in_specs=[ANY] * n, out_specs=[ANY] * n, out_shape=out_shape,
                   input_output_aliases={t: t for t in range(n)},
                   scratch_shapes=[pltpu.SemaphoreType.DMA((n,))] * 2,
                   compiler_params=pltpu.CompilerParams(has_side_effects=True))(*bufs)


def allreduce_small(part):
    rows = part.shape[0]

    def body(p_ref, o_ref, buf, ssem, rsem, lsem):
        x, y, c = _coords()
        me = 4 * x + 2 * y + c
        mine = pltpu.make_async_copy(p_ref, buf.at[me], lsem)
        mine.start()
        copies = []
        for r in range(1, 8):
            fx, fy, fc = (r >> 2) & 1, (r >> 1) & 1, r & 1
            peer = ((1 - x) if fx else x, (1 - y) if fy else y, (1 - c) if fc else c)
            copies.append(pltpu.make_async_remote_copy(
                src_ref=p_ref, dst_ref=buf.at[me], send_sem=ssem.at[r - 1], recv_sem=rsem.at[r - 1],
                device_id=peer, device_id_type=MESH))
        for cp in copies:
            cp.start()
        for r in range(1, 8):
            fx, fy, fc = (r >> 2) & 1, (r >> 1) & 1, r & 1
            src = 4 * ((1 - x) if fx else x) + 2 * ((1 - y) if fy else y) + ((1 - c) if fc else c)
            pltpu.make_async_remote_copy(
                src_ref=buf.at[src], dst_ref=buf.at[src], send_sem=ssem.at[r - 1], recv_sem=rsem.at[r - 1],
                device_id=(x, y, c), device_id_type=MESH).wait_recv()
        for cp in copies:
            cp.wait_send()
        mine.wait()
        acc = buf[0]
        for d in range(1, 8):
            acc = acc + buf[d]
        o_ref[...] = acc

    return _pallas(body, name="allreduce_small",
                   in_specs=[pl.BlockSpec(memory_space=pltpu.VMEM)], out_specs=pl.BlockSpec(memory_space=pltpu.VMEM),
                   out_shape=_sds((rows, 128), F32),
                   scratch_shapes=[pltpu.VMEM((8, rows, 128), F32), pltpu.SemaphoreType.DMA((7,)),
                                   pltpu.SemaphoreType.DMA((7,)), pltpu.SemaphoreType.DMA],
                   compiler_params=pltpu.CompilerParams(has_side_effects=True, vmem_limit_bytes=VMEM_BIG))(part)


def _rows128(a):
    flat = a.reshape(-1)
    rows = -(-flat.shape[0] // 128)
    rows8 = -(-rows // 8) * 8
    flat = jnp.pad(flat, (0, rows8 * 128 - flat.shape[0]))
    return flat.reshape(rows8, 128)


def kernel(x, norm_mix_g, norm_ffn_g, final_g, a_w_in, a_v_gain, a_w_s, a_b_s, a_w_out, b_w_qkv, b_rel_bias, b_w_out, ffn_w_gate, ffn_w_up, ffn_w_down, loss_target, m_norm_mix_g, m_norm_ffn_g, m_final_g, m_a_w_in, m_a_v_gain, m_a_w_s, m_a_b_s, m_a_w_out, m_b_w_qkv, m_b_rel_bias, m_b_w_out, m_ffn_w_gate, m_ffn_w_up, m_ffn_w_down, v_norm_mix_g, v_norm_ffn_g, v_final_g, v_a_w_in, v_a_v_gain, v_a_w_s, v_a_b_s, v_a_w_out, v_b_w_qkv, v_b_rel_bias, v_b_w_out, v_ffn_w_gate, v_ffn_w_up, v_ffn_w_down):
    T = x.shape[1]
    weights = dict(norm_mix_g=norm_mix_g, norm_ffn_g=norm_ffn_g, final_g=final_g, a_w_in=a_w_in, a_v_gain=a_v_gain,
                   a_w_s=a_w_s, a_b_s=a_b_s, a_w_out=a_w_out, b_w_qkv=b_w_qkv, b_rel_bias=b_rel_bias,
                   b_w_out=b_w_out, ffn_w_gate=ffn_w_gate, ffn_w_up=ffn_w_up, ffn_w_down=ffn_w_down)
    mom_m = dict(norm_mix_g=m_norm_mix_g, norm_ffn_g=m_norm_ffn_g, final_g=m_final_g, a_w_in=m_a_w_in,
                 a_v_gain=m_a_v_gain, a_w_s=m_a_w_s, a_b_s=m_a_b_s, a_w_out=m_a_w_out, b_w_qkv=m_b_w_qkv,
                 b_rel_bias=m_b_rel_bias, b_w_out=m_b_w_out, ffn_w_gate=m_ffn_w_gate, ffn_w_up=m_ffn_w_up,
                 ffn_w_down=m_ffn_w_down)
    mom_v = dict(norm_mix_g=v_norm_mix_g, norm_ffn_g=v_norm_ffn_g, final_g=v_final_g, a_w_in=v_a_w_in,
                 a_v_gain=v_a_v_gain, a_w_s=v_a_w_s, a_b_s=v_a_b_s, a_w_out=v_a_w_out, b_w_qkv=v_b_w_qkv,
                 b_rel_bias=v_b_rel_bias, b_w_out=v_b_w_out, ffn_w_gate=v_ffn_w_gate, ffn_w_up=v_ffn_w_up,
                 ffn_w_down=v_ffn_w_down)
    order = list(weights)
    transposed = ("ffn_w_gate", "ffn_w_up")
    for k in transposed:
        weights[k], mom_m[k], mom_v[k] = (jnp.swapaxes(a, 1, 2) for a in (weights[k], mom_m[k], mom_v[k]))

    xi, yi, ci = _coords()
    me = jnp.stack([ci, 2 * xi + yi]).astype(jnp.int32)

    shard16 = {k: cast_bf16(weights[k], "cast_" + k) for k in BIG}

    def layer_tensors(i):
        mix = ("a_w_in", "a_w_out") if i % 2 == 0 else ("b_w_qkv", "b_w_out")
        return [(k, i // 2) for k in mix] + [(k, i) for k in ("ffn_w_gate", "ffn_w_up", "ffn_w_down")]

    def gather_items(keys):
        return [(shard16[k], l, GATHER_KIND[k]) for k, l in keys]

    def grad_metas(keys):
        return [(GATHER_KIND[k],) + tuple(weights[k].shape[1:]) for k, _ in keys]

    FFN = ("ffn_w_gate", "ffn_w_up", "ffn_w_down")
    k0a = [("a_w_out", 0), ("ffn_w_gate", 0)]
    k0b = [("ffn_w_up", 0), ("ffn_w_down", 0)]
    k1a = [("b_w_qkv", 0), ("b_w_out", 0), ("ffn_w_gate", 1)]
    k1b = [("ffn_w_up", 1), ("ffn_w_down", 1)]
    k3a = [("b_w_qkv", 1), ("b_w_out", 1), ("ffn_w_gate", 3)]
    k3b = [("ffn_w_up", 3), ("ffn_w_down", 3)]
    plans = {
        "a_in_l0": [("g1", k0a)], "sgu_fwd_l0": [("g2", k0a), ("g1", k0b)], "a_out_l0": [("g2", k0b)],
        "ffn_up_l0": [("g1", k1a)], "ffn_down_l0": [("g2", k1a), ("g1", k1b)], "b_qkv_l1": [("g2", k1b)],
        "attn_fwd_l1": [("g1", layer_tensors(2))], "b_out_l1": [("g2", layer_tensors(2))],
        "ffn_up_l1": [("g1", k3a)], "ffn_down_l1": [("g2", k3a)],
        "a_in_l2": [("g1", k3b)], "sgu_fwd_l2": [("g2", k3b)],
        "ffn_bwd_dh_l2": [("ex", layer_tensors(3))], "sgu_bwd_l2": [("sc", layer_tensors(3))],
        "ffn_bwd_dh_l1": [("ex", layer_tensors(2))], "attn_bwd_l1": [("sc", layer_tensors(2))],
        "ffn_bwd_dh_l0": [("ex", layer_tensors(1))], "ffn_bwd_dhn_l0": [("sc", k1a)],
        "a_out_bwd_l0": [("ex", [(k, 0) for k in FFN])],
        "sgu_bwd_l0": [("sc", k1b), ("ex", [("a_w_out", 0)])],
        "a_in_bwd_l0": [("sc", [("ffn_w_gate", 0), ("ffn_w_up", 0)])],
        "dw_in_l0": [("sc", [("ffn_w_down", 0), ("a_w_out", 0)])],
    }
    part16, full16 = {}, {}
    sib, p16, own_parts, recv_parts = {}, {}, {}, {}

    def make_comm(kind, keys):
        if kind == "g1":
            return gather_stage1(gather_items(keys)), lambda outs: part16.update(zip(keys, outs))
        if kind == "g2":
            return (gather_stage2(gather_items(keys), [part16[kl] for kl in keys]),
                    lambda outs: full16.update(zip(keys, outs)))
        if kind == "ex":
            return (exchange_halves([big_grads[k][l] for k, l in keys], grad_metas(keys)),
                    lambda outs: sib.update(zip(keys, outs)))
        for kl, m_ in zip(keys, grad_metas(keys)):
            p16[kl], own_parts[kl] = pair_sum(me, big_grads[kl[0]][kl[1]], sib[kl], m_, "pair_sum_%s_l%d" % kl)
        return (scatter_partials([p16[kl] for kl in keys], grad_metas(keys)),
                lambda outs: recv_parts.update(zip(keys, outs)))

    def run(name, make):
        steps = plans.get(name)
        if not steps:
            return make(None)
        made = [make_comm(kind, keys) for kind, keys in steps]
        main, outs = make(combine([c for c, _ in made]))
        for c, done in made:
            done(outs[:len(c.out_shapes)])
            outs = outs[len(c.out_shapes):]
        return main

    def weight(k, l):
        w = full16[(k, l)]
        if k == "a_w_out":
            return w.reshape(1, GH, D)
        return w.reshape(1, D, D) if k == "b_w_out" else w

    full16[("a_w_in", 0)] = run_comm(gather_both(gather_items([("a_w_in", 0)])), "gather_first")[0]

    row512 = lambda: pl.BlockSpec((512, D), lambda i, j: (i, 0))

    xcur = x.reshape(T, D)
    saved = []
    for i in range(DEPTH):
        j = i // 2
        tag = "_l%d" % i
        st = {"x_in": xcur}
        hn = rms_fwd(xcur, norm_mix_g[i][None], "rms_mix" + tag)
        st["hn"] = hn
        if i % 2 == 0:
            pre = run("a_in" + tag, lambda comm: matmul(
                "a_in" + tag, NN, hn, pl.BlockSpec((1024, D), lambda i_, j_: (i_, 0)),
                weight("a_w_in", j), pl.BlockSpec((None, D, 1024), lambda i_, j_: (0, 0, j_)),
                _sds((T, 2 * GH), BF16), pl.BlockSpec((1024, 1024), lambda i_, j_: (i_, j_)),
                (T // 1024, 4), comm=comm))
            y = run("sgu_fwd" + tag, lambda comm: sgu_fwd(
                pre, a_v_gain[j][None], a_w_s[j], a_b_s[j][:, :, None], "sgu_fwd" + tag, comm=comm))
            xmid = run("a_out" + tag, lambda comm: matmul(
                "a_out" + tag, NN, y, pl.BlockSpec((512, GH), lambda i_, j_: (i_, 0)),
                weight("a_w_out", j), pl.BlockSpec((None, GH, D), lambda i_, j_: (0, 0, 0)),
                _sds((T, D), F32), row512(), (T // 512, 1), res=xcur, res_spec=row512(), comm=comm))
            st.update(pre=pre, y=y)
        else:
            qkvp = run("b_qkv" + tag, lambda comm: proj_qkv(hn, weight("b_w_qkv", j), 0, "b_qkv" + tag, comm=comm))
            wb = jnp.transpose(bias_build(b_rel_bias[j], "bias_build" + tag), (1, 0, 2))
            o = run("attn_fwd" + tag, lambda comm: attn_fwd(qkvp, wb, "attn_fwd" + tag, comm=comm))
            xmid = run("b_out" + tag, lambda comm: matmul(
                "b_out" + tag, NN, o, pl.BlockSpec((512, D), lambda i_, j_: (i_, 0)),
                weight("b_w_out", j), pl.BlockSpec((None, D, D), lambda i_, j_: (0, 0, 0)),
                _sds((T, D), F32), row512(), (T // 512, 1), res=xcur, res_spec=row512(), comm=comm))
            st.update(qkvp=qkvp, wb=wb, o=o)
        hn2 = rms_fwd(xmid, norm_ffn_g[i][None], "rms_ffn" + tag)
        g, u, h = run("ffn_up" + tag, lambda comm: ffn_up(
            hn2, weight("ffn_w_gate", i), weight("ffn_w_up", i), 0, "ffn_up" + tag, comm=comm))
        xcur = run("ffn_down" + tag, lambda comm: ffn_down_res(
            h, weight("ffn_w_down", i), xmid, 0, "ffn_down" + tag, comm=comm))
        st.update(x_mid=xmid, hn2=hn2, g=g, u=u, h=h)
        saved.append(st)

    loss_part, dx, dxb, d_final = final_loss(xcur, final_g[None], loss_target.reshape(T, D), "final_loss")

    tk = min(2048, T)
    big_grads = {k: [None] * weights[k].shape[0] for k in BIG}
    small = {"norm_mix_g": [None] * DEPTH, "norm_ffn_g": [None] * DEPTH, "a_v_gain": [None] * 2,
             "a_w_s": [None] * 2, "a_b_s": [None] * 2, "b_rel_bias": [None] * 2}
    tok = lambda width: pl.BlockSpec((tk, width), lambda j_, k_: (k_, 0))
    part = lambda: pl.BlockSpec((None, tk, FS), lambda j_, k_: (j_, k_, 0))
    for i in reversed(range(DEPTH)):
        j = i // 2
        tag = "_l%d" % i
        st = saved[i]
        dg, du = run("ffn_bwd_dh" + tag, lambda comm: ffn_bwd_dh(
            dxb, weight("ffn_w_down", i), st["g"], st["u"], 0, "ffn_bwd_dh" + tag, comm=comm))
        big_grads["ffn_w_down"][i] = wgrad(
            "dw_down" + tag, st["h"], part(), dxb, tok(D), _sds((N_CHIPS, FS, D), F32),
            pl.BlockSpec((None, FS, D), lambda j_, k_: (j_, 0, 0)), N_CHIPS, T, tk)
        dhn2 = run("ffn_bwd_dhn" + tag, lambda comm: ffn_bwd_dhn(
            dg, du, weight("ffn_w_gate", i), weight("ffn_w_up", i), 0, "ffn_bwd_dhn" + tag, comm=comm))
        for nm, dz in (("ffn_w_gate", dg), ("ffn_w_up", du)):
            big_grads[nm][i] = wgrad(
                "d" + nm + tag, dz, part(), st["hn2"], tok(D), _sds((N_CHIPS, FS, D), F32),
                pl.BlockSpec((None, FS, D), lambda j_, k_: (j_, 0, 0)), N_CHIPS, T, tk)
        dx, dxb, dgn = rms_bwd(st["x_mid"], dhn2, norm_ffn_g[i][None], dx, "rms_ffn_bwd" + tag)
        small["norm_ffn_g"][i] = dgn
        if i % 2 == 0:
            dy = run("a_out_bwd" + tag, lambda comm: matmul(
                "a_out_bwd" + tag, NT, dxb, pl.BlockSpec((1024, D), lambda i_, j_: (i_, 0)),
                weight("a_w_out", j), pl.BlockSpec((None, 1024, D), lambda i_, j_: (0, j_, 0)),
                _sds((T, GH), BF16), pl.BlockSpec((1024, 1024), lambda i_, j_: (i_, j_)), (T // 1024, 2), comm=comm))
            big_grads["a_w_out"][j] = wgrad(
                "dw_aout" + tag, st["y"], pl.BlockSpec((tk, 1024), lambda j_, k_: (k_, j_)), dxb, tok(D),
                _sds((GH, D), F32), pl.BlockSpec((1024, D), lambda j_, k_: (j_, 0)), 2, T, tk
            ).reshape(N_CHIPS, GH // N_CHIPS, D)
            dpre, d_ws, d_bs, d_gain = run("sgu_bwd" + tag, lambda comm: sgu_bwd(
                st["pre"], dy, a_v_gain[j][None], a_w_s[j], a_b_s[j][:, :, None], "sgu_bwd" + tag, comm=comm))
            small["a_w_s"][j], small["a_b_s"][j], small["a_v_gain"][j] = d_ws, d_bs, d_gain
            dhn = run("a_in_bwd" + tag, lambda comm: matmul(
                "a_in_bwd" + tag, NT, dpre, pl.BlockSpec((512, 2 * GH), lambda i_, j_: (i_, 0)),
                weight("a_w_in", j), pl.BlockSpec((None, D, 2 * GH), lambda i_, j_: (0, 0, 0)),
                _sds((T, D), F32), row512(), (T // 512, 1), comm=comm))
            big_grads["a_w_in"][j] = run("dw_in" + tag, lambda comm: wgrad(
                "dw_in" + tag, st["hn"], tok(D), dpre, pl.BlockSpec((tk, 1024), lambda j_, k_: (k_, j_)),
                _sds((D, 2 * GH), F32), pl.BlockSpec((D, 1024), lambda j_, k_: (0, j_)), 4, T, tk, comm=comm))
        else:
            do = matmul("b_out_bwd" + tag, NT, dxb, pl.BlockSpec((1024, D), lambda i_, j_: (i_, 0)),
                        weight("b_w_out", j), pl.BlockSpec((None, D, D), lambda i_, j_: (0, 0, 0)),
                        _sds((T, D), BF16), pl.BlockSpec((1024, D), lambda i_, j_: (i_, 0)), (T // 1024, 1))
            big_grads["b_w_out"][j] = wgrad(
                "dw_bout" + tag, st["o"], tok(D), dxb, tok(D),
                _sds((D, D), F32), pl.BlockSpec((D, D), lambda j_, k_: (0, 0)), 1, T, tk
            ).reshape(N_CHIPS, D // N_CHIPS, D)
            dqkvp, dwb = run("attn_bwd" + tag, lambda comm: attn_bwd(
                st["qkvp"], st["o"], do, st["wb"], "attn_bwd" + tag, comm=comm))
            small["b_rel_bias"][j] = bias_grad(jnp.transpose(dwb, (1, 0, 2)), "bias_grad" + tag)
            dhn = qkv_bwd(dqkvp, weight("b_w_qkv", j), 0, "b_qkv_bwd" + tag)
            big_grads["b_w_qkv"][j] = wgrad(
                "dw_qkv" + tag, st["hn"], tok(D), dqkvp,
                pl.BlockSpec((None, tk, D), lambda j_, k_: (j_, k_ + FRONT // tk, 0)),
                _sds((D, 3 * D), F32), pl.BlockSpec((D, D), lambda j_, k_: (0, j_)), 3, T, tk)
        dx, dxb, dgn = rms_bwd(st["x_in"], dhn, norm_mix_g[i][None], dx, "rms_mix_bwd" + tag)
        small["norm_mix_g"][i] = dgn

    small_grads = {
        "norm_mix_g": jnp.concatenate(small["norm_mix_g"], axis=0),
        "norm_ffn_g": jnp.concatenate(small["norm_ffn_g"], axis=0),
        "final_g": d_final.reshape(D),
        "a_v_gain": jnp.concatenate(small["a_v_gain"], axis=0),
        "a_w_s": jnp.stack(small["a_w_s"]),
        "a_b_s": jnp.stack(small["a_b_s"]).reshape(2, SGU_G, SGU_BLOCK),
        "b_rel_bias": jnp.stack(small["b_rel_bias"]),
    }
    small_names = list(small_grads)
    packed = [_rows128(small_grads[k]) for k in small_names] + [_rows128(loss_part[:, :1])]
    offs = [0]
    for p in packed:
        offs.append(offs[-1] + p.shape[0])
    reduced = allreduce_small(jnp.concatenate(packed, axis=0))
    grads = {}
    for t, k in enumerate(small_names):
        nelem = small_grads[k].size
        grads[k] = reduced[offs[t]:offs[t + 1]].reshape(-1)[:nelem].reshape(weights[k].shape)
    loss = reduced[offs[len(small_names)], 0]

    last = [("a_w_in", 0)]
    for kind, name in (("ex", "exchange_last"), ("sc", "scatter_last")):
        comm, done = make_comm(kind, last)
        done(run_comm(comm, name))
    bufs = {k: jnp.zeros(weights[k].shape, F32) for k in BIG}
    for i in range(DEPTH):
        for kl, m_ in zip(layer_tensors(i), grad_metas(layer_tensors(i))):
            bufs[kl[0]] = final_sum(me, own_parts[kl], recv_parts[kl], bufs[kl[0]], kl[1], m_,
                                    "final_sum_%s_l%d" % kl)
    shared = share_final([bufs[k] for k in BIG])
    for k, gfull in zip(BIG, shared):
        grads[k] = gfull

    delta, new_m, new_v = {}, {}, {}
    for k in order:
        shp = weights[k].shape
        if k in BIG:
            view = shp
        elif k == "a_w_s":
            view = (2, SGU_G * SGU_BLOCK, SGU_BLOCK)
        elif len(shp) == 1:
            view = (1, 1, shp[0])
        elif len(shp) == 2:
            view = (1,) + shp
        else:
            view = shp
        d_, m_, v_ = adamw(weights[k].reshape(view), grads[k].reshape(view), mom_m[k].reshape(view),
                           mom_v[k].reshape(view), "adamw_" + k)
        delta[k], new_m[k], new_v[k] = d_.reshape(shp), m_.reshape(shp), v_.reshape(shp)
    for k in transposed:
        for tree in (grads, delta, new_m, new_v):
            tree[k] = jnp.swapaxes(tree[k], 1, 2)

    return (loss, dx.reshape(1, T, D), *[grads[k] for k in order], *[delta[k] for k in order],
            *[new_m[k] for k in order], *[new_v[k] for k in order])
```

```python
import functools

import jax
import jax.numpy as jnp
from jax import lax
from jax.experimental import pallas as pl
from jax.experimental.pallas import tpu as pltpu

F32 = jnp.float32
BF16 = jnp.bfloat16
MESH = pl.DeviceIdType.MESH

D = 1024
DEPTH = 4
EPS = 1e-6
SGU_BLOCK = 128
GH = 2048
SGU_G = 8
SGU_GD = GH // SGU_G
N_HEADS = 16
HEAD_DIM = 64
CHUNK = 64
PAD = 8 * CHUNK
FRONT = 2048
QB = 128
KW = PAD + QB
N_REL = 192
REL_MIN = -(CHUNK - 1)
REL_MAX = 128
D_FF = 2816
FS = D_FF // 4
NEG = -1e30
SCALE = HEAD_DIM ** -0.5
N_CHIPS = 4

ADAM_LR = 0.001
ADAM_B1 = 0.9
ADAM_B2 = 0.999
ADAM_EPS = 1e-08
ADAM_WD = 0.01
ADAM_STEP = 10

VMEM_BIG = 56 * 1024 * 1024

NN = ((1,), (0,))
NT = ((1,), (1,))
TN = ((0,), (0,))


def _dot(a, b, dims):
    return lax.dot_general(a, b, (dims, ((), ())), preferred_element_type=F32)


class Comm:
    def __init__(self, ins, out_shapes, sems, start, wait, aliases=None):
        self.ins, self.out_shapes, self.sems = list(ins), list(out_shapes), list(sems)
        self.start, self.wait, self.aliases = start, wait, dict(aliases or {})


def _host(body, comm, kw):
    grid = tuple(kw["grid"])
    in_specs = list(kw["in_specs"])
    single = not isinstance(kw["out_specs"], (list, tuple))
    out_specs = [kw["out_specs"]] if single else list(kw["out_specs"])
    out_shape = [kw["out_shape"]] if single else list(kw["out_shape"])
    scratch = list(kw.get("scratch_shapes", ()))
    counts = (len(in_specs), len(comm.ins), len(out_specs), len(comm.out_shapes), len(scratch))

    def hosted(*refs):
        parts, p = [], 0
        for cnt in counts:
            parts.append(refs[p:p + cnt])
            p += cnt
        main_in, c_in, main_out, c_out, main_scr = parts
        sems = refs[p:]
        ids = [pl.program_id(a) for a in range(len(grid))]
        first = functools.reduce(jnp.logical_and, [i == 0 for i in ids])
        last = functools.reduce(jnp.logical_and, [i == n - 1 for i, n in zip(ids, grid)])
        pl.when(first)(lambda: comm.start(c_in, c_out, sems))
        body(*main_in, *main_out, *main_scr)
        pl.when(last)(lambda: comm.wait(c_in, c_out, sems))

    old = kw["compiler_params"]
    kw = dict(kw, in_specs=in_specs + [ANY] * len(comm.ins), out_specs=out_specs + [ANY] * len(comm.out_shapes),
              out_shape=out_shape + comm.out_shapes, scratch_shapes=scratch + comm.sems,
              compiler_params=pltpu.CompilerParams(dimension_semantics=("arbitrary",) * len(grid),
                                                   vmem_limit_bytes=old.vmem_limit_bytes, has_side_effects=True))
    if comm.aliases:
        kw["input_output_aliases"] = {counts[0] + i: counts[2] + o for i, o in comm.aliases.items()}
    return hosted, kw


def _pallas(body, comm=None, **kw):
    if comm is not None:
        body, kw = _host(body, comm, kw)
    return pl.pallas_call(body, **kw)


def _split_outs(outs, comm, n_main):
    outs = list(outs) if isinstance(outs, (list, tuple)) else [outs]
    main = outs[:n_main]
    return (main[0] if n_main == 1 else main), outs[n_main:]


def run_comm(comm, name):
    nci, nco = len(comm.ins), len(comm.out_shapes)

    def body(*refs):
        c_in, c_out, sems = refs[:nci], refs[nci:nci + nco], refs[nci + nco:]
        comm.start(c_in, c_out, sems)
        comm.wait(c_in, c_out, sems)

    kw = {}
    if comm.aliases:
        kw["input_output_aliases"] = dict(comm.aliases)
    return _pallas(body, name=name, in_specs=[ANY] * nci, out_specs=[ANY] * nco, out_shape=comm.out_shapes,
                   scratch_shapes=comm.sems, compiler_params=pltpu.CompilerParams(has_side_effects=True),
                   **kw)(*comm.ins)


def combine(comms):
    if len(comms) == 1:
        return comms[0]
    spans, ni, no, ns = [], 0, 0, 0
    for c in comms:
        spans.append((slice(ni, ni + len(c.ins)), slice(no, no + len(c.out_shapes)), slice(ns, ns + len(c.sems))))
        ni, no, ns = ni + len(c.ins), no + len(c.out_shapes), ns + len(c.sems)

    def start(ins, outs, sems):
        for c, (si, so, ss) in zip(comms, spans):
            c.start(ins[si], outs[so], sems[ss])

    def wait(ins, outs, sems):
        for c, (si, so, ss) in zip(comms, spans):
            c.wait(ins[si], outs[so], sems[ss])

    aliases = {}
    for c, (si, so, _) in zip(comms, spans):
        aliases.update({si.start + i: so.start + o for i, o in c.aliases.items()})
    return Comm([a for c in comms for a in c.ins], [o for c in comms for o in c.out_shapes],
                [s for c in comms for s in c.sems], start, wait, aliases)


def _call(body, comm, n_main, args, **kw):
    if comm is None:
        return _pallas(body, **kw)(*args)
    return _split_outs(_pallas(body, comm=comm, **kw)(*args, *comm.ins), comm, n_main)


def _params(sem=None, vmem=None):
    return pltpu.CompilerParams(dimension_semantics=sem, vmem_limit_bytes=vmem)


def _sds(shape, dtype):
    return jax.ShapeDtypeStruct(tuple(shape), dtype)


_GELU_C = 0.7978845608028654


def _gelu(x):
    t = jnp.tanh(_GELU_C * (x + 0.044715 * (x * x * x)))
    return 0.5 * x * (1.0 + t)


def _gelu_and_grad(x):
    x2 = x * x
    t = jnp.tanh(_GELU_C * (x + 0.044715 * (x2 * x)))
    val = 0.5 * x * (1.0 + t)
    grad = 0.5 * (1.0 + t) + 0.5 * x * (1.0 - t * t) * (_GELU_C * (1.0 + 3.0 * 0.044715 * x2))
    return val, grad


def _sigmoid(x):
    return 0.5 * (jnp.tanh(0.5 * x) + 1.0)


def cast_bf16(w, name):
    L, R, C = w.shape

    def body(w_ref, o_ref):
        o_ref[...] = w_ref[...].astype(BF16)

    spec = pl.BlockSpec((None, R, C), lambda l: (l, 0, 0))
    return _pallas(body, name=name, grid=(L,), in_specs=[spec], out_specs=spec,
                   out_shape=_sds((L, R, C), BF16), compiler_params=_params(("parallel",)))(w)


def rms_fwd(x, g, name, tm=512):
    T = x.shape[0]

    def body(x_ref, g_ref, o_ref):
        xf = x_ref[...]
        r = lax.rsqrt(jnp.mean(xf * xf, axis=-1, keepdims=True) + EPS)
        o_ref[...] = ((xf * r) * g_ref[...]).astype(BF16)

    row = pl.BlockSpec((tm, D), lambda i: (i, 0))
    return _pallas(body, name=name, grid=(T // tm,),
                   in_specs=[row, pl.BlockSpec((1, D), lambda i: (0, 0))], out_specs=row,
                   out_shape=_sds((T, D), BF16), compiler_params=_params(("parallel",)))(x, g)


def dgrad_rms(name, compute, args, specs, x, g, dres, tm=512, comm=None):
    T = x.shape[0]
    n = T // tm
    k = len(args)

    def body(*refs):
        x_ref, g_ref, dres_ref, dx_ref, dxb_ref, dg_ref, acc_ref = refs[k:]
        i = pl.program_id(0)
        xf = x_ref[...]
        r = lax.rsqrt(jnp.mean(xf * xf, axis=-1, keepdims=True) + EPS)
        xhat = xf * r
        dhf = compute(*refs[:k])
        part = (dhf * xhat).reshape(tm // 8, 8, D).sum(axis=0)

        @pl.when(i == 0)
        def _():
            acc_ref[...] = part

        @pl.when(i > 0)
        def _():
            acc_ref[...] += part

        dxhat = dhf * g_ref[...]
        dx = dres_ref[...] + r * (dxhat - xhat * jnp.mean(dxhat * xhat, axis=-1, keepdims=True))
        dx_ref[...] = dx
        dxb_ref[...] = dx.astype(BF16)

        @pl.when(i == n - 1)
        def _():
            dg_ref[...] = jnp.sum(acc_ref[...], axis=0, keepdims=True)

    row = pl.BlockSpec((tm, D), lambda i: (i, 0))
    vec = pl.BlockSpec((1, D), lambda i: (0, 0))
    return _call(body, comm, 3, (*args, x, g, dres), name=name, grid=(n,),
                 in_specs=list(specs) + [row, vec, row], out_specs=[row, row, vec],
                 out_shape=[_sds((T, D), F32), _sds((T, D), BF16), _sds((1, D), F32)],
                 scratch_shapes=[pltpu.VMEM((8, D), F32)],
                 compiler_params=_params(("arbitrary",), VMEM_BIG))


def final_loss(x, g, tgt, name, tm=256):
    T = x.shape[0]
    n = T // tm

    def body(x_ref, g_ref, t_ref, loss_ref, dx_ref, dxb_ref, dg_ref, acc_ref, lacc_ref):
        i = pl.program_id(0)
        xf = x_ref[...]
        r = lax.rsqrt(jnp.mean(xf * xf, axis=-1, keepdims=True) + EPS)
        xhat = xf * r
        gg = g_ref[...]
        e = xhat * gg - t_ref[...]
        dy = e * (1.0 / D)
        part = (dy * xhat).reshape(tm // 8, 8, D).sum(axis=0)
        lpart = (e * e).reshape(tm // 8, 8, D).sum(axis=0)

        @pl.when(i == 0)
        def _():
            acc_ref[...] = part
            lacc_ref[...] = lpart

        @pl.when(i > 0)
        def _():
            acc_ref[...] += part
            lacc_ref[...] += lpart

        dxhat = dy * gg
        dx = r * (dxhat - xhat * jnp.mean(dxhat * xhat, axis=-1, keepdims=True))
        dx_ref[...] = dx
        dxb_ref[...] = dx.astype(BF16)

        @pl.when(i == n - 1)
        def _():
            dg_ref[...] = jnp.sum(acc_ref[...], axis=0, keepdims=True)
            total = jnp.sum(jnp.sum(lacc_ref[...], axis=0, keepdims=True), axis=1, keepdims=True)
            loss_ref[...] = jnp.broadcast_to(total * (0.5 / D), (1, 128))

    row = pl.BlockSpec((tm, D), lambda i: (i, 0))
    vec = pl.BlockSpec((1, D), lambda i: (0, 0))
    return _pallas(body, name=name, grid=(n,), in_specs=[row, vec, row],
                   out_specs=[pl.BlockSpec((1, 128), lambda i: (0, 0)), row, row, vec],
                   out_shape=[_sds((1, 128), F32), _sds((T, D), F32), _sds((T, D), BF16), _sds((1, D), F32)],
                   scratch_shapes=[pltpu.VMEM((8, D), F32), pltpu.VMEM((8, D), F32)],
                   compiler_params=_params(("arbitrary",)))(x, g, tgt)


def matmul(name, dims, a, a_spec, b, b_spec, out_shape, out_spec, grid, *, acc=False, res=None, res_spec=None,
           comm=None):
    has_res = res is not None

    def body(*refs):
        a_ref, b_ref = refs[0], refs[1]
        r_ref = refs[2] if has_res else None
        o_ref = refs[-1]
        d = _dot(a_ref[...], b_ref[...], dims)
        if not acc:
            if has_res:
                d = d + r_ref[...]
            o_ref[...] = d.astype(o_ref.dtype)
        else:
            k = pl.program_id(len(grid) - 1)

            @pl.when(k == 0)
            def _():
                o_ref[...] = (d + r_ref[...]) if has_res else d

            @pl.when(k > 0)
            def _():
                o_ref[...] += d

    sem = ("parallel",) * (len(grid) - 1) + (("arbitrary",) if acc else ("parallel",))
    ins = [a, b] + ([res] if has_res else [])
    specs = [a_spec, b_spec] + ([res_spec] if has_res else [])
    return _call(body, comm, 1, ins, name=name, grid=grid, in_specs=specs, out_specs=out_spec, out_shape=out_shape,
                 compiler_params=_params(sem, VMEM_BIG))


def wgrad(name, a, a_spec, b, b_spec, out_shape, out_spec, J, T, tk, comm=None):
    return matmul(name, TN, a, a_spec, b, b_spec, out_shape, out_spec, (J, T // tk), acc=True, comm=comm)


def _sgu_mask():
    p = lax.broadcasted_iota(jnp.int32, (SGU_BLOCK, SGU_BLOCK), 0)
    q = lax.broadcasted_iota(jnp.int32, (SGU_BLOCK, SGU_BLOCK), 1)
    return lax.shift_right_logical(q, 6) <= lax.shift_right_logical(p, 6)


def sgu_fwd(pre, gain, w_s, b_s, name, comm=None):
    T = pre.shape[0]

    def body(pre_ref, gain_ref, ws_ref, bs_ref, y_ref):
        mask = _sgu_mask()
        u = _gelu(pre_ref[:, :GH].astype(F32))
        va = _gelu(pre_ref[:, GH:].astype(F32))
        r = lax.rsqrt(jnp.mean(va * va, axis=-1, keepdims=True) + EPS)
        vn = ((va * r) * gain_ref[...]).astype(BF16)
        for g in range(SGU_G):
            sl = slice(g * SGU_GD, (g + 1) * SGU_GD)
            wm = jnp.where(mask, ws_ref[g], 0.0).astype(BF16)
            vm = _dot(wm, vn[:, sl], NN) + bs_ref[g]
            y_ref[:, sl] = (u[:, sl] * vm).astype(BF16)

    return _call(
        body, comm, 1, (pre, gain, w_s, b_s), name=name, grid=(T // SGU_BLOCK,),
        in_specs=[pl.BlockSpec((SGU_BLOCK, 2 * GH), lambda i: (i, 0)),
                  pl.BlockSpec((1, GH), lambda i: (0, 0)),
                  pl.BlockSpec((SGU_G, SGU_BLOCK, SGU_BLOCK), lambda i: (0, 0, 0)),
                  pl.BlockSpec((SGU_G, SGU_BLOCK, 1), lambda i: (0, 0, 0))],
        out_specs=pl.BlockSpec((SGU_BLOCK, GH), lambda i: (i, 0)),
        out_shape=_sds((T, GH), BF16), compiler_params=_params(("parallel",)))


def sgu_bwd(pre, dy, gain, w_s, b_s, name, comm=None):
    T = pre.shape[0]
    n = T // SGU_BLOCK

    def body(pre_ref, dy_ref, gain_ref, ws_ref, bs_ref, dpre_ref, dws_ref, dbs_ref, dgain_ref, gacc_ref):
        i = pl.program_id(0)

        @pl.when(i == 0)
        def _():
            dws_ref[...] = jnp.zeros_like(dws_ref)
            dbs_ref[...] = jnp.zeros_like(dbs_ref)
            gacc_ref[...] = jnp.zeros_like(gacc_ref)

        mask = _sgu_mask()
        u, du_dpre = _gelu_and_grad(pre_ref[:, :GH].astype(F32))
        va, dva_dpre = _gelu_and_grad(pre_ref[:, GH:].astype(F32))
        r = lax.rsqrt(jnp.mean(va * va, axis=-1, keepdims=True) + EPS)
        vhat = va * r
        gain_v = gain_ref[...]
        vn = (vhat * gain_v).astype(BF16)
        dyf = dy_ref[...].astype(F32)
        dvn_parts = []
        for g in range(SGU_G):
            sl = slice(g * SGU_GD, (g + 1) * SGU_GD)
            wm = jnp.where(mask, ws_ref[g], 0.0).astype(BF16)
            vm = _dot(wm, vn[:, sl], NN) + bs_ref[g]
            dpre_ref[:, sl] = ((dyf[:, sl] * vm) * du_dpre[:, sl]).astype(BF16)
            dvm = dyf[:, sl] * u[:, sl]
            dbs_ref[g] += jnp.sum(dvm, axis=-1, keepdims=True)
            dvm16 = dvm.astype(BF16)
            dws_ref[g] += jnp.where(mask, _dot(dvm16, vn[:, sl], NT), 0.0)
            dvn_parts.append(_dot(wm, dvm16, TN))
        dvn = jnp.concatenate(dvn_parts, axis=-1)
        gacc_ref[...] += (dvn * vhat).reshape(SGU_BLOCK // 8, 8, GH).sum(axis=0)
        dvhat = dvn * gain_v
        dva = r * (dvhat - vhat * jnp.mean(dvhat * vhat, axis=-1, keepdims=True))
        dpre_ref[:, GH:] = (dva * dva_dpre).astype(BF16)

        @pl.when(i == n - 1)
        def _():
            dgain_ref[...] = jnp.sum(gacc_ref[...], axis=0, keepdims=True)

    const3 = lambda i: (0, 0, 0)
    return _call(
        body, comm, 4, (pre, dy, gain, w_s, b_s), name=name, grid=(n,),
        in_specs=[pl.BlockSpec((SGU_BLOCK, 2 * GH), lambda i: (i, 0)),
                  pl.BlockSpec((SGU_BLOCK, GH), lambda i: (i, 0)),
                  pl.BlockSpec((1, GH), lambda i: (0, 0)),
                  pl.BlockSpec((SGU_G, SGU_BLOCK, SGU_BLOCK), const3),
                  pl.BlockSpec((SGU_G, SGU_BLOCK, 1), const3)],
        out_specs=[pl.BlockSpec((SGU_BLOCK, 2 * GH), lambda i: (i, 0)),
                   pl.BlockSpec((SGU_G, SGU_BLOCK, SGU_BLOCK), const3),
                   pl.BlockSpec((SGU_G, SGU_BLOCK, 1), const3),
                   pl.BlockSpec((1, GH), lambda i: (0, 0))],
        out_shape=[_sds((T, 2 * GH), BF16), _sds((SGU_G, SGU_BLOCK, SGU_BLOCK), F32),
                   _sds((SGU_G, SGU_BLOCK, 1), F32), _sds((1, GH), F32)],
        scratch_shapes=[pltpu.VMEM((8, GH), F32)],
        compiler_params=_params(("arbitrary",)))


def _rel_onehot(i):
    j = lax.broadcasted_iota(jnp.int32, (N_REL, KW), 1)
    r = lax.broadcasted_iota(jnp.int32, (N_REL, KW), 0)
    idx = jnp.clip(i - j + PAD, REL_MIN, REL_MAX) - REL_MIN
    return (idx == r).astype(BF16)


def _split3(v):
    hi = v.astype(BF16)
    r1 = v - hi.astype(F32)
    mid = r1.astype(BF16)
    lo = (r1 - mid.astype(F32)).astype(BF16)
    return hi, mid, lo


def bias_build(rel_bias, name):
    def body(rb_ref, o_ref):
        parts = _split3(rb_ref[...])

        def row(i, carry):
            oh = _rel_onehot(i)
            val = _dot(parts[0], oh, NN) + _dot(parts[1], oh, NN) + _dot(parts[2], oh, NN)
            j = lax.broadcasted_iota(jnp.int32, (1, KW), 1)
            rel = lax.shift_right_logical(i, 6) - lax.shift_right_logical(j, 6) + 8
            ok = (rel >= 0) & (rel <= 8)
            o_ref[i] = jnp.where(ok, val, NEG)
            return carry

        lax.fori_loop(0, QB, row, 0)

    return _pallas(body, name=name, out_shape=_sds((QB, N_HEADS, KW), F32),
                   in_specs=[pl.BlockSpec(memory_space=pltpu.VMEM)],
                   out_specs=pl.BlockSpec(memory_space=pltpu.VMEM))(rel_bias)


def bias_grad(dwb, name):
    def body(d_ref, o_ref):
        def row(i, acc):
            oh = _rel_onehot(i)
            hi, mid, lo = _split3(d_ref[i])
            return acc + (_dot(hi, oh, NT) + _dot(mid, oh, NT) + _dot(lo, oh, NT))

        o_ref[...] = lax.fori_loop(0, QB, row, jnp.zeros((N_HEADS, N_REL), F32))

    return _pallas(body, name=name, out_shape=_sds((N_HEADS, N_REL), F32),
                   in_specs=[pl.BlockSpec(memory_space=pltpu.VMEM)],
                   out_specs=pl.BlockSpec(memory_space=pltpu.VMEM))(dwb)


def _attn_block(qkv_ref, blk):
    r0 = pl.multiple_of(blk * QB, QB)
    qs = qkv_ref[0, pl.ds(r0 + FRONT, QB), :] * SCALE
    k2 = qkv_ref[1, pl.ds(r0 + (FRONT - PAD), KW), :]
    v2 = qkv_ref[2, pl.ds(r0 + (FRONT - PAD), KW), :]
    col = lax.broadcasted_iota(jnp.int32, (1, KW), 1)
    return r0, qs, k2, v2, col >= PAD - blk * QB


def _head_mask(h):
    lane = lax.broadcasted_iota(jnp.int32, (1, 2 * HEAD_DIM), 1)
    return (lane < HEAD_DIM) if h == 0 else (lane >= HEAD_DIM)


def _stack_heads(a):
    zero = jnp.zeros_like(a)
    return jnp.concatenate([jnp.where(_head_mask(0), a, zero), jnp.where(_head_mask(1), a, zero)], axis=0)


def _unstack_heads(a):
    return jnp.where(_head_mask(0), a[:QB], a[QB:])


def _attn_exp(qst, k2, w_ref, kvalid):
    s = jnp.where(kvalid, _dot(qst, k2, NT) + w_ref[...].reshape(2 * QB, KW), NEG)
    e = jnp.exp(s - jnp.max(s, axis=-1, keepdims=True))
    return e, 1.0 / jnp.sum(e, axis=-1, keepdims=True)


ATTN_G = 4


def attn_fwd(qkvp, wb, name, comm=None):
    T = qkvp.shape[1] - FRONT

    def body(qkv_ref, w_ref, o_ref):
        b = pl.program_id(1)
        for t in range(ATTN_G):
            _, qs, k2, v2, kvalid = _attn_block(qkv_ref, b * ATTN_G + t)
            e, inv = _attn_exp(_stack_heads(qs), k2, w_ref, kvalid)
            o_ref[t * QB:(t + 1) * QB, :] = _unstack_heads(_dot(e.astype(BF16), v2, NN) * inv).astype(BF16)

    return _call(
        body, comm, 1, (qkvp, wb), name=name, grid=(N_HEADS // 2, T // (QB * ATTN_G)),
        in_specs=[pl.BlockSpec((3, FRONT + T, 2 * HEAD_DIM), lambda hp, b: (0, 0, hp)),
                  pl.BlockSpec((2, QB, KW), lambda hp, b: (hp, 0, 0))],
        out_specs=pl.BlockSpec((QB * ATTN_G, 2 * HEAD_DIM), lambda hp, b: (b, hp)),
        out_shape=_sds((T, D), BF16),
        compiler_params=_params(("parallel", "arbitrary"), VMEM_BIG))


def attn_bwd(qkvp, o, do, wb, name, comm=None):
    T = qkvp.shape[1] - FRONT
    nb = T // (QB * ATTN_G)

    def body(qkv_ref, o_ref, do_ref, w_ref, dqkv_ref, dw_ref, dk_acc, dv_acc):
        b = pl.program_id(1)

        @pl.when(b == 0)
        def _():
            dk_acc[...] = jnp.zeros_like(dk_acc)
            dv_acc[...] = jnp.zeros_like(dv_acc)
            dw_ref[...] = jnp.zeros_like(dw_ref)
            dqkv_ref[0, 0:FRONT, :] = jnp.zeros((FRONT, 2 * HEAD_DIM), BF16)

        dws = None
        for t in range(ATTN_G):
            r0, qs, k2, v2, kvalid = _attn_block(qkv_ref, b * ATTN_G + t)
            qst = _stack_heads(qs)
            e, inv = _attn_exp(qst, k2, w_ref, kvalid)
            do2 = do_ref[t * QB:(t + 1) * QB, :]
            dost = _stack_heads(do2)
            prod = _stack_heads(do2.astype(F32) * o_ref[t * QB:(t + 1) * QB, :].astype(F32))
            delta = jnp.sum(prod, axis=-1, keepdims=True)
            ds = e * ((_dot(dost, v2, NT) - delta) * inv)
            dws = ds if dws is None else dws + ds
            ds16 = ds.astype(BF16)
            dq = _unstack_heads(_dot(ds16, k2, NN)) * SCALE
            dqkv_ref[0, pl.ds(r0 + FRONT, QB), :] = dq.astype(BF16)
            dk_acc[pl.ds(r0 + (FRONT - PAD), KW), :] += _dot(ds16, qst, TN)
            dv_acc[pl.ds(r0 + (FRONT - PAD), KW), :] += _dot(e.astype(BF16), (dost.astype(F32) * inv).astype(BF16), TN)
        dw_ref[...] += dws.reshape(2, QB, KW)

        @pl.when(b == nb - 1)
        def _():
            dqkv_ref[1] = dk_acc[...].astype(BF16)
            dqkv_ref[2] = dv_acc[...].astype(BF16)

    slab = pl.BlockSpec((3, FRONT + T, 2 * HEAD_DIM), lambda hp, b: (0, 0, hp))
    wspec = pl.BlockSpec((2, QB, KW), lambda hp, b: (hp, 0, 0))
    rows = pl.BlockSpec((QB * ATTN_G, 2 * HEAD_DIM), lambda hp, b: (b, hp))
    return _call(
        body, comm, 2, (qkvp, o, do, wb), name=name, grid=(N_HEADS // 2, nb),
        in_specs=[slab, rows, rows, wspec],
        out_specs=[slab, wspec],
        out_shape=[_sds((3, FRONT + T, D), BF16), _sds((N_HEADS, QB, KW), F32)],
        scratch_shapes=[pltpu.VMEM((FRONT + T, 2 * HEAD_DIM), F32), pltpu.VMEM((FRONT + T, 2 * HEAD_DIM), F32)],
        compiler_params=_params(("parallel", "arbitrary"), VMEM_BIG))


def proj_qkv(hn, w, l, name, tm=512, comm=None):
    T = hn.shape[0]
    pb = FRONT // tm

    def body(a_ref, b_ref, o_ref):
        i = pl.program_id(1)

        @pl.when(i < pb)
        def _():
            o_ref[...] = jnp.zeros_like(o_ref)

        @pl.when(i >= pb)
        def _():
            o_ref[...] = _dot(a_ref[...], b_ref[...], NN).astype(BF16)

    return _call(
        body, comm, 1, (hn, w), name=name, grid=(3, pb + T // tm),
        in_specs=[pl.BlockSpec((tm, D), lambda p, i: (jnp.maximum(i - pb, 0), 0)),
                  pl.BlockSpec((None, D, D), lambda p, i: (l, 0, p))],
        out_specs=pl.BlockSpec((None, tm, D), lambda p, i: (p, i, 0)),
        out_shape=_sds((3, FRONT + T, D), BF16),
        compiler_params=_params(("parallel", "parallel"), VMEM_BIG))


def ffn_up(hn, wg, wu, l, name, tm=1024, comm=None):
    T = hn.shape[0]

    def body(a_ref, wg_ref, wu_ref, g_ref, u_ref, h_ref):
        a = a_ref[...]
        g = _dot(a, wg_ref[...], NT)
        u = _dot(a, wu_ref[...], NT)
        s = _sigmoid(g)
        silu = g * s
        g_ref[...] = (u * (s * (1.0 + g * (1.0 - s)))).astype(BF16)
        u_ref[...] = silu.astype(BF16)
        h_ref[...] = (silu * u).astype(BF16)

    wspec = pl.BlockSpec((None, None, FS, D), lambda s, i: (l, s, 0, 0))
    ospec = pl.BlockSpec((None, tm, FS), lambda s, i: (s, i, 0))
    return _call(
        body, comm, 3, (hn, wg, wu), name=name, grid=(N_CHIPS, T // tm),
        in_specs=[pl.BlockSpec((tm, D), lambda s, i: (i, 0)), wspec, wspec],
        out_specs=[ospec, ospec, ospec],
        out_shape=[_sds((N_CHIPS, T, FS), BF16)] * 3,
        compiler_params=_params(("parallel", "parallel"), VMEM_BIG))


def ffn_bwd_dh(dxb, wd, g, u, l, name, tm=1024, comm=None):
    T = dxb.shape[0]

    def body(a_ref, wd_ref, g_ref, u_ref, dg_ref, du_ref):
        dh = _dot(a_ref[...], wd_ref[...], NT)
        dg_ref[...] = (dh * g_ref[...].astype(F32)).astype(BF16)
        du_ref[...] = (dh * u_ref[...].astype(F32)).astype(BF16)

    aspec = pl.BlockSpec((None, tm, FS), lambda s, i: (s, i, 0))
    return _call(
        body, comm, 2, (dxb, wd, g, u), name=name, grid=(N_CHIPS, T // tm),
        in_specs=[pl.BlockSpec((tm, D), lambda s, i: (i, 0)),
                  pl.BlockSpec((None, None, FS, D), lambda s, i: (l, s, 0, 0)), aspec, aspec],
        out_specs=[aspec, aspec],
        out_shape=[_sds((N_CHIPS, T, FS), BF16)] * 2,
        compiler_params=_params(("parallel", "parallel"), VMEM_BIG))


def ffn_dgrad(dg, du, wg, wu, tm=512):
    def compute(dg_ref, du_ref, wg_ref, wu_ref):
        d = None
        for s in range(N_CHIPS):
            t = _dot(dg_ref[s], wg_ref[s], NN) + _dot(du_ref[s], wu_ref[s], NN)
            d = t if d is None else d + t
        return d

    aspec = pl.BlockSpec((N_CHIPS, tm, FS), lambda i: (0, i, 0))
    wspec = pl.BlockSpec((None, N_CHIPS, FS, D), lambda i: (0, 0, 0, 0), pipeline_mode=pl.Buffered(1))
    return compute, (dg, du, wg, wu), [aspec, aspec, wspec, wspec]


def qkv_dgrad(dqkvp, w, tm=512):
    def compute(a_ref, w_ref):
        d = None
        for p in range(3):
            t = _dot(a_ref[p], w_ref[:, p * D:(p + 1) * D], NT)
            d = t if d is None else d + t
        return d

    return compute, (dqkvp, w), [pl.BlockSpec((3, tm, D), lambda i: (0, i + FRONT // tm, 0)),
                                 pl.BlockSpec((None, D, 3 * D), lambda i: (0, 0, 0))]


def in_dgrad(dpre, w, tm=512):
    def compute(a_ref, w_ref):
        return _dot(a_ref[...], w_ref[...], NT)

    return compute, (dpre, w), [pl.BlockSpec((tm, 2 * GH), lambda i: (i, 0)),
                                pl.BlockSpec((None, D, 2 * GH), lambda i: (0, 0, 0))]


def _rms_rows(x, g):
    r = lax.rsqrt(jnp.mean(x * x, axis=-1, keepdims=True) + EPS)
    return ((x * r) * g).astype(BF16)


def residual_proj(name, compute, args, specs, res, norm_g, tm=512, comm=None):
    T = res.shape[0]
    k = len(args)
    with_norm = norm_g is not None

    def body(*refs):
        d = refs[k][...] + compute(*refs[:k])
        if with_norm:
            refs[k + 2][...] = d
            refs[k + 3][...] = _rms_rows(d, refs[k + 1][...])
        else:
            refs[k + 1][...] = d

    row = pl.BlockSpec((tm, D), lambda i: (i, 0))
    vec = pl.BlockSpec((1, D), lambda i: (0, 0))
    if with_norm:
        return _call(body, comm, 2, (*args, res, norm_g), name=name, grid=(T // tm,),
                     in_specs=list(specs) + [row, vec], out_specs=[row, row],
                     out_shape=[_sds((T, D), F32), _sds((T, D), BF16)],
                     compiler_params=_params(("parallel",), VMEM_BIG))
    return _call(body, comm, 1, (*args, res), name=name, grid=(T // tm,), in_specs=list(specs) + [row],
                 out_specs=row, out_shape=_sds((T, D), F32), compiler_params=_params(("parallel",), VMEM_BIG))


def ffn_down(h, wd, tm=512):
    def compute(h_ref, wd_ref):
        d = None
        for s in range(N_CHIPS):
            t = _dot(h_ref[s], wd_ref[s], NN)
            d = t if d is None else d + t
        return d

    return compute, (h, wd), [pl.BlockSpec((N_CHIPS, tm, FS), lambda i: (0, i, 0)),
                              pl.BlockSpec((None, N_CHIPS, FS, D), lambda i: (0, 0, 0, 0))]


def out_proj(a, w, tm=512):
    K = a.shape[1]

    def compute(a_ref, w_ref):
        return _dot(a_ref[...], w_ref[...], NN)

    return compute, (a, w), [pl.BlockSpec((tm, K), lambda i: (i, 0)), pl.BlockSpec((None, K, D), lambda i: (0, 0, 0))]


def adamw(w, g, m, v, name):
    L, R, C = w.shape

    def body(w_ref, g_ref, m_ref, v_ref, d_ref, nm_ref, nv_ref):
        gf = g_ref[...]
        nm = ADAM_B1 * m_ref[...] + (1.0 - ADAM_B1) * gf
        nv = ADAM_B2 * v_ref[...] + (1.0 - ADAM_B2) * (gf * gf)
        m_hat = nm / (1.0 - ADAM_B1 ** ADAM_STEP)
        v_hat = nv / (1.0 - ADAM_B2 ** ADAM_STEP)
        d_ref[...] = -ADAM_LR * (m_hat / (jnp.sqrt(v_hat) + ADAM_EPS) + ADAM_WD * w_ref[...])
        nm_ref[...] = nm
        nv_ref[...] = nv

    tr = R // 4 if R % 32 == 0 else R
    spec = pl.BlockSpec((None, tr, C), lambda l, r: (l, r, 0))
    return _pallas(body, name=name, grid=(L, R // tr), in_specs=[spec] * 4, out_specs=[spec] * 3,
                   out_shape=[_sds((L, R, C), F32)] * 3,
                   compiler_params=_params(("parallel", "parallel")))(w, g, m, v)


def _coords():
    return lax.axis_index("x"), lax.axis_index("y"), lax.axis_index("c")


def _other_chips(x, y):
    out = []
    for fx, fy in ((1, 0), (0, 1), (1, 1)):
        px = (1 - x) if fx else x
        py = (1 - y) if fy else y
        out.append((px, py))
    return out


def _flip_index(s, j):
    sx, sy = s // 2, s % 2
    fx, fy = ((1, 0), (0, 1), (1, 1))[j]
    return 2 * (sx ^ fx) + (sy ^ fy)


def _for_my_chip(sme, fn):
    for s in range(N_CHIPS):
        pl.when(sme == s)(functools.partial(fn, s))


ANY = pl.BlockSpec(memory_space=pl.ANY)

GATHER_KIND = {"a_w_in": "col", "b_w_qkv": "col", "a_w_out": "row", "b_w_out": "row",
               "ffn_w_gate": "row", "ffn_w_up": "row", "ffn_w_down": "row"}
BIG = tuple(GATHER_KIND)


def _gathered_shape(kind, shape):
    L, R, C = shape
    return (L, R, N_CHIPS * C) if kind == "col" else (L, N_CHIPS, R, C)


def _shard_rows(ref, kind, s, r0, rn, C):
    if kind == "col":
        return ref.at[:, pl.ds(r0, rn), s * C:(s + 1) * C]
    return ref.at[:, s, pl.ds(r0, rn), :]


def gather_stage1(items):
    n = len(items)
    dims = [it[0].shape[1:] for it in items]

    def copies(ins, outs, sems, s, with_landed=True):
        lsem, ssem, rsem = sems
        x, y, c = _coords()
        chips = _other_chips(x, y)
        local, send, landed = [], [], []
        for t, (_, li, kind) in enumerate(items):
            R, C = dims[t]
            r0 = pl.multiple_of(c * (R // 2), 8)
            local.append(pltpu.make_async_copy(ins[t].at[pl.ds(li, 1)], _shard_rows(outs[t], kind, s, 0, R, C),
                                               lsem.at[t]))
            for j in range(3):
                pair = dict(send_sem=ssem.at[3 * t + j], recv_sem=rsem.at[3 * t + j],
                            device_id=(chips[j][0], chips[j][1], c), device_id_type=MESH)
                send.append(pltpu.make_async_remote_copy(
                    src_ref=ins[t].at[pl.ds(li, 1), pl.ds(r0, R // 2), :],
                    dst_ref=_shard_rows(outs[t], kind, s, r0, R // 2, C), **pair))
                if with_landed:
                    got = _shard_rows(outs[t], kind, _flip_index(s, j), r0, R // 2, C)
                    landed.append(pltpu.make_async_remote_copy(src_ref=got, dst_ref=got, **pair))
        return local, send, landed

    def start(ins, outs, sems):
        def run(s):
            local, send, _ = copies(ins, outs, sems, s, with_landed=False)
            for cp in local + send:
                cp.start()
        x, y, _ = _coords()
        _for_my_chip(2 * x + y, run)

    def wait(ins, outs, sems):
        def run(s):
            local, send, landed = copies(ins, outs, sems, s)
            for cp in landed:
                cp.wait_recv()
            for cp in send:
                cp.wait_send()
            for cp in local:
                cp.wait()
        x, y, _ = _coords()
        _for_my_chip(2 * x + y, run)

    out_shapes = [_sds(_gathered_shape(kind, (1,) + tuple(dims[t])), BF16) for t, (_, _, kind) in enumerate(items)]
    sems = [pltpu.SemaphoreType.DMA((n,)), pltpu.SemaphoreType.DMA((3 * n,)), pltpu.SemaphoreType.DMA((3 * n,))]
    return Comm([it[0] for it in items], out_shapes, sems, start, wait)


def gather_stage2(items, gathered):
    n = len(items)
    dims = [it[0].shape[1:] for it in items]

    def copies(outs, sems, s, with_landed=True):
        ssem, rsem = sems
        x, y, c = _coords()
        send, landed = [], []
        for t, (_, _, kind) in enumerate(items):
            R, C = dims[t]
            for j in range(3):
                pair = dict(send_sem=ssem.at[3 * t + j], recv_sem=rsem.at[3 * t + j],
                            device_id=(x, y, 1 - c), device_id_type=MESH)
                mine = _shard_rows(outs[t], kind, _flip_index(s, j), pl.multiple_of(c * (R // 2), 8), R // 2, C)
                send.append(pltpu.make_async_remote_copy(src_ref=mine, dst_ref=mine, **pair))
                if with_landed:
                    other = _shard_rows(outs[t], kind, _flip_index(s, j), pl.multiple_of((1 - c) * (R // 2), 8),
                                        R // 2, C)
                    landed.append(pltpu.make_async_remote_copy(src_ref=other, dst_ref=other, **pair))
        return send, landed

    def start(ins, outs, sems):
        def run(s):
            for cp in copies(outs, sems, s, with_landed=False)[0]:
                cp.start()
        x, y, _ = _coords()
        _for_my_chip(2 * x + y, run)

    def wait(ins, outs, sems):
        def run(s):
            send, landed = copies(outs, sems, s)
            for cp in landed:
                cp.wait_recv()
            for cp in send:
                cp.wait_send()
        x, y, _ = _coords()
        _for_my_chip(2 * x + y, run)

    out_shapes = [_sds(g.shape, BF16) for g in gathered]
    sems = [pltpu.SemaphoreType.DMA((3 * n,)), pltpu.SemaphoreType.DMA((3 * n,))]
    return Comm(gathered, out_shapes, sems, start, wait, aliases={t: t for t in range(n)})


def gather_both(items):
    s1 = gather_stage1(items)
    s2 = gather_stage2(items, s1.out_shapes)
    n1 = len(s1.sems)

    def wait(ins, outs, sems):
        s1.wait(ins, outs, sems[:n1])
        s2.start((), outs, sems[n1:])
        s2.wait((), outs, sems[n1:])

    return Comm(s1.ins, s1.out_shapes, s1.sems + s2.sems, lambda ins, outs, sems: s1.start(ins, outs, sems[:n1]), wait)


def _half_shape(kind, R, C):
    return (R // 2, N_CHIPS * C) if kind == "col" else (N_CHIPS, R // 2, C)


def exchange_halves(grads, metas):
    n = len(grads)

    def copies(ins, outs, sems):
        ssem, rsem = sems
        x, y, c = _coords()
        out = []
        for t, (kind, R, C) in enumerate(metas):
            r0 = pl.multiple_of((1 - c) * (R // 2), 8)
            src = ins[t].at[pl.ds(r0, R // 2), :] if kind == "col" else ins[t].at[:, pl.ds(r0, R // 2), :]
            out.append(pltpu.make_async_remote_copy(
                src_ref=src, dst_ref=outs[t], send_sem=ssem.at[t], recv_sem=rsem.at[t],
                device_id=(x, y, 1 - c), device_id_type=MESH))
        return out

    def start(ins, outs, sems):
        for cp in copies(ins, outs, sems):
            cp.start()

    def wait(ins, outs, sems):
        for cp in copies(ins, outs, sems):
            cp.wait()

    return Comm(grads, [_sds(_half_shape(*m), F32) for m in metas], [pltpu.SemaphoreType.DMA((n,))] * 2, start, wait)


def pair_sum(me, g, sib, meta, name):
    kind, R, C = meta
    h = R // 2

    def body(me_ref, g_ref, sib_ref, p16_ref, own_ref):
        s = pl.program_id(0)
        v = g_ref[...] + sib_ref[...]
        p16_ref[...] = v.astype(BF16)

        @pl.when(s == me_ref[1])
        def _():
            own_ref[...] = v

    if kind == "col":
        gspec = pl.BlockSpec((h, C), lambda s, me_ref: (me_ref[0], s))
        sspec = pl.BlockSpec((h, C), lambda s, me_ref: (0, s))
    else:
        gspec = pl.BlockSpec((None, h, C), lambda s, me_ref: (s, me_ref[0], 0))
        sspec = pl.BlockSpec((None, h, C), lambda s, me_ref: (s, 0, 0))
    grid_spec = pltpu.PrefetchScalarGridSpec(
        num_scalar_prefetch=1, grid=(N_CHIPS,), in_specs=[gspec, sspec],
        out_specs=[sspec, pl.BlockSpec((h, C), lambda s, me_ref: (0, 0))])
    return _pallas(body, name=name, grid_spec=grid_spec,
                   out_shape=[_sds(_half_shape(*meta), BF16), _sds((h, C), F32)],
                   compiler_params=_params(("arbitrary",), VMEM_BIG))(me, g, sib)


def scatter_partials(p16s, metas):
    n = len(p16s)

    def copies(ins, outs, sems, s):
        ssem, rsem = sems
        x, y, c = _coords()
        chips = _other_chips(x, y)
        out = []
        for t, (kind, R, C) in enumerate(metas):
            for j in range(3):
                sj = _flip_index(s, j)
                src = ins[t].at[:, sj * C:(sj + 1) * C] if kind == "col" else ins[t].at[sj]
                out.append(pltpu.make_async_remote_copy(
                    src_ref=src, dst_ref=outs[t].at[j], send_sem=ssem.at[3 * t + j], recv_sem=rsem.at[3 * t + j],
                    device_id=(chips[j][0], chips[j][1], c), device_id_type=MESH))
        return out

    def start(ins, outs, sems):
        def run(s):
            for cp in copies(ins, outs, sems, s):
                cp.start()
        x, y, _ = _coords()
        _for_my_chip(2 * x + y, run)

    def wait(ins, outs, sems):
        def run(s):
            for cp in copies(ins, outs, sems, s):
                cp.wait()
        x, y, _ = _coords()
        _for_my_chip(2 * x + y, run)

    return Comm(p16s, [_sds((3, R // 2, C), BF16) for (_, R, C) in metas],
                [pltpu.SemaphoreType.DMA((3 * n,))] * 2, start, wait)


def final_sum(me, own, q, buf, l, meta, name):
    _, R, C = meta
    h = R // 2

    def body(me_ref, own_ref, q_ref, buf_ref, o_ref):
        del buf_ref
        o_ref[...] = ((own_ref[...] + q_ref[0].astype(F32)) + q_ref[1].astype(F32)) + q_ref[2].astype(F32)

    grid_spec = pltpu.PrefetchScalarGridSpec(
        num_scalar_prefetch=1, grid=(1,),
        in_specs=[pl.BlockSpec((h, C), lambda i, me_ref: (0, 0)),
                  pl.BlockSpec((3, h, C), lambda i, me_ref: (0, 0, 0)), ANY],
        out_specs=pl.BlockSpec((None, h, C), lambda i, me_ref: (l, me_ref[0], 0)))
    return _pallas(body, name=name, grid_spec=grid_spec, out_shape=_sds(buf.shape, F32),
                   input_output_aliases={3: 0},
                   compiler_params=_params(("arbitrary",), VMEM_BIG))(me, own, q, buf)


def share_final(bufs):
    n = len(bufs)

    def body(*refs):
        ins, outs = refs[:n], refs[n:2 * n]
        ssem, rsem = refs[2 * n:]
        del ins
        x, y, c = _coords()
        copies = []
        for t in range(n):
            R = bufs[t].shape[1]
            r0 = pl.multiple_of(c * (R // 2), 8)
            blk = outs[t].at[:, pl.ds(r0, R // 2), :]
            copies.append(pltpu.make_async_remote_copy(
                src_ref=blk, dst_ref=blk, send_sem=ssem.at[t], recv_sem=rsem.at[t],
                device_id=(x, y, 1 - c), device_id_type=MESH))
        for cp in copies:
            cp.start()
        for t in range(n):
            R = bufs[t].shape[1]
            r1 = pl.multiple_of((1 - c) * (R // 2), 8)
            other = outs[t].at[:, pl.ds(r1, R // 2), :]
            pltpu.make_async_remote_copy(
                src_ref=other, dst_ref=other, send_sem=ssem.at[t], recv_sem=rsem.at[t],
                device_id=(x, y, 1 - c), device_id_type=MESH).wait_recv()
        for cp in copies:
            cp.wait_send()

    out_shape = [_sds(b.shape, F32) for b in bufs]
    return _pallas(body, name="share_final", in_specs=[ANY] * n, out_specs=[ANY] * n, out_shape=out_shape,
                   input_output_aliases={t: t for t in range(n)},
                   scratch_shapes=[pltpu.SemaphoreType.DMA((n,))] * 2,
                   compiler_params=pltpu.CompilerParams(has_side_effects=True))(*bufs)


def allreduce_small(part):
    rows = part.shape[0]

    def body(p_ref, o_ref, buf, ssem, rsem, lsem):
        x, y, c = _coords()
        me = 4 * x + 2 * y + c
        mine = pltpu.make_async_copy(p_ref, buf.at[me], lsem)
        mine.start()
        copies = []
        for r in range(1, 8):
            fx, fy, fc = (r >> 2) & 1, (r >> 1) & 1, r & 1
            peer = ((1 - x) if fx else x, (1 - y) if fy else y, (1 - c) if fc else c)
            copies.append(pltpu.make_async_remote_copy(
                src_ref=p_ref, dst_ref=buf.at[me], send_sem=ssem.at[r - 1], recv_sem=rsem.at[r - 1],
                device_id=peer, device_id_type=MESH))
        for cp in copies:
            cp.start()
        for r in range(1, 8):
            fx, fy, fc = (r >> 2) & 1, (r >> 1) & 1, r & 1
            src = 4 * ((1 - x) if fx else x) + 2 * ((1 - y) if fy else y) + ((1 - c) if fc else c)
            pltpu.make_async_remote_copy(
                src_ref=buf.at[src], dst_ref=buf.at[src], send_sem=ssem.at[r - 1], recv_sem=rsem.at[r - 1],
                device_id=(x, y, c), device_id_type=MESH).wait_recv()
        for cp in copies:
            cp.wait_send()
        mine.wait()
        acc = buf[0]
        for d in range(1, 8):
            acc = acc + buf[d]
        o_ref[...] = acc

    return _pallas(body, name="allreduce_small",
                   in_specs=[pl.BlockSpec(memory_space=pltpu.VMEM)], out_specs=pl.BlockSpec(memory_space=pltpu.VMEM),
                   out_shape=_sds((rows, 128), F32),
                   scratch_shapes=[pltpu.VMEM((8, rows, 128), F32), pltpu.SemaphoreType.DMA((7,)),
                                   pltpu.SemaphoreType.DMA((7,)), pltpu.SemaphoreType.DMA],
                   compiler_params=pltpu.CompilerParams(has_side_effects=True, vmem_limit_bytes=VMEM_BIG))(part)


def _rows128(a):
    flat = a.reshape(-1)
    rows = -(-flat.shape[0] // 128)
    rows8 = -(-rows // 8) * 8
    flat = jnp.pad(flat, (0, rows8 * 128 - flat.shape[0]))
    return flat.reshape(rows8, 128)


def kernel(x, norm_mix_g, norm_ffn_g, final_g, a_w_in, a_v_gain, a_w_s, a_b_s, a_w_out, b_w_qkv, b_rel_bias, b_w_out, ffn_w_gate, ffn_w_up, ffn_w_down, loss_target, m_norm_mix_g, m_norm_ffn_g, m_final_g, m_a_w_in, m_a_v_gain, m_a_w_s, m_a_b_s, m_a_w_out, m_b_w_qkv, m_b_rel_bias, m_b_w_out, m_ffn_w_gate, m_ffn_w_up, m_ffn_w_down, v_norm_mix_g, v_norm_ffn_g, v_final_g, v_a_w_in, v_a_v_gain, v_a_w_s, v_a_b_s, v_a_w_out, v_b_w_qkv, v_b_rel_bias, v_b_w_out, v_ffn_w_gate, v_ffn_w_up, v_ffn_w_down):
    T = x.shape[1]
    weights = dict(norm_mix_g=norm_mix_g, norm_ffn_g=norm_ffn_g, final_g=final_g, a_w_in=a_w_in, a_v_gain=a_v_gain,
                   a_w_s=a_w_s, a_b_s=a_b_s, a_w_out=a_w_out, b_w_qkv=b_w_qkv, b_rel_bias=b_rel_bias,
                   b_w_out=b_w_out, ffn_w_gate=ffn_w_gate, ffn_w_up=ffn_w_up, ffn_w_down=ffn_w_down)
    mom_m = dict(norm_mix_g=m_norm_mix_g, norm_ffn_g=m_norm_ffn_g, final_g=m_final_g, a_w_in=m_a_w_in,
                 a_v_gain=m_a_v_gain, a_w_s=m_a_w_s, a_b_s=m_a_b_s, a_w_out=m_a_w_out, b_w_qkv=m_b_w_qkv,
                 b_rel_bias=m_b_rel_bias, b_w_out=m_b_w_out, ffn_w_gate=m_ffn_w_gate, ffn_w_up=m_ffn_w_up,
                 ffn_w_down=m_ffn_w_down)
    mom_v = dict(norm_mix_g=v_norm_mix_g, norm_ffn_g=v_norm_ffn_g, final_g=v_final_g, a_w_in=v_a_w_in,
                 a_v_gain=v_a_v_gain, a_w_s=v_a_w_s, a_b_s=v_a_b_s, a_w_out=v_a_w_out, b_w_qkv=v_b_w_qkv,
                 b_rel_bias=v_b_rel_bias, b_w_out=v_b_w_out, ffn_w_gate=v_ffn_w_gate, ffn_w_up=v_ffn_w_up,
                 ffn_w_down=v_ffn_w_down)
    order = list(weights)
    transposed = ("ffn_w_gate", "ffn_w_up")
    for k in transposed:
        weights[k], mom_m[k], mom_v[k] = (jnp.swapaxes(a, 1, 2) for a in (weights[k], mom_m[k], mom_v[k]))

    xi, yi, ci = _coords()
    me = jnp.stack([ci, 2 * xi + yi]).astype(jnp.int32)

    shard16 = {k: cast_bf16(weights[k], "cast_" + k) for k in BIG}

    def layer_tensors(i):
        mix = ("a_w_in", "a_w_out") if i % 2 == 0 else ("b_w_qkv", "b_w_out")
        return [(k, i // 2) for k in mix] + [(k, i) for k in ("ffn_w_gate", "ffn_w_up", "ffn_w_down")]

    def gather_items(keys):
        return [(shard16[k], l, GATHER_KIND[k]) for k, l in keys]

    def grad_metas(keys):
        return [(GATHER_KIND[k],) + tuple(weights[k].shape[1:]) for k, _ in keys]

    FFN = ("ffn_w_gate", "ffn_w_up", "ffn_w_down")
    k0a = [("a_w_out", 0), ("ffn_w_gate", 0)]
    k0b = [("ffn_w_up", 0), ("ffn_w_down", 0)]
    k1a = [("b_w_qkv", 0), ("b_w_out", 0), ("ffn_w_gate", 1)]
    k1b = [("ffn_w_up", 1), ("ffn_w_down", 1)]
    k3a = [("b_w_qkv", 1), ("b_w_out", 1), ("ffn_w_gate", 3)]
    k3b = [("ffn_w_up", 3), ("ffn_w_down", 3)]
    plans = {
        "a_in_l0": [("g1", k0a)], "sgu_fwd_l0": [("g2", k0a), ("g1", k0b)], "a_out_l0": [("g2", k0b)],
        "ffn_up_l0": [("g1", k1a)], "ffn_down_l0": [("g2", k1a), ("g1", k1b)], "b_qkv_l1": [("g2", k1b)],
        "attn_fwd_l1": [("g1", layer_tensors(2))], "b_out_l1": [("g2", layer_tensors(2))],
        "ffn_up_l1": [("g1", k3a)], "ffn_down_l1": [("g2", k3a)],
        "a_in_l2": [("g1", k3b)], "sgu_fwd_l2": [("g2", k3b)],
        "ffn_bwd_dh_l2": [("ex", layer_tensors(3))], "sgu_bwd_l2": [("sc", layer_tensors(3))],
        "ffn_bwd_dh_l1": [("ex", layer_tensors(2))], "attn_bwd_l1": [("sc", layer_tensors(2))],
        "ffn_bwd_dh_l0": [("ex", layer_tensors(1))], "ffn_bwd_dhn_l0": [("sc", k1a)],
        "a_out_bwd_l0": [("ex", [(k, 0) for k in FFN])],
        "sgu_bwd_l0": [("sc", k1b), ("ex", [("a_w_out", 0)])],
        "a_in_bwd_l0": [("sc", [("ffn_w_gate", 0), ("ffn_w_up", 0)])],
        "dw_in_l0": [("sc", [("ffn_w_down", 0), ("a_w_out", 0)])],
    }
    part16, full16 = {}, {}
    sib, p16, own_parts, recv_parts = {}, {}, {}, {}

    def make_comm(kind, keys):
        if kind == "g1":
            return gather_stage1(gather_items(keys)), lambda outs: part16.update(zip(keys, outs))
        if kind == "g2":
            return (gather_stage2(gather_items(keys), [part16[kl] for kl in keys]),
                    lambda outs: full16.update(zip(keys, outs)))
        if kind == "ex":
            return (exchange_halves([big_grads[k][l] for k, l in keys], grad_metas(keys)),
                    lambda outs: sib.update(zip(keys, outs)))
        for kl, m_ in zip(keys, grad_metas(keys)):
            p16[kl], own_parts[kl] = pair_sum(me, big_grads[kl[0]][kl[1]], sib[kl], m_, "pair_sum_%s_l%d" % kl)
        return (scatter_partials([p16[kl] for kl in keys], grad_metas(keys)),
                lambda outs: recv_parts.update(zip(keys, outs)))

    def run(name, make):
        steps = plans.get(name)
        if not steps:
            return make(None)
        made = [make_comm(kind, keys) for kind, keys in steps]
        main, outs = make(combine([c for c, _ in made]))
        for c, done in made:
            done(outs[:len(c.out_shapes)])
            outs = outs[len(c.out_shapes):]
        return main

    def weight(k, l):
        w = full16[(k, l)]
        if k == "a_w_out":
            return w.reshape(1, GH, D)
        return w.reshape(1, D, D) if k == "b_w_out" else w

    full16[("a_w_in", 0)] = run_comm(gather_both(gather_items([("a_w_in", 0)])), "gather_first")[0]

    xcur = x.reshape(T, D)
    hn = rms_fwd(xcur, norm_mix_g[0][None], "rms_mix_l0")
    saved = []
    for i in range(DEPTH):
        j = i // 2
        tag = "_l%d" % i
        st = {"x_in": xcur, "hn": hn}
        if i % 2 == 0:
            pre = run("a_in" + tag, lambda comm: matmul(
                "a_in" + tag, NN, hn, pl.BlockSpec((1024, D), lambda i_, j_: (i_, 0)),
                weight("a_w_in", j), pl.BlockSpec((None, D, 1024), lambda i_, j_: (0, 0, j_)),
                _sds((T, 2 * GH), BF16), pl.BlockSpec((1024, 1024), lambda i_, j_: (i_, j_)),
                (T // 1024, 4), comm=comm))
            y = run("sgu_fwd" + tag, lambda comm: sgu_fwd(
                pre, a_v_gain[j][None], a_w_s[j], a_b_s[j][:, :, None], "sgu_fwd" + tag, comm=comm))
            xmid, hn2 = run("a_out" + tag, lambda comm: residual_proj(
                "a_out" + tag, *out_proj(y, weight("a_w_out", j)), xcur, norm_ffn_g[i][None], comm=comm))
            st.update(pre=pre, y=y)
        else:
            qkvp = run("b_qkv" + tag, lambda comm: proj_qkv(hn, weight("b_w_qkv", j), 0, "b_qkv" + tag, comm=comm))
            wb = jnp.transpose(bias_build(b_rel_bias[j], "bias_build" + tag), (1, 0, 2))
            o = run("attn_fwd" + tag, lambda comm: attn_fwd(qkvp, wb, "attn_fwd" + tag, comm=comm))
            xmid, hn2 = run("b_out" + tag, lambda comm: residual_proj(
                "b_out" + tag, *out_proj(o, weight("b_w_out", j)), xcur, norm_ffn_g[i][None], comm=comm))
            st.update(qkvp=qkvp, wb=wb, o=o)
        g, u, h = run("ffn_up" + tag, lambda comm: ffn_up(
            hn2, weight("ffn_w_gate", i), weight("ffn_w_up", i), 0, "ffn_up" + tag, comm=comm))
        next_g = norm_mix_g[i + 1][None] if i + 1 < DEPTH else None
        down = run("ffn_down" + tag, lambda comm: residual_proj(
            "ffn_down" + tag, *ffn_down(h, weight("ffn_w_down", i)), xmid, next_g, comm=comm))
        xcur, hn = down if next_g is not None else (down, None)
        st.update(x_mid=xmid, hn2=hn2, g=g, u=u, h=h)
        saved.append(st)

    loss_part, dx, dxb, d_final = final_loss(xcur, final_g[None], loss_target.reshape(T, D), "final_loss")

    tk = min(2048, T)
    big_grads = {k: [None] * weights[k].shape[0] for k in BIG}
    small = {"norm_mix_g": [None] * DEPTH, "norm_ffn_g": [None] * DEPTH, "a_v_gain": [None] * 2,
             "a_w_s": [None] * 2, "a_b_s": [None] * 2, "b_rel_bias": [None] * 2}
    tok = lambda width: pl.BlockSpec((tk, width), lambda j_, k_: (k_, 0))
    part = lambda: pl.BlockSpec((None, tk, FS), lambda j_, k_: (j_, k_, 0))
    for i in reversed(range(DEPTH)):
        j = i // 2
        tag = "_l%d" % i
        st = saved[i]
        dg, du = run("ffn_bwd_dh" + tag, lambda comm: ffn_bwd_dh(
            dxb, weight("ffn_w_down", i), st["g"], st["u"], 0, "ffn_bwd_dh" + tag, comm=comm))
        big_grads["ffn_w_down"][i] = wgrad(
            "dw_down" + tag, st["h"], part(), dxb, tok(D), _sds((N_CHIPS, FS, D), F32),
            pl.BlockSpec((None, FS, D), lambda j_, k_: (j_, 0, 0)), N_CHIPS, T, tk)
        dx_mid, dxb_mid, dgn = run("ffn_bwd_dhn" + tag, lambda comm: dgrad_rms(
            "ffn_bwd_dhn" + tag, *ffn_dgrad(dg, du, weight("ffn_w_gate", i), weight("ffn_w_up", i)),
            st["x_mid"], norm_ffn_g[i][None], dx, comm=comm))
        for nm, dz in (("ffn_w_gate", dg), ("ffn_w_up", du)):
            big_grads[nm][i] = wgrad(
                "d" + nm + tag, dz, part(), st["hn2"], tok(D), _sds((N_CHIPS, FS, D), F32),
                pl.BlockSpec((None, FS, D), lambda j_, k_: (j_, 0, 0)), N_CHIPS, T, tk)
        dx, dxb = dx_mid, dxb_mid
        small["norm_ffn_g"][i] = dgn
        if i % 2 == 0:
            dy = run("a_out_bwd" + tag, lambda comm: matmul(
                "a_out_bwd" + tag, NT, dxb, pl.BlockSpec((1024, D), lambda i_, j_: (i_, 0)),
                weight("a_w_out", j), pl.BlockSpec((None, 1024, D), lambda i_, j_: (0, j_, 0)),
                _sds((T, GH), BF16), pl.BlockSpec((1024, 1024), lambda i_, j_: (i_, j_)), (T // 1024, 2), comm=comm))
            big_grads["a_w_out"][j] = wgrad(
                "dw_aout" + tag, st["y"], pl.BlockSpec((tk, 1024), lambda j_, k_: (k_, j_)), dxb, tok(D),
                _sds((GH, D), F32), pl.BlockSpec((1024, D), lambda j_, k_: (j_, 0)), 2, T, tk
            ).reshape(N_CHIPS, GH // N_CHIPS, D)
            dpre, d_ws, d_bs, d_gain = run("sgu_bwd" + tag, lambda comm: sgu_bwd(
                st["pre"], dy, a_v_gain[j][None], a_w_s[j], a_b_s[j][:, :, None], "sgu_bwd" + tag, comm=comm))
            small["a_w_s"][j], small["a_b_s"][j], small["a_v_gain"][j] = d_ws, d_bs, d_gain
            dx_in, dxb_in, dgn = run("a_in_bwd" + tag, lambda comm: dgrad_rms(
                "a_in_bwd" + tag, *in_dgrad(dpre, weight("a_w_in", j)),
                st["x_in"], norm_mix_g[i][None], dx, comm=comm))
            big_grads["a_w_in"][j] = run("dw_in" + tag, lambda comm: wgrad(
                "dw_in" + tag, st["hn"], tok(D), dpre, pl.BlockSpec((tk, 1024), lambda j_, k_: (k_, j_)),
                _sds((D, 2 * GH), F32), pl.BlockSpec((D, 1024), lambda j_, k_: (0, j_)), 4, T, tk, comm=comm))
        else:
            do = matmul("b_out_bwd" + tag, NT, dxb, pl.BlockSpec((1024, D), lambda i_, j_: (i_, 0)),
                        weight("b_w_out", j), pl.BlockSpec((None, D, D), lambda i_, j_: (0, 0, 0)),
                        _sds((T, D), BF16), pl.BlockSpec((1024, D), lambda i_, j_: (i_, 0)), (T // 1024, 1))
            big_grads["b_w_out"][j] = wgrad(
                "dw_bout" + tag, st["o"], tok(D), dxb, tok(D),
                _sds((D, D), F32), pl.BlockSpec((D, D), lambda j_, k_: (0, 0)), 1, T, tk
            ).reshape(N_CHIPS, D // N_CHIPS, D)
            dqkvp, dwb = run("attn_bwd" + tag, lambda comm: attn_bwd(
                st["qkvp"], st["o"], do, st["wb"], "attn_bwd" + tag, comm=comm))
            small["b_rel_bias"][j] = bias_grad(jnp.transpose(dwb, (1, 0, 2)), "bias_grad" + tag)
            dx_in, dxb_in, dgn = dgrad_rms(
                "b_qkv_bwd" + tag, *qkv_dgrad(dqkvp, weight("b_w_qkv", j)),
                st["x_in"], norm_mix_g[i][None], dx)
            big_grads["b_w_qkv"][j] = wgrad(
                "dw_qkv" + tag, st["hn"], tok(D), dqkvp,
                pl.BlockSpec((None, tk, D), lambda j_, k_: (j_, k_ + FRONT // tk, 0)),
                _sds((D, 3 * D), F32), pl.BlockSpec((D, D), lambda j_, k_: (0, j_)), 3, T, tk)
        dx, dxb = dx_in, dxb_in
        small["norm_mix_g"][i] = dgn

    small_grads = {
        "norm_mix_g": jnp.concatenate(small["norm_mix_g"], axis=0),
        "norm_ffn_g": jnp.concatenate(small["norm_ffn_g"], axis=0),
        "final_g": d_final.reshape(D),
        "a_v_gain": jnp.concatenate(small["a_v_gain"], axis=0),
        "a_w_s": jnp.stack(small["a_w_s"]),
        "a_b_s": jnp.stack(small["a_b_s"]).reshape(2, SGU_G, SGU_BLOCK),
        "b_rel_bias": jnp.stack(small["b_rel_bias"]),
    }
    small_names = list(small_grads)
    packed = [_rows128(small_grads[k]) for k in small_names] + [_rows128(loss_part[:, :1])]
    offs = [0]
    for p in packed:
        offs.append(offs[-1] + p.shape[0])
    reduced = allreduce_small(jnp.concatenate(packed, axis=0))
    grads = {}
    for t, k in enumerate(small_names):
        nelem = small_grads[k].size
        grads[k] = reduced[offs[t]:offs[t + 1]].reshape(-1)[:nelem].reshape(weights[k].shape)
    loss = reduced[offs[len(small_names)], 0]

    last = [("a_w_in", 0)]
    for kind, name in (("ex", "exchange_last"), ("sc", "scatter_last")):
        comm, done = make_comm(kind, last)
        done(run_comm(comm, name))
    bufs = {k: jnp.zeros(weights[k].shape, F32) for k in BIG}
    for i in range(DEPTH):
        for kl, m_ in zip(layer_tensors(i), grad_metas(layer_tensors(i))):
            bufs[kl[0]] = final_sum(me, own_parts[kl], recv_parts[kl], bufs[kl[0]], kl[1], m_,
                                    "final_sum_%s_l%d" % kl)
    shared = share_final([bufs[k] for k in BIG])
    for k, gfull in zip(BIG, shared):
        grads[k] = gfull

    delta, new_m, new_v = {}, {}, {}
    for k in order:
        shp = weights[k].shape
        if k in BIG:
            view = shp
        elif k == "a_w_s":
            view = (2, SGU_G * SGU_BLOCK, SGU_BLOCK)
        elif len(shp) == 1:
            view = (1, 1, shp[0])
        elif len(shp) == 2:
            view = (1,) + shp
        else:
            view = shp
        d_, m_, v_ = adamw(weights[k].reshape(view), grads[k].reshape(view), mom_m[k].reshape(view),
                           mom_v[k].reshape(view), "adamw_" + k)
        delta[k], new_m[k], new_v[k] = d_.reshape(shp), m_.reshape(shp), v_.reshape(shp)
    for k in transposed:
        for tree in (grads, delta, new_m, new_v):
            tree[k] = jnp.swapaxes(tree[k], 1, 2)

    return (loss, dx.reshape(1, T, D), *[grads[k] for k in order], *[delta[k] for k in order],
            *[new_m[k] for k in order], *[new_v[k] for k in order])
```

```python
import functools

import jax
import jax.numpy as jnp
from jax import lax
from jax.experimental import pallas as pl
from jax.experimental.pallas import tpu as pltpu

F32 = jnp.float32
BF16 = jnp.bfloat16
MESH = pl.DeviceIdType.MESH

D = 1024
DEPTH = 4
EPS = 1e-6
SGU_BLOCK = 128
GH = 2048
SGU_G = 8
SGU_GD = GH // SGU_G
N_HEADS = 16
HEAD_DIM = 64
CHUNK = 64
PAD = 8 * CHUNK
FRONT = 2048
QB = 128
KW = PAD + QB
N_REL = 192
REL_MIN = -(CHUNK - 1)
REL_MAX = 128
D_FF = 2816
FS = D_FF // 4
NEG = -1e30
SCALE = HEAD_DIM ** -0.5
N_CHIPS = 4

ADAM_LR = 0.001
ADAM_B1 = 0.9
ADAM_B2 = 0.999
ADAM_EPS = 1e-08
ADAM_WD = 0.01
ADAM_STEP = 10

VMEM_BIG = 56 * 1024 * 1024

NN = ((1,), (0,))
NT = ((1,), (1,))
TN = ((0,), (0,))


def _dot(a, b, dims):
    return lax.dot_general(a, b, (dims, ((), ())), preferred_element_type=F32)


class Comm:
    def __init__(self, ins, out_shapes, sems, start, wait, aliases=None):
        self.ins, self.out_shapes, self.sems = list(ins), list(out_shapes), list(sems)
        self.start, self.wait, self.aliases = start, wait, dict(aliases or {})


def _host(body, comm, kw):
    grid = tuple(kw["grid"])
    in_specs = list(kw["in_specs"])
    single = not isinstance(kw["out_specs"], (list, tuple))
    out_specs = [kw["out_specs"]] if single else list(kw["out_specs"])
    out_shape = [kw["out_shape"]] if single else list(kw["out_shape"])
    scratch = list(kw.get("scratch_shapes", ()))
    counts = (len(in_specs), len(comm.ins), len(out_specs), len(comm.out_shapes), len(scratch))

    def hosted(*refs):
        parts, p = [], 0
        for cnt in counts:
            parts.append(refs[p:p + cnt])
            p += cnt
        main_in, c_in, main_out, c_out, main_scr = parts
        sems = refs[p:]
        ids = [pl.program_id(a) for a in range(len(grid))]
        first = functools.reduce(jnp.logical_and, [i == 0 for i in ids])
        last = functools.reduce(jnp.logical_and, [i == n - 1 for i, n in zip(ids, grid)])
        pl.when(first)(lambda: comm.start(c_in, c_out, sems))
        body(*main_in, *main_out, *main_scr)
        pl.when(last)(lambda: comm.wait(c_in, c_out, sems))

    old = kw["compiler_params"]
    kw = dict(kw, in_specs=in_specs + [ANY] * len(comm.ins), out_specs=out_specs + [ANY] * len(comm.out_shapes),
              out_shape=out_shape + comm.out_shapes, scratch_shapes=scratch + comm.sems,
              compiler_params=pltpu.CompilerParams(dimension_semantics=("arbitrary",) * len(grid),
                                                   vmem_limit_bytes=old.vmem_limit_bytes, has_side_effects=True))
    if comm.aliases:
        kw["input_output_aliases"] = {counts[0] + i: counts[2] + o for i, o in comm.aliases.items()}
    return hosted, kw


def _pallas(body, comm=None, **kw):
    if comm is not None:
        body, kw = _host(body, comm, kw)
    return pl.pallas_call(body, **kw)


def _split_outs(outs, comm, n_main):
    outs = list(outs) if isinstance(outs, (list, tuple)) else [outs]
    main = outs[:n_main]
    return (main[0] if n_main == 1 else main), outs[n_main:]


def run_comm(comm, name):
    nci, nco = len(comm.ins), len(comm.out_shapes)

    def body(*refs):
        c_in, c_out, sems = refs[:nci], refs[nci:nci + nco], refs[nci + nco:]
        comm.start(c_in, c_out, sems)
        comm.wait(c_in, c_out, sems)

    kw = {}
    if comm.aliases:
        kw["input_output_aliases"] = dict(comm.aliases)
    return _pallas(body, name=name, in_specs=[ANY] * nci, out_specs=[ANY] * nco, out_shape=comm.out_shapes,
                   scratch_shapes=comm.sems, compiler_params=pltpu.CompilerParams(has_side_effects=True),
                   **kw)(*comm.ins)


def combine(comms):
    if len(comms) == 1:
        return comms[0]
    spans, ni, no, ns = [], 0, 0, 0
    for c in comms:
        spans.append((slice(ni, ni + len(c.ins)), slice(no, no + len(c.out_shapes)), slice(ns, ns + len(c.sems))))
        ni, no, ns = ni + len(c.ins), no + len(c.out_shapes), ns + len(c.sems)

    def start(ins, outs, sems):
        for c, (si, so, ss) in zip(comms, spans):
            c.start(ins[si], outs[so], sems[ss])

    def wait(ins, outs, sems):
        for c, (si, so, ss) in zip(comms, spans):
            c.wait(ins[si], outs[so], sems[ss])

    aliases = {}
    for c, (si, so, _) in zip(comms, spans):
        aliases.update({si.start + i: so.start + o for i, o in c.aliases.items()})
    return Comm([a for c in comms for a in c.ins], [o for c in comms for o in c.out_shapes],
                [s for c in comms for s in c.sems], start, wait, aliases)


def _call(body, comm, n_main, args, **kw):
    if comm is None:
        return _pallas(body, **kw)(*args)
    return _split_outs(_pallas(body, comm=comm, **kw)(*args, *comm.ins), comm, n_main)


def _params(sem=None, vmem=None):
    return pltpu.CompilerParams(dimension_semantics=sem, vmem_limit_bytes=vmem)


def _sds(shape, dtype):
    return jax.ShapeDtypeStruct(tuple(shape), dtype)


_GELU_C = 0.7978845608028654


def _gelu(x):
    t = jnp.tanh(_GELU_C * (x + 0.044715 * (x * x * x)))
    return 0.5 * x * (1.0 + t)


def _gelu_and_grad(x):
    x2 = x * x
    t = jnp.tanh(_GELU_C * (x + 0.044715 * (x2 * x)))
    val = 0.5 * x * (1.0 + t)
    grad = 0.5 * (1.0 + t) + 0.5 * x * (1.0 - t * t) * (_GELU_C * (1.0 + 3.0 * 0.044715 * x2))
    return val, grad


def _sigmoid(x):
    return 0.5 * (jnp.tanh(0.5 * x) + 1.0)


def cast_bf16(w, name):
    L, R, C = w.shape

    def body(w_ref, o_ref):
        o_ref[...] = w_ref[...].astype(BF16)

    spec = pl.BlockSpec((None, R, C), lambda l: (l, 0, 0))
    return _pallas(body, name=name, grid=(L,), in_specs=[spec], out_specs=spec,
                   out_shape=_sds((L, R, C), BF16), compiler_params=_params(("parallel",)))(w)


def rms_fwd(x, g, name, tm=512):
    T = x.shape[0]

    def body(x_ref, g_ref, o_ref):
        xf = x_ref[...]
        r = lax.rsqrt(jnp.mean(xf * xf, axis=-1, keepdims=True) + EPS)
        o_ref[...] = ((xf * r) * g_ref[...]).astype(BF16)

    row = pl.BlockSpec((tm, D), lambda i: (i, 0))
    return _pallas(body, name=name, grid=(T // tm,),
                   in_specs=[row, pl.BlockSpec((1, D), lambda i: (0, 0))], out_specs=row,
                   out_shape=_sds((T, D), BF16), compiler_params=_params(("parallel",)))(x, g)


def dgrad_rms(name, compute, args, specs, x, g, dres, tm=512, comm=None):
    T = x.shape[0]
    n = T // tm
    k = len(args)

    def body(*refs):
        x_ref, g_ref, dres_ref, dx_ref, dxb_ref, dg_ref, acc_ref = refs[k:]
        i = pl.program_id(0)
        xf = x_ref[...]
        r = lax.rsqrt(jnp.mean(xf * xf, axis=-1, keepdims=True) + EPS)
        xhat = xf * r
        dhf = compute(*refs[:k])
        part = (dhf * xhat).reshape(tm // 8, 8, D).sum(axis=0)

        @pl.when(i == 0)
        def _():
            acc_ref[...] = part

        @pl.when(i > 0)
        def _():
            acc_ref[...] += part

        dxhat = dhf * g_ref[...]
        dx = dres_ref[...] + r * (dxhat - xhat * jnp.mean(dxhat * xhat, axis=-1, keepdims=True))
        dx_ref[...] = dx
        dxb_ref[...] = dx.astype(BF16)

        @pl.when(i == n - 1)
        def _():
            dg_ref[...] = jnp.sum(acc_ref[...], axis=0, keepdims=True)

    row = pl.BlockSpec((tm, D), lambda i: (i, 0))
    vec = pl.BlockSpec((1, D), lambda i: (0, 0))
    return _call(body, comm, 3, (*args, x, g, dres), name=name, grid=(n,),
                 in_specs=list(specs) + [row, vec, row], out_specs=[row, row, vec],
                 out_shape=[_sds((T, D), F32), _sds((T, D), BF16), _sds((1, D), F32)],
                 scratch_shapes=[pltpu.VMEM((8, D), F32)],
                 compiler_params=_params(("arbitrary",), VMEM_BIG))


def final_loss(x, g, tgt, name, tm=256):
    T = x.shape[0]
    n = T // tm

    def body(x_ref, g_ref, t_ref, loss_ref, dx_ref, dxb_ref, dg_ref, acc_ref, lacc_ref):
        i = pl.program_id(0)
        xf = x_ref[...]
        r = lax.rsqrt(jnp.mean(xf * xf, axis=-1, keepdims=True) + EPS)
        xhat = xf * r
        gg = g_ref[...]
        e = xhat * gg - t_ref[...]
        dy = e * (1.0 / D)
        part = (dy * xhat).reshape(tm // 8, 8, D).sum(axis=0)
        lpart = (e * e).reshape(tm // 8, 8, D).sum(axis=0)

        @pl.when(i == 0)
        def _():
            acc_ref[...] = part
            lacc_ref[...] = lpart

        @pl.when(i > 0)
        def _():
            acc_ref[...] += part
            lacc_ref[...] += lpart

        dxhat = dy * gg
        dx = r * (dxhat - xhat * jnp.mean(dxhat * xhat, axis=-1, keepdims=True))
        dx_ref[...] = dx
        dxb_ref[...] = dx.astype(BF16)

        @pl.when(i == n - 1)
        def _():
            dg_ref[...] = jnp.sum(acc_ref[...], axis=0, keepdims=True)
            total = jnp.sum(jnp.sum(lacc_ref[...], axis=0, keepdims=True), axis=1, keepdims=True)
            loss_ref[...] = jnp.broadcast_to(total * (0.5 / D), (1, 128))

    row = pl.BlockSpec((tm, D), lambda i: (i, 0))
    vec = pl.BlockSpec((1, D), lambda i: (0, 0))
    return _pallas(body, name=name, grid=(n,), in_specs=[row, vec, row],
                   out_specs=[pl.BlockSpec((1, 128), lambda i: (0, 0)), row, row, vec],
                   out_shape=[_sds((1, 128), F32), _sds((T, D), F32), _sds((T, D), BF16), _sds((1, D), F32)],
                   scratch_shapes=[pltpu.VMEM((8, D), F32), pltpu.VMEM((8, D), F32)],
                   compiler_params=_params(("arbitrary",)))(x, g, tgt)


def matmul(name, dims, a, a_spec, b, b_spec, out_shape, out_spec, grid, *, acc=False, res=None, res_spec=None,
           comm=None):
    has_res = res is not None

    def body(*refs):
        a_ref, b_ref = refs[0], refs[1]
        r_ref = refs[2] if has_res else None
        o_ref = refs[-1]
        d = _dot(a_ref[...], b_ref[...], dims)
        if not acc:
            if has_res:
                d = d + r_ref[...]
            o_ref[...] = d.astype(o_ref.dtype)
        else:
            k = pl.program_id(len(grid) - 1)

            @pl.when(k == 0)
            def _():
                o_ref[...] = (d + r_ref[...]) if has_res else d

            @pl.when(k > 0)
            def _():
                o_ref[...] += d

    sem = ("parallel",) * (len(grid) - 1) + (("arbitrary",) if acc else ("parallel",))
    ins = [a, b] + ([res] if has_res else [])
    specs = [a_spec, b_spec] + ([res_spec] if has_res else [])
    return _call(body, comm, 1, ins, name=name, grid=grid, in_specs=specs, out_specs=out_spec, out_shape=out_shape,
                 compiler_params=_params(sem, VMEM_BIG))


def wgrad(name, a, a_spec, b, b_spec, out_shape, out_spec, J, T, tk, comm=None):
    return matmul(name, TN, a, a_spec, b, b_spec, out_shape, out_spec, (J, T // tk), acc=True, comm=comm)


def _sgu_mask():
    p = lax.broadcasted_iota(jnp.int32, (SGU_BLOCK, SGU_BLOCK), 0)
    q = lax.broadcasted_iota(jnp.int32, (SGU_BLOCK, SGU_BLOCK), 1)
    return lax.shift_right_logical(q, 6) <= lax.shift_right_logical(p, 6)


def sgu_fwd(pre, gain, w_s, b_s, name, comm=None):
    T = pre.shape[0]

    def body(pre_ref, gain_ref, ws_ref, bs_ref, y_ref):
        mask = _sgu_mask()
        u = _gelu(pre_ref[:, :GH].astype(F32))
        va = _gelu(pre_ref[:, GH:].astype(F32))
        r = lax.rsqrt(jnp.mean(va * va, axis=-1, keepdims=True) + EPS)
        vn = ((va * r) * gain_ref[...]).astype(BF16)
        for g in range(SGU_G):
            sl = slice(g * SGU_GD, (g + 1) * SGU_GD)
            wm = jnp.where(mask, ws_ref[g], 0.0).astype(BF16)
            vm = _dot(wm, vn[:, sl], NN) + bs_ref[g]
            y_ref[:, sl] = (u[:, sl] * vm).astype(BF16)

    return _call(
        body, comm, 1, (pre, gain, w_s, b_s), name=name, grid=(T // SGU_BLOCK,),
        in_specs=[pl.BlockSpec((SGU_BLOCK, 2 * GH), lambda i: (i, 0)),
                  pl.BlockSpec((1, GH), lambda i: (0, 0)),
                  pl.BlockSpec((SGU_G, SGU_BLOCK, SGU_BLOCK), lambda i: (0, 0, 0)),
                  pl.BlockSpec((SGU_G, SGU_BLOCK, 1), lambda i: (0, 0, 0))],
        out_specs=pl.BlockSpec((SGU_BLOCK, GH), lambda i: (i, 0)),
        out_shape=_sds((T, GH), BF16), compiler_params=_params(("parallel",)))


def sgu_bwd(pre, dy, gain, w_s, b_s, name, comm=None):
    T = pre.shape[0]
    n = T // SGU_BLOCK

    def body(pre_ref, dy_ref, gain_ref, ws_ref, bs_ref, dpre_ref, dws_ref, dbs_ref, dgain_ref, gacc_ref):
        i = pl.program_id(0)

        @pl.when(i == 0)
        def _():
            dws_ref[...] = jnp.zeros_like(dws_ref)
            dbs_ref[...] = jnp.zeros_like(dbs_ref)
            gacc_ref[...] = jnp.zeros_like(gacc_ref)

        mask = _sgu_mask()
        u, du_dpre = _gelu_and_grad(pre_ref[:, :GH].astype(F32))
        va, dva_dpre = _gelu_and_grad(pre_ref[:, GH:].astype(F32))
        r = lax.rsqrt(jnp.mean(va * va, axis=-1, keepdims=True) + EPS)
        vhat = va * r
        gain_v = gain_ref[...]
        vn = (vhat * gain_v).astype(BF16)
        dyf = dy_ref[...].astype(F32)
        dvn_parts = []
        for g in range(SGU_G):
            sl = slice(g * SGU_GD, (g + 1) * SGU_GD)
            wm = jnp.where(mask, ws_ref[g], 0.0).astype(BF16)
            vm = _dot(wm, vn[:, sl], NN) + bs_ref[g]
            dpre_ref[:, sl] = ((dyf[:, sl] * vm) * du_dpre[:, sl]).astype(BF16)
            dvm = dyf[:, sl] * u[:, sl]
            dbs_ref[g] += jnp.sum(dvm, axis=-1, keepdims=True)
            dvm16 = dvm.astype(BF16)
            dws_ref[g] += jnp.where(mask, _dot(dvm16, vn[:, sl], NT), 0.0)
            dvn_parts.append(_dot(wm, dvm16, TN))
        dvn = jnp.concatenate(dvn_parts, axis=-1)
        gacc_ref[...] += (dvn * vhat).reshape(SGU_BLOCK // 8, 8, GH).sum(axis=0)
        dvhat = dvn * gain_v
        dva = r * (dvhat - vhat * jnp.mean(dvhat * vhat, axis=-1, keepdims=True))
        dpre_ref[:, GH:] = (dva * dva_dpre).astype(BF16)

        @pl.when(i == n - 1)
        def _():
            dgain_ref[...] = jnp.sum(gacc_ref[...], axis=0, keepdims=True)

    const3 = lambda i: (0, 0, 0)
    return _call(
        body, comm, 4, (pre, dy, gain, w_s, b_s), name=name, grid=(n,),
        in_specs=[pl.BlockSpec((SGU_BLOCK, 2 * GH), lambda i: (i, 0)),
                  pl.BlockSpec((SGU_BLOCK, GH), lambda i: (i, 0)),
                  pl.BlockSpec((1, GH), lambda i: (0, 0)),
                  pl.BlockSpec((SGU_G, SGU_BLOCK, SGU_BLOCK), const3),
                  pl.BlockSpec((SGU_G, SGU_BLOCK, 1), const3)],
        out_specs=[pl.BlockSpec((SGU_BLOCK, 2 * GH), lambda i: (i, 0)),
                   pl.BlockSpec((SGU_G, SGU_BLOCK, SGU_BLOCK), const3),
                   pl.BlockSpec((SGU_G, SGU_BLOCK, 1), const3),
                   pl.BlockSpec((1, GH), lambda i: (0, 0))],
        out_shape=[_sds((T, 2 * GH), BF16), _sds((SGU_G, SGU_BLOCK, SGU_BLOCK), F32),
                   _sds((SGU_G, SGU_BLOCK, 1), F32), _sds((1, GH), F32)],
        scratch_shapes=[pltpu.VMEM((8, GH), F32)],
        compiler_params=_params(("arbitrary",)))


DIAG = 768


def _diag_onehot():
    n = lax.broadcasted_iota(jnp.int32, (N_REL, DIAG), 1)
    r = lax.broadcasted_iota(jnp.int32, (N_REL, DIAG), 0)
    idx = jnp.clip(KW - 1 - n, REL_MIN, REL_MAX) - REL_MIN
    return (idx == r).astype(BF16)


def _split3(v):
    hi = v.astype(BF16)
    r1 = v - hi.astype(F32)
    mid = r1.astype(BF16)
    lo = (r1 - mid.astype(F32)).astype(BF16)
    return hi, mid, lo


def bias_build(rel_bias, name):
    def body(rb_ref, o_ref):
        oh = _diag_onehot()
        hi, mid, lo = _split3(rb_ref[...])
        u = _dot(hi, oh, NN) + _dot(mid, oh, NN) + _dot(lo, oh, NN)
        j = lax.broadcasted_iota(jnp.int32, (1, KW), 1)

        def row(i, carry):
            val = pltpu.roll(u, (i + (DIAG - QB + 1)) % DIAG, 1)[:, :KW]
            rel = lax.shift_right_logical(i, 6) - lax.shift_right_logical(j, 6) + 8
            ok = (rel >= 0) & (rel <= 8)
            o_ref[i] = jnp.where(ok, val, NEG)
            return carry

        lax.fori_loop(0, QB, row, 0)

    return _pallas(body, name=name, out_shape=_sds((QB, N_HEADS, KW), F32),
                   in_specs=[pl.BlockSpec(memory_space=pltpu.VMEM)],
                   out_specs=pl.BlockSpec(memory_space=pltpu.VMEM))(rel_bias)


def bias_grad(dwb, name):
    def body(d_ref, o_ref):
        def row(i, acc):
            return acc + pltpu.roll(d_ref[i], QB - 1 - i, 1)

        du = lax.fori_loop(0, QB, row, jnp.zeros((N_HEADS, DIAG), F32))
        oh = _diag_onehot()
        hi, mid, lo = _split3(du)
        o_ref[...] = _dot(hi, oh, NT) + _dot(mid, oh, NT) + _dot(lo, oh, NT)

    return _pallas(body, name=name, out_shape=_sds((N_HEADS, N_REL), F32),
                   in_specs=[pl.BlockSpec(memory_space=pltpu.VMEM)],
                   out_specs=pl.BlockSpec(memory_space=pltpu.VMEM))(dwb)


def _attn_block(qkv_ref, blk):
    r0 = pl.multiple_of(blk * QB, QB)
    qs = qkv_ref[0, pl.ds(r0 + FRONT, QB), :] * SCALE
    k2 = qkv_ref[1, pl.ds(r0 + (FRONT - PAD), KW), :]
    v2 = qkv_ref[2, pl.ds(r0 + (FRONT - PAD), KW), :]
    col = lax.broadcasted_iota(jnp.int32, (1, KW), 1)
    return r0, qs, k2, v2, col >= PAD - blk * QB


def _head_mask(h):
    lane = lax.broadcasted_iota(jnp.int32, (1, 2 * HEAD_DIM), 1)
    return (lane < HEAD_DIM) if h == 0 else (lane >= HEAD_DIM)


def _stack_heads(a):
    zero = jnp.zeros_like(a)
    return jnp.concatenate([jnp.where(_head_mask(0), a, zero), jnp.where(_head_mask(1), a, zero)], axis=0)


def _unstack_heads(a):
    return jnp.where(_head_mask(0), a[:QB], a[QB:])


def _attn_exp(qst, k2, w_ref, kvalid):
    s = jnp.where(kvalid, _dot(qst, k2, NT) + w_ref[...].reshape(2 * QB, KW), NEG)
    e = jnp.exp(s - jnp.max(s, axis=-1, keepdims=True))
    return e, 1.0 / jnp.sum(e, axis=-1, keepdims=True)


ATTN_G = 4


def attn_fwd(qkvp, wb, name, comm=None):
    T = qkvp.shape[1] - FRONT

    def body(qkv_ref, w_ref, o_ref):
        b = pl.program_id(1)
        for t in range(ATTN_G):
            _, qs, k2, v2, kvalid = _attn_block(qkv_ref, b * ATTN_G + t)
            e, inv = _attn_exp(_stack_heads(qs), k2, w_ref, kvalid)
            o_ref[t * QB:(t + 1) * QB, :] = _unstack_heads(_dot(e.astype(BF16), v2, NN) * inv).astype(BF16)

    return _call(
        body, comm, 1, (qkvp, wb), name=name, grid=(N_HEADS // 2, T // (QB * ATTN_G)),
        in_specs=[pl.BlockSpec((3, FRONT + T, 2 * HEAD_DIM), lambda hp, b: (0, 0, hp)),
                  pl.BlockSpec((2, QB, KW), lambda hp, b: (hp, 0, 0))],
        out_specs=pl.BlockSpec((QB * ATTN_G, 2 * HEAD_DIM), lambda hp, b: (b, hp)),
        out_shape=_sds((T, D), BF16),
        compiler_params=_params(("parallel", "arbitrary"), VMEM_BIG))


def attn_bwd(qkvp, o, do, wb, name, comm=None):
    T = qkvp.shape[1] - FRONT
    nb = T // (QB * ATTN_G)

    def body(qkv_ref, o_ref, do_ref, w_ref, dqkv_ref, dw_ref, dk_acc, dv_acc):
        b = pl.program_id(1)

        @pl.when(b == 0)
        def _():
            dk_acc[...] = jnp.zeros_like(dk_acc)
            dv_acc[...] = jnp.zeros_like(dv_acc)
            dw_ref[...] = jnp.zeros_like(dw_ref)
            dqkv_ref[0, 0:FRONT, :] = jnp.zeros((FRONT, 2 * HEAD_DIM), BF16)

        dws = None
        for t in range(ATTN_G):
            r0, qs, k2, v2, kvalid = _attn_block(qkv_ref, b * ATTN_G + t)
            qst = _stack_heads(qs)
            e, inv = _attn_exp(qst, k2, w_ref, kvalid)
            do2 = do_ref[t * QB:(t + 1) * QB, :]
            dost = _stack_heads(do2)
            prod = _stack_heads(do2.astype(F32) * o_ref[t * QB:(t + 1) * QB, :].astype(F32))
            delta = jnp.sum(prod, axis=-1, keepdims=True)
            ds = e * ((_dot(dost, v2, NT) - delta) * inv)
            dws = ds if dws is None else dws + ds
            ds16 = ds.astype(BF16)
            dq = _unstack_heads(_dot(ds16, k2, NN)) * SCALE
            dqkv_ref[0, pl.ds(r0 + FRONT, QB), :] = dq.astype(BF16)
            dk_acc[pl.ds(r0 + (FRONT - PAD), KW), :] += _dot(ds16, qst, TN)
            dv_acc[pl.ds(r0 + (FRONT - PAD), KW), :] += _dot(e.astype(BF16), (dost.astype(F32) * inv).astype(BF16), TN)
        dw_ref[...] += dws.reshape(2, QB, KW)

        @pl.when(b == nb - 1)
        def _():
            dqkv_ref[1] = dk_acc[...].astype(BF16)
            dqkv_ref[2] = dv_acc[...].astype(BF16)

    slab = pl.BlockSpec((3, FRONT + T, 2 * HEAD_DIM), lambda hp, b: (0, 0, hp))
    wspec = pl.BlockSpec((2, QB, KW), lambda hp, b: (hp, 0, 0))
    rows = pl.BlockSpec((QB * ATTN_G, 2 * HEAD_DIM), lambda hp, b: (b, hp))
    return _call(
        body, comm, 2, (qkvp, o, do, wb), name=name, grid=(N_HEADS // 2, nb),
        in_specs=[slab, rows, rows, wspec],
        out_specs=[slab, wspec],
        out_shape=[_sds((3, FRONT + T, D), BF16), _sds((N_HEADS, QB, KW), F32)],
        scratch_shapes=[pltpu.VMEM((FRONT + T, 2 * HEAD_DIM), F32), pltpu.VMEM((FRONT + T, 2 * HEAD_DIM), F32)],
        compiler_params=_params(("parallel", "arbitrary"), VMEM_BIG))


def proj_qkv(hn, w, l, name, tm=512, comm=None):
    T = hn.shape[0]
    pb = FRONT // tm

    def body(a_ref, b_ref, o_ref):
        i = pl.program_id(1)

        @pl.when(i < pb)
        def _():
            o_ref[...] = jnp.zeros_like(o_ref)

        @pl.when(i >= pb)
        def _():
            o_ref[...] = _dot(a_ref[...], b_ref[...], NN).astype(BF16)

    return _call(
        body, comm, 1, (hn, w), name=name, grid=(3, pb + T // tm),
        in_specs=[pl.BlockSpec((tm, D), lambda p, i: (jnp.maximum(i - pb, 0), 0)),
                  pl.BlockSpec((None, D, D), lambda p, i: (l, 0, p))],
        out_specs=pl.BlockSpec((None, tm, D), lambda p, i: (p, i, 0)),
        out_shape=_sds((3, FRONT + T, D), BF16),
        compiler_params=_params(("parallel", "parallel"), VMEM_BIG))


def ffn_up(hn, wg, wu, l, name, tm=1024, comm=None):
    T = hn.shape[0]

    def body(a_ref, wg_ref, wu_ref, g_ref, u_ref, h_ref):
        a = a_ref[...]
        g = _dot(a, wg_ref[...], NT)
        u = _dot(a, wu_ref[...], NT)
        s = _sigmoid(g)
        silu = g * s
        g_ref[...] = (u * (s * (1.0 + g * (1.0 - s)))).astype(BF16)
        u_ref[...] = silu.astype(BF16)
        h_ref[...] = (silu * u).astype(BF16)

    wspec = pl.BlockSpec((None, None, FS, D), lambda s, i: (l, s, 0, 0))
    ospec = pl.BlockSpec((None, tm, FS), lambda s, i: (s, i, 0))
    return _call(
        body, comm, 3, (hn, wg, wu), name=name, grid=(N_CHIPS, T // tm),
        in_specs=[pl.BlockSpec((tm, D), lambda s, i: (i, 0)), wspec, wspec],
        out_specs=[ospec, ospec, ospec],
        out_shape=[_sds((N_CHIPS, T, FS), BF16)] * 3,
        compiler_params=_params(("parallel", "parallel"), VMEM_BIG))


def ffn_bwd_dh(dxb, wd, g, u, l, name, tm=2048, comm=None):
    T = dxb.shape[0]
    tm = min(tm, T)

    def body(a_ref, wd_ref, g_ref, u_ref, dg_ref, du_ref):
        dh = _dot(a_ref[...], wd_ref[...], NT)
        dg_ref[...] = (dh * g_ref[...].astype(F32)).astype(BF16)
        du_ref[...] = (dh * u_ref[...].astype(F32)).astype(BF16)

    aspec = pl.BlockSpec((None, tm, FS), lambda i, s: (s, i, 0))
    return _call(
        body, comm, 2, (dxb, wd, g, u), name=name, grid=(T // tm, N_CHIPS),
        in_specs=[pl.BlockSpec((tm, D), lambda i, s: (i, 0)),
                  pl.BlockSpec((None, None, FS, D), lambda i, s: (l, s, 0, 0)), aspec, aspec],
        out_specs=[aspec, aspec],
        out_shape=[_sds((N_CHIPS, T, FS), BF16)] * 2,
        compiler_params=_params(("parallel", "parallel"), VMEM_BIG))


def ffn_dgrad(dg, du, wg, wu, tm=512):
    def compute(dg_ref, du_ref, wg_ref, wu_ref):
        d = None
        for s in range(N_CHIPS):
            t = _dot(dg_ref[s], wg_ref[s], NN) + _dot(du_ref[s], wu_ref[s], NN)
            d = t if d is None else d + t
        return d

    aspec = pl.BlockSpec((N_CHIPS, tm, FS), lambda i: (0, i, 0))
    wspec = pl.BlockSpec((None, N_CHIPS, FS, D), lambda i: (0, 0, 0, 0), pipeline_mode=pl.Buffered(1))
    return compute, (dg, du, wg, wu), [aspec, aspec, wspec, wspec]


def qkv_dgrad(dqkvp, w, tm=512):
    def compute(a_ref, w_ref):
        d = None
        for p in range(3):
            t = _dot(a_ref[p], w_ref[:, p * D:(p + 1) * D], NT)
            d = t if d is None else d + t
        return d

    return compute, (dqkvp, w), [pl.BlockSpec((3, tm, D), lambda i: (0, i + FRONT // tm, 0)),
                                 pl.BlockSpec((None, D, 3 * D), lambda i: (0, 0, 0))]


def in_dgrad(dpre, w, tm=512):
    def compute(a_ref, w_ref):
        return _dot(a_ref[...], w_ref[...], NT)

    return compute, (dpre, w), [pl.BlockSpec((tm, 2 * GH), lambda i: (i, 0)),
                                pl.BlockSpec((None, D, 2 * GH), lambda i: (0, 0, 0))]


def _rms_rows(x, g):
    r = lax.rsqrt(jnp.mean(x * x, axis=-1, keepdims=True) + EPS)
    return ((x * r) * g).astype(BF16)


def residual_proj(name, compute, args, specs, res, norm_g, tm=512, comm=None):
    T = res.shape[0]
    k = len(args)
    with_norm = norm_g is not None

    def body(*refs):
        d = refs[k][...] + compute(*refs[:k])
        if with_norm:
            refs[k + 2][...] = d
            refs[k + 3][...] = _rms_rows(d, refs[k + 1][...])
        else:
            refs[k + 1][...] = d

    row = pl.BlockSpec((tm, D), lambda i: (i, 0))
    vec = pl.BlockSpec((1, D), lambda i: (0, 0))
    if with_norm:
        return _call(body, comm, 2, (*args, res, norm_g), name=name, grid=(T // tm,),
                     in_specs=list(specs) + [row, vec], out_specs=[row, row],
                     out_shape=[_sds((T, D), F32), _sds((T, D), BF16)],
                     compiler_params=_params(("parallel",), VMEM_BIG))
    return _call(body, comm, 1, (*args, res), name=name, grid=(T // tm,), in_specs=list(specs) + [row],
                 out_specs=row, out_shape=_sds((T, D), F32), compiler_params=_params(("parallel",), VMEM_BIG))


def ffn_down(h, wd, tm=512):
    def compute(h_ref, wd_ref):
        d = None
        for s in range(N_CHIPS):
            t = _dot(h_ref[s], wd_ref[s], NN)
            d = t if d is None else d + t
        return d

    return compute, (h, wd), [pl.BlockSpec((N_CHIPS, tm, FS), lambda i: (0, i, 0)),
                              pl.BlockSpec((None, N_CHIPS, FS, D), lambda i: (0, 0, 0, 0))]


def out_proj(a, w, tm=512):
    K = a.shape[1]

    def compute(a_ref, w_ref):
        return _dot(a_ref[...], w_ref[...], NN)

    return compute, (a, w), [pl.BlockSpec((tm, K), lambda i: (i, 0)), pl.BlockSpec((None, K, D), lambda i: (0, 0, 0))]


def adamw(w, g, m, v, name):
    L, R, C = w.shape

    def body(w_ref, g_ref, m_ref, v_ref, d_ref, nm_ref, nv_ref):
        gf = g_ref[...]
        nm = ADAM_B1 * m_ref[...] + (1.0 - ADAM_B1) * gf
        nv = ADAM_B2 * v_ref[...] + (1.0 - ADAM_B2) * (gf * gf)
        m_hat = nm / (1.0 - ADAM_B1 ** ADAM_STEP)
        v_hat = nv / (1.0 - ADAM_B2 ** ADAM_STEP)
        d_ref[...] = -ADAM_LR * (m_hat / (jnp.sqrt(v_hat) + ADAM_EPS) + ADAM_WD * w_ref[...])
        nm_ref[...] = nm
        nv_ref[...] = nv

    tr = R // 4 if R % 32 == 0 else R
    spec = pl.BlockSpec((None, tr, C), lambda l, r: (l, r, 0))
    return _pallas(body, name=name, grid=(L, R // tr), in_specs=[spec] * 4, out_specs=[spec] * 3,
                   out_shape=[_sds((L, R, C), F32)] * 3,
                   compiler_params=_params(("parallel", "parallel")))(w, g, m, v)


def _coords():
    return lax.axis_index("x"), lax.axis_index("y"), lax.axis_index("c")


def _other_chips(x, y):
    out = []
    for fx, fy in ((1, 0), (0, 1), (1, 1)):
        px = (1 - x) if fx else x
        py = (1 - y) if fy else y
        out.append((px, py))
    return out


def _flip_index(s, j):
    sx, sy = s // 2, s % 2
    fx, fy = ((1, 0), (0, 1), (1, 1))[j]
    return 2 * (sx ^ fx) + (sy ^ fy)


def _for_my_chip(sme, fn):
    for s in range(N_CHIPS):
        pl.when(sme == s)(functools.partial(fn, s))


ANY = pl.BlockSpec(memory_space=pl.ANY)

GATHER_KIND = {"a_w_in": "col", "b_w_qkv": "col", "a_w_out": "row", "b_w_out": "row",
               "ffn_w_gate": "row", "ffn_w_up": "row", "ffn_w_down": "row"}
BIG = tuple(GATHER_KIND)


def _gathered_shape(kind, shape):
    L, R, C = shape
    return (L, R, N_CHIPS * C) if kind == "col" else (L, N_CHIPS, R, C)


def _shard_rows(ref, kind, s, r0, rn, C):
    if kind == "col":
        return ref.at[:, pl.ds(r0, rn), s * C:(s + 1) * C]
    return ref.at[:, s, pl.ds(r0, rn), :]


def gather_stage1(items):
    n = len(items)
    dims = [it[0].shape[1:] for it in items]

    def copies(ins, outs, sems, s, with_landed=True):
        lsem, ssem, rsem = sems
        x, y, c = _coords()
        chips = _other_chips(x, y)
        local, send, landed = [], [], []
        for t, (_, li, kind) in enumerate(items):
            R, C = dims[t]
            r0 = pl.multiple_of(c * (R // 2), 8)
            local.append(pltpu.make_async_copy(ins[t].at[pl.ds(li, 1)], _shard_rows(outs[t], kind, s, 0, R, C),
                                               lsem.at[t]))
            for j in range(3):
                pair = dict(send_sem=ssem.at[3 * t + j], recv_sem=rsem.at[3 * t + j],
                            device_id=(chips[j][0], chips[j][1], c), device_id_type=MESH)
                send.append(pltpu.make_async_remote_copy(
                    src_ref=ins[t].at[pl.ds(li, 1), pl.ds(r0, R // 2), :],
                    dst_ref=_shard_rows(outs[t], kind, s, r0, R // 2, C), **pair))
                if with_landed:
                    got = _shard_rows(outs[t], kind, _flip_index(s, j), r0, R // 2, C)
                    landed.append(pltpu.make_async_remote_copy(src_ref=got, dst_ref=got, **pair))
        return local, send, landed

    def start(ins, outs, sems):
        def run(s):
            local, send, _ = copies(ins, outs, sems, s, with_landed=False)
            for cp in local + send:
                cp.start()
        x, y, _ = _coords()
        _for_my_chip(2 * x + y, run)

    def wait(ins, outs, sems):
        def run(s):
            local, send, landed = copies(ins, outs, sems, s)
            for cp in landed:
                cp.wait_recv()
            for cp in send:
                cp.wait_send()
            for cp in local:
                cp.wait()
        x, y, _ = _coords()
        _for_my_chip(2 * x + y, run)

    out_shapes = [_sds(_gathered_shape(kind, (1,) + tuple(dims[t])), BF16) for t, (_, _, kind) in enumerate(items)]
    sems = [pltpu.SemaphoreType.DMA((n,)), pltpu.SemaphoreType.DMA((3 * n,)), pltpu.SemaphoreType.DMA((3 * n,))]
    return Comm([it[0] for it in items], out_shapes, sems, start, wait)


def gather_stage2(items, gathered):
    n = len(items)
    dims = [it[0].shape[1:] for it in items]

    def copies(outs, sems, s, with_landed=True):
        ssem, rsem = sems
        x, y, c = _coords()
        send, landed = [], []
        for t, (_, _, kind) in enumerate(items):
            R, C = dims[t]
            for j in range(3):
                pair = dict(send_sem=ssem.at[3 * t + j], recv_sem=rsem.at[3 * t + j],
                            device_id=(x, y, 1 - c), device_id_type=MESH)
                mine = _shard_rows(outs[t], kind, _flip_index(s, j), pl.multiple_of(c * (R // 2), 8), R // 2, C)
                send.append(pltpu.make_async_remote_copy(src_ref=mine, dst_ref=mine, **pair))
                if with_landed:
                    other = _shard_rows(outs[t], kind, _flip_index(s, j), pl.multiple_of((1 - c) * (R // 2), 8),
                                        R // 2, C)
                    landed.append(pltpu.make_async_remote_copy(src_ref=other, dst_ref=other, **pair))
        return send, landed

    def start(ins, outs, sems):
        def run(s):
            for cp in copies(outs, sems, s, with_landed=False)[0]:
                cp.start()
        x, y, _ = _coords()
        _for_my_chip(2 * x + y, run)

    def wait(ins, outs, sems):
        def run(s):
            send, landed = copies(outs, sems, s)
            for cp in landed:
                cp.wait_recv()
            for cp in send:
                cp.wait_send()
        x, y, _ = _coords()
        _for_my_chip(2 * x + y, run)

    out_shapes = [_sds(g.shape, BF16) for g in gathered]
    sems = [pltpu.SemaphoreType.DMA((3 * n,)), pltpu.SemaphoreType.DMA((3 * n,))]
    return Comm(gathered, out_shapes, sems, start, wait, aliases={t: t for t in range(n)})


def gather_both(items):
    s1 = gather_stage1(items)
    s2 = gather_stage2(items, s1.out_shapes)
    n1 = len(s1.sems)

    def wait(ins, outs, sems):
        s1.wait(ins, outs, sems[:n1])
        s2.start((), outs, sems[n1:])
        s2.wait((), outs, sems[n1:])

    return Comm(s1.ins, s1.out_shapes, s1.sems + s2.sems, lambda ins, outs, sems: s1.start(ins, outs, sems[:n1]), wait)


def _half_shape(kind, R, C):
    return (R // 2, N_CHIPS * C) if kind == "col" else (N_CHIPS, R // 2, C)


def exchange_halves(grads, metas):
    n = len(grads)

    def copies(ins, outs, sems):
        ssem, rsem = sems
        x, y, c = _coords()
        out = []
        for t, (kind, R, C) in enumerate(metas):
            r0 = pl.multiple_of((1 - c) * (R // 2), 8)
            src = ins[t].at[pl.ds(r0, R // 2), :] if kind == "col" else ins[t].at[:, pl.ds(r0, R // 2), :]
            out.append(pltpu.make_async_remote_copy(
                src_ref=src, dst_ref=outs[t], send_sem=ssem.at[t], recv_sem=rsem.at[t],
                device_id=(x, y, 1 - c), device_id_type=MESH))
        return out

    def start(ins, outs, sems):
        for cp in copies(ins, outs, sems):
            cp.start()

    def wait(ins, outs, sems):
        for cp in copies(ins, outs, sems):
            cp.wait()

    return Comm(grads, [_sds(_half_shape(*m), F32) for m in metas], [pltpu.SemaphoreType.DMA((n,))] * 2, start, wait)


def pair_sum(me, g, sib, meta, name):
    kind, R, C = meta
    h = R // 2

    def body(me_ref, g_ref, sib_ref, p16_ref, own_ref):
        s = pl.program_id(0)
        v = g_ref[...] + sib_ref[...]
        p16_ref[...] = v.astype(BF16)

        @pl.when(s == me_ref[1])
        def _():
            own_ref[...] = v

    if kind == "col":
        gspec = pl.BlockSpec((h, C), lambda s, me_ref: (me_ref[0], s))
        sspec = pl.BlockSpec((h, C), lambda s, me_ref: (0, s))
    else:
        gspec = pl.BlockSpec((None, h, C), lambda s, me_ref: (s, me_ref[0], 0))
        sspec = pl.BlockSpec((None, h, C), lambda s, me_ref: (s, 0, 0))
    grid_spec = pltpu.PrefetchScalarGridSpec(
        num_scalar_prefetch=1, grid=(N_CHIPS,), in_specs=[gspec, sspec],
        out_specs=[sspec, pl.BlockSpec((h, C), lambda s, me_ref: (0, 0))])
    return _pallas(body, name=name, grid_spec=grid_spec,
                   out_shape=[_sds(_half_shape(*meta), BF16), _sds((h, C), F32)],
                   compiler_params=_params(("arbitrary",), VMEM_BIG))(me, g, sib)


def scatter_partials(p16s, metas):
    n = len(p16s)

    def copies(ins, outs, sems, s):
        ssem, rsem = sems
        x, y, c = _coords()
        chips = _other_chips(x, y)
        out = []
        for t, (kind, R, C) in enumerate(metas):
            for j in range(3):
                sj = _flip_index(s, j)
                src = ins[t].at[:, sj * C:(sj + 1) * C] if kind == "col" else ins[t].at[sj]
                out.append(pltpu.make_async_remote_copy(
                    src_ref=src, dst_ref=outs[t].at[j], send_sem=ssem.at[3 * t + j], recv_sem=rsem.at[3 * t + j],
                    device_id=(chips[j][0], chips[j][1], c), device_id_type=MESH))
        return out

    def start(ins, outs, sems):
        def run(s):
            for cp in copies(ins, outs, sems, s):
                cp.start()
        x, y, _ = _coords()
        _for_my_chip(2 * x + y, run)

    def wait(ins, outs, sems):
        def run(s):
            for cp in copies(ins, outs, sems, s):
                cp.wait()
        x, y, _ = _coords()
        _for_my_chip(2 * x + y, run)

    return Comm(p16s, [_sds((3, R // 2, C), BF16) for (_, R, C) in metas],
                [pltpu.SemaphoreType.DMA((3 * n,))] * 2, start, wait)


def final_sum(me, own, q, buf, l, meta, name):
    _, R, C = meta
    h = R // 2

    def body(me_ref, own_ref, q_ref, buf_ref, o_ref):
        del buf_ref
        o_ref[...] = ((own_ref[...] + q_ref[0].astype(F32)) + q_ref[1].astype(F32)) + q_ref[2].astype(F32)

    grid_spec = pltpu.PrefetchScalarGridSpec(
        num_scalar_prefetch=1, grid=(1,),
        in_specs=[pl.BlockSpec((h, C), lambda i, me_ref: (0, 0)),
                  pl.BlockSpec((3, h, C), lambda i, me_ref: (0, 0, 0)), ANY],
        out_specs=pl.BlockSpec((None, h, C), lambda i, me_ref: (l, me_ref[0], 0)))
    return _pallas(body, name=name, grid_spec=grid_spec, out_shape=_sds(buf.shape, F32),
                   input_output_aliases={3: 0},
                   compiler_params=_params(("arbitrary",), VMEM_BIG))(me, own, q, buf)


def share_final(bufs):
    n = len(bufs)

    def body(*refs):
        ins, outs = refs[:n], refs[n:2 * n]
        ssem, rsem = refs[2 * n:]
        del ins
        x, y, c = _coords()
        copies = []
        for t in range(n):
            R = bufs[t].shape[1]
            r0 = pl.multiple_of(c * (R // 2), 8)
            blk = outs[t].at[:, pl.ds(r0, R // 2), :]
            copies.append(pltpu.make_async_remote_copy(
                src_ref=blk, dst_ref=blk, send_sem=ssem.at[t], recv_sem=rsem.at[t],
                device_id=(x, y, 1 - c), device_id_type=MESH))
        for cp in copies:
            cp.start()
        for t in range(n):
            R = bufs[t].shape[1]
            r1 = pl.multiple_of((1 - c) * (R // 2), 8)
            other = outs[t].at[:, pl.ds(r1, R // 2), :]
            pltpu.make_async_remote_copy(
                src_ref=other, dst_ref=other, send_sem=ssem.at[t], recv_sem=rsem.at[t],
                device_id=(x, y, 1 - c), device_id_type=MESH).wait_recv()
        for cp in copies:
            cp.wait_send()

    out_shape = [_sds(b.shape, F32) for b in bufs]
    return _pallas(body, name="share_final", in_specs=[ANY] * n, out_specs=[ANY] * n, out_shape=out_shape,
                   input_output_aliases={t: t for t in range(n)},
                   scratch_shapes=[pltpu.SemaphoreType.DMA((n,))] * 2,
                   compiler_params=pltpu.CompilerParams(has_side_effects=True))(*bufs)


def allreduce_small(part):
    rows = part.shape[0]
    h = rows // 2

    def body(p_ref, o_ref, sib_buf, pair_buf, chip_buf, ssem, rsem):
        x, y, c = _coords()
        sibling = dict(device_id=(x, y, 1 - c), device_id_type=MESH)
        mine = pl.ds(pl.multiple_of(c * h, 8), h)
        theirs = pl.ds(pl.multiple_of((1 - c) * h, 8), h)

        swap = pltpu.make_async_remote_copy(src_ref=p_ref.at[theirs], dst_ref=sib_buf, send_sem=ssem.at[0],
                                            recv_sem=rsem.at[0], **sibling)
        swap.start()
        swap.wait()
        pair_buf[...] = p_ref[mine, :] + sib_buf[...]

        chips = _other_chips(x, y)
        sends = [pltpu.make_async_remote_copy(src_ref=pair_buf, dst_ref=chip_buf.at[j], send_sem=ssem.at[1 + j],
                                              recv_sem=rsem.at[1 + j], device_id=(chips[j][0], chips[j][1], c),
                                              device_id_type=MESH) for j in range(3)]
        for cp in sends:
            cp.start()
        for cp in sends:
            cp.wait()

        def total(s):
            terms = {s: pair_buf[...]}
            for j in range(3):
                terms[_flip_index(s, j)] = chip_buf[j]
            o_ref[mine, :] = ((terms[0] + terms[1]) + terms[2]) + terms[3]

        _for_my_chip(2 * x + y, total)

        back = pltpu.make_async_remote_copy(src_ref=o_ref.at[mine], dst_ref=o_ref.at[mine], send_sem=ssem.at[4],
                                            recv_sem=rsem.at[4], **sibling)
        back.start()
        pltpu.make_async_remote_copy(src_ref=o_ref.at[theirs], dst_ref=o_ref.at[theirs], send_sem=ssem.at[4],
                                     recv_sem=rsem.at[4], **sibling).wait_recv()
        back.wait_send()

    return _pallas(body, name="allreduce_small",
                   in_specs=[pl.BlockSpec(memory_space=pltpu.VMEM)], out_specs=pl.BlockSpec(memory_space=pltpu.VMEM),
                   out_shape=_sds((rows, 128), F32),
                   scratch_shapes=[pltpu.VMEM((h, 128), F32), pltpu.VMEM((h, 128), F32), pltpu.VMEM((3, h, 128), F32),
                                   pltpu.SemaphoreType.DMA((5,)), pltpu.SemaphoreType.DMA((5,))],
                   compiler_params=pltpu.CompilerParams(has_side_effects=True))(part)


def _rows128(a):
    flat = a.reshape(-1)
    rows = -(-flat.shape[0] // 128)
    rows8 = -(-rows // 8) * 8
    flat = jnp.pad(flat, (0, rows8 * 128 - flat.shape[0]))
    return flat.reshape(rows8, 128)


def kernel(x, norm_mix_g, norm_ffn_g, final_g, a_w_in, a_v_gain, a_w_s, a_b_s, a_w_out, b_w_qkv, b_rel_bias, b_w_out, ffn_w_gate, ffn_w_up, ffn_w_down, loss_target, m_norm_mix_g, m_norm_ffn_g, m_final_g, m_a_w_in, m_a_v_gain, m_a_w_s, m_a_b_s, m_a_w_out, m_b_w_qkv, m_b_rel_bias, m_b_w_out, m_ffn_w_gate, m_ffn_w_up, m_ffn_w_down, v_norm_mix_g, v_norm_ffn_g, v_final_g, v_a_w_in, v_a_v_gain, v_a_w_s, v_a_b_s, v_a_w_out, v_b_w_qkv, v_b_rel_bias, v_b_w_out, v_ffn_w_gate, v_ffn_w_up, v_ffn_w_down):
    T = x.shape[1]
    weights = dict(norm_mix_g=norm_mix_g, norm_ffn_g=norm_ffn_g, final_g=final_g, a_w_in=a_w_in, a_v_gain=a_v_gain,
                   a_w_s=a_w_s, a_b_s=a_b_s, a_w_out=a_w_out, b_w_qkv=b_w_qkv, b_rel_bias=b_rel_bias,
                   b_w_out=b_w_out, ffn_w_gate=ffn_w_gate, ffn_w_up=ffn_w_up, ffn_w_down=ffn_w_down)
    mom_m = dict(norm_mix_g=m_norm_mix_g, norm_ffn_g=m_norm_ffn_g, final_g=m_final_g, a_w_in=m_a_w_in,
                 a_v_gain=m_a_v_gain, a_w_s=m_a_w_s, a_b_s=m_a_b_s, a_w_out=m_a_w_out, b_w_qkv=m_b_w_qkv,
                 b_rel_bias=m_b_rel_bias, b_w_out=m_b_w_out, ffn_w_gate=m_ffn_w_gate, ffn_w_up=m_ffn_w_up,
                 ffn_w_down=m_ffn_w_down)
    mom_v = dict(norm_mix_g=v_norm_mix_g, norm_ffn_g=v_norm_ffn_g, final_g=v_final_g, a_w_in=v_a_w_in,
                 a_v_gain=v_a_v_gain, a_w_s=v_a_w_s, a_b_s=v_a_b_s, a_w_out=v_a_w_out, b_w_qkv=v_b_w_qkv,
                 b_rel_bias=v_b_rel_bias, b_w_out=v_b_w_out, ffn_w_gate=v_ffn_w_gate, ffn_w_up=v_ffn_w_up,
                 ffn_w_down=v_ffn_w_down)
    order = list(weights)
    transposed = ("ffn_w_gate", "ffn_w_up")
    for k in transposed:
        weights[k], mom_m[k], mom_v[k] = (jnp.swapaxes(a, 1, 2) for a in (weights[k], mom_m[k], mom_v[k]))

    xi, yi, ci = _coords()
    me = jnp.stack([ci, 2 * xi + yi]).astype(jnp.int32)

    shard16 = {k: cast_bf16(weights[k], "cast_" + k) for k in BIG}

    def layer_tensors(i):
        mix = ("a_w_in", "a_w_out") if i % 2 == 0 else ("b_w_qkv", "b_w_out")
        return [(k, i // 2) for k in mix] + [(k, i) for k in ("ffn_w_gate", "ffn_w_up", "ffn_w_down")]

    def gather_items(keys):
        return [(shard16[k], l, GATHER_KIND[k]) for k, l in keys]

    def grad_metas(keys):
        return [(GATHER_KIND[k],) + tuple(weights[k].shape[1:]) for k, _ in keys]

    FFN = ("ffn_w_gate", "ffn_w_up", "ffn_w_down")
    k0a = [("a_w_out", 0), ("ffn_w_gate", 0)]
    k0b = [("ffn_w_up", 0), ("ffn_w_down", 0)]
    k1a = [("b_w_qkv", 0), ("b_w_out", 0), ("ffn_w_gate", 1)]
    k1b = [("ffn_w_up", 1), ("ffn_w_down", 1)]
    k3a = [("b_w_qkv", 1), ("b_w_out", 1), ("ffn_w_gate", 3)]
    k3b = [("ffn_w_up", 3), ("ffn_w_down", 3)]
    plans = {
        "a_in_l0": [("g1", k0a)], "sgu_fwd_l0": [("g2", k0a), ("g1", k0b)], "a_out_l0": [("g2", k0b)],
        "ffn_up_l0": [("g1", k1a)], "ffn_down_l0": [("g2", k1a), ("g1", k1b)], "b_qkv_l1": [("g2", k1b)],
        "attn_fwd_l1": [("g1", layer_tensors(2))], "b_out_l1": [("g2", layer_tensors(2))],
        "ffn_up_l1": [("g1", k3a)], "ffn_down_l1": [("g2", k3a)],
        "a_in_l2": [("g1", k3b)], "sgu_fwd_l2": [("g2", k3b)],
        "ffn_bwd_dh_l2": [("ex", layer_tensors(3))], "sgu_bwd_l2": [("sc", layer_tensors(3))],
        "ffn_bwd_dh_l1": [("ex", layer_tensors(2))], "attn_bwd_l1": [("sc", layer_tensors(2))],
        "ffn_bwd_dh_l0": [("ex", layer_tensors(1))], "ffn_bwd_dhn_l0": [("sc", k1a)],
        "a_out_bwd_l0": [("ex", [(k, 0) for k in FFN])],
        "sgu_bwd_l0": [("sc", k1b), ("ex", [("a_w_out", 0)])],
        "a_in_bwd_l0": [("sc", [("ffn_w_gate", 0), ("ffn_w_up", 0)])],
        "dw_in_l0": [("sc", [("ffn_w_down", 0), ("a_w_out", 0)])],
    }
    part16, full16 = {}, {}
    sib, p16, own_parts, recv_parts = {}, {}, {}, {}

    def make_comm(kind, keys):
        if kind == "g1":
            return gather_stage1(gather_items(keys)), lambda outs: part16.update(zip(keys, outs))
        if kind == "g2":
            return (gather_stage2(gather_items(keys), [part16[kl] for kl in keys]),
                    lambda outs: full16.update(zip(keys, outs)))
        if kind == "ex":
            return (exchange_halves([big_grads[k][l] for k, l in keys], grad_metas(keys)),
                    lambda outs: sib.update(zip(keys, outs)))
        for kl, m_ in zip(keys, grad_metas(keys)):
            p16[kl], own_parts[kl] = pair_sum(me, big_grads[kl[0]][kl[1]], sib[kl], m_, "pair_sum_%s_l%d" % kl)
        return (scatter_partials([p16[kl] for kl in keys], grad_metas(keys)),
                lambda outs: recv_parts.update(zip(keys, outs)))

    def run(name, make):
        steps = plans.get(name)
        if not steps:
            return make(None)
        made = [make_comm(kind, keys) for kind, keys in steps]
        main, outs = make(combine([c for c, _ in made]))
        for c, done in made:
            done(outs[:len(c.out_shapes)])
            outs = outs[len(c.out_shapes):]
        return main

    def weight(k, l):
        w = full16[(k, l)]
        if k == "a_w_out":
            return w.reshape(1, GH, D)
        return w.reshape(1, D, D) if k == "b_w_out" else w

    full16[("a_w_in", 0)] = run_comm(gather_both(gather_items([("a_w_in", 0)])), "gather_first")[0]

    xcur = x.reshape(T, D)
    hn = rms_fwd(xcur, norm_mix_g[0][None], "rms_mix_l0")
    saved = []
    for i in range(DEPTH):
        j = i // 2
        tag = "_l%d" % i
        st = {"x_in": xcur, "hn": hn}
        if i % 2 == 0:
            pre = run("a_in" + tag, lambda comm: matmul(
                "a_in" + tag, NN, hn, pl.BlockSpec((1024, D), lambda i_, j_: (i_, 0)),
                weight("a_w_in", j), pl.BlockSpec((None, D, 1024), lambda i_, j_: (0, 0, j_)),
                _sds((T, 2 * GH), BF16), pl.BlockSpec((1024, 1024), lambda i_, j_: (i_, j_)),
                (T // 1024, 4), comm=comm))
            y = run("sgu_fwd" + tag, lambda comm: sgu_fwd(
                pre, a_v_gain[j][None], a_w_s[j], a_b_s[j][:, :, None], "sgu_fwd" + tag, comm=comm))
            xmid, hn2 = run("a_out" + tag, lambda comm: residual_proj(
                "a_out" + tag, *out_proj(y, weight("a_w_out", j)), xcur, norm_ffn_g[i][None], comm=comm))
            st.update(pre=pre, y=y)
        else:
            qkvp = run("b_qkv" + tag, lambda comm: proj_qkv(hn, weight("b_w_qkv", j), 0, "b_qkv" + tag, comm=comm))
            wb = jnp.transpose(bias_build(b_rel_bias[j], "bias_build" + tag), (1, 0, 2))
            o = run("attn_fwd" + tag, lambda comm: attn_fwd(qkvp, wb, "attn_fwd" + tag, comm=comm))
            xmid, hn2 = run("b_out" + tag, lambda comm: residual_proj(
                "b_out" + tag, *out_proj(o, weight("b_w_out", j)), xcur, norm_ffn_g[i][None], comm=comm))
            st.update(qkvp=qkvp, wb=wb, o=o)
        g, u, h = run("ffn_up" + tag, lambda comm: ffn_up(
            hn2, weight("ffn_w_gate", i), weight("ffn_w_up", i), 0, "ffn_up" + tag, comm=comm))
        next_g = norm_mix_g[i + 1][None] if i + 1 < DEPTH else None
        down = run("ffn_down" + tag, lambda comm: residual_proj(
            "ffn_down" + tag, *ffn_down(h, weight("ffn_w_down", i)), xmid, next_g, comm=comm))
        xcur, hn = down if next_g is not None else (down, None)
        st.update(x_mid=xmid, hn2=hn2, g=g, u=u, h=h)
        saved.append(st)

    loss_part, dx, dxb, d_final = final_loss(xcur, final_g[None], loss_target.reshape(T, D), "final_loss")

    tk = min(2048, T)
    big_grads = {k: [None] * weights[k].shape[0] for k in BIG}
    small = {"norm_mix_g": [None] * DEPTH, "norm_ffn_g": [None] * DEPTH, "a_v_gain": [None] * 2,
             "a_w_s": [None] * 2, "a_b_s": [None] * 2, "b_rel_bias": [None] * 2}
    tok = lambda width: pl.BlockSpec((tk, width), lambda j_, k_: (k_, 0))
    part = lambda: pl.BlockSpec((None, tk, FS), lambda j_, k_: (j_, k_, 0))
    for i in reversed(range(DEPTH)):
        j = i // 2
        tag = "_l%d" % i
        st = saved[i]
        dg, du = run("ffn_bwd_dh" + tag, lambda comm: ffn_bwd_dh(
            dxb, weight("ffn_w_down", i), st["g"], st["u"], 0, "ffn_bwd_dh" + tag, comm=comm))
        big_grads["ffn_w_down"][i] = wgrad(
            "dw_down" + tag, st["h"], part(), dxb, tok(D), _sds((N_CHIPS, FS, D), F32),
            pl.BlockSpec((None, FS, D), lambda j_, k_: (j_, 0, 0)), N_CHIPS, T, tk)
        dx_mid, dxb_mid, dgn = run("ffn_bwd_dhn" + tag, lambda comm: dgrad_rms(
            "ffn_bwd_dhn" + tag, *ffn_dgrad(dg, du, weight("ffn_w_gate", i), weight("ffn_w_up", i)),
            st["x_mid"], norm_ffn_g[i][None], dx, comm=comm))
        for nm, dz in (("ffn_w_gate", dg), ("ffn_w_up", du)):
            big_grads[nm][i] = wgrad(
                "d" + nm + tag, dz, part(), st["hn2"], tok(D), _sds((N_CHIPS, FS, D), F32),
                pl.BlockSpec((None, FS, D), lambda j_, k_: (j_, 0, 0)), N_CHIPS, T, tk)
        dx, dxb = dx_mid, dxb_mid
        small["norm_ffn_g"][i] = dgn
        if i % 2 == 0:
            dy = run("a_out_bwd" + tag, lambda comm: matmul(
                "a_out_bwd" + tag, NT, dxb, pl.BlockSpec((1024, D), lambda i_, j_: (i_, 0)),
                weight("a_w_out", j), pl.BlockSpec((None, 1024, D), lambda i_, j_: (0, j_, 0)),
                _sds((T, GH), BF16), pl.BlockSpec((1024, 1024), lambda i_, j_: (i_, j_)), (T // 1024, 2), comm=comm))
            big_grads["a_w_out"][j] = wgrad(
                "dw_aout" + tag, st["y"], pl.BlockSpec((tk, 1024), lambda j_, k_: (k_, j_)), dxb, tok(D),
                _sds((GH, D), F32), pl.BlockSpec((1024, D), lambda j_, k_: (j_, 0)), 2, T, tk
            ).reshape(N_CHIPS, GH // N_CHIPS, D)
            dpre, d_ws, d_bs, d_gain = run("sgu_bwd" + tag, lambda comm: sgu_bwd(
                st["pre"], dy, a_v_gain[j][None], a_w_s[j], a_b_s[j][:, :, None], "sgu_bwd" + tag, comm=comm))
            small["a_w_s"][j], small["a_b_s"][j], small["a_v_gain"][j] = d_ws, d_bs, d_gain
            dx_in, dxb_in, dgn = run("a_in_bwd" + tag, lambda comm: dgrad_rms(
                "a_in_bwd" + tag, *in_dgrad(dpre, weight("a_w_in", j)),
                st["x_in"], norm_mix_g[i][None], dx, comm=comm))
            big_grads["a_w_in"][j] = run("dw_in" + tag, lambda comm: wgrad(
                "dw_in" + tag, st["hn"], tok(D), dpre, pl.BlockSpec((tk, 1024), lambda j_, k_: (k_, j_)),
                _sds((D, 2 * GH), F32), pl.BlockSpec((D, 1024), lambda j_, k_: (0, j_)), 4, T, tk, comm=comm))
        else:
            do = matmul("b_out_bwd" + tag, NT, dxb, pl.BlockSpec((1024, D), lambda i_, j_: (i_, 0)),
                        weight("b_w_out", j), pl.BlockSpec((None, D, D), lambda i_, j_: (0, 0, 0)),
                        _sds((T, D), BF16), pl.BlockSpec((1024, D), lambda i_, j_: (i_, 0)), (T // 1024, 1))
            big_grads["b_w_out"][j] = wgrad(
                "dw_bout" + tag, st["o"], tok(D), dxb, tok(D),
                _sds((D, D), F32), pl.BlockSpec((D, D), lambda j_, k_: (0, 0)), 1, T, tk
            ).reshape(N_CHIPS, D // N_CHIPS, D)
            dqkvp, dwb = run("attn_bwd" + tag, lambda comm: attn_bwd(
                st["qkvp"], st["o"], do, st["wb"], "attn_bwd" + tag, comm=comm))
            small["b_rel_bias"][j] = bias_grad(
                jnp.pad(jnp.transpose(dwb, (1, 0, 2)), ((0, 0), (0, 0), (0, DIAG - KW))), "bias_grad" + tag)
            dx_in, dxb_in, dgn = dgrad_rms(
                "b_qkv_bwd" + tag, *qkv_dgrad(dqkvp, weight("b_w_qkv", j)),
                st["x_in"], norm_mix_g[i][None], dx)
            big_grads["b_w_qkv"][j] = wgrad(
                "dw_qkv" + tag, st["hn"], tok(D), dqkvp,
                pl.BlockSpec((None, tk, D), lambda j_, k_: (j_, k_ + FRONT // tk, 0)),
                _sds((D, 3 * D), F32), pl.BlockSpec((D, D), lambda j_, k_: (0, j_)), 3, T, tk)
        dx, dxb = dx_in, dxb_in
        small["norm_mix_g"][i] = dgn

    small_grads = {
        "norm_mix_g": jnp.concatenate(small["norm_mix_g"], axis=0),
        "norm_ffn_g": jnp.concatenate(small["norm_ffn_g"], axis=0),
        "final_g": d_final.reshape(D),
        "a_v_gain": jnp.concatenate(small["a_v_gain"], axis=0),
        "a_w_s": jnp.stack(small["a_w_s"]),
        "a_b_s": jnp.stack(small["a_b_s"]).reshape(2, SGU_G, SGU_BLOCK),
        "b_rel_bias": jnp.stack(small["b_rel_bias"]),
    }
    small_names = list(small_grads)
    packed = [_rows128(small_grads[k]) for k in small_names] + [_rows128(loss_part[:, :1])]
    offs = [0]
    for p in packed:
        offs.append(offs[-1] + p.shape[0])
    reduced = allreduce_small(jnp.concatenate(packed, axis=0))
    grads = {}
    for t, k in enumerate(small_names):
        nelem = small_grads[k].size
        grads[k] = reduced[offs[t]:offs[t + 1]].reshape(-1)[:nelem].reshape(weights[k].shape)
    loss = reduced[offs[len(small_names)], 0]

    last = [("a_w_in", 0)]
    for kind, name in (("ex", "exchange_last"), ("sc", "scatter_last")):
        comm, done = make_comm(kind, last)
        done(run_comm(comm, name))
    bufs = {k: jnp.zeros(weights[k].shape, F32) for k in BIG}
    for i in range(DEPTH):
        for kl, m_ in zip(layer_tensors(i), grad_metas(layer_tensors(i))):
            bufs[kl[0]] = final_sum(me, own_parts[kl], recv_parts[kl], bufs[kl[0]], kl[1], m_,
                                    "final_sum_%s_l%d" % kl)
    shared = share_final([bufs[k] for k in BIG])
    for k, gfull in zip(BIG, shared):
        grads[k] = gfull

    delta, new_m, new_v = {}, {}, {}
    for k in order:
        shp = weights[k].shape
        if k in BIG:
            view = shp
        elif k == "a_w_s":
            view = (2, SGU_G * SGU_BLOCK, SGU_BLOCK)
        elif len(shp) == 1:
            view = (1, 1, shp[0])
        elif len(shp) == 2:
            view = (1,) + shp
        else:
            view = shp
        d_, m_, v_ = adamw(weights[k].reshape(view), grads[k].reshape(view), mom_m[k].reshape(view),
                           mom_v[k].reshape(view), "adamw_" + k)
        delta[k], new_m[k], new_v[k] = d_.reshape(shp), m_.reshape(shp), v_.reshape(shp)
    for k in transposed:
        for tree in (grads, delta, new_m, new_v):
            tree[k] = jnp.swapaxes(tree[k], 1, 2)

    return (loss, dx.reshape(1, T, D), *[grads[k] for k in order], *[delta[k] for k in order],
            *[new_m[k] for k in order], *[new_v[k] for k in order])
```

```python
import functools

import jax
import jax.numpy as jnp
from jax import lax
from jax.experimental import pallas as pl
from jax.experimental.pallas import tpu as pltpu

F32 = jnp.float32
BF16 = jnp.bfloat16
MESH = pl.DeviceIdType.MESH

D = 1024
DEPTH = 4
EPS = 1e-6
SGU_BLOCK = 128
GH = 2048
SGU_G = 8
SGU_GD = GH // SGU_G
N_HEADS = 16
HEAD_DIM = 64
CHUNK = 64
PAD = 8 * CHUNK
FRONT = 2048
QB = 128
KW = PAD + QB
N_REL = 192
REL_MIN = -(CHUNK - 1)
REL_MAX = 128
D_FF = 2816
FS = D_FF // 4
NEG = -1e30
SCALE = HEAD_DIM ** -0.5
N_CHIPS = 4

ADAM_LR = 0.001
ADAM_B1 = 0.9
ADAM_B2 = 0.999
ADAM_EPS = 1e-08
ADAM_WD = 0.01
ADAM_STEP = 10

VMEM_BIG = 56 * 1024 * 1024

NN = ((1,), (0,))
NT = ((1,), (1,))
TN = ((0,), (0,))


def _dot(a, b, dims):
    return lax.dot_general(a, b, (dims, ((), ())), preferred_element_type=F32)


class Comm:
    def __init__(self, ins, out_shapes, sems, start, wait, aliases=None):
        self.ins, self.out_shapes, self.sems = list(ins), list(out_shapes), list(sems)
        self.start, self.wait, self.aliases = start, wait, dict(aliases or {})


def _host(body, comm, kw):
    grid = tuple(kw["grid"])
    in_specs = list(kw["in_specs"])
    single = not isinstance(kw["out_specs"], (list, tuple))
    out_specs = [kw["out_specs"]] if single else list(kw["out_specs"])
    out_shape = [kw["out_shape"]] if single else list(kw["out_shape"])
    scratch = list(kw.get("scratch_shapes", ()))
    counts = (len(in_specs), len(comm.ins), len(out_specs), len(comm.out_shapes), len(scratch))

    def hosted(*refs):
        parts, p = [], 0
        for cnt in counts:
            parts.append(refs[p:p + cnt])
            p += cnt
        main_in, c_in, main_out, c_out, main_scr = parts
        sems = refs[p:]
        ids = [pl.program_id(a) for a in range(len(grid))]
        first = functools.reduce(jnp.logical_and, [i == 0 for i in ids])
        last = functools.reduce(jnp.logical_and, [i == n - 1 for i, n in zip(ids, grid)])
        pl.when(first)(lambda: comm.start(c_in, c_out, sems))
        body(*main_in, *main_out, *main_scr)
        pl.when(last)(lambda: comm.wait(c_in, c_out, sems))

    old = kw["compiler_params"]
    kw = dict(kw, in_specs=in_specs + [ANY] * len(comm.ins), out_specs=out_specs + [ANY] * len(comm.out_shapes),
              out_shape=out_shape + comm.out_shapes, scratch_shapes=scratch + comm.sems,
              compiler_params=pltpu.CompilerParams(dimension_semantics=("arbitrary",) * len(grid),
                                                   vmem_limit_bytes=old.vmem_limit_bytes, has_side_effects=True))
    if comm.aliases:
        kw["input_output_aliases"] = {counts[0] + i: counts[2] + o for i, o in comm.aliases.items()}
    return hosted, kw


def _pallas(body, comm=None, **kw):
    if comm is not None:
        body, kw = _host(body, comm, kw)
    return pl.pallas_call(body, **kw)


def _split_outs(outs, comm, n_main):
    outs = list(outs) if isinstance(outs, (list, tuple)) else [outs]
    main = outs[:n_main]
    return (main[0] if n_main == 1 else main), outs[n_main:]


def run_comm(comm, name):
    nci, nco = len(comm.ins), len(comm.out_shapes)

    def body(*refs):
        c_in, c_out, sems = refs[:nci], refs[nci:nci + nco], refs[nci + nco:]
        comm.start(c_in, c_out, sems)
        comm.wait(c_in, c_out, sems)

    kw = {}
    if comm.aliases:
        kw["input_output_aliases"] = dict(comm.aliases)
    return _pallas(body, name=name, in_specs=[ANY] * nci, out_specs=[ANY] * nco, out_shape=comm.out_shapes,
                   scratch_shapes=comm.sems, compiler_params=pltpu.CompilerParams(has_side_effects=True),
                   **kw)(*comm.ins)


def combine(comms):
    if len(comms) == 1:
        return comms[0]
    spans, ni, no, ns = [], 0, 0, 0
    for c in comms:
        spans.append((slice(ni, ni + len(c.ins)), slice(no, no + len(c.out_shapes)), slice(ns, ns + len(c.sems))))
        ni, no, ns = ni + len(c.ins), no + len(c.out_shapes), ns + len(c.sems)

    def start(ins, outs, sems):
        for c, (si, so, ss) in zip(comms, spans):
            c.start(ins[si], outs[so], sems[ss])

    def wait(ins, outs, sems):
        for c, (si, so, ss) in zip(comms, spans):
            c.wait(ins[si], outs[so], sems[ss])

    aliases = {}
    for c, (si, so, _) in zip(comms, spans):
        aliases.update({si.start + i: so.start + o for i, o in c.aliases.items()})
    return Comm([a for c in comms for a in c.ins], [o for c in comms for o in c.out_shapes],
                [s for c in comms for s in c.sems], start, wait, aliases)


def _call(body, comm, n_main, args, **kw):
    if comm is None:
        return _pallas(body, **kw)(*args)
    return _split_outs(_pallas(body, comm=comm, **kw)(*args, *comm.ins), comm, n_main)


def _params(sem=None, vmem=None):
    return pltpu.CompilerParams(dimension_semantics=sem, vmem_limit_bytes=vmem)


def _sds(shape, dtype):
    return jax.ShapeDtypeStruct(tuple(shape), dtype)


_GELU_C = 0.7978845608028654


_GELU_A = _GELU_C * 0.044715


def _gelu(x):
    t = jnp.tanh(x * (_GELU_C + _GELU_A * (x * x)))
    h = 0.5 * x
    return h + h * t


def _gelu_and_grad(x):
    x2 = x * x
    t = jnp.tanh(x * (_GELU_C + _GELU_A * x2))
    h = 0.5 * x
    val = h + h * t
    grad = (0.5 + 0.5 * t) + (h * (1.0 - t * t)) * (_GELU_C + (3.0 * _GELU_A) * x2)
    return val, grad


def _sigmoid(x):
    return 0.5 * (jnp.tanh(0.5 * x) + 1.0)


def cast_bf16(w, name):
    L, R, C = w.shape

    def body(w_ref, o_ref):
        o_ref[...] = w_ref[...].astype(BF16)

    spec = pl.BlockSpec((None, R, C), lambda l: (l, 0, 0))
    return _pallas(body, name=name, grid=(L,), in_specs=[spec], out_specs=spec,
                   out_shape=_sds((L, R, C), BF16), compiler_params=_params(("parallel",)))(w)


def rms_fwd(x, g, name, tm=512):
    T = x.shape[0]

    def body(x_ref, g_ref, o_ref):
        xf = x_ref[...]
        r = lax.rsqrt(jnp.mean(xf * xf, axis=-1, keepdims=True) + EPS)
        o_ref[...] = ((xf * r) * g_ref[...]).astype(BF16)

    row = pl.BlockSpec((tm, D), lambda i: (i, 0))
    return _pallas(body, name=name, grid=(T // tm,),
                   in_specs=[row, pl.BlockSpec((1, D), lambda i: (0, 0))], out_specs=row,
                   out_shape=_sds((T, D), BF16), compiler_params=_params(("parallel",)))(x, g)


def dgrad_rms(name, compute, args, specs, x, g, dres, tm=512, comm=None):
    T = x.shape[0]
    n = T // tm
    k = len(args)

    def body(*refs):
        x_ref, g_ref, dres_ref, dx_ref, dxb_ref, dg_ref, acc_ref = refs[k:]
        i = pl.program_id(0)
        xf = x_ref[...]
        r = lax.rsqrt(jnp.mean(xf * xf, axis=-1, keepdims=True) + EPS)
        xhat = xf * r
        dhf = compute(*refs[:k])
        part = (dhf * xhat).reshape(tm // 8, 8, D).sum(axis=0)

        @pl.when(i == 0)
        def _():
            acc_ref[...] = part

        @pl.when(i > 0)
        def _():
            acc_ref[...] += part

        dxhat = dhf * g_ref[...]
        dx = dres_ref[...] + r * (dxhat - xhat * jnp.mean(dxhat * xhat, axis=-1, keepdims=True))
        dx_ref[...] = dx
        dxb_ref[...] = dx.astype(BF16)

        @pl.when(i == n - 1)
        def _():
            dg_ref[...] = jnp.sum(acc_ref[...], axis=0, keepdims=True)

    row = pl.BlockSpec((tm, D), lambda i: (i, 0))
    vec = pl.BlockSpec((1, D), lambda i: (0, 0))
    return _call(body, comm, 3, (*args, x, g, dres), name=name, grid=(n,),
                 in_specs=list(specs) + [row, vec, row], out_specs=[row, row, vec],
                 out_shape=[_sds((T, D), F32), _sds((T, D), BF16), _sds((1, D), F32)],
                 scratch_shapes=[pltpu.VMEM((8, D), F32)],
                 compiler_params=_params(("arbitrary",), VMEM_BIG))


def final_loss(x, g, tgt, name, tm=256):
    T = x.shape[0]
    n = T // tm

    def body(x_ref, g_ref, t_ref, loss_ref, dx_ref, dxb_ref, dg_ref, acc_ref, lacc_ref):
        i = pl.program_id(0)
        xf = x_ref[...]
        r = lax.rsqrt(jnp.mean(xf * xf, axis=-1, keepdims=True) + EPS)
        xhat = xf * r
        gg = g_ref[...]
        e = xhat * gg - t_ref[...]
        dy = e * (1.0 / D)
        part = (dy * xhat).reshape(tm // 8, 8, D).sum(axis=0)
        lpart = (e * e).reshape(tm // 8, 8, D).sum(axis=0)

        @pl.when(i == 0)
        def _():
            acc_ref[...] = part
            lacc_ref[...] = lpart

        @pl.when(i > 0)
        def _():
            acc_ref[...] += part
            lacc_ref[...] += lpart

        dxhat = dy * gg
        dx = r * (dxhat - xhat * jnp.mean(dxhat * xhat, axis=-1, keepdims=True))
        dx_ref[...] = dx
        dxb_ref[...] = dx.astype(BF16)

        @pl.when(i == n - 1)
        def _():
            dg_ref[...] = jnp.sum(acc_ref[...], axis=0, keepdims=True)
            total = jnp.sum(jnp.sum(lacc_ref[...], axis=0, keepdims=True), axis=1, keepdims=True)
            loss_ref[...] = jnp.broadcast_to(total * (0.5 / D), (1, 128))

    row = pl.BlockSpec((tm, D), lambda i: (i, 0))
    vec = pl.BlockSpec((1, D), lambda i: (0, 0))
    return _pallas(body, name=name, grid=(n,), in_specs=[row, vec, row],
                   out_specs=[pl.BlockSpec((1, 128), lambda i: (0, 0)), row, row, vec],
                   out_shape=[_sds((1, 128), F32), _sds((T, D), F32), _sds((T, D), BF16), _sds((1, D), F32)],
                   scratch_shapes=[pltpu.VMEM((8, D), F32), pltpu.VMEM((8, D), F32)],
                   compiler_params=_params(("arbitrary",)))(x, g, tgt)


def matmul(name, dims, a, a_spec, b, b_spec, out_shape, out_spec, grid, *, acc=False, res=None, res_spec=None,
           comm=None):
    has_res = res is not None

    def body(*refs):
        a_ref, b_ref = refs[0], refs[1]
        r_ref = refs[2] if has_res else None
        o_ref = refs[-1]
        d = _dot(a_ref[...], b_ref[...], dims)
        if not acc:
            if has_res:
                d = d + r_ref[...]
            o_ref[...] = d.astype(o_ref.dtype)
        else:
            k = pl.program_id(len(grid) - 1)

            @pl.when(k == 0)
            def _():
                o_ref[...] = (d + r_ref[...]) if has_res else d

            @pl.when(k > 0)
            def _():
                o_ref[...] += d

    sem = ("parallel",) * (len(grid) - 1) + (("arbitrary",) if acc else ("parallel",))
    ins = [a, b] + ([res] if has_res else [])
    specs = [a_spec, b_spec] + ([res_spec] if has_res else [])
    return _call(body, comm, 1, ins, name=name, grid=grid, in_specs=specs, out_specs=out_spec, out_shape=out_shape,
                 compiler_params=_params(sem, VMEM_BIG))


def wgrad(name, a, a_spec, b, b_spec, out_shape, out_spec, J, T, tk, comm=None):
    return matmul(name, TN, a, a_spec, b, b_spec, out_shape, out_spec, (J, T // tk), acc=True, comm=comm)


def _sgu_mask():
    p = lax.broadcasted_iota(jnp.int32, (SGU_BLOCK, SGU_BLOCK), 0)
    q = lax.broadcasted_iota(jnp.int32, (SGU_BLOCK, SGU_BLOCK), 1)
    return lax.shift_right_logical(q, 6) <= lax.shift_right_logical(p, 6)


def sgu_fwd(pre, gain, w_s, b_s, name, comm=None):
    T = pre.shape[0]

    def body(pre_ref, gain_ref, ws_ref, bs_ref, y_ref):
        mask = _sgu_mask()
        u = _gelu(pre_ref[:, :GH].astype(F32))
        va = _gelu(pre_ref[:, GH:].astype(F32))
        r = lax.rsqrt(jnp.mean(va * va, axis=-1, keepdims=True) + EPS)
        vn = ((va * r) * gain_ref[...]).astype(BF16)
        for g in range(SGU_G):
            sl = slice(g * SGU_GD, (g + 1) * SGU_GD)
            wm = jnp.where(mask, ws_ref[g], 0.0).astype(BF16)
            vm = _dot(wm, vn[:, sl], NN) + bs_ref[g]
            y_ref[:, sl] = (u[:, sl] * vm).astype(BF16)

    return _call(
        body, comm, 1, (pre, gain, w_s, b_s), name=name, grid=(T // SGU_BLOCK,),
        in_specs=[pl.BlockSpec((SGU_BLOCK, 2 * GH), lambda i: (i, 0)),
                  pl.BlockSpec((1, GH), lambda i: (0, 0)),
                  pl.BlockSpec((SGU_G, SGU_BLOCK, SGU_BLOCK), lambda i: (0, 0, 0)),
                  pl.BlockSpec((SGU_G, SGU_BLOCK, 1), lambda i: (0, 0, 0))],
        out_specs=pl.BlockSpec((SGU_BLOCK, GH), lambda i: (i, 0)),
        out_shape=_sds((T, GH), BF16), compiler_params=_params(("parallel",)))


def sgu_bwd(pre, dy, gain, w_s, b_s, name, comm=None):
    T = pre.shape[0]
    n = T // SGU_BLOCK

    def body(pre_ref, dy_ref, gain_ref, ws_ref, bs_ref, dpre_ref, dws_ref, dbs_ref, dgain_ref, gacc_ref):
        i = pl.program_id(0)

        @pl.when(i == 0)
        def _():
            dws_ref[...] = jnp.zeros_like(dws_ref)
            dbs_ref[...] = jnp.zeros_like(dbs_ref)
            gacc_ref[...] = jnp.zeros_like(gacc_ref)

        mask = _sgu_mask()
        u, du_dpre = _gelu_and_grad(pre_ref[:, :GH].astype(F32))
        va, dva_dpre = _gelu_and_grad(pre_ref[:, GH:].astype(F32))
        r = lax.rsqrt(jnp.mean(va * va, axis=-1, keepdims=True) + EPS)
        vhat = va * r
        gain_v = gain_ref[...]
        vn = (vhat * gain_v).astype(BF16)
        dyf = dy_ref[...].astype(F32)
        dvn_parts = []
        for g in range(SGU_G):
            sl = slice(g * SGU_GD, (g + 1) * SGU_GD)
            wm = jnp.where(mask, ws_ref[g], 0.0).astype(BF16)
            vm = _dot(wm, vn[:, sl], NN) + bs_ref[g]
            dpre_ref[:, sl] = ((dyf[:, sl] * vm) * du_dpre[:, sl]).astype(BF16)
            dvm = dyf[:, sl] * u[:, sl]
            dbs_ref[g] += jnp.sum(dvm, axis=-1, keepdims=True)
            dvm16 = dvm.astype(BF16)
            dws_ref[g] += jnp.where(mask, _dot(dvm16, vn[:, sl], NT), 0.0)
            dvn_parts.append(_dot(wm, dvm16, TN))
        dvn = jnp.concatenate(dvn_parts, axis=-1)
        gacc_ref[...] += (dvn * vhat).reshape(SGU_BLOCK // 8, 8, GH).sum(axis=0)
        dvhat = dvn * gain_v
        dva = r * (dvhat - vhat * jnp.mean(dvhat * vhat, axis=-1, keepdims=True))
        dpre_ref[:, GH:] = (dva * dva_dpre).astype(BF16)

        @pl.when(i == n - 1)
        def _():
            dgain_ref[...] = jnp.sum(gacc_ref[...], axis=0, keepdims=True)

    const3 = lambda i: (0, 0, 0)
    return _call(
        body, comm, 4, (pre, dy, gain, w_s, b_s), name=name, grid=(n,),
        in_specs=[pl.BlockSpec((SGU_BLOCK, 2 * GH), lambda i: (i, 0)),
                  pl.BlockSpec((SGU_BLOCK, GH), lambda i: (i, 0)),
                  pl.BlockSpec((1, GH), lambda i: (0, 0)),
                  pl.BlockSpec((SGU_G, SGU_BLOCK, SGU_BLOCK), const3),
                  pl.BlockSpec((SGU_G, SGU_BLOCK, 1), const3)],
        out_specs=[pl.BlockSpec((SGU_BLOCK, 2 * GH), lambda i: (i, 0)),
                   pl.BlockSpec((SGU_G, SGU_BLOCK, SGU_BLOCK), const3),
                   pl.BlockSpec((SGU_G, SGU_BLOCK, 1), const3),
                   pl.BlockSpec((1, GH), lambda i: (0, 0))],
        out_shape=[_sds((T, 2 * GH), BF16), _sds((SGU_G, SGU_BLOCK, SGU_BLOCK), F32),
                   _sds((SGU_G, SGU_BLOCK, 1), F32), _sds((1, GH), F32)],
        scratch_shapes=[pltpu.VMEM((8, GH), F32)],
        compiler_params=_params(("arbitrary",)))


DIAG = 768


def _diag_onehot():
    n = lax.broadcasted_iota(jnp.int32, (N_REL, DIAG), 1)
    r = lax.broadcasted_iota(jnp.int32, (N_REL, DIAG), 0)
    idx = jnp.clip(KW - 1 - n, REL_MIN, REL_MAX) - REL_MIN
    return (idx == r).astype(BF16)


def _split3(v):
    hi = v.astype(BF16)
    r1 = v - hi.astype(F32)
    mid = r1.astype(BF16)
    lo = (r1 - mid.astype(F32)).astype(BF16)
    return hi, mid, lo


def bias_build(rel_bias, name):
    def body(rb_ref, o_ref):
        oh = _diag_onehot()
        hi, mid, lo = _split3(rb_ref[...])
        u = (_dot(hi, oh, NN) + _dot(mid, oh, NN) + _dot(lo, oh, NN)) * LOG2E
        j = lax.broadcasted_iota(jnp.int32, (1, KW), 1)

        def row(i, carry):
            val = pltpu.roll(u, (i + (DIAG - QB + 1)) % DIAG, 1)[:, :KW]
            rel = lax.shift_right_logical(i, 6) - lax.shift_right_logical(j, 6) + 8
            ok = (rel >= 0) & (rel <= 8)
            o_ref[i] = jnp.where(ok, val, NEG)
            return carry

        lax.fori_loop(0, QB, row, 0)

    return _pallas(body, name=name, out_shape=_sds((QB, N_HEADS, KW), F32),
                   in_specs=[pl.BlockSpec(memory_space=pltpu.VMEM)],
                   out_specs=pl.BlockSpec(memory_space=pltpu.VMEM))(rel_bias)


def bias_grad(dwb, name):
    def body(d_ref, o_ref):
        def row(i, acc):
            return acc + pltpu.roll(d_ref[i], QB - 1 - i, 1)

        du = lax.fori_loop(0, QB, row, jnp.zeros((N_HEADS, DIAG), F32))
        oh = _diag_onehot()
        hi, mid, lo = _split3(du)
        o_ref[...] = _dot(hi, oh, NT) + _dot(mid, oh, NT) + _dot(lo, oh, NT)

    return _pallas(body, name=name, out_shape=_sds((N_HEADS, N_REL), F32),
                   in_specs=[pl.BlockSpec(memory_space=pltpu.VMEM)],
                   out_specs=pl.BlockSpec(memory_space=pltpu.VMEM))(dwb)


LOG2E = 1.4426950408889634
Q_SCALE = SCALE * LOG2E


def _attn_block(qkv_ref, blk, masked):
    r0 = pl.multiple_of(blk * QB, QB)
    qs = qkv_ref[0, pl.ds(r0 + FRONT, QB), :]
    k2 = qkv_ref[1, pl.ds(r0 + (FRONT - PAD), KW), :]
    v2 = qkv_ref[2, pl.ds(r0 + (FRONT - PAD), KW), :]
    kvalid = (lax.broadcasted_iota(jnp.int32, (1, KW), 1) >= PAD - blk * QB) if masked else None
    return r0, qs, k2, v2, kvalid


def _head_mask(h):
    lane = lax.broadcasted_iota(jnp.int32, (1, 2 * HEAD_DIM), 1)
    return (lane < HEAD_DIM) if h == 0 else (lane >= HEAD_DIM)


def _stack_heads(a):
    zero = jnp.zeros_like(a)
    return jnp.concatenate([jnp.where(_head_mask(0), a, zero), jnp.where(_head_mask(1), a, zero)], axis=0)


def _unstack_heads(a):
    return jnp.where(_head_mask(0), a[:QB], a[QB:])


def _attn_exp(qst, k2, w_ref, kvalid):
    s = _dot(qst, k2, NT) + w_ref[...].reshape(2 * QB, KW)
    if kvalid is not None:
        s = jnp.where(kvalid, s, NEG)
    e = jnp.exp2(s - jnp.max(s, axis=-1, keepdims=True))
    return e, 1.0 / jnp.sum(e, axis=-1, keepdims=True)


ATTN_G = 4
ATTN_STEP = QB * ATTN_G


def _masked_and_not(b, fn):
    n_masked = -(-PAD // ATTN_STEP)
    pl.when(b < n_masked)(functools.partial(fn, True))
    pl.when(b >= n_masked)(functools.partial(fn, False))


def attn_fwd(qkvp, wb, name, comm=None):
    T = qkvp.shape[1] - FRONT

    def body(qkv_ref, w_ref, o_ref):
        b = pl.program_id(1)

        def blocks(masked):
            for t in range(ATTN_G):
                _, qs, k2, v2, kvalid = _attn_block(qkv_ref, b * ATTN_G + t, masked)
                e, inv = _attn_exp(_stack_heads(qs), k2, w_ref, kvalid)
                o_ref[t * QB:(t + 1) * QB, :] = _unstack_heads(_dot(e.astype(BF16), v2, NN) * inv).astype(BF16)

        _masked_and_not(b, blocks)

    return _call(
        body, comm, 1, (qkvp, wb), name=name, grid=(N_HEADS // 2, T // (QB * ATTN_G)),
        in_specs=[pl.BlockSpec((3, FRONT + T, 2 * HEAD_DIM), lambda hp, b: (0, 0, hp)),
                  pl.BlockSpec((2, QB, KW), lambda hp, b: (hp, 0, 0))],
        out_specs=pl.BlockSpec((QB * ATTN_G, 2 * HEAD_DIM), lambda hp, b: (b, hp)),
        out_shape=_sds((T, D), BF16),
        compiler_params=_params(("parallel", "arbitrary"), VMEM_BIG))


def attn_bwd(qkvp, o, do, wb, name, comm=None):
    T = qkvp.shape[1] - FRONT
    nb = T // (QB * ATTN_G)

    def body(qkv_ref, o_ref, do_ref, w_ref, dqkv_ref, dw_ref, dk_acc, dv_acc):
        b = pl.program_id(1)

        @pl.when(b == 0)
        def _():
            dk_acc[...] = jnp.zeros_like(dk_acc)
            dv_acc[...] = jnp.zeros_like(dv_acc)
            dw_ref[...] = jnp.zeros_like(dw_ref)
            dqkv_ref[0, 0:FRONT, :] = jnp.zeros((FRONT, 2 * HEAD_DIM), BF16)

        def blocks(masked):
            dws = None
            for t in range(ATTN_G):
                r0, qs, k2, v2, kvalid = _attn_block(qkv_ref, b * ATTN_G + t, masked)
                qst = _stack_heads(qs)
                e, inv = _attn_exp(qst, k2, w_ref, kvalid)
                do2 = do_ref[t * QB:(t + 1) * QB, :]
                dost = _stack_heads(do2)
                prod = _stack_heads(do2.astype(F32) * o_ref[t * QB:(t + 1) * QB, :].astype(F32))
                delta = jnp.sum(prod, axis=-1, keepdims=True)
                ds = e * ((_dot(dost, v2, NT) - delta) * inv)
                dws = ds if dws is None else dws + ds
                ds16 = ds.astype(BF16)
                dq = _unstack_heads(_dot(ds16, k2, NN)) * SCALE
                dqkv_ref[0, pl.ds(r0 + FRONT, QB), :] = dq.astype(BF16)
                dk_acc[pl.ds(r0 + (FRONT - PAD), KW), :] += _dot(ds16, qst, TN)
                dv_acc[pl.ds(r0 + (FRONT - PAD), KW), :] += _dot(
                    e.astype(BF16), (dost.astype(F32) * inv).astype(BF16), TN)
            dw_ref[...] += dws.reshape(2, QB, KW)

        _masked_and_not(b, blocks)

        @pl.when(b == nb - 1)
        def _():
            dqkv_ref[1] = (dk_acc[...] * (1.0 / LOG2E)).astype(BF16)
            dqkv_ref[2] = dv_acc[...].astype(BF16)

    slab = pl.BlockSpec((3, FRONT + T, 2 * HEAD_DIM), lambda hp, b: (0, 0, hp))
    wspec = pl.BlockSpec((2, QB, KW), lambda hp, b: (hp, 0, 0))
    rows = pl.BlockSpec((QB * ATTN_G, 2 * HEAD_DIM), lambda hp, b: (b, hp))
    return _call(
        body, comm, 2, (qkvp, o, do, wb), name=name, grid=(N_HEADS // 2, nb),
        in_specs=[slab, rows, rows, wspec],
        out_specs=[slab, wspec],
        out_shape=[_sds((3, FRONT + T, D), BF16), _sds((N_HEADS, QB, KW), F32)],
        scratch_shapes=[pltpu.VMEM((FRONT + T, 2 * HEAD_DIM), F32), pltpu.VMEM((FRONT + T, 2 * HEAD_DIM), F32)],
        compiler_params=_params(("parallel", "arbitrary"), VMEM_BIG))


def proj_qkv(hn, w, l, name, tm=512, comm=None):
    T = hn.shape[0]
    pb = FRONT // tm

    def body(a_ref, b_ref, o_ref):
        i = pl.program_id(1)

        @pl.when(i < pb)
        def _():
            o_ref[...] = jnp.zeros_like(o_ref)

        @pl.when(i >= pb)
        def _():
            scale = jnp.where(pl.program_id(0) == 0, Q_SCALE, 1.0).astype(F32)
            o_ref[...] = (_dot(a_ref[...], b_ref[...], NN) * scale).astype(BF16)

    return _call(
        body, comm, 1, (hn, w), name=name, grid=(3, pb + T // tm),
        in_specs=[pl.BlockSpec((tm, D), lambda p, i: (jnp.maximum(i - pb, 0), 0)),
                  pl.BlockSpec((None, D, D), lambda p, i: (l, 0, p))],
        out_specs=pl.BlockSpec((None, tm, D), lambda p, i: (p, i, 0)),
        out_shape=_sds((3, FRONT + T, D), BF16),
        compiler_params=_params(("parallel", "parallel"), VMEM_BIG))


def ffn_up(hn, wg, wu, l, name, tm=1024, comm=None):
    T = hn.shape[0]

    def body(a_ref, wg_ref, wu_ref, g_ref, u_ref, h_ref):
        a = a_ref[...]
        g = _dot(a, wg_ref[...], NT)
        u = _dot(a, wu_ref[...], NT)
        s = _sigmoid(g)
        silu = g * s
        g_ref[...] = (u * (s * (1.0 + g * (1.0 - s)))).astype(BF16)
        u_ref[...] = silu.astype(BF16)
        h_ref[...] = (silu * u).astype(BF16)

    wspec = pl.BlockSpec((None, None, FS, D), lambda s, i: (l, s, 0, 0))
    ospec = pl.BlockSpec((None, tm, FS), lambda s, i: (s, i, 0))
    return _call(
        body, comm, 3, (hn, wg, wu), name=name, grid=(N_CHIPS, T // tm),
        in_specs=[pl.BlockSpec((tm, D), lambda s, i: (i, 0)), wspec, wspec],
        out_specs=[ospec, ospec, ospec],
        out_shape=[_sds((N_CHIPS, T, FS), BF16)] * 3,
        compiler_params=_params(("parallel", "parallel"), VMEM_BIG))


def ffn_bwd_dh(dxb, wd, g, u, l, name, tm=2048, comm=None):
    T = dxb.shape[0]
    tm = min(tm, T)

    def body(a_ref, wd_ref, g_ref, u_ref, dg_ref, du_ref):
        dh = _dot(a_ref[...], wd_ref[...], NT)
        dg_ref[...] = (dh * g_ref[...].astype(F32)).astype(BF16)
        du_ref[...] = (dh * u_ref[...].astype(F32)).astype(BF16)

    aspec = pl.BlockSpec((None, tm, FS), lambda i, s: (s, i, 0))
    return _call(
        body, comm, 2, (dxb, wd, g, u), name=name, grid=(T // tm, N_CHIPS),
        in_specs=[pl.BlockSpec((tm, D), lambda i, s: (i, 0)),
                  pl.BlockSpec((None, None, FS, D), lambda i, s: (l, s, 0, 0)), aspec, aspec],
        out_specs=[aspec, aspec],
        out_shape=[_sds((N_CHIPS, T, FS), BF16)] * 2,
        compiler_params=_params(("parallel", "parallel"), VMEM_BIG))


def ffn_dgrad(dg, du, wg, wu, tm=512):
    def compute(dg_ref, du_ref, wg_ref, wu_ref):
        d = None
        for s in range(N_CHIPS):
            t = _dot(dg_ref[s], wg_ref[s], NN) + _dot(du_ref[s], wu_ref[s], NN)
            d = t if d is None else d + t
        return d

    aspec = pl.BlockSpec((N_CHIPS, tm, FS), lambda i: (0, i, 0))
    wspec = pl.BlockSpec((None, N_CHIPS, FS, D), lambda i: (0, 0, 0, 0), pipeline_mode=pl.Buffered(1))
    return compute, (dg, du, wg, wu), [aspec, aspec, wspec, wspec]


def qkv_dgrad(dqkvp, w, tm=512):
    def compute(a_ref, w_ref):
        d = None
        for p in range(3):
            t = _dot(a_ref[p], w_ref[:, p * D:(p + 1) * D], NT)
            d = t if d is None else d + t
        return d

    return compute, (dqkvp, w), [pl.BlockSpec((3, tm, D), lambda i: (0, i + FRONT // tm, 0)),
                                 pl.BlockSpec((None, D, 3 * D), lambda i: (0, 0, 0))]


def in_dgrad(dpre, w, tm=512):
    def compute(a_ref, w_ref):
        return _dot(a_ref[...], w_ref[...], NT)

    return compute, (dpre, w), [pl.BlockSpec((tm, 2 * GH), lambda i: (i, 0)),
                                pl.BlockSpec((None, D, 2 * GH), lambda i: (0, 0, 0))]


def _rms_rows(x, g):
    r = lax.rsqrt(jnp.mean(x * x, axis=-1, keepdims=True) + EPS)
    return ((x * r) * g).astype(BF16)


def residual_proj(name, compute, args, specs, res, norm_g, tm=512, comm=None):
    T = res.shape[0]
    k = len(args)
    with_norm = norm_g is not None

    def body(*refs):
        d = refs[k][...] + compute(*refs[:k])
        if with_norm:
            refs[k + 2][...] = d
            refs[k + 3][...] = _rms_rows(d, refs[k + 1][...])
        else:
            refs[k + 1][...] = d

    row = pl.BlockSpec((tm, D), lambda i: (i, 0))
    vec = pl.BlockSpec((1, D), lambda i: (0, 0))
    if with_norm:
        return _call(body, comm, 2, (*args, res, norm_g), name=name, grid=(T // tm,),
                     in_specs=list(specs) + [row, vec], out_specs=[row, row],
                     out_shape=[_sds((T, D), F32), _sds((T, D), BF16)],
                     compiler_params=_params(("parallel",), VMEM_BIG))
    return _call(body, comm, 1, (*args, res), name=name, grid=(T // tm,), in_specs=list(specs) + [row],
                 out_specs=row, out_shape=_sds((T, D), F32), compiler_params=_params(("parallel",), VMEM_BIG))


def ffn_down(h, wd, tm=512):
    def compute(h_ref, wd_ref):
        d = None
        for s in range(N_CHIPS):
            t = _dot(h_ref[s], wd_ref[s], NN)
            d = t if d is None else d + t
        return d

    return compute, (h, wd), [pl.BlockSpec((N_CHIPS, tm, FS), lambda i: (0, i, 0)),
                              pl.BlockSpec((None, N_CHIPS, FS, D), lambda i: (0, 0, 0, 0))]


def out_proj(a, w, tm=512):
    K = a.shape[1]

    def compute(a_ref, w_ref):
        return _dot(a_ref[...], w_ref[...], NN)

    return compute, (a, w), [pl.BlockSpec((tm, K), lambda i: (i, 0)), pl.BlockSpec((None, K, D), lambda i: (0, 0, 0))]


def adamw(w, g, m, v, name):
    L, R, C = w.shape

    def body(w_ref, g_ref, m_ref, v_ref, d_ref, nm_ref, nv_ref):
        gf = g_ref[...]
        nm = ADAM_B1 * m_ref[...] + (1.0 - ADAM_B1) * gf
        nv = ADAM_B2 * v_ref[...] + (1.0 - ADAM_B2) * (gf * gf)
        m_hat = nm / (1.0 - ADAM_B1 ** ADAM_STEP)
        v_hat = nv / (1.0 - ADAM_B2 ** ADAM_STEP)
        d_ref[...] = -ADAM_LR * (m_hat / (jnp.sqrt(v_hat) + ADAM_EPS) + ADAM_WD * w_ref[...])
        nm_ref[...] = nm
        nv_ref[...] = nv

    tr = R // 4 if R % 32 == 0 else R
    spec = pl.BlockSpec((None, tr, C), lambda l, r: (l, r, 0))
    return _pallas(body, name=name, grid=(L, R // tr), in_specs=[spec] * 4, out_specs=[spec] * 3,
                   out_shape=[_sds((L, R, C), F32)] * 3,
                   compiler_params=_params(("parallel", "parallel")))(w, g, m, v)


def _coords():
    return lax.axis_index("x"), lax.axis_index("y"), lax.axis_index("c")


def _other_chips(x, y):
    out = []
    for fx, fy in ((1, 0), (0, 1), (1, 1)):
        px = (1 - x) if fx else x
        py = (1 - y) if fy else y
        out.append((px, py))
    return out


def _flip_index(s, j):
    sx, sy = s // 2, s % 2
    fx, fy = ((1, 0), (0, 1), (1, 1))[j]
    return 2 * (sx ^ fx) + (sy ^ fy)


def _for_my_chip(sme, fn):
    for s in range(N_CHIPS):
        pl.when(sme == s)(functools.partial(fn, s))


ANY = pl.BlockSpec(memory_space=pl.ANY)

GATHER_KIND = {"a_w_in": "col", "b_w_qkv": "col", "a_w_out": "row", "b_w_out": "row",
               "ffn_w_gate": "row", "ffn_w_up": "row", "ffn_w_down": "row"}
BIG = tuple(GATHER_KIND)


def _gathered_shape(kind, shape):
    L, R, C = shape
    return (L, R, N_CHIPS * C) if kind == "col" else (L, N_CHIPS, R, C)


def _shard_rows(ref, kind, s, r0, rn, C):
    if kind == "col":
        return ref.at[:, pl.ds(r0, rn), s * C:(s + 1) * C]
    return ref.at[:, s, pl.ds(r0, rn), :]


def gather_stage1(items):
    n = len(items)
    dims = [it[0].shape[1:] for it in items]

    def copies(ins, outs, sems, s, with_landed=True):
        lsem, ssem, rsem = sems
        x, y, c = _coords()
        chips = _other_chips(x, y)
        local, send, landed = [], [], []
        for t, (_, li, kind) in enumerate(items):
            R, C = dims[t]
            r0 = pl.multiple_of(c * (R // 2), 8)
            local.append(pltpu.make_async_copy(ins[t].at[pl.ds(li, 1)], _shard_rows(outs[t], kind, s, 0, R, C),
                                               lsem.at[t]))
            for j in range(3):
                pair = dict(send_sem=ssem.at[3 * t + j], recv_sem=rsem.at[3 * t + j],
                            device_id=(chips[j][0], chips[j][1], c), device_id_type=MESH)
                send.append(pltpu.make_async_remote_copy(
                    src_ref=ins[t].at[pl.ds(li, 1), pl.ds(r0, R // 2), :],
                    dst_ref=_shard_rows(outs[t], kind, s, r0, R // 2, C), **pair))
                if with_landed:
                    got = _shard_rows(outs[t], kind, _flip_index(s, j), r0, R // 2, C)
                    landed.append(pltpu.make_async_remote_copy(src_ref=got, dst_ref=got, **pair))
        return local, send, landed

    def start(ins, outs, sems):
        def run(s):
            local, send, _ = copies(ins, outs, sems, s, with_landed=False)
            for cp in local + send:
                cp.start()
        x, y, _ = _coords()
        _for_my_chip(2 * x + y, run)

    def wait(ins, outs, sems):
        def run(s):
            local, send, landed = copies(ins, outs, sems, s)
            for cp in landed:
                cp.wait_recv()
            for cp in send:
                cp.wait_send()
            for cp in local:
                cp.wait()
        x, y, _ = _coords()
        _for_my_chip(2 * x + y, run)

    out_shapes = [_sds(_gathered_shape(kind, (1,) + tuple(dims[t])), BF16) for t, (_, _, kind) in enumerate(items)]
    sems = [pltpu.SemaphoreType.DMA((n,)), pltpu.SemaphoreType.DMA((3 * n,)), pltpu.SemaphoreType.DMA((3 * n,))]
    return Comm([it[0] for it in items], out_shapes, sems, start, wait)


def gather_stage2(items, gathered):
    n = len(items)
    dims = [it[0].shape[1:] for it in items]

    def copies(outs, sems, s, with_landed=True):
        ssem, rsem = sems
        x, y, c = _coords()
        send, landed = [], []
        for t, (_, _, kind) in enumerate(items):
            R, C = dims[t]
            for j in range(3):
                pair = dict(send_sem=ssem.at[3 * t + j], recv_sem=rsem.at[3 * t + j],
                            device_id=(x, y, 1 - c), device_id_type=MESH)
                mine = _shard_rows(outs[t], kind, _flip_index(s, j), pl.multiple_of(c * (R // 2), 8), R // 2, C)
                send.append(pltpu.make_async_remote_copy(src_ref=mine, dst_ref=mine, **pair))
                if with_landed:
                    other = _shard_rows(outs[t], kind, _flip_index(s, j), pl.multiple_of((1 - c) * (R // 2), 8),
                                        R // 2, C)
                    landed.append(pltpu.make_async_remote_copy(src_ref=other, dst_ref=other, **pair))
        return send, landed

    def start(ins, outs, sems):
        def run(s):
            for cp in copies(outs, sems, s, with_landed=False)[0]:
                cp.start()
        x, y, _ = _coords()
        _for_my_chip(2 * x + y, run)

    def wait(ins, outs, sems):
        def run(s):
            send, landed = copies(outs, sems, s)
            for cp in landed:
                cp.wait_recv()
            for cp in send:
                cp.wait_send()
        x, y, _ = _coords()
        _for_my_chip(2 * x + y, run)

    out_shapes = [_sds(g.shape, BF16) for g in gathered]
    sems = [pltpu.SemaphoreType.DMA((3 * n,)), pltpu.SemaphoreType.DMA((3 * n,))]
    return Comm(gathered, out_shapes, sems, start, wait, aliases={t: t for t in range(n)})


def gather_both(items):
    s1 = gather_stage1(items)
    s2 = gather_stage2(items, s1.out_shapes)
    n1 = len(s1.sems)

    def wait(ins, outs, sems):
        s1.wait(ins, outs, sems[:n1])
        s2.start((), outs, sems[n1:])
        s2.wait((), outs, sems[n1:])

    return Comm(s1.ins, s1.out_shapes, s1.sems + s2.sems, lambda ins, outs, sems: s1.start(ins, outs, sems[:n1]), wait)


def _half_shape(kind, R, C):
    return (R // 2, N_CHIPS * C) if kind == "col" else (N_CHIPS, R // 2, C)


def exchange_halves(grads, metas):
    n = len(grads)

    def copies(ins, outs, sems):
        ssem, rsem = sems
        x, y, c = _coords()
        out = []
        for t, (kind, R, C) in enumerate(metas):
            r0 = pl.multiple_of((1 - c) * (R // 2), 8)
            src = ins[t].at[pl.ds(r0, R // 2), :] if kind == "col" else ins[t].at[:, pl.ds(r0, R // 2), :]
            out.append(pltpu.make_async_remote_copy(
                src_ref=src, dst_ref=outs[t], send_sem=ssem.at[t], recv_sem=rsem.at[t],
                device_id=(x, y, 1 - c), device_id_type=MESH))
        return out

    def start(ins, outs, sems):
        for cp in copies(ins, outs, sems):
            cp.start()

    def wait(ins, outs, sems):
        for cp in copies(ins, outs, sems):
            cp.wait()

    return Comm(grads, [_sds(_half_shape(*m), F32) for m in metas], [pltpu.SemaphoreType.DMA((n,))] * 2, start, wait)


def pair_sum(me, g, sib, meta, name):
    kind, R, C = meta
    h = R // 2

    def body(me_ref, g_ref, sib_ref, p16_ref, own_ref):
        s = pl.program_id(0)
        v = g_ref[...] + sib_ref[...]
        p16_ref[...] = v.astype(BF16)

        @pl.when(s == me_ref[1])
        def _():
            own_ref[...] = v

    if kind == "col":
        gspec = pl.BlockSpec((h, C), lambda s, me_ref: (me_ref[0], s))
        sspec = pl.BlockSpec((h, C), lambda s, me_ref: (0, s))
    else:
        gspec = pl.BlockSpec((None, h, C), lambda s, me_ref: (s, me_ref[0], 0))
        sspec = pl.BlockSpec((None, h, C), lambda s, me_ref: (s, 0, 0))
    grid_spec = pltpu.PrefetchScalarGridSpec(
        num_scalar_prefetch=1, grid=(N_CHIPS,), in_specs=[gspec, sspec],
        out_specs=[sspec, pl.BlockSpec((h, C), lambda s, me_ref: (0, 0))])
    return _pallas(body, name=name, grid_spec=grid_spec,
                   out_shape=[_sds(_half_shape(*meta), BF16), _sds((h, C), F32)],
                   compiler_params=_params(("arbitrary",), VMEM_BIG))(me, g, sib)


def scatter_partials(p16s, metas):
    n = len(p16s)

    def copies(ins, outs, sems, s):
        ssem, rsem = sems
        x, y, c = _coords()
        chips = _other_chips(x, y)
        out = []
        for t, (kind, R, C) in enumerate(metas):
            for j in range(3):
                sj = _flip_index(s, j)
                src = ins[t].at[:, sj * C:(sj + 1) * C] if kind == "col" else ins[t].at[sj]
                out.append(pltpu.make_async_remote_copy(
                    src_ref=src, dst_ref=outs[t].at[j], send_sem=ssem.at[3 * t + j], recv_sem=rsem.at[3 * t + j],
                    device_id=(chips[j][0], chips[j][1], c), device_id_type=MESH))
        return out

    def start(ins, outs, sems):
        def run(s):
            for cp in copies(ins, outs, sems, s):
                cp.start()
        x, y, _ = _coords()
        _for_my_chip(2 * x + y, run)

    def wait(ins, outs, sems):
        def run(s):
            for cp in copies(ins, outs, sems, s):
                cp.wait()
        x, y, _ = _coords()
        _for_my_chip(2 * x + y, run)

    return Comm(p16s, [_sds((3, R // 2, C), BF16) for (_, R, C) in metas],
                [pltpu.SemaphoreType.DMA((3 * n,))] * 2, start, wait)


def final_sum(me, own, q, buf, l, meta, name):
    _, R, C = meta
    h = R // 2

    def body(me_ref, own_ref, q_ref, buf_ref, o_ref):
        del buf_ref
        o_ref[...] = ((own_ref[...] + q_ref[0].astype(F32)) + q_ref[1].astype(F32)) + q_ref[2].astype(F32)

    grid_spec = pltpu.PrefetchScalarGridSpec(
        num_scalar_prefetch=1, grid=(1,),
        in_specs=[pl.BlockSpec((h, C), lambda i, me_ref: (0, 0)),
                  pl.BlockSpec((3, h, C), lambda i, me_ref: (0, 0, 0)), ANY],
        out_specs=pl.BlockSpec((None, h, C), lambda i, me_ref: (l, me_ref[0], 0)))
    return _pallas(body, name=name, grid_spec=grid_spec, out_shape=_sds(buf.shape, F32),
                   input_output_aliases={3: 0},
                   compiler_params=_params(("arbitrary",), VMEM_BIG))(me, own, q, buf)


def share_final(bufs):
    n = len(bufs)

    def body(*refs):
        ins, outs = refs[:n], refs[n:2 * n]
        ssem, rsem = refs[2 * n:]
        del ins
        x, y, c = _coords()
        copies = []
        for t in range(n):
            R = bufs[t].shape[1]
            r0 = pl.multiple_of(c * (R // 2), 8)
            blk = outs[t].at[:, pl.ds(r0, R // 2), :]
            copies.append(pltpu.make_async_remote_copy(
                src_ref=blk, dst_ref=blk, send_sem=ssem.at[t], recv_sem=rsem.at[t],
                device_id=(x, y, 1 - c), device_id_type=MESH))
        for cp in copies:
            cp.start()
        for t in range(n):
            R = bufs[t].shape[1]
            r1 = pl.multiple_of((1 - c) * (R // 2), 8)
            other = outs[t].at[:, pl.ds(r1, R // 2), :]
            pltpu.make_async_remote_copy(
                src_ref=other, dst_ref=other, send_sem=ssem.at[t], recv_sem=rsem.at[t],
                device_id=(x, y, 1 - c), device_id_type=MESH).wait_recv()
        for cp in copies:
            cp.wait_send()

    out_shape = [_sds(b.shape, F32) for b in bufs]
    return _pallas(body, name="share_final", in_specs=[ANY] * n, out_specs=[ANY] * n, out_shape=out_shape,
                   input_output_aliases={t: t for t in range(n)},
                   scratch_shapes=[pltpu.SemaphoreType.DMA((n,))] * 2,
                   compiler_params=pltpu.CompilerParams(has_side_effects=True))(*bufs)


def allreduce_small(part):
    rows = part.shape[0]
    h = rows // 2

    def body(p_ref, o_ref, sib_buf, pair_buf, chip_buf, ssem, rsem):
        x, y, c = _coords()
        sibling = dict(device_id=(x, y, 1 - c), device_id_type=MESH)
        mine = pl.ds(pl.multiple_of(c * h, 8), h)
        theirs = pl.ds(pl.multiple_of((1 - c) * h, 8), h)

        swap = pltpu.make_async_remote_copy(src_ref=p_ref.at[theirs], dst_ref=sib_buf, send_sem=ssem.at[0],
                                            recv_sem=rsem.at[0], **sibling)
        swap.start()
        swap.wait()
        pair_buf[...] = p_ref[mine, :] + sib_buf[...]

        chips = _other_chips(x, y)
        sends = [pltpu.make_async_remote_copy(src_ref=pair_buf, dst_ref=chip_buf.at[j], send_sem=ssem.at[1 + j],
                                              recv_sem=rsem.at[1 + j], device_id=(chips[j][0], chips[j][1], c),
                                              device_id_type=MESH) for j in range(3)]
        for cp in sends:
            cp.start()
        for cp in sends:
            cp.wait()

        def total(s):
            terms = {s: pair_buf[...]}
            for j in range(3):
                terms[_flip_index(s, j)] = chip_buf[j]
            o_ref[mine, :] = ((terms[0] + terms[1]) + terms[2]) + terms[3]

        _for_my_chip(2 * x + y, total)

        back = pltpu.make_async_remote_copy(src_ref=o_ref.at[mine], dst_ref=o_ref.at[mine], send_sem=ssem.at[4],
                                            recv_sem=rsem.at[4], **sibling)
        back.start()
        pltpu.make_async_remote_copy(src_ref=o_ref.at[theirs], dst_ref=o_ref.at[theirs], send_sem=ssem.at[4],
                                     recv_sem=rsem.at[4], **sibling).wait_recv()
        back.wait_send()

    return _pallas(body, name="allreduce_small",
                   in_specs=[pl.BlockSpec(memory_space=pltpu.VMEM)], out_specs=pl.BlockSpec(memory_space=pltpu.VMEM),
                   out_shape=_sds((rows, 128), F32),
                   scratch_shapes=[pltpu.VMEM((h, 128), F32), pltpu.VMEM((h, 128), F32), pltpu.VMEM((3, h, 128), F32),
                                   pltpu.SemaphoreType.DMA((5,)), pltpu.SemaphoreType.DMA((5,))],
                   compiler_params=pltpu.CompilerParams(has_side_effects=True))(part)


def _rows128(a):
    flat = a.reshape(-1)
    rows = -(-flat.shape[0] // 128)
    rows8 = -(-rows // 8) * 8
    flat = jnp.pad(flat, (0, rows8 * 128 - flat.shape[0]))
    return flat.reshape(rows8, 128)


def kernel(x, norm_mix_g, norm_ffn_g, final_g, a_w_in, a_v_gain, a_w_s, a_b_s, a_w_out, b_w_qkv, b_rel_bias, b_w_out, ffn_w_gate, ffn_w_up, ffn_w_down, loss_target, m_norm_mix_g, m_norm_ffn_g, m_final_g, m_a_w_in, m_a_v_gain, m_a_w_s, m_a_b_s, m_a_w_out, m_b_w_qkv, m_b_rel_bias, m_b_w_out, m_ffn_w_gate, m_ffn_w_up, m_ffn_w_down, v_norm_mix_g, v_norm_ffn_g, v_final_g, v_a_w_in, v_a_v_gain, v_a_w_s, v_a_b_s, v_a_w_out, v_b_w_qkv, v_b_rel_bias, v_b_w_out, v_ffn_w_gate, v_ffn_w_up, v_ffn_w_down):
    T = x.shape[1]
    weights = dict(norm_mix_g=norm_mix_g, norm_ffn_g=norm_ffn_g, final_g=final_g, a_w_in=a_w_in, a_v_gain=a_v_gain,
                   a_w_s=a_w_s, a_b_s=a_b_s, a_w_out=a_w_out, b_w_qkv=b_w_qkv, b_rel_bias=b_rel_bias,
                   b_w_out=b_w_out, ffn_w_gate=ffn_w_gate, ffn_w_up=ffn_w_up, ffn_w_down=ffn_w_down)
    mom_m = dict(norm_mix_g=m_norm_mix_g, norm_ffn_g=m_norm_ffn_g, final_g=m_final_g, a_w_in=m_a_w_in,
                 a_v_gain=m_a_v_gain, a_w_s=m_a_w_s, a_b_s=m_a_b_s, a_w_out=m_a_w_out, b_w_qkv=m_b_w_qkv,
                 b_rel_bias=m_b_rel_bias, b_w_out=m_b_w_out, ffn_w_gate=m_ffn_w_gate, ffn_w_up=m_ffn_w_up,
                 ffn_w_down=m_ffn_w_down)
    mom_v = dict(norm_mix_g=v_norm_mix_g, norm_ffn_g=v_norm_ffn_g, final_g=v_final_g, a_w_in=v_a_w_in,
                 a_v_gain=v_a_v_gain, a_w_s=v_a_w_s, a_b_s=v_a_b_s, a_w_out=v_a_w_out, b_w_qkv=v_b_w_qkv,
                 b_rel_bias=v_b_rel_bias, b_w_out=v_b_w_out, ffn_w_gate=v_ffn_w_gate, ffn_w_up=v_ffn_w_up,
                 ffn_w_down=v_ffn_w_down)
    order = list(weights)
    transposed = ("ffn_w_gate", "ffn_w_up")
    for k in transposed:
        weights[k], mom_m[k], mom_v[k] = (jnp.swapaxes(a, 1, 2) for a in (weights[k], mom_m[k], mom_v[k]))

    xi, yi, ci = _coords()
    me = jnp.stack([ci, 2 * xi + yi]).astype(jnp.int32)

    shard16 = {k: cast_bf16(weights[k], "cast_" + k) for k in BIG}

    def layer_tensors(i):
        mix = ("a_w_in", "a_w_out") if i % 2 == 0 else ("b_w_qkv", "b_w_out")
        return [(k, i // 2) for k in mix] + [(k, i) for k in ("ffn_w_gate", "ffn_w_up", "ffn_w_down")]

    def gather_items(keys):
        return [(shard16[k], l, GATHER_KIND[k]) for k, l in keys]

    def grad_metas(keys):
        return [(GATHER_KIND[k],) + tuple(weights[k].shape[1:]) for k, _ in keys]

    FFN = ("ffn_w_gate", "ffn_w_up", "ffn_w_down")
    k0a = [("a_w_out", 0), ("ffn_w_gate", 0)]
    k0b = [("ffn_w_up", 0), ("ffn_w_down", 0)]
    k1a = [("b_w_qkv", 0), ("b_w_out", 0), ("ffn_w_gate", 1)]
    k1b = [("ffn_w_up", 1), ("ffn_w_down", 1)]
    k3a = [("b_w_qkv", 1), ("b_w_out", 1), ("ffn_w_gate", 3)]
    k3b = [("ffn_w_up", 3), ("ffn_w_down", 3)]
    plans = {
        "a_in_l0": [("g1", k0a)], "sgu_fwd_l0": [("g2", k0a), ("g1", k0b)], "a_out_l0": [("g2", k0b)],
        "ffn_up_l0": [("g1", k1a)], "ffn_down_l0": [("g2", k1a), ("g1", k1b)], "b_qkv_l1": [("g2", k1b)],
        "attn_fwd_l1": [("g1", layer_tensors(2))], "b_out_l1": [("g2", layer_tensors(2))],
        "ffn_up_l1": [("g1", k3a)], "ffn_down_l1": [("g2", k3a)],
        "a_in_l2": [("g1", k3b)], "sgu_fwd_l2": [("g2", k3b)],
        "ffn_bwd_dh_l2": [("ex", layer_tensors(3))], "sgu_bwd_l2": [("sc", layer_tensors(3))],
        "ffn_bwd_dh_l1": [("ex", layer_tensors(2))], "attn_bwd_l1": [("sc", layer_tensors(2))],
        "ffn_bwd_dh_l0": [("ex", layer_tensors(1))], "ffn_bwd_dhn_l0": [("sc", k1a)],
        "dffn_w_gate_l0": [("sc", [("ffn_w_up", 1)])], "dffn_w_up_l0": [("sc", [("ffn_w_down", 1)])],
        "a_out_bwd_l0": [("ex", [(k, 0) for k in FFN])],
        "sgu_bwd_l0": [("sc", [("ffn_w_gate", 0), ("ffn_w_up", 0)]), ("ex", [("a_w_out", 0)])],
        "dw_in_l0": [("sc", [("ffn_w_down", 0), ("a_w_out", 0)])],
    }
    part16, full16 = {}, {}
    sib, p16, own_parts, recv_parts = {}, {}, {}, {}

    def make_comm(kind, keys):
        if kind == "g1":
            return gather_stage1(gather_items(keys)), lambda outs: part16.update(zip(keys, outs))
        if kind == "g2":
            return (gather_stage2(gather_items(keys), [part16[kl] for kl in keys]),
                    lambda outs: full16.update(zip(keys, outs)))
        if kind == "ex":
            return (exchange_halves([big_grads[k][l] for k, l in keys], grad_metas(keys)),
                    lambda outs: sib.update(zip(keys, outs)))
        for kl, m_ in zip(keys, grad_metas(keys)):
            p16[kl], own_parts[kl] = pair_sum(me, big_grads[kl[0]][kl[1]], sib[kl], m_, "pair_sum_%s_l%d" % kl)
        return (scatter_partials([p16[kl] for kl in keys], grad_metas(keys)),
                lambda outs: recv_parts.update(zip(keys, outs)))

    def run(name, make):
        steps = plans.get(name)
        if not steps:
            return make(None)
        made = [make_comm(kind, keys) for kind, keys in steps]
        main, outs = make(combine([c for c, _ in made]))
        for c, done in made:
            done(outs[:len(c.out_shapes)])
            outs = outs[len(c.out_shapes):]
        return main

    def weight(k, l):
        w = full16[(k, l)]
        if k == "a_w_out":
            return w.reshape(1, GH, D)
        return w.reshape(1, D, D) if k == "b_w_out" else w

    full16[("a_w_in", 0)] = run_comm(gather_both(gather_items([("a_w_in", 0)])), "gather_first")[0]

    xcur = x.reshape(T, D)
    hn = rms_fwd(xcur, norm_mix_g[0][None], "rms_mix_l0")
    saved = []
    for i in range(DEPTH):
        j = i // 2
        tag = "_l%d" % i
        st = {"x_in": xcur, "hn": hn}
        if i % 2 == 0:
            pre = run("a_in" + tag, lambda comm: matmul(
                "a_in" + tag, NN, hn, pl.BlockSpec((1024, D), lambda i_, j_: (i_, 0)),
                weight("a_w_in", j), pl.BlockSpec((None, D, 1024), lambda i_, j_: (0, 0, j_)),
                _sds((T, 2 * GH), BF16), pl.BlockSpec((1024, 1024), lambda i_, j_: (i_, j_)),
                (T // 1024, 4), comm=comm))
            y = run("sgu_fwd" + tag, lambda comm: sgu_fwd(
                pre, a_v_gain[j][None], a_w_s[j], a_b_s[j][:, :, None], "sgu_fwd" + tag, comm=comm))
            xmid, hn2 = run("a_out" + tag, lambda comm: residual_proj(
                "a_out" + tag, *out_proj(y, weight("a_w_out", j)), xcur, norm_ffn_g[i][None], comm=comm))
            st.update(pre=pre, y=y)
        else:
            qkvp = run("b_qkv" + tag, lambda comm: proj_qkv(hn, weight("b_w_qkv", j), 0, "b_qkv" + tag, comm=comm))
            wb = jnp.transpose(bias_build(b_rel_bias[j], "bias_build" + tag), (1, 0, 2))
            o = run("attn_fwd" + tag, lambda comm: attn_fwd(qkvp, wb, "attn_fwd" + tag, comm=comm))
            xmid, hn2 = run("b_out" + tag, lambda comm: residual_proj(
                "b_out" + tag, *out_proj(o, weight("b_w_out", j)), xcur, norm_ffn_g[i][None], comm=comm))
            st.update(qkvp=qkvp, wb=wb, o=o)
        g, u, h = run("ffn_up" + tag, lambda comm: ffn_up(
            hn2, weight("ffn_w_gate", i), weight("ffn_w_up", i), 0, "ffn_up" + tag, comm=comm))
        next_g = norm_mix_g[i + 1][None] if i + 1 < DEPTH else None
        down = run("ffn_down" + tag, lambda comm: residual_proj(
            "ffn_down" + tag, *ffn_down(h, weight("ffn_w_down", i)), xmid, next_g, comm=comm))
        xcur, hn = down if next_g is not None else (down, None)
        st.update(x_mid=xmid, hn2=hn2, g=g, u=u, h=h)
        saved.append(st)

    loss_part, dx, dxb, d_final = final_loss(xcur, final_g[None], loss_target.reshape(T, D), "final_loss")

    tk = min(2048, T)
    big_grads = {k: [None] * weights[k].shape[0] for k in BIG}
    small = {"norm_mix_g": [None] * DEPTH, "norm_ffn_g": [None] * DEPTH, "a_v_gain": [None] * 2,
             "a_w_s": [None] * 2, "a_b_s": [None] * 2, "b_rel_bias": [None] * 2}
    tok = lambda width: pl.BlockSpec((tk, width), lambda j_, k_: (k_, 0))
    part = lambda: pl.BlockSpec((None, tk, FS), lambda j_, k_: (j_, k_, 0))
    for i in reversed(range(DEPTH)):
        j = i // 2
        tag = "_l%d" % i
        st = saved[i]
        dg, du = run("ffn_bwd_dh" + tag, lambda comm: ffn_bwd_dh(
            dxb, weight("ffn_w_down", i), st["g"], st["u"], 0, "ffn_bwd_dh" + tag, comm=comm))
        big_grads["ffn_w_down"][i] = wgrad(
            "dw_down" + tag, st["h"], part(), dxb, tok(D), _sds((N_CHIPS, FS, D), F32),
            pl.BlockSpec((None, FS, D), lambda j_, k_: (j_, 0, 0)), N_CHIPS, T, tk)
        dx_mid, dxb_mid, dgn = run("ffn_bwd_dhn" + tag, lambda comm: dgrad_rms(
            "ffn_bwd_dhn" + tag, *ffn_dgrad(dg, du, weight("ffn_w_gate", i), weight("ffn_w_up", i)),
            st["x_mid"], norm_ffn_g[i][None], dx, comm=comm))
        for nm, dz in (("ffn_w_gate", dg), ("ffn_w_up", du)):
            big_grads[nm][i] = run("d" + nm + tag, lambda comm: wgrad(
                "d" + nm + tag, dz, part(), st["hn2"], tok(D), _sds((N_CHIPS, FS, D), F32),
                pl.BlockSpec((None, FS, D), lambda j_, k_: (j_, 0, 0)), N_CHIPS, T, tk, comm=comm))
        dx, dxb = dx_mid, dxb_mid
        small["norm_ffn_g"][i] = dgn
        if i % 2 == 0:
            dy = run("a_out_bwd" + tag, lambda comm: matmul(
                "a_out_bwd" + tag, NT, dxb, pl.BlockSpec((1024, D), lambda i_, j_: (i_, 0)),
                weight("a_w_out", j), pl.BlockSpec((None, 1024, D), lambda i_, j_: (0, j_, 0)),
                _sds((T, GH), BF16), pl.BlockSpec((1024, 1024), lambda i_, j_: (i_, j_)), (T // 1024, 2), comm=comm))
            big_grads["a_w_out"][j] = wgrad(
                "dw_aout" + tag, st["y"], pl.BlockSpec((tk, 1024), lambda j_, k_: (k_, j_)), dxb, tok(D),
                _sds((GH, D), F32), pl.BlockSpec((1024, D), lambda j_, k_: (j_, 0)), 2, T, tk
            ).reshape(N_CHIPS, GH // N_CHIPS, D)
            dpre, d_ws, d_bs, d_gain = run("sgu_bwd" + tag, lambda comm: sgu_bwd(
                st["pre"], dy, a_v_gain[j][None], a_w_s[j], a_b_s[j][:, :, None], "sgu_bwd" + tag, comm=comm))
            small["a_w_s"][j], small["a_b_s"][j], small["a_v_gain"][j] = d_ws, d_bs, d_gain
            dx_in, dxb_in, dgn = run("a_in_bwd" + tag, lambda comm: dgrad_rms(
                "a_in_bwd" + tag, *in_dgrad(dpre, weight("a_w_in", j)),
                st["x_in"], norm_mix_g[i][None], dx, comm=comm))
            big_grads["a_w_in"][j] = run("dw_in" + tag, lambda comm: wgrad(
                "dw_in" + tag, st["hn"], tok(D), dpre, pl.BlockSpec((tk, 1024), lambda j_, k_: (k_, j_)),
                _sds((D, 2 * GH), F32), pl.BlockSpec((D, 1024), lambda j_, k_: (0, j_)), 4, T, tk, comm=comm))
        else:
            do = matmul("b_out_bwd" + tag, NT, dxb, pl.BlockSpec((1024, D), lambda i_, j_: (i_, 0)),
                        weight("b_w_out", j), pl.BlockSpec((None, D, D), lambda i_, j_: (0, 0, 0)),
                        _sds((T, D), BF16), pl.BlockSpec((1024, D), lambda i_, j_: (i_, 0)), (T // 1024, 1))
            big_grads["b_w_out"][j] = wgrad(
                "dw_bout" + tag, st["o"], tok(D), dxb, tok(D),
                _sds((D, D), F32), pl.BlockSpec((D, D), lambda j_, k_: (0, 0)), 1, T, tk
            ).reshape(N_CHIPS, D // N_CHIPS, D)
            dqkvp, dwb = run("attn_bwd" + tag, lambda comm: attn_bwd(
                st["qkvp"], st["o"], do, st["wb"], "attn_bwd" + tag, comm=comm))
            small["b_rel_bias"][j] = bias_grad(
                jnp.pad(jnp.transpose(dwb, (1, 0, 2)), ((0, 0), (0, 0), (0, DIAG - KW))), "bias_grad" + tag)
            dx_in, dxb_in, dgn = dgrad_rms(
                "b_qkv_bwd" + tag, *qkv_dgrad(dqkvp, weight("b_w_qkv", j)),
                st["x_in"], norm_mix_g[i][None], dx)
            big_grads["b_w_qkv"][j] = wgrad(
                "dw_qkv" + tag, st["hn"], tok(D), dqkvp,
                pl.BlockSpec((None, tk, D), lambda j_, k_: (j_, k_ + FRONT // tk, 0)),
                _sds((D, 3 * D), F32), pl.BlockSpec((D, D), lambda j_, k_: (0, j_)), 3, T, tk)
        dx, dxb = dx_in, dxb_in
        small["norm_mix_g"][i] = dgn

    small_grads = {
        "norm_mix_g": jnp.concatenate(small["norm_mix_g"], axis=0),
        "norm_ffn_g": jnp.concatenate(small["norm_ffn_g"], axis=0),
        "final_g": d_final.reshape(D),
        "a_v_gain": jnp.concatenate(small["a_v_gain"], axis=0),
        "a_w_s": jnp.stack(small["a_w_s"]),
        "a_b_s": jnp.stack(small["a_b_s"]).reshape(2, SGU_G, SGU_BLOCK),
        "b_rel_bias": jnp.stack(small["b_rel_bias"]),
    }
    small_names = list(small_grads)
    packed = [_rows128(small_grads[k]) for k in small_names] + [_rows128(loss_part[:, :1])]
    offs = [0]
    for p in packed:
        offs.append(offs[-1] + p.shape[0])
    reduced = allreduce_small(jnp.concatenate(packed, axis=0))
    grads = {}
    for t, k in enumerate(small_names):
        nelem = small_grads[k].size
        grads[k] = reduced[offs[t]:offs[t + 1]].reshape(-1)[:nelem].reshape(weights[k].shape)
    loss = reduced[offs[len(small_names)], 0]

    last = [("a_w_in", 0)]
    for kind, name in (("ex", "exchange_last"), ("sc", "scatter_last")):
        comm, done = make_comm(kind, last)
        done(run_comm(comm, name))
    bufs = {k: jnp.zeros(weights[k].shape, F32) for k in BIG}
    for i in range(DEPTH):
        for kl, m_ in zip(layer_tensors(i), grad_metas(layer_tensors(i))):
            bufs[kl[0]] = final_sum(me, own_parts[kl], recv_parts[kl], bufs[kl[0]], kl[1], m_,
                                    "final_sum_%s_l%d" % kl)
    shared = share_final([bufs[k] for k in BIG])
    for k, gfull in zip(BIG, shared):
        grads[k] = gfull

    delta, new_m, new_v = {}, {}, {}
    for k in order:
        shp = weights[k].shape
        if k in BIG:
            view = shp
        elif k == "a_w_s":
            view = (2, SGU_G * SGU_BLOCK, SGU_BLOCK)
        elif len(shp) == 1:
            view = (1, 1, shp[0])
        elif len(shp) == 2:
            view = (1,) + shp
        else:
            view = shp
        d_, m_, v_ = adamw(weights[k].reshape(view), grads[k].reshape(view), mom_m[k].reshape(view),
                           mom_v[k].reshape(view), "adamw_" + k)
        delta[k], new_m[k], new_v[k] = d_.reshape(shp), m_.reshape(shp), v_.reshape(shp)
    for k in transposed:
        for tree in (grads, delta, new_m, new_v):
            tree[k] = jnp.swapaxes(tree[k], 1, 2)

    return (loss, dx.reshape(1, T, D), *[grads[k] for k in order], *[delta[k] for k in order],
            *[new_m[k] for k in order], *[new_v[k] for k in order])
```

```python
import functools

import jax
import jax.numpy as jnp
from jax import lax
from jax.experimental import pallas as pl
from jax.experimental.pallas import tpu as pltpu

F32 = jnp.float32
BF16 = jnp.bfloat16
MESH = pl.DeviceIdType.MESH

D = 1024
DEPTH = 4
EPS = 1e-6
SGU_BLOCK = 128
GH = 2048
SGU_G = 8
SGU_GD = GH // SGU_G
N_HEADS = 16
HEAD_DIM = 64
CHUNK = 64
PAD = 8 * CHUNK
FRONT = 2048
QB = 128
KW = PAD + QB
N_REL = 192
REL_MIN = -(CHUNK - 1)
REL_MAX = 128
D_FF = 2816
FS = D_FF // 4
NEG = -1e30
SCALE = HEAD_DIM ** -0.5
N_CHIPS = 4

ADAM_LR = 0.001
ADAM_B1 = 0.9
ADAM_B2 = 0.999
ADAM_EPS = 1e-08
ADAM_WD = 0.01
ADAM_STEP = 10

VMEM_BIG = 56 * 1024 * 1024

NN = ((1,), (0,))
NT = ((1,), (1,))
TN = ((0,), (0,))


def _dot(a, b, dims):
    return lax.dot_general(a, b, (dims, ((), ())), preferred_element_type=F32)


class Comm:
    def __init__(self, ins, out_shapes, sems, start, wait, aliases=None):
        self.ins, self.out_shapes, self.sems = list(ins), list(out_shapes), list(sems)
        self.start, self.wait, self.aliases = start, wait, dict(aliases or {})


def _host(body, comm, kw):
    grid = tuple(kw["grid"])
    in_specs = list(kw["in_specs"])
    single = not isinstance(kw["out_specs"], (list, tuple))
    out_specs = [kw["out_specs"]] if single else list(kw["out_specs"])
    out_shape = [kw["out_shape"]] if single else list(kw["out_shape"])
    scratch = list(kw.get("scratch_shapes", ()))
    counts = (len(in_specs), len(comm.ins), len(out_specs), len(comm.out_shapes), len(scratch))

    def hosted(*refs):
        parts, p = [], 0
        for cnt in counts:
            parts.append(refs[p:p + cnt])
            p += cnt
        main_in, c_in, main_out, c_out, main_scr = parts
        sems = refs[p:]
        ids = [pl.program_id(a) for a in range(len(grid))]
        first = functools.reduce(jnp.logical_and, [i == 0 for i in ids])
        last = functools.reduce(jnp.logical_and, [i == n - 1 for i, n in zip(ids, grid)])
        pl.when(first)(lambda: comm.start(c_in, c_out, sems))
        body(*main_in, *main_out, *main_scr)
        pl.when(last)(lambda: comm.wait(c_in, c_out, sems))

    old = kw["compiler_params"]
    kw = dict(kw, in_specs=in_specs + [ANY] * len(comm.ins), out_specs=out_specs + [ANY] * len(comm.out_shapes),
              out_shape=out_shape + comm.out_shapes, scratch_shapes=scratch + comm.sems,
              compiler_params=pltpu.CompilerParams(dimension_semantics=("arbitrary",) * len(grid),
                                                   vmem_limit_bytes=old.vmem_limit_bytes, has_side_effects=True))
    if comm.aliases:
        kw["input_output_aliases"] = {counts[0] + i: counts[2] + o for i, o in comm.aliases.items()}
    return hosted, kw


def _pallas(body, comm=None, **kw):
    if comm is not None:
        body, kw = _host(body, comm, kw)
    return pl.pallas_call(body, **kw)


def _split_outs(outs, comm, n_main):
    outs = list(outs) if isinstance(outs, (list, tuple)) else [outs]
    main = outs[:n_main]
    return (main[0] if n_main == 1 else main), outs[n_main:]


def run_comm(comm, name):
    nci, nco = len(comm.ins), len(comm.out_shapes)

    def body(*refs):
        c_in, c_out, sems = refs[:nci], refs[nci:nci + nco], refs[nci + nco:]
        comm.start(c_in, c_out, sems)
        comm.wait(c_in, c_out, sems)

    kw = {}
    if comm.aliases:
        kw["input_output_aliases"] = dict(comm.aliases)
    return _pallas(body, name=name, in_specs=[ANY] * nci, out_specs=[ANY] * nco, out_shape=comm.out_shapes,
                   scratch_shapes=comm.sems, compiler_params=pltpu.CompilerParams(has_side_effects=True),
                   **kw)(*comm.ins)


def combine(comms):
    if len(comms) == 1:
        return comms[0]
    spans, ni, no, ns = [], 0, 0, 0
    for c in comms:
        spans.append((slice(ni, ni + len(c.ins)), slice(no, no + len(c.out_shapes)), slice(ns, ns + len(c.sems))))
        ni, no, ns = ni + len(c.ins), no + len(c.out_shapes), ns + len(c.sems)

    def start(ins, outs, sems):
        for c, (si, so, ss) in zip(comms, spans):
            c.start(ins[si], outs[so], sems[ss])

    def wait(ins, outs, sems):
        for c, (si, so, ss) in zip(comms, spans):
            c.wait(ins[si], outs[so], sems[ss])

    aliases = {}
    for c, (si, so, _) in zip(comms, spans):
        aliases.update({si.start + i: so.start + o for i, o in c.aliases.items()})
    return Comm([a for c in comms for a in c.ins], [o for c in comms for o in c.out_shapes],
                [s for c in comms for s in c.sems], start, wait, aliases)


def _call(body, comm, n_main, args, **kw):
    if comm is None:
        return _pallas(body, **kw)(*args)
    return _split_outs(_pallas(body, comm=comm, **kw)(*args, *comm.ins), comm, n_main)


def _params(sem=None, vmem=None):
    return pltpu.CompilerParams(dimension_semantics=sem, vmem_limit_bytes=vmem)


def _sds(shape, dtype):
    return jax.ShapeDtypeStruct(tuple(shape), dtype)


_GELU_C = 0.7978845608028654


_GELU_A = _GELU_C * 0.044715


def _gelu(x):
    t = jnp.tanh(x * (_GELU_C + _GELU_A * (x * x)))
    h = 0.5 * x
    return h + h * t


def _gelu_and_grad(x):
    x2 = x * x
    t = jnp.tanh(x * (_GELU_C + _GELU_A * x2))
    h = 0.5 * x
    val = h + h * t
    grad = (0.5 + 0.5 * t) + (h * (1.0 - t * t)) * (_GELU_C + (3.0 * _GELU_A) * x2)
    return val, grad


def _sigmoid(x):
    return 0.5 * (jnp.tanh(0.5 * x) + 1.0)


def cast_bf16(w, name):
    L, R, C = w.shape

    def body(w_ref, o_ref):
        o_ref[...] = w_ref[...].astype(BF16)

    spec = pl.BlockSpec((None, R, C), lambda l: (l, 0, 0))
    return _pallas(body, name=name, grid=(L,), in_specs=[spec], out_specs=spec,
                   out_shape=_sds((L, R, C), BF16), compiler_params=_params(("parallel",)))(w)


def rms_fwd(x, g, name, tm=512):
    T = x.shape[0]

    def body(x_ref, g_ref, o_ref):
        xf = x_ref[...]
        r = lax.rsqrt(jnp.mean(xf * xf, axis=-1, keepdims=True) + EPS)
        o_ref[...] = ((xf * r) * g_ref[...]).astype(BF16)

    row = pl.BlockSpec((tm, D), lambda i: (i, 0))
    return _pallas(body, name=name, grid=(T // tm,),
                   in_specs=[row, pl.BlockSpec((1, D), lambda i: (0, 0))], out_specs=row,
                   out_shape=_sds((T, D), BF16), compiler_params=_params(("parallel",)))(x, g)


def dgrad_rms(name, compute, args, specs, x, g, dres, tm=512, comm=None):
    T = x.shape[0]
    n = T // tm
    k = len(args)

    def body(*refs):
        x_ref, g_ref, dres_ref, dx_ref, dxb_ref, dg_ref, acc_ref = refs[k:]
        i = pl.program_id(0)
        xf = x_ref[...]
        r = lax.rsqrt(jnp.mean(xf * xf, axis=-1, keepdims=True) + EPS)
        xhat = xf * r
        dhf = compute(*refs[:k])
        part = (dhf * xhat).reshape(tm // 8, 8, D).sum(axis=0)

        @pl.when(i == 0)
        def _():
            acc_ref[...] = part

        @pl.when(i > 0)
        def _():
            acc_ref[...] += part

        dxhat = dhf * g_ref[...]
        dx = dres_ref[...] + r * (dxhat - xhat * jnp.mean(dxhat * xhat, axis=-1, keepdims=True))
        dx_ref[...] = dx
        dxb_ref[...] = dx.astype(BF16)

        @pl.when(i == n - 1)
        def _():
            dg_ref[...] = jnp.sum(acc_ref[...], axis=0, keepdims=True)

    row = pl.BlockSpec((tm, D), lambda i: (i, 0))
    vec = pl.BlockSpec((1, D), lambda i: (0, 0))
    return _call(body, comm, 3, (*args, x, g, dres), name=name, grid=(n,),
                 in_specs=list(specs) + [row, vec, row], out_specs=[row, row, vec],
                 out_shape=[_sds((T, D), F32), _sds((T, D), BF16), _sds((1, D), F32)],
                 scratch_shapes=[pltpu.VMEM((8, D), F32)],
                 compiler_params=_params(("arbitrary",), VMEM_BIG))


def final_loss(x, g, tgt, name, tm=256):
    T = x.shape[0]
    n = T // tm

    def body(x_ref, g_ref, t_ref, loss_ref, dx_ref, dxb_ref, dg_ref, acc_ref, lacc_ref):
        i = pl.program_id(0)
        xf = x_ref[...]
        r = lax.rsqrt(jnp.mean(xf * xf, axis=-1, keepdims=True) + EPS)
        xhat = xf * r
        gg = g_ref[...]
        e = xhat * gg - t_ref[...]
        dy = e * (1.0 / D)
        part = (dy * xhat).reshape(tm // 8, 8, D).sum(axis=0)
        lpart = (e * e).reshape(tm // 8, 8, D).sum(axis=0)

        @pl.when(i == 0)
        def _():
            acc_ref[...] = part
            lacc_ref[...] = lpart

        @pl.when(i > 0)
        def _():
            acc_ref[...] += part
            lacc_ref[...] += lpart

        dxhat = dy * gg
        dx = r * (dxhat - xhat * jnp.mean(dxhat * xhat, axis=-1, keepdims=True))
        dx_ref[...] = dx
        dxb_ref[...] = dx.astype(BF16)

        @pl.when(i == n - 1)
        def _():
            dg_ref[...] = jnp.sum(acc_ref[...], axis=0, keepdims=True)
            total = jnp.sum(jnp.sum(lacc_ref[...], axis=0, keepdims=True), axis=1, keepdims=True)
            loss_ref[...] = jnp.broadcast_to(total * (0.5 / D), (1, 128))

    row = pl.BlockSpec((tm, D), lambda i: (i, 0))
    vec = pl.BlockSpec((1, D), lambda i: (0, 0))
    return _pallas(body, name=name, grid=(n,), in_specs=[row, vec, row],
                   out_specs=[pl.BlockSpec((1, 128), lambda i: (0, 0)), row, row, vec],
                   out_shape=[_sds((1, 128), F32), _sds((T, D), F32), _sds((T, D), BF16), _sds((1, D), F32)],
                   scratch_shapes=[pltpu.VMEM((8, D), F32), pltpu.VMEM((8, D), F32)],
                   compiler_params=_params(("arbitrary",)))(x, g, tgt)


def matmul(name, dims, a, a_spec, b, b_spec, out_shape, out_spec, grid, *, acc=False, res=None, res_spec=None,
           comm=None):
    has_res = res is not None

    def body(*refs):
        a_ref, b_ref = refs[0], refs[1]
        r_ref = refs[2] if has_res else None
        o_ref = refs[-1]
        d = _dot(a_ref[...], b_ref[...], dims)
        if not acc:
            if has_res:
                d = d + r_ref[...]
            o_ref[...] = d.astype(o_ref.dtype)
        else:
            k = pl.program_id(len(grid) - 1)

            @pl.when(k == 0)
            def _():
                o_ref[...] = (d + r_ref[...]) if has_res else d

            @pl.when(k > 0)
            def _():
                o_ref[...] += d

    sem = ("parallel",) * (len(grid) - 1) + (("arbitrary",) if acc else ("parallel",))
    ins = [a, b] + ([res] if has_res else [])
    specs = [a_spec, b_spec] + ([res_spec] if has_res else [])
    return _call(body, comm, 1, ins, name=name, grid=grid, in_specs=specs, out_specs=out_spec, out_shape=out_shape,
                 compiler_params=_params(sem, VMEM_BIG))


def wgrad(name, a, a_spec, b, b_spec, out_shape, out_spec, J, T, tk, comm=None):
    return matmul(name, TN, a, a_spec, b, b_spec, out_shape, out_spec, (J, T // tk), acc=True, comm=comm)


def _sgu_mask():
    p = lax.broadcasted_iota(jnp.int32, (SGU_BLOCK, SGU_BLOCK), 0)
    q = lax.broadcasted_iota(jnp.int32, (SGU_BLOCK, SGU_BLOCK), 1)
    return lax.shift_right_logical(q, 6) <= lax.shift_right_logical(p, 6)


def sgu_fwd(pre, gain, w_s, b_s, name, comm=None):
    T = pre.shape[0]

    def body(pre_ref, gain_ref, ws_ref, bs_ref, y_ref):
        mask = _sgu_mask()
        u = _gelu(pre_ref[:, :GH].astype(F32))
        va = _gelu(pre_ref[:, GH:].astype(F32))
        r = lax.rsqrt(jnp.mean(va * va, axis=-1, keepdims=True) + EPS)
        vn = ((va * r) * gain_ref[...]).astype(BF16)
        for g in range(SGU_G):
            sl = slice(g * SGU_GD, (g + 1) * SGU_GD)
            wm = jnp.where(mask, ws_ref[g], 0.0).astype(BF16)
            vm = _dot(wm, vn[:, sl], NN) + bs_ref[g]
            y_ref[:, sl] = (u[:, sl] * vm).astype(BF16)

    return _call(
        body, comm, 1, (pre, gain, w_s, b_s), name=name, grid=(T // SGU_BLOCK,),
        in_specs=[pl.BlockSpec((SGU_BLOCK, 2 * GH), lambda i: (i, 0)),
                  pl.BlockSpec((1, GH), lambda i: (0, 0)),
                  pl.BlockSpec((SGU_G, SGU_BLOCK, SGU_BLOCK), lambda i: (0, 0, 0)),
                  pl.BlockSpec((SGU_G, SGU_BLOCK, 1), lambda i: (0, 0, 0))],
        out_specs=pl.BlockSpec((SGU_BLOCK, GH), lambda i: (i, 0)),
        out_shape=_sds((T, GH), BF16), compiler_params=_params(("parallel",)))


def sgu_in_bwd(pre, dy, gain, w_s, b_s, w_in, x, g, dres, name, tm=256, comm=None):
    T = pre.shape[0]
    n = T // tm

    def body(pre_ref, dy_ref, gain_ref, ws_ref, bs_ref, w_ref, x_ref, g_ref, dres_ref,
             dpre_ref, dws_ref, dbs_ref, dgain_ref, dx_ref, dxb_ref, dg_ref, stage, gacc_ref, nacc_ref):
        i = pl.program_id(0)

        @pl.when(i == 0)
        def _():
            dws_ref[...] = jnp.zeros_like(dws_ref)
            dbs_ref[...] = jnp.zeros_like(dbs_ref)
            gacc_ref[...] = jnp.zeros_like(gacc_ref)
            nacc_ref[...] = jnp.zeros_like(nacc_ref)
            stage[...] = jnp.zeros_like(stage)

        live = (i < n).astype(F32)
        slot = lax.rem(i, 2)

        xf = x_ref[...]
        rx = lax.rsqrt(jnp.mean(xf * xf, axis=-1, keepdims=True) + EPS)
        xhat = xf * rx
        dhf = _dot(stage[1 - slot], w_ref[...], NT)
        nacc_ref[...] += (dhf * xhat).reshape(tm // 8, 8, D).sum(axis=0)
        dxhat = dhf * g_ref[...]
        dx = dres_ref[...] + rx * (dxhat - xhat * jnp.mean(dxhat * xhat, axis=-1, keepdims=True))
        dx_ref[...] = dx
        dxb_ref[...] = dx.astype(BF16)

        mask = _sgu_mask()
        gain_v = gain_ref[...]
        for sb in range(tm // SGU_BLOCK):
            rows = slice(sb * SGU_BLOCK, (sb + 1) * SGU_BLOCK)
            u, du_dpre = _gelu_and_grad(pre_ref[rows, :GH].astype(F32))
            va, dva_dpre = _gelu_and_grad(pre_ref[rows, GH:].astype(F32))
            r = lax.rsqrt(jnp.mean(va * va, axis=-1, keepdims=True) + EPS)
            vhat = va * r
            vn = (vhat * gain_v).astype(BF16)
            dyf = dy_ref[rows, :].astype(F32)
            dvn_parts = []
            for grp in range(SGU_G):
                sl = slice(grp * SGU_GD, (grp + 1) * SGU_GD)
                wm = jnp.where(mask, ws_ref[grp], 0.0).astype(BF16)
                vm = _dot(wm, vn[:, sl], NN) + bs_ref[grp]
                dpre_u = ((dyf[:, sl] * vm) * du_dpre[:, sl]).astype(BF16)
                dpre_ref[rows, sl] = dpre_u
                stage[slot, rows, sl] = dpre_u
                dvm = dyf[:, sl] * u[:, sl]
                dbs_ref[grp] += live * jnp.sum(dvm, axis=-1, keepdims=True)
                dvm16 = dvm.astype(BF16)
                dws_ref[grp] += jnp.where(mask, live * _dot(dvm16, vn[:, sl], NT), 0.0)
                dvn_parts.append(_dot(wm, dvm16, TN))
            dvn = jnp.concatenate(dvn_parts, axis=-1)
            gacc_ref[...] += live * (dvn * vhat).reshape(SGU_BLOCK // 8, 8, GH).sum(axis=0)
            dvhat = dvn * gain_v
            dva = r * (dvhat - vhat * jnp.mean(dvhat * vhat, axis=-1, keepdims=True))
            dpre_v = (dva * dva_dpre).astype(BF16)
            dpre_ref[rows, GH:] = dpre_v
            stage[slot, rows, GH:] = dpre_v

        @pl.when(i == n)
        def _():
            dgain_ref[...] = jnp.sum(gacc_ref[...], axis=0, keepdims=True)
            dg_ref[...] = jnp.sum(nacc_ref[...], axis=0, keepdims=True)

    const3 = lambda i: (0, 0, 0)
    cur = lambda i: (jnp.minimum(i, n - 1), 0)
    prev = lambda i: (jnp.maximum(i - 1, 0), 0)
    vec = lambda width: pl.BlockSpec((1, width), lambda i: (0, 0))
    return _call(
        body, comm, 7, (pre, dy, gain, w_s, b_s, w_in, x, g, dres), name=name, grid=(n + 1,),
        in_specs=[pl.BlockSpec((tm, 2 * GH), cur), pl.BlockSpec((tm, GH), cur), vec(GH),
                  pl.BlockSpec((SGU_G, SGU_BLOCK, SGU_BLOCK), const3),
                  pl.BlockSpec((SGU_G, SGU_BLOCK, 1), const3),
                  pl.BlockSpec((None, D, 2 * GH), const3, pipeline_mode=pl.Buffered(1)),
                  pl.BlockSpec((tm, D), prev), vec(D), pl.BlockSpec((tm, D), prev)],
        out_specs=[pl.BlockSpec((tm, 2 * GH), cur),
                   pl.BlockSpec((SGU_G, SGU_BLOCK, SGU_BLOCK), const3),
                   pl.BlockSpec((SGU_G, SGU_BLOCK, 1), const3), vec(GH),
                   pl.BlockSpec((tm, D), prev), pl.BlockSpec((tm, D), prev), vec(D)],
        out_shape=[_sds((T, 2 * GH), BF16), _sds((SGU_G, SGU_BLOCK, SGU_BLOCK), F32),
                   _sds((SGU_G, SGU_BLOCK, 1), F32), _sds((1, GH), F32),
                   _sds((T, D), F32), _sds((T, D), BF16), _sds((1, D), F32)],
        scratch_shapes=[pltpu.VMEM((2, tm, 2 * GH), BF16), pltpu.VMEM((8, GH), F32), pltpu.VMEM((8, D), F32)],
        compiler_params=_params(("arbitrary",), VMEM_BIG))


DIAG = 768


def _diag_onehot():
    n = lax.broadcasted_iota(jnp.int32, (N_REL, DIAG), 1)
    r = lax.broadcasted_iota(jnp.int32, (N_REL, DIAG), 0)
    idx = jnp.clip(KW - 1 - n, REL_MIN, REL_MAX) - REL_MIN
    return (idx == r).astype(BF16)


def _split3(v):
    hi = v.astype(BF16)
    r1 = v - hi.astype(F32)
    mid = r1.astype(BF16)
    lo = (r1 - mid.astype(F32)).astype(BF16)
    return hi, mid, lo


def bias_build(rel_bias, name):
    def body(rb_ref, o_ref):
        oh = _diag_onehot()
        hi, mid, lo = _split3(rb_ref[...])
        u = (_dot(hi, oh, NN) + _dot(mid, oh, NN) + _dot(lo, oh, NN)) * LOG2E
        j = lax.broadcasted_iota(jnp.int32, (1, KW), 1)

        def row(i, carry):
            val = pltpu.roll(u, (i + (DIAG - QB + 1)) % DIAG, 1)[:, :KW]
            rel = lax.shift_right_logical(i, 6) - lax.shift_right_logical(j, 6) + 8
            ok = (rel >= 0) & (rel <= 8)
            o_ref[i] = jnp.where(ok, val, NEG)
            return carry

        lax.fori_loop(0, QB, row, 0)

    return _pallas(body, name=name, out_shape=_sds((QB, N_HEADS, KW), F32),
                   in_specs=[pl.BlockSpec(memory_space=pltpu.VMEM)],
                   out_specs=pl.BlockSpec(memory_space=pltpu.VMEM))(rel_bias)


def bias_grad(dwb, name):
    def body(d_ref, o_ref):
        def row(i, acc):
            return acc + pltpu.roll(d_ref[i], QB - 1 - i, 1)

        du = lax.fori_loop(0, QB, row, jnp.zeros((N_HEADS, DIAG), F32))
        oh = _diag_onehot()
        hi, mid, lo = _split3(du)
        o_ref[...] = _dot(hi, oh, NT) + _dot(mid, oh, NT) + _dot(lo, oh, NT)

    return _pallas(body, name=name, out_shape=_sds((N_HEADS, N_REL), F32),
                   in_specs=[pl.BlockSpec(memory_space=pltpu.VMEM)],
                   out_specs=pl.BlockSpec(memory_space=pltpu.VMEM))(dwb)


LOG2E = 1.4426950408889634
Q_SCALE = SCALE * LOG2E


def _attn_block(qkv_ref, blk, masked):
    r0 = pl.multiple_of(blk * QB, QB)
    qs = qkv_ref[0, pl.ds(r0 + FRONT, QB), :]
    k2 = qkv_ref[1, pl.ds(r0 + (FRONT - PAD), KW), :]
    v2 = qkv_ref[2, pl.ds(r0 + (FRONT - PAD), KW), :]
    kvalid = (lax.broadcasted_iota(jnp.int32, (1, KW), 1) >= PAD - blk * QB) if masked else None
    return r0, qs, k2, v2, kvalid


def _head_mask(h):
    lane = lax.broadcasted_iota(jnp.int32, (1, 2 * HEAD_DIM), 1)
    return (lane < HEAD_DIM) if h == 0 else (lane >= HEAD_DIM)


def _stack_heads(a):
    zero = jnp.zeros_like(a)
    return jnp.concatenate([jnp.where(_head_mask(0), a, zero), jnp.where(_head_mask(1), a, zero)], axis=0)


def _unstack_heads(a):
    return jnp.where(_head_mask(0), a[:QB], a[QB:])


def _attn_exp(qst, k2, w_ref, kvalid):
    s = _dot(qst, k2, NT) + w_ref[...].reshape(2 * QB, KW)
    if kvalid is not None:
        s = jnp.where(kvalid, s, NEG)
    e = jnp.exp2(s - jnp.max(s, axis=-1, keepdims=True))
    return e, 1.0 / jnp.sum(e, axis=-1, keepdims=True)


ATTN_G = 4
ATTN_STEP = QB * ATTN_G


def _masked_and_not(b, fn):
    n_masked = -(-PAD // ATTN_STEP)
    pl.when(b < n_masked)(functools.partial(fn, True))
    pl.when(b >= n_masked)(functools.partial(fn, False))


def attn_fwd(qkvp, wb, name, comm=None):
    T = qkvp.shape[1] - FRONT

    def body(qkv_ref, w_ref, o_ref):
        b = pl.program_id(1)

        def blocks(masked):
            for t in range(ATTN_G):
                _, qs, k2, v2, kvalid = _attn_block(qkv_ref, b * ATTN_G + t, masked)
                e, inv = _attn_exp(_stack_heads(qs), k2, w_ref, kvalid)
                o_ref[t * QB:(t + 1) * QB, :] = _unstack_heads(_dot(e.astype(BF16), v2, NN) * inv).astype(BF16)

        _masked_and_not(b, blocks)

    return _call(
        body, comm, 1, (qkvp, wb), name=name, grid=(N_HEADS // 2, T // (QB * ATTN_G)),
        in_specs=[pl.BlockSpec((3, FRONT + T, 2 * HEAD_DIM), lambda hp, b: (0, 0, hp)),
                  pl.BlockSpec((2, QB, KW), lambda hp, b: (hp, 0, 0))],
        out_specs=pl.BlockSpec((QB * ATTN_G, 2 * HEAD_DIM), lambda hp, b: (b, hp)),
        out_shape=_sds((T, D), BF16),
        compiler_params=_params(("parallel", "arbitrary"), VMEM_BIG))


def attn_bwd(qkvp, o, do, wb, name, comm=None):
    T = qkvp.shape[1] - FRONT
    nb = T // (QB * ATTN_G)

    def body(qkv_ref, o_ref, do_ref, w_ref, dqkv_ref, dw_ref, dk_acc, dv_acc):
        b = pl.program_id(1)

        @pl.when(b == 0)
        def _():
            dk_acc[...] = jnp.zeros_like(dk_acc)
            dv_acc[...] = jnp.zeros_like(dv_acc)
            dw_ref[...] = jnp.zeros_like(dw_ref)
            dqkv_ref[0, 0:FRONT, :] = jnp.zeros((FRONT, 2 * HEAD_DIM), BF16)

        def blocks(masked):
            dws = None
            for t in range(ATTN_G):
                r0, qs, k2, v2, kvalid = _attn_block(qkv_ref, b * ATTN_G + t, masked)
                qst = _stack_heads(qs)
                e, inv = _attn_exp(qst, k2, w_ref, kvalid)
                do2 = do_ref[t * QB:(t + 1) * QB, :]
                dost = _stack_heads(do2)
                prod = _stack_heads(do2.astype(F32) * o_ref[t * QB:(t + 1) * QB, :].astype(F32))
                delta = jnp.sum(prod, axis=-1, keepdims=True)
                ds = e * ((_dot(dost, v2, NT) - delta) * inv)
                dws = ds if dws is None else dws + ds
                ds16 = ds.astype(BF16)
                dq = _unstack_heads(_dot(ds16, k2, NN)) * SCALE
                dqkv_ref[0, pl.ds(r0 + FRONT, QB), :] = dq.astype(BF16)
                dk_acc[pl.ds(r0 + (FRONT - PAD), KW), :] += _dot(ds16, qst, TN)
                dv_acc[pl.ds(r0 + (FRONT - PAD), KW), :] += _dot(
                    e.astype(BF16), (dost.astype(F32) * inv).astype(BF16), TN)
            dw_ref[...] += dws.reshape(2, QB, KW)

        _masked_and_not(b, blocks)

        @pl.when(b == nb - 1)
        def _():
            dqkv_ref[1] = (dk_acc[...] * (1.0 / LOG2E)).astype(BF16)
            dqkv_ref[2] = dv_acc[...].astype(BF16)

    slab = pl.BlockSpec((3, FRONT + T, 2 * HEAD_DIM), lambda hp, b: (0, 0, hp))
    wspec = pl.BlockSpec((2, QB, KW), lambda hp, b: (hp, 0, 0))
    rows = pl.BlockSpec((QB * ATTN_G, 2 * HEAD_DIM), lambda hp, b: (b, hp))
    return _call(
        body, comm, 2, (qkvp, o, do, wb), name=name, grid=(N_HEADS // 2, nb),
        in_specs=[slab, rows, rows, wspec],
        out_specs=[slab, wspec],
        out_shape=[_sds((3, FRONT + T, D), BF16), _sds((N_HEADS, QB, KW), F32)],
        scratch_shapes=[pltpu.VMEM((FRONT + T, 2 * HEAD_DIM), F32), pltpu.VMEM((FRONT + T, 2 * HEAD_DIM), F32)],
        compiler_params=_params(("parallel", "arbitrary"), VMEM_BIG))


def proj_qkv(hn, w, l, name, tm=512, comm=None):
    T = hn.shape[0]
    pb = FRONT // tm

    def body(a_ref, b_ref, o_ref):
        i = pl.program_id(1)

        @pl.when(i < pb)
        def _():
            o_ref[...] = jnp.zeros_like(o_ref)

        @pl.when(i >= pb)
        def _():
            scale = jnp.where(pl.program_id(0) == 0, Q_SCALE, 1.0).astype(F32)
            o_ref[...] = (_dot(a_ref[...], b_ref[...], NN) * scale).astype(BF16)

    return _call(
        body, comm, 1, (hn, w), name=name, grid=(3, pb + T // tm),
        in_specs=[pl.BlockSpec((tm, D), lambda p, i: (jnp.maximum(i - pb, 0), 0)),
                  pl.BlockSpec((None, D, D), lambda p, i: (l, 0, p))],
        out_specs=pl.BlockSpec((None, tm, D), lambda p, i: (p, i, 0)),
        out_shape=_sds((3, FRONT + T, D), BF16),
        compiler_params=_params(("parallel", "parallel"), VMEM_BIG))


def ffn_up(hn, wg, wu, l, name, tm=1024, comm=None):
    T = hn.shape[0]

    def body(a_ref, wg_ref, wu_ref, g_ref, u_ref, h_ref):
        a = a_ref[...]
        g = _dot(a, wg_ref[...], NT)
        u = _dot(a, wu_ref[...], NT)
        s = _sigmoid(g)
        silu = g * s
        g_ref[...] = (u * (s * (1.0 + g * (1.0 - s)))).astype(BF16)
        u_ref[...] = silu.astype(BF16)
        h_ref[...] = (silu * u).astype(BF16)

    wspec = pl.BlockSpec((None, None, FS, D), lambda s, i: (l, s, 0, 0))
    ospec = pl.BlockSpec((None, tm, FS), lambda s, i: (s, i, 0))
    return _call(
        body, comm, 3, (hn, wg, wu), name=name, grid=(N_CHIPS, T // tm),
        in_specs=[pl.BlockSpec((tm, D), lambda s, i: (i, 0)), wspec, wspec],
        out_specs=[ospec, ospec, ospec],
        out_shape=[_sds((N_CHIPS, T, FS), BF16)] * 3,
        compiler_params=_params(("parallel", "parallel"), VMEM_BIG))


def ffn_bwd_dh(dxb, wd, g, u, l, name, tm=2048, comm=None):
    T = dxb.shape[0]
    tm = min(tm, T)

    def body(a_ref, wd_ref, g_ref, u_ref, dg_ref, du_ref):
        dh = _dot(a_ref[...], wd_ref[...], NT)
        dg_ref[...] = (dh * g_ref[...].astype(F32)).astype(BF16)
        du_ref[...] = (dh * u_ref[...].astype(F32)).astype(BF16)

    aspec = pl.BlockSpec((None, tm, FS), lambda i, s: (s, i, 0))
    return _call(
        body, comm, 2, (dxb, wd, g, u), name=name, grid=(T // tm, N_CHIPS),
        in_specs=[pl.BlockSpec((tm, D), lambda i, s: (i, 0)),
                  pl.BlockSpec((None, None, FS, D), lambda i, s: (l, s, 0, 0)), aspec, aspec],
        out_specs=[aspec, aspec],
        out_shape=[_sds((N_CHIPS, T, FS), BF16)] * 2,
        compiler_params=_params(("parallel", "parallel"), VMEM_BIG))


def ffn_dgrad(dg, du, wg, wu, tm=512):
    def compute(dg_ref, du_ref, wg_ref, wu_ref):
        d = None
        for s in range(N_CHIPS):
            t = _dot(dg_ref[s], wg_ref[s], NN) + _dot(du_ref[s], wu_ref[s], NN)
            d = t if d is None else d + t
        return d

    aspec = pl.BlockSpec((N_CHIPS, tm, FS), lambda i: (0, i, 0))
    wspec = pl.BlockSpec((None, N_CHIPS, FS, D), lambda i: (0, 0, 0, 0), pipeline_mode=pl.Buffered(1))
    return compute, (dg, du, wg, wu), [aspec, aspec, wspec, wspec]


def qkv_dgrad(dqkvp, w, tm=512):
    def compute(a_ref, w_ref):
        d = None
        for p in range(3):
            t = _dot(a_ref[p], w_ref[:, p * D:(p + 1) * D], NT)
            d = t if d is None else d + t
        return d

    return compute, (dqkvp, w), [pl.BlockSpec((3, tm, D), lambda i: (0, i + FRONT // tm, 0)),
                                 pl.BlockSpec((None, D, 3 * D), lambda i: (0, 0, 0))]


def in_dgrad(dpre, w, tm=512):
    def compute(a_ref, w_ref):
        return _dot(a_ref[...], w_ref[...], NT)

    return compute, (dpre, w), [pl.BlockSpec((tm, 2 * GH), lambda i: (i, 0)),
                                pl.BlockSpec((None, D, 2 * GH), lambda i: (0, 0, 0))]


def _rms_rows(x, g):
    r = lax.rsqrt(jnp.mean(x * x, axis=-1, keepdims=True) + EPS)
    return ((x * r) * g).astype(BF16)


def residual_proj(name, compute, args, specs, res, norm_g, tm=512, comm=None):
    T = res.shape[0]
    k = len(args)
    with_norm = norm_g is not None

    def body(*refs):
        d = refs[k][...] + compute(*refs[:k])
        if with_norm:
            refs[k + 2][...] = d
            refs[k + 3][...] = _rms_rows(d, refs[k + 1][...])
        else:
            refs[k + 1][...] = d

    row = pl.BlockSpec((tm, D), lambda i: (i, 0))
    vec = pl.BlockSpec((1, D), lambda i: (0, 0))
    if with_norm:
        return _call(body, comm, 2, (*args, res, norm_g), name=name, grid=(T // tm,),
                     in_specs=list(specs) + [row, vec], out_specs=[row, row],
                     out_shape=[_sds((T, D), F32), _sds((T, D), BF16)],
                     compiler_params=_params(("parallel",), VMEM_BIG))
    return _call(body, comm, 1, (*args, res), name=name, grid=(T // tm,), in_specs=list(specs) + [row],
                 out_specs=row, out_shape=_sds((T, D), F32), compiler_params=_params(("parallel",), VMEM_BIG))


def ffn_down(h, wd, tm=512):
    def compute(h_ref, wd_ref):
        d = None
        for s in range(N_CHIPS):
            t = _dot(h_ref[s], wd_ref[s], NN)
            d = t if d is None else d + t
        return d

    return compute, (h, wd), [pl.BlockSpec((N_CHIPS, tm, FS), lambda i: (0, i, 0)),
                              pl.BlockSpec((None, N_CHIPS, FS, D), lambda i: (0, 0, 0, 0))]


def out_proj(a, w, tm=512):
    K = a.shape[1]

    def compute(a_ref, w_ref):
        return _dot(a_ref[...], w_ref[...], NN)

    return compute, (a, w), [pl.BlockSpec((tm, K), lambda i: (i, 0)), pl.BlockSpec((None, K, D), lambda i: (0, 0, 0))]


def adamw(w, g, m, v, name):
    L, R, C = w.shape

    def body(w_ref, g_ref, m_ref, v_ref, d_ref, nm_ref, nv_ref):
        gf = g_ref[...]
        nm = ADAM_B1 * m_ref[...] + (1.0 - ADAM_B1) * gf
        nv = ADAM_B2 * v_ref[...] + (1.0 - ADAM_B2) * (gf * gf)
        m_hat = nm / (1.0 - ADAM_B1 ** ADAM_STEP)
        v_hat = nv / (1.0 - ADAM_B2 ** ADAM_STEP)
        d_ref[...] = -ADAM_LR * (m_hat / (jnp.sqrt(v_hat) + ADAM_EPS) + ADAM_WD * w_ref[...])
        nm_ref[...] = nm
        nv_ref[...] = nv

    tr = R // 4 if R % 32 == 0 else R
    spec = pl.BlockSpec((None, tr, C), lambda l, r: (l, r, 0))
    return _pallas(body, name=name, grid=(L, R // tr), in_specs=[spec] * 4, out_specs=[spec] * 3,
                   out_shape=[_sds((L, R, C), F32)] * 3,
                   compiler_params=_params(("parallel", "parallel")))(w, g, m, v)


def _coords():
    return lax.axis_index("x"), lax.axis_index("y"), lax.axis_index("c")


def _other_chips(x, y):
    out = []
    for fx, fy in ((1, 0), (0, 1), (1, 1)):
        px = (1 - x) if fx else x
        py = (1 - y) if fy else y
        out.append((px, py))
    return out


def _flip_index(s, j):
    sx, sy = s // 2, s % 2
    fx, fy = ((1, 0), (0, 1), (1, 1))[j]
    return 2 * (sx ^ fx) + (sy ^ fy)


def _for_my_chip(sme, fn):
    for s in range(N_CHIPS):
        pl.when(sme == s)(functools.partial(fn, s))


ANY = pl.BlockSpec(memory_space=pl.ANY)

GATHER_KIND = {"a_w_in": "col", "b_w_qkv": "col", "a_w_out": "row", "b_w_out": "row",
               "ffn_w_gate": "row", "ffn_w_up": "row", "ffn_w_down": "row"}
BIG = tuple(GATHER_KIND)


def _gathered_shape(kind, shape):
    L, R, C = shape
    return (L, R, N_CHIPS * C) if kind == "col" else (L, N_CHIPS, R, C)


def _shard_rows(ref, kind, s, r0, rn, C):
    if kind == "col":
        return ref.at[:, pl.ds(r0, rn), s * C:(s + 1) * C]
    return ref.at[:, s, pl.ds(r0, rn), :]


def gather_stage1(items):
    n = len(items)
    dims = [it[0].shape[1:] for it in items]

    def copies(ins, outs, sems, s, with_landed=True):
        lsem, ssem, rsem = sems
        x, y, c = _coords()
        chips = _other_chips(x, y)
        local, send, landed = [], [], []
        for t, (_, li, kind) in enumerate(items):
            R, C = dims[t]
            r0 = pl.multiple_of(c * (R // 2), 8)
            local.append(pltpu.make_async_copy(ins[t].at[pl.ds(li, 1)], _shard_rows(outs[t], kind, s, 0, R, C),
                                               lsem.at[t]))
            for j in range(3):
                pair = dict(send_sem=ssem.at[3 * t + j], recv_sem=rsem.at[3 * t + j],
                            device_id=(chips[j][0], chips[j][1], c), device_id_type=MESH)
                send.append(pltpu.make_async_remote_copy(
                    src_ref=ins[t].at[pl.ds(li, 1), pl.ds(r0, R // 2), :],
                    dst_ref=_shard_rows(outs[t], kind, s, r0, R // 2, C), **pair))
                if with_landed:
                    got = _shard_rows(outs[t], kind, _flip_index(s, j), r0, R // 2, C)
                    landed.append(pltpu.make_async_remote_copy(src_ref=got, dst_ref=got, **pair))
        return local, send, landed

    def start(ins, outs, sems):
        def run(s):
            local, send, _ = copies(ins, outs, sems, s, with_landed=False)
            for cp in local + send:
                cp.start()
        x, y, _ = _coords()
        _for_my_chip(2 * x + y, run)

    def wait(ins, outs, sems):
        def run(s):
            local, send, landed = copies(ins, outs, sems, s)
            for cp in landed:
                cp.wait_recv()
            for cp in send:
                cp.wait_send()
            for cp in local:
                cp.wait()
        x, y, _ = _coords()
        _for_my_chip(2 * x + y, run)

    out_shapes = [_sds(_gathered_shape(kind, (1,) + tuple(dims[t])), BF16) for t, (_, _, kind) in enumerate(items)]
    sems = [pltpu.SemaphoreType.DMA((n,)), pltpu.SemaphoreType.DMA((3 * n,)), pltpu.SemaphoreType.DMA((3 * n,))]
    return Comm([it[0] for it in items], out_shapes, sems, start, wait)


def gather_stage2(items, gathered):
    n = len(items)
    dims = [it[0].shape[1:] for it in items]

    def copies(outs, sems, s, with_landed=True):
        ssem, rsem = sems
        x, y, c = _coords()
        send, landed = [], []
        for t, (_, _, kind) in enumerate(items):
            R, C = dims[t]
            for j in range(3):
                pair = dict(send_sem=ssem.at[3 * t + j], recv_sem=rsem.at[3 * t + j],
                            device_id=(x, y, 1 - c), device_id_type=MESH)
                mine = _shard_rows(outs[t], kind, _flip_index(s, j), pl.multiple_of(c * (R // 2), 8), R // 2, C)
                send.append(pltpu.make_async_remote_copy(src_ref=mine, dst_ref=mine, **pair))
                if with_landed:
                    other = _shard_rows(outs[t], kind, _flip_index(s, j), pl.multiple_of((1 - c) * (R // 2), 8),
                                        R // 2, C)
                    landed.append(pltpu.make_async_remote_copy(src_ref=other, dst_ref=other, **pair))
        return send, landed

    def start(ins, outs, sems):
        def run(s):
            for cp in copies(outs, sems, s, with_landed=False)[0]:
                cp.start()
        x, y, _ = _coords()
        _for_my_chip(2 * x + y, run)

    def wait(ins, outs, sems):
        def run(s):
            send, landed = copies(outs, sems, s)
            for cp in landed:
                cp.wait_recv()
            for cp in send:
                cp.wait_send()
        x, y, _ = _coords()
        _for_my_chip(2 * x + y, run)

    out_shapes = [_sds(g.shape, BF16) for g in gathered]
    sems = [pltpu.SemaphoreType.DMA((3 * n,)), pltpu.SemaphoreType.DMA((3 * n,))]
    return Comm(gathered, out_shapes, sems, start, wait, aliases={t: t for t in range(n)})


def gather_both(items):
    s1 = gather_stage1(items)
    s2 = gather_stage2(items, s1.out_shapes)
    n1 = len(s1.sems)

    def wait(ins, outs, sems):
        s1.wait(ins, outs, sems[:n1])
        s2.start((), outs, sems[n1:])
        s2.wait((), outs, sems[n1:])

    return Comm(s1.ins, s1.out_shapes, s1.sems + s2.sems, lambda ins, outs, sems: s1.start(ins, outs, sems[:n1]), wait)


def _half_shape(kind, R, C):
    return (R // 2, N_CHIPS * C) if kind == "col" else (N_CHIPS, R // 2, C)


def exchange_halves(grads, metas):
    n = len(grads)

    def copies(ins, outs, sems):
        ssem, rsem = sems
        x, y, c = _coords()
        out = []
        for t, (kind, R, C) in enumerate(metas):
            r0 = pl.multiple_of((1 - c) * (R // 2), 8)
            src = ins[t].at[pl.ds(r0, R // 2), :] if kind == "col" else ins[t].at[:, pl.ds(r0, R // 2), :]
            out.append(pltpu.make_async_remote_copy(
                src_ref=src, dst_ref=outs[t], send_sem=ssem.at[t], recv_sem=rsem.at[t],
                device_id=(x, y, 1 - c), device_id_type=MESH))
        return out

    def start(ins, outs, sems):
        for cp in copies(ins, outs, sems):
            cp.start()

    def wait(ins, outs, sems):
        for cp in copies(ins, outs, sems):
            cp.wait()

    return Comm(grads, [_sds(_half_shape(*m), F32) for m in metas], [pltpu.SemaphoreType.DMA((n,))] * 2, start, wait)


def pair_sum(me, g, sib, meta, name):
    kind, R, C = meta
    h = R // 2

    def body(me_ref, g_ref, sib_ref, p16_ref, own_ref):
        s = pl.program_id(0)
        v = g_ref[...] + sib_ref[...]
        p16_ref[...] = v.astype(BF16)

        @pl.when(s == me_ref[1])
        def _():
            own_ref[...] = v

    if kind == "col":
        gspec = pl.BlockSpec((h, C), lambda s, me_ref: (me_ref[0], s))
        sspec = pl.BlockSpec((h, C), lambda s, me_ref: (0, s))
    else:
        gspec = pl.BlockSpec((None, h, C), lambda s, me_ref: (s, me_ref[0], 0))
        sspec = pl.BlockSpec((None, h, C), lambda s, me_ref: (s, 0, 0))
    grid_spec = pltpu.PrefetchScalarGridSpec(
        num_scalar_prefetch=1, grid=(N_CHIPS,), in_specs=[gspec, sspec],
        out_specs=[sspec, pl.BlockSpec((h, C), lambda s, me_ref: (0, 0))])
    return _pallas(body, name=name, grid_spec=grid_spec,
                   out_shape=[_sds(_half_shape(*meta), BF16), _sds((h, C), F32)],
                   compiler_params=_params(("arbitrary",), VMEM_BIG))(me, g, sib)


def scatter_partials(p16s, metas):
    n = len(p16s)

    def copies(ins, outs, sems, s):
        ssem, rsem = sems
        x, y, c = _coords()
        chips = _other_chips(x, y)
        out = []
        for t, (kind, R, C) in enumerate(metas):
            for j in range(3):
                sj = _flip_index(s, j)
                src = ins[t].at[:, sj * C:(sj + 1) * C] if kind == "col" else ins[t].at[sj]
                out.append(pltpu.make_async_remote_copy(
                    src_ref=src, dst_ref=outs[t].at[j], send_sem=ssem.at[3 * t + j], recv_sem=rsem.at[3 * t + j],
                    device_id=(chips[j][0], chips[j][1], c), device_id_type=MESH))
        return out

    def start(ins, outs, sems):
        def run(s):
            for cp in copies(ins, outs, sems, s):
                cp.start()
        x, y, _ = _coords()
        _for_my_chip(2 * x + y, run)

    def wait(ins, outs, sems):
        def run(s):
            for cp in copies(ins, outs, sems, s):
                cp.wait()
        x, y, _ = _coords()
        _for_my_chip(2 * x + y, run)

    return Comm(p16s, [_sds((3, R // 2, C), BF16) for (_, R, C) in metas],
                [pltpu.SemaphoreType.DMA((3 * n,))] * 2, start, wait)


def final_sum(me, own, q, buf, l, meta, name):
    _, R, C = meta
    h = R // 2

    def body(me_ref, own_ref, q_ref, buf_ref, o_ref):
        del buf_ref
        o_ref[...] = ((own_ref[...] + q_ref[0].astype(F32)) + q_ref[1].astype(F32)) + q_ref[2].astype(F32)

    grid_spec = pltpu.PrefetchScalarGridSpec(
        num_scalar_prefetch=1, grid=(1,),
        in_specs=[pl.BlockSpec((h, C), lambda i, me_ref: (0, 0)),
                  pl.BlockSpec((3, h, C), lambda i, me_ref: (0, 0, 0)), ANY],
        out_specs=pl.BlockSpec((None, h, C), lambda i, me_ref: (l, me_ref[0], 0)))
    return _pallas(body, name=name, grid_spec=grid_spec, out_shape=_sds(buf.shape, F32),
                   input_output_aliases={3: 0},
                   compiler_params=_params(("arbitrary",), VMEM_BIG))(me, own, q, buf)


def share_final(bufs):
    n = len(bufs)

    def body(*refs):
        ins, outs = refs[:n], refs[n:2 * n]
        ssem, rsem = refs[2 * n:]
        del ins
        x, y, c = _coords()
        copies = []
        for t in range(n):
            R = bufs[t].shape[1]
            r0 = pl.multiple_of(c * (R // 2), 8)
            blk = outs[t].at[:, pl.ds(r0, R // 2), :]
            copies.append(pltpu.make_async_remote_copy(
                src_ref=blk, dst_ref=blk, send_sem=ssem.at[t], recv_sem=rsem.at[t],
                device_id=(x, y, 1 - c), device_id_type=MESH))
        for cp in copies:
            cp.start()
        for t in range(n):
            R = bufs[t].shape[1]
            r1 = pl.multiple_of((1 - c) * (R // 2), 8)
            other = outs[t].at[:, pl.ds(r1, R // 2), :]
            pltpu.make_async_remote_copy(
                src_ref=other, dst_ref=other, send_sem=ssem.at[t], recv_sem=rsem.at[t],
                device_id=(x, y, 1 - c), device_id_type=MESH).wait_recv()
        for cp in copies:
            cp.wait_send()

    out_shape = [_sds(b.shape, F32) for b in bufs]
    return _pallas(body, name="share_final", in_specs=[ANY] * n, out_specs=[ANY] * n, out_shape=out_shape,
                   input_output_aliases={t: t for t in range(n)},
                   scratch_shapes=[pltpu.SemaphoreType.DMA((n,))] * 2,
                   compiler_params=pltpu.CompilerParams(has_side_effects=True))(*bufs)


def allreduce_small(part):
    rows = part.shape[0]
    h = rows // 2

    def body(p_ref, o_ref, sib_buf, pair_buf, chip_buf, ssem, rsem):
        x, y, c = _coords()
        sibling = dict(device_id=(x, y, 1 - c), device_id_type=MESH)
        mine = pl.ds(pl.multiple_of(c * h, 8), h)
        theirs = pl.ds(pl.multiple_of((1 - c) * h, 8), h)

        swap = pltpu.make_async_remote_copy(src_ref=p_ref.at[theirs], dst_ref=sib_buf, send_sem=ssem.at[0],
                                            recv_sem=rsem.at[0], **sibling)
        swap.start()
        swap.wait()
        pair_buf[...] = p_ref[mine, :] + sib_buf[...]

        chips = _other_chips(x, y)
        sends = [pltpu.make_async_remote_copy(src_ref=pair_buf, dst_ref=chip_buf.at[j], send_sem=ssem.at[1 + j],
                                              recv_sem=rsem.at[1 + j], device_id=(chips[j][0], chips[j][1], c),
                                              device_id_type=MESH) for j in range(3)]
        for cp in sends:
            cp.start()
        for cp in sends:
            cp.wait()

        def total(s):
            terms = {s: pair_buf[...]}
            for j in range(3):
                terms[_flip_index(s, j)] = chip_buf[j]
            o_ref[mine, :] = ((terms[0] + terms[1]) + terms[2]) + terms[3]

        _for_my_chip(2 * x + y, total)

        back = pltpu.make_async_remote_copy(src_ref=o_ref.at[mine], dst_ref=o_ref.at[mine], send_sem=ssem.at[4],
                                            recv_sem=rsem.at[4], **sibling)
        back.start()
        pltpu.make_async_remote_copy(src_ref=o_ref.at[theirs], dst_ref=o_ref.at[theirs], send_sem=ssem.at[4],
                                     recv_sem=rsem.at[4], **sibling).wait_recv()
        back.wait_send()

    return _pallas(body, name="allreduce_small",
                   in_specs=[pl.BlockSpec(memory_space=pltpu.VMEM)], out_specs=pl.BlockSpec(memory_space=pltpu.VMEM),
                   out_shape=_sds((rows, 128), F32),
                   scratch_shapes=[pltpu.VMEM((h, 128), F32), pltpu.VMEM((h, 128), F32), pltpu.VMEM((3, h, 128), F32),
                                   pltpu.SemaphoreType.DMA((5,)), pltpu.SemaphoreType.DMA((5,))],
                   compiler_params=pltpu.CompilerParams(has_side_effects=True))(part)


def _rows128(a):
    flat = a.reshape(-1)
    rows = -(-flat.shape[0] // 128)
    rows8 = -(-rows // 8) * 8
    flat = jnp.pad(flat, (0, rows8 * 128 - flat.shape[0]))
    return flat.reshape(rows8, 128)


def kernel(x, norm_mix_g, norm_ffn_g, final_g, a_w_in, a_v_gain, a_w_s, a_b_s, a_w_out, b_w_qkv, b_rel_bias, b_w_out, ffn_w_gate, ffn_w_up, ffn_w_down, loss_target, m_norm_mix_g, m_norm_ffn_g, m_final_g, m_a_w_in, m_a_v_gain, m_a_w_s, m_a_b_s, m_a_w_out, m_b_w_qkv, m_b_rel_bias, m_b_w_out, m_ffn_w_gate, m_ffn_w_up, m_ffn_w_down, v_norm_mix_g, v_norm_ffn_g, v_final_g, v_a_w_in, v_a_v_gain, v_a_w_s, v_a_b_s, v_a_w_out, v_b_w_qkv, v_b_rel_bias, v_b_w_out, v_ffn_w_gate, v_ffn_w_up, v_ffn_w_down):
    T = x.shape[1]
    weights = dict(norm_mix_g=norm_mix_g, norm_ffn_g=norm_ffn_g, final_g=final_g, a_w_in=a_w_in, a_v_gain=a_v_gain,
                   a_w_s=a_w_s, a_b_s=a_b_s, a_w_out=a_w_out, b_w_qkv=b_w_qkv, b_rel_bias=b_rel_bias,
                   b_w_out=b_w_out, ffn_w_gate=ffn_w_gate, ffn_w_up=ffn_w_up, ffn_w_down=ffn_w_down)
    mom_m = dict(norm_mix_g=m_norm_mix_g, norm_ffn_g=m_norm_ffn_g, final_g=m_final_g, a_w_in=m_a_w_in,
                 a_v_gain=m_a_v_gain, a_w_s=m_a_w_s, a_b_s=m_a_b_s, a_w_out=m_a_w_out, b_w_qkv=m_b_w_qkv,
                 b_rel_bias=m_b_rel_bias, b_w_out=m_b_w_out, ffn_w_gate=m_ffn_w_gate, ffn_w_up=m_ffn_w_up,
                 ffn_w_down=m_ffn_w_down)
    mom_v = dict(norm_mix_g=v_norm_mix_g, norm_ffn_g=v_norm_ffn_g, final_g=v_final_g, a_w_in=v_a_w_in,
                 a_v_gain=v_a_v_gain, a_w_s=v_a_w_s, a_b_s=v_a_b_s, a_w_out=v_a_w_out, b_w_qkv=v_b_w_qkv,
                 b_rel_bias=v_b_rel_bias, b_w_out=v_b_w_out, ffn_w_gate=v_ffn_w_gate, ffn_w_up=v_ffn_w_up,
                 ffn_w_down=v_ffn_w_down)
    order = list(weights)
    transposed = ("ffn_w_gate", "ffn_w_up")
    for k in transposed:
        weights[k], mom_m[k], mom_v[k] = (jnp.swapaxes(a, 1, 2) for a in (weights[k], mom_m[k], mom_v[k]))

    xi, yi, ci = _coords()
    me = jnp.stack([ci, 2 * xi + yi]).astype(jnp.int32)

    shard16 = {k: cast_bf16(weights[k], "cast_" + k) for k in BIG}

    def layer_tensors(i):
        mix = ("a_w_in", "a_w_out") if i % 2 == 0 else ("b_w_qkv", "b_w_out")
        return [(k, i // 2) for k in mix] + [(k, i) for k in ("ffn_w_gate", "ffn_w_up", "ffn_w_down")]

    def gather_items(keys):
        return [(shard16[k], l, GATHER_KIND[k]) for k, l in keys]

    def grad_metas(keys):
        return [(GATHER_KIND[k],) + tuple(weights[k].shape[1:]) for k, _ in keys]

    FFN = ("ffn_w_gate", "ffn_w_up", "ffn_w_down")
    k0a = [("a_w_out", 0), ("ffn_w_gate", 0)]
    k0b = [("ffn_w_up", 0), ("ffn_w_down", 0)]
    k1a = [("b_w_qkv", 0), ("b_w_out", 0), ("ffn_w_gate", 1)]
    k1b = [("ffn_w_up", 1), ("ffn_w_down", 1)]
    k3a = [("b_w_qkv", 1), ("b_w_out", 1), ("ffn_w_gate", 3)]
    k3b = [("ffn_w_up", 3), ("ffn_w_down", 3)]
    plans = {
        "a_in_l0": [("g1", k0a)], "sgu_fwd_l0": [("g2", k0a), ("g1", k0b)], "a_out_l0": [("g2", k0b)],
        "ffn_up_l0": [("g1", k1a)], "ffn_down_l0": [("g2", k1a), ("g1", k1b)], "b_qkv_l1": [("g2", k1b)],
        "attn_fwd_l1": [("g1", layer_tensors(2))], "b_out_l1": [("g2", layer_tensors(2))],
        "ffn_up_l1": [("g1", k3a)], "ffn_down_l1": [("g2", k3a)],
        "a_in_l2": [("g1", k3b)], "sgu_fwd_l2": [("g2", k3b)],
        "ffn_bwd_dh_l2": [("ex", layer_tensors(3))], "sgu_bwd_l2": [("sc", layer_tensors(3))],
        "ffn_bwd_dh_l1": [("ex", layer_tensors(2))], "attn_bwd_l1": [("sc", layer_tensors(2))],
        "ffn_bwd_dh_l0": [("ex", layer_tensors(1))], "ffn_bwd_dhn_l0": [("sc", k1a)],
        "dffn_w_gate_l0": [("sc", [("ffn_w_up", 1)])], "dffn_w_up_l0": [("sc", [("ffn_w_down", 1)])],
        "a_out_bwd_l0": [("ex", [(k, 0) for k in FFN])],
        "sgu_bwd_l0": [("sc", [("ffn_w_gate", 0), ("ffn_w_up", 0)]), ("ex", [("a_w_out", 0)])],
        "dw_in_l0": [("sc", [("ffn_w_down", 0), ("a_w_out", 0)])],
    }
    part16, full16 = {}, {}
    sib, p16, own_parts, recv_parts = {}, {}, {}, {}

    def make_comm(kind, keys):
        if kind == "g1":
            return gather_stage1(gather_items(keys)), lambda outs: part16.update(zip(keys, outs))
        if kind == "g2":
            return (gather_stage2(gather_items(keys), [part16[kl] for kl in keys]),
                    lambda outs: full16.update(zip(keys, outs)))
        if kind == "ex":
            return (exchange_halves([big_grads[k][l] for k, l in keys], grad_metas(keys)),
                    lambda outs: sib.update(zip(keys, outs)))
        for kl, m_ in zip(keys, grad_metas(keys)):
            p16[kl], own_parts[kl] = pair_sum(me, big_grads[kl[0]][kl[1]], sib[kl], m_, "pair_sum_%s_l%d" % kl)
        return (scatter_partials([p16[kl] for kl in keys], grad_metas(keys)),
                lambda outs: recv_parts.update(zip(keys, outs)))

    def run(name, make):
        steps = plans.get(name)
        if not steps:
            return make(None)
        made = [make_comm(kind, keys) for kind, keys in steps]
        main, outs = make(combine([c for c, _ in made]))
        for c, done in made:
            done(outs[:len(c.out_shapes)])
            outs = outs[len(c.out_shapes):]
        return main

    def weight(k, l):
        w = full16[(k, l)]
        if k == "a_w_out":
            return w.reshape(1, GH, D)
        return w.reshape(1, D, D) if k == "b_w_out" else w

    full16[("a_w_in", 0)] = run_comm(gather_both(gather_items([("a_w_in", 0)])), "gather_first")[0]

    xcur = x.reshape(T, D)
    hn = rms_fwd(xcur, norm_mix_g[0][None], "rms_mix_l0")
    saved = []
    for i in range(DEPTH):
        j = i // 2
        tag = "_l%d" % i
        st = {"x_in": xcur, "hn": hn}
        if i % 2 == 0:
            pre = run("a_in" + tag, lambda comm: matmul(
                "a_in" + tag, NN, hn, pl.BlockSpec((1024, D), lambda i_, j_: (i_, 0)),
                weight("a_w_in", j), pl.BlockSpec((None, D, 1024), lambda i_, j_: (0, 0, j_)),
                _sds((T, 2 * GH), BF16), pl.BlockSpec((1024, 1024), lambda i_, j_: (i_, j_)),
                (T // 1024, 4), comm=comm))
            y = run("sgu_fwd" + tag, lambda comm: sgu_fwd(
                pre, a_v_gain[j][None], a_w_s[j], a_b_s[j][:, :, None], "sgu_fwd" + tag, comm=comm))
            xmid, hn2 = run("a_out" + tag, lambda comm: residual_proj(
                "a_out" + tag, *out_proj(y, weight("a_w_out", j)), xcur, norm_ffn_g[i][None], comm=comm))
            st.update(pre=pre, y=y)
        else:
            qkvp = run("b_qkv" + tag, lambda comm: proj_qkv(hn, weight("b_w_qkv", j), 0, "b_qkv" + tag, comm=comm))
            wb = jnp.transpose(bias_build(b_rel_bias[j], "bias_build" + tag), (1, 0, 2))
            o = run("attn_fwd" + tag, lambda comm: attn_fwd(qkvp, wb, "attn_fwd" + tag, comm=comm))
            xmid, hn2 = run("b_out" + tag, lambda comm: residual_proj(
                "b_out" + tag, *out_proj(o, weight("b_w_out", j)), xcur, norm_ffn_g[i][None], comm=comm))
            st.update(qkvp=qkvp, wb=wb, o=o)
        g, u, h = run("ffn_up" + tag, lambda comm: ffn_up(
            hn2, weight("ffn_w_gate", i), weight("ffn_w_up", i), 0, "ffn_up" + tag, comm=comm))
        next_g = norm_mix_g[i + 1][None] if i + 1 < DEPTH else None
        down = run("ffn_down" + tag, lambda comm: residual_proj(
            "ffn_down" + tag, *ffn_down(h, weight("ffn_w_down", i)), xmid, next_g, comm=comm))
        xcur, hn = down if next_g is not None else (down, None)
        st.update(x_mid=xmid, hn2=hn2, g=g, u=u, h=h)
        saved.append(st)

    loss_part, dx, dxb, d_final = final_loss(xcur, final_g[None], loss_target.reshape(T, D), "final_loss")

    tk = min(2048, T)
    big_grads = {k: [None] * weights[k].shape[0] for k in BIG}
    small = {"norm_mix_g": [None] * DEPTH, "norm_ffn_g": [None] * DEPTH, "a_v_gain": [None] * 2,
             "a_w_s": [None] * 2, "a_b_s": [None] * 2, "b_rel_bias": [None] * 2}
    tok = lambda width: pl.BlockSpec((tk, width), lambda j_, k_: (k_, 0))
    part = lambda: pl.BlockSpec((None, tk, FS), lambda j_, k_: (j_, k_, 0))
    for i in reversed(range(DEPTH)):
        j = i // 2
        tag = "_l%d" % i
        st = saved[i]
        dg, du = run("ffn_bwd_dh" + tag, lambda comm: ffn_bwd_dh(
            dxb, weight("ffn_w_down", i), st["g"], st["u"], 0, "ffn_bwd_dh" + tag, comm=comm))
        big_grads["ffn_w_down"][i] = wgrad(
            "dw_down" + tag, st["h"], part(), dxb, tok(D), _sds((N_CHIPS, FS, D), F32),
            pl.BlockSpec((None, FS, D), lambda j_, k_: (j_, 0, 0)), N_CHIPS, T, tk)
        dx_mid, dxb_mid, dgn = run("ffn_bwd_dhn" + tag, lambda comm: dgrad_rms(
            "ffn_bwd_dhn" + tag, *ffn_dgrad(dg, du, weight("ffn_w_gate", i), weight("ffn_w_up", i)),
            st["x_mid"], norm_ffn_g[i][None], dx, comm=comm))
        for nm, dz in (("ffn_w_gate", dg), ("ffn_w_up", du)):
            big_grads[nm][i] = run("d" + nm + tag, lambda comm: wgrad(
                "d" + nm + tag, dz, part(), st["hn2"], tok(D), _sds((N_CHIPS, FS, D), F32),
                pl.BlockSpec((None, FS, D), lambda j_, k_: (j_, 0, 0)), N_CHIPS, T, tk, comm=comm))
        dx, dxb = dx_mid, dxb_mid
        small["norm_ffn_g"][i] = dgn
        if i % 2 == 0:
            dy = run("a_out_bwd" + tag, lambda comm: matmul(
                "a_out_bwd" + tag, NT, dxb, pl.BlockSpec((1024, D), lambda i_, j_: (i_, 0)),
                weight("a_w_out", j), pl.BlockSpec((None, 1024, D), lambda i_, j_: (0, j_, 0)),
                _sds((T, GH), BF16), pl.BlockSpec((1024, 1024), lambda i_, j_: (i_, j_)), (T // 1024, 2), comm=comm))
            big_grads["a_w_out"][j] = wgrad(
                "dw_aout" + tag, st["y"], pl.BlockSpec((tk, 1024), lambda j_, k_: (k_, j_)), dxb, tok(D),
                _sds((GH, D), F32), pl.BlockSpec((1024, D), lambda j_, k_: (j_, 0)), 2, T, tk
            ).reshape(N_CHIPS, GH // N_CHIPS, D)
            dpre, d_ws, d_bs, d_gain, dx_in, dxb_in, dgn = run("sgu_bwd" + tag, lambda comm: sgu_in_bwd(
                st["pre"], dy, a_v_gain[j][None], a_w_s[j], a_b_s[j][:, :, None], weight("a_w_in", j),
                st["x_in"], norm_mix_g[i][None], dx, "sgu_bwd" + tag, comm=comm))
            small["a_w_s"][j], small["a_b_s"][j], small["a_v_gain"][j] = d_ws, d_bs, d_gain
            big_grads["a_w_in"][j] = run("dw_in" + tag, lambda comm: wgrad(
                "dw_in" + tag, st["hn"], tok(D), dpre, pl.BlockSpec((tk, 1024), lambda j_, k_: (k_, j_)),
                _sds((D, 2 * GH), F32), pl.BlockSpec((D, 1024), lambda j_, k_: (0, j_)), 4, T, tk, comm=comm))
        else:
            do = matmul("b_out_bwd" + tag, NT, dxb, pl.BlockSpec((1024, D), lambda i_, j_: (i_, 0)),
                        weight("b_w_out", j), pl.BlockSpec((None, D, D), lambda i_, j_: (0, 0, 0)),
                        _sds((T, D), BF16), pl.BlockSpec((1024, D), lambda i_, j_: (i_, 0)), (T // 1024, 1))
            big_grads["b_w_out"][j] = wgrad(
                "dw_bout" + tag, st["o"], tok(D), dxb, tok(D),
                _sds((D, D), F32), pl.BlockSpec((D, D), lambda j_, k_: (0, 0)), 1, T, tk
            ).reshape(N_CHIPS, D // N_CHIPS, D)
            dqkvp, dwb = run("attn_bwd" + tag, lambda comm: attn_bwd(
                st["qkvp"], st["o"], do, st["wb"], "attn_bwd" + tag, comm=comm))
            small["b_rel_bias"][j] = bias_grad(
                jnp.pad(jnp.transpose(dwb, (1, 0, 2)), ((0, 0), (0, 0), (0, DIAG - KW))), "bias_grad" + tag)
            dx_in, dxb_in, dgn = dgrad_rms(
                "b_qkv_bwd" + tag, *qkv_dgrad(dqkvp, weight("b_w_qkv", j)),
                st["x_in"], norm_mix_g[i][None], dx)
            big_grads["b_w_qkv"][j] = wgrad(
                "dw_qkv" + tag, st["hn"], tok(D), dqkvp,
                pl.BlockSpec((None, tk, D), lambda j_, k_: (j_, k_ + FRONT // tk, 0)),
                _sds((D, 3 * D), F32), pl.BlockSpec((D, D), lambda j_, k_: (0, j_)), 3, T, tk)
        dx, dxb = dx_in, dxb_in
        small["norm_mix_g"][i] = dgn

    small_grads = {
        "norm_mix_g": jnp.concatenate(small["norm_mix_g"], axis=0),
        "norm_ffn_g": jnp.concatenate(small["norm_ffn_g"], axis=0),
        "final_g": d_final.reshape(D),
        "a_v_gain": jnp.concatenate(small["a_v_gain"], axis=0),
        "a_w_s": jnp.stack(small["a_w_s"]),
        "a_b_s": jnp.stack(small["a_b_s"]).reshape(2, SGU_G, SGU_BLOCK),
        "b_rel_bias": jnp.stack(small["b_rel_bias"]),
    }
    small_names = list(small_grads)
    packed = [_rows128(small_grads[k]) for k in small_names] + [_rows128(loss_part[:, :1])]
    offs = [0]
    for p in packed:
        offs.append(offs[-1] + p.shape[0])
    reduced = allreduce_small(jnp.concatenate(packed, axis=0))
    grads = {}
    for t, k in enumerate(small_names):
        nelem = small_grads[k].size
        grads[k] = reduced[offs[t]:offs[t + 1]].reshape(-1)[:nelem].reshape(weights[k].shape)
    loss = reduced[offs[len(small_names)], 0]

    last = [("a_w_in", 0)]
    for kind, name in (("ex", "exchange_last"), ("sc", "scatter_last")):
        comm, done = make_comm(kind, last)
        done(run_comm(comm, name))
    bufs = {k: jnp.zeros(weights[k].shape, F32) for k in BIG}
    for i in range(DEPTH):
        for kl, m_ in zip(layer_tensors(i), grad_metas(layer_tensors(i))):
            bufs[kl[0]] = final_sum(me, own_parts[kl], recv_parts[kl], bufs[kl[0]], kl[1], m_,
                                    "final_sum_%s_l%d" % kl)
    shared = share_final([bufs[k] for k in BIG])
    for k, gfull in zip(BIG, shared):
        grads[k] = gfull

    delta, new_m, new_v = {}, {}, {}
    for k in order:
        shp = weights[k].shape
        if k in BIG:
            view = shp
        elif k == "a_w_s":
            view = (2, SGU_G * SGU_BLOCK, SGU_BLOCK)
        elif len(shp) == 1:
            view = (1, 1, shp[0])
        elif len(shp) == 2:
            view = (1,) + shp
        else:
            view = shp
        d_, m_, v_ = adamw(weights[k].reshape(view), grads[k].reshape(view), mom_m[k].reshape(view),
                           mom_v[k].reshape(view), "adamw_" + k)
        delta[k], new_m[k], new_v[k] = d_.reshape(shp), m_.reshape(shp), v_.reshape(shp)
    for k in transposed:
        for tree in (grads, delta, new_m, new_v):
            tree[k] = jnp.swapaxes(tree[k], 1, 2)

    return (loss, dx.reshape(1, T, D), *[grads[k] for k in order], *[delta[k] for k in order],
            *[new_m[k] for k in order], *[new_v[k] for k in order])
```

```python
import functools

import jax
import jax.numpy as jnp
from jax import lax
from jax.experimental import pallas as pl
from jax.experimental.pallas import tpu as pltpu

F32 = jnp.float32
BF16 = jnp.bfloat16
MESH = pl.DeviceIdType.MESH

D = 1024
DEPTH = 4
EPS = 1e-6
SGU_BLOCK = 128
GH = 2048
SGU_G = 8
SGU_GD = GH // SGU_G
N_HEADS = 16
HEAD_DIM = 64
CHUNK = 64
PAD = 8 * CHUNK
FRONT = 2048
QB = 128
KW = PAD + QB
N_REL = 192
REL_MIN = -(CHUNK - 1)
REL_MAX = 128
D_FF = 2816
FS = D_FF // 4
NEG = -1e30
SCALE = HEAD_DIM ** -0.5
N_CHIPS = 4

ADAM_LR = 0.001
ADAM_B1 = 0.9
ADAM_B2 = 0.999
ADAM_EPS = 1e-08
ADAM_WD = 0.01
ADAM_STEP = 10

VMEM_BIG = 56 * 1024 * 1024

NN = ((1,), (0,))
NT = ((1,), (1,))
TN = ((0,), (0,))


def _dot(a, b, dims):
    return lax.dot_general(a, b, (dims, ((), ())), preferred_element_type=F32)


class Comm:
    def __init__(self, ins, out_shapes, sems, start, wait, aliases=None):
        self.ins, self.out_shapes, self.sems = list(ins), list(out_shapes), list(sems)
        self.start, self.wait, self.aliases = start, wait, dict(aliases or {})


def _host(body, comm, kw):
    grid = tuple(kw["grid"])
    in_specs = list(kw["in_specs"])
    single = not isinstance(kw["out_specs"], (list, tuple))
    out_specs = [kw["out_specs"]] if single else list(kw["out_specs"])
    out_shape = [kw["out_shape"]] if single else list(kw["out_shape"])
    scratch = list(kw.get("scratch_shapes", ()))
    counts = (len(in_specs), len(comm.ins), len(out_specs), len(comm.out_shapes), len(scratch))

    def hosted(*refs):
        parts, p = [], 0
        for cnt in counts:
            parts.append(refs[p:p + cnt])
            p += cnt
        main_in, c_in, main_out, c_out, main_scr = parts
        sems = refs[p:]
        ids = [pl.program_id(a) for a in range(len(grid))]
        first = functools.reduce(jnp.logical_and, [i == 0 for i in ids])
        last = functools.reduce(jnp.logical_and, [i == n - 1 for i, n in zip(ids, grid)])
        pl.when(first)(lambda: comm.start(c_in, c_out, sems))
        body(*main_in, *main_out, *main_scr)
        pl.when(last)(lambda: comm.wait(c_in, c_out, sems))

    old = kw["compiler_params"]
    kw = dict(kw, in_specs=in_specs + [ANY] * len(comm.ins), out_specs=out_specs + [ANY] * len(comm.out_shapes),
              out_shape=out_shape + comm.out_shapes, scratch_shapes=scratch + comm.sems,
              compiler_params=pltpu.CompilerParams(dimension_semantics=("arbitrary",) * len(grid),
                                                   vmem_limit_bytes=old.vmem_limit_bytes, has_side_effects=True))
    if comm.aliases:
        kw["input_output_aliases"] = {counts[0] + i: counts[2] + o for i, o in comm.aliases.items()}
    return hosted, kw


def _pallas(body, comm=None, **kw):
    if comm is not None:
        body, kw = _host(body, comm, kw)
    return pl.pallas_call(body, **kw)


def _split_outs(outs, comm, n_main):
    outs = list(outs) if isinstance(outs, (list, tuple)) else [outs]
    main = outs[:n_main]
    return (main[0] if n_main == 1 else main), outs[n_main:]


def run_comm(comm, name):
    nci, nco = len(comm.ins), len(comm.out_shapes)

    def body(*refs):
        c_in, c_out, sems = refs[:nci], refs[nci:nci + nco], refs[nci + nco:]
        comm.start(c_in, c_out, sems)
        comm.wait(c_in, c_out, sems)

    kw = {}
    if comm.aliases:
        kw["input_output_aliases"] = dict(comm.aliases)
    return _pallas(body, name=name, in_specs=[ANY] * nci, out_specs=[ANY] * nco, out_shape=comm.out_shapes,
                   scratch_shapes=comm.sems, compiler_params=pltpu.CompilerParams(has_side_effects=True),
                   **kw)(*comm.ins)


def combine(comms):
    if len(comms) == 1:
        return comms[0]
    spans, ni, no, ns = [], 0, 0, 0
    for c in comms:
        spans.append((slice(ni, ni + len(c.ins)), slice(no, no + len(c.out_shapes)), slice(ns, ns + len(c.sems))))
        ni, no, ns = ni + len(c.ins), no + len(c.out_shapes), ns + len(c.sems)

    def start(ins, outs, sems):
        for c, (si, so, ss) in zip(comms, spans):
            c.start(ins[si], outs[so], sems[ss])

    def wait(ins, outs, sems):
        for c, (si, so, ss) in zip(comms, spans):
            c.wait(ins[si], outs[so], sems[ss])

    aliases = {}
    for c, (si, so, _) in zip(comms, spans):
        aliases.update({si.start + i: so.start + o for i, o in c.aliases.items()})
    return Comm([a for c in comms for a in c.ins], [o for c in comms for o in c.out_shapes],
                [s for c in comms for s in c.sems], start, wait, aliases)


def _call(body, comm, n_main, args, **kw):
    if comm is None:
        return _pallas(body, **kw)(*args)
    return _split_outs(_pallas(body, comm=comm, **kw)(*args, *comm.ins), comm, n_main)


def _params(sem=None, vmem=None):
    return pltpu.CompilerParams(dimension_semantics=sem, vmem_limit_bytes=vmem)


def _sds(shape, dtype):
    return jax.ShapeDtypeStruct(tuple(shape), dtype)


_GELU_C = 0.7978845608028654


_GELU_A = _GELU_C * 0.044715


def _gelu(x):
    t = jnp.tanh(x * (_GELU_C + _GELU_A * (x * x)))
    h = 0.5 * x
    return h + h * t


def _gelu_and_grad(x):
    x2 = x * x
    t = jnp.tanh(x * (_GELU_C + _GELU_A * x2))
    h = 0.5 * x
    val = h + h * t
    grad = (0.5 + 0.5 * t) + (h * (1.0 - t * t)) * (_GELU_C + (3.0 * _GELU_A) * x2)
    return val, grad


def _sigmoid(x):
    return 0.5 * (jnp.tanh(0.5 * x) + 1.0)


def cast_bf16(w, name):
    L, R, C = w.shape

    def body(w_ref, o_ref):
        o_ref[...] = w_ref[...].astype(BF16)

    spec = pl.BlockSpec((None, R, C), lambda l: (l, 0, 0))
    return _pallas(body, name=name, grid=(L,), in_specs=[spec], out_specs=spec,
                   out_shape=_sds((L, R, C), BF16), compiler_params=_params(("parallel",)))(w)


def rms_fwd(x, g, name, tm=512, comm=None):
    T = x.shape[0]

    def body(x_ref, g_ref, o_ref):
        o_ref[...] = _rms_rows(x_ref[...], g_ref[...])

    row = pl.BlockSpec((tm, D), lambda i: (i, 0))
    return _call(body, comm, 1, (x, g), name=name, grid=(T // tm,),
                 in_specs=[row, pl.BlockSpec((1, D), lambda i: (0, 0))], out_specs=row,
                 out_shape=_sds((T, D), BF16), compiler_params=_params(("parallel",)))


def dgrad_rms(name, compute, args, specs, x, g, dres, tm=512, comm=None):
    T = x.shape[0]
    n = T // tm
    k = len(args)

    def body(*refs):
        x_ref, g_ref, dres_ref, dx_ref, dxb_ref, dg_ref, acc_ref = refs[k:]
        i = pl.program_id(0)
        xf = x_ref[...]
        r = lax.rsqrt(jnp.mean(xf * xf, axis=-1, keepdims=True) + EPS)
        xhat = xf * r
        dhf = compute(*refs[:k])
        part = (dhf * xhat).reshape(tm // 8, 8, D).sum(axis=0)

        @pl.when(i == 0)
        def _():
            acc_ref[...] = part

        @pl.when(i > 0)
        def _():
            acc_ref[...] += part

        dxhat = dhf * g_ref[...]
        dx = dres_ref[...] + r * (dxhat - xhat * jnp.mean(dxhat * xhat, axis=-1, keepdims=True))
        dx_ref[...] = dx
        dxb_ref[...] = dx.astype(BF16)

        @pl.when(i == n - 1)
        def _():
            dg_ref[...] = jnp.sum(acc_ref[...], axis=0, keepdims=True)

    row = pl.BlockSpec((tm, D), lambda i: (i, 0))
    vec = pl.BlockSpec((1, D), lambda i: (0, 0))
    return _call(body, comm, 3, (*args, x, g, dres), name=name, grid=(n,),
                 in_specs=list(specs) + [row, vec, row], out_specs=[row, row, vec],
                 out_shape=[_sds((T, D), F32), _sds((T, D), BF16), _sds((1, D), F32)],
                 scratch_shapes=[pltpu.VMEM((8, D), F32)],
                 compiler_params=_params(("arbitrary",), VMEM_BIG))


def final_loss(x, g, tgt, name, tm=256):
    T = x.shape[0]
    n = T // tm

    def body(x_ref, g_ref, t_ref, loss_ref, dx_ref, dxb_ref, dg_ref, acc_ref, lacc_ref):
        i = pl.program_id(0)
        xf = x_ref[...]
        r = lax.rsqrt(jnp.mean(xf * xf, axis=-1, keepdims=True) + EPS)
        xhat = xf * r
        gg = g_ref[...]
        e = xhat * gg - t_ref[...]
        dy = e * (1.0 / D)
        part = (dy * xhat).reshape(tm // 8, 8, D).sum(axis=0)
        lpart = (e * e).reshape(tm // 8, 8, D).sum(axis=0)

        @pl.when(i == 0)
        def _():
            acc_ref[...] = part
            lacc_ref[...] = lpart

        @pl.when(i > 0)
        def _():
            acc_ref[...] += part
            lacc_ref[...] += lpart

        dxhat = dy * gg
        dx = r * (dxhat - xhat * jnp.mean(dxhat * xhat, axis=-1, keepdims=True))
        dx_ref[...] = dx
        dxb_ref[...] = dx.astype(BF16)

        @pl.when(i == n - 1)
        def _():
            dg_ref[...] = jnp.sum(acc_ref[...], axis=0, keepdims=True)
            total = jnp.sum(jnp.sum(lacc_ref[...], axis=0, keepdims=True), axis=1, keepdims=True)
            loss_ref[...] = jnp.broadcast_to(total * (0.5 / D), (1, 128))

    row = pl.BlockSpec((tm, D), lambda i: (i, 0))
    vec = pl.BlockSpec((1, D), lambda i: (0, 0))
    return _pallas(body, name=name, grid=(n,), in_specs=[row, vec, row],
                   out_specs=[pl.BlockSpec((1, 128), lambda i: (0, 0)), row, row, vec],
                   out_shape=[_sds((1, 128), F32), _sds((T, D), F32), _sds((T, D), BF16), _sds((1, D), F32)],
                   scratch_shapes=[pltpu.VMEM((8, D), F32), pltpu.VMEM((8, D), F32)],
                   compiler_params=_params(("arbitrary",)))(x, g, tgt)


def matmul(name, dims, a, a_spec, b, b_spec, out_shape, out_spec, grid, *, acc=False, res=None, res_spec=None,
           comm=None):
    has_res = res is not None

    def body(*refs):
        a_ref, b_ref = refs[0], refs[1]
        r_ref = refs[2] if has_res else None
        o_ref = refs[-1]
        d = _dot(a_ref[...], b_ref[...], dims)
        if not acc:
            if has_res:
                d = d + r_ref[...]
            o_ref[...] = d.astype(o_ref.dtype)
        else:
            k = pl.program_id(len(grid) - 1)

            @pl.when(k == 0)
            def _():
                o_ref[...] = (d + r_ref[...]) if has_res else d

            @pl.when(k > 0)
            def _():
                o_ref[...] += d

    sem = ("parallel",) * (len(grid) - 1) + (("arbitrary",) if acc else ("parallel",))
    ins = [a, b] + ([res] if has_res else [])
    specs = [a_spec, b_spec] + ([res_spec] if has_res else [])
    return _call(body, comm, 1, ins, name=name, grid=grid, in_specs=specs, out_specs=out_spec, out_shape=out_shape,
                 compiler_params=_params(sem, VMEM_BIG))


def wgrad(name, a, a_spec, b, b_spec, out_shape, out_spec, J, T, tk, comm=None):
    return matmul(name, TN, a, a_spec, b, b_spec, out_shape, out_spec, (J, T // tk), acc=True, comm=comm)


def _sgu_mask():
    p = lax.broadcasted_iota(jnp.int32, (SGU_BLOCK, SGU_BLOCK), 0)
    q = lax.broadcasted_iota(jnp.int32, (SGU_BLOCK, SGU_BLOCK), 1)
    return lax.shift_right_logical(q, 6) <= lax.shift_right_logical(p, 6)


def sgu_fwd(pre, gain, w_s, b_s, name, comm=None):
    T = pre.shape[0]

    def body(pre_ref, gain_ref, ws_ref, bs_ref, y_ref):
        mask = _sgu_mask()
        u = _gelu(pre_ref[:, :GH].astype(F32))
        va = _gelu(pre_ref[:, GH:].astype(F32))
        r = lax.rsqrt(jnp.mean(va * va, axis=-1, keepdims=True) + EPS)
        vn = ((va * r) * gain_ref[...]).astype(BF16)
        for g in range(SGU_G):
            sl = slice(g * SGU_GD, (g + 1) * SGU_GD)
            wm = jnp.where(mask, ws_ref[g], 0.0).astype(BF16)
            vm = _dot(wm, vn[:, sl], NN) + bs_ref[g]
            y_ref[:, sl] = (u[:, sl] * vm).astype(BF16)

    return _call(
        body, comm, 1, (pre, gain, w_s, b_s), name=name, grid=(T // SGU_BLOCK,),
        in_specs=[pl.BlockSpec((SGU_BLOCK, 2 * GH), lambda i: (i, 0)),
                  pl.BlockSpec((1, GH), lambda i: (0, 0)),
                  pl.BlockSpec((SGU_G, SGU_BLOCK, SGU_BLOCK), lambda i: (0, 0, 0)),
                  pl.BlockSpec((SGU_G, SGU_BLOCK, 1), lambda i: (0, 0, 0))],
        out_specs=pl.BlockSpec((SGU_BLOCK, GH), lambda i: (i, 0)),
        out_shape=_sds((T, GH), BF16), compiler_params=_params(("parallel",)))


def sgu_bwd(pre, dy, gain, w_s, b_s, name, tm=SGU_BLOCK, comm=None):
    T = pre.shape[0]
    n = T // tm

    def body(pre_ref, dy_ref, gain_ref, ws_ref, bs_ref, dpre_ref, dws_ref, dbs_ref, dgain_ref, gacc_ref):
        i = pl.program_id(0)

        @pl.when(i == 0)
        def _():
            dws_ref[...] = jnp.zeros_like(dws_ref)
            dbs_ref[...] = jnp.zeros_like(dbs_ref)
            gacc_ref[...] = jnp.zeros_like(gacc_ref)

        mask = _sgu_mask()
        gain_v = gain_ref[...]
        for sb in range(tm // SGU_BLOCK):
            rows = slice(sb * SGU_BLOCK, (sb + 1) * SGU_BLOCK)
            u, du_dpre = _gelu_and_grad(pre_ref[rows, :GH].astype(F32))
            va, dva_dpre = _gelu_and_grad(pre_ref[rows, GH:].astype(F32))
            r = lax.rsqrt(jnp.mean(va * va, axis=-1, keepdims=True) + EPS)
            vhat = va * r
            vn = (vhat * gain_v).astype(BF16)
            dyf = dy_ref[rows, :].astype(F32)
            dvn_parts = []
            for grp in range(SGU_G):
                sl = slice(grp * SGU_GD, (grp + 1) * SGU_GD)
                wm = jnp.where(mask, ws_ref[grp], 0.0).astype(BF16)
                vm = _dot(wm, vn[:, sl], NN) + bs_ref[grp]
                dpre_ref[rows, sl] = ((dyf[:, sl] * vm) * du_dpre[:, sl]).astype(BF16)
                dvm = dyf[:, sl] * u[:, sl]
                dbs_ref[grp] += jnp.sum(dvm, axis=-1, keepdims=True)
                dvm16 = dvm.astype(BF16)
                dws_ref[grp] += jnp.where(mask, _dot(dvm16, vn[:, sl], NT), 0.0)
                dvn_parts.append(_dot(wm, dvm16, TN))
            dvn = jnp.concatenate(dvn_parts, axis=-1)
            gacc_ref[...] += (dvn * vhat).reshape(SGU_BLOCK // 8, 8, GH).sum(axis=0)
            dvhat = dvn * gain_v
            dva = r * (dvhat - vhat * jnp.mean(dvhat * vhat, axis=-1, keepdims=True))
            dpre_ref[rows, GH:] = (dva * dva_dpre).astype(BF16)

        @pl.when(i == n - 1)
        def _():
            dgain_ref[...] = jnp.sum(gacc_ref[...], axis=0, keepdims=True)

    const3 = lambda i: (0, 0, 0)
    return _call(
        body, comm, 4, (pre, dy, gain, w_s, b_s), name=name, grid=(n,),
        in_specs=[pl.BlockSpec((tm, 2 * GH), lambda i: (i, 0)),
                  pl.BlockSpec((tm, GH), lambda i: (i, 0)),
                  pl.BlockSpec((1, GH), lambda i: (0, 0)),
                  pl.BlockSpec((SGU_G, SGU_BLOCK, SGU_BLOCK), const3),
                  pl.BlockSpec((SGU_G, SGU_BLOCK, 1), const3)],
        out_specs=[pl.BlockSpec((tm, 2 * GH), lambda i: (i, 0)),
                   pl.BlockSpec((SGU_G, SGU_BLOCK, SGU_BLOCK), const3),
                   pl.BlockSpec((SGU_G, SGU_BLOCK, 1), const3),
                   pl.BlockSpec((1, GH), lambda i: (0, 0))],
        out_shape=[_sds((T, 2 * GH), BF16), _sds((SGU_G, SGU_BLOCK, SGU_BLOCK), F32),
                   _sds((SGU_G, SGU_BLOCK, 1), F32), _sds((1, GH), F32)],
        scratch_shapes=[pltpu.VMEM((8, GH), F32)],
        compiler_params=_params(("arbitrary",)))


DIAG = 768


def _diag_onehot():
    n = lax.broadcasted_iota(jnp.int32, (N_REL, DIAG), 1)
    r = lax.broadcasted_iota(jnp.int32, (N_REL, DIAG), 0)
    idx = jnp.clip(KW - 1 - n, REL_MIN, REL_MAX) - REL_MIN
    return (idx == r).astype(BF16)


def _split3(v):
    hi = v.astype(BF16)
    r1 = v - hi.astype(F32)
    mid = r1.astype(BF16)
    lo = (r1 - mid.astype(F32)).astype(BF16)
    return hi, mid, lo


def bias_build(rel_bias, name):
    def body(rb_ref, o_ref):
        oh = _diag_onehot()
        hi, mid, lo = _split3(rb_ref[...])
        u = (_dot(hi, oh, NN) + _dot(mid, oh, NN) + _dot(lo, oh, NN)) * LOG2E
        j = lax.broadcasted_iota(jnp.int32, (1, KW), 1)

        def row(i, carry):
            val = pltpu.roll(u, (i + (DIAG - QB + 1)) % DIAG, 1)[:, :KW]
            rel = lax.shift_right_logical(i, 6) - lax.shift_right_logical(j, 6) + 8
            ok = (rel >= 0) & (rel <= 8)
            o_ref[i] = jnp.where(ok, val, NEG)
            return carry

        lax.fori_loop(0, QB, row, 0)

    return _pallas(body, name=name, out_shape=_sds((QB, N_HEADS, KW), F32),
                   in_specs=[pl.BlockSpec(memory_space=pltpu.VMEM)],
                   out_specs=pl.BlockSpec(memory_space=pltpu.VMEM))(rel_bias)


def bias_grad(dwb, name):
    def body(d_ref, o_ref):
        def row(i, acc):
            return acc + pltpu.roll(d_ref[i], QB - 1 - i, 1)

        du = lax.fori_loop(0, QB, row, jnp.zeros((N_HEADS, DIAG), F32))
        oh = _diag_onehot()
        hi, mid, lo = _split3(du)
        o_ref[...] = _dot(hi, oh, NT) + _dot(mid, oh, NT) + _dot(lo, oh, NT)

    return _pallas(body, name=name, out_shape=_sds((N_HEADS, N_REL), F32),
                   in_specs=[pl.BlockSpec(memory_space=pltpu.VMEM)],
                   out_specs=pl.BlockSpec(memory_space=pltpu.VMEM))(dwb)


LOG2E = 1.4426950408889634
Q_SCALE = SCALE * LOG2E


def _attn_block(qkv_ref, blk, masked):
    r0 = pl.multiple_of(blk * QB, QB)
    qs = qkv_ref[0, pl.ds(r0 + FRONT, QB), :]
    k2 = qkv_ref[1, pl.ds(r0 + (FRONT - PAD), KW), :]
    v2 = qkv_ref[2, pl.ds(r0 + (FRONT - PAD), KW), :]
    kvalid = (lax.broadcasted_iota(jnp.int32, (1, KW), 1) >= PAD - blk * QB) if masked else None
    return r0, qs, k2, v2, kvalid


def _head_mask(h):
    lane = lax.broadcasted_iota(jnp.int32, (1, 2 * HEAD_DIM), 1)
    return (lane < HEAD_DIM) if h == 0 else (lane >= HEAD_DIM)


def _stack_heads(a):
    zero = jnp.zeros_like(a)
    return jnp.concatenate([jnp.where(_head_mask(0), a, zero), jnp.where(_head_mask(1), a, zero)], axis=0)


def _unstack_heads(a):
    return jnp.where(_head_mask(0), a[:QB], a[QB:])


def _attn_exp(qst, k2, w_ref, kvalid):
    s = _dot(qst, k2, NT) + w_ref[...].reshape(2 * QB, KW)
    if kvalid is not None:
        s = jnp.where(kvalid, s, NEG)
    e = jnp.exp2(s - jnp.max(s, axis=-1, keepdims=True))
    return e, 1.0 / jnp.sum(e, axis=-1, keepdims=True)


ATTN_G = 4
ATTN_STEP = QB * ATTN_G


def _masked_and_not(b, fn):
    n_masked = -(-PAD // ATTN_STEP)
    pl.when(b < n_masked)(functools.partial(fn, True))
    pl.when(b >= n_masked)(functools.partial(fn, False))


def attn_fwd(qkvp, wb, name, comm=None):
    T = qkvp.shape[1] - FRONT

    def body(qkv_ref, w_ref, o_ref):
        b = pl.program_id(1)

        def blocks(masked):
            for t in range(ATTN_G):
                _, qs, k2, v2, kvalid = _attn_block(qkv_ref, b * ATTN_G + t, masked)
                e, inv = _attn_exp(_stack_heads(qs), k2, w_ref, kvalid)
                o_ref[t * QB:(t + 1) * QB, :] = _unstack_heads(_dot(e.astype(BF16), v2, NN) * inv).astype(BF16)

        _masked_and_not(b, blocks)

    return _call(
        body, comm, 1, (qkvp, wb), name=name, grid=(N_HEADS // 2, T // (QB * ATTN_G)),
        in_specs=[pl.BlockSpec((3, FRONT + T, 2 * HEAD_DIM), lambda hp, b: (0, 0, hp)),
                  pl.BlockSpec((2, QB, KW), lambda hp, b: (hp, 0, 0))],
        out_specs=pl.BlockSpec((QB * ATTN_G, 2 * HEAD_DIM), lambda hp, b: (b, hp)),
        out_shape=_sds((T, D), BF16),
        compiler_params=_params(("parallel", "arbitrary"), VMEM_BIG))


def attn_bwd(qkvp, o, do, wb, name, comm=None):
    T = qkvp.shape[1] - FRONT
    nb = T // (QB * ATTN_G)

    def body(qkv_ref, o_ref, do_ref, w_ref, dqkv_ref, dw_ref, dk_acc, dv_acc):
        b = pl.program_id(1)

        @pl.when(b == 0)
        def _():
            dk_acc[...] = jnp.zeros_like(dk_acc)
            dv_acc[...] = jnp.zeros_like(dv_acc)
            dw_ref[...] = jnp.zeros_like(dw_ref)
            dqkv_ref[0, 0:FRONT, :] = jnp.zeros((FRONT, 2 * HEAD_DIM), BF16)

        def blocks(masked):
            dws = None
            for t in range(ATTN_G):
                r0, qs, k2, v2, kvalid = _attn_block(qkv_ref, b * ATTN_G + t, masked)
                qst = _stack_heads(qs)
                e, inv = _attn_exp(qst, k2, w_ref, kvalid)
                do2 = do_ref[t * QB:(t + 1) * QB, :]
                dost = _stack_heads(do2)
                prod = _stack_heads(do2.astype(F32) * o_ref[t * QB:(t + 1) * QB, :].astype(F32))
                delta = jnp.sum(prod, axis=-1, keepdims=True)
                ds = e * ((_dot(dost, v2, NT) - delta) * inv)
                dws = ds if dws is None else dws + ds
                ds16 = ds.astype(BF16)
                dq = _unstack_heads(_dot(ds16, k2, NN)) * SCALE
                dqkv_ref[0, pl.ds(r0 + FRONT, QB), :] = dq.astype(BF16)
                dk_acc[pl.ds(r0 + (FRONT - PAD), KW), :] += _dot(ds16, qst, TN)
                dv_acc[pl.ds(r0 + (FRONT - PAD), KW), :] += _dot(
                    e.astype(BF16), (dost.astype(F32) * inv).astype(BF16), TN)
            dw_ref[...] += dws.reshape(2, QB, KW)

        _masked_and_not(b, blocks)

        @pl.when(b == nb - 1)
        def _():
            dqkv_ref[1] = (dk_acc[...] * (1.0 / LOG2E)).astype(BF16)
            dqkv_ref[2] = dv_acc[...].astype(BF16)

    slab = pl.BlockSpec((3, FRONT + T, 2 * HEAD_DIM), lambda hp, b: (0, 0, hp))
    wspec = pl.BlockSpec((2, QB, KW), lambda hp, b: (hp, 0, 0))
    rows = pl.BlockSpec((QB * ATTN_G, 2 * HEAD_DIM), lambda hp, b: (b, hp))
    return _call(
        body, comm, 2, (qkvp, o, do, wb), name=name, grid=(N_HEADS // 2, nb),
        in_specs=[slab, rows, rows, wspec],
        out_specs=[slab, wspec],
        out_shape=[_sds((3, FRONT + T, D), BF16), _sds((N_HEADS, QB, KW), F32)],
        scratch_shapes=[pltpu.VMEM((FRONT + T, 2 * HEAD_DIM), F32), pltpu.VMEM((FRONT + T, 2 * HEAD_DIM), F32)],
        compiler_params=_params(("parallel", "arbitrary"), VMEM_BIG))


def proj_qkv(hn, w, l, name, tm=512, comm=None):
    T = hn.shape[0]
    pb = FRONT // tm

    def body(a_ref, b_ref, o_ref):
        i = pl.program_id(1)

        @pl.when(i < pb)
        def _():
            o_ref[...] = jnp.zeros_like(o_ref)

        @pl.when(i >= pb)
        def _():
            scale = jnp.where(pl.program_id(0) == 0, Q_SCALE, 1.0).astype(F32)
            o_ref[...] = (_dot(a_ref[...], b_ref[...], NN) * scale).astype(BF16)

    return _call(
        body, comm, 1, (hn, w), name=name, grid=(3, pb + T // tm),
        in_specs=[pl.BlockSpec((tm, D), lambda p, i: (jnp.maximum(i - pb, 0), 0)),
                  pl.BlockSpec((None, D, D), lambda p, i: (l, 0, p))],
        out_specs=pl.BlockSpec((None, tm, D), lambda p, i: (p, i, 0)),
        out_shape=_sds((3, FRONT + T, D), BF16),
        compiler_params=_params(("parallel", "parallel"), VMEM_BIG))


def ffn_up(hn, wg, wu, l, name, tm=1024, comm=None):
    T = hn.shape[0]

    def body(a_ref, wg_ref, wu_ref, g_ref, u_ref, h_ref):
        a = a_ref[...]
        g = _dot(a, wg_ref[...], NT)
        u = _dot(a, wu_ref[...], NT)
        s = _sigmoid(g)
        silu = g * s
        g_ref[...] = (u * (s * (1.0 + g * (1.0 - s)))).astype(BF16)
        u_ref[...] = silu.astype(BF16)
        h_ref[...] = (silu * u).astype(BF16)

    wspec = pl.BlockSpec((None, None, FS, D), lambda s, i: (l, s, 0, 0))
    ospec = pl.BlockSpec((None, tm, FS), lambda s, i: (s, i, 0))
    return _call(
        body, comm, 3, (hn, wg, wu), name=name, grid=(N_CHIPS, T // tm),
        in_specs=[pl.BlockSpec((tm, D), lambda s, i: (i, 0)), wspec, wspec],
        out_specs=[ospec, ospec, ospec],
        out_shape=[_sds((N_CHIPS, T, FS), BF16)] * 3,
        compiler_params=_params(("parallel", "parallel"), VMEM_BIG))


def ffn_bwd_dh(dxb, wd, g, u, l, name, tm=2048, comm=None):
    T = dxb.shape[0]
    tm = min(tm, T)

    def body(a_ref, wd_ref, g_ref, u_ref, dg_ref, du_ref):
        dh = _dot(a_ref[...], wd_ref[...], NT)
        dg_ref[...] = (dh * g_ref[...].astype(F32)).astype(BF16)
        du_ref[...] = (dh * u_ref[...].astype(F32)).astype(BF16)

    aspec = pl.BlockSpec((None, tm, FS), lambda i, s: (s, i, 0))
    return _call(
        body, comm, 2, (dxb, wd, g, u), name=name, grid=(T // tm, N_CHIPS),
        in_specs=[pl.BlockSpec((tm, D), lambda i, s: (i, 0)),
                  pl.BlockSpec((None, None, FS, D), lambda i, s: (l, s, 0, 0)), aspec, aspec],
        out_specs=[aspec, aspec],
        out_shape=[_sds((N_CHIPS, T, FS), BF16)] * 2,
        compiler_params=_params(("parallel", "parallel"), VMEM_BIG))


def ffn_dgrad(dg, du, wg, wu, tm=512):
    def compute(dg_ref, du_ref, wg_ref, wu_ref):
        d = None
        for s in range(N_CHIPS):
            t = _dot(dg_ref[s], wg_ref[s], NN) + _dot(du_ref[s], wu_ref[s], NN)
            d = t if d is None else d + t
        return d

    aspec = pl.BlockSpec((N_CHIPS, tm, FS), lambda i: (0, i, 0))
    wspec = pl.BlockSpec((None, N_CHIPS, FS, D), lambda i: (0, 0, 0, 0), pipeline_mode=pl.Buffered(1))
    return compute, (dg, du, wg, wu), [aspec, aspec, wspec, wspec]


def qkv_dgrad(dqkvp, w, tm=512):
    def compute(a_ref, w_ref):
        d = None
        for p in range(3):
            t = _dot(a_ref[p], w_ref[:, p * D:(p + 1) * D], NT)
            d = t if d is None else d + t
        return d

    return compute, (dqkvp, w), [pl.BlockSpec((3, tm, D), lambda i: (0, i + FRONT // tm, 0)),
                                 pl.BlockSpec((None, D, 3 * D), lambda i: (0, 0, 0))]


def in_dgrad(dpre, w, tm=512):
    def compute(a_ref, w_ref):
        return _dot(a_ref[...], w_ref[...], NT)

    return compute, (dpre, w), [pl.BlockSpec((tm, 2 * GH), lambda i: (i, 0)),
                                pl.BlockSpec((None, D, 2 * GH), lambda i: (0, 0, 0))]


def _rms_rows(x, g):
    r = lax.rsqrt(jnp.mean(x * x, axis=-1, keepdims=True) + EPS)
    return ((x * r) * g).astype(BF16)


def residual_proj(name, compute, args, specs, res, norm_g, tm=512, comm=None):
    T = res.shape[0]
    k = len(args)
    with_norm = norm_g is not None

    def body(*refs):
        d = refs[k][...] + compute(*refs[:k])
        if with_norm:
            refs[k + 2][...] = d
            refs[k + 3][...] = _rms_rows(d, refs[k + 1][...])
        else:
            refs[k + 1][...] = d

    row = pl.BlockSpec((tm, D), lambda i: (i, 0))
    vec = pl.BlockSpec((1, D), lambda i: (0, 0))
    if with_norm:
        return _call(body, comm, 2, (*args, res, norm_g), name=name, grid=(T // tm,),
                     in_specs=list(specs) + [row, vec], out_specs=[row, row],
                     out_shape=[_sds((T, D), F32), _sds((T, D), BF16)],
                     compiler_params=_params(("parallel",), VMEM_BIG))
    return _call(body, comm, 1, (*args, res), name=name, grid=(T // tm,), in_specs=list(specs) + [row],
                 out_specs=row, out_shape=_sds((T, D), F32), compiler_params=_params(("parallel",), VMEM_BIG))


def ffn_down(h, wd, tm=512):
    def compute(h_ref, wd_ref):
        d = None
        for s in range(N_CHIPS):
            t = _dot(h_ref[s], wd_ref[s], NN)
            d = t if d is None else d + t
        return d

    return compute, (h, wd), [pl.BlockSpec((N_CHIPS, tm, FS), lambda i: (0, i, 0)),
                              pl.BlockSpec((None, N_CHIPS, FS, D), lambda i: (0, 0, 0, 0))]


def out_proj(a, w, tm=512):
    K = a.shape[1]

    def compute(a_ref, w_ref):
        return _dot(a_ref[...], w_ref[...], NN)

    return compute, (a, w), [pl.BlockSpec((tm, K), lambda i: (i, 0)), pl.BlockSpec((None, K, D), lambda i: (0, 0, 0))]


def adamw(w, g, m, v, name):
    L, R, C = w.shape

    def body(w_ref, g_ref, m_ref, v_ref, go_ref, d_ref, nm_ref, nv_ref):
        gf = g_ref[...]
        go_ref[...] = gf
        nm = ADAM_B1 * m_ref[...] + (1.0 - ADAM_B1) * gf
        nv = ADAM_B2 * v_ref[...] + (1.0 - ADAM_B2) * (gf * gf)
        m_hat = nm / (1.0 - ADAM_B1 ** ADAM_STEP)
        v_hat = nv / (1.0 - ADAM_B2 ** ADAM_STEP)
        d_ref[...] = -ADAM_LR * (m_hat / (jnp.sqrt(v_hat) + ADAM_EPS) + ADAM_WD * w_ref[...])
        nm_ref[...] = nm
        nv_ref[...] = nv

    tr = R // 4 if R % 32 == 0 else R
    spec = pl.BlockSpec((None, tr, C), lambda l, r: (l, r, 0))
    return _pallas(body, name=name, grid=(L, R // tr), in_specs=[spec] * 4, out_specs=[spec] * 4,
                   out_shape=[_sds((L, R, C), F32)] * 4,
                   compiler_params=_params(("parallel", "parallel")))(w, g, m, v)


def _coords():
    return lax.axis_index("x"), lax.axis_index("y"), lax.axis_index("c")


def _other_chips(x, y):
    out = []
    for fx, fy in ((1, 0), (0, 1), (1, 1)):
        px = (1 - x) if fx else x
        py = (1 - y) if fy else y
        out.append((px, py))
    return out


def _flip_index(s, j):
    sx, sy = s // 2, s % 2
    fx, fy = ((1, 0), (0, 1), (1, 1))[j]
    return 2 * (sx ^ fx) + (sy ^ fy)


def _for_my_chip(sme, fn):
    for s in range(N_CHIPS):
        pl.when(sme == s)(functools.partial(fn, s))


ANY = pl.BlockSpec(memory_space=pl.ANY)

GATHER_KIND = {"a_w_in": "col", "b_w_qkv": "col", "a_w_out": "row", "b_w_out": "row",
               "ffn_w_gate": "row", "ffn_w_up": "row", "ffn_w_down": "row"}
BIG = tuple(GATHER_KIND)


def _gathered_shape(kind, shape):
    L, R, C = shape
    return (L, R, N_CHIPS * C) if kind == "col" else (L, N_CHIPS, R, C)


def _shard_rows(ref, kind, s, r0, rn, C):
    if kind == "col":
        return ref.at[:, pl.ds(r0, rn), s * C:(s + 1) * C]
    return ref.at[:, s, pl.ds(r0, rn), :]


def gather_stage1(items):
    n = len(items)
    dims = [it[0].shape[1:] for it in items]

    def copies(ins, outs, sems, s, with_landed=True):
        lsem, ssem, rsem = sems
        x, y, c = _coords()
        chips = _other_chips(x, y)
        local, send, landed = [], [], []
        for t, (_, li, kind) in enumerate(items):
            R, C = dims[t]
            r0 = pl.multiple_of(c * (R // 2), 8)
            local.append(pltpu.make_async_copy(ins[t].at[pl.ds(li, 1)], _shard_rows(outs[t], kind, s, 0, R, C),
                                               lsem.at[t]))
            for j in range(3):
                pair = dict(send_sem=ssem.at[3 * t + j], recv_sem=rsem.at[3 * t + j],
                            device_id=(chips[j][0], chips[j][1], c), device_id_type=MESH)
                send.append(pltpu.make_async_remote_copy(
                    src_ref=ins[t].at[pl.ds(li, 1), pl.ds(r0, R // 2), :],
                    dst_ref=_shard_rows(outs[t], kind, s, r0, R // 2, C), **pair))
                if with_landed:
                    got = _shard_rows(outs[t], kind, _flip_index(s, j), r0, R // 2, C)
                    landed.append(pltpu.make_async_remote_copy(src_ref=got, dst_ref=got, **pair))
        return local, send, landed

    def start(ins, outs, sems):
        def run(s):
            local, send, _ = copies(ins, outs, sems, s, with_landed=False)
            for cp in local + send:
                cp.start()
        x, y, _ = _coords()
        _for_my_chip(2 * x + y, run)

    def wait(ins, outs, sems):
        def run(s):
            local, send, landed = copies(ins, outs, sems, s)
            for cp in landed:
                cp.wait_recv()
            for cp in send:
                cp.wait_send()
            for cp in local:
                cp.wait()
        x, y, _ = _coords()
        _for_my_chip(2 * x + y, run)

    out_shapes = [_sds(_gathered_shape(kind, (1,) + tuple(dims[t])), BF16) for t, (_, _, kind) in enumerate(items)]
    sems = [pltpu.SemaphoreType.DMA((n,)), pltpu.SemaphoreType.DMA((3 * n,)), pltpu.SemaphoreType.DMA((3 * n,))]
    return Comm([it[0] for it in items], out_shapes, sems, start, wait)


def gather_stage2(items, gathered):
    n = len(items)
    dims = [it[0].shape[1:] for it in items]

    def copies(outs, sems, s, with_landed=True):
        ssem, rsem = sems
        x, y, c = _coords()
        send, landed = [], []
        for t, (_, _, kind) in enumerate(items):
            R, C = dims[t]
            for j in range(3):
                pair = dict(send_sem=ssem.at[3 * t + j], recv_sem=rsem.at[3 * t + j],
                            device_id=(x, y, 1 - c), device_id_type=MESH)
                mine = _shard_rows(outs[t], kind, _flip_index(s, j), pl.multiple_of(c * (R // 2), 8), R // 2, C)
                send.append(pltpu.make_async_remote_copy(src_ref=mine, dst_ref=mine, **pair))
                if with_landed:
                    other = _shard_rows(outs[t], kind, _flip_index(s, j), pl.multiple_of((1 - c) * (R // 2), 8),
                                        R // 2, C)
                    landed.append(pltpu.make_async_remote_copy(src_ref=other, dst_ref=other, **pair))
        return send, landed

    def start(ins, outs, sems):
        def run(s):
            for cp in copies(outs, sems, s, with_landed=False)[0]:
                cp.start()
        x, y, _ = _coords()
        _for_my_chip(2 * x + y, run)

    def wait(ins, outs, sems):
        def run(s):
            send, landed = copies(outs, sems, s)
            for cp in landed:
                cp.wait_recv()
            for cp in send:
                cp.wait_send()
        x, y, _ = _coords()
        _for_my_chip(2 * x + y, run)

    out_shapes = [_sds(g.shape, BF16) for g in gathered]
    sems = [pltpu.SemaphoreType.DMA((3 * n,)), pltpu.SemaphoreType.DMA((3 * n,))]
    return Comm(gathered, out_shapes, sems, start, wait, aliases={t: t for t in range(n)})


def gather_both(items):
    s1 = gather_stage1(items)
    s2 = gather_stage2(items, s1.out_shapes)
    n1 = len(s1.sems)

    def wait(ins, outs, sems):
        s1.wait(ins, outs, sems[:n1])
        s2.start((), outs, sems[n1:])
        s2.wait((), outs, sems[n1:])

    return Comm(s1.ins, s1.out_shapes, s1.sems + s2.sems, lambda ins, outs, sems: s1.start(ins, outs, sems[:n1]), wait)


def _half_shape(kind, R, C):
    return (R // 2, N_CHIPS * C) if kind == "col" else (N_CHIPS, R // 2, C)


def exchange_halves(grads, metas):
    n = len(grads)

    def copies(ins, outs, sems):
        ssem, rsem = sems
        x, y, c = _coords()
        out = []
        for t, (kind, R, C) in enumerate(metas):
            r0 = pl.multiple_of((1 - c) * (R // 2), 8)
            src = ins[t].at[pl.ds(r0, R // 2), :] if kind == "col" else ins[t].at[:, pl.ds(r0, R // 2), :]
            out.append(pltpu.make_async_remote_copy(
                src_ref=src, dst_ref=outs[t], send_sem=ssem.at[t], recv_sem=rsem.at[t],
                device_id=(x, y, 1 - c), device_id_type=MESH))
        return out

    def start(ins, outs, sems):
        for cp in copies(ins, outs, sems):
            cp.start()

    def wait(ins, outs, sems):
        for cp in copies(ins, outs, sems):
            cp.wait()

    return Comm(grads, [_sds(_half_shape(*m), F32) for m in metas], [pltpu.SemaphoreType.DMA((n,))] * 2, start, wait)


def pair_sum(me, g, sib, meta, name):
    kind, R, C = meta
    h = R // 2

    def body(me_ref, g_ref, sib_ref, p16_ref, own_ref):
        s = pl.program_id(0)
        v = g_ref[...] + sib_ref[...]
        p16_ref[...] = v.astype(BF16)

        @pl.when(s == me_ref[1])
        def _():
            own_ref[...] = v

    if kind == "col":
        gspec = pl.BlockSpec((h, C), lambda s, me_ref: (me_ref[0], s))
        sspec = pl.BlockSpec((h, C), lambda s, me_ref: (0, s))
    else:
        gspec = pl.BlockSpec((None, h, C), lambda s, me_ref: (s, me_ref[0], 0))
        sspec = pl.BlockSpec((None, h, C), lambda s, me_ref: (s, 0, 0))
    grid_spec = pltpu.PrefetchScalarGridSpec(
        num_scalar_prefetch=1, grid=(N_CHIPS,), in_specs=[gspec, sspec],
        out_specs=[sspec, pl.BlockSpec((h, C), lambda s, me_ref: (0, 0))])
    return _pallas(body, name=name, grid_spec=grid_spec,
                   out_shape=[_sds(_half_shape(*meta), BF16), _sds((h, C), F32)],
                   compiler_params=_params(("arbitrary",), VMEM_BIG))(me, g, sib)


def scatter_partials(p16s, metas):
    n = len(p16s)

    def copies(ins, outs, sems, s):
        ssem, rsem = sems
        x, y, c = _coords()
        chips = _other_chips(x, y)
        out = []
        for t, (kind, R, C) in enumerate(metas):
            for j in range(3):
                sj = _flip_index(s, j)
                src = ins[t].at[:, sj * C:(sj + 1) * C] if kind == "col" else ins[t].at[sj]
                out.append(pltpu.make_async_remote_copy(
                    src_ref=src, dst_ref=outs[t].at[j], send_sem=ssem.at[3 * t + j], recv_sem=rsem.at[3 * t + j],
                    device_id=(chips[j][0], chips[j][1], c), device_id_type=MESH))
        return out

    def start(ins, outs, sems):
        def run(s):
            for cp in copies(ins, outs, sems, s):
                cp.start()
        x, y, _ = _coords()
        _for_my_chip(2 * x + y, run)

    def wait(ins, outs, sems):
        def run(s):
            for cp in copies(ins, outs, sems, s):
                cp.wait()
        x, y, _ = _coords()
        _for_my_chip(2 * x + y, run)

    return Comm(p16s, [_sds((3, R // 2, C), BF16) for (_, R, C) in metas],
                [pltpu.SemaphoreType.DMA((3 * n,))] * 2, start, wait)


def final_sum(me, own, q, buf, l, meta, name):
    _, R, C = meta
    h = R // 2

    def body(me_ref, own_ref, q_ref, buf_ref, o_ref):
        del buf_ref
        o_ref[...] = ((own_ref[...] + q_ref[0].astype(F32)) + q_ref[1].astype(F32)) + q_ref[2].astype(F32)

    grid_spec = pltpu.PrefetchScalarGridSpec(
        num_scalar_prefetch=1, grid=(1,),
        in_specs=[pl.BlockSpec((h, C), lambda i, me_ref: (0, 0)),
                  pl.BlockSpec((3, h, C), lambda i, me_ref: (0, 0, 0)), ANY],
        out_specs=pl.BlockSpec((None, h, C), lambda i, me_ref: (l, me_ref[0], 0)))
    return _pallas(body, name=name, grid_spec=grid_spec, out_shape=_sds(buf.shape, F32),
                   input_output_aliases={3: 0},
                   compiler_params=_params(("arbitrary",), VMEM_BIG))(me, own, q, buf)


def share_final(bufs):
    n = len(bufs)

    def body(*refs):
        ins, outs = refs[:n], refs[n:2 * n]
        ssem, rsem = refs[2 * n:]
        del ins
        x, y, c = _coords()
        copies = []
        for t in range(n):
            R = bufs[t].shape[1]
            r0 = pl.multiple_of(c * (R // 2), 8)
            blk = outs[t].at[:, pl.ds(r0, R // 2), :]
            copies.append(pltpu.make_async_remote_copy(
                src_ref=blk, dst_ref=blk, send_sem=ssem.at[t], recv_sem=rsem.at[t],
                device_id=(x, y, 1 - c), device_id_type=MESH))
        for cp in copies:
            cp.start()
        for t in range(n):
            R = bufs[t].shape[1]
            r1 = pl.multiple_of((1 - c) * (R // 2), 8)
            other = outs[t].at[:, pl.ds(r1, R // 2), :]
            pltpu.make_async_remote_copy(
                src_ref=other, dst_ref=other, send_sem=ssem.at[t], recv_sem=rsem.at[t],
                device_id=(x, y, 1 - c), device_id_type=MESH).wait_recv()
        for cp in copies:
            cp.wait_send()

    out_shape = [_sds(b.shape, F32) for b in bufs]
    return _pallas(body, name="share_final", in_specs=[ANY] * n, out_specs=[ANY] * n, out_shape=out_shape,
                   input_output_aliases={t: t for t in range(n)},
                   scratch_shapes=[pltpu.SemaphoreType.DMA((n,))] * 2,
                   compiler_params=pltpu.CompilerParams(has_side_effects=True))(*bufs)


def allreduce_small(part):
    rows = part.shape[0]
    h = rows // 2

    def body(p_ref, o_ref, sib_buf, pair_buf, chip_buf, ssem, rsem):
        x, y, c = _coords()
        sibling = dict(device_id=(x, y, 1 - c), device_id_type=MESH)
        mine = pl.ds(pl.multiple_of(c * h, 8), h)
        theirs = pl.ds(pl.multiple_of((1 - c) * h, 8), h)

        swap = pltpu.make_async_remote_copy(src_ref=p_ref.at[theirs], dst_ref=sib_buf, send_sem=ssem.at[0],
                                            recv_sem=rsem.at[0], **sibling)
        swap.start()
        swap.wait()
        pair_buf[...] = p_ref[mine, :] + sib_buf[...]

        chips = _other_chips(x, y)
        sends = [pltpu.make_async_remote_copy(src_ref=pair_buf, dst_ref=chip_buf.at[j], send_sem=ssem.at[1 + j],
                                              recv_sem=rsem.at[1 + j], device_id=(chips[j][0], chips[j][1], c),
                                              device_id_type=MESH) for j in range(3)]
        for cp in sends:
            cp.start()
        for cp in sends:
            cp.wait()

        def total(s):
            terms = {s: pair_buf[...]}
            for j in range(3):
                terms[_flip_index(s, j)] = chip_buf[j]
            o_ref[mine, :] = ((terms[0] + terms[1]) + terms[2]) + terms[3]

        _for_my_chip(2 * x + y, total)

        back = pltpu.make_async_remote_copy(src_ref=o_ref.at[mine], dst_ref=o_ref.at[mine], send_sem=ssem.at[4],
                                            recv_sem=rsem.at[4], **sibling)
        back.start()
        pltpu.make_async_remote_copy(src_ref=o_ref.at[theirs], dst_ref=o_ref.at[theirs], send_sem=ssem.at[4],
                                     recv_sem=rsem.at[4], **sibling).wait_recv()
        back.wait_send()

    return _pallas(body, name="allreduce_small",
                   in_specs=[pl.BlockSpec(memory_space=pltpu.VMEM)], out_specs=pl.BlockSpec(memory_space=pltpu.VMEM),
                   out_shape=_sds((rows, 128), F32),
                   scratch_shapes=[pltpu.VMEM((h, 128), F32), pltpu.VMEM((h, 128), F32), pltpu.VMEM((3, h, 128), F32),
                                   pltpu.SemaphoreType.DMA((5,)), pltpu.SemaphoreType.DMA((5,))],
                   compiler_params=pltpu.CompilerParams(has_side_effects=True))(part)


def _rows128(a):
    flat = a.reshape(-1)
    rows = -(-flat.shape[0] // 128)
    rows8 = -(-rows // 8) * 8
    flat = jnp.pad(flat, (0, rows8 * 128 - flat.shape[0]))
    return flat.reshape(rows8, 128)


def kernel(x, norm_mix_g, norm_ffn_g, final_g, a_w_in, a_v_gain, a_w_s, a_b_s, a_w_out, b_w_qkv, b_rel_bias, b_w_out, ffn_w_gate, ffn_w_up, ffn_w_down, loss_target, m_norm_mix_g, m_norm_ffn_g, m_final_g, m_a_w_in, m_a_v_gain, m_a_w_s, m_a_b_s, m_a_w_out, m_b_w_qkv, m_b_rel_bias, m_b_w_out, m_ffn_w_gate, m_ffn_w_up, m_ffn_w_down, v_norm_mix_g, v_norm_ffn_g, v_final_g, v_a_w_in, v_a_v_gain, v_a_w_s, v_a_b_s, v_a_w_out, v_b_w_qkv, v_b_rel_bias, v_b_w_out, v_ffn_w_gate, v_ffn_w_up, v_ffn_w_down):
    T = x.shape[1]
    weights = dict(norm_mix_g=norm_mix_g, norm_ffn_g=norm_ffn_g, final_g=final_g, a_w_in=a_w_in, a_v_gain=a_v_gain,
                   a_w_s=a_w_s, a_b_s=a_b_s, a_w_out=a_w_out, b_w_qkv=b_w_qkv, b_rel_bias=b_rel_bias,
                   b_w_out=b_w_out, ffn_w_gate=ffn_w_gate, ffn_w_up=ffn_w_up, ffn_w_down=ffn_w_down)
    mom_m = dict(norm_mix_g=m_norm_mix_g, norm_ffn_g=m_norm_ffn_g, final_g=m_final_g, a_w_in=m_a_w_in,
                 a_v_gain=m_a_v_gain, a_w_s=m_a_w_s, a_b_s=m_a_b_s, a_w_out=m_a_w_out, b_w_qkv=m_b_w_qkv,
                 b_rel_bias=m_b_rel_bias, b_w_out=m_b_w_out, ffn_w_gate=m_ffn_w_gate, ffn_w_up=m_ffn_w_up,
                 ffn_w_down=m_ffn_w_down)
    mom_v = dict(norm_mix_g=v_norm_mix_g, norm_ffn_g=v_norm_ffn_g, final_g=v_final_g, a_w_in=v_a_w_in,
                 a_v_gain=v_a_v_gain, a_w_s=v_a_w_s, a_b_s=v_a_b_s, a_w_out=v_a_w_out, b_w_qkv=v_b_w_qkv,
                 b_rel_bias=v_b_rel_bias, b_w_out=v_b_w_out, ffn_w_gate=v_ffn_w_gate, ffn_w_up=v_ffn_w_up,
                 ffn_w_down=v_ffn_w_down)
    order = list(weights)
    transposed = ("ffn_w_gate", "ffn_w_up")
    for k in transposed:
        weights[k], mom_m[k], mom_v[k] = (jnp.swapaxes(a, 1, 2) for a in (weights[k], mom_m[k], mom_v[k]))

    xi, yi, ci = _coords()
    me = jnp.stack([ci, 2 * xi + yi]).astype(jnp.int32)

    shard16 = {k: cast_bf16(weights[k], "cast_" + k) for k in BIG}

    def layer_tensors(i):
        mix = ("a_w_in", "a_w_out") if i % 2 == 0 else ("b_w_qkv", "b_w_out")
        return [(k, i // 2) for k in mix] + [(k, i) for k in ("ffn_w_gate", "ffn_w_up", "ffn_w_down")]

    def gather_items(keys):
        return [(shard16[k], l, GATHER_KIND[k]) for k, l in keys]

    def grad_metas(keys):
        return [(GATHER_KIND[k],) + tuple(weights[k].shape[1:]) for k, _ in keys]

    FFN = ("ffn_w_gate", "ffn_w_up", "ffn_w_down")
    k0a = [("a_w_out", 0), ("ffn_w_gate", 0)]
    k0b = [("ffn_w_up", 0), ("ffn_w_down", 0)]
    k1a = [("b_w_qkv", 0), ("b_w_out", 0), ("ffn_w_gate", 1)]
    k1b = [("ffn_w_up", 1), ("ffn_w_down", 1)]
    k3a = [("b_w_qkv", 1), ("b_w_out", 1), ("ffn_w_gate", 3)]
    k3b = [("ffn_w_up", 3), ("ffn_w_down", 3)]
    plans = {
        "a_in_l0": [("g1", k0a)], "sgu_fwd_l0": [("g2", k0a), ("g1", k0b)], "a_out_l0": [("g2", k0b)],
        "rms_mix_l0": [("g1", [("a_w_in", 0)])],
        "ffn_up_l0": [("g1", k1a)], "ffn_down_l0": [("g2", k1a), ("g1", k1b[:1])],
        "b_qkv_l1": [("g2", k1b[:1]), ("g1", k1b[1:])],
        "attn_fwd_l1": [("g2", k1b[1:]), ("g1", layer_tensors(2))], "b_out_l1": [("g2", layer_tensors(2))],
        "ffn_up_l1": [("g1", k3a)], "ffn_down_l1": [("g2", k3a)],
        "a_in_l2": [("g1", k3b)], "sgu_fwd_l2": [("g2", k3b)],
        "ffn_bwd_dh_l2": [("ex", layer_tensors(3))], "sgu_bwd_l2": [("sc", layer_tensors(3))],
        "ffn_bwd_dh_l1": [("ex", layer_tensors(2))], "attn_bwd_l1": [("sc", layer_tensors(2))],
        "ffn_bwd_dh_l0": [("ex", layer_tensors(1))], "ffn_bwd_dhn_l0": [("sc", k1a)],
        "dffn_w_gate_l0": [("sc", [("ffn_w_up", 1)])], "dffn_w_up_l0": [("sc", [("ffn_w_down", 1)])],
        "a_out_bwd_l0": [("ex", [(k, 0) for k in FFN])],
        "sgu_bwd_l0": [("sc", [("ffn_w_gate", 0), ("ffn_w_up", 0)]), ("ex", [("a_w_out", 0)])],
        "dw_in_l0": [("sc", [("ffn_w_down", 0), ("a_w_out", 0)])],
    }
    part16, full16 = {}, {}
    sib, p16, own_parts, recv_parts = {}, {}, {}, {}

    def make_comm(kind, keys):
        if kind == "g1":
            return gather_stage1(gather_items(keys)), lambda outs: part16.update(zip(keys, outs))
        if kind == "g2":
            return (gather_stage2(gather_items(keys), [part16[kl] for kl in keys]),
                    lambda outs: full16.update(zip(keys, outs)))
        if kind == "ex":
            return (exchange_halves([big_grads[k][l] for k, l in keys], grad_metas(keys)),
                    lambda outs: sib.update(zip(keys, outs)))
        for kl, m_ in zip(keys, grad_metas(keys)):
            p16[kl], own_parts[kl] = pair_sum(me, big_grads[kl[0]][kl[1]], sib[kl], m_, "pair_sum_%s_l%d" % kl)
        return (scatter_partials([p16[kl] for kl in keys], grad_metas(keys)),
                lambda outs: recv_parts.update(zip(keys, outs)))

    def run(name, make):
        steps = plans.get(name)
        if not steps:
            return make(None)
        made = [make_comm(kind, keys) for kind, keys in steps]
        main, outs = make(combine([c for c, _ in made]))
        for c, done in made:
            done(outs[:len(c.out_shapes)])
            outs = outs[len(c.out_shapes):]
        return main

    def weight(k, l):
        w = full16[(k, l)]
        if k == "a_w_out":
            return w.reshape(1, GH, D)
        return w.reshape(1, D, D) if k == "b_w_out" else w


    xcur = x.reshape(T, D)
    hn = run("rms_mix_l0", lambda comm: rms_fwd(xcur, norm_mix_g[0][None], "rms_mix_l0", comm=comm))
    comm, done = make_comm("g2", [("a_w_in", 0)])
    done(run_comm(comm, "gather_first_d2d"))
    saved = []
    for i in range(DEPTH):
        j = i // 2
        tag = "_l%d" % i
        st = {"x_in": xcur, "hn": hn}
        if i % 2 == 0:
            pre = run("a_in" + tag, lambda comm: matmul(
                "a_in" + tag, NN, hn, pl.BlockSpec((1024, D), lambda i_, j_: (i_, 0)),
                weight("a_w_in", j), pl.BlockSpec((None, D, 1024), lambda i_, j_: (0, 0, j_)),
                _sds((T, 2 * GH), BF16), pl.BlockSpec((1024, 1024), lambda i_, j_: (i_, j_)),
                (T // 1024, 4), comm=comm))
            y = run("sgu_fwd" + tag, lambda comm: sgu_fwd(
                pre, a_v_gain[j][None], a_w_s[j], a_b_s[j][:, :, None], "sgu_fwd" + tag, comm=comm))
            xmid, hn2 = run("a_out" + tag, lambda comm: residual_proj(
                "a_out" + tag, *out_proj(y, weight("a_w_out", j)), xcur, norm_ffn_g[i][None], comm=comm))
            st.update(pre=pre, y=y)
        else:
            qkvp = run("b_qkv" + tag, lambda comm: proj_qkv(hn, weight("b_w_qkv", j), 0, "b_qkv" + tag, comm=comm))
            wb = jnp.transpose(bias_build(b_rel_bias[j], "bias_build" + tag), (1, 0, 2))
            o = run("attn_fwd" + tag, lambda comm: attn_fwd(qkvp, wb, "attn_fwd" + tag, comm=comm))
            xmid, hn2 = run("b_out" + tag, lambda comm: residual_proj(
                "b_out" + tag, *out_proj(o, weight("b_w_out", j)), xcur, norm_ffn_g[i][None], comm=comm))
            st.update(qkvp=qkvp, wb=wb, o=o)
        g, u, h = run("ffn_up" + tag, lambda comm: ffn_up(
            hn2, weight("ffn_w_gate", i), weight("ffn_w_up", i), 0, "ffn_up" + tag, comm=comm))
        next_g = norm_mix_g[i + 1][None] if i + 1 < DEPTH else None
        down = run("ffn_down" + tag, lambda comm: residual_proj(
            "ffn_down" + tag, *ffn_down(h, weight("ffn_w_down", i)), xmid, next_g, comm=comm))
        xcur, hn = down if next_g is not None else (down, None)
        st.update(x_mid=xmid, hn2=hn2, g=g, u=u, h=h)
        saved.append(st)

    loss_part, dx, dxb, d_final = final_loss(xcur, final_g[None], loss_target.reshape(T, D), "final_loss")

    tk = min(2048, T)
    big_grads = {k: [None] * weights[k].shape[0] for k in BIG}
    small = {"norm_mix_g": [None] * DEPTH, "norm_ffn_g": [None] * DEPTH, "a_v_gain": [None] * 2,
             "a_w_s": [None] * 2, "a_b_s": [None] * 2, "b_rel_bias": [None] * 2}
    tok = lambda width: pl.BlockSpec((tk, width), lambda j_, k_: (k_, 0))
    part = lambda: pl.BlockSpec((None, tk, FS), lambda j_, k_: (j_, k_, 0))
    for i in reversed(range(DEPTH)):
        j = i // 2
        tag = "_l%d" % i
        st = saved[i]
        dg, du = run("ffn_bwd_dh" + tag, lambda comm: ffn_bwd_dh(
            dxb, weight("ffn_w_down", i), st["g"], st["u"], 0, "ffn_bwd_dh" + tag, comm=comm))
        big_grads["ffn_w_down"][i] = wgrad(
            "dw_down" + tag, st["h"], part(), dxb, tok(D), _sds((N_CHIPS, FS, D), F32),
            pl.BlockSpec((None, FS, D), lambda j_, k_: (j_, 0, 0)), N_CHIPS, T, tk)
        dx_mid, dxb_mid, dgn = run("ffn_bwd_dhn" + tag, lambda comm: dgrad_rms(
            "ffn_bwd_dhn" + tag, *ffn_dgrad(dg, du, weight("ffn_w_gate", i), weight("ffn_w_up", i)),
            st["x_mid"], norm_ffn_g[i][None], dx, comm=comm))
        for nm, dz in (("ffn_w_gate", dg), ("ffn_w_up", du)):
            big_grads[nm][i] = run("d" + nm + tag, lambda comm: wgrad(
                "d" + nm + tag, dz, part(), st["hn2"], tok(D), _sds((N_CHIPS, FS, D), F32),
                pl.BlockSpec((None, FS, D), lambda j_, k_: (j_, 0, 0)), N_CHIPS, T, tk, comm=comm))
        dx, dxb = dx_mid, dxb_mid
        small["norm_ffn_g"][i] = dgn
        if i % 2 == 0:
            dy = run("a_out_bwd" + tag, lambda comm: matmul(
                "a_out_bwd" + tag, NT, dxb, pl.BlockSpec((1024, D), lambda i_, j_: (i_, 0)),
                weight("a_w_out", j), pl.BlockSpec((None, 1024, D), lambda i_, j_: (0, j_, 0)),
                _sds((T, GH), BF16), pl.BlockSpec((1024, 1024), lambda i_, j_: (i_, j_)), (T // 1024, 2), comm=comm))
            big_grads["a_w_out"][j] = wgrad(
                "dw_aout" + tag, st["y"], pl.BlockSpec((tk, 1024), lambda j_, k_: (k_, j_)), dxb, tok(D),
                _sds((GH, D), F32), pl.BlockSpec((1024, D), lambda j_, k_: (j_, 0)), 2, T, tk
            ).reshape(N_CHIPS, GH // N_CHIPS, D)
            dpre, d_ws, d_bs, d_gain = run("sgu_bwd" + tag, lambda comm: sgu_bwd(
                st["pre"], dy, a_v_gain[j][None], a_w_s[j], a_b_s[j][:, :, None], "sgu_bwd" + tag,
                tm=2 * SGU_BLOCK, comm=comm))
            small["a_w_s"][j], small["a_b_s"][j], small["a_v_gain"][j] = d_ws, d_bs, d_gain
            dx_in, dxb_in, dgn = run("a_in_bwd" + tag, lambda comm: dgrad_rms(
                "a_in_bwd" + tag, *in_dgrad(dpre, weight("a_w_in", j)),
                st["x_in"], norm_mix_g[i][None], dx, comm=comm))
            big_grads["a_w_in"][j] = run("dw_in" + tag, lambda comm: wgrad(
                "dw_in" + tag, st["hn"], tok(D), dpre, pl.BlockSpec((tk, 1024), lambda j_, k_: (k_, j_)),
                _sds((D, 2 * GH), F32), pl.BlockSpec((D, 1024), lambda j_, k_: (0, j_)), 4, T, tk, comm=comm))
        else:
            do = matmul("b_out_bwd" + tag, NT, dxb, pl.BlockSpec((1024, D), lambda i_, j_: (i_, 0)),
                        weight("b_w_out", j), pl.BlockSpec((None, D, D), lambda i_, j_: (0, 0, 0)),
                        _sds((T, D), BF16), pl.BlockSpec((1024, D), lambda i_, j_: (i_, 0)), (T // 1024, 1))
            big_grads["b_w_out"][j] = wgrad(
                "dw_bout" + tag, st["o"], tok(D), dxb, tok(D),
                _sds((D, D), F32), pl.BlockSpec((D, D), lambda j_, k_: (0, 0)), 1, T, tk
            ).reshape(N_CHIPS, D // N_CHIPS, D)
            dqkvp, dwb = run("attn_bwd" + tag, lambda comm: attn_bwd(
                st["qkvp"], st["o"], do, st["wb"], "attn_bwd" + tag, comm=comm))
            small["b_rel_bias"][j] = bias_grad(
                jnp.pad(jnp.transpose(dwb, (1, 0, 2)), ((0, 0), (0, 0), (0, DIAG - KW))), "bias_grad" + tag)
            dx_in, dxb_in, dgn = dgrad_rms(
                "b_qkv_bwd" + tag, *qkv_dgrad(dqkvp, weight("b_w_qkv", j)),
                st["x_in"], norm_mix_g[i][None], dx)
            big_grads["b_w_qkv"][j] = wgrad(
                "dw_qkv" + tag, st["hn"], tok(D), dqkvp,
                pl.BlockSpec((None, tk, D), lambda j_, k_: (j_, k_ + FRONT // tk, 0)),
                _sds((D, 3 * D), F32), pl.BlockSpec((D, D), lambda j_, k_: (0, j_)), 3, T, tk)
        dx, dxb = dx_in, dxb_in
        small["norm_mix_g"][i] = dgn

    small_grads = {
        "norm_mix_g": jnp.concatenate(small["norm_mix_g"], axis=0),
        "norm_ffn_g": jnp.concatenate(small["norm_ffn_g"], axis=0),
        "final_g": d_final.reshape(D),
        "a_v_gain": jnp.concatenate(small["a_v_gain"], axis=0),
        "a_w_s": jnp.stack(small["a_w_s"]),
        "a_b_s": jnp.stack(small["a_b_s"]).reshape(2, SGU_G, SGU_BLOCK),
        "b_rel_bias": jnp.stack(small["b_rel_bias"]),
    }
    small_names = list(small_grads)
    packed = [_rows128(small_grads[k]) for k in small_names] + [_rows128(loss_part[:, :1])]
    offs = [0]
    for p in packed:
        offs.append(offs[-1] + p.shape[0])
    reduced = allreduce_small(jnp.concatenate(packed, axis=0))
    grads = {}
    for t, k in enumerate(small_names):
        nelem = small_grads[k].size
        grads[k] = reduced[offs[t]:offs[t + 1]].reshape(-1)[:nelem].reshape(weights[k].shape)
    loss = reduced[offs[len(small_names)], 0]

    last = [("a_w_in", 0)]
    for kind, name in (("ex", "exchange_last"), ("sc", "scatter_last")):
        comm, done = make_comm(kind, last)
        done(run_comm(comm, name))
    bufs = {k: jnp.zeros(weights[k].shape, F32) for k in BIG}
    for i in range(DEPTH):
        for kl, m_ in zip(layer_tensors(i), grad_metas(layer_tensors(i))):
            bufs[kl[0]] = final_sum(me, own_parts[kl], recv_parts[kl], bufs[kl[0]], kl[1], m_,
                                    "final_sum_%s_l%d" % kl)
    shared = share_final([bufs[k] for k in BIG])
    for k, gfull in zip(BIG, shared):
        grads[k] = gfull

    delta, new_m, new_v = {}, {}, {}
    for k in order:
        shp = weights[k].shape
        if k in BIG:
            view = shp
        elif k == "a_w_s":
            view = (2, SGU_G * SGU_BLOCK, SGU_BLOCK)
        elif len(shp) == 1:
            view = (1, 1, shp[0])
        elif len(shp) == 2:
            view = (1,) + shp
        else:
            view = shp
        g_, d_, m_, v_ = adamw(weights[k].reshape(view), grads[k].reshape(view), mom_m[k].reshape(view),
                               mom_v[k].reshape(view), "adamw_" + k)
        grads[k], delta[k], new_m[k], new_v[k] = g_.reshape(shp), d_.reshape(shp), m_.reshape(shp), v_.reshape(shp)
    for k in transposed:
        for tree in (grads, delta, new_m, new_v):
            tree[k] = jnp.swapaxes(tree[k], 1, 2)

    return (loss, dx.reshape(1, T, D), *[grads[k] for k in order], *[delta[k] for k in order],
            *[new_m[k] for k in order], *[new_v[k] for k in order])
```

```python
import functools

import jax
import jax.numpy as jnp
from jax import lax
from jax.experimental import pallas as pl
from jax.experimental.pallas import tpu as pltpu

F32 = jnp.float32
BF16 = jnp.bfloat16
MESH = pl.DeviceIdType.MESH

D = 1024
DEPTH = 4
EPS = 1e-6
SGU_BLOCK = 128
GH = 2048
SGU_G = 8
SGU_GD = GH // SGU_G
N_HEADS = 16
HEAD_DIM = 64
CHUNK = 64
PAD = 8 * CHUNK
FRONT = 2048
QB = 128
KW = PAD + QB
N_REL = 192
REL_MIN = -(CHUNK - 1)
REL_MAX = 128
D_FF = 2816
FS = D_FF // 4
NEG = -1e30
SCALE = HEAD_DIM ** -0.5
N_CHIPS = 4

ADAM_LR = 0.001
ADAM_B1 = 0.9
ADAM_B2 = 0.999
ADAM_EPS = 1e-08
ADAM_WD = 0.01
ADAM_STEP = 10

VMEM_BIG = 56 * 1024 * 1024

NN = ((1,), (0,))
NT = ((1,), (1,))
TN = ((0,), (0,))


def _dot(a, b, dims):
    return lax.dot_general(a, b, (dims, ((), ())), preferred_element_type=F32)


class Comm:
    def __init__(self, ins, out_shapes, sems, start, wait, aliases=None):
        self.ins, self.out_shapes, self.sems = list(ins), list(out_shapes), list(sems)
        self.start, self.wait, self.aliases = start, wait, dict(aliases or {})


def _host(body, comm, kw):
    grid = tuple(kw["grid"])
    in_specs = list(kw["in_specs"])
    single = not isinstance(kw["out_specs"], (list, tuple))
    out_specs = [kw["out_specs"]] if single else list(kw["out_specs"])
    out_shape = [kw["out_shape"]] if single else list(kw["out_shape"])
    scratch = list(kw.get("scratch_shapes", ()))
    counts = (len(in_specs), len(comm.ins), len(out_specs), len(comm.out_shapes), len(scratch))

    def hosted(*refs):
        parts, p = [], 0
        for cnt in counts:
            parts.append(refs[p:p + cnt])
            p += cnt
        main_in, c_in, main_out, c_out, main_scr = parts
        sems = refs[p:]
        ids = [pl.program_id(a) for a in range(len(grid))]
        first = functools.reduce(jnp.logical_and, [i == 0 for i in ids])
        last = functools.reduce(jnp.logical_and, [i == n - 1 for i, n in zip(ids, grid)])
        pl.when(first)(lambda: comm.start(c_in, c_out, sems))
        body(*main_in, *main_out, *main_scr)
        pl.when(last)(lambda: comm.wait(c_in, c_out, sems))

    old = kw["compiler_params"]
    kw = dict(kw, in_specs=in_specs + [ANY] * len(comm.ins), out_specs=out_specs + [ANY] * len(comm.out_shapes),
              out_shape=out_shape + comm.out_shapes, scratch_shapes=scratch + comm.sems,
              compiler_params=pltpu.CompilerParams(dimension_semantics=("arbitrary",) * len(grid),
                                                   vmem_limit_bytes=old.vmem_limit_bytes, has_side_effects=True))
    if comm.aliases:
        kw["input_output_aliases"] = {counts[0] + i: counts[2] + o for i, o in comm.aliases.items()}
    return hosted, kw


def _pallas(body, comm=None, **kw):
    if comm is not None:
        body, kw = _host(body, comm, kw)
    return pl.pallas_call(body, **kw)


def _split_outs(outs, comm, n_main):
    outs = list(outs) if isinstance(outs, (list, tuple)) else [outs]
    main = outs[:n_main]
    return (main[0] if n_main == 1 else main), outs[n_main:]


def run_comm(comm, name):
    nci, nco = len(comm.ins), len(comm.out_shapes)

    def body(*refs):
        c_in, c_out, sems = refs[:nci], refs[nci:nci + nco], refs[nci + nco:]
        comm.start(c_in, c_out, sems)
        comm.wait(c_in, c_out, sems)

    kw = {}
    if comm.aliases:
        kw["input_output_aliases"] = dict(comm.aliases)
    return _pallas(body, name=name, in_specs=[ANY] * nci, out_specs=[ANY] * nco, out_shape=comm.out_shapes,
                   scratch_shapes=comm.sems, compiler_params=pltpu.CompilerParams(has_side_effects=True),
                   **kw)(*comm.ins)


def combine(comms):
    if len(comms) == 1:
        return comms[0]
    spans, ni, no, ns = [], 0, 0, 0
    for c in comms:
        spans.append((slice(ni, ni + len(c.ins)), slice(no, no + len(c.out_shapes)), slice(ns, ns + len(c.sems))))
        ni, no, ns = ni + len(c.ins), no + len(c.out_shapes), ns + len(c.sems)

    def start(ins, outs, sems):
        for c, (si, so, ss) in zip(comms, spans):
            c.start(ins[si], outs[so], sems[ss])

    def wait(ins, outs, sems):
        for c, (si, so, ss) in zip(comms, spans):
            c.wait(ins[si], outs[so], sems[ss])

    aliases = {}
    for c, (si, so, _) in zip(comms, spans):
        aliases.update({si.start + i: so.start + o for i, o in c.aliases.items()})
    return Comm([a for c in comms for a in c.ins], [o for c in comms for o in c.out_shapes],
                [s for c in comms for s in c.sems], start, wait, aliases)


def _call(body, comm, n_main, args, **kw):
    if comm is None:
        return _pallas(body, **kw)(*args)
    return _split_outs(_pallas(body, comm=comm, **kw)(*args, *comm.ins), comm, n_main)


def _params(sem=None, vmem=None):
    return pltpu.CompilerParams(dimension_semantics=sem, vmem_limit_bytes=vmem)


def _sds(shape, dtype):
    return jax.ShapeDtypeStruct(tuple(shape), dtype)


_GELU_C = 0.7978845608028654


_GELU_A = _GELU_C * 0.044715


def _gelu(x):
    t = jnp.tanh(x * (_GELU_C + _GELU_A * (x * x)))
    h = 0.5 * x
    return h + h * t


def _gelu_and_grad(x):
    x2 = x * x
    t = jnp.tanh(x * (_GELU_C + _GELU_A * x2))
    h = 0.5 * x
    val = h + h * t
    grad = (0.5 + 0.5 * t) + (h * (1.0 - t * t)) * (_GELU_C + (3.0 * _GELU_A) * x2)
    return val, grad


def _sigmoid(x):
    return 0.5 * (jnp.tanh(0.5 * x) + 1.0)


def cast_bf16(w, name):
    L, R, C = w.shape

    def body(w_ref, o_ref):
        o_ref[...] = w_ref[...].astype(BF16)

    spec = pl.BlockSpec((None, R, C), lambda l: (l, 0, 0))
    return _pallas(body, name=name, grid=(L,), in_specs=[spec], out_specs=spec,
                   out_shape=_sds((L, R, C), BF16), compiler_params=_params(("parallel",)))(w)


def rms_fwd(x, g, name, tm=512, comm=None):
    T = x.shape[0]

    def body(x_ref, g_ref, o_ref):
        o_ref[...] = _rms_rows(x_ref[...], g_ref[...])

    row = pl.BlockSpec((tm, D), lambda i: (i, 0))
    return _call(body, comm, 1, (x, g), name=name, grid=(T // tm,),
                 in_specs=[row, pl.BlockSpec((1, D), lambda i: (0, 0))], out_specs=row,
                 out_shape=_sds((T, D), BF16), compiler_params=_params(("parallel",)))


def dgrad_rms(name, compute, args, specs, x, g, dres, tm=512, comm=None):
    T = x.shape[0]
    n = T // tm
    k = len(args)

    def body(*refs):
        x_ref, g_ref, dres_ref, dx_ref, dxb_ref, dg_ref, acc_ref = refs[k:]
        i = pl.program_id(0)
        xf = x_ref[...]
        r = lax.rsqrt(jnp.mean(xf * xf, axis=-1, keepdims=True) + EPS)
        xhat = xf * r
        dhf = compute(*refs[:k])
        part = (dhf * xhat).reshape(tm // 8, 8, D).sum(axis=0)

        @pl.when(i == 0)
        def _():
            acc_ref[...] = part

        @pl.when(i > 0)
        def _():
            acc_ref[...] += part

        dxhat = dhf * g_ref[...]
        dx = dres_ref[...] + r * (dxhat - xhat * jnp.mean(dxhat * xhat, axis=-1, keepdims=True))
        dx_ref[...] = dx
        dxb_ref[...] = dx.astype(BF16)

        @pl.when(i == n - 1)
        def _():
            dg_ref[...] = jnp.sum(acc_ref[...], axis=0, keepdims=True)

    row = pl.BlockSpec((tm, D), lambda i: (i, 0))
    vec = pl.BlockSpec((1, D), lambda i: (0, 0))
    return _call(body, comm, 3, (*args, x, g, dres), name=name, grid=(n,),
                 in_specs=list(specs) + [row, vec, row], out_specs=[row, row, vec],
                 out_shape=[_sds((T, D), F32), _sds((T, D), BF16), _sds((1, D), F32)],
                 scratch_shapes=[pltpu.VMEM((8, D), F32)],
                 compiler_params=_params(("arbitrary",), VMEM_BIG))


def final_loss(x, g, tgt, name, tm=256):
    T = x.shape[0]
    n = T // tm

    def body(x_ref, g_ref, t_ref, loss_ref, dx_ref, dxb_ref, dg_ref, acc_ref, lacc_ref):
        i = pl.program_id(0)
        xf = x_ref[...]
        r = lax.rsqrt(jnp.mean(xf * xf, axis=-1, keepdims=True) + EPS)
        xhat = xf * r
        gg = g_ref[...]
        e = xhat * gg - t_ref[...]
        dy = e * (1.0 / D)
        part = (dy * xhat).reshape(tm // 8, 8, D).sum(axis=0)
        lpart = (e * e).reshape(tm // 8, 8, D).sum(axis=0)

        @pl.when(i == 0)
        def _():
            acc_ref[...] = part
            lacc_ref[...] = lpart

        @pl.when(i > 0)
        def _():
            acc_ref[...] += part
            lacc_ref[...] += lpart

        dxhat = dy * gg
        dx = r * (dxhat - xhat * jnp.mean(dxhat * xhat, axis=-1, keepdims=True))
        dx_ref[...] = dx
        dxb_ref[...] = dx.astype(BF16)

        @pl.when(i == n - 1)
        def _():
            dg_ref[...] = jnp.sum(acc_ref[...], axis=0, keepdims=True)
            total = jnp.sum(jnp.sum(lacc_ref[...], axis=0, keepdims=True), axis=1, keepdims=True)
            loss_ref[...] = jnp.broadcast_to(total * (0.5 / D), (1, 128))

    row = pl.BlockSpec((tm, D), lambda i: (i, 0))
    vec = pl.BlockSpec((1, D), lambda i: (0, 0))
    return _pallas(body, name=name, grid=(n,), in_specs=[row, vec, row],
                   out_specs=[pl.BlockSpec((1, 128), lambda i: (0, 0)), row, row, vec],
                   out_shape=[_sds((1, 128), F32), _sds((T, D), F32), _sds((T, D), BF16), _sds((1, D), F32)],
                   scratch_shapes=[pltpu.VMEM((8, D), F32), pltpu.VMEM((8, D), F32)],
                   compiler_params=_params(("arbitrary",)))(x, g, tgt)


def matmul(name, dims, a, a_spec, b, b_spec, out_shape, out_spec, grid, *, acc=False, res=None, res_spec=None,
           comm=None):
    has_res = res is not None

    def body(*refs):
        a_ref, b_ref = refs[0], refs[1]
        r_ref = refs[2] if has_res else None
        o_ref = refs[-1]
        d = _dot(a_ref[...], b_ref[...], dims)
        if not acc:
            if has_res:
                d = d + r_ref[...]
            o_ref[...] = d.astype(o_ref.dtype)
        else:
            k = pl.program_id(len(grid) - 1)

            @pl.when(k == 0)
            def _():
                o_ref[...] = (d + r_ref[...]) if has_res else d

            @pl.when(k > 0)
            def _():
                o_ref[...] += d

    sem = ("parallel",) * (len(grid) - 1) + (("arbitrary",) if acc else ("parallel",))
    ins = [a, b] + ([res] if has_res else [])
    specs = [a_spec, b_spec] + ([res_spec] if has_res else [])
    return _call(body, comm, 1, ins, name=name, grid=grid, in_specs=specs, out_specs=out_spec, out_shape=out_shape,
                 compiler_params=_params(sem, VMEM_BIG))


def wgrad(name, a, a_spec, b, b_spec, out_shape, out_spec, J, T, tk, comm=None):
    return matmul(name, TN, a, a_spec, b, b_spec, out_shape, out_spec, (J, T // tk), acc=True, comm=comm)


def _sgu_mask():
    p = lax.broadcasted_iota(jnp.int32, (SGU_BLOCK, SGU_BLOCK), 0)
    q = lax.broadcasted_iota(jnp.int32, (SGU_BLOCK, SGU_BLOCK), 1)
    return lax.shift_right_logical(q, 6) <= lax.shift_right_logical(p, 6)


def sgu_fwd(pre, gain, w_s, b_s, name, comm=None):
    T = pre.shape[0]

    def body(pre_ref, gain_ref, ws_ref, bs_ref, y_ref):
        mask = _sgu_mask()
        u = _gelu(pre_ref[:, :GH].astype(F32))
        va = _gelu(pre_ref[:, GH:].astype(F32))
        r = lax.rsqrt(jnp.mean(va * va, axis=-1, keepdims=True) + EPS)
        vn = ((va * r) * gain_ref[...]).astype(BF16)
        for g in range(SGU_G):
            sl = slice(g * SGU_GD, (g + 1) * SGU_GD)
            wm = jnp.where(mask, ws_ref[g], 0.0).astype(BF16)
            vm = _dot(wm, vn[:, sl], NN) + bs_ref[g]
            y_ref[:, sl] = (u[:, sl] * vm).astype(BF16)

    return _call(
        body, comm, 1, (pre, gain, w_s, b_s), name=name, grid=(T // SGU_BLOCK,),
        in_specs=[pl.BlockSpec((SGU_BLOCK, 2 * GH), lambda i: (i, 0)),
                  pl.BlockSpec((1, GH), lambda i: (0, 0)),
                  pl.BlockSpec((SGU_G, SGU_BLOCK, SGU_BLOCK), lambda i: (0, 0, 0)),
                  pl.BlockSpec((SGU_G, SGU_BLOCK, 1), lambda i: (0, 0, 0))],
        out_specs=pl.BlockSpec((SGU_BLOCK, GH), lambda i: (i, 0)),
        out_shape=_sds((T, GH), BF16), compiler_params=_params(("parallel",)))


def sgu_bwd(pre, dy, gain, w_s, b_s, name, tm=SGU_BLOCK, comm=None):
    T = pre.shape[0]
    n = T // tm

    def body(pre_ref, dy_ref, gain_ref, ws_ref, bs_ref, dpre_ref, dws_ref, dbs_ref, dgain_ref, gacc_ref):
        i = pl.program_id(0)

        @pl.when(i == 0)
        def _():
            dws_ref[...] = jnp.zeros_like(dws_ref)
            dbs_ref[...] = jnp.zeros_like(dbs_ref)
            gacc_ref[...] = jnp.zeros_like(gacc_ref)

        mask = _sgu_mask()
        gain_v = gain_ref[...]
        for sb in range(tm // SGU_BLOCK):
            rows = slice(sb * SGU_BLOCK, (sb + 1) * SGU_BLOCK)
            u, du_dpre = _gelu_and_grad(pre_ref[rows, :GH].astype(F32))
            va, dva_dpre = _gelu_and_grad(pre_ref[rows, GH:].astype(F32))
            r = lax.rsqrt(jnp.mean(va * va, axis=-1, keepdims=True) + EPS)
            vhat = va * r
            vn = (vhat * gain_v).astype(BF16)
            dyf = dy_ref[rows, :].astype(F32)
            dvn_parts = []
            for grp in range(SGU_G):
                sl = slice(grp * SGU_GD, (grp + 1) * SGU_GD)
                wm = jnp.where(mask, ws_ref[grp], 0.0).astype(BF16)
                vm = _dot(wm, vn[:, sl], NN) + bs_ref[grp]
                dpre_ref[rows, sl] = ((dyf[:, sl] * vm) * du_dpre[:, sl]).astype(BF16)
                dvm = dyf[:, sl] * u[:, sl]
                dbs_ref[grp] += jnp.sum(dvm, axis=-1, keepdims=True)
                dvm16 = dvm.astype(BF16)
                dws_ref[grp] += jnp.where(mask, _dot(dvm16, vn[:, sl], NT), 0.0)
                dvn_parts.append(_dot(wm, dvm16, TN))
            dvn = jnp.concatenate(dvn_parts, axis=-1)
            gacc_ref[...] += (dvn * vhat).reshape(SGU_BLOCK // 8, 8, GH).sum(axis=0)
            dvhat = dvn * gain_v
            dva = r * (dvhat - vhat * jnp.mean(dvhat * vhat, axis=-1, keepdims=True))
            dpre_ref[rows, GH:] = (dva * dva_dpre).astype(BF16)

        @pl.when(i == n - 1)
        def _():
            dgain_ref[...] = jnp.sum(gacc_ref[...], axis=0, keepdims=True)

    const3 = lambda i: (0, 0, 0)
    return _call(
        body, comm, 4, (pre, dy, gain, w_s, b_s), name=name, grid=(n,),
        in_specs=[pl.BlockSpec((tm, 2 * GH), lambda i: (i, 0)),
                  pl.BlockSpec((tm, GH), lambda i: (i, 0)),
                  pl.BlockSpec((1, GH), lambda i: (0, 0)),
                  pl.BlockSpec((SGU_G, SGU_BLOCK, SGU_BLOCK), const3),
                  pl.BlockSpec((SGU_G, SGU_BLOCK, 1), const3)],
        out_specs=[pl.BlockSpec((tm, 2 * GH), lambda i: (i, 0)),
                   pl.BlockSpec((SGU_G, SGU_BLOCK, SGU_BLOCK), const3),
                   pl.BlockSpec((SGU_G, SGU_BLOCK, 1), const3),
                   pl.BlockSpec((1, GH), lambda i: (0, 0))],
        out_shape=[_sds((T, 2 * GH), BF16), _sds((SGU_G, SGU_BLOCK, SGU_BLOCK), F32),
                   _sds((SGU_G, SGU_BLOCK, 1), F32), _sds((1, GH), F32)],
        scratch_shapes=[pltpu.VMEM((8, GH), F32)],
        compiler_params=_params(("arbitrary",)))


DIAG = 768


def _diag_onehot():
    n = lax.broadcasted_iota(jnp.int32, (N_REL, DIAG), 1)
    r = lax.broadcasted_iota(jnp.int32, (N_REL, DIAG), 0)
    idx = jnp.clip(KW - 1 - n, REL_MIN, REL_MAX) - REL_MIN
    return (idx == r).astype(BF16)


def _split3(v):
    hi = v.astype(BF16)
    r1 = v - hi.astype(F32)
    mid = r1.astype(BF16)
    lo = (r1 - mid.astype(F32)).astype(BF16)
    return hi, mid, lo


def bias_build(rel_bias, name):
    def body(rb_ref, o_ref):
        oh = _diag_onehot()
        hi, mid, lo = _split3(rb_ref[...])
        u = (_dot(hi, oh, NN) + _dot(mid, oh, NN) + _dot(lo, oh, NN)) * LOG2E
        j = lax.broadcasted_iota(jnp.int32, (1, KW), 1)

        def row(i, carry):
            val = pltpu.roll(u, (i + (DIAG - QB + 1)) % DIAG, 1)[:, :KW]
            rel = lax.shift_right_logical(i, 6) - lax.shift_right_logical(j, 6) + 8
            ok = (rel >= 0) & (rel <= 8)
            o_ref[i] = jnp.where(ok, val, NEG)
            return carry

        lax.fori_loop(0, QB, row, 0)

    return _pallas(body, name=name, out_shape=_sds((QB, N_HEADS, KW), F32),
                   in_specs=[pl.BlockSpec(memory_space=pltpu.VMEM)],
                   out_specs=pl.BlockSpec(memory_space=pltpu.VMEM))(rel_bias)


def bias_grad(dwb, name):
    def body(d_ref, o_ref):
        def row(i, acc):
            return acc + pltpu.roll(d_ref[i], QB - 1 - i, 1)

        du = lax.fori_loop(0, QB, row, jnp.zeros((N_HEADS, DIAG), F32))
        oh = _diag_onehot()
        hi, mid, lo = _split3(du)
        o_ref[...] = _dot(hi, oh, NT) + _dot(mid, oh, NT) + _dot(lo, oh, NT)

    return _pallas(body, name=name, out_shape=_sds((N_HEADS, N_REL), F32),
                   in_specs=[pl.BlockSpec(memory_space=pltpu.VMEM)],
                   out_specs=pl.BlockSpec(memory_space=pltpu.VMEM))(dwb)


LOG2E = 1.4426950408889634
Q_SCALE = SCALE * LOG2E


def _attn_block(qkv_ref, blk, masked):
    r0 = pl.multiple_of(blk * QB, QB)
    qs = qkv_ref[0, pl.ds(r0 + FRONT, QB), :]
    k2 = qkv_ref[1, pl.ds(r0 + (FRONT - PAD), KW), :]
    v2 = qkv_ref[2, pl.ds(r0 + (FRONT - PAD), KW), :]
    kvalid = (lax.broadcasted_iota(jnp.int32, (1, KW), 1) >= PAD - blk * QB) if masked else None
    return r0, qs, k2, v2, kvalid


def _head_mask(h):
    lane = lax.broadcasted_iota(jnp.int32, (1, 2 * HEAD_DIM), 1)
    return (lane < HEAD_DIM) if h == 0 else (lane >= HEAD_DIM)


def _stack_heads(a):
    zero = jnp.zeros_like(a)
    return jnp.concatenate([jnp.where(_head_mask(0), a, zero), jnp.where(_head_mask(1), a, zero)], axis=0)


def _rows_by_head(a):
    return jnp.concatenate([a[:, :KW], a[:, KW:]], axis=0)


def _per_head(lo, hi):
    return jnp.where(_head_mask(0), lo, hi)


def _attn_exp(qs, kst, w_ref, kvalid):
    s = _dot(qs, kst, NT) + jnp.concatenate([w_ref[0], w_ref[1]], axis=1)
    if kvalid is not None:
        s = jnp.where(jnp.concatenate([kvalid, kvalid], axis=1), s, NEG)
    es, invs = [], []
    for h in range(2):
        sh = s[:, h * KW:(h + 1) * KW]
        eh = jnp.exp2(sh - jnp.max(sh, axis=-1, keepdims=True))
        es.append(eh)
        invs.append(1.0 / jnp.sum(eh, axis=-1, keepdims=True))
    return jnp.concatenate(es, axis=1), invs


ATTN_G = 4
ATTN_STEP = QB * ATTN_G


def _masked_and_not(b, fn):
    n_masked = -(-PAD // ATTN_STEP)
    pl.when(b < n_masked)(functools.partial(fn, True))
    pl.when(b >= n_masked)(functools.partial(fn, False))


def attn_fwd(qkvp, wb, name, comm=None):
    T = qkvp.shape[1] - FRONT

    def body(qkv_ref, w_ref, o_ref):
        b = pl.program_id(1)

        def blocks(masked):
            for t in range(ATTN_G):
                _, qs, k2, v2, kvalid = _attn_block(qkv_ref, b * ATTN_G + t, masked)
                e, inv = _attn_exp(qs, _stack_heads(k2), w_ref, kvalid)
                o = _dot(e.astype(BF16), _stack_heads(v2), NN) * _per_head(*inv)
                o_ref[t * QB:(t + 1) * QB, :] = o.astype(BF16)

        _masked_and_not(b, blocks)

    return _call(
        body, comm, 1, (qkvp, wb), name=name, grid=(N_HEADS // 2, T // (QB * ATTN_G)),
        in_specs=[pl.BlockSpec((3, FRONT + T, 2 * HEAD_DIM), lambda hp, b: (0, 0, hp)),
                  pl.BlockSpec((2, QB, KW), lambda hp, b: (hp, 0, 0))],
        out_specs=pl.BlockSpec((QB * ATTN_G, 2 * HEAD_DIM), lambda hp, b: (b, hp)),
        out_shape=_sds((T, D), BF16),
        compiler_params=_params(("parallel", "arbitrary"), VMEM_BIG))


def attn_bwd(qkvp, o, do, wb, name, comm=None):
    T = qkvp.shape[1] - FRONT
    nb = T // (QB * ATTN_G)

    def body(qkv_ref, o_ref, do_ref, w_ref, dqkv_ref, dw_ref, dk_acc, dv_acc):
        b = pl.program_id(1)

        @pl.when(b == 0)
        def _():
            dk_acc[...] = jnp.zeros_like(dk_acc)
            dv_acc[...] = jnp.zeros_like(dv_acc)
            dw_ref[...] = jnp.zeros_like(dw_ref)
            dqkv_ref[0, 0:FRONT, :] = jnp.zeros((FRONT, 2 * HEAD_DIM), BF16)

        def blocks(masked):
            dws = None
            for t in range(ATTN_G):
                r0, qs, k2, v2, kvalid = _attn_block(qkv_ref, b * ATTN_G + t, masked)
                kst = _stack_heads(k2)
                e, inv = _attn_exp(qs, kst, w_ref, kvalid)
                do2 = do_ref[t * QB:(t + 1) * QB, :]
                dof = do2.astype(F32)
                prod = dof * o_ref[t * QB:(t + 1) * QB, :].astype(F32)
                dp = _dot(do2, _stack_heads(v2), NT)
                parts = []
                for h in range(2):
                    delta = jnp.sum(jnp.where(_head_mask(h), prod, 0.0), axis=-1, keepdims=True)
                    half = slice(h * KW, (h + 1) * KW)
                    parts.append(e[:, half] * ((dp[:, half] - delta) * inv[h]))
                ds = jnp.concatenate(parts, axis=1)
                dws = ds if dws is None else dws + ds
                ds16 = ds.astype(BF16)
                dqkv_ref[0, pl.ds(r0 + FRONT, QB), :] = (_dot(ds16, kst, NN) * SCALE).astype(BF16)
                dk_acc[pl.ds(r0 + (FRONT - PAD), KW), :] += _dot(_rows_by_head(ds16), _stack_heads(qs), TN)
                dv_acc[pl.ds(r0 + (FRONT - PAD), KW), :] += _dot(
                    _rows_by_head(e.astype(BF16)), _stack_heads((dof * _per_head(*inv)).astype(BF16)), TN)
            dw_ref[0] += dws[:, :KW]
            dw_ref[1] += dws[:, KW:]

        _masked_and_not(b, blocks)

        @pl.when(b == nb - 1)
        def _():
            dqkv_ref[1] = (dk_acc[...] * (1.0 / LOG2E)).astype(BF16)
            dqkv_ref[2] = dv_acc[...].astype(BF16)

    slab = pl.BlockSpec((3, FRONT + T, 2 * HEAD_DIM), lambda hp, b: (0, 0, hp))
    wspec = pl.BlockSpec((2, QB, KW), lambda hp, b: (hp, 0, 0))
    rows = pl.BlockSpec((QB * ATTN_G, 2 * HEAD_DIM), lambda hp, b: (b, hp))
    return _call(
        body, comm, 2, (qkvp, o, do, wb), name=name, grid=(N_HEADS // 2, nb),
        in_specs=[slab, rows, rows, wspec],
        out_specs=[slab, wspec],
        out_shape=[_sds((3, FRONT + T, D), BF16), _sds((N_HEADS, QB, KW), F32)],
        scratch_shapes=[pltpu.VMEM((FRONT + T, 2 * HEAD_DIM), F32), pltpu.VMEM((FRONT + T, 2 * HEAD_DIM), F32)],
        compiler_params=_params(("parallel", "arbitrary"), VMEM_BIG))


def proj_qkv(hn, w, l, name, tm=512, comm=None):
    T = hn.shape[0]
    pb = FRONT // tm

    def body(a_ref, b_ref, o_ref):
        i = pl.program_id(1)

        @pl.when(i < pb)
        def _():
            o_ref[...] = jnp.zeros_like(o_ref)

        @pl.when(i >= pb)
        def _():
            scale = jnp.where(pl.program_id(0) == 0, Q_SCALE, 1.0).astype(F32)
            o_ref[...] = (_dot(a_ref[...], b_ref[...], NN) * scale).astype(BF16)

    return _call(
        body, comm, 1, (hn, w), name=name, grid=(3, pb + T // tm),
        in_specs=[pl.BlockSpec((tm, D), lambda p, i: (jnp.maximum(i - pb, 0), 0)),
                  pl.BlockSpec((None, D, D), lambda p, i: (l, 0, p))],
        out_specs=pl.BlockSpec((None, tm, D), lambda p, i: (p, i, 0)),
        out_shape=_sds((3, FRONT + T, D), BF16),
        compiler_params=_params(("parallel", "parallel"), VMEM_BIG))


def ffn_up(hn, wg, wu, l, name, tm=1024, comm=None):
    T = hn.shape[0]

    def body(a_ref, wg_ref, wu_ref, g_ref, u_ref, h_ref):
        a = a_ref[...]
        g = _dot(a, wg_ref[...], NT)
        u = _dot(a, wu_ref[...], NT)
        s = _sigmoid(g)
        silu = g * s
        g_ref[...] = (u * (s * (1.0 + g * (1.0 - s)))).astype(BF16)
        u_ref[...] = silu.astype(BF16)
        h_ref[...] = (silu * u).astype(BF16)

    wspec = pl.BlockSpec((None, None, FS, D), lambda s, i: (l, s, 0, 0))
    ospec = pl.BlockSpec((None, tm, FS), lambda s, i: (s, i, 0))
    return _call(
        body, comm, 3, (hn, wg, wu), name=name, grid=(N_CHIPS, T // tm),
        in_specs=[pl.BlockSpec((tm, D), lambda s, i: (i, 0)), wspec, wspec],
        out_specs=[ospec, ospec, ospec],
        out_shape=[_sds((N_CHIPS, T, FS), BF16)] * 3,
        compiler_params=_params(("parallel", "parallel"), VMEM_BIG))


def ffn_bwd_dh(dxb, wd, g, u, l, name, tm=2048, comm=None):
    T = dxb.shape[0]
    tm = min(tm, T)

    def body(a_ref, wd_ref, g_ref, u_ref, dg_ref, du_ref):
        dh = _dot(a_ref[...], wd_ref[...], NT)
        dg_ref[...] = (dh * g_ref[...].astype(F32)).astype(BF16)
        du_ref[...] = (dh * u_ref[...].astype(F32)).astype(BF16)

    aspec = pl.BlockSpec((None, tm, FS), lambda i, s: (s, i, 0))
    return _call(
        body, comm, 2, (dxb, wd, g, u), name=name, grid=(T // tm, N_CHIPS),
        in_specs=[pl.BlockSpec((tm, D), lambda i, s: (i, 0)),
                  pl.BlockSpec((None, None, FS, D), lambda i, s: (l, s, 0, 0)), aspec, aspec],
        out_specs=[aspec, aspec],
        out_shape=[_sds((N_CHIPS, T, FS), BF16)] * 2,
        compiler_params=_params(("parallel", "parallel"), VMEM_BIG))


def ffn_dgrad(dg, du, wg, wu, tm=512):
    def compute(dg_ref, du_ref, wg_ref, wu_ref):
        d = None
        for s in range(N_CHIPS):
            t = _dot(dg_ref[s], wg_ref[s], NN) + _dot(du_ref[s], wu_ref[s], NN)
            d = t if d is None else d + t
        return d

    aspec = pl.BlockSpec((N_CHIPS, tm, FS), lambda i: (0, i, 0))
    wspec = pl.BlockSpec((None, N_CHIPS, FS, D), lambda i: (0, 0, 0, 0), pipeline_mode=pl.Buffered(1))
    return compute, (dg, du, wg, wu), [aspec, aspec, wspec, wspec]


def qkv_dgrad(dqkvp, w, tm=512):
    def compute(a_ref, w_ref):
        d = None
        for p in range(3):
            t = _dot(a_ref[p], w_ref[:, p * D:(p + 1) * D], NT)
            d = t if d is None else d + t
        return d

    return compute, (dqkvp, w), [pl.BlockSpec((3, tm, D), lambda i: (0, i + FRONT // tm, 0)),
                                 pl.BlockSpec((None, D, 3 * D), lambda i: (0, 0, 0))]


def in_dgrad(dpre, w, tm=512):
    def compute(a_ref, w_ref):
        return _dot(a_ref[...], w_ref[...], NT)

    return compute, (dpre, w), [pl.BlockSpec((tm, 2 * GH), lambda i: (i, 0)),
                                pl.BlockSpec((None, D, 2 * GH), lambda i: (0, 0, 0))]


def _rms_rows(x, g):
    r = lax.rsqrt(jnp.mean(x * x, axis=-1, keepdims=True) + EPS)
    return ((x * r) * g).astype(BF16)


def residual_proj(name, compute, args, specs, res, norm_g, tm=512, comm=None):
    T = res.shape[0]
    k = len(args)
    with_norm = norm_g is not None

    def body(*refs):
        d = refs[k][...] + compute(*refs[:k])
        if with_norm:
            refs[k + 2][...] = d
            refs[k + 3][...] = _rms_rows(d, refs[k + 1][...])
        else:
            refs[k + 1][...] = d

    row = pl.BlockSpec((tm, D), lambda i: (i, 0))
    vec = pl.BlockSpec((1, D), lambda i: (0, 0))
    if with_norm:
        return _call(body, comm, 2, (*args, res, norm_g), name=name, grid=(T // tm,),
                     in_specs=list(specs) + [row, vec], out_specs=[row, row],
                     out_shape=[_sds((T, D), F32), _sds((T, D), BF16)],
                     compiler_params=_params(("parallel",), VMEM_BIG))
    return _call(body, comm, 1, (*args, res), name=name, grid=(T // tm,), in_specs=list(specs) + [row],
                 out_specs=row, out_shape=_sds((T, D), F32), compiler_params=_params(("parallel",), VMEM_BIG))


def ffn_down(h, wd, tm=512):
    def compute(h_ref, wd_ref):
        d = None
        for s in range(N_CHIPS):
            t = _dot(h_ref[s], wd_ref[s], NN)
            d = t if d is None else d + t
        return d

    return compute, (h, wd), [pl.BlockSpec((N_CHIPS, tm, FS), lambda i: (0, i, 0)),
                              pl.BlockSpec((None, N_CHIPS, FS, D), lambda i: (0, 0, 0, 0))]


def out_proj(a, w, tm=512):
    K = a.shape[1]

    def compute(a_ref, w_ref):
        return _dot(a_ref[...], w_ref[...], NN)

    return compute, (a, w), [pl.BlockSpec((tm, K), lambda i: (i, 0)), pl.BlockSpec((None, K, D), lambda i: (0, 0, 0))]


def adamw(w, g, m, v, name):
    L, R, C = w.shape

    def body(w_ref, g_ref, m_ref, v_ref, go_ref, d_ref, nm_ref, nv_ref):
        gf = g_ref[...]
        go_ref[...] = gf
        nm = ADAM_B1 * m_ref[...] + (1.0 - ADAM_B1) * gf
        nv = ADAM_B2 * v_ref[...] + (1.0 - ADAM_B2) * (gf * gf)
        m_hat = nm / (1.0 - ADAM_B1 ** ADAM_STEP)
        v_hat = nv / (1.0 - ADAM_B2 ** ADAM_STEP)
        d_ref[...] = -ADAM_LR * (m_hat / (jnp.sqrt(v_hat) + ADAM_EPS) + ADAM_WD * w_ref[...])
        nm_ref[...] = nm
        nv_ref[...] = nv

    tr = R // 4 if R % 32 == 0 else R
    spec = pl.BlockSpec((None, tr, C), lambda l, r: (l, r, 0))
    return _pallas(body, name=name, grid=(L, R // tr), in_specs=[spec] * 4, out_specs=[spec] * 4,
                   out_shape=[_sds((L, R, C), F32)] * 4,
                   compiler_params=_params(("parallel", "parallel")))(w, g, m, v)


def _coords():
    return lax.axis_index("x"), lax.axis_index("y"), lax.axis_index("c")


def _other_chips(x, y):
    out = []
    for fx, fy in ((1, 0), (0, 1), (1, 1)):
        px = (1 - x) if fx else x
        py = (1 - y) if fy else y
        out.append((px, py))
    return out


def _flip_index(s, j):
    sx, sy = s // 2, s % 2
    fx, fy = ((1, 0), (0, 1), (1, 1))[j]
    return 2 * (sx ^ fx) + (sy ^ fy)


def _for_my_chip(sme, fn):
    for s in range(N_CHIPS):
        pl.when(sme == s)(functools.partial(fn, s))


ANY = pl.BlockSpec(memory_space=pl.ANY)

GATHER_KIND = {"a_w_in": "col", "b_w_qkv": "col", "a_w_out": "row", "b_w_out": "row",
               "ffn_w_gate": "row", "ffn_w_up": "row", "ffn_w_down": "row"}
BIG = tuple(GATHER_KIND)


def _gathered_shape(kind, shape):
    L, R, C = shape
    return (L, R, N_CHIPS * C) if kind == "col" else (L, N_CHIPS, R, C)


def _shard_rows(ref, kind, s, r0, rn, C):
    if kind == "col":
        return ref.at[:, pl.ds(r0, rn), s * C:(s + 1) * C]
    return ref.at[:, s, pl.ds(r0, rn), :]


def gather_stage1(items):
    n = len(items)
    dims = [it[0].shape[1:] for it in items]

    def copies(ins, outs, sems, s, with_landed=True):
        lsem, ssem, rsem = sems
        x, y, c = _coords()
        chips = _other_chips(x, y)
        local, send, landed = [], [], []
        for t, (_, li, kind) in enumerate(items):
            R, C = dims[t]
            r0 = pl.multiple_of(c * (R // 2), 8)
            local.append(pltpu.make_async_copy(ins[t].at[pl.ds(li, 1)], _shard_rows(outs[t], kind, s, 0, R, C),
                                               lsem.at[t]))
            for j in range(3):
                pair = dict(send_sem=ssem.at[3 * t + j], recv_sem=rsem.at[3 * t + j],
                            device_id=(chips[j][0], chips[j][1], c), device_id_type=MESH)
                send.append(pltpu.make_async_remote_copy(
                    src_ref=ins[t].at[pl.ds(li, 1), pl.ds(r0, R // 2), :],
                    dst_ref=_shard_rows(outs[t], kind, s, r0, R // 2, C), **pair))
                if with_landed:
                    got = _shard_rows(outs[t], kind, _flip_index(s, j), r0, R // 2, C)
                    landed.append(pltpu.make_async_remote_copy(src_ref=got, dst_ref=got, **pair))
        return local, send, landed

    def start(ins, outs, sems):
        def run(s):
            local, send, _ = copies(ins, outs, sems, s, with_landed=False)
            for cp in local + send:
                cp.start()
        x, y, _ = _coords()
        _for_my_chip(2 * x + y, run)

    def wait(ins, outs, sems):
        def run(s):
            local, send, landed = copies(ins, outs, sems, s)
            for cp in landed:
                cp.wait_recv()
            for cp in send:
                cp.wait_send()
            for cp in local:
                cp.wait()
        x, y, _ = _coords()
        _for_my_chip(2 * x + y, run)

    out_shapes = [_sds(_gathered_shape(kind, (1,) + tuple(dims[t])), BF16) for t, (_, _, kind) in enumerate(items)]
    sems = [pltpu.SemaphoreType.DMA((n,)), pltpu.SemaphoreType.DMA((3 * n,)), pltpu.SemaphoreType.DMA((3 * n,))]
    return Comm([it[0] for it in items], out_shapes, sems, start, wait)


def gather_stage2(items, gathered):
    n = len(items)
    dims = [it[0].shape[1:] for it in items]

    def copies(outs, sems, s, with_landed=True):
        ssem, rsem = sems
        x, y, c = _coords()
        send, landed = [], []
        for t, (_, _, kind) in enumerate(items):
            R, C = dims[t]
            for j in range(3):
                pair = dict(send_sem=ssem.at[3 * t + j], recv_sem=rsem.at[3 * t + j],
                            device_id=(x, y, 1 - c), device_id_type=MESH)
                mine = _shard_rows(outs[t], kind, _flip_index(s, j), pl.multiple_of(c * (R // 2), 8), R // 2, C)
                send.append(pltpu.make_async_remote_copy(src_ref=mine, dst_ref=mine, **pair))
                if with_landed:
                    other = _shard_rows(outs[t], kind, _flip_index(s, j), pl.multiple_of((1 - c) * (R // 2), 8),
                                        R // 2, C)
                    landed.append(pltpu.make_async_remote_copy(src_ref=other, dst_ref=other, **pair))
        return send, landed

    def start(ins, outs, sems):
        def run(s):
            for cp in copies(outs, sems, s, with_landed=False)[0]:
                cp.start()
        x, y, _ = _coords()
        _for_my_chip(2 * x + y, run)

    def wait(ins, outs, sems):
        def run(s):
            send, landed = copies(outs, sems, s)
            for cp in landed:
                cp.wait_recv()
            for cp in send:
                cp.wait_send()
        x, y, _ = _coords()
        _for_my_chip(2 * x + y, run)

    out_shapes = [_sds(g.shape, BF16) for g in gathered]
    sems = [pltpu.SemaphoreType.DMA((3 * n,)), pltpu.SemaphoreType.DMA((3 * n,))]
    return Comm(gathered, out_shapes, sems, start, wait, aliases={t: t for t in range(n)})


def gather_both(items):
    s1 = gather_stage1(items)
    s2 = gather_stage2(items, s1.out_shapes)
    n1 = len(s1.sems)

    def wait(ins, outs, sems):
        s1.wait(ins, outs, sems[:n1])
        s2.start((), outs, sems[n1:])
        s2.wait((), outs, sems[n1:])

    return Comm(s1.ins, s1.out_shapes, s1.sems + s2.sems, lambda ins, outs, sems: s1.start(ins, outs, sems[:n1]), wait)


def _half_shape(kind, R, C):
    return (R // 2, N_CHIPS * C) if kind == "col" else (N_CHIPS, R // 2, C)


def exchange_halves(grads, metas):
    n = len(grads)

    def copies(ins, outs, sems):
        ssem, rsem = sems
        x, y, c = _coords()
        out = []
        for t, (kind, R, C) in enumerate(metas):
            r0 = pl.multiple_of((1 - c) * (R // 2), 8)
            src = ins[t].at[pl.ds(r0, R // 2), :] if kind == "col" else ins[t].at[:, pl.ds(r0, R // 2), :]
            out.append(pltpu.make_async_remote_copy(
                src_ref=src, dst_ref=outs[t], send_sem=ssem.at[t], recv_sem=rsem.at[t],
                device_id=(x, y, 1 - c), device_id_type=MESH))
        return out

    def start(ins, outs, sems):
        for cp in copies(ins, outs, sems):
            cp.start()

    def wait(ins, outs, sems):
        for cp in copies(ins, outs, sems):
            cp.wait()

    return Comm(grads, [_sds(_half_shape(*m), F32) for m in metas], [pltpu.SemaphoreType.DMA((n,))] * 2, start, wait)


def pair_sum(me, g, sib, meta, name):
    kind, R, C = meta
    h = R // 2

    def body(me_ref, g_ref, sib_ref, p16_ref, own_ref):
        s = pl.program_id(0)
        v = g_ref[...] + sib_ref[...]
        p16_ref[...] = v.astype(BF16)

        @pl.when(s == me_ref[1])
        def _():
            own_ref[...] = v

    if kind == "col":
        gspec = pl.BlockSpec((h, C), lambda s, me_ref: (me_ref[0], s))
        sspec = pl.BlockSpec((h, C), lambda s, me_ref: (0, s))
    else:
        gspec = pl.BlockSpec((None, h, C), lambda s, me_ref: (s, me_ref[0], 0))
        sspec = pl.BlockSpec((None, h, C), lambda s, me_ref: (s, 0, 0))
    grid_spec = pltpu.PrefetchScalarGridSpec(
        num_scalar_prefetch=1, grid=(N_CHIPS,), in_specs=[gspec, sspec],
        out_specs=[sspec, pl.BlockSpec((h, C), lambda s, me_ref: (0, 0))])
    return _pallas(body, name=name, grid_spec=grid_spec,
                   out_shape=[_sds(_half_shape(*meta), BF16), _sds((h, C), F32)],
                   compiler_params=_params(("arbitrary",), VMEM_BIG))(me, g, sib)


def scatter_partials(p16s, metas):
    n = len(p16s)

    def copies(ins, outs, sems, s):
        ssem, rsem = sems
        x, y, c = _coords()
        chips = _other_chips(x, y)
        out = []
        for t, (kind, R, C) in enumerate(metas):
            for j in range(3):
                sj = _flip_index(s, j)
                src = ins[t].at[:, sj * C:(sj + 1) * C] if kind == "col" else ins[t].at[sj]
                out.append(pltpu.make_async_remote_copy(
                    src_ref=src, dst_ref=outs[t].at[j], send_sem=ssem.at[3 * t + j], recv_sem=rsem.at[3 * t + j],
                    device_id=(chips[j][0], chips[j][1], c), device_id_type=MESH))
        return out

    def start(ins, outs, sems):
        def run(s):
            for cp in copies(ins, outs, sems, s):
                cp.start()
        x, y, _ = _coords()
        _for_my_chip(2 * x + y, run)

    def wait(ins, outs, sems):
        def run(s):
            for cp in copies(ins, outs, sems, s):
                cp.wait()
        x, y, _ = _coords()
        _for_my_chip(2 * x + y, run)

    return Comm(p16s, [_sds((3, R // 2, C), BF16) for (_, R, C) in metas],
                [pltpu.SemaphoreType.DMA((3 * n,))] * 2, start, wait)


def final_sum(me, own, q, buf, l, meta, name):
    _, R, C = meta
    h = R // 2

    def body(me_ref, own_ref, q_ref, buf_ref, o_ref):
        del buf_ref
        o_ref[...] = ((own_ref[...] + q_ref[0].astype(F32)) + q_ref[1].astype(F32)) + q_ref[2].astype(F32)

    grid_spec = pltpu.PrefetchScalarGridSpec(
        num_scalar_prefetch=1, grid=(1,),
        in_specs=[pl.BlockSpec((h, C), lambda i, me_ref: (0, 0)),
                  pl.BlockSpec((3, h, C), lambda i, me_ref: (0, 0, 0)), ANY],
        out_specs=pl.BlockSpec((None, h, C), lambda i, me_ref: (l, me_ref[0], 0)))
    return _pallas(body, name=name, grid_spec=grid_spec, out_shape=_sds(buf.shape, F32),
                   input_output_aliases={3: 0},
                   compiler_params=_params(("arbitrary",), VMEM_BIG))(me, own, q, buf)


def share_final(bufs):
    n = len(bufs)

    def body(*refs):
        ins, outs = refs[:n], refs[n:2 * n]
        ssem, rsem = refs[2 * n:]
        del ins
        x, y, c = _coords()
        copies = []
        for t in range(n):
            R = bufs[t].shape[1]
            r0 = pl.multiple_of(c * (R // 2), 8)
            blk = outs[t].at[:, pl.ds(r0, R // 2), :]
            copies.append(pltpu.make_async_remote_copy(
                src_ref=blk, dst_ref=blk, send_sem=ssem.at[t], recv_sem=rsem.at[t],
                device_id=(x, y, 1 - c), device_id_type=MESH))
        for cp in copies:
            cp.start()
        for t in range(n):
            R = bufs[t].shape[1]
            r1 = pl.multiple_of((1 - c) * (R // 2), 8)
            other = outs[t].at[:, pl.ds(r1, R // 2), :]
            pltpu.make_async_remote_copy(
                src_ref=other, dst_ref=other, send_sem=ssem.at[t], recv_sem=rsem.at[t],
                device_id=(x, y, 1 - c), device_id_type=MESH).wait_recv()
        for cp in copies:
            cp.wait_send()

    out_shape = [_sds(b.shape, F32) for b in bufs]
    return _pallas(body, name="share_final", in_specs=[ANY] * n, out_specs=[ANY] * n, out_shape=out_shape,
                   input_output_aliases={t: t for t in range(n)},
                   scratch_shapes=[pltpu.SemaphoreType.DMA((n,))] * 2,
                   compiler_params=pltpu.CompilerParams(has_side_effects=True))(*bufs)


def allreduce_small(part):
    rows = part.shape[0]
    h = rows // 2

    def body(p_ref, o_ref, sib_buf, pair_buf, chip_buf, ssem, rsem):
        x, y, c = _coords()
        sibling = dict(device_id=(x, y, 1 - c), device_id_type=MESH)
        mine = pl.ds(pl.multiple_of(c * h, 8), h)
        theirs = pl.ds(pl.multiple_of((1 - c) * h, 8), h)

        swap = pltpu.make_async_remote_copy(src_ref=p_ref.at[theirs], dst_ref=sib_buf, send_sem=ssem.at[0],
                                            recv_sem=rsem.at[0], **sibling)
        swap.start()
        swap.wait()
        pair_buf[...] = p_ref[mine, :] + sib_buf[...]

        chips = _other_chips(x, y)
        sends = [pltpu.make_async_remote_copy(src_ref=pair_buf, dst_ref=chip_buf.at[j], send_sem=ssem.at[1 + j],
                                              recv_sem=rsem.at[1 + j], device_id=(chips[j][0], chips[j][1], c),
                                              device_id_type=MESH) for j in range(3)]
        for cp in sends:
            cp.start()
        for cp in sends:
            cp.wait()

        def total(s):
            terms = {s: pair_buf[...]}
            for j in range(3):
                terms[_flip_index(s, j)] = chip_buf[j]
            o_ref[mine, :] = ((terms[0] + terms[1]) + terms[2]) + terms[3]

        _for_my_chip(2 * x + y, total)

        back = pltpu.make_async_remote_copy(src_ref=o_ref.at[mine], dst_ref=o_ref.at[mine], send_sem=ssem.at[4],
                                            recv_sem=rsem.at[4], **sibling)
        back.start()
        pltpu.make_async_remote_copy(src_ref=o_ref.at[theirs], dst_ref=o_ref.at[theirs], send_sem=ssem.at[4],
                                     recv_sem=rsem.at[4], **sibling).wait_recv()
        back.wait_send()

    return _pallas(body, name="allreduce_small",
                   in_specs=[pl.BlockSpec(memory_space=pltpu.VMEM)], out_specs=pl.BlockSpec(memory_space=pltpu.VMEM),
                   out_shape=_sds((rows, 128), F32),
                   scratch_shapes=[pltpu.VMEM((h, 128), F32), pltpu.VMEM((h, 128), F32), pltpu.VMEM((3, h, 128), F32),
                                   pltpu.SemaphoreType.DMA((5,)), pltpu.SemaphoreType.DMA((5,))],
                   compiler_params=pltpu.CompilerParams(has_side_effects=True))(part)


def _rows128(a):
    flat = a.reshape(-1)
    rows = -(-flat.shape[0] // 128)
    rows8 = -(-rows // 8) * 8
    flat = jnp.pad(flat, (0, rows8 * 128 - flat.shape[0]))
    return flat.reshape(rows8, 128)


def kernel(x, norm_mix_g, norm_ffn_g, final_g, a_w_in, a_v_gain, a_w_s, a_b_s, a_w_out, b_w_qkv, b_rel_bias, b_w_out, ffn_w_gate, ffn_w_up, ffn_w_down, loss_target, m_norm_mix_g, m_norm_ffn_g, m_final_g, m_a_w_in, m_a_v_gain, m_a_w_s, m_a_b_s, m_a_w_out, m_b_w_qkv, m_b_rel_bias, m_b_w_out, m_ffn_w_gate, m_ffn_w_up, m_ffn_w_down, v_norm_mix_g, v_norm_ffn_g, v_final_g, v_a_w_in, v_a_v_gain, v_a_w_s, v_a_b_s, v_a_w_out, v_b_w_qkv, v_b_rel_bias, v_b_w_out, v_ffn_w_gate, v_ffn_w_up, v_ffn_w_down):
    T = x.shape[1]
    weights = dict(norm_mix_g=norm_mix_g, norm_ffn_g=norm_ffn_g, final_g=final_g, a_w_in=a_w_in, a_v_gain=a_v_gain,
                   a_w_s=a_w_s, a_b_s=a_b_s, a_w_out=a_w_out, b_w_qkv=b_w_qkv, b_rel_bias=b_rel_bias,
                   b_w_out=b_w_out, ffn_w_gate=ffn_w_gate, ffn_w_up=ffn_w_up, ffn_w_down=ffn_w_down)
    mom_m = dict(norm_mix_g=m_norm_mix_g, norm_ffn_g=m_norm_ffn_g, final_g=m_final_g, a_w_in=m_a_w_in,
                 a_v_gain=m_a_v_gain, a_w_s=m_a_w_s, a_b_s=m_a_b_s, a_w_out=m_a_w_out, b_w_qkv=m_b_w_qkv,
                 b_rel_bias=m_b_rel_bias, b_w_out=m_b_w_out, ffn_w_gate=m_ffn_w_gate, ffn_w_up=m_ffn_w_up,
                 ffn_w_down=m_ffn_w_down)
    mom_v = dict(norm_mix_g=v_norm_mix_g, norm_ffn_g=v_norm_ffn_g, final_g=v_final_g, a_w_in=v_a_w_in,
                 a_v_gain=v_a_v_gain, a_w_s=v_a_w_s, a_b_s=v_a_b_s, a_w_out=v_a_w_out, b_w_qkv=v_b_w_qkv,
                 b_rel_bias=v_b_rel_bias, b_w_out=v_b_w_out, ffn_w_gate=v_ffn_w_gate, ffn_w_up=v_ffn_w_up,
                 ffn_w_down=v_ffn_w_down)
    order = list(weights)
    transposed = ("ffn_w_gate", "ffn_w_up")
    for k in transposed:
        weights[k], mom_m[k], mom_v[k] = (jnp.swapaxes(a, 1, 2) for a in (weights[k], mom_m[k], mom_v[k]))

    xi, yi, ci = _coords()
    me = jnp.stack([ci, 2 * xi + yi]).astype(jnp.int32)

    shard16 = {k: cast_bf16(weights[k], "cast_" + k) for k in BIG}

    def layer_tensors(i):
        mix = ("a_w_in", "a_w_out") if i % 2 == 0 else ("b_w_qkv", "b_w_out")
        return [(k, i // 2) for k in mix] + [(k, i) for k in ("ffn_w_gate", "ffn_w_up", "ffn_w_down")]

    def gather_items(keys):
        return [(shard16[k], l, GATHER_KIND[k]) for k, l in keys]

    def grad_metas(keys):
        return [(GATHER_KIND[k],) + tuple(weights[k].shape[1:]) for k, _ in keys]

    FFN = ("ffn_w_gate", "ffn_w_up", "ffn_w_down")
    k0a = [("a_w_out", 0), ("ffn_w_gate", 0)]
    k0b = [("ffn_w_up", 0), ("ffn_w_down", 0)]
    k1a = [("b_w_qkv", 0), ("b_w_out", 0), ("ffn_w_gate", 1)]
    k1b = [("ffn_w_up", 1), ("ffn_w_down", 1)]
    k3a = [("b_w_qkv", 1), ("b_w_out", 1), ("ffn_w_gate", 3)]
    k3b = [("ffn_w_up", 3), ("ffn_w_down", 3)]
    plans = {
        "a_in_l0": [("g1", k0a)], "sgu_fwd_l0": [("g2", k0a), ("g1", k0b)], "a_out_l0": [("g2", k0b)],
        "rms_mix_l0": [("g1", [("a_w_in", 0)])],
        "ffn_up_l0": [("g1", k1a)], "ffn_down_l0": [("g2", k1a), ("g1", k1b[:1])],
        "b_qkv_l1": [("g2", k1b[:1]), ("g1", k1b[1:])],
        "attn_fwd_l1": [("g2", k1b[1:]), ("g1", layer_tensors(2))], "b_out_l1": [("g2", layer_tensors(2))],
        "ffn_up_l1": [("g1", k3a)], "ffn_down_l1": [("g2", k3a)],
        "a_in_l2": [("g1", k3b)], "sgu_fwd_l2": [("g2", k3b)],
        "ffn_bwd_dh_l2": [("ex", layer_tensors(3))], "sgu_bwd_l2": [("sc", layer_tensors(3))],
        "ffn_bwd_dh_l1": [("ex", layer_tensors(2))], "attn_bwd_l1": [("sc", layer_tensors(2))],
        "ffn_bwd_dh_l0": [("ex", layer_tensors(1))], "ffn_bwd_dhn_l0": [("sc", k1a)],
        "dffn_w_gate_l0": [("sc", [("ffn_w_up", 1)])], "dffn_w_up_l0": [("sc", [("ffn_w_down", 1)])],
        "a_out_bwd_l0": [("ex", [(k, 0) for k in FFN])],
        "sgu_bwd_l0": [("sc", [("ffn_w_gate", 0), ("ffn_w_up", 0)]), ("ex", [("a_w_out", 0)])],
        "dw_in_l0": [("sc", [("ffn_w_down", 0), ("a_w_out", 0)])],
    }
    part16, full16 = {}, {}
    sib, p16, own_parts, recv_parts = {}, {}, {}, {}

    def make_comm(kind, keys):
        if kind == "g1":
            return gather_stage1(gather_items(keys)), lambda outs: part16.update(zip(keys, outs))
        if kind == "g2":
            return (gather_stage2(gather_items(keys), [part16[kl] for kl in keys]),
                    lambda outs: full16.update(zip(keys, outs)))
        if kind == "ex":
            return (exchange_halves([big_grads[k][l] for k, l in keys], grad_metas(keys)),
                    lambda outs: sib.update(zip(keys, outs)))
        for kl, m_ in zip(keys, grad_metas(keys)):
            p16[kl], own_parts[kl] = pair_sum(me, big_grads[kl[0]][kl[1]], sib[kl], m_, "pair_sum_%s_l%d" % kl)
        return (scatter_partials([p16[kl] for kl in keys], grad_metas(keys)),
                lambda outs: recv_parts.update(zip(keys, outs)))

    def run(name, make):
        steps = plans.get(name)
        if not steps:
            return make(None)
        made = [make_comm(kind, keys) for kind, keys in steps]
        main, outs = make(combine([c for c, _ in made]))
        for c, done in made:
            done(outs[:len(c.out_shapes)])
            outs = outs[len(c.out_shapes):]
        return main

    def weight(k, l):
        w = full16[(k, l)]
        if k == "a_w_out":
            return w.reshape(1, GH, D)
        return w.reshape(1, D, D) if k == "b_w_out" else w


    xcur = x.reshape(T, D)
    hn = run("rms_mix_l0", lambda comm: rms_fwd(xcur, norm_mix_g[0][None], "rms_mix_l0", comm=comm))
    comm, done = make_comm("g2", [("a_w_in", 0)])
    done(run_comm(comm, "gather_first_d2d"))
    saved = []
    for i in range(DEPTH):
        j = i // 2
        tag = "_l%d" % i
        st = {"x_in": xcur, "hn": hn}
        if i % 2 == 0:
            pre = run("a_in" + tag, lambda comm: matmul(
                "a_in" + tag, NN, hn, pl.BlockSpec((1024, D), lambda i_, j_: (i_, 0)),
                weight("a_w_in", j), pl.BlockSpec((None, D, 1024), lambda i_, j_: (0, 0, j_)),
                _sds((T, 2 * GH), BF16), pl.BlockSpec((1024, 1024), lambda i_, j_: (i_, j_)),
                (T // 1024, 4), comm=comm))
            y = run("sgu_fwd" + tag, lambda comm: sgu_fwd(
                pre, a_v_gain[j][None], a_w_s[j], a_b_s[j][:, :, None], "sgu_fwd" + tag, comm=comm))
            xmid, hn2 = run("a_out" + tag, lambda comm: residual_proj(
                "a_out" + tag, *out_proj(y, weight("a_w_out", j)), xcur, norm_ffn_g[i][None], comm=comm))
            st.update(pre=pre, y=y)
        else:
            qkvp = run("b_qkv" + tag, lambda comm: proj_qkv(hn, weight("b_w_qkv", j), 0, "b_qkv" + tag, comm=comm))
            wb = jnp.transpose(bias_build(b_rel_bias[j], "bias_build" + tag), (1, 0, 2))
            o = run("attn_fwd" + tag, lambda comm: attn_fwd(qkvp, wb, "attn_fwd" + tag, comm=comm))
            xmid, hn2 = run("b_out" + tag, lambda comm: residual_proj(
                "b_out" + tag, *out_proj(o, weight("b_w_out", j)), xcur, norm_ffn_g[i][None], comm=comm))
            st.update(qkvp=qkvp, wb=wb, o=o)
        g, u, h = run("ffn_up" + tag, lambda comm: ffn_up(
            hn2, weight("ffn_w_gate", i), weight("ffn_w_up", i), 0, "ffn_up" + tag, comm=comm))
        next_g = norm_mix_g[i + 1][None] if i + 1 < DEPTH else None
        down = run("ffn_down" + tag, lambda comm: residual_proj(
            "ffn_down" + tag, *ffn_down(h, weight("ffn_w_down", i)), xmid, next_g, comm=comm))
        xcur, hn = down if next_g is not None else (down, None)
        st.update(x_mid=xmid, hn2=hn2, g=g, u=u, h=h)
        saved.append(st)

    loss_part, dx, dxb, d_final = final_loss(xcur, final_g[None], loss_target.reshape(T, D), "final_loss")

    tk = min(2048, T)
    big_grads = {k: [None] * weights[k].shape[0] for k in BIG}
    small = {"norm_mix_g": [None] * DEPTH, "norm_ffn_g": [None] * DEPTH, "a_v_gain": [None] * 2,
             "a_w_s": [None] * 2, "a_b_s": [None] * 2, "b_rel_bias": [None] * 2}
    tok = lambda width: pl.BlockSpec((tk, width), lambda j_, k_: (k_, 0))
    part = lambda: pl.BlockSpec((None, tk, FS), lambda j_, k_: (j_, k_, 0))
    for i in reversed(range(DEPTH)):
        j = i // 2
        tag = "_l%d" % i
        st = saved[i]
        dg, du = run("ffn_bwd_dh" + tag, lambda comm: ffn_bwd_dh(
            dxb, weight("ffn_w_down", i), st["g"], st["u"], 0, "ffn_bwd_dh" + tag, comm=comm))
        big_grads["ffn_w_down"][i] = wgrad(
            "dw_down" + tag, st["h"], part(), dxb, tok(D), _sds((N_CHIPS, FS, D), F32),
            pl.BlockSpec((None, FS, D), lambda j_, k_: (j_, 0, 0)), N_CHIPS, T, tk)
        dx_mid, dxb_mid, dgn = run("ffn_bwd_dhn" + tag, lambda comm: dgrad_rms(
            "ffn_bwd_dhn" + tag, *ffn_dgrad(dg, du, weight("ffn_w_gate", i), weight("ffn_w_up", i)),
            st["x_mid"], norm_ffn_g[i][None], dx, comm=comm))
        for nm, dz in (("ffn_w_gate", dg), ("ffn_w_up", du)):
            big_grads[nm][i] = run("d" + nm + tag, lambda comm: wgrad(
                "d" + nm + tag, dz, part(), st["hn2"], tok(D), _sds((N_CHIPS, FS, D), F32),
                pl.BlockSpec((None, FS, D), lambda j_, k_: (j_, 0, 0)), N_CHIPS, T, tk, comm=comm))
        dx, dxb = dx_mid, dxb_mid
        small["norm_ffn_g"][i] = dgn
        if i % 2 == 0:
            dy = run("a_out_bwd" + tag, lambda comm: matmul(
                "a_out_bwd" + tag, NT, dxb, pl.BlockSpec((1024, D), lambda i_, j_: (i_, 0)),
                weight("a_w_out", j), pl.BlockSpec((None, 1024, D), lambda i_, j_: (0, j_, 0)),
                _sds((T, GH), BF16), pl.BlockSpec((1024, 1024), lambda i_, j_: (i_, j_)), (T // 1024, 2), comm=comm))
            big_grads["a_w_out"][j] = wgrad(
                "dw_aout" + tag, st["y"], pl.BlockSpec((tk, 1024), lambda j_, k_: (k_, j_)), dxb, tok(D),
                _sds((GH, D), F32), pl.BlockSpec((1024, D), lambda j_, k_: (j_, 0)), 2, T, tk
            ).reshape(N_CHIPS, GH // N_CHIPS, D)
            dpre, d_ws, d_bs, d_gain = run("sgu_bwd" + tag, lambda comm: sgu_bwd(
                st["pre"], dy, a_v_gain[j][None], a_w_s[j], a_b_s[j][:, :, None], "sgu_bwd" + tag,
                tm=2 * SGU_BLOCK, comm=comm))
            small["a_w_s"][j], small["a_b_s"][j], small["a_v_gain"][j] = d_ws, d_bs, d_gain
            dx_in, dxb_in, dgn = run("a_in_bwd" + tag, lambda comm: dgrad_rms(
                "a_in_bwd" + tag, *in_dgrad(dpre, weight("a_w_in", j)),
                st["x_in"], norm_mix_g[i][None], dx, comm=comm))
            big_grads["a_w_in"][j] = run("dw_in" + tag, lambda comm: wgrad(
                "dw_in" + tag, st["hn"], tok(D), dpre, pl.BlockSpec((tk, 1024), lambda j_, k_: (k_, j_)),
                _sds((D, 2 * GH), F32), pl.BlockSpec((D, 1024), lambda j_, k_: (0, j_)), 4, T, tk, comm=comm))
        else:
            do = matmul("b_out_bwd" + tag, NT, dxb, pl.BlockSpec((1024, D), lambda i_, j_: (i_, 0)),
                        weight("b_w_out", j), pl.BlockSpec((None, D, D), lambda i_, j_: (0, 0, 0)),
                        _sds((T, D), BF16), pl.BlockSpec((1024, D), lambda i_, j_: (i_, 0)), (T // 1024, 1))
            big_grads["b_w_out"][j] = wgrad(
                "dw_bout" + tag, st["o"], tok(D), dxb, tok(D),
                _sds((D, D), F32), pl.BlockSpec((D, D), lambda j_, k_: (0, 0)), 1, T, tk
            ).reshape(N_CHIPS, D // N_CHIPS, D)
            dqkvp, dwb = run("attn_bwd" + tag, lambda comm: attn_bwd(
                st["qkvp"], st["o"], do, st["wb"], "attn_bwd" + tag, comm=comm))
            small["b_rel_bias"][j] = bias_grad(
                jnp.pad(jnp.transpose(dwb, (1, 0, 2)), ((0, 0), (0, 0), (0, DIAG - KW))), "bias_grad" + tag)
            dx_in, dxb_in, dgn = dgrad_rms(
                "b_qkv_bwd" + tag, *qkv_dgrad(dqkvp, weight("b_w_qkv", j)),
                st["x_in"], norm_mix_g[i][None], dx)
            big_grads["b_w_qkv"][j] = wgrad(
                "dw_qkv" + tag, st["hn"], tok(D), dqkvp,
                pl.BlockSpec((None, tk, D), lambda j_, k_: (j_, k_ + FRONT // tk, 0)),
                _sds((D, 3 * D), F32), pl.BlockSpec((D, D), lambda j_, k_: (0, j_)), 3, T, tk)
        dx, dxb = dx_in, dxb_in
        small["norm_mix_g"][i] = dgn

    small_grads = {
        "norm_mix_g": jnp.concatenate(small["norm_mix_g"], axis=0),
        "norm_ffn_g": jnp.concatenate(small["norm_ffn_g"], axis=0),
        "final_g": d_final.reshape(D),
        "a_v_gain": jnp.concatenate(small["a_v_gain"], axis=0),
        "a_w_s": jnp.stack(small["a_w_s"]),
        "a_b_s": jnp.stack(small["a_b_s"]).reshape(2, SGU_G, SGU_BLOCK),
        "b_rel_bias": jnp.stack(small["b_rel_bias"]),
    }
    small_names = list(small_grads)
    packed = [_rows128(small_grads[k]) for k in small_names] + [_rows128(loss_part[:, :1])]
    offs = [0]
    for p in packed:
        offs.append(offs[-1] + p.shape[0])
    reduced = allreduce_small(jnp.concatenate(packed, axis=0))
    grads = {}
    for t, k in enumerate(small_names):
        nelem = small_grads[k].size
        grads[k] = reduced[offs[t]:offs[t + 1]].reshape(-1)[:nelem].reshape(weights[k].shape)
    loss = reduced[offs[len(small_names)], 0]

    last = [("a_w_in", 0)]
    for kind, name in (("ex", "exchange_last"), ("sc", "scatter_last")):
        comm, done = make_comm(kind, last)
        done(run_comm(comm, name))
    bufs = {k: jnp.zeros(weights[k].shape, F32) for k in BIG}
    for i in range(DEPTH):
        for kl, m_ in zip(layer_tensors(i), grad_metas(layer_tensors(i))):
            bufs[kl[0]] = final_sum(me, own_parts[kl], recv_parts[kl], bufs[kl[0]], kl[1], m_,
                                    "final_sum_%s_l%d" % kl)
    shared = share_final([bufs[k] for k in BIG])
    for k, gfull in zip(BIG, shared):
        grads[k] = gfull

    delta, new_m, new_v = {}, {}, {}
    for k in order:
        shp = weights[k].shape
        if k in BIG:
            view = shp
        elif k == "a_w_s":
            view = (2, SGU_G * SGU_BLOCK, SGU_BLOCK)
        elif len(shp) == 1:
            view = (1, 1, shp[0])
        elif len(shp) == 2:
            view = (1,) + shp
        else:
            view = shp
        g_, d_, m_, v_ = adamw(weights[k].reshape(view), grads[k].reshape(view), mom_m[k].reshape(view),
                               mom_v[k].reshape(view), "adamw_" + k)
        grads[k], delta[k], new_m[k], new_v[k] = g_.reshape(shp), d_.reshape(shp), m_.reshape(shp), v_.reshape(shp)
    for k in transposed:
        for tree in (grads, delta, new_m, new_v):
            tree[k] = jnp.swapaxes(tree[k], 1, 2)

    return (loss, dx.reshape(1, T, D), *[grads[k] for k in order], *[delta[k] for k in order],
            *[new_m[k] for k in order], *[new_v[k] for k in order])
```

```python
import functools

import jax
import jax.numpy as jnp
from jax import lax
from jax.experimental import pallas as pl
from jax.experimental.pallas import tpu as pltpu

F32 = jnp.float32
BF16 = jnp.bfloat16
MESH = pl.DeviceIdType.MESH

D = 1024
DEPTH = 4
EPS = 1e-6
SGU_BLOCK = 128
GH = 2048
SGU_G = 8
SGU_GD = GH // SGU_G
N_HEADS = 16
HEAD_DIM = 64
CHUNK = 64
PAD = 8 * CHUNK
FRONT = 2048
QB = 128
KW = PAD + QB
N_REL = 192
REL_MIN = -(CHUNK - 1)
REL_MAX = 128
D_FF = 2816
FS = D_FF // 4
NEG = -1e30
SCALE = HEAD_DIM ** -0.5
N_CHIPS = 4

ADAM_LR = 0.001
ADAM_B1 = 0.9
ADAM_B2 = 0.999
ADAM_EPS = 1e-08
ADAM_WD = 0.01
ADAM_STEP = 10

VMEM_BIG = 56 * 1024 * 1024

NN = ((1,), (0,))
NT = ((1,), (1,))
TN = ((0,), (0,))


def _dot(a, b, dims):
    return lax.dot_general(a, b, (dims, ((), ())), preferred_element_type=F32)


class Comm:
    def __init__(self, ins, out_shapes, sems, start, wait, aliases=None):
        self.ins, self.out_shapes, self.sems = list(ins), list(out_shapes), list(sems)
        self.start, self.wait, self.aliases = start, wait, dict(aliases or {})


def _host(body, comm, kw):
    grid = tuple(kw["grid"])
    in_specs = list(kw["in_specs"])
    single = not isinstance(kw["out_specs"], (list, tuple))
    out_specs = [kw["out_specs"]] if single else list(kw["out_specs"])
    out_shape = [kw["out_shape"]] if single else list(kw["out_shape"])
    scratch = list(kw.get("scratch_shapes", ()))
    counts = (len(in_specs), len(comm.ins), len(out_specs), len(comm.out_shapes), len(scratch))

    def hosted(*refs):
        parts, p = [], 0
        for cnt in counts:
            parts.append(refs[p:p + cnt])
            p += cnt
        main_in, c_in, main_out, c_out, main_scr = parts
        sems = refs[p:]
        ids = [pl.program_id(a) for a in range(len(grid))]
        first = functools.reduce(jnp.logical_and, [i == 0 for i in ids])
        last = functools.reduce(jnp.logical_and, [i == n - 1 for i, n in zip(ids, grid)])
        pl.when(first)(lambda: comm.start(c_in, c_out, sems))
        body(*main_in, *main_out, *main_scr)
        pl.when(last)(lambda: comm.wait(c_in, c_out, sems))

    old = kw["compiler_params"]
    kw = dict(kw, in_specs=in_specs + [ANY] * len(comm.ins), out_specs=out_specs + [ANY] * len(comm.out_shapes),
              out_shape=out_shape + comm.out_shapes, scratch_shapes=scratch + comm.sems,
              compiler_params=pltpu.CompilerParams(dimension_semantics=("arbitrary",) * len(grid),
                                                   vmem_limit_bytes=old.vmem_limit_bytes, has_side_effects=True))
    if comm.aliases:
        kw["input_output_aliases"] = {counts[0] + i: counts[2] + o for i, o in comm.aliases.items()}
    return hosted, kw


def _pallas(body, comm=None, **kw):
    if comm is not None:
        body, kw = _host(body, comm, kw)
    return pl.pallas_call(body, **kw)


def _split_outs(outs, comm, n_main):
    outs = list(outs) if isinstance(outs, (list, tuple)) else [outs]
    main = outs[:n_main]
    return (main[0] if n_main == 1 else main), outs[n_main:]


def run_comm(comm, name):
    nci, nco = len(comm.ins), len(comm.out_shapes)

    def body(*refs):
        c_in, c_out, sems = refs[:nci], refs[nci:nci + nco], refs[nci + nco:]
        comm.start(c_in, c_out, sems)
        comm.wait(c_in, c_out, sems)

    kw = {}
    if comm.aliases:
        kw["input_output_aliases"] = dict(comm.aliases)
    return _pallas(body, name=name, in_specs=[ANY] * nci, out_specs=[ANY] * nco, out_shape=comm.out_shapes,
                   scratch_shapes=comm.sems, compiler_params=pltpu.CompilerParams(has_side_effects=True),
                   **kw)(*comm.ins)


def combine(comms):
    if len(comms) == 1:
        return comms[0]
    spans, ni, no, ns = [], 0, 0, 0
    for c in comms:
        spans.append((slice(ni, ni + len(c.ins)), slice(no, no + len(c.out_shapes)), slice(ns, ns + len(c.sems))))
        ni, no, ns = ni + len(c.ins), no + len(c.out_shapes), ns + len(c.sems)

    def start(ins, outs, sems):
        for c, (si, so, ss) in zip(comms, spans):
            c.start(ins[si], outs[so], sems[ss])

    def wait(ins, outs, sems):
        for c, (si, so, ss) in zip(comms, spans):
            c.wait(ins[si], outs[so], sems[ss])

    aliases = {}
    for c, (si, so, _) in zip(comms, spans):
        aliases.update({si.start + i: so.start + o for i, o in c.aliases.items()})
    return Comm([a for c in comms for a in c.ins], [o for c in comms for o in c.out_shapes],
                [s for c in comms for s in c.sems], start, wait, aliases)


def _call(body, comm, n_main, args, **kw):
    if comm is None:
        return _pallas(body, **kw)(*args)
    return _split_outs(_pallas(body, comm=comm, **kw)(*args, *comm.ins), comm, n_main)


def _params(sem=None, vmem=None):
    return pltpu.CompilerParams(dimension_semantics=sem, vmem_limit_bytes=vmem)


def _sds(shape, dtype):
    return jax.ShapeDtypeStruct(tuple(shape), dtype)


_GELU_C = 0.7978845608028654


_GELU_A = _GELU_C * 0.044715


def _gelu(x):
    t = jnp.tanh(x * (_GELU_C + _GELU_A * (x * x)))
    h = 0.5 * x
    return h + h * t


def _gelu_and_grad(x):
    x2 = x * x
    t = jnp.tanh(x * (_GELU_C + _GELU_A * x2))
    h = 0.5 * x
    val = h + h * t
    grad = (0.5 + 0.5 * t) + (h * (1.0 - t * t)) * (_GELU_C + (3.0 * _GELU_A) * x2)
    return val, grad


def _sigmoid(x):
    return 0.5 * (jnp.tanh(0.5 * x) + 1.0)


def cast_bf16(w, name):
    L, R, C = w.shape

    def body(w_ref, o_ref):
        o_ref[...] = w_ref[...].astype(BF16)

    spec = pl.BlockSpec((None, R, C), lambda l: (l, 0, 0))
    return _pallas(body, name=name, grid=(L,), in_specs=[spec], out_specs=spec,
                   out_shape=_sds((L, R, C), BF16), compiler_params=_params(("parallel",)))(w)


def rms_fwd(x, g, name, tm=512, comm=None):
    T = x.shape[0]

    def body(x_ref, g_ref, o_ref):
        o_ref[...] = _rms_rows(x_ref[...], g_ref[...])

    row = pl.BlockSpec((tm, D), lambda i: (i, 0))
    return _call(body, comm, 1, (x, g), name=name, grid=(T // tm,),
                 in_specs=[row, pl.BlockSpec((1, D), lambda i: (0, 0))], out_specs=row,
                 out_shape=_sds((T, D), BF16), compiler_params=_params(("parallel",)))


def dgrad_rms(name, compute, args, specs, x, g, dres, tm=512, comm=None):
    T = x.shape[0]
    n = T // tm
    k = len(args)

    def body(*refs):
        x_ref, g_ref, dres_ref, dx_ref, dxb_ref, dg_ref, acc_ref = refs[k:]
        i = pl.program_id(0)
        xf = x_ref[...]
        r = lax.rsqrt(jnp.mean(xf * xf, axis=-1, keepdims=True) + EPS)
        xhat = xf * r
        dhf = compute(*refs[:k])
        part = (dhf * xhat).reshape(tm // 8, 8, D).sum(axis=0)

        @pl.when(i == 0)
        def _():
            acc_ref[...] = part

        @pl.when(i > 0)
        def _():
            acc_ref[...] += part

        dxhat = dhf * g_ref[...]
        dx = dres_ref[...] + r * (dxhat - xhat * jnp.mean(dxhat * xhat, axis=-1, keepdims=True))
        dx_ref[...] = dx
        dxb_ref[...] = dx.astype(BF16)

        @pl.when(i == n - 1)
        def _():
            dg_ref[...] = jnp.sum(acc_ref[...], axis=0, keepdims=True)

    row = pl.BlockSpec((tm, D), lambda i: (i, 0))
    vec = pl.BlockSpec((1, D), lambda i: (0, 0))
    return _call(body, comm, 3, (*args, x, g, dres), name=name, grid=(n,),
                 in_specs=list(specs) + [row, vec, row], out_specs=[row, row, vec],
                 out_shape=[_sds((T, D), F32), _sds((T, D), BF16), _sds((1, D), F32)],
                 scratch_shapes=[pltpu.VMEM((8, D), F32)],
                 compiler_params=_params(("arbitrary",), VMEM_BIG))


def final_loss(x, g, tgt, name, tm=256):
    T = x.shape[0]
    n = T // tm

    def body(x_ref, g_ref, t_ref, loss_ref, dx_ref, dxb_ref, dg_ref, acc_ref, lacc_ref):
        i = pl.program_id(0)
        xf = x_ref[...]
        r = lax.rsqrt(jnp.mean(xf * xf, axis=-1, keepdims=True) + EPS)
        xhat = xf * r
        gg = g_ref[...]
        e = xhat * gg - t_ref[...]
        dy = e * (1.0 / D)
        part = (dy * xhat).reshape(tm // 8, 8, D).sum(axis=0)
        lpart = (e * e).reshape(tm // 8, 8, D).sum(axis=0)

        @pl.when(i == 0)
        def _():
            acc_ref[...] = part
            lacc_ref[...] = lpart

        @pl.when(i > 0)
        def _():
            acc_ref[...] += part
            lacc_ref[...] += lpart

        dxhat = dy * gg
        dx = r * (dxhat - xhat * jnp.mean(dxhat * xhat, axis=-1, keepdims=True))
        dx_ref[...] = dx
        dxb_ref[...] = dx.astype(BF16)

        @pl.when(i == n - 1)
        def _():
            dg_ref[...] = jnp.sum(acc_ref[...], axis=0, keepdims=True)
            total = jnp.sum(jnp.sum(lacc_ref[...], axis=0, keepdims=True), axis=1, keepdims=True)
            loss_ref[...] = jnp.broadcast_to(total * (0.5 / D), (1, 128))

    row = pl.BlockSpec((tm, D), lambda i: (i, 0))
    vec = pl.BlockSpec((1, D), lambda i: (0, 0))
    return _pallas(body, name=name, grid=(n,), in_specs=[row, vec, row],
                   out_specs=[pl.BlockSpec((1, 128), lambda i: (0, 0)), row, row, vec],
                   out_shape=[_sds((1, 128), F32), _sds((T, D), F32), _sds((T, D), BF16), _sds((1, D), F32)],
                   scratch_shapes=[pltpu.VMEM((8, D), F32), pltpu.VMEM((8, D), F32)],
                   compiler_params=_params(("arbitrary",)))(x, g, tgt)


def matmul(name, dims, a, a_spec, b, b_spec, out_shape, out_spec, grid, *, acc=False, res=None, res_spec=None,
           comm=None):
    has_res = res is not None

    def body(*refs):
        a_ref, b_ref = refs[0], refs[1]
        r_ref = refs[2] if has_res else None
        o_ref = refs[-1]
        d = _dot(a_ref[...], b_ref[...], dims)
        if not acc:
            if has_res:
                d = d + r_ref[...]
            o_ref[...] = d.astype(o_ref.dtype)
        else:
            k = pl.program_id(len(grid) - 1)

            @pl.when(k == 0)
            def _():
                o_ref[...] = (d + r_ref[...]) if has_res else d

            @pl.when(k > 0)
            def _():
                o_ref[...] += d

    sem = ("parallel",) * (len(grid) - 1) + (("arbitrary",) if acc else ("parallel",))
    ins = [a, b] + ([res] if has_res else [])
    specs = [a_spec, b_spec] + ([res_spec] if has_res else [])
    return _call(body, comm, 1, ins, name=name, grid=grid, in_specs=specs, out_specs=out_spec, out_shape=out_shape,
                 compiler_params=_params(sem, VMEM_BIG))


def wgrad(name, a, a_spec, b, b_spec, out_shape, out_spec, J, T, tk, comm=None):
    return matmul(name, TN, a, a_spec, b, b_spec, out_shape, out_spec, (J, T // tk), acc=True, comm=comm)


def _sgu_mask():
    p = lax.broadcasted_iota(jnp.int32, (SGU_BLOCK, SGU_BLOCK), 0)
    q = lax.broadcasted_iota(jnp.int32, (SGU_BLOCK, SGU_BLOCK), 1)
    return lax.shift_right_logical(q, 6) <= lax.shift_right_logical(p, 6)


def sgu_fwd(pre, gain, w_s, b_s, name, comm=None):
    T = pre.shape[0]

    def body(pre_ref, gain_ref, ws_ref, bs_ref, y_ref):
        mask = _sgu_mask()
        u = _gelu(pre_ref[:, :GH].astype(F32))
        va = _gelu(pre_ref[:, GH:].astype(F32))
        r = lax.rsqrt(jnp.mean(va * va, axis=-1, keepdims=True) + EPS)
        vn = ((va * r) * gain_ref[...]).astype(BF16)
        for g in range(SGU_G):
            sl = slice(g * SGU_GD, (g + 1) * SGU_GD)
            wm = jnp.where(mask, ws_ref[g], 0.0).astype(BF16)
            vm = _dot(wm, vn[:, sl], NN) + bs_ref[g]
            y_ref[:, sl] = (u[:, sl] * vm).astype(BF16)

    return _call(
        body, comm, 1, (pre, gain, w_s, b_s), name=name, grid=(T // SGU_BLOCK,),
        in_specs=[pl.BlockSpec((SGU_BLOCK, 2 * GH), lambda i: (i, 0)),
                  pl.BlockSpec((1, GH), lambda i: (0, 0)),
                  pl.BlockSpec((SGU_G, SGU_BLOCK, SGU_BLOCK), lambda i: (0, 0, 0)),
                  pl.BlockSpec((SGU_G, SGU_BLOCK, 1), lambda i: (0, 0, 0))],
        out_specs=pl.BlockSpec((SGU_BLOCK, GH), lambda i: (i, 0)),
        out_shape=_sds((T, GH), BF16), compiler_params=_params(("parallel",)))


def sgu_bwd(pre, dy, gain, w_s, b_s, name, tm=SGU_BLOCK, comm=None):
    T = pre.shape[0]
    n = T // tm

    def body(pre_ref, dy_ref, gain_ref, ws_ref, bs_ref, dpre_ref, dws_ref, dbs_ref, dgain_ref, gacc_ref):
        i = pl.program_id(0)

        @pl.when(i == 0)
        def _():
            dws_ref[...] = jnp.zeros_like(dws_ref)
            dbs_ref[...] = jnp.zeros_like(dbs_ref)
            gacc_ref[...] = jnp.zeros_like(gacc_ref)

        mask = _sgu_mask()
        gain_v = gain_ref[...]
        for sb in range(tm // SGU_BLOCK):
            rows = slice(sb * SGU_BLOCK, (sb + 1) * SGU_BLOCK)
            u, du_dpre = _gelu_and_grad(pre_ref[rows, :GH].astype(F32))
            va, dva_dpre = _gelu_and_grad(pre_ref[rows, GH:].astype(F32))
            r = lax.rsqrt(jnp.mean(va * va, axis=-1, keepdims=True) + EPS)
            vhat = va * r
            vn = (vhat * gain_v).astype(BF16)
            dyf = dy_ref[rows, :].astype(F32)
            dvn_parts = []
            for grp in range(SGU_G):
                sl = slice(grp * SGU_GD, (grp + 1) * SGU_GD)
                wm = jnp.where(mask, ws_ref[grp], 0.0).astype(BF16)
                vm = _dot(wm, vn[:, sl], NN) + bs_ref[grp]
                dpre_ref[rows, sl] = ((dyf[:, sl] * vm) * du_dpre[:, sl]).astype(BF16)
                dvm = dyf[:, sl] * u[:, sl]
                dbs_ref[grp] += jnp.sum(dvm, axis=-1, keepdims=True)
                dvm16 = dvm.astype(BF16)
                dws_ref[grp] += jnp.where(mask, _dot(dvm16, vn[:, sl], NT), 0.0)
                dvn_parts.append(_dot(wm, dvm16, TN))
            dvn = jnp.concatenate(dvn_parts, axis=-1)
            gacc_ref[...] += (dvn * vhat).reshape(SGU_BLOCK // 8, 8, GH).sum(axis=0)
            dvhat = dvn * gain_v
            dva = r * (dvhat - vhat * jnp.mean(dvhat * vhat, axis=-1, keepdims=True))
            dpre_ref[rows, GH:] = (dva * dva_dpre).astype(BF16)

        @pl.when(i == n - 1)
        def _():
            dgain_ref[...] = jnp.sum(gacc_ref[...], axis=0, keepdims=True)

    const3 = lambda i: (0, 0, 0)
    return _call(
        body, comm, 4, (pre, dy, gain, w_s, b_s), name=name, grid=(n,),
        in_specs=[pl.BlockSpec((tm, 2 * GH), lambda i: (i, 0)),
                  pl.BlockSpec((tm, GH), lambda i: (i, 0)),
                  pl.BlockSpec((1, GH), lambda i: (0, 0)),
                  pl.BlockSpec((SGU_G, SGU_BLOCK, SGU_BLOCK), const3),
                  pl.BlockSpec((SGU_G, SGU_BLOCK, 1), const3)],
        out_specs=[pl.BlockSpec((tm, 2 * GH), lambda i: (i, 0)),
                   pl.BlockSpec((SGU_G, SGU_BLOCK, SGU_BLOCK), const3),
                   pl.BlockSpec((SGU_G, SGU_BLOCK, 1), const3),
                   pl.BlockSpec((1, GH), lambda i: (0, 0))],
        out_shape=[_sds((T, 2 * GH), BF16), _sds((SGU_G, SGU_BLOCK, SGU_BLOCK), F32),
                   _sds((SGU_G, SGU_BLOCK, 1), F32), _sds((1, GH), F32)],
        scratch_shapes=[pltpu.VMEM((8, GH), F32)],
        compiler_params=_params(("arbitrary",)))


DIAG = 768


def _diag_onehot():
    n = lax.broadcasted_iota(jnp.int32, (N_REL, DIAG), 1)
    r = lax.broadcasted_iota(jnp.int32, (N_REL, DIAG), 0)
    idx = jnp.clip(KW - 1 - n, REL_MIN, REL_MAX) - REL_MIN
    return (idx == r).astype(BF16)


def _split3(v):
    hi = v.astype(BF16)
    r1 = v - hi.astype(F32)
    mid = r1.astype(BF16)
    lo = (r1 - mid.astype(F32)).astype(BF16)
    return hi, mid, lo


def bias_build(rel_bias, name):
    def body(rb_ref, o_ref):
        oh = _diag_onehot()
        hi, mid, lo = _split3(rb_ref[...])
        u = (_dot(hi, oh, NN) + _dot(mid, oh, NN) + _dot(lo, oh, NN)) * LOG2E
        j = lax.broadcasted_iota(jnp.int32, (1, KW), 1)

        def row(i, carry):
            val = pltpu.roll(u, (i + (DIAG - QB + 1)) % DIAG, 1)[:, :KW]
            rel = lax.shift_right_logical(i, 6) - lax.shift_right_logical(j, 6) + 8
            ok = (rel >= 0) & (rel <= 8)
            o_ref[i] = jnp.where(ok, val, NEG)
            return carry

        lax.fori_loop(0, QB, row, 0)

    return _pallas(body, name=name, out_shape=_sds((QB, N_HEADS, KW), F32),
                   in_specs=[pl.BlockSpec(memory_space=pltpu.VMEM)],
                   out_specs=pl.BlockSpec(memory_space=pltpu.VMEM))(rel_bias)


def bias_grad(dwb, name):
    def body(d_ref, o_ref):
        def row(i, acc):
            return acc + pltpu.roll(d_ref[i], QB - 1 - i, 1)

        du = lax.fori_loop(0, QB, row, jnp.zeros((N_HEADS, DIAG), F32))
        oh = _diag_onehot()
        hi, mid, lo = _split3(du)
        o_ref[...] = _dot(hi, oh, NT) + _dot(mid, oh, NT) + _dot(lo, oh, NT)

    return _pallas(body, name=name, out_shape=_sds((N_HEADS, N_REL), F32),
                   in_specs=[pl.BlockSpec(memory_space=pltpu.VMEM)],
                   out_specs=pl.BlockSpec(memory_space=pltpu.VMEM))(dwb)


LOG2E = 1.4426950408889634
Q_SCALE = SCALE * LOG2E


def _attn_block(qkv_ref, blk, masked):
    r0 = pl.multiple_of(blk * QB, QB)
    qs = qkv_ref[0, pl.ds(r0 + FRONT, QB), :]
    kvalid = (lax.broadcasted_iota(jnp.int32, (1, KW), 1) >= PAD - blk * QB) if masked else None
    return r0, qs, kvalid


def _step_windows(qkv_ref, b):
    r0 = pl.multiple_of(b * ATTN_STEP, ATTN_STEP) + (FRONT - PAD)
    out = []
    for part in (1, 2):
        a = qkv_ref[part, pl.ds(r0, PAD + ATTN_STEP), :]
        zero = jnp.zeros_like(a)
        out.append([jnp.where(_head_mask(h), a, zero) for h in range(2)])
    return out


def _window(stacks, t):
    return jnp.concatenate([s[t * QB:t * QB + KW] for s in stacks], axis=0)


def _head_mask(h):
    lane = lax.broadcasted_iota(jnp.int32, (1, 2 * HEAD_DIM), 1)
    return (lane < HEAD_DIM) if h == 0 else (lane >= HEAD_DIM)


def _stack_heads(a):
    zero = jnp.zeros_like(a)
    return jnp.concatenate([jnp.where(_head_mask(0), a, zero), jnp.where(_head_mask(1), a, zero)], axis=0)


def _rows_by_head(a):
    return jnp.concatenate([a[:, :KW], a[:, KW:]], axis=0)


def _per_head(lo, hi):
    return jnp.where(_head_mask(0), lo, hi)


def _attn_exp(qs, kst, w_ref, kvalid):
    s = _dot(qs, kst, NT) + jnp.concatenate([w_ref[0], w_ref[1]], axis=1)
    if kvalid is not None:
        s = jnp.where(jnp.concatenate([kvalid, kvalid], axis=1), s, NEG)
    es, invs = [], []
    for h in range(2):
        sh = s[:, h * KW:(h + 1) * KW]
        eh = jnp.exp2(sh - jnp.max(sh, axis=-1, keepdims=True))
        es.append(eh)
        invs.append(1.0 / jnp.sum(eh, axis=-1, keepdims=True))
    return jnp.concatenate(es, axis=1), invs


ATTN_G = 8
ATTN_STEP = QB * ATTN_G


def _masked_and_not(b, fn):
    n_masked = -(-PAD // ATTN_STEP)
    pl.when(b < n_masked)(functools.partial(fn, True))
    pl.when(b >= n_masked)(functools.partial(fn, False))


def attn_fwd(qkvp, wb, name, comm=None):
    T = qkvp.shape[1] - FRONT

    def body(qkv_ref, w_ref, o_ref):
        b = pl.program_id(1)

        def blocks(masked):
            keys, values = _step_windows(qkv_ref, b)
            for t in range(ATTN_G):
                _, qs, kvalid = _attn_block(qkv_ref, b * ATTN_G + t, masked)
                e, inv = _attn_exp(qs, _window(keys, t), w_ref, kvalid)
                o = _dot(e.astype(BF16), _window(values, t), NN) * _per_head(*inv)
                o_ref[t * QB:(t + 1) * QB, :] = o.astype(BF16)

        _masked_and_not(b, blocks)

    return _call(
        body, comm, 1, (qkvp, wb), name=name, grid=(N_HEADS // 2, T // (QB * ATTN_G)),
        in_specs=[pl.BlockSpec((3, FRONT + T, 2 * HEAD_DIM), lambda hp, b: (0, 0, hp)),
                  pl.BlockSpec((2, QB, KW), lambda hp, b: (hp, 0, 0))],
        out_specs=pl.BlockSpec((QB * ATTN_G, 2 * HEAD_DIM), lambda hp, b: (b, hp)),
        out_shape=_sds((T, D), BF16),
        compiler_params=_params(("parallel", "arbitrary"), VMEM_BIG))


def attn_bwd(qkvp, o, do, wb, name, comm=None):
    T = qkvp.shape[1] - FRONT
    nb = T // (QB * ATTN_G)

    def body(qkv_ref, o_ref, do_ref, w_ref, dqkv_ref, dw_ref, dk_acc, dv_acc):
        b = pl.program_id(1)

        @pl.when(b == 0)
        def _():
            dk_acc[...] = jnp.zeros_like(dk_acc)
            dv_acc[...] = jnp.zeros_like(dv_acc)
            dw_ref[...] = jnp.zeros_like(dw_ref)
            dqkv_ref[0, 0:FRONT, :] = jnp.zeros((FRONT, 2 * HEAD_DIM), BF16)

        def blocks(masked):
            dws = None
            keys, values = _step_windows(qkv_ref, b)
            for t in range(ATTN_G):
                r0, qs, kvalid = _attn_block(qkv_ref, b * ATTN_G + t, masked)
                kst = _window(keys, t)
                e, inv = _attn_exp(qs, kst, w_ref, kvalid)
                do2 = do_ref[t * QB:(t + 1) * QB, :]
                dof = do2.astype(F32)
                prod = dof * o_ref[t * QB:(t + 1) * QB, :].astype(F32)
                dp = _dot(do2, _window(values, t), NT)
                parts = []
                for h in range(2):
                    delta = jnp.sum(jnp.where(_head_mask(h), prod, 0.0), axis=-1, keepdims=True)
                    half = slice(h * KW, (h + 1) * KW)
                    parts.append(e[:, half] * ((dp[:, half] - delta) * inv[h]))
                ds = jnp.concatenate(parts, axis=1)
                dws = ds if dws is None else dws + ds
                ds16 = ds.astype(BF16)
                dqkv_ref[0, pl.ds(r0 + FRONT, QB), :] = (_dot(ds16, kst, NN) * SCALE).astype(BF16)
                dk_acc[pl.ds(r0 + (FRONT - PAD), KW), :] += _dot(_rows_by_head(ds16), _stack_heads(qs), TN)
                dv_acc[pl.ds(r0 + (FRONT - PAD), KW), :] += _dot(
                    _rows_by_head(e.astype(BF16)), _stack_heads((dof * _per_head(*inv)).astype(BF16)), TN)
            dw_ref[0] += dws[:, :KW]
            dw_ref[1] += dws[:, KW:]

        _masked_and_not(b, blocks)

        @pl.when(b == nb - 1)
        def _():
            dqkv_ref[1] = (dk_acc[...] * (1.0 / LOG2E)).astype(BF16)
            dqkv_ref[2] = dv_acc[...].astype(BF16)

    slab = pl.BlockSpec((3, FRONT + T, 2 * HEAD_DIM), lambda hp, b: (0, 0, hp))
    wspec = pl.BlockSpec((2, QB, KW), lambda hp, b: (hp, 0, 0))
    rows = pl.BlockSpec((QB * ATTN_G, 2 * HEAD_DIM), lambda hp, b: (b, hp))
    return _call(
        body, comm, 2, (qkvp, o, do, wb), name=name, grid=(N_HEADS // 2, nb),
        in_specs=[slab, rows, rows, wspec],
        out_specs=[slab, wspec],
        out_shape=[_sds((3, FRONT + T, D), BF16), _sds((N_HEADS, QB, KW), F32)],
        scratch_shapes=[pltpu.VMEM((FRONT + T, 2 * HEAD_DIM), F32), pltpu.VMEM((FRONT + T, 2 * HEAD_DIM), F32)],
        compiler_params=_params(("parallel", "arbitrary"), VMEM_BIG))


def proj_qkv(hn, w, l, name, tm=512, comm=None):
    T = hn.shape[0]
    pb = FRONT // tm

    def body(a_ref, b_ref, o_ref):
        i = pl.program_id(1)

        @pl.when(i < pb)
        def _():
            o_ref[...] = jnp.zeros_like(o_ref)

        @pl.when(i >= pb)
        def _():
            scale = jnp.where(pl.program_id(0) == 0, Q_SCALE, 1.0).astype(F32)
            o_ref[...] = (_dot(a_ref[...], b_ref[...], NN) * scale).astype(BF16)

    return _call(
        body, comm, 1, (hn, w), name=name, grid=(3, pb + T // tm),
        in_specs=[pl.BlockSpec((tm, D), lambda p, i: (jnp.maximum(i - pb, 0), 0)),
                  pl.BlockSpec((None, D, D), lambda p, i: (l, 0, p))],
        out_specs=pl.BlockSpec((None, tm, D), lambda p, i: (p, i, 0)),
        out_shape=_sds((3, FRONT + T, D), BF16),
        compiler_params=_params(("parallel", "parallel"), VMEM_BIG))


def ffn_up(hn, wg, wu, l, name, tm=1024, comm=None):
    T = hn.shape[0]

    def body(a_ref, wg_ref, wu_ref, g_ref, u_ref, h_ref):
        a = a_ref[...]
        g = _dot(a, wg_ref[...], NT)
        u = _dot(a, wu_ref[...], NT)
        s = _sigmoid(g)
        silu = g * s
        g_ref[...] = (u * (s * (1.0 + g * (1.0 - s)))).astype(BF16)
        u_ref[...] = silu.astype(BF16)
        h_ref[...] = (silu * u).astype(BF16)

    wspec = pl.BlockSpec((None, None, FS, D), lambda s, i: (l, s, 0, 0))
    ospec = pl.BlockSpec((None, tm, FS), lambda s, i: (s, i, 0))
    return _call(
        body, comm, 3, (hn, wg, wu), name=name, grid=(N_CHIPS, T // tm),
        in_specs=[pl.BlockSpec((tm, D), lambda s, i: (i, 0)), wspec, wspec],
        out_specs=[ospec, ospec, ospec],
        out_shape=[_sds((N_CHIPS, T, FS), BF16)] * 3,
        compiler_params=_params(("parallel", "parallel"), VMEM_BIG))


def ffn_bwd_dh(dxb, wd, g, u, l, name, tm=2048, comm=None):
    T = dxb.shape[0]
    tm = min(tm, T)

    def body(a_ref, wd_ref, g_ref, u_ref, dg_ref, du_ref):
        dh = _dot(a_ref[...], wd_ref[...], NT)
        dg_ref[...] = (dh * g_ref[...].astype(F32)).astype(BF16)
        du_ref[...] = (dh * u_ref[...].astype(F32)).astype(BF16)

    aspec = pl.BlockSpec((None, tm, FS), lambda i, s: (s, i, 0))
    return _call(
        body, comm, 2, (dxb, wd, g, u), name=name, grid=(T // tm, N_CHIPS),
        in_specs=[pl.BlockSpec((tm, D), lambda i, s: (i, 0)),
                  pl.BlockSpec((None, None, FS, D), lambda i, s: (l, s, 0, 0)), aspec, aspec],
        out_specs=[aspec, aspec],
        out_shape=[_sds((N_CHIPS, T, FS), BF16)] * 2,
        compiler_params=_params(("parallel", "parallel"), VMEM_BIG))


def ffn_dgrad(dg, du, wg, wu, tm=512):
    def compute(dg_ref, du_ref, wg_ref, wu_ref):
        d = None
        for s in range(N_CHIPS):
            t = _dot(dg_ref[s], wg_ref[s], NN) + _dot(du_ref[s], wu_ref[s], NN)
            d = t if d is None else d + t
        return d

    aspec = pl.BlockSpec((N_CHIPS, tm, FS), lambda i: (0, i, 0))
    wspec = pl.BlockSpec((None, N_CHIPS, FS, D), lambda i: (0, 0, 0, 0), pipeline_mode=pl.Buffered(1))
    return compute, (dg, du, wg, wu), [aspec, aspec, wspec, wspec]


def qkv_dgrad(dqkvp, w, tm=512):
    def compute(a_ref, w_ref):
        d = None
        for p in range(3):
            t = _dot(a_ref[p], w_ref[:, p * D:(p + 1) * D], NT)
            d = t if d is None else d + t
        return d

    return compute, (dqkvp, w), [pl.BlockSpec((3, tm, D), lambda i: (0, i + FRONT // tm, 0)),
                                 pl.BlockSpec((None, D, 3 * D), lambda i: (0, 0, 0))]


def in_dgrad(dpre, w, tm=512):
    def compute(a_ref, w_ref):
        return _dot(a_ref[...], w_ref[...], NT)

    return compute, (dpre, w), [pl.BlockSpec((tm, 2 * GH), lambda i: (i, 0)),
                                pl.BlockSpec((None, D, 2 * GH), lambda i: (0, 0, 0))]


def _rms_rows(x, g):
    r = lax.rsqrt(jnp.mean(x * x, axis=-1, keepdims=True) + EPS)
    return ((x * r) * g).astype(BF16)


def residual_proj(name, compute, args, specs, res, norm_g, tm=512, comm=None):
    T = res.shape[0]
    k = len(args)
    with_norm = norm_g is not None

    def body(*refs):
        d = refs[k][...] + compute(*refs[:k])
        if with_norm:
            refs[k + 2][...] = d
            refs[k + 3][...] = _rms_rows(d, refs[k + 1][...])
        else:
            refs[k + 1][...] = d

    row = pl.BlockSpec((tm, D), lambda i: (i, 0))
    vec = pl.BlockSpec((1, D), lambda i: (0, 0))
    if with_norm:
        return _call(body, comm, 2, (*args, res, norm_g), name=name, grid=(T // tm,),
                     in_specs=list(specs) + [row, vec], out_specs=[row, row],
                     out_shape=[_sds((T, D), F32), _sds((T, D), BF16)],
                     compiler_params=_params(("parallel",), VMEM_BIG))
    return _call(body, comm, 1, (*args, res), name=name, grid=(T // tm,), in_specs=list(specs) + [row],
                 out_specs=row, out_shape=_sds((T, D), F32), compiler_params=_params(("parallel",), VMEM_BIG))


def ffn_down(h, wd, tm=512):
    def compute(h_ref, wd_ref):
        d = None
        for s in range(N_CHIPS):
            t = _dot(h_ref[s], wd_ref[s], NN)
            d = t if d is None else d + t
        return d

    return compute, (h, wd), [pl.BlockSpec((N_CHIPS, tm, FS), lambda i: (0, i, 0)),
                              pl.BlockSpec((None, N_CHIPS, FS, D), lambda i: (0, 0, 0, 0))]


def out_proj(a, w, tm=512):
    K = a.shape[1]

    def compute(a_ref, w_ref):
        return _dot(a_ref[...], w_ref[...], NN)

    return compute, (a, w), [pl.BlockSpec((tm, K), lambda i: (i, 0)), pl.BlockSpec((None, K, D), lambda i: (0, 0, 0))]


def adamw(w, g, m, v, name):
    L, R, C = w.shape

    def body(w_ref, g_ref, m_ref, v_ref, go_ref, d_ref, nm_ref, nv_ref):
        gf = g_ref[...]
        go_ref[...] = gf
        nm = ADAM_B1 * m_ref[...] + (1.0 - ADAM_B1) * gf
        nv = ADAM_B2 * v_ref[...] + (1.0 - ADAM_B2) * (gf * gf)
        m_hat = nm / (1.0 - ADAM_B1 ** ADAM_STEP)
        v_hat = nv / (1.0 - ADAM_B2 ** ADAM_STEP)
        d_ref[...] = -ADAM_LR * (m_hat / (jnp.sqrt(v_hat) + ADAM_EPS) + ADAM_WD * w_ref[...])
        nm_ref[...] = nm
        nv_ref[...] = nv

    tr = R // 4 if R % 32 == 0 else R
    spec = pl.BlockSpec((None, tr, C), lambda l, r: (l, r, 0))
    return _pallas(body, name=name, grid=(L, R // tr), in_specs=[spec] * 4, out_specs=[spec] * 4,
                   out_shape=[_sds((L, R, C), F32)] * 4,
                   compiler_params=_params(("parallel", "parallel")))(w, g, m, v)


def _coords():
    return lax.axis_index("x"), lax.axis_index("y"), lax.axis_index("c")


def _other_chips(x, y):
    out = []
    for fx, fy in ((1, 0), (0, 1), (1, 1)):
        px = (1 - x) if fx else x
        py = (1 - y) if fy else y
        out.append((px, py))
    return out


def _flip_index(s, j):
    sx, sy = s // 2, s % 2
    fx, fy = ((1, 0), (0, 1), (1, 1))[j]
    return 2 * (sx ^ fx) + (sy ^ fy)


def _for_my_chip(sme, fn):
    for s in range(N_CHIPS):
        pl.when(sme == s)(functools.partial(fn, s))


ANY = pl.BlockSpec(memory_space=pl.ANY)

GATHER_KIND = {"a_w_in": "col", "b_w_qkv": "col", "a_w_out": "row", "b_w_out": "row",
               "ffn_w_gate": "row", "ffn_w_up": "row", "ffn_w_down": "row"}
BIG = tuple(GATHER_KIND)


def _gathered_shape(kind, shape):
    L, R, C = shape
    return (L, R, N_CHIPS * C) if kind == "col" else (L, N_CHIPS, R, C)


def _shard_rows(ref, kind, s, r0, rn, C):
    if kind == "col":
        return ref.at[:, pl.ds(r0, rn), s * C:(s + 1) * C]
    return ref.at[:, s, pl.ds(r0, rn), :]


def gather_stage1(items):
    n = len(items)
    dims = [it[0].shape[1:] for it in items]

    def copies(ins, outs, sems, s, with_landed=True):
        lsem, ssem, rsem = sems
        x, y, c = _coords()
        chips = _other_chips(x, y)
        local, send, landed = [], [], []
        for t, (_, li, kind) in enumerate(items):
            R, C = dims[t]
            r0 = pl.multiple_of(c * (R // 2), 8)
            local.append(pltpu.make_async_copy(ins[t].at[pl.ds(li, 1)], _shard_rows(outs[t], kind, s, 0, R, C),
                                               lsem.at[t]))
            for j in range(3):
                pair = dict(send_sem=ssem.at[3 * t + j], recv_sem=rsem.at[3 * t + j],
                            device_id=(chips[j][0], chips[j][1], c), device_id_type=MESH)
                send.append(pltpu.make_async_remote_copy(
                    src_ref=ins[t].at[pl.ds(li, 1), pl.ds(r0, R // 2), :],
                    dst_ref=_shard_rows(outs[t], kind, s, r0, R // 2, C), **pair))
                if with_landed:
                    got = _shard_rows(outs[t], kind, _flip_index(s, j), r0, R // 2, C)
                    landed.append(pltpu.make_async_remote_copy(src_ref=got, dst_ref=got, **pair))
        return local, send, landed

    def start(ins, outs, sems):
        def run(s):
            local, send, _ = copies(ins, outs, sems, s, with_landed=False)
            for cp in local + send:
                cp.start()
        x, y, _ = _coords()
        _for_my_chip(2 * x + y, run)

    def wait(ins, outs, sems):
        def run(s):
            local, send, landed = copies(ins, outs, sems, s)
            for cp in landed:
                cp.wait_recv()
            for cp in send:
                cp.wait_send()
            for cp in local:
                cp.wait()
        x, y, _ = _coords()
        _for_my_chip(2 * x + y, run)

    out_shapes = [_sds(_gathered_shape(kind, (1,) + tuple(dims[t])), BF16) for t, (_, _, kind) in enumerate(items)]
    sems = [pltpu.SemaphoreType.DMA((n,)), pltpu.SemaphoreType.DMA((3 * n,)), pltpu.SemaphoreType.DMA((3 * n,))]
    return Comm([it[0] for it in items], out_shapes, sems, start, wait)


def gather_stage2(items, gathered):
    n = len(items)
    dims = [it[0].shape[1:] for it in items]

    def copies(outs, sems, s, with_landed=True):
        ssem, rsem = sems
        x, y, c = _coords()
        send, landed = [], []
        for t, (_, _, kind) in enumerate(items):
            R, C = dims[t]
            for j in range(3):
                pair = dict(send_sem=ssem.at[3 * t + j], recv_sem=rsem.at[3 * t + j],
                            device_id=(x, y, 1 - c), device_id_type=MESH)
                mine = _shard_rows(outs[t], kind, _flip_index(s, j), pl.multiple_of(c * (R // 2), 8), R // 2, C)
                send.append(pltpu.make_async_remote_copy(src_ref=mine, dst_ref=mine, **pair))
                if with_landed:
                    other = _shard_rows(outs[t], kind, _flip_index(s, j), pl.multiple_of((1 - c) * (R // 2), 8),
                                        R // 2, C)
                    landed.append(pltpu.make_async_remote_copy(src_ref=other, dst_ref=other, **pair))
        return send, landed

    def start(ins, outs, sems):
        def run(s):
            for cp in copies(outs, sems, s, with_landed=False)[0]:
                cp.start()
        x, y, _ = _coords()
        _for_my_chip(2 * x + y, run)

    def wait(ins, outs, sems):
        def run(s):
            send, landed = copies(outs, sems, s)
            for cp in landed:
                cp.wait_recv()
            for cp in send:
                cp.wait_send()
        x, y, _ = _coords()
        _for_my_chip(2 * x + y, run)

    out_shapes = [_sds(g.shape, BF16) for g in gathered]
    sems = [pltpu.SemaphoreType.DMA((3 * n,)), pltpu.SemaphoreType.DMA((3 * n,))]
    return Comm(gathered, out_shapes, sems, start, wait, aliases={t: t for t in range(n)})


def _half_shape(kind, R, C):
    return (R // 2, N_CHIPS * C) if kind == "col" else (N_CHIPS, R // 2, C)


def exchange_halves(grads, metas):
    n = len(grads)

    def copies(ins, outs, sems):
        ssem, rsem = sems
        x, y, c = _coords()
        out = []
        for t, (kind, R, C) in enumerate(metas):
            r0 = pl.multiple_of((1 - c) * (R // 2), 8)
            src = ins[t].at[pl.ds(r0, R // 2), :] if kind == "col" else ins[t].at[:, pl.ds(r0, R // 2), :]
            out.append(pltpu.make_async_remote_copy(
                src_ref=src, dst_ref=outs[t], send_sem=ssem.at[t], recv_sem=rsem.at[t],
                device_id=(x, y, 1 - c), device_id_type=MESH))
        return out

    def start(ins, outs, sems):
        for cp in copies(ins, outs, sems):
            cp.start()

    def wait(ins, outs, sems):
        for cp in copies(ins, outs, sems):
            cp.wait()

    return Comm(grads, [_sds(_half_shape(*m), F32) for m in metas], [pltpu.SemaphoreType.DMA((n,))] * 2, start, wait)


def pair_sum(me, g, sib, meta, name):
    kind, R, C = meta
    h = R // 2

    def body(me_ref, g_ref, sib_ref, p16_ref, own_ref):
        s = pl.program_id(0)
        v = g_ref[...] + sib_ref[...]
        p16_ref[...] = v.astype(BF16)

        @pl.when(s == me_ref[1])
        def _():
            own_ref[...] = v

    if kind == "col":
        gspec = pl.BlockSpec((h, C), lambda s, me_ref: (me_ref[0], s))
        sspec = pl.BlockSpec((h, C), lambda s, me_ref: (0, s))
    else:
        gspec = pl.BlockSpec((None, h, C), lambda s, me_ref: (s, me_ref[0], 0))
        sspec = pl.BlockSpec((None, h, C), lambda s, me_ref: (s, 0, 0))
    grid_spec = pltpu.PrefetchScalarGridSpec(
        num_scalar_prefetch=1, grid=(N_CHIPS,), in_specs=[gspec, sspec],
        out_specs=[sspec, pl.BlockSpec((h, C), lambda s, me_ref: (0, 0))])
    return _pallas(body, name=name, grid_spec=grid_spec,
                   out_shape=[_sds(_half_shape(*meta), BF16), _sds((h, C), F32)],
                   compiler_params=_params(("arbitrary",), VMEM_BIG))(me, g, sib)


def scatter_partials(p16s, metas):
    n = len(p16s)

    def copies(ins, outs, sems, s):
        ssem, rsem = sems
        x, y, c = _coords()
        chips = _other_chips(x, y)
        out = []
        for t, (kind, R, C) in enumerate(metas):
            for j in range(3):
                sj = _flip_index(s, j)
                src = ins[t].at[:, sj * C:(sj + 1) * C] if kind == "col" else ins[t].at[sj]
                out.append(pltpu.make_async_remote_copy(
                    src_ref=src, dst_ref=outs[t].at[j], send_sem=ssem.at[3 * t + j], recv_sem=rsem.at[3 * t + j],
                    device_id=(chips[j][0], chips[j][1], c), device_id_type=MESH))
        return out

    def start(ins, outs, sems):
        def run(s):
            for cp in copies(ins, outs, sems, s):
                cp.start()
        x, y, _ = _coords()
        _for_my_chip(2 * x + y, run)

    def wait(ins, outs, sems):
        def run(s):
            for cp in copies(ins, outs, sems, s):
                cp.wait()
        x, y, _ = _coords()
        _for_my_chip(2 * x + y, run)

    return Comm(p16s, [_sds((3, R // 2, C), BF16) for (_, R, C) in metas],
                [pltpu.SemaphoreType.DMA((3 * n,))] * 2, start, wait)


def final_sum(me, own, q, buf, l, meta, name):
    _, R, C = meta
    h = R // 2

    def body(me_ref, own_ref, q_ref, buf_ref, o_ref):
        del buf_ref
        o_ref[...] = ((own_ref[...] + q_ref[0].astype(F32)) + q_ref[1].astype(F32)) + q_ref[2].astype(F32)

    grid_spec = pltpu.PrefetchScalarGridSpec(
        num_scalar_prefetch=1, grid=(1,),
        in_specs=[pl.BlockSpec((h, C), lambda i, me_ref: (0, 0)),
                  pl.BlockSpec((3, h, C), lambda i, me_ref: (0, 0, 0)), ANY],
        out_specs=pl.BlockSpec((None, h, C), lambda i, me_ref: (l, me_ref[0], 0)))
    return _pallas(body, name=name, grid_spec=grid_spec, out_shape=_sds(buf.shape, F32),
                   input_output_aliases={3: 0},
                   compiler_params=_params(("arbitrary",), VMEM_BIG))(me, own, q, buf)


def share_final(bufs):
    n = len(bufs)

    def body(*refs):
        ins, outs = refs[:n], refs[n:2 * n]
        ssem, rsem = refs[2 * n:]
        del ins
        x, y, c = _coords()
        copies = []
        for t in range(n):
            R = bufs[t].shape[1]
            r0 = pl.multiple_of(c * (R // 2), 8)
            blk = outs[t].at[:, pl.ds(r0, R // 2), :]
            copies.append(pltpu.make_async_remote_copy(
                src_ref=blk, dst_ref=blk, send_sem=ssem.at[t], recv_sem=rsem.at[t],
                device_id=(x, y, 1 - c), device_id_type=MESH))
        for cp in copies:
            cp.start()
        for t in range(n):
            R = bufs[t].shape[1]
            r1 = pl.multiple_of((1 - c) * (R // 2), 8)
            other = outs[t].at[:, pl.ds(r1, R // 2), :]
            pltpu.make_async_remote_copy(
                src_ref=other, dst_ref=other, send_sem=ssem.at[t], recv_sem=rsem.at[t],
                device_id=(x, y, 1 - c), device_id_type=MESH).wait_recv()
        for cp in copies:
            cp.wait_send()

    out_shape = [_sds(b.shape, F32) for b in bufs]
    return _pallas(body, name="share_final", in_specs=[ANY] * n, out_specs=[ANY] * n, out_shape=out_shape,
                   input_output_aliases={t: t for t in range(n)},
                   scratch_shapes=[pltpu.SemaphoreType.DMA((n,))] * 2,
                   compiler_params=pltpu.CompilerParams(has_side_effects=True))(*bufs)


def allreduce_small(part):
    rows = part.shape[0]
    h = rows // 2

    def body(p_ref, o_ref, sib_buf, pair_buf, chip_buf, ssem, rsem):
        x, y, c = _coords()
        sibling = dict(device_id=(x, y, 1 - c), device_id_type=MESH)
        mine = pl.ds(pl.multiple_of(c * h, 8), h)
        theirs = pl.ds(pl.multiple_of((1 - c) * h, 8), h)

        swap = pltpu.make_async_remote_copy(src_ref=p_ref.at[theirs], dst_ref=sib_buf, send_sem=ssem.at[0],
                                            recv_sem=rsem.at[0], **sibling)
        swap.start()
        swap.wait()
        pair_buf[...] = p_ref[mine, :] + sib_buf[...]

        chips = _other_chips(x, y)
        sends = [pltpu.make_async_remote_copy(src_ref=pair_buf, dst_ref=chip_buf.at[j], send_sem=ssem.at[1 + j],
                                              recv_sem=rsem.at[1 + j], device_id=(chips[j][0], chips[j][1], c),
                                              device_id_type=MESH) for j in range(3)]
        for cp in sends:
            cp.start()
        for cp in sends:
            cp.wait()

        def total(s):
            terms = {s: pair_buf[...]}
            for j in range(3):
                terms[_flip_index(s, j)] = chip_buf[j]
            o_ref[mine, :] = ((terms[0] + terms[1]) + terms[2]) + terms[3]

        _for_my_chip(2 * x + y, total)

        back = pltpu.make_async_remote_copy(src_ref=o_ref.at[mine], dst_ref=o_ref.at[mine], send_sem=ssem.at[4],
                                            recv_sem=rsem.at[4], **sibling)
        back.start()
        pltpu.make_async_remote_copy(src_ref=o_ref.at[theirs], dst_ref=o_ref.at[theirs], send_sem=ssem.at[4],
                                     recv_sem=rsem.at[4], **sibling).wait_recv()
        back.wait_send()

    return _pallas(body, name="allreduce_small",
                   in_specs=[pl.BlockSpec(memory_space=pltpu.VMEM)], out_specs=pl.BlockSpec(memory_space=pltpu.VMEM),
                   out_shape=_sds((rows, 128), F32),
                   scratch_shapes=[pltpu.VMEM((h, 128), F32), pltpu.VMEM((h, 128), F32), pltpu.VMEM((3, h, 128), F32),
                                   pltpu.SemaphoreType.DMA((5,)), pltpu.SemaphoreType.DMA((5,))],
                   compiler_params=pltpu.CompilerParams(has_side_effects=True))(part)


def _rows128(a):
    flat = a.reshape(-1)
    rows = -(-flat.shape[0] // 128)
    rows8 = -(-rows // 8) * 8
    flat = jnp.pad(flat, (0, rows8 * 128 - flat.shape[0]))
    return flat.reshape(rows8, 128)


def kernel(x, norm_mix_g, norm_ffn_g, final_g, a_w_in, a_v_gain, a_w_s, a_b_s, a_w_out, b_w_qkv, b_rel_bias, b_w_out, ffn_w_gate, ffn_w_up, ffn_w_down, loss_target, m_norm_mix_g, m_norm_ffn_g, m_final_g, m_a_w_in, m_a_v_gain, m_a_w_s, m_a_b_s, m_a_w_out, m_b_w_qkv, m_b_rel_bias, m_b_w_out, m_ffn_w_gate, m_ffn_w_up, m_ffn_w_down, v_norm_mix_g, v_norm_ffn_g, v_final_g, v_a_w_in, v_a_v_gain, v_a_w_s, v_a_b_s, v_a_w_out, v_b_w_qkv, v_b_rel_bias, v_b_w_out, v_ffn_w_gate, v_ffn_w_up, v_ffn_w_down):
    T = x.shape[1]
    weights = dict(norm_mix_g=norm_mix_g, norm_ffn_g=norm_ffn_g, final_g=final_g, a_w_in=a_w_in, a_v_gain=a_v_gain,
                   a_w_s=a_w_s, a_b_s=a_b_s, a_w_out=a_w_out, b_w_qkv=b_w_qkv, b_rel_bias=b_rel_bias,
                   b_w_out=b_w_out, ffn_w_gate=ffn_w_gate, ffn_w_up=ffn_w_up, ffn_w_down=ffn_w_down)
    mom_m = dict(norm_mix_g=m_norm_mix_g, norm_ffn_g=m_norm_ffn_g, final_g=m_final_g, a_w_in=m_a_w_in,
                 a_v_gain=m_a_v_gain, a_w_s=m_a_w_s, a_b_s=m_a_b_s, a_w_out=m_a_w_out, b_w_qkv=m_b_w_qkv,
                 b_rel_bias=m_b_rel_bias, b_w_out=m_b_w_out, ffn_w_gate=m_ffn_w_gate, ffn_w_up=m_ffn_w_up,
                 ffn_w_down=m_ffn_w_down)
    mom_v = dict(norm_mix_g=v_norm_mix_g, norm_ffn_g=v_norm_ffn_g, final_g=v_final_g, a_w_in=v_a_w_in,
                 a_v_gain=v_a_v_gain, a_w_s=v_a_w_s, a_b_s=v_a_b_s, a_w_out=v_a_w_out, b_w_qkv=v_b_w_qkv,
                 b_rel_bias=v_b_rel_bias, b_w_out=v_b_w_out, ffn_w_gate=v_ffn_w_gate, ffn_w_up=v_ffn_w_up,
                 ffn_w_down=v_ffn_w_down)
    order = list(weights)
    transposed = ("ffn_w_gate", "ffn_w_up")
    for k in transposed:
        weights[k], mom_m[k], mom_v[k] = (jnp.swapaxes(a, 1, 2) for a in (weights[k], mom_m[k], mom_v[k]))

    xi, yi, ci = _coords()
    me = jnp.stack([ci, 2 * xi + yi]).astype(jnp.int32)

    shard16 = {k: cast_bf16(weights[k], "cast_" + k) for k in BIG}

    def layer_tensors(i):
        mix = ("a_w_in", "a_w_out") if i % 2 == 0 else ("b_w_qkv", "b_w_out")
        return [(k, i // 2) for k in mix] + [(k, i) for k in ("ffn_w_gate", "ffn_w_up", "ffn_w_down")]

    def gather_items(keys):
        return [(shard16[k], l, GATHER_KIND[k]) for k, l in keys]

    def grad_metas(keys):
        return [(GATHER_KIND[k],) + tuple(weights[k].shape[1:]) for k, _ in keys]

    FFN = ("ffn_w_gate", "ffn_w_up", "ffn_w_down")
    k0a = [("a_w_out", 0), ("ffn_w_gate", 0)]
    k0b = [("ffn_w_up", 0), ("ffn_w_down", 0)]
    k1a = [("b_w_qkv", 0), ("b_w_out", 0), ("ffn_w_gate", 1)]
    k1b = [("ffn_w_up", 1), ("ffn_w_down", 1)]
    k3a = [("b_w_qkv", 1), ("b_w_out", 1), ("ffn_w_gate", 3)]
    k3b = [("ffn_w_up", 3), ("ffn_w_down", 3)]
    plans = {
        "a_in_l0": [("g1", k0a)], "sgu_fwd_l0": [("g2", k0a), ("g1", k0b)], "a_out_l0": [("g2", k0b)],
        "rms_mix_l0": [("g1", [("a_w_in", 0)])],
        "ffn_up_l0": [("g1", k1a)], "ffn_down_l0": [("g2", k1a), ("g1", k1b[:1])],
        "b_qkv_l1": [("g2", k1b[:1]), ("g1", k1b[1:])],
        "attn_fwd_l1": [("g2", k1b[1:]), ("g1", layer_tensors(2))], "b_out_l1": [("g2", layer_tensors(2))],
        "ffn_up_l1": [("g1", k3a)], "ffn_down_l1": [("g2", k3a)],
        "a_in_l2": [("g1", k3b)], "sgu_fwd_l2": [("g2", k3b)],
        "ffn_bwd_dh_l2": [("ex", layer_tensors(3))], "sgu_bwd_l2": [("sc", layer_tensors(3))],
        "ffn_bwd_dh_l1": [("ex", layer_tensors(2))], "attn_bwd_l1": [("sc", layer_tensors(2))],
        "ffn_bwd_dh_l0": [("ex", layer_tensors(1))], "ffn_bwd_dhn_l0": [("sc", k1a)],
        "dffn_w_gate_l0": [("sc", [("ffn_w_up", 1)])], "dffn_w_up_l0": [("sc", [("ffn_w_down", 1)])],
        "a_out_bwd_l0": [("ex", [(k, 0) for k in FFN])],
        "sgu_bwd_l0": [("sc", [("ffn_w_gate", 0), ("ffn_w_up", 0)]), ("ex", [("a_w_out", 0)])],
        "dw_in_l0": [("sc", [("ffn_w_down", 0), ("a_w_out", 0)])],
    }
    part16, full16 = {}, {}
    sib, p16, own_parts, recv_parts = {}, {}, {}, {}

    def make_comm(kind, keys):
        if kind == "g1":
            return gather_stage1(gather_items(keys)), lambda outs: part16.update(zip(keys, outs))
        if kind == "g2":
            return (gather_stage2(gather_items(keys), [part16[kl] for kl in keys]),
                    lambda outs: full16.update(zip(keys, outs)))
        if kind == "ex":
            return (exchange_halves([big_grads[k][l] for k, l in keys], grad_metas(keys)),
                    lambda outs: sib.update(zip(keys, outs)))
        for kl, m_ in zip(keys, grad_metas(keys)):
            p16[kl], own_parts[kl] = pair_sum(me, big_grads[kl[0]][kl[1]], sib[kl], m_, "pair_sum_%s_l%d" % kl)
        return (scatter_partials([p16[kl] for kl in keys], grad_metas(keys)),
                lambda outs: recv_parts.update(zip(keys, outs)))

    def run(name, make):
        steps = plans.get(name)
        if not steps:
            return make(None)
        made = [make_comm(kind, keys) for kind, keys in steps]
        main, outs = make(combine([c for c, _ in made]))
        for c, done in made:
            done(outs[:len(c.out_shapes)])
            outs = outs[len(c.out_shapes):]
        return main

    def weight(k, l):
        w = full16[(k, l)]
        if k == "a_w_out":
            return w.reshape(1, GH, D)
        return w.reshape(1, D, D) if k == "b_w_out" else w


    xcur = x.reshape(T, D)
    hn = run("rms_mix_l0", lambda comm: rms_fwd(xcur, norm_mix_g[0][None], "rms_mix_l0", comm=comm))
    comm, done = make_comm("g2", [("a_w_in", 0)])
    done(run_comm(comm, "gather_first_d2d"))
    saved = []
    for i in range(DEPTH):
        j = i // 2
        tag = "_l%d" % i
        st = {"x_in": xcur, "hn": hn}
        if i % 2 == 0:
            pre = run("a_in" + tag, lambda comm: matmul(
                "a_in" + tag, NN, hn, pl.BlockSpec((1024, D), lambda i_, j_: (i_, 0)),
                weight("a_w_in", j), pl.BlockSpec((None, D, 1024), lambda i_, j_: (0, 0, j_)),
                _sds((T, 2 * GH), BF16), pl.BlockSpec((1024, 1024), lambda i_, j_: (i_, j_)),
                (T // 1024, 4), comm=comm))
            y = run("sgu_fwd" + tag, lambda comm: sgu_fwd(
                pre, a_v_gain[j][None], a_w_s[j], a_b_s[j][:, :, None], "sgu_fwd" + tag, comm=comm))
            xmid, hn2 = run("a_out" + tag, lambda comm: residual_proj(
                "a_out" + tag, *out_proj(y, weight("a_w_out", j)), xcur, norm_ffn_g[i][None], comm=comm))
            st.update(pre=pre, y=y)
        else:
            qkvp = run("b_qkv" + tag, lambda comm: proj_qkv(hn, weight("b_w_qkv", j), 0, "b_qkv" + tag, comm=comm))
            wb = jnp.transpose(bias_build(b_rel_bias[j], "bias_build" + tag), (1, 0, 2))
            o = run("attn_fwd" + tag, lambda comm: attn_fwd(qkvp, wb, "attn_fwd" + tag, comm=comm))
            xmid, hn2 = run("b_out" + tag, lambda comm: residual_proj(
                "b_out" + tag, *out_proj(o, weight("b_w_out", j)), xcur, norm_ffn_g[i][None], comm=comm))
            st.update(qkvp=qkvp, wb=wb, o=o)
        g, u, h = run("ffn_up" + tag, lambda comm: ffn_up(
            hn2, weight("ffn_w_gate", i), weight("ffn_w_up", i), 0, "ffn_up" + tag, comm=comm))
        next_g = norm_mix_g[i + 1][None] if i + 1 < DEPTH else None
        down = run("ffn_down" + tag, lambda comm: residual_proj(
            "ffn_down" + tag, *ffn_down(h, weight("ffn_w_down", i)), xmid, next_g, comm=comm))
        xcur, hn = down if next_g is not None else (down, None)
        st.update(x_mid=xmid, hn2=hn2, g=g, u=u, h=h)
        saved.append(st)

    loss_part, dx, dxb, d_final = final_loss(xcur, final_g[None], loss_target.reshape(T, D), "final_loss")

    tk = min(2048, T)
    big_grads = {k: [None] * weights[k].shape[0] for k in BIG}
    small = {"norm_mix_g": [None] * DEPTH, "norm_ffn_g": [None] * DEPTH, "a_v_gain": [None] * 2,
             "a_w_s": [None] * 2, "a_b_s": [None] * 2, "b_rel_bias": [None] * 2}
    tok = lambda width: pl.BlockSpec((tk, width), lambda j_, k_: (k_, 0))
    part = lambda: pl.BlockSpec((None, tk, FS), lambda j_, k_: (j_, k_, 0))
    for i in reversed(range(DEPTH)):
        j = i // 2
        tag = "_l%d" % i
        st = saved[i]
        dg, du = run("ffn_bwd_dh" + tag, lambda comm: ffn_bwd_dh(
            dxb, weight("ffn_w_down", i), st["g"], st["u"], 0, "ffn_bwd_dh" + tag, comm=comm))
        big_grads["ffn_w_down"][i] = wgrad(
            "dw_down" + tag, st["h"], part(), dxb, tok(D), _sds((N_CHIPS, FS, D), F32),
            pl.BlockSpec((None, FS, D), lambda j_, k_: (j_, 0, 0)), N_CHIPS, T, tk)
        dx_mid, dxb_mid, dgn = run("ffn_bwd_dhn" + tag, lambda comm: dgrad_rms(
            "ffn_bwd_dhn" + tag, *ffn_dgrad(dg, du, weight("ffn_w_gate", i), weight("ffn_w_up", i)),
            st["x_mid"], norm_ffn_g[i][None], dx, comm=comm))
        for nm, dz in (("ffn_w_gate", dg), ("ffn_w_up", du)):
            big_grads[nm][i] = run("d" + nm + tag, lambda comm: wgrad(
                "d" + nm + tag, dz, part(), st["hn2"], tok(D), _sds((N_CHIPS, FS, D), F32),
                pl.BlockSpec((None, FS, D), lambda j_, k_: (j_, 0, 0)), N_CHIPS, T, tk, comm=comm))
        dx, dxb = dx_mid, dxb_mid
        small["norm_ffn_g"][i] = dgn
        if i % 2 == 0:
            dy = run("a_out_bwd" + tag, lambda comm: matmul(
                "a_out_bwd" + tag, NT, dxb, pl.BlockSpec((1024, D), lambda i_, j_: (i_, 0)),
                weight("a_w_out", j), pl.BlockSpec((None, 1024, D), lambda i_, j_: (0, j_, 0)),
                _sds((T, GH), BF16), pl.BlockSpec((1024, 1024), lambda i_, j_: (i_, j_)), (T // 1024, 2), comm=comm))
            big_grads["a_w_out"][j] = wgrad(
                "dw_aout" + tag, st["y"], pl.BlockSpec((tk, 1024), lambda j_, k_: (k_, j_)), dxb, tok(D),
                _sds((GH, D), F32), pl.BlockSpec((1024, D), lambda j_, k_: (j_, 0)), 2, T, tk
            ).reshape(N_CHIPS, GH // N_CHIPS, D)
            dpre, d_ws, d_bs, d_gain = run("sgu_bwd" + tag, lambda comm: sgu_bwd(
                st["pre"], dy, a_v_gain[j][None], a_w_s[j], a_b_s[j][:, :, None], "sgu_bwd" + tag,
                tm=2 * SGU_BLOCK, comm=comm))
            small["a_w_s"][j], small["a_b_s"][j], small["a_v_gain"][j] = d_ws, d_bs, d_gain
            dx_in, dxb_in, dgn = run("a_in_bwd" + tag, lambda comm: dgrad_rms(
                "a_in_bwd" + tag, *in_dgrad(dpre, weight("a_w_in", j)),
                st["x_in"], norm_mix_g[i][None], dx, comm=comm))
            big_grads["a_w_in"][j] = run("dw_in" + tag, lambda comm: wgrad(
                "dw_in" + tag, st["hn"], tok(D), dpre, pl.BlockSpec((tk, 1024), lambda j_, k_: (k_, j_)),
                _sds((D, 2 * GH), F32), pl.BlockSpec((D, 1024), lambda j_, k_: (0, j_)), 4, T, tk, comm=comm))
        else:
            do = matmul("b_out_bwd" + tag, NT, dxb, pl.BlockSpec((1024, D), lambda i_, j_: (i_, 0)),
                        weight("b_w_out", j), pl.BlockSpec((None, D, D), lambda i_, j_: (0, 0, 0)),
                        _sds((T, D), BF16), pl.BlockSpec((1024, D), lambda i_, j_: (i_, 0)), (T // 1024, 1))
            big_grads["b_w_out"][j] = wgrad(
                "dw_bout" + tag, st["o"], tok(D), dxb, tok(D),
                _sds((D, D), F32), pl.BlockSpec((D, D), lambda j_, k_: (0, 0)), 1, T, tk
            ).reshape(N_CHIPS, D // N_CHIPS, D)
            dqkvp, dwb = run("attn_bwd" + tag, lambda comm: attn_bwd(
                st["qkvp"], st["o"], do, st["wb"], "attn_bwd" + tag, comm=comm))
            small["b_rel_bias"][j] = bias_grad(
                jnp.pad(jnp.transpose(dwb, (1, 0, 2)), ((0, 0), (0, 0), (0, DIAG - KW))), "bias_grad" + tag)
            dx_in, dxb_in, dgn = dgrad_rms(
                "b_qkv_bwd" + tag, *qkv_dgrad(dqkvp, weight("b_w_qkv", j)),
                st["x_in"], norm_mix_g[i][None], dx)
            big_grads["b_w_qkv"][j] = wgrad(
                "dw_qkv" + tag, st["hn"], tok(D), dqkvp,
                pl.BlockSpec((None, tk, D), lambda j_, k_: (j_, k_ + FRONT // tk, 0)),
                _sds((D, 3 * D), F32), pl.BlockSpec((D, D), lambda j_, k_: (0, j_)), 3, T, tk)
        dx, dxb = dx_in, dxb_in
        small["norm_mix_g"][i] = dgn

    small_grads = {
        "norm_mix_g": jnp.concatenate(small["norm_mix_g"], axis=0),
        "norm_ffn_g": jnp.concatenate(small["norm_ffn_g"], axis=0),
        "final_g": d_final.reshape(D),
        "a_v_gain": jnp.concatenate(small["a_v_gain"], axis=0),
        "a_w_s": jnp.stack(small["a_w_s"]),
        "a_b_s": jnp.stack(small["a_b_s"]).reshape(2, SGU_G, SGU_BLOCK),
        "b_rel_bias": jnp.stack(small["b_rel_bias"]),
    }
    small_names = list(small_grads)
    packed = [_rows128(small_grads[k]) for k in small_names] + [_rows128(loss_part[:, :1])]
    offs = [0]
    for p in packed:
        offs.append(offs[-1] + p.shape[0])
    reduced = allreduce_small(jnp.concatenate(packed, axis=0))
    grads = {}
    for t, k in enumerate(small_names):
        nelem = small_grads[k].size
        grads[k] = reduced[offs[t]:offs[t + 1]].reshape(-1)[:nelem].reshape(weights[k].shape)
    loss = reduced[offs[len(small_names)], 0]

    last = [("a_w_in", 0)]
    for kind, name in (("ex", "exchange_last"), ("sc", "scatter_last")):
        comm, done = make_comm(kind, last)
        done(run_comm(comm, name))
    bufs = {k: jnp.zeros(weights[k].shape, F32) for k in BIG}
    for i in range(DEPTH):
        for kl, m_ in zip(layer_tensors(i), grad_metas(layer_tensors(i))):
            bufs[kl[0]] = final_sum(me, own_parts[kl], recv_parts[kl], bufs[kl[0]], kl[1], m_,
                                    "final_sum_%s_l%d" % kl)
    shared = share_final([bufs[k] for k in BIG])
    for k, gfull in zip(BIG, shared):
        grads[k] = gfull

    delta, new_m, new_v = {}, {}, {}
    for k in order:
        shp = weights[k].shape
        if k in BIG:
            view = shp
        elif k == "a_w_s":
            view = (2, SGU_G * SGU_BLOCK, SGU_BLOCK)
        elif len(shp) == 1:
            view = (1, 1, shp[0])
        elif len(shp) == 2:
            view = (1,) + shp
        else:
            view = shp
        g_, d_, m_, v_ = adamw(weights[k].reshape(view), grads[k].reshape(view), mom_m[k].reshape(view),
                               mom_v[k].reshape(view), "adamw_" + k)
        grads[k], delta[k], new_m[k], new_v[k] = g_.reshape(shp), d_.reshape(shp), m_.reshape(shp), v_.reshape(shp)
    for k in transposed:
        for tree in (grads, delta, new_m, new_v):
            tree[k] = jnp.swapaxes(tree[k], 1, 2)

    return (loss, dx.reshape(1, T, D), *[grads[k] for k in order], *[delta[k] for k in order],
            *[new_m[k] for k in order], *[new_v[k] for k in order])
```

```python
import functools

import jax
import jax.numpy as jnp
from jax import lax
from jax.experimental import pallas as pl
from jax.experimental.pallas import tpu as pltpu

F32 = jnp.float32
BF16 = jnp.bfloat16
MESH = pl.DeviceIdType.MESH

D = 1024
DEPTH = 4
EPS = 1e-6
SGU_BLOCK = 128
GH = 2048
SGU_G = 8
SGU_GD = GH // SGU_G
N_HEADS = 16
HEAD_DIM = 64
CHUNK = 64
PAD = 8 * CHUNK
FRONT = 2048
QB = 128
KW = PAD + QB
N_REL = 192
REL_MIN = -(CHUNK - 1)
REL_MAX = 128
D_FF = 2816
FS = D_FF // 4
NEG = -1e30
SCALE = HEAD_DIM ** -0.5
N_CHIPS = 4

ADAM_LR = 0.001
ADAM_B1 = 0.9
ADAM_B2 = 0.999
ADAM_EPS = 1e-08
ADAM_WD = 0.01
ADAM_STEP = 10

VMEM_BIG = 56 * 1024 * 1024

NN = ((1,), (0,))
NT = ((1,), (1,))
TN = ((0,), (0,))


def _dot(a, b, dims):
    return lax.dot_general(a, b, (dims, ((), ())), preferred_element_type=F32)


class Comm:
    def __init__(self, ins, out_shapes, sems, start, wait, aliases=None):
        self.ins, self.out_shapes, self.sems = list(ins), list(out_shapes), list(sems)
        self.start, self.wait, self.aliases = start, wait, dict(aliases or {})


def _host(body, comm, kw):
    grid = tuple(kw["grid"])
    in_specs = list(kw["in_specs"])
    single = not isinstance(kw["out_specs"], (list, tuple))
    out_specs = [kw["out_specs"]] if single else list(kw["out_specs"])
    out_shape = [kw["out_shape"]] if single else list(kw["out_shape"])
    scratch = list(kw.get("scratch_shapes", ()))
    counts = (len(in_specs), len(comm.ins), len(out_specs), len(comm.out_shapes), len(scratch))

    def hosted(*refs):
        parts, p = [], 0
        for cnt in counts:
            parts.append(refs[p:p + cnt])
            p += cnt
        main_in, c_in, main_out, c_out, main_scr = parts
        sems = refs[p:]
        ids = [pl.program_id(a) for a in range(len(grid))]
        first = functools.reduce(jnp.logical_and, [i == 0 for i in ids])
        last = functools.reduce(jnp.logical_and, [i == n - 1 for i, n in zip(ids, grid)])
        pl.when(first)(lambda: comm.start(c_in, c_out, sems))
        body(*main_in, *main_out, *main_scr)
        pl.when(last)(lambda: comm.wait(c_in, c_out, sems))

    old = kw["compiler_params"]
    kw = dict(kw, in_specs=in_specs + [ANY] * len(comm.ins), out_specs=out_specs + [ANY] * len(comm.out_shapes),
              out_shape=out_shape + comm.out_shapes, scratch_shapes=scratch + comm.sems,
              compiler_params=pltpu.CompilerParams(dimension_semantics=("arbitrary",) * len(grid),
                                                   vmem_limit_bytes=old.vmem_limit_bytes, has_side_effects=True))
    if comm.aliases:
        kw["input_output_aliases"] = {counts[0] + i: counts[2] + o for i, o in comm.aliases.items()}
    return hosted, kw


def _pallas(body, comm=None, **kw):
    if comm is not None:
        body, kw = _host(body, comm, kw)
    return pl.pallas_call(body, **kw)


def _split_outs(outs, comm, n_main):
    outs = list(outs) if isinstance(outs, (list, tuple)) else [outs]
    main = outs[:n_main]
    return (main[0] if n_main == 1 else main), outs[n_main:]


def run_comm(comm, name):
    nci, nco = len(comm.ins), len(comm.out_shapes)

    def body(*refs):
        c_in, c_out, sems = refs[:nci], refs[nci:nci + nco], refs[nci + nco:]
        comm.start(c_in, c_out, sems)
        comm.wait(c_in, c_out, sems)

    kw = {}
    if comm.aliases:
        kw["input_output_aliases"] = dict(comm.aliases)
    return _pallas(body, name=name, in_specs=[ANY] * nci, out_specs=[ANY] * nco, out_shape=comm.out_shapes,
                   scratch_shapes=comm.sems, compiler_params=pltpu.CompilerParams(has_side_effects=True),
                   **kw)(*comm.ins)


def combine(comms):
    if len(comms) == 1:
        return comms[0]
    spans, ni, no, ns = [], 0, 0, 0
    for c in comms:
        spans.append((slice(ni, ni + len(c.ins)), slice(no, no + len(c.out_shapes)), slice(ns, ns + len(c.sems))))
        ni, no, ns = ni + len(c.ins), no + len(c.out_shapes), ns + len(c.sems)

    def start(ins, outs, sems):
        for c, (si, so, ss) in zip(comms, spans):
            c.start(ins[si], outs[so], sems[ss])

    def wait(ins, outs, sems):
        for c, (si, so, ss) in zip(comms, spans):
            c.wait(ins[si], outs[so], sems[ss])

    aliases = {}
    for c, (si, so, _) in zip(comms, spans):
        aliases.update({si.start + i: so.start + o for i, o in c.aliases.items()})
    return Comm([a for c in comms for a in c.ins], [o for c in comms for o in c.out_shapes],
                [s for c in comms for s in c.sems], start, wait, aliases)


def _call(body, comm, n_main, args, **kw):
    if comm is None:
        return _pallas(body, **kw)(*args)
    return _split_outs(_pallas(body, comm=comm, **kw)(*args, *comm.ins), comm, n_main)


def _params(sem=None, vmem=None):
    return pltpu.CompilerParams(dimension_semantics=sem, vmem_limit_bytes=vmem)


def _sds(shape, dtype):
    return jax.ShapeDtypeStruct(tuple(shape), dtype)


_GELU_C = 0.7978845608028654


_GELU_A = _GELU_C * 0.044715


def _gelu(x):
    t = jnp.tanh(x * (_GELU_C + _GELU_A * (x * x)))
    h = 0.5 * x
    return h + h * t


def _gelu_and_grad(x):
    x2 = x * x
    t = jnp.tanh(x * (_GELU_C + _GELU_A * x2))
    h = 0.5 * x
    val = h + h * t
    grad = (0.5 + 0.5 * t) + (h * (1.0 - t * t)) * (_GELU_C + (3.0 * _GELU_A) * x2)
    return val, grad


def _sigmoid(x):
    return 0.5 * (jnp.tanh(0.5 * x) + 1.0)


def cast_bf16(w, name):
    L, R, C = w.shape

    def body(w_ref, o_ref):
        o_ref[...] = w_ref[...].astype(BF16)

    spec = pl.BlockSpec((None, R, C), lambda l: (l, 0, 0))
    return _pallas(body, name=name, grid=(L,), in_specs=[spec], out_specs=spec,
                   out_shape=_sds((L, R, C), BF16), compiler_params=_params(("parallel",)))(w)


def rms_fwd(x, g, name, tm=512, comm=None):
    T = x.shape[0]

    def body(x_ref, g_ref, o_ref):
        o_ref[...] = _rms_rows(x_ref[...], g_ref[...])

    row = pl.BlockSpec((tm, D), lambda i: (i, 0))
    return _call(body, comm, 1, (x, g), name=name, grid=(T // tm,),
                 in_specs=[row, pl.BlockSpec((1, D), lambda i: (0, 0))], out_specs=row,
                 out_shape=_sds((T, D), BF16), compiler_params=_params(("parallel",)))


def dgrad_rms(name, compute, args, specs, x, g, dres, tm=512, comm=None):
    T = x.shape[0]
    n = T // tm
    k = len(args)

    def body(*refs):
        x_ref, g_ref, dres_ref, dx_ref, dxb_ref, dg_ref, acc_ref = refs[k:]
        i = pl.program_id(0)
        xf = x_ref[...]
        r = lax.rsqrt(jnp.mean(xf * xf, axis=-1, keepdims=True) + EPS)
        xhat = xf * r
        dhf = compute(*refs[:k])
        part = (dhf * xhat).reshape(tm // 8, 8, D).sum(axis=0)

        @pl.when(i == 0)
        def _():
            acc_ref[...] = part

        @pl.when(i > 0)
        def _():
            acc_ref[...] += part

        dxhat = dhf * g_ref[...]
        dx = dres_ref[...] + r * (dxhat - xhat * jnp.mean(dxhat * xhat, axis=-1, keepdims=True))
        dx_ref[...] = dx
        dxb_ref[...] = dx.astype(BF16)

        @pl.when(i == n - 1)
        def _():
            dg_ref[...] = jnp.sum(acc_ref[...], axis=0, keepdims=True)

    row = pl.BlockSpec((tm, D), lambda i: (i, 0))
    vec = pl.BlockSpec((1, D), lambda i: (0, 0))
    return _call(body, comm, 3, (*args, x, g, dres), name=name, grid=(n,),
                 in_specs=list(specs) + [row, vec, row], out_specs=[row, row, vec],
                 out_shape=[_sds((T, D), F32), _sds((T, D), BF16), _sds((1, D), F32)],
                 scratch_shapes=[pltpu.VMEM((8, D), F32)],
                 compiler_params=_params(("arbitrary",), VMEM_BIG))


def final_loss(x, g, tgt, name, tm=256):
    T = x.shape[0]
    n = T // tm

    def body(x_ref, g_ref, t_ref, loss_ref, dx_ref, dxb_ref, dg_ref, acc_ref, lacc_ref):
        i = pl.program_id(0)
        xf = x_ref[...]
        r = lax.rsqrt(jnp.mean(xf * xf, axis=-1, keepdims=True) + EPS)
        xhat = xf * r
        gg = g_ref[...]
        e = xhat * gg - t_ref[...]
        dy = e * (1.0 / D)
        part = (dy * xhat).reshape(tm // 8, 8, D).sum(axis=0)
        lpart = (e * e).reshape(tm // 8, 8, D).sum(axis=0)

        @pl.when(i == 0)
        def _():
            acc_ref[...] = part
            lacc_ref[...] = lpart

        @pl.when(i > 0)
        def _():
            acc_ref[...] += part
            lacc_ref[...] += lpart

        dxhat = dy * gg
        dx = r * (dxhat - xhat * jnp.mean(dxhat * xhat, axis=-1, keepdims=True))
        dx_ref[...] = dx
        dxb_ref[...] = dx.astype(BF16)

        @pl.when(i == n - 1)
        def _():
            dg_ref[...] = jnp.sum(acc_ref[...], axis=0, keepdims=True)
            total = jnp.sum(jnp.sum(lacc_ref[...], axis=0, keepdims=True), axis=1, keepdims=True)
            loss_ref[...] = jnp.broadcast_to(total * (0.5 / D), (1, 128))

    row = pl.BlockSpec((tm, D), lambda i: (i, 0))
    vec = pl.BlockSpec((1, D), lambda i: (0, 0))
    return _pallas(body, name=name, grid=(n,), in_specs=[row, vec, row],
                   out_specs=[pl.BlockSpec((1, 128), lambda i: (0, 0)), row, row, vec],
                   out_shape=[_sds((1, 128), F32), _sds((T, D), F32), _sds((T, D), BF16), _sds((1, D), F32)],
                   scratch_shapes=[pltpu.VMEM((8, D), F32), pltpu.VMEM((8, D), F32)],
                   compiler_params=_params(("arbitrary",)))(x, g, tgt)


def matmul(name, dims, a, a_spec, b, b_spec, out_shape, out_spec, grid, *, acc=False, res=None, res_spec=None,
           comm=None):
    has_res = res is not None

    def body(*refs):
        a_ref, b_ref = refs[0], refs[1]
        r_ref = refs[2] if has_res else None
        o_ref = refs[-1]
        d = _dot(a_ref[...], b_ref[...], dims)
        if not acc:
            if has_res:
                d = d + r_ref[...]
            o_ref[...] = d.astype(o_ref.dtype)
        else:
            k = pl.program_id(len(grid) - 1)

            @pl.when(k == 0)
            def _():
                o_ref[...] = (d + r_ref[...]) if has_res else d

            @pl.when(k > 0)
            def _():
                o_ref[...] += d

    sem = ("parallel",) * (len(grid) - 1) + (("arbitrary",) if acc else ("parallel",))
    ins = [a, b] + ([res] if has_res else [])
    specs = [a_spec, b_spec] + ([res_spec] if has_res else [])
    return _call(body, comm, 1, ins, name=name, grid=grid, in_specs=specs, out_specs=out_spec, out_shape=out_shape,
                 compiler_params=_params(sem, VMEM_BIG))


def wgrad(name, a, a_spec, b, b_spec, out_shape, out_spec, J, T, tk, comm=None):
    return matmul(name, TN, a, a_spec, b, b_spec, out_shape, out_spec, (J, T // tk), acc=True, comm=comm)


def _sgu_mask():
    p = lax.broadcasted_iota(jnp.int32, (SGU_BLOCK, SGU_BLOCK), 0)
    q = lax.broadcasted_iota(jnp.int32, (SGU_BLOCK, SGU_BLOCK), 1)
    return lax.shift_right_logical(q, 6) <= lax.shift_right_logical(p, 6)


def sgu_fwd(pre, gain, w_s, b_s, name, comm=None):
    T = pre.shape[0]

    def body(pre_ref, gain_ref, ws_ref, bs_ref, y_ref):
        mask = _sgu_mask()
        u = _gelu(pre_ref[:, :GH].astype(F32))
        va = _gelu(pre_ref[:, GH:].astype(F32))
        r = lax.rsqrt(jnp.mean(va * va, axis=-1, keepdims=True) + EPS)
        vn = ((va * r) * gain_ref[...]).astype(BF16)
        for g in range(SGU_G):
            sl = slice(g * SGU_GD, (g + 1) * SGU_GD)
            wm = jnp.where(mask, ws_ref[g], 0.0).astype(BF16)
            vm = _dot(wm, vn[:, sl], NN) + bs_ref[g]
            y_ref[:, sl] = (u[:, sl] * vm).astype(BF16)

    return _call(
        body, comm, 1, (pre, gain, w_s, b_s), name=name, grid=(T // SGU_BLOCK,),
        in_specs=[pl.BlockSpec((SGU_BLOCK, 2 * GH), lambda i: (i, 0)),
                  pl.BlockSpec((1, GH), lambda i: (0, 0)),
                  pl.BlockSpec((SGU_G, SGU_BLOCK, SGU_BLOCK), lambda i: (0, 0, 0)),
                  pl.BlockSpec((SGU_G, SGU_BLOCK, 1), lambda i: (0, 0, 0))],
        out_specs=pl.BlockSpec((SGU_BLOCK, GH), lambda i: (i, 0)),
        out_shape=_sds((T, GH), BF16), compiler_params=_params(("parallel",)))


def sgu_bwd(pre, dy, gain, w_s, b_s, name, tm=SGU_BLOCK, comm=None):
    T = pre.shape[0]
    n = T // tm

    def body(pre_ref, dy_ref, gain_ref, ws_ref, bs_ref, dpre_ref, dws_ref, dbs_ref, dgain_ref, gacc_ref):
        i = pl.program_id(0)

        @pl.when(i == 0)
        def _():
            dws_ref[...] = jnp.zeros_like(dws_ref)
            dbs_ref[...] = jnp.zeros_like(dbs_ref)
            gacc_ref[...] = jnp.zeros_like(gacc_ref)

        mask = _sgu_mask()
        gain_v = gain_ref[...]
        for sb in range(tm // SGU_BLOCK):
            rows = slice(sb * SGU_BLOCK, (sb + 1) * SGU_BLOCK)
            u, du_dpre = _gelu_and_grad(pre_ref[rows, :GH].astype(F32))
            va, dva_dpre = _gelu_and_grad(pre_ref[rows, GH:].astype(F32))
            r = lax.rsqrt(jnp.mean(va * va, axis=-1, keepdims=True) + EPS)
            vhat = va * r
            vn = (vhat * gain_v).astype(BF16)
            dyf = dy_ref[rows, :].astype(F32)
            dvn_parts = []
            for grp in range(SGU_G):
                sl = slice(grp * SGU_GD, (grp + 1) * SGU_GD)
                wm = jnp.where(mask, ws_ref[grp], 0.0).astype(BF16)
                vm = _dot(wm, vn[:, sl], NN) + bs_ref[grp]
                dpre_ref[rows, sl] = ((dyf[:, sl] * vm) * du_dpre[:, sl]).astype(BF16)
                dvm = dyf[:, sl] * u[:, sl]
                dbs_ref[grp] += jnp.sum(dvm, axis=-1, keepdims=True)
                dvm16 = dvm.astype(BF16)
                dws_ref[grp] += jnp.where(mask, _dot(dvm16, vn[:, sl], NT), 0.0)
                dvn_parts.append(_dot(wm, dvm16, TN))
            dvn = jnp.concatenate(dvn_parts, axis=-1)
            gacc_ref[...] += (dvn * vhat).reshape(SGU_BLOCK // 8, 8, GH).sum(axis=0)
            dvhat = dvn * gain_v
            dva = r * (dvhat - vhat * jnp.mean(dvhat * vhat, axis=-1, keepdims=True))
            dpre_ref[rows, GH:] = (dva * dva_dpre).astype(BF16)

        @pl.when(i == n - 1)
        def _():
            dgain_ref[...] = jnp.sum(gacc_ref[...], axis=0, keepdims=True)

    const3 = lambda i: (0, 0, 0)
    return _call(
        body, comm, 4, (pre, dy, gain, w_s, b_s), name=name, grid=(n,),
        in_specs=[pl.BlockSpec((tm, 2 * GH), lambda i: (i, 0)),
                  pl.BlockSpec((tm, GH), lambda i: (i, 0)),
                  pl.BlockSpec((1, GH), lambda i: (0, 0)),
                  pl.BlockSpec((SGU_G, SGU_BLOCK, SGU_BLOCK), const3),
                  pl.BlockSpec((SGU_G, SGU_BLOCK, 1), const3)],
        out_specs=[pl.BlockSpec((tm, 2 * GH), lambda i: (i, 0)),
                   pl.BlockSpec((SGU_G, SGU_BLOCK, SGU_BLOCK), const3),
                   pl.BlockSpec((SGU_G, SGU_BLOCK, 1), const3),
                   pl.BlockSpec((1, GH), lambda i: (0, 0))],
        out_shape=[_sds((T, 2 * GH), BF16), _sds((SGU_G, SGU_BLOCK, SGU_BLOCK), F32),
                   _sds((SGU_G, SGU_BLOCK, 1), F32), _sds((1, GH), F32)],
        scratch_shapes=[pltpu.VMEM((8, GH), F32)],
        compiler_params=_params(("arbitrary",)))


DIAG = 768


def _diag_onehot():
    n = lax.broadcasted_iota(jnp.int32, (N_REL, DIAG), 1)
    r = lax.broadcasted_iota(jnp.int32, (N_REL, DIAG), 0)
    idx = jnp.clip(KW - 1 - n, REL_MIN, REL_MAX) - REL_MIN
    return (idx == r).astype(BF16)


def _split3(v):
    hi = v.astype(BF16)
    r1 = v - hi.astype(F32)
    mid = r1.astype(BF16)
    lo = (r1 - mid.astype(F32)).astype(BF16)
    return hi, mid, lo


def bias_build(rel_bias, name):
    def body(rb_ref, o_ref):
        oh = _diag_onehot()
        hi, mid, lo = _split3(rb_ref[...])
        u = (_dot(hi, oh, NN) + _dot(mid, oh, NN) + _dot(lo, oh, NN)) * LOG2E
        j = lax.broadcasted_iota(jnp.int32, (1, KW), 1)

        def row(i, carry):
            val = pltpu.roll(u, (i + (DIAG - QB + 1)) % DIAG, 1)[:, :KW]
            rel = lax.shift_right_logical(i, 6) - lax.shift_right_logical(j, 6) + 8
            ok = (rel >= 0) & (rel <= 8)
            o_ref[i] = jnp.where(ok, val, NEG)
            return carry

        lax.fori_loop(0, QB, row, 0)

    return _pallas(body, name=name, out_shape=_sds((QB, N_HEADS, KW), F32),
                   in_specs=[pl.BlockSpec(memory_space=pltpu.VMEM)],
                   out_specs=pl.BlockSpec(memory_space=pltpu.VMEM))(rel_bias)


def bias_grad(dwb, name):
    def body(d_ref, o_ref):
        def row(i, acc):
            return acc + pltpu.roll(d_ref[i], QB - 1 - i, 1)

        du = lax.fori_loop(0, QB, row, jnp.zeros((N_HEADS, DIAG), F32))
        oh = _diag_onehot()
        hi, mid, lo = _split3(du)
        o_ref[...] = _dot(hi, oh, NT) + _dot(mid, oh, NT) + _dot(lo, oh, NT)

    return _pallas(body, name=name, out_shape=_sds((N_HEADS, N_REL), F32),
                   in_specs=[pl.BlockSpec(memory_space=pltpu.VMEM)],
                   out_specs=pl.BlockSpec(memory_space=pltpu.VMEM))(dwb)


LOG2E = 1.4426950408889634
Q_SCALE = SCALE * LOG2E


def _attn_block(qkv_ref, blk, masked):
    r0 = pl.multiple_of(blk * QB, QB)
    qs = qkv_ref[0, pl.ds(r0 + FRONT, QB), :]
    kvalid = (lax.broadcasted_iota(jnp.int32, (1, KW), 1) >= PAD - blk * QB) if masked else None
    return r0, qs, kvalid


def _step_windows(qkv_ref, b, step):
    r0 = pl.multiple_of(b * step, QB) + (FRONT - PAD)
    out = []
    for part in (1, 2):
        a = qkv_ref[part, pl.ds(r0, PAD + step), :]
        zero = jnp.zeros_like(a)
        out.append([jnp.where(_head_mask(h), a, zero) for h in range(2)])
    return out


def _window(stacks, t):
    return jnp.concatenate([s[t * QB:t * QB + KW] for s in stacks], axis=0)


def _head_mask(h):
    lane = lax.broadcasted_iota(jnp.int32, (1, 2 * HEAD_DIM), 1)
    return (lane < HEAD_DIM) if h == 0 else (lane >= HEAD_DIM)


def _stack_heads(a):
    zero = jnp.zeros_like(a)
    return jnp.concatenate([jnp.where(_head_mask(0), a, zero), jnp.where(_head_mask(1), a, zero)], axis=0)


def _rows_by_head(a):
    return jnp.concatenate([a[:, :KW], a[:, KW:]], axis=0)


def _per_head(lo, hi):
    return jnp.where(_head_mask(0), lo, hi)


def _attn_exp(qs, kst, w_ref, kvalid):
    s = _dot(qs, kst, NT) + jnp.concatenate([w_ref[0], w_ref[1]], axis=1)
    if kvalid is not None:
        s = jnp.where(jnp.concatenate([kvalid, kvalid], axis=1), s, NEG)
    es, invs = [], []
    for h in range(2):
        sh = s[:, h * KW:(h + 1) * KW]
        eh = jnp.exp2(sh - jnp.max(sh, axis=-1, keepdims=True))
        es.append(eh)
        invs.append(1.0 / jnp.sum(eh, axis=-1, keepdims=True))
    return jnp.concatenate(es, axis=1), invs


ATTN_G = 8


def _blocks_per_step(T):
    return min(ATTN_G, T // QB)


def _masked_and_not(b, fn, step):
    n_masked = -(-PAD // step)
    pl.when(b < n_masked)(functools.partial(fn, True))
    pl.when(b >= n_masked)(functools.partial(fn, False))


def attn_fwd(qkvp, wb, name, comm=None):
    T = qkvp.shape[1] - FRONT
    G = _blocks_per_step(T)

    def body(qkv_ref, w_ref, o_ref):
        b = pl.program_id(1)

        def blocks(masked):
            keys, values = _step_windows(qkv_ref, b, G * QB)
            for t in range(G):
                _, qs, kvalid = _attn_block(qkv_ref, b * G + t, masked)
                e, inv = _attn_exp(qs, _window(keys, t), w_ref, kvalid)
                o = _dot(e.astype(BF16), _window(values, t), NN) * _per_head(*inv)
                o_ref[t * QB:(t + 1) * QB, :] = o.astype(BF16)

        _masked_and_not(b, blocks, G * QB)

    return _call(
        body, comm, 1, (qkvp, wb), name=name, grid=(N_HEADS // 2, T // (QB * G)),
        in_specs=[pl.BlockSpec((3, FRONT + T, 2 * HEAD_DIM), lambda hp, b: (0, 0, hp)),
                  pl.BlockSpec((2, QB, KW), lambda hp, b: (hp, 0, 0))],
        out_specs=pl.BlockSpec((QB * G, 2 * HEAD_DIM), lambda hp, b: (b, hp)),
        out_shape=_sds((T, D), BF16),
        compiler_params=_params(("parallel", "arbitrary"), VMEM_BIG))


def attn_bwd(qkvp, o, do, wb, name, comm=None):
    T = qkvp.shape[1] - FRONT
    G = _blocks_per_step(T)
    nb = T // (QB * G)

    def body(qkv_ref, o_ref, do_ref, w_ref, dqkv_ref, dw_ref, dk_acc, dv_acc):
        b = pl.program_id(1)

        @pl.when(b == 0)
        def _():
            dk_acc[...] = jnp.zeros_like(dk_acc)
            dv_acc[...] = jnp.zeros_like(dv_acc)
            dw_ref[...] = jnp.zeros_like(dw_ref)
            dqkv_ref[0, 0:FRONT, :] = jnp.zeros((FRONT, 2 * HEAD_DIM), BF16)

        def blocks(masked):
            dws = None
            keys, values = _step_windows(qkv_ref, b, G * QB)
            for t in range(G):
                r0, qs, kvalid = _attn_block(qkv_ref, b * G + t, masked)
                kst = _window(keys, t)
                e, inv = _attn_exp(qs, kst, w_ref, kvalid)
                do2 = do_ref[t * QB:(t + 1) * QB, :]
                dof = do2.astype(F32)
                prod = dof * o_ref[t * QB:(t + 1) * QB, :].astype(F32)
                dp = _dot(do2, _window(values, t), NT)
                parts = []
                for h in range(2):
                    delta = jnp.sum(jnp.where(_head_mask(h), prod, 0.0), axis=-1, keepdims=True)
                    half = slice(h * KW, (h + 1) * KW)
                    parts.append(e[:, half] * ((dp[:, half] - delta) * inv[h]))
                ds = jnp.concatenate(parts, axis=1)
                dws = ds if dws is None else dws + ds
                ds16 = ds.astype(BF16)
                dqkv_ref[0, pl.ds(r0 + FRONT, QB), :] = (_dot(ds16, kst, NN) * SCALE).astype(BF16)
                dk_acc[pl.ds(r0 + (FRONT - PAD), KW), :] += _dot(_rows_by_head(ds16), _stack_heads(qs), TN)
                dv_acc[pl.ds(r0 + (FRONT - PAD), KW), :] += _dot(
                    _rows_by_head(e.astype(BF16)), _stack_heads((dof * _per_head(*inv)).astype(BF16)), TN)
            dw_ref[0] += dws[:, :KW]
            dw_ref[1] += dws[:, KW:]

        _masked_and_not(b, blocks, G * QB)

        @pl.when(b == nb - 1)
        def _():
            dqkv_ref[1] = (dk_acc[...] * (1.0 / LOG2E)).astype(BF16)
            dqkv_ref[2] = dv_acc[...].astype(BF16)

    slab = pl.BlockSpec((3, FRONT + T, 2 * HEAD_DIM), lambda hp, b: (0, 0, hp))
    wspec = pl.BlockSpec((2, QB, KW), lambda hp, b: (hp, 0, 0))
    rows = pl.BlockSpec((QB * G, 2 * HEAD_DIM), lambda hp, b: (b, hp))
    return _call(
        body, comm, 2, (qkvp, o, do, wb), name=name, grid=(N_HEADS // 2, nb),
        in_specs=[slab, rows, rows, wspec],
        out_specs=[slab, wspec],
        out_shape=[_sds((3, FRONT + T, D), BF16), _sds((N_HEADS, QB, KW), F32)],
        scratch_shapes=[pltpu.VMEM((FRONT + T, 2 * HEAD_DIM), F32), pltpu.VMEM((FRONT + T, 2 * HEAD_DIM), F32)],
        compiler_params=_params(("parallel", "arbitrary"), VMEM_BIG))


def proj_qkv(hn, w, l, name, tm=512, comm=None):
    T = hn.shape[0]
    pb = FRONT // tm

    def body(a_ref, b_ref, o_ref):
        i = pl.program_id(1)

        @pl.when(i < pb)
        def _():
            o_ref[...] = jnp.zeros_like(o_ref)

        @pl.when(i >= pb)
        def _():
            scale = jnp.where(pl.program_id(0) == 0, Q_SCALE, 1.0).astype(F32)
            o_ref[...] = (_dot(a_ref[...], b_ref[...], NN) * scale).astype(BF16)

    return _call(
        body, comm, 1, (hn, w), name=name, grid=(3, pb + T // tm),
        in_specs=[pl.BlockSpec((tm, D), lambda p, i: (jnp.maximum(i - pb, 0), 0)),
                  pl.BlockSpec((None, D, D), lambda p, i: (l, 0, p))],
        out_specs=pl.BlockSpec((None, tm, D), lambda p, i: (p, i, 0)),
        out_shape=_sds((3, FRONT + T, D), BF16),
        compiler_params=_params(("parallel", "parallel"), VMEM_BIG))


def ffn_up(hn, wg, wu, l, name, tm=1024, comm=None):
    T = hn.shape[0]

    def body(a_ref, wg_ref, wu_ref, g_ref, u_ref, h_ref):
        a = a_ref[...]
        g = _dot(a, wg_ref[...], NT)
        u = _dot(a, wu_ref[...], NT)
        s = _sigmoid(g)
        silu = g * s
        g_ref[...] = (u * (s * (1.0 + g * (1.0 - s)))).astype(BF16)
        u_ref[...] = silu.astype(BF16)
        h_ref[...] = (silu * u).astype(BF16)

    wspec = pl.BlockSpec((None, None, FS, D), lambda s, i: (l, s, 0, 0))
    ospec = pl.BlockSpec((None, tm, FS), lambda s, i: (s, i, 0))
    return _call(
        body, comm, 3, (hn, wg, wu), name=name, grid=(N_CHIPS, T // tm),
        in_specs=[pl.BlockSpec((tm, D), lambda s, i: (i, 0)), wspec, wspec],
        out_specs=[ospec, ospec, ospec],
        out_shape=[_sds((N_CHIPS, T, FS), BF16)] * 3,
        compiler_params=_params(("parallel", "parallel"), VMEM_BIG))


def ffn_bwd_dh(dxb, wd, g, u, l, name, tm=2048, comm=None):
    T = dxb.shape[0]
    tm = min(tm, T)

    def body(a_ref, wd_ref, g_ref, u_ref, dg_ref, du_ref):
        dh = _dot(a_ref[...], wd_ref[...], NT)
        dg_ref[...] = (dh * g_ref[...].astype(F32)).astype(BF16)
        du_ref[...] = (dh * u_ref[...].astype(F32)).astype(BF16)

    aspec = pl.BlockSpec((None, tm, FS), lambda i, s: (s, i, 0))
    return _call(
        body, comm, 2, (dxb, wd, g, u), name=name, grid=(T // tm, N_CHIPS),
        in_specs=[pl.BlockSpec((tm, D), lambda i, s: (i, 0)),
                  pl.BlockSpec((None, None, FS, D), lambda i, s: (l, s, 0, 0)), aspec, aspec],
        out_specs=[aspec, aspec],
        out_shape=[_sds((N_CHIPS, T, FS), BF16)] * 2,
        compiler_params=_params(("parallel", "parallel"), VMEM_BIG))


def ffn_dgrad(dg, du, wg, wu, tm=512):
    def compute(dg_ref, du_ref, wg_ref, wu_ref):
        d = None
        for s in range(N_CHIPS):
            t = _dot(dg_ref[s], wg_ref[s], NN) + _dot(du_ref[s], wu_ref[s], NN)
            d = t if d is None else d + t
        return d

    aspec = pl.BlockSpec((N_CHIPS, tm, FS), lambda i: (0, i, 0))
    wspec = pl.BlockSpec((None, N_CHIPS, FS, D), lambda i: (0, 0, 0, 0), pipeline_mode=pl.Buffered(1))
    return compute, (dg, du, wg, wu), [aspec, aspec, wspec, wspec]


def qkv_dgrad(dqkvp, w, tm=512):
    def compute(a_ref, w_ref):
        d = None
        for p in range(3):
            t = _dot(a_ref[p], w_ref[:, p * D:(p + 1) * D], NT)
            d = t if d is None else d + t
        return d

    return compute, (dqkvp, w), [pl.BlockSpec((3, tm, D), lambda i: (0, i + FRONT // tm, 0)),
                                 pl.BlockSpec((None, D, 3 * D), lambda i: (0, 0, 0))]


def in_dgrad(dpre, w, tm=512):
    def compute(a_ref, w_ref):
        return _dot(a_ref[...], w_ref[...], NT)

    return compute, (dpre, w), [pl.BlockSpec((tm, 2 * GH), lambda i: (i, 0)),
                                pl.BlockSpec((None, D, 2 * GH), lambda i: (0, 0, 0))]


def _rms_rows(x, g):
    r = lax.rsqrt(jnp.mean(x * x, axis=-1, keepdims=True) + EPS)
    return ((x * r) * g).astype(BF16)


def residual_proj(name, compute, args, specs, res, norm_g, tm=512, comm=None):
    T = res.shape[0]
    k = len(args)
    with_norm = norm_g is not None

    def body(*refs):
        d = refs[k][...] + compute(*refs[:k])
        if with_norm:
            refs[k + 2][...] = d
            refs[k + 3][...] = _rms_rows(d, refs[k + 1][...])
        else:
            refs[k + 1][...] = d

    row = pl.BlockSpec((tm, D), lambda i: (i, 0))
    vec = pl.BlockSpec((1, D), lambda i: (0, 0))
    if with_norm:
        return _call(body, comm, 2, (*args, res, norm_g), name=name, grid=(T // tm,),
                     in_specs=list(specs) + [row, vec], out_specs=[row, row],
                     out_shape=[_sds((T, D), F32), _sds((T, D), BF16)],
                     compiler_params=_params(("parallel",), VMEM_BIG))
    return _call(body, comm, 1, (*args, res), name=name, grid=(T // tm,), in_specs=list(specs) + [row],
                 out_specs=row, out_shape=_sds((T, D), F32), compiler_params=_params(("parallel",), VMEM_BIG))


def ffn_down(h, wd, tm=512):
    def compute(h_ref, wd_ref):
        d = None
        for s in range(N_CHIPS):
            t = _dot(h_ref[s], wd_ref[s], NN)
            d = t if d is None else d + t
        return d

    return compute, (h, wd), [pl.BlockSpec((N_CHIPS, tm, FS), lambda i: (0, i, 0)),
                              pl.BlockSpec((None, N_CHIPS, FS, D), lambda i: (0, 0, 0, 0))]


def out_proj(a, w, tm=512):
    K = a.shape[1]

    def compute(a_ref, w_ref):
        return _dot(a_ref[...], w_ref[...], NN)

    return compute, (a, w), [pl.BlockSpec((tm, K), lambda i: (i, 0)), pl.BlockSpec((None, K, D), lambda i: (0, 0, 0))]


def adamw(w, g, m, v, name):
    L, R, C = w.shape

    def body(w_ref, g_ref, m_ref, v_ref, go_ref, d_ref, nm_ref, nv_ref):
        gf = g_ref[...]
        go_ref[...] = gf
        nm = ADAM_B1 * m_ref[...] + (1.0 - ADAM_B1) * gf
        nv = ADAM_B2 * v_ref[...] + (1.0 - ADAM_B2) * (gf * gf)
        m_hat = nm / (1.0 - ADAM_B1 ** ADAM_STEP)
        v_hat = nv / (1.0 - ADAM_B2 ** ADAM_STEP)
        d_ref[...] = -ADAM_LR * (m_hat / (jnp.sqrt(v_hat) + ADAM_EPS) + ADAM_WD * w_ref[...])
        nm_ref[...] = nm
        nv_ref[...] = nv

    tr = R // 4 if R % 32 == 0 else R
    spec = pl.BlockSpec((None, tr, C), lambda l, r: (l, r, 0))
    return _pallas(body, name=name, grid=(L, R // tr), in_specs=[spec] * 4, out_specs=[spec] * 4,
                   out_shape=[_sds((L, R, C), F32)] * 4,
                   compiler_params=_params(("parallel", "parallel")))(w, g, m, v)


def _coords():
    return lax.axis_index("x"), lax.axis_index("y"), lax.axis_index("c")


def _other_chips(x, y):
    out = []
    for fx, fy in ((1, 0), (0, 1), (1, 1)):
        px = (1 - x) if fx else x
        py = (1 - y) if fy else y
        out.append((px, py))
    return out


def _flip_index(s, j):
    sx, sy = s // 2, s % 2
    fx, fy = ((1, 0), (0, 1), (1, 1))[j]
    return 2 * (sx ^ fx) + (sy ^ fy)


def _for_my_chip(sme, fn):
    for s in range(N_CHIPS):
        pl.when(sme == s)(functools.partial(fn, s))


ANY = pl.BlockSpec(memory_space=pl.ANY)

GATHER_KIND = {"a_w_in": "col", "b_w_qkv": "col", "a_w_out": "row", "b_w_out": "row",
               "ffn_w_gate": "row", "ffn_w_up": "row", "ffn_w_down": "row"}
BIG = tuple(GATHER_KIND)


def _gathered_shape(kind, shape):
    L, R, C = shape
    return (L, R, N_CHIPS * C) if kind == "col" else (L, N_CHIPS, R, C)


def _shard_rows(ref, kind, s, r0, rn, C):
    if kind == "col":
        return ref.at[:, pl.ds(r0, rn), s * C:(s + 1) * C]
    return ref.at[:, s, pl.ds(r0, rn), :]


def gather_stage1(items):
    n = len(items)
    dims = [it[0].shape[1:] for it in items]

    def copies(ins, outs, sems, s, with_landed=True):
        lsem, ssem, rsem = sems
        x, y, c = _coords()
        chips = _other_chips(x, y)
        local, send, landed = [], [], []
        for t, (_, li, kind) in enumerate(items):
            R, C = dims[t]
            r0 = pl.multiple_of(c * (R // 2), 8)
            local.append(pltpu.make_async_copy(ins[t].at[pl.ds(li, 1)], _shard_rows(outs[t], kind, s, 0, R, C),
                                               lsem.at[t]))
            for j in range(3):
                pair = dict(send_sem=ssem.at[3 * t + j], recv_sem=rsem.at[3 * t + j],
                            device_id=(chips[j][0], chips[j][1], c), device_id_type=MESH)
                send.append(pltpu.make_async_remote_copy(
                    src_ref=ins[t].at[pl.ds(li, 1), pl.ds(r0, R // 2), :],
                    dst_ref=_shard_rows(outs[t], kind, s, r0, R // 2, C), **pair))
                if with_landed:
                    got = _shard_rows(outs[t], kind, _flip_index(s, j), r0, R // 2, C)
                    landed.append(pltpu.make_async_remote_copy(src_ref=got, dst_ref=got, **pair))
        return local, send, landed

    def start(ins, outs, sems):
        def run(s):
            local, send, _ = copies(ins, outs, sems, s, with_landed=False)
            for cp in local + send:
                cp.start()
        x, y, _ = _coords()
        _for_my_chip(2 * x + y, run)

    def wait(ins, outs, sems):
        def run(s):
            local, send, landed = copies(ins, outs, sems, s)
            for cp in landed:
                cp.wait_recv()
            for cp in send:
                cp.wait_send()
            for cp in local:
                cp.wait()
        x, y, _ = _coords()
        _for_my_chip(2 * x + y, run)

    out_shapes = [_sds(_gathered_shape(kind, (1,) + tuple(dims[t])), BF16) for t, (_, _, kind) in enumerate(items)]
    sems = [pltpu.SemaphoreType.DMA((n,)), pltpu.SemaphoreType.DMA((3 * n,)), pltpu.SemaphoreType.DMA((3 * n,))]
    return Comm([it[0] for it in items], out_shapes, sems, start, wait)


def gather_stage2(items, gathered):
    n = len(items)
    dims = [it[0].shape[1:] for it in items]

    def copies(outs, sems, s, with_landed=True):
        ssem, rsem = sems
        x, y, c = _coords()
        send, landed = [], []
        for t, (_, _, kind) in enumerate(items):
            R, C = dims[t]
            for j in range(3):
                pair = dict(send_sem=ssem.at[3 * t + j], recv_sem=rsem.at[3 * t + j],
                            device_id=(x, y, 1 - c), device_id_type=MESH)
                mine = _shard_rows(outs[t], kind, _flip_index(s, j), pl.multiple_of(c * (R // 2), 8), R // 2, C)
                send.append(pltpu.make_async_remote_copy(src_ref=mine, dst_ref=mine, **pair))
                if with_landed:
                    other = _shard_rows(outs[t], kind, _flip_index(s, j), pl.multiple_of((1 - c) * (R // 2), 8),
                                        R // 2, C)
                    landed.append(pltpu.make_async_remote_copy(src_ref=other, dst_ref=other, **pair))
        return send, landed

    def start(ins, outs, sems):
        def run(s):
            for cp in copies(outs, sems, s, with_landed=False)[0]:
                cp.start()
        x, y, _ = _coords()
        _for_my_chip(2 * x + y, run)

    def wait(ins, outs, sems):
        def run(s):
            send, landed = copies(outs, sems, s)
            for cp in landed:
                cp.wait_recv()
            for cp in send:
                cp.wait_send()
        x, y, _ = _coords()
        _for_my_chip(2 * x + y, run)

    out_shapes = [_sds(g.shape, BF16) for g in gathered]
    sems = [pltpu.SemaphoreType.DMA((3 * n,)), pltpu.SemaphoreType.DMA((3 * n,))]
    return Comm(gathered, out_shapes, sems, start, wait, aliases={t: t for t in range(n)})


def _half_shape(kind, R, C):
    return (R // 2, N_CHIPS * C) if kind == "col" else (N_CHIPS, R // 2, C)


def exchange_halves(grads, metas):
    n = len(grads)

    def copies(ins, outs, sems):
        ssem, rsem = sems
        x, y, c = _coords()
        out = []
        for t, (kind, R, C) in enumerate(metas):
            r0 = pl.multiple_of((1 - c) * (R // 2), 8)
            src = ins[t].at[pl.ds(r0, R // 2), :] if kind == "col" else ins[t].at[:, pl.ds(r0, R // 2), :]
            out.append(pltpu.make_async_remote_copy(
                src_ref=src, dst_ref=outs[t], send_sem=ssem.at[t], recv_sem=rsem.at[t],
                device_id=(x, y, 1 - c), device_id_type=MESH))
        return out

    def start(ins, outs, sems):
        for cp in copies(ins, outs, sems):
            cp.start()

    def wait(ins, outs, sems):
        for cp in copies(ins, outs, sems):
            cp.wait()

    return Comm(grads, [_sds(_half_shape(*m), F32) for m in metas], [pltpu.SemaphoreType.DMA((n,))] * 2, start, wait)


def pair_sum(me, g, sib, meta, name):
    kind, R, C = meta
    h = R // 2

    def body(me_ref, g_ref, sib_ref, p16_ref, own_ref):
        s = pl.program_id(0)
        v = g_ref[...] + sib_ref[...]
        p16_ref[...] = v.astype(BF16)

        @pl.when(s == me_ref[1])
        def _():
            own_ref[...] = v

    if kind == "col":
        gspec = pl.BlockSpec((h, C), lambda s, me_ref: (me_ref[0], s))
        sspec = pl.BlockSpec((h, C), lambda s, me_ref: (0, s))
    else:
        gspec = pl.BlockSpec((None, h, C), lambda s, me_ref: (s, me_ref[0], 0))
        sspec = pl.BlockSpec((None, h, C), lambda s, me_ref: (s, 0, 0))
    grid_spec = pltpu.PrefetchScalarGridSpec(
        num_scalar_prefetch=1, grid=(N_CHIPS,), in_specs=[gspec, sspec],
        out_specs=[sspec, pl.BlockSpec((h, C), lambda s, me_ref: (0, 0))])
    return _pallas(body, name=name, grid_spec=grid_spec,
                   out_shape=[_sds(_half_shape(*meta), BF16), _sds((h, C), F32)],
                   compiler_params=_params(("arbitrary",), VMEM_BIG))(me, g, sib)


def scatter_partials(p16s, metas):
    n = len(p16s)

    def copies(ins, outs, sems, s):
        ssem, rsem = sems
        x, y, c = _coords()
        chips = _other_chips(x, y)
        out = []
        for t, (kind, R, C) in enumerate(metas):
            for j in range(3):
                sj = _flip_index(s, j)
                src = ins[t].at[:, sj * C:(sj + 1) * C] if kind == "col" else ins[t].at[sj]
                out.append(pltpu.make_async_remote_copy(
                    src_ref=src, dst_ref=outs[t].at[j], send_sem=ssem.at[3 * t + j], recv_sem=rsem.at[3 * t + j],
                    device_id=(chips[j][0], chips[j][1], c), device_id_type=MESH))
        return out

    def start(ins, outs, sems):
        def run(s):
            for cp in copies(ins, outs, sems, s):
                cp.start()
        x, y, _ = _coords()
        _for_my_chip(2 * x + y, run)

    def wait(ins, outs, sems):
        def run(s):
            for cp in copies(ins, outs, sems, s):
                cp.wait()
        x, y, _ = _coords()
        _for_my_chip(2 * x + y, run)

    return Comm(p16s, [_sds((3, R // 2, C), BF16) for (_, R, C) in metas],
                [pltpu.SemaphoreType.DMA((3 * n,))] * 2, start, wait)


def final_sum(me, own, q, buf, shape, l, meta, name):
    _, R, C = meta
    h = R // 2

    def body(me_ref, own_ref, q_ref, *rest):
        rest[-1][...] = ((own_ref[...] + q_ref[0].astype(F32)) + q_ref[1].astype(F32)) + q_ref[2].astype(F32)

    grid_spec = pltpu.PrefetchScalarGridSpec(
        num_scalar_prefetch=1, grid=(1,),
        in_specs=[pl.BlockSpec((h, C), lambda i, me_ref: (0, 0)),
                  pl.BlockSpec((3, h, C), lambda i, me_ref: (0, 0, 0))] + ([] if buf is None else [ANY]),
        out_specs=pl.BlockSpec((None, h, C), lambda i, me_ref: (l, me_ref[0], 0)))
    alias = {} if buf is None else {"input_output_aliases": {3: 0}}
    args = (me, own, q) if buf is None else (me, own, q, buf)
    return _pallas(body, name=name, grid_spec=grid_spec, out_shape=_sds(shape, F32),
                   compiler_params=_params(("arbitrary",), VMEM_BIG), **alias)(*args)


def share_final(bufs):
    n = len(bufs)

    def body(*refs):
        ins, outs = refs[:n], refs[n:2 * n]
        ssem, rsem = refs[2 * n:]
        del ins
        x, y, c = _coords()
        copies = []
        for t in range(n):
            R = bufs[t].shape[1]
            r0 = pl.multiple_of(c * (R // 2), 8)
            blk = outs[t].at[:, pl.ds(r0, R // 2), :]
            copies.append(pltpu.make_async_remote_copy(
                src_ref=blk, dst_ref=blk, send_sem=ssem.at[t], recv_sem=rsem.at[t],
                device_id=(x, y, 1 - c), device_id_type=MESH))
        for cp in copies:
            cp.start()
        for t in range(n):
            R = bufs[t].shape[1]
            r1 = pl.multiple_of((1 - c) * (R // 2), 8)
            other = outs[t].at[:, pl.ds(r1, R // 2), :]
            pltpu.make_async_remote_copy(
                src_ref=other, dst_ref=other, send_sem=ssem.at[t], recv_sem=rsem.at[t],
                device_id=(x, y, 1 - c), device_id_type=MESH).wait_recv()
        for cp in copies:
            cp.wait_send()

    out_shape = [_sds(b.shape, F32) for b in bufs]
    return _pallas(body, name="share_final", in_specs=[ANY] * n, out_specs=[ANY] * n, out_shape=out_shape,
                   input_output_aliases={t: t for t in range(n)},
                   scratch_shapes=[pltpu.SemaphoreType.DMA((n,))] * 2,
                   compiler_params=pltpu.CompilerParams(has_side_effects=True))(*bufs)


def allreduce_small(part):
    rows = part.shape[0]
    h = rows // 2

    def body(p_ref, o_ref, sib_buf, pair_buf, chip_buf, ssem, rsem):
        x, y, c = _coords()
        sibling = dict(device_id=(x, y, 1 - c), device_id_type=MESH)
        mine = pl.ds(pl.multiple_of(c * h, 8), h)
        theirs = pl.ds(pl.multiple_of((1 - c) * h, 8), h)

        swap = pltpu.make_async_remote_copy(src_ref=p_ref.at[theirs], dst_ref=sib_buf, send_sem=ssem.at[0],
                                            recv_sem=rsem.at[0], **sibling)
        swap.start()
        swap.wait()
        pair_buf[...] = p_ref[mine, :] + sib_buf[...]

        chips = _other_chips(x, y)
        sends = [pltpu.make_async_remote_copy(src_ref=pair_buf, dst_ref=chip_buf.at[j], send_sem=ssem.at[1 + j],
                                              recv_sem=rsem.at[1 + j], device_id=(chips[j][0], chips[j][1], c),
                                              device_id_type=MESH) for j in range(3)]
        for cp in sends:
            cp.start()
        for cp in sends:
            cp.wait()

        def total(s):
            terms = {s: pair_buf[...]}
            for j in range(3):
                terms[_flip_index(s, j)] = chip_buf[j]
            o_ref[mine, :] = ((terms[0] + terms[1]) + terms[2]) + terms[3]

        _for_my_chip(2 * x + y, total)

        back = pltpu.make_async_remote_copy(src_ref=o_ref.at[mine], dst_ref=o_ref.at[mine], send_sem=ssem.at[4],
                                            recv_sem=rsem.at[4], **sibling)
        back.start()
        pltpu.make_async_remote_copy(src_ref=o_ref.at[theirs], dst_ref=o_ref.at[theirs], send_sem=ssem.at[4],
                                     recv_sem=rsem.at[4], **sibling).wait_recv()
        back.wait_send()

    return _pallas(body, name="allreduce_small",
                   in_specs=[pl.BlockSpec(memory_space=pltpu.VMEM)], out_specs=pl.BlockSpec(memory_space=pltpu.VMEM),
                   out_shape=_sds((rows, 128), F32),
                   scratch_shapes=[pltpu.VMEM((h, 128), F32), pltpu.VMEM((h, 128), F32), pltpu.VMEM((3, h, 128), F32),
                                   pltpu.SemaphoreType.DMA((5,)), pltpu.SemaphoreType.DMA((5,))],
                   compiler_params=pltpu.CompilerParams(has_side_effects=True))(part)


def _rows128(a):
    flat = a.reshape(-1)
    rows = -(-flat.shape[0] // 128)
    rows8 = -(-rows // 8) * 8
    flat = jnp.pad(flat, (0, rows8 * 128 - flat.shape[0]))
    return flat.reshape(rows8, 128)


def kernel(x, norm_mix_g, norm_ffn_g, final_g, a_w_in, a_v_gain, a_w_s, a_b_s, a_w_out, b_w_qkv, b_rel_bias, b_w_out, ffn_w_gate, ffn_w_up, ffn_w_down, loss_target, m_norm_mix_g, m_norm_ffn_g, m_final_g, m_a_w_in, m_a_v_gain, m_a_w_s, m_a_b_s, m_a_w_out, m_b_w_qkv, m_b_rel_bias, m_b_w_out, m_ffn_w_gate, m_ffn_w_up, m_ffn_w_down, v_norm_mix_g, v_norm_ffn_g, v_final_g, v_a_w_in, v_a_v_gain, v_a_w_s, v_a_b_s, v_a_w_out, v_b_w_qkv, v_b_rel_bias, v_b_w_out, v_ffn_w_gate, v_ffn_w_up, v_ffn_w_down):
    T = x.shape[1]
    weights = dict(norm_mix_g=norm_mix_g, norm_ffn_g=norm_ffn_g, final_g=final_g, a_w_in=a_w_in, a_v_gain=a_v_gain,
                   a_w_s=a_w_s, a_b_s=a_b_s, a_w_out=a_w_out, b_w_qkv=b_w_qkv, b_rel_bias=b_rel_bias,
                   b_w_out=b_w_out, ffn_w_gate=ffn_w_gate, ffn_w_up=ffn_w_up, ffn_w_down=ffn_w_down)
    mom_m = dict(norm_mix_g=m_norm_mix_g, norm_ffn_g=m_norm_ffn_g, final_g=m_final_g, a_w_in=m_a_w_in,
                 a_v_gain=m_a_v_gain, a_w_s=m_a_w_s, a_b_s=m_a_b_s, a_w_out=m_a_w_out, b_w_qkv=m_b_w_qkv,
                 b_rel_bias=m_b_rel_bias, b_w_out=m_b_w_out, ffn_w_gate=m_ffn_w_gate, ffn_w_up=m_ffn_w_up,
                 ffn_w_down=m_ffn_w_down)
    mom_v = dict(norm_mix_g=v_norm_mix_g, norm_ffn_g=v_norm_ffn_g, final_g=v_final_g, a_w_in=v_a_w_in,
                 a_v_gain=v_a_v_gain, a_w_s=v_a_w_s, a_b_s=v_a_b_s, a_w_out=v_a_w_out, b_w_qkv=v_b_w_qkv,
                 b_rel_bias=v_b_rel_bias, b_w_out=v_b_w_out, ffn_w_gate=v_ffn_w_gate, ffn_w_up=v_ffn_w_up,
                 ffn_w_down=v_ffn_w_down)
    order = list(weights)
    transposed = ("ffn_w_gate", "ffn_w_up")
    for k in transposed:
        weights[k], mom_m[k], mom_v[k] = (jnp.swapaxes(a, 1, 2) for a in (weights[k], mom_m[k], mom_v[k]))

    xi, yi, ci = _coords()
    me = jnp.stack([ci, 2 * xi + yi]).astype(jnp.int32)

    shard16 = {k: cast_bf16(weights[k], "cast_" + k) for k in BIG}

    def layer_tensors(i):
        mix = ("a_w_in", "a_w_out") if i % 2 == 0 else ("b_w_qkv", "b_w_out")
        return [(k, i // 2) for k in mix] + [(k, i) for k in ("ffn_w_gate", "ffn_w_up", "ffn_w_down")]

    def gather_items(keys):
        return [(shard16[k], l, GATHER_KIND[k]) for k, l in keys]

    def grad_metas(keys):
        return [(GATHER_KIND[k],) + tuple(weights[k].shape[1:]) for k, _ in keys]

    FFN = ("ffn_w_gate", "ffn_w_up", "ffn_w_down")
    k0a = [("a_w_out", 0), ("ffn_w_gate", 0)]
    k0b = [("ffn_w_up", 0), ("ffn_w_down", 0)]
    k1a = [("b_w_qkv", 0), ("b_w_out", 0), ("ffn_w_gate", 1)]
    k1b = [("ffn_w_up", 1), ("ffn_w_down", 1)]
    k3a = [("b_w_qkv", 1), ("b_w_out", 1), ("ffn_w_gate", 3)]
    k3b = [("ffn_w_up", 3), ("ffn_w_down", 3)]
    plans = {
        "a_in_l0": [("g1", k0a)], "sgu_fwd_l0": [("g2", k0a), ("g1", k0b)], "a_out_l0": [("g2", k0b)],
        "rms_mix_l0": [("g1", [("a_w_in", 0)])],
        "ffn_up_l0": [("g1", k1a)], "ffn_down_l0": [("g2", k1a), ("g1", k1b[:1])],
        "b_qkv_l1": [("g2", k1b[:1]), ("g1", k1b[1:])],
        "attn_fwd_l1": [("g2", k1b[1:]), ("g1", layer_tensors(2))], "b_out_l1": [("g2", layer_tensors(2))],
        "ffn_up_l1": [("g1", k3a)], "ffn_down_l1": [("g2", k3a)],
        "a_in_l2": [("g1", k3b)], "sgu_fwd_l2": [("g2", k3b)],
        "ffn_bwd_dh_l2": [("ex", layer_tensors(3))], "sgu_bwd_l2": [("sc", layer_tensors(3))],
        "ffn_bwd_dh_l1": [("ex", layer_tensors(2))], "attn_bwd_l1": [("sc", layer_tensors(2))],
        "ffn_bwd_dh_l0": [("ex", layer_tensors(1))], "ffn_bwd_dhn_l0": [("sc", k1a)],
        "dffn_w_gate_l0": [("sc", [("ffn_w_up", 1)])], "dffn_w_up_l0": [("sc", [("ffn_w_down", 1)])],
        "a_out_bwd_l0": [("ex", [(k, 0) for k in FFN])],
        "sgu_bwd_l0": [("sc", [("ffn_w_gate", 0), ("ffn_w_up", 0)]), ("ex", [("a_w_out", 0)])],
        "dw_in_l0": [("sc", [("ffn_w_down", 0), ("a_w_out", 0)])],
    }
    part16, full16 = {}, {}
    sib, p16, own_parts, recv_parts = {}, {}, {}, {}

    def make_comm(kind, keys):
        if kind == "g1":
            return gather_stage1(gather_items(keys)), lambda outs: part16.update(zip(keys, outs))
        if kind == "g2":
            return (gather_stage2(gather_items(keys), [part16[kl] for kl in keys]),
                    lambda outs: full16.update(zip(keys, outs)))
        if kind == "ex":
            return (exchange_halves([big_grads[k][l] for k, l in keys], grad_metas(keys)),
                    lambda outs: sib.update(zip(keys, outs)))
        for kl, m_ in zip(keys, grad_metas(keys)):
            p16[kl], own_parts[kl] = pair_sum(me, big_grads[kl[0]][kl[1]], sib[kl], m_, "pair_sum_%s_l%d" % kl)
        return (scatter_partials([p16[kl] for kl in keys], grad_metas(keys)),
                lambda outs: recv_parts.update(zip(keys, outs)))

    def run(name, make):
        steps = plans.get(name)
        if not steps:
            return make(None)
        made = [make_comm(kind, keys) for kind, keys in steps]
        main, outs = make(combine([c for c, _ in made]))
        for c, done in made:
            done(outs[:len(c.out_shapes)])
            outs = outs[len(c.out_shapes):]
        return main

    def weight(k, l):
        w = full16[(k, l)]
        if k == "a_w_out":
            return w.reshape(1, GH, D)
        return w.reshape(1, D, D) if k == "b_w_out" else w


    xcur = x.reshape(T, D)
    hn = run("rms_mix_l0", lambda comm: rms_fwd(xcur, norm_mix_g[0][None], "rms_mix_l0", comm=comm))
    comm, done = make_comm("g2", [("a_w_in", 0)])
    done(run_comm(comm, "gather_first_d2d"))
    saved = []
    for i in range(DEPTH):
        j = i // 2
        tag = "_l%d" % i
        st = {"x_in": xcur, "hn": hn}
        if i % 2 == 0:
            pre = run("a_in" + tag, lambda comm: matmul(
                "a_in" + tag, NN, hn, pl.BlockSpec((1024, D), lambda i_, j_: (i_, 0)),
                weight("a_w_in", j), pl.BlockSpec((None, D, 1024), lambda i_, j_: (0, 0, j_)),
                _sds((T, 2 * GH), BF16), pl.BlockSpec((1024, 1024), lambda i_, j_: (i_, j_)),
                (T // 1024, 4), comm=comm))
            y = run("sgu_fwd" + tag, lambda comm: sgu_fwd(
                pre, a_v_gain[j][None], a_w_s[j], a_b_s[j][:, :, None], "sgu_fwd" + tag, comm=comm))
            xmid, hn2 = run("a_out" + tag, lambda comm: residual_proj(
                "a_out" + tag, *out_proj(y, weight("a_w_out", j)), xcur, norm_ffn_g[i][None], comm=comm))
            st.update(pre=pre, y=y)
        else:
            qkvp = run("b_qkv" + tag, lambda comm: proj_qkv(hn, weight("b_w_qkv", j), 0, "b_qkv" + tag, comm=comm))
            wb = jnp.transpose(bias_build(b_rel_bias[j], "bias_build" + tag), (1, 0, 2))
            o = run("attn_fwd" + tag, lambda comm: attn_fwd(qkvp, wb, "attn_fwd" + tag, comm=comm))
            xmid, hn2 = run("b_out" + tag, lambda comm: residual_proj(
                "b_out" + tag, *out_proj(o, weight("b_w_out", j)), xcur, norm_ffn_g[i][None], comm=comm))
            st.update(qkvp=qkvp, wb=wb, o=o)
        g, u, h = run("ffn_up" + tag, lambda comm: ffn_up(
            hn2, weight("ffn_w_gate", i), weight("ffn_w_up", i), 0, "ffn_up" + tag, comm=comm))
        next_g = norm_mix_g[i + 1][None] if i + 1 < DEPTH else None
        down = run("ffn_down" + tag, lambda comm: residual_proj(
            "ffn_down" + tag, *ffn_down(h, weight("ffn_w_down", i)), xmid, next_g, comm=comm))
        xcur, hn = down if next_g is not None else (down, None)
        st.update(x_mid=xmid, hn2=hn2, g=g, u=u, h=h)
        saved.append(st)

    loss_part, dx, dxb, d_final = final_loss(xcur, final_g[None], loss_target.reshape(T, D), "final_loss")

    tk = min(2048, T)
    big_grads = {k: [None] * weights[k].shape[0] for k in BIG}
    small = {"norm_mix_g": [None] * DEPTH, "norm_ffn_g": [None] * DEPTH, "a_v_gain": [None] * 2,
             "a_w_s": [None] * 2, "a_b_s": [None] * 2, "b_rel_bias": [None] * 2}
    tok = lambda width: pl.BlockSpec((tk, width), lambda j_, k_: (k_, 0))
    part = lambda: pl.BlockSpec((None, tk, FS), lambda j_, k_: (j_, k_, 0))
    for i in reversed(range(DEPTH)):
        j = i // 2
        tag = "_l%d" % i
        st = saved[i]
        dg, du = run("ffn_bwd_dh" + tag, lambda comm: ffn_bwd_dh(
            dxb, weight("ffn_w_down", i), st["g"], st["u"], 0, "ffn_bwd_dh" + tag, comm=comm))
        big_grads["ffn_w_down"][i] = wgrad(
            "dw_down" + tag, st["h"], part(), dxb, tok(D), _sds((N_CHIPS, FS, D), F32),
            pl.BlockSpec((None, FS, D), lambda j_, k_: (j_, 0, 0)), N_CHIPS, T, tk)
        dx_mid, dxb_mid, dgn = run("ffn_bwd_dhn" + tag, lambda comm: dgrad_rms(
            "ffn_bwd_dhn" + tag, *ffn_dgrad(dg, du, weight("ffn_w_gate", i), weight("ffn_w_up", i)),
            st["x_mid"], norm_ffn_g[i][None], dx, comm=comm))
        for nm, dz in (("ffn_w_gate", dg), ("ffn_w_up", du)):
            big_grads[nm][i] = run("d" + nm + tag, lambda comm: wgrad(
                "d" + nm + tag, dz, part(), st["hn2"], tok(D), _sds((N_CHIPS, FS, D), F32),
                pl.BlockSpec((None, FS, D), lambda j_, k_: (j_, 0, 0)), N_CHIPS, T, tk, comm=comm))
        dx, dxb = dx_mid, dxb_mid
        small["norm_ffn_g"][i] = dgn
        if i % 2 == 0:
            dy = run("a_out_bwd" + tag, lambda comm: matmul(
                "a_out_bwd" + tag, NT, dxb, pl.BlockSpec((1024, D), lambda i_, j_: (i_, 0)),
                weight("a_w_out", j), pl.BlockSpec((None, 1024, D), lambda i_, j_: (0, j_, 0)),
                _sds((T, GH), BF16), pl.BlockSpec((1024, 1024), lambda i_, j_: (i_, j_)), (T // 1024, 2), comm=comm))
            big_grads["a_w_out"][j] = wgrad(
                "dw_aout" + tag, st["y"], pl.BlockSpec((tk, 1024), lambda j_, k_: (k_, j_)), dxb, tok(D),
                _sds((GH, D), F32), pl.BlockSpec((1024, D), lambda j_, k_: (j_, 0)), 2, T, tk
            ).reshape(N_CHIPS, GH // N_CHIPS, D)
            dpre, d_ws, d_bs, d_gain = run("sgu_bwd" + tag, lambda comm: sgu_bwd(
                st["pre"], dy, a_v_gain[j][None], a_w_s[j], a_b_s[j][:, :, None], "sgu_bwd" + tag,
                tm=2 * SGU_BLOCK, comm=comm))
            small["a_w_s"][j], small["a_b_s"][j], small["a_v_gain"][j] = d_ws, d_bs, d_gain
            dx_in, dxb_in, dgn = run("a_in_bwd" + tag, lambda comm: dgrad_rms(
                "a_in_bwd" + tag, *in_dgrad(dpre, weight("a_w_in", j)),
                st["x_in"], norm_mix_g[i][None], dx, comm=comm))
            big_grads["a_w_in"][j] = run("dw_in" + tag, lambda comm: wgrad(
                "dw_in" + tag, st["hn"], tok(D), dpre, pl.BlockSpec((tk, 1024), lambda j_, k_: (k_, j_)),
                _sds((D, 2 * GH), F32), pl.BlockSpec((D, 1024), lambda j_, k_: (0, j_)), 4, T, tk, comm=comm))
        else:
            do = matmul("b_out_bwd" + tag, NT, dxb, pl.BlockSpec((1024, D), lambda i_, j_: (i_, 0)),
                        weight("b_w_out", j), pl.BlockSpec((None, D, D), lambda i_, j_: (0, 0, 0)),
                        _sds((T, D), BF16), pl.BlockSpec((1024, D), lambda i_, j_: (i_, 0)), (T // 1024, 1))
            big_grads["b_w_out"][j] = wgrad(
                "dw_bout" + tag, st["o"], tok(D), dxb, tok(D),
                _sds((D, D), F32), pl.BlockSpec((D, D), lambda j_, k_: (0, 0)), 1, T, tk
            ).reshape(N_CHIPS, D // N_CHIPS, D)
            dqkvp, dwb = run("attn_bwd" + tag, lambda comm: attn_bwd(
                st["qkvp"], st["o"], do, st["wb"], "attn_bwd" + tag, comm=comm))
            small["b_rel_bias"][j] = bias_grad(
                jnp.pad(jnp.transpose(dwb, (1, 0, 2)), ((0, 0), (0, 0), (0, DIAG - KW))), "bias_grad" + tag)
            dx_in, dxb_in, dgn = dgrad_rms(
                "b_qkv_bwd" + tag, *qkv_dgrad(dqkvp, weight("b_w_qkv", j)),
                st["x_in"], norm_mix_g[i][None], dx)
            big_grads["b_w_qkv"][j] = wgrad(
                "dw_qkv" + tag, st["hn"], tok(D), dqkvp,
                pl.BlockSpec((None, tk, D), lambda j_, k_: (j_, k_ + FRONT // tk, 0)),
                _sds((D, 3 * D), F32), pl.BlockSpec((D, D), lambda j_, k_: (0, j_)), 3, T, tk)
        dx, dxb = dx_in, dxb_in
        small["norm_mix_g"][i] = dgn

    small_grads = {
        "norm_mix_g": jnp.concatenate(small["norm_mix_g"], axis=0),
        "norm_ffn_g": jnp.concatenate(small["norm_ffn_g"], axis=0),
        "final_g": d_final.reshape(D),
        "a_v_gain": jnp.concatenate(small["a_v_gain"], axis=0),
        "a_w_s": jnp.stack(small["a_w_s"]),
        "a_b_s": jnp.stack(small["a_b_s"]).reshape(2, SGU_G, SGU_BLOCK),
        "b_rel_bias": jnp.stack(small["b_rel_bias"]),
    }
    small_names = list(small_grads)
    packed = [_rows128(small_grads[k]) for k in small_names] + [_rows128(loss_part[:, :1])]
    offs = [0]
    for p in packed:
        offs.append(offs[-1] + p.shape[0])
    reduced = allreduce_small(jnp.concatenate(packed, axis=0))
    grads = {}
    for t, k in enumerate(small_names):
        nelem = small_grads[k].size
        grads[k] = reduced[offs[t]:offs[t + 1]].reshape(-1)[:nelem].reshape(weights[k].shape)
    loss = reduced[offs[len(small_names)], 0]

    last = [("a_w_in", 0)]
    for kind, name in (("ex", "exchange_last"), ("sc", "scatter_last")):
        comm, done = make_comm(kind, last)
        done(run_comm(comm, name))
    bufs = {k: None for k in BIG}
    for i in range(DEPTH):
        for kl, m_ in zip(layer_tensors(i), grad_metas(layer_tensors(i))):
            bufs[kl[0]] = final_sum(me, own_parts[kl], recv_parts[kl], bufs[kl[0]], weights[kl[0]].shape, kl[1], m_,
                                    "final_sum_%s_l%d" % kl)
    shared = share_final([bufs[k] for k in BIG])
    for k, gfull in zip(BIG, shared):
        grads[k] = gfull

    delta, new_m, new_v = {}, {}, {}
    for k in order:
        shp = weights[k].shape
        if k in BIG:
            view = shp
        elif k == "a_w_s":
            view = (2, SGU_G * SGU_BLOCK, SGU_BLOCK)
        elif len(shp) == 1:
            view = (1, 1, shp[0])
        elif len(shp) == 2:
            view = (1,) + shp
        else:
            view = shp
        g_, d_, m_, v_ = adamw(weights[k].reshape(view), grads[k].reshape(view), mom_m[k].reshape(view),
                               mom_v[k].reshape(view), "adamw_" + k)
        grads[k], delta[k], new_m[k], new_v[k] = g_.reshape(shp), d_.reshape(shp), m_.reshape(shp), v_.reshape(shp)
    for k in transposed:
        for tree in (grads, delta, new_m, new_v):
            tree[k] = jnp.swapaxes(tree[k], 1, 2)

    return (loss, dx.reshape(1, T, D), *[grads[k] for k in order], *[delta[k] for k in order],
            *[new_m[k] for k in order], *[new_v[k] for k in order])
```

```python
import functools

import jax
import jax.numpy as jnp
from jax import lax
from jax.experimental import pallas as pl
from jax.experimental.pallas import tpu as pltpu

F32 = jnp.float32
BF16 = jnp.bfloat16
MESH = pl.DeviceIdType.MESH

D = 1024
DEPTH = 4
EPS = 1e-6
SGU_BLOCK = 128
GH = 2048
SGU_G = 8
SGU_GD = GH // SGU_G
N_HEADS = 16
HEAD_DIM = 64
CHUNK = 64
PAD = 8 * CHUNK
FRONT = 2048
QB = 128
KW = PAD + QB
N_REL = 192
REL_MIN = -(CHUNK - 1)
REL_MAX = 128
D_FF = 2816
FS = D_FF // 4
NEG = -1e30
SCALE = HEAD_DIM ** -0.5
N_CHIPS = 4

ADAM_LR = 0.001
ADAM_B1 = 0.9
ADAM_B2 = 0.999
ADAM_EPS = 1e-08
ADAM_WD = 0.01
ADAM_STEP = 10

VMEM_BIG = 56 * 1024 * 1024

NN = ((1,), (0,))
NT = ((1,), (1,))
TN = ((0,), (0,))


def _dot(a, b, dims):
    return lax.dot_general(a, b, (dims, ((), ())), preferred_element_type=F32)


class Comm:
    def __init__(self, ins, out_shapes, sems, start, wait, aliases=None):
        self.ins, self.out_shapes, self.sems = list(ins), list(out_shapes), list(sems)
        self.start, self.wait, self.aliases = start, wait, dict(aliases or {})


def _host(body, comm, kw):
    grid = tuple(kw["grid"])
    in_specs = list(kw["in_specs"])
    single = not isinstance(kw["out_specs"], (list, tuple))
    out_specs = [kw["out_specs"]] if single else list(kw["out_specs"])
    out_shape = [kw["out_shape"]] if single else list(kw["out_shape"])
    scratch = list(kw.get("scratch_shapes", ()))
    counts = (len(in_specs), len(comm.ins), len(out_specs), len(comm.out_shapes), len(scratch))

    def hosted(*refs):
        parts, p = [], 0
        for cnt in counts:
            parts.append(refs[p:p + cnt])
            p += cnt
        main_in, c_in, main_out, c_out, main_scr = parts
        sems = refs[p:]
        ids = [pl.program_id(a) for a in range(len(grid))]
        first = functools.reduce(jnp.logical_and, [i == 0 for i in ids])
        last = functools.reduce(jnp.logical_and, [i == n - 1 for i, n in zip(ids, grid)])
        pl.when(first)(lambda: comm.start(c_in, c_out, sems))
        body(*main_in, *main_out, *main_scr)
        pl.when(last)(lambda: comm.wait(c_in, c_out, sems))

    old = kw["compiler_params"]
    kw = dict(kw, in_specs=in_specs + [ANY] * len(comm.ins), out_specs=out_specs + [ANY] * len(comm.out_shapes),
              out_shape=out_shape + comm.out_shapes, scratch_shapes=scratch + comm.sems,
              compiler_params=pltpu.CompilerParams(dimension_semantics=("arbitrary",) * len(grid),
                                                   vmem_limit_bytes=old.vmem_limit_bytes, has_side_effects=True))
    if comm.aliases:
        kw["input_output_aliases"] = {counts[0] + i: counts[2] + o for i, o in comm.aliases.items()}
    return hosted, kw


def _pallas(body, comm=None, **kw):
    if comm is not None:
        body, kw = _host(body, comm, kw)
    return pl.pallas_call(body, **kw)


def _split_outs(outs, comm, n_main):
    outs = list(outs) if isinstance(outs, (list, tuple)) else [outs]
    main = outs[:n_main]
    return (main[0] if n_main == 1 else main), outs[n_main:]


def run_comm(comm, name):
    nci, nco = len(comm.ins), len(comm.out_shapes)

    def body(*refs):
        c_in, c_out, sems = refs[:nci], refs[nci:nci + nco], refs[nci + nco:]
        comm.start(c_in, c_out, sems)
        comm.wait(c_in, c_out, sems)

    kw = {}
    if comm.aliases:
        kw["input_output_aliases"] = dict(comm.aliases)
    return _pallas(body, name=name, in_specs=[ANY] * nci, out_specs=[ANY] * nco, out_shape=comm.out_shapes,
                   scratch_shapes=comm.sems, compiler_params=pltpu.CompilerParams(has_side_effects=True),
                   **kw)(*comm.ins)


def combine(comms):
    if len(comms) == 1:
        return comms[0]
    spans, ni, no, ns = [], 0, 0, 0
    for c in comms:
        spans.append((slice(ni, ni + len(c.ins)), slice(no, no + len(c.out_shapes)), slice(ns, ns + len(c.sems))))
        ni, no, ns = ni + len(c.ins), no + len(c.out_shapes), ns + len(c.sems)

    def start(ins, outs, sems):
        for c, (si, so, ss) in zip(comms, spans):
            c.start(ins[si], outs[so], sems[ss])

    def wait(ins, outs, sems):
        for c, (si, so, ss) in zip(comms, spans):
            c.wait(ins[si], outs[so], sems[ss])

    aliases = {}
    for c, (si, so, _) in zip(comms, spans):
        aliases.update({si.start + i: so.start + o for i, o in c.aliases.items()})
    return Comm([a for c in comms for a in c.ins], [o for c in comms for o in c.out_shapes],
                [s for c in comms for s in c.sems], start, wait, aliases)


def _call(body, comm, n_main, args, **kw):
    if comm is None:
        return _pallas(body, **kw)(*args)
    return _split_outs(_pallas(body, comm=comm, **kw)(*args, *comm.ins), comm, n_main)


def _params(sem=None, vmem=None):
    return pltpu.CompilerParams(dimension_semantics=sem, vmem_limit_bytes=vmem)


def _sds(shape, dtype):
    return jax.ShapeDtypeStruct(tuple(shape), dtype)


_GELU_C = 0.7978845608028654


_GELU_A = _GELU_C * 0.044715


def _gelu(x):
    t = jnp.tanh(x * (_GELU_C + _GELU_A * (x * x)))
    h = 0.5 * x
    return h + h * t


def _gelu_and_grad(x):
    x2 = x * x
    t = jnp.tanh(x * (_GELU_C + _GELU_A * x2))
    h = 0.5 * x
    val = h + h * t
    grad = (0.5 + 0.5 * t) + (h * (1.0 - t * t)) * (_GELU_C + (3.0 * _GELU_A) * x2)
    return val, grad


def _sigmoid(x):
    return 0.5 * (jnp.tanh(0.5 * x) + 1.0)


def cast_bf16(w, name):
    L, R, C = w.shape

    def body(w_ref, o_ref):
        o_ref[...] = w_ref[...].astype(BF16)

    spec = pl.BlockSpec((None, R, C), lambda l: (l, 0, 0))
    return _pallas(body, name=name, grid=(L,), in_specs=[spec], out_specs=spec,
                   out_shape=_sds((L, R, C), BF16), compiler_params=_params(("parallel",)))(w)


def rms_fwd(x, g, name, tm=512, comm=None):
    T = x.shape[0]

    def body(x_ref, g_ref, o_ref):
        o_ref[...] = _rms_rows(x_ref[...], g_ref[...])

    row = pl.BlockSpec((tm, D), lambda i: (i, 0))
    return _call(body, comm, 1, (x, g), name=name, grid=(T // tm,),
                 in_specs=[row, pl.BlockSpec((1, D), lambda i: (0, 0))], out_specs=row,
                 out_shape=_sds((T, D), BF16), compiler_params=_params(("parallel",)))


def dgrad_rms(name, compute, args, specs, x, g, dres, tm=512, comm=None):
    T = x.shape[0]
    n = T // tm
    k = len(args)

    def body(*refs):
        x_ref, g_ref, dres_ref, dx_ref, dxb_ref, dg_ref, acc_ref = refs[k:]
        i = pl.program_id(0)
        xf = x_ref[...]
        r = lax.rsqrt(jnp.mean(xf * xf, axis=-1, keepdims=True) + EPS)
        xhat = xf * r
        dhf = compute(*refs[:k])
        part = (dhf * xhat).reshape(tm // 8, 8, D).sum(axis=0)

        @pl.when(i == 0)
        def _():
            acc_ref[...] = part

        @pl.when(i > 0)
        def _():
            acc_ref[...] += part

        dxhat = dhf * g_ref[...]
        dx = dres_ref[...] + r * (dxhat - xhat * jnp.mean(dxhat * xhat, axis=-1, keepdims=True))
        dx_ref[...] = dx
        dxb_ref[...] = dx.astype(BF16)

        @pl.when(i == n - 1)
        def _():
            dg_ref[...] = jnp.sum(acc_ref[...], axis=0, keepdims=True)

    row = pl.BlockSpec((tm, D), lambda i: (i, 0))
    vec = pl.BlockSpec((1, D), lambda i: (0, 0))
    return _call(body, comm, 3, (*args, x, g, dres), name=name, grid=(n,),
                 in_specs=list(specs) + [row, vec, row], out_specs=[row, row, vec],
                 out_shape=[_sds((T, D), F32), _sds((T, D), BF16), _sds((1, D), F32)],
                 scratch_shapes=[pltpu.VMEM((8, D), F32)],
                 compiler_params=_params(("arbitrary",), VMEM_BIG))


def final_loss(x, g, tgt, name, tm=256):
    T = x.shape[0]
    n = T // tm

    def body(x_ref, g_ref, t_ref, loss_ref, dx_ref, dxb_ref, dg_ref, acc_ref, lacc_ref):
        i = pl.program_id(0)
        xf = x_ref[...]
        r = lax.rsqrt(jnp.mean(xf * xf, axis=-1, keepdims=True) + EPS)
        xhat = xf * r
        gg = g_ref[...]
        e = xhat * gg - t_ref[...]
        dy = e * (1.0 / D)
        part = (dy * xhat).reshape(tm // 8, 8, D).sum(axis=0)
        lpart = (e * e).reshape(tm // 8, 8, D).sum(axis=0)

        @pl.when(i == 0)
        def _():
            acc_ref[...] = part
            lacc_ref[...] = lpart

        @pl.when(i > 0)
        def _():
            acc_ref[...] += part
            lacc_ref[...] += lpart

        dxhat = dy * gg
        dx = r * (dxhat - xhat * jnp.mean(dxhat * xhat, axis=-1, keepdims=True))
        dx_ref[...] = dx
        dxb_ref[...] = dx.astype(BF16)

        @pl.when(i == n - 1)
        def _():
            dg_ref[...] = jnp.sum(acc_ref[...], axis=0, keepdims=True)
            total = jnp.sum(jnp.sum(lacc_ref[...], axis=0, keepdims=True), axis=1, keepdims=True)
            loss_ref[...] = jnp.broadcast_to(total * (0.5 / D), (1, 128))

    row = pl.BlockSpec((tm, D), lambda i: (i, 0))
    vec = pl.BlockSpec((1, D), lambda i: (0, 0))
    return _pallas(body, name=name, grid=(n,), in_specs=[row, vec, row],
                   out_specs=[pl.BlockSpec((1, 128), lambda i: (0, 0)), row, row, vec],
                   out_shape=[_sds((1, 128), F32), _sds((T, D), F32), _sds((T, D), BF16), _sds((1, D), F32)],
                   scratch_shapes=[pltpu.VMEM((8, D), F32), pltpu.VMEM((8, D), F32)],
                   compiler_params=_params(("arbitrary",)))(x, g, tgt)


def matmul(name, dims, a, a_spec, b, b_spec, out_shape, out_spec, grid, *, acc=False, res=None, res_spec=None,
           comm=None):
    has_res = res is not None

    def body(*refs):
        a_ref, b_ref = refs[0], refs[1]
        r_ref = refs[2] if has_res else None
        o_ref = refs[-1]
        d = _dot(a_ref[...], b_ref[...], dims)
        if not acc:
            if has_res:
                d = d + r_ref[...]
            o_ref[...] = d.astype(o_ref.dtype)
        else:
            k = pl.program_id(len(grid) - 1)

            @pl.when(k == 0)
            def _():
                o_ref[...] = (d + r_ref[...]) if has_res else d

            @pl.when(k > 0)
            def _():
                o_ref[...] += d

    sem = ("parallel",) * (len(grid) - 1) + (("arbitrary",) if acc else ("parallel",))
    ins = [a, b] + ([res] if has_res else [])
    specs = [a_spec, b_spec] + ([res_spec] if has_res else [])
    return _call(body, comm, 1, ins, name=name, grid=grid, in_specs=specs, out_specs=out_spec, out_shape=out_shape,
                 compiler_params=_params(sem, VMEM_BIG))


def wgrad(name, a, a_spec, b, b_spec, out_shape, out_spec, J, T, tk, comm=None):
    return matmul(name, TN, a, a_spec, b, b_spec, out_shape, out_spec, (J, T // tk), acc=True, comm=comm)


def _sgu_mask():
    p = lax.broadcasted_iota(jnp.int32, (SGU_BLOCK, SGU_BLOCK), 0)
    q = lax.broadcasted_iota(jnp.int32, (SGU_BLOCK, SGU_BLOCK), 1)
    return lax.shift_right_logical(q, 6) <= lax.shift_right_logical(p, 6)


def sgu_fwd(pre, gain, w_s, b_s, name, comm=None):
    T = pre.shape[0]

    def body(pre_ref, gain_ref, ws_ref, bs_ref, y_ref):
        mask = _sgu_mask()
        u = _gelu(pre_ref[:, :GH].astype(F32))
        va = _gelu(pre_ref[:, GH:].astype(F32))
        r = lax.rsqrt(jnp.mean(va * va, axis=-1, keepdims=True) + EPS)
        vn = ((va * r) * gain_ref[...]).astype(BF16)
        for g in range(SGU_G):
            sl = slice(g * SGU_GD, (g + 1) * SGU_GD)
            wm = jnp.where(mask, ws_ref[g], 0.0).astype(BF16)
            vm = _dot(wm, vn[:, sl], NN) + bs_ref[g]
            y_ref[:, sl] = (u[:, sl] * vm).astype(BF16)

    return _call(
        body, comm, 1, (pre, gain, w_s, b_s), name=name, grid=(T // SGU_BLOCK,),
        in_specs=[pl.BlockSpec((SGU_BLOCK, 2 * GH), lambda i: (i, 0)),
                  pl.BlockSpec((1, GH), lambda i: (0, 0)),
                  pl.BlockSpec((SGU_G, SGU_BLOCK, SGU_BLOCK), lambda i: (0, 0, 0)),
                  pl.BlockSpec((SGU_G, SGU_BLOCK, 1), lambda i: (0, 0, 0))],
        out_specs=pl.BlockSpec((SGU_BLOCK, GH), lambda i: (i, 0)),
        out_shape=_sds((T, GH), BF16), compiler_params=_params(("parallel",)))


def sgu_bwd(pre, dy, gain, w_s, b_s, name, tm=SGU_BLOCK, comm=None):
    T = pre.shape[0]
    n = T // tm

    def body(pre_ref, dy_ref, gain_ref, ws_ref, bs_ref, dpre_ref, dws_ref, dbs_ref, dgain_ref, gacc_ref):
        i = pl.program_id(0)

        @pl.when(i == 0)
        def _():
            dws_ref[...] = jnp.zeros_like(dws_ref)
            dbs_ref[...] = jnp.zeros_like(dbs_ref)
            gacc_ref[...] = jnp.zeros_like(gacc_ref)

        mask = _sgu_mask()
        gain_v = gain_ref[...]
        for sb in range(tm // SGU_BLOCK):
            rows = slice(sb * SGU_BLOCK, (sb + 1) * SGU_BLOCK)
            u, du_dpre = _gelu_and_grad(pre_ref[rows, :GH].astype(F32))
            va, dva_dpre = _gelu_and_grad(pre_ref[rows, GH:].astype(F32))
            r = lax.rsqrt(jnp.mean(va * va, axis=-1, keepdims=True) + EPS)
            vhat = va * r
            vn = (vhat * gain_v).astype(BF16)
            dyf = dy_ref[rows, :].astype(F32)
            dvn_parts = []
            for grp in range(SGU_G):
                sl = slice(grp * SGU_GD, (grp + 1) * SGU_GD)
                wm = jnp.where(mask, ws_ref[grp], 0.0).astype(BF16)
                vm = _dot(wm, vn[:, sl], NN) + bs_ref[grp]
                dpre_ref[rows, sl] = ((dyf[:, sl] * vm) * du_dpre[:, sl]).astype(BF16)
                dvm = dyf[:, sl] * u[:, sl]
                dbs_ref[grp] += jnp.sum(dvm, axis=-1, keepdims=True)
                dvm16 = dvm.astype(BF16)
                dws_ref[grp] += jnp.where(mask, _dot(dvm16, vn[:, sl], NT), 0.0)
                dvn_parts.append(_dot(wm, dvm16, TN))
            dvn = jnp.concatenate(dvn_parts, axis=-1)
            gacc_ref[...] += (dvn * vhat).reshape(SGU_BLOCK // 8, 8, GH).sum(axis=0)
            dvhat = dvn * gain_v
            dva = r * (dvhat - vhat * jnp.mean(dvhat * vhat, axis=-1, keepdims=True))
            dpre_ref[rows, GH:] = (dva * dva_dpre).astype(BF16)

        @pl.when(i == n - 1)
        def _():
            dgain_ref[...] = jnp.sum(gacc_ref[...], axis=0, keepdims=True)

    const3 = lambda i: (0, 0, 0)
    return _call(
        body, comm, 4, (pre, dy, gain, w_s, b_s), name=name, grid=(n,),
        in_specs=[pl.BlockSpec((tm, 2 * GH), lambda i: (i, 0)),
                  pl.BlockSpec((tm, GH), lambda i: (i, 0)),
                  pl.BlockSpec((1, GH), lambda i: (0, 0)),
                  pl.BlockSpec((SGU_G, SGU_BLOCK, SGU_BLOCK), const3),
                  pl.BlockSpec((SGU_G, SGU_BLOCK, 1), const3)],
        out_specs=[pl.BlockSpec((tm, 2 * GH), lambda i: (i, 0)),
                   pl.BlockSpec((SGU_G, SGU_BLOCK, SGU_BLOCK), const3),
                   pl.BlockSpec((SGU_G, SGU_BLOCK, 1), const3),
                   pl.BlockSpec((1, GH), lambda i: (0, 0))],
        out_shape=[_sds((T, 2 * GH), BF16), _sds((SGU_G, SGU_BLOCK, SGU_BLOCK), F32),
                   _sds((SGU_G, SGU_BLOCK, 1), F32), _sds((1, GH), F32)],
        scratch_shapes=[pltpu.VMEM((8, GH), F32)],
        compiler_params=_params(("arbitrary",)))


DIAG = 768


def _diag_onehot():
    n = lax.broadcasted_iota(jnp.int32, (N_REL, DIAG), 1)
    r = lax.broadcasted_iota(jnp.int32, (N_REL, DIAG), 0)
    idx = jnp.clip(KW - 1 - n, REL_MIN, REL_MAX) - REL_MIN
    return (idx == r).astype(BF16)


def _split3(v):
    hi = v.astype(BF16)
    r1 = v - hi.astype(F32)
    mid = r1.astype(BF16)
    lo = (r1 - mid.astype(F32)).astype(BF16)
    return hi, mid, lo


def bias_build(rel_bias, name):
    def body(rb_ref, o_ref):
        oh = _diag_onehot()
        hi, mid, lo = _split3(rb_ref[...])
        u = (_dot(hi, oh, NN) + _dot(mid, oh, NN) + _dot(lo, oh, NN)) * LOG2E
        j = lax.broadcasted_iota(jnp.int32, (1, KW), 1)

        def row(i, carry):
            val = pltpu.roll(u, (i + (DIAG - QB + 1)) % DIAG, 1)[:, :KW]
            rel = lax.shift_right_logical(i, 6) - lax.shift_right_logical(j, 6) + 8
            ok = (rel >= 0) & (rel <= 8)
            o_ref[i] = jnp.where(ok, val, NEG)
            return carry

        lax.fori_loop(0, QB, row, 0)

    return _pallas(body, name=name, out_shape=_sds((QB, N_HEADS, KW), F32),
                   in_specs=[pl.BlockSpec(memory_space=pltpu.VMEM)],
                   out_specs=pl.BlockSpec(memory_space=pltpu.VMEM))(rel_bias)


def bias_grad(dwb, name):
    def body(d_ref, o_ref):
        def row(i, acc):
            return acc + pltpu.roll(d_ref[i], QB - 1 - i, 1)

        du = lax.fori_loop(0, QB, row, jnp.zeros((N_HEADS, DIAG), F32))
        oh = _diag_onehot()
        hi, mid, lo = _split3(du)
        o_ref[...] = _dot(hi, oh, NT) + _dot(mid, oh, NT) + _dot(lo, oh, NT)

    return _pallas(body, name=name, out_shape=_sds((N_HEADS, N_REL), F32),
                   in_specs=[pl.BlockSpec(memory_space=pltpu.VMEM)],
                   out_specs=pl.BlockSpec(memory_space=pltpu.VMEM))(dwb)


LOG2E = 1.4426950408889634
Q_SCALE = SCALE * LOG2E


def _attn_block(qkv_ref, blk, masked):
    r0 = pl.multiple_of(blk * QB, QB)
    qs = qkv_ref[0, pl.ds(r0 + FRONT, QB), :]
    kvalid = (lax.broadcasted_iota(jnp.int32, (1, KW), 1) >= PAD - blk * QB) if masked else None
    return r0, qs, kvalid


def _step_windows(qkv_ref, b, step):
    r0 = pl.multiple_of(b * step, QB) + (FRONT - PAD)
    out = []
    for part in (1, 2):
        a = qkv_ref[part, pl.ds(r0, PAD + step), :]
        zero = jnp.zeros_like(a)
        out.append([jnp.where(_head_mask(h), a, zero) for h in range(2)])
    return out


def _window(stacks, t):
    return jnp.concatenate([s[t * QB:t * QB + KW] for s in stacks], axis=0)


def _head_mask(h):
    lane = lax.broadcasted_iota(jnp.int32, (1, 2 * HEAD_DIM), 1)
    return (lane < HEAD_DIM) if h == 0 else (lane >= HEAD_DIM)


def _stack_heads(a):
    zero = jnp.zeros_like(a)
    return jnp.concatenate([jnp.where(_head_mask(0), a, zero), jnp.where(_head_mask(1), a, zero)], axis=0)


def _rows_by_head(a):
    return jnp.concatenate([a[:, :KW], a[:, KW:]], axis=0)


def _per_head(lo, hi):
    return jnp.where(_head_mask(0), lo, hi)


def _attn_exp(qs, kst, w_ref, kvalid):
    s = _dot(qs, kst, NT) + jnp.concatenate([w_ref[0], w_ref[1]], axis=1)
    if kvalid is not None:
        s = jnp.where(jnp.concatenate([kvalid, kvalid], axis=1), s, NEG)
    es, invs = [], []
    for h in range(2):
        sh = s[:, h * KW:(h + 1) * KW]
        eh = jnp.exp2(sh - jnp.max(sh, axis=-1, keepdims=True))
        es.append(eh)
        invs.append(1.0 / jnp.sum(eh, axis=-1, keepdims=True))
    return jnp.concatenate(es, axis=1), invs


ATTN_G = 8


def _blocks_per_step(T):
    return min(ATTN_G, T // QB)


def _masked_and_not(b, fn, step):
    n_masked = -(-PAD // step)
    pl.when(b < n_masked)(functools.partial(fn, True))
    pl.when(b >= n_masked)(functools.partial(fn, False))


def attn_fwd(qkvp, wb, name, comm=None):
    T = qkvp.shape[1] - FRONT
    G = _blocks_per_step(T)

    def body(qkv_ref, w_ref, o_ref):
        b = pl.program_id(1)

        def blocks(masked):
            keys, values = _step_windows(qkv_ref, b, G * QB)
            for t in range(G):
                _, qs, kvalid = _attn_block(qkv_ref, b * G + t, masked)
                e, inv = _attn_exp(qs, _window(keys, t), w_ref, kvalid)
                o = _dot(e.astype(BF16), _window(values, t), NN) * _per_head(*inv)
                o_ref[t * QB:(t + 1) * QB, :] = o.astype(BF16)

        _masked_and_not(b, blocks, G * QB)

    return _call(
        body, comm, 1, (qkvp, wb), name=name, grid=(N_HEADS // 2, T // (QB * G)),
        in_specs=[pl.BlockSpec((3, FRONT + T, 2 * HEAD_DIM), lambda hp, b: (0, 0, hp)),
                  pl.BlockSpec((2, QB, KW), lambda hp, b: (hp, 0, 0))],
        out_specs=pl.BlockSpec((QB * G, 2 * HEAD_DIM), lambda hp, b: (b, hp)),
        out_shape=_sds((T, D), BF16),
        compiler_params=_params(("parallel", "arbitrary"), VMEM_BIG))


def attn_bwd(qkvp, o, do, wb, name, comm=None):
    T = qkvp.shape[1] - FRONT
    G = _blocks_per_step(T)
    nb = T // (QB * G)

    def body(qkv_ref, o_ref, do_ref, w_ref, dqkv_ref, dw_ref, dk_acc, dv_acc):
        b = pl.program_id(1)

        @pl.when(b == 0)
        def _():
            dk_acc[...] = jnp.zeros_like(dk_acc)
            dv_acc[...] = jnp.zeros_like(dv_acc)
            dw_ref[...] = jnp.zeros_like(dw_ref)
            dqkv_ref[0, 0:FRONT, :] = jnp.zeros((FRONT, 2 * HEAD_DIM), BF16)

        def blocks(masked):
            dws = None
            keys, values = _step_windows(qkv_ref, b, G * QB)
            for t in range(G):
                r0, qs, kvalid = _attn_block(qkv_ref, b * G + t, masked)
                kst = _window(keys, t)
                e, inv = _attn_exp(qs, kst, w_ref, kvalid)
                do2 = do_ref[t * QB:(t + 1) * QB, :]
                dof = do2.astype(F32)
                prod = dof * o_ref[t * QB:(t + 1) * QB, :].astype(F32)
                dp = _dot(do2, _window(values, t), NT)
                parts = []
                for h in range(2):
                    delta = jnp.sum(jnp.where(_head_mask(h), prod, 0.0), axis=-1, keepdims=True)
                    half = slice(h * KW, (h + 1) * KW)
                    parts.append(e[:, half] * ((dp[:, half] - delta) * inv[h]))
                ds = jnp.concatenate(parts, axis=1)
                dws = ds if dws is None else dws + ds
                ds16 = ds.astype(BF16)
                dqkv_ref[0, pl.ds(r0 + FRONT, QB), :] = (_dot(ds16, kst, NN) * SCALE).astype(BF16)
                dk_acc[pl.ds(r0 + (FRONT - PAD), KW), :] += _dot(_rows_by_head(ds16), _stack_heads(qs), TN)
                dv_acc[pl.ds(r0 + (FRONT - PAD), KW), :] += _dot(
                    _rows_by_head(e.astype(BF16)), _stack_heads((dof * _per_head(*inv)).astype(BF16)), TN)
            dw_ref[0] += dws[:, :KW]
            dw_ref[1] += dws[:, KW:]

        _masked_and_not(b, blocks, G * QB)

        @pl.when(b == nb - 1)
        def _():
            dqkv_ref[1] = (dk_acc[...] * (1.0 / LOG2E)).astype(BF16)
            dqkv_ref[2] = dv_acc[...].astype(BF16)

    slab = pl.BlockSpec((3, FRONT + T, 2 * HEAD_DIM), lambda hp, b: (0, 0, hp))
    wspec = pl.BlockSpec((2, QB, KW), lambda hp, b: (hp, 0, 0))
    rows = pl.BlockSpec((QB * G, 2 * HEAD_DIM), lambda hp, b: (b, hp))
    return _call(
        body, comm, 2, (qkvp, o, do, wb), name=name, grid=(N_HEADS // 2, nb),
        in_specs=[slab, rows, rows, wspec],
        out_specs=[slab, wspec],
        out_shape=[_sds((3, FRONT + T, D), BF16), _sds((N_HEADS, QB, KW), F32)],
        scratch_shapes=[pltpu.VMEM((FRONT + T, 2 * HEAD_DIM), F32), pltpu.VMEM((FRONT + T, 2 * HEAD_DIM), F32)],
        compiler_params=_params(("parallel", "arbitrary"), VMEM_BIG))


def proj_qkv(hn, w, l, name, tm=512, comm=None):
    T = hn.shape[0]
    pb = FRONT // tm

    def body(a_ref, b_ref, o_ref):
        i = pl.program_id(1)

        @pl.when(i < pb)
        def _():
            o_ref[...] = jnp.zeros_like(o_ref)

        @pl.when(i >= pb)
        def _():
            scale = jnp.where(pl.program_id(0) == 0, Q_SCALE, 1.0).astype(F32)
            o_ref[...] = (_dot(a_ref[...], b_ref[...], NN) * scale).astype(BF16)

    return _call(
        body, comm, 1, (hn, w), name=name, grid=(3, pb + T // tm),
        in_specs=[pl.BlockSpec((tm, D), lambda p, i: (jnp.maximum(i - pb, 0), 0)),
                  pl.BlockSpec((None, D, D), lambda p, i: (l, 0, p))],
        out_specs=pl.BlockSpec((None, tm, D), lambda p, i: (p, i, 0)),
        out_shape=_sds((3, FRONT + T, D), BF16),
        compiler_params=_params(("parallel", "parallel"), VMEM_BIG))


def ffn_up(hn, wg, wu, l, name, tm=1024, comm=None):
    T = hn.shape[0]

    def body(a_ref, wg_ref, wu_ref, g_ref, u_ref, h_ref):
        a = a_ref[...]
        g = _dot(a, wg_ref[...], NT)
        u = _dot(a, wu_ref[...], NT)
        s = _sigmoid(g)
        silu = g * s
        g_ref[...] = (u * (s * (1.0 + g * (1.0 - s)))).astype(BF16)
        u_ref[...] = silu.astype(BF16)
        h_ref[...] = (silu * u).astype(BF16)

    wspec = pl.BlockSpec((None, None, FS, D), lambda s, i: (l, s, 0, 0))
    ospec = pl.BlockSpec((None, tm, FS), lambda s, i: (s, i, 0))
    return _call(
        body, comm, 3, (hn, wg, wu), name=name, grid=(N_CHIPS, T // tm),
        in_specs=[pl.BlockSpec((tm, D), lambda s, i: (i, 0)), wspec, wspec],
        out_specs=[ospec, ospec, ospec],
        out_shape=[_sds((N_CHIPS, T, FS), BF16)] * 3,
        compiler_params=_params(("parallel", "parallel"), VMEM_BIG))


def ffn_bwd_dh(dxb, wd, g, u, l, name, tm=2048, comm=None):
    T = dxb.shape[0]
    tm = min(tm, T)

    def body(a_ref, wd_ref, g_ref, u_ref, dg_ref, du_ref):
        dh = _dot(a_ref[...], wd_ref[...], NT)
        dg_ref[...] = (dh * g_ref[...].astype(F32)).astype(BF16)
        du_ref[...] = (dh * u_ref[...].astype(F32)).astype(BF16)

    aspec = pl.BlockSpec((None, tm, FS), lambda i, s: (s, i, 0))
    return _call(
        body, comm, 2, (dxb, wd, g, u), name=name, grid=(T // tm, N_CHIPS),
        in_specs=[pl.BlockSpec((tm, D), lambda i, s: (i, 0)),
                  pl.BlockSpec((None, None, FS, D), lambda i, s: (l, s, 0, 0)), aspec, aspec],
        out_specs=[aspec, aspec],
        out_shape=[_sds((N_CHIPS, T, FS), BF16)] * 2,
        compiler_params=_params(("parallel", "parallel"), VMEM_BIG))


def ffn_dgrad(dg, du, wg, wu, tm=512):
    def compute(dg_ref, du_ref, wg_ref, wu_ref):
        d = None
        for s in range(N_CHIPS):
            t = _dot(dg_ref[s], wg_ref[s], NN) + _dot(du_ref[s], wu_ref[s], NN)
            d = t if d is None else d + t
        return d

    aspec = pl.BlockSpec((N_CHIPS, tm, FS), lambda i: (0, i, 0))
    wspec = pl.BlockSpec((None, N_CHIPS, FS, D), lambda i: (0, 0, 0, 0), pipeline_mode=pl.Buffered(1))
    return compute, (dg, du, wg, wu), [aspec, aspec, wspec, wspec]


def qkv_dgrad(dqkvp, w, tm=512):
    def compute(a_ref, w_ref):
        d = None
        for p in range(3):
            t = _dot(a_ref[p], w_ref[:, p * D:(p + 1) * D], NT)
            d = t if d is None else d + t
        return d

    return compute, (dqkvp, w), [pl.BlockSpec((3, tm, D), lambda i: (0, i + FRONT // tm, 0)),
                                 pl.BlockSpec((None, D, 3 * D), lambda i: (0, 0, 0))]


def in_dgrad(dpre, w, tm=512):
    def compute(a_ref, w_ref):
        return _dot(a_ref[...], w_ref[...], NT)

    return compute, (dpre, w), [pl.BlockSpec((tm, 2 * GH), lambda i: (i, 0)),
                                pl.BlockSpec((None, D, 2 * GH), lambda i: (0, 0, 0))]


def _rms_rows(x, g):
    r = lax.rsqrt(jnp.mean(x * x, axis=-1, keepdims=True) + EPS)
    return ((x * r) * g).astype(BF16)


def residual_proj(name, compute, args, specs, res, norm_g, tm=512, comm=None):
    T = res.shape[0]
    k = len(args)
    with_norm = norm_g is not None

    def body(*refs):
        d = refs[k][...] + compute(*refs[:k])
        if with_norm:
            refs[k + 2][...] = d
            refs[k + 3][...] = _rms_rows(d, refs[k + 1][...])
        else:
            refs[k + 1][...] = d

    row = pl.BlockSpec((tm, D), lambda i: (i, 0))
    vec = pl.BlockSpec((1, D), lambda i: (0, 0))
    if with_norm:
        return _call(body, comm, 2, (*args, res, norm_g), name=name, grid=(T // tm,),
                     in_specs=list(specs) + [row, vec], out_specs=[row, row],
                     out_shape=[_sds((T, D), F32), _sds((T, D), BF16)],
                     compiler_params=_params(("parallel",), VMEM_BIG))
    return _call(body, comm, 1, (*args, res), name=name, grid=(T // tm,), in_specs=list(specs) + [row],
                 out_specs=row, out_shape=_sds((T, D), F32), compiler_params=_params(("parallel",), VMEM_BIG))


def ffn_down(h, wd, tm=512):
    def compute(h_ref, wd_ref):
        d = None
        for s in range(N_CHIPS):
            t = _dot(h_ref[s], wd_ref[s], NN)
            d = t if d is None else d + t
        return d

    return compute, (h, wd), [pl.BlockSpec((N_CHIPS, tm, FS), lambda i: (0, i, 0)),
                              pl.BlockSpec((None, N_CHIPS, FS, D), lambda i: (0, 0, 0, 0))]


def out_proj(a, w, tm=512):
    K = a.shape[1]

    def compute(a_ref, w_ref):
        return _dot(a_ref[...], w_ref[...], NN)

    return compute, (a, w), [pl.BlockSpec((tm, K), lambda i: (i, 0)), pl.BlockSpec((None, K, D), lambda i: (0, 0, 0))]


def adamw(w, g, m, v, name):
    L, R, C = w.shape

    def body(w_ref, g_ref, m_ref, v_ref, go_ref, d_ref, nm_ref, nv_ref):
        gf = g_ref[...]
        go_ref[...] = gf
        nm = ADAM_B1 * m_ref[...] + (1.0 - ADAM_B1) * gf
        nv = ADAM_B2 * v_ref[...] + (1.0 - ADAM_B2) * (gf * gf)
        m_hat = nm / (1.0 - ADAM_B1 ** ADAM_STEP)
        v_hat = nv / (1.0 - ADAM_B2 ** ADAM_STEP)
        d_ref[...] = -ADAM_LR * (m_hat / (jnp.sqrt(v_hat) + ADAM_EPS) + ADAM_WD * w_ref[...])
        nm_ref[...] = nm
        nv_ref[...] = nv

    tr = R // 4 if R % 32 == 0 else R
    spec = pl.BlockSpec((None, tr, C), lambda l, r: (l, r, 0))
    return _pallas(body, name=name, grid=(L, R // tr), in_specs=[spec] * 4, out_specs=[spec] * 4,
                   out_shape=[_sds((L, R, C), F32)] * 4,
                   compiler_params=_params(("parallel", "parallel")))(w, g, m, v)


def _coords():
    return lax.axis_index("x"), lax.axis_index("y"), lax.axis_index("c")


def _other_chips(x, y):
    out = []
    for fx, fy in ((1, 0), (0, 1), (1, 1)):
        px = (1 - x) if fx else x
        py = (1 - y) if fy else y
        out.append((px, py))
    return out


def _flip_index(s, j):
    sx, sy = s // 2, s % 2
    fx, fy = ((1, 0), (0, 1), (1, 1))[j]
    return 2 * (sx ^ fx) + (sy ^ fy)


def _for_my_chip(sme, fn):
    for s in range(N_CHIPS):
        pl.when(sme == s)(functools.partial(fn, s))


ANY = pl.BlockSpec(memory_space=pl.ANY)

GATHER_KIND = {"a_w_in": "col", "b_w_qkv": "col", "a_w_out": "row", "b_w_out": "row",
               "ffn_w_gate": "row", "ffn_w_up": "row", "ffn_w_down": "row"}
BIG = tuple(GATHER_KIND)


def _gathered_shape(kind, shape):
    L, R, C = shape
    return (L, R, N_CHIPS * C) if kind == "col" else (L, N_CHIPS, R, C)


def _shard_rows(ref, kind, s, r0, rn, C):
    if kind == "col":
        return ref.at[:, pl.ds(r0, rn), s * C:(s + 1) * C]
    return ref.at[:, s, pl.ds(r0, rn), :]


def gather_stage1(items):
    n = len(items)
    dims = [it[0].shape[1:] for it in items]

    def copies(ins, outs, sems, s, with_landed=True):
        lsem, ssem, rsem = sems
        x, y, c = _coords()
        chips = _other_chips(x, y)
        local, send, landed = [], [], []
        for t, (_, li, kind) in enumerate(items):
            R, C = dims[t]
            r0 = pl.multiple_of(c * (R // 2), 8)
            local.append(pltpu.make_async_copy(ins[t].at[pl.ds(li, 1)], _shard_rows(outs[t], kind, s, 0, R, C),
                                               lsem.at[t]))
            for j in range(3):
                pair = dict(send_sem=ssem.at[3 * t + j], recv_sem=rsem.at[3 * t + j],
                            device_id=(chips[j][0], chips[j][1], c), device_id_type=MESH)
                send.append(pltpu.make_async_remote_copy(
                    src_ref=ins[t].at[pl.ds(li, 1), pl.ds(r0, R // 2), :],
                    dst_ref=_shard_rows(outs[t], kind, s, r0, R // 2, C), **pair))
                if with_landed:
                    got = _shard_rows(outs[t], kind, _flip_index(s, j), r0, R // 2, C)
                    landed.append(pltpu.make_async_remote_copy(src_ref=got, dst_ref=got, **pair))
        return local, send, landed

    def start(ins, outs, sems):
        def run(s):
            local, send, _ = copies(ins, outs, sems, s, with_landed=False)
            for cp in local + send:
                cp.start()
        x, y, _ = _coords()
        _for_my_chip(2 * x + y, run)

    def wait(ins, outs, sems):
        def run(s):
            local, send, landed = copies(ins, outs, sems, s)
            for cp in landed:
                cp.wait_recv()
            for cp in send:
                cp.wait_send()
            for cp in local:
                cp.wait()
        x, y, _ = _coords()
        _for_my_chip(2 * x + y, run)

    out_shapes = [_sds(_gathered_shape(kind, (1,) + tuple(dims[t])), BF16) for t, (_, _, kind) in enumerate(items)]
    sems = [pltpu.SemaphoreType.DMA((n,)), pltpu.SemaphoreType.DMA((3 * n,)), pltpu.SemaphoreType.DMA((3 * n,))]
    return Comm([it[0] for it in items], out_shapes, sems, start, wait)


def gather_stage2(items, gathered):
    n = len(items)
    dims = [it[0].shape[1:] for it in items]

    def copies(outs, sems, s, with_landed=True):
        ssem, rsem = sems
        x, y, c = _coords()
        send, landed = [], []
        for t, (_, _, kind) in enumerate(items):
            R, C = dims[t]
            for j in range(3):
                pair = dict(send_sem=ssem.at[3 * t + j], recv_sem=rsem.at[3 * t + j],
                            device_id=(x, y, 1 - c), device_id_type=MESH)
                mine = _shard_rows(outs[t], kind, _flip_index(s, j), pl.multiple_of(c * (R // 2), 8), R // 2, C)
                send.append(pltpu.make_async_remote_copy(src_ref=mine, dst_ref=mine, **pair))
                if with_landed:
                    other = _shard_rows(outs[t], kind, _flip_index(s, j), pl.multiple_of((1 - c) * (R // 2), 8),
                                        R // 2, C)
                    landed.append(pltpu.make_async_remote_copy(src_ref=other, dst_ref=other, **pair))
        return send, landed

    def start(ins, outs, sems):
        def run(s):
            for cp in copies(outs, sems, s, with_landed=False)[0]:
                cp.start()
        x, y, _ = _coords()
        _for_my_chip(2 * x + y, run)

    def wait(ins, outs, sems):
        def run(s):
            send, landed = copies(outs, sems, s)
            for cp in landed:
                cp.wait_recv()
            for cp in send:
                cp.wait_send()
        x, y, _ = _coords()
        _for_my_chip(2 * x + y, run)

    out_shapes = [_sds(g.shape, BF16) for g in gathered]
    sems = [pltpu.SemaphoreType.DMA((3 * n,)), pltpu.SemaphoreType.DMA((3 * n,))]
    return Comm(gathered, out_shapes, sems, start, wait, aliases={t: t for t in range(n)})


def _half_shape(kind, R, C):
    return (R // 2, N_CHIPS * C) if kind == "col" else (N_CHIPS, R // 2, C)


def exchange_halves(grads, metas):
    n = len(grads)

    def copies(ins, outs, sems):
        ssem, rsem = sems
        x, y, c = _coords()
        out = []
        for t, (kind, R, C) in enumerate(metas):
            r0 = pl.multiple_of((1 - c) * (R // 2), 8)
            src = ins[t].at[pl.ds(r0, R // 2), :] if kind == "col" else ins[t].at[:, pl.ds(r0, R // 2), :]
            out.append(pltpu.make_async_remote_copy(
                src_ref=src, dst_ref=outs[t], send_sem=ssem.at[t], recv_sem=rsem.at[t],
                device_id=(x, y, 1 - c), device_id_type=MESH))
        return out

    def start(ins, outs, sems):
        for cp in copies(ins, outs, sems):
            cp.start()

    def wait(ins, outs, sems):
        for cp in copies(ins, outs, sems):
            cp.wait()

    return Comm(grads, [_sds(_half_shape(*m), F32) for m in metas], [pltpu.SemaphoreType.DMA((n,))] * 2, start, wait)


def pair_sum(me, g, sib, meta, name):
    kind, R, C = meta
    h = R // 2

    def body(me_ref, g_ref, sib_ref, p16_ref, own_ref):
        s = pl.program_id(0)
        v = g_ref[...] + sib_ref[...]
        p16_ref[...] = v.astype(BF16)

        @pl.when(s == me_ref[1])
        def _():
            own_ref[...] = v

    if kind == "col":
        gspec = pl.BlockSpec((h, C), lambda s, me_ref: (me_ref[0], s))
        sspec = pl.BlockSpec((h, C), lambda s, me_ref: (0, s))
    else:
        gspec = pl.BlockSpec((None, h, C), lambda s, me_ref: (s, me_ref[0], 0))
        sspec = pl.BlockSpec((None, h, C), lambda s, me_ref: (s, 0, 0))
    grid_spec = pltpu.PrefetchScalarGridSpec(
        num_scalar_prefetch=1, grid=(N_CHIPS,), in_specs=[gspec, sspec],
        out_specs=[sspec, pl.BlockSpec((h, C), lambda s, me_ref: (0, 0))])
    return _pallas(body, name=name, grid_spec=grid_spec,
                   out_shape=[_sds(_half_shape(*meta), BF16), _sds((h, C), F32)],
                   compiler_params=_params(("arbitrary",), VMEM_BIG))(me, g, sib)


def scatter_partials(p16s, metas):
    n = len(p16s)

    def copies(ins, outs, sems, s):
        ssem, rsem = sems
        x, y, c = _coords()
        chips = _other_chips(x, y)
        out = []
        for t, (kind, R, C) in enumerate(metas):
            for j in range(3):
                sj = _flip_index(s, j)
                src = ins[t].at[:, sj * C:(sj + 1) * C] if kind == "col" else ins[t].at[sj]
                out.append(pltpu.make_async_remote_copy(
                    src_ref=src, dst_ref=outs[t].at[j], send_sem=ssem.at[3 * t + j], recv_sem=rsem.at[3 * t + j],
                    device_id=(chips[j][0], chips[j][1], c), device_id_type=MESH))
        return out

    def start(ins, outs, sems):
        def run(s):
            for cp in copies(ins, outs, sems, s):
                cp.start()
        x, y, _ = _coords()
        _for_my_chip(2 * x + y, run)

    def wait(ins, outs, sems):
        def run(s):
            for cp in copies(ins, outs, sems, s):
                cp.wait()
        x, y, _ = _coords()
        _for_my_chip(2 * x + y, run)

    return Comm(p16s, [_sds((3, R // 2, C), BF16) for (_, R, C) in metas],
                [pltpu.SemaphoreType.DMA((3 * n,))] * 2, start, wait)


def final_sum(me, own, q, buf, shape, l, meta, name):
    _, R, C = meta
    h = R // 2

    def body(me_ref, own_ref, q_ref, *rest):
        rest[-1][...] = ((own_ref[...] + q_ref[0].astype(F32)) + q_ref[1].astype(F32)) + q_ref[2].astype(F32)

    grid_spec = pltpu.PrefetchScalarGridSpec(
        num_scalar_prefetch=1, grid=(1,),
        in_specs=[pl.BlockSpec((h, C), lambda i, me_ref: (0, 0)),
                  pl.BlockSpec((3, h, C), lambda i, me_ref: (0, 0, 0))] + ([] if buf is None else [ANY]),
        out_specs=pl.BlockSpec((None, h, C), lambda i, me_ref: (l, me_ref[0], 0)))
    alias = {} if buf is None else {"input_output_aliases": {3: 0}}
    args = (me, own, q) if buf is None else (me, own, q, buf)
    return _pallas(body, name=name, grid_spec=grid_spec, out_shape=_sds(shape, F32),
                   compiler_params=_params(("arbitrary",), VMEM_BIG), **alias)(*args)


def share_final(bufs):
    n = len(bufs)

    def body(*refs):
        ins, outs = refs[:n], refs[n:2 * n]
        ssem, rsem = refs[2 * n:]
        del ins
        x, y, c = _coords()
        copies = []
        for t in range(n):
            R = bufs[t].shape[1]
            r0 = pl.multiple_of(c * (R // 2), 8)
            blk = outs[t].at[:, pl.ds(r0, R // 2), :]
            copies.append(pltpu.make_async_remote_copy(
                src_ref=blk, dst_ref=blk, send_sem=ssem.at[t], recv_sem=rsem.at[t],
                device_id=(x, y, 1 - c), device_id_type=MESH))
        for cp in copies:
            cp.start()
        for t in range(n):
            R = bufs[t].shape[1]
            r1 = pl.multiple_of((1 - c) * (R // 2), 8)
            other = outs[t].at[:, pl.ds(r1, R // 2), :]
            pltpu.make_async_remote_copy(
                src_ref=other, dst_ref=other, send_sem=ssem.at[t], recv_sem=rsem.at[t],
                device_id=(x, y, 1 - c), device_id_type=MESH).wait_recv()
        for cp in copies:
            cp.wait_send()

    out_shape = [_sds(b.shape, F32) for b in bufs]
    return _pallas(body, name="share_final", in_specs=[ANY] * n, out_specs=[ANY] * n, out_shape=out_shape,
                   input_output_aliases={t: t for t in range(n)},
                   scratch_shapes=[pltpu.SemaphoreType.DMA((n,))] * 2,
                   compiler_params=pltpu.CompilerParams(has_side_effects=True))(*bufs)


def allreduce_small(part):
    rows = part.shape[0]
    h = rows // 2

    def body(p_ref, o_ref, sib_buf, pair_buf, chip_buf, ssem, rsem):
        x, y, c = _coords()
        sibling = dict(device_id=(x, y, 1 - c), device_id_type=MESH)
        mine = pl.ds(pl.multiple_of(c * h, 8), h)
        theirs = pl.ds(pl.multiple_of((1 - c) * h, 8), h)

        swap = pltpu.make_async_remote_copy(src_ref=p_ref.at[theirs], dst_ref=sib_buf, send_sem=ssem.at[0],
                                            recv_sem=rsem.at[0], **sibling)
        swap.start()
        swap.wait()
        pair_buf[...] = p_ref[mine, :] + sib_buf[...]

        chips = _other_chips(x, y)
        sends = [pltpu.make_async_remote_copy(src_ref=pair_buf, dst_ref=chip_buf.at[j], send_sem=ssem.at[1 + j],
                                              recv_sem=rsem.at[1 + j], device_id=(chips[j][0], chips[j][1], c),
                                              device_id_type=MESH) for j in range(3)]
        for cp in sends:
            cp.start()
        for cp in sends:
            cp.wait()

        def total(s):
            terms = {s: pair_buf[...]}
            for j in range(3):
                terms[_flip_index(s, j)] = chip_buf[j]
            o_ref[mine, :] = ((terms[0] + terms[1]) + terms[2]) + terms[3]

        _for_my_chip(2 * x + y, total)

        back = pltpu.make_async_remote_copy(src_ref=o_ref.at[mine], dst_ref=o_ref.at[mine], send_sem=ssem.at[4],
                                            recv_sem=rsem.at[4], **sibling)
        back.start()
        pltpu.make_async_remote_copy(src_ref=o_ref.at[theirs], dst_ref=o_ref.at[theirs], send_sem=ssem.at[4],
                                     recv_sem=rsem.at[4], **sibling).wait_recv()
        back.wait_send()

    return _pallas(body, name="allreduce_small",
                   in_specs=[pl.BlockSpec(memory_space=pltpu.VMEM)], out_specs=pl.BlockSpec(memory_space=pltpu.VMEM),
                   out_shape=_sds((rows, 128), F32),
                   scratch_shapes=[pltpu.VMEM((h, 128), F32), pltpu.VMEM((h, 128), F32), pltpu.VMEM((3, h, 128), F32),
                                   pltpu.SemaphoreType.DMA((5,)), pltpu.SemaphoreType.DMA((5,))],
                   compiler_params=pltpu.CompilerParams(has_side_effects=True))(part)


def _rows128(a):
    flat = a.reshape(-1)
    rows = -(-flat.shape[0] // 128)
    rows8 = -(-rows // 8) * 8
    flat = jnp.pad(flat, (0, rows8 * 128 - flat.shape[0]))
    return flat.reshape(rows8, 128)


def kernel(x, norm_mix_g, norm_ffn_g, final_g, a_w_in, a_v_gain, a_w_s, a_b_s, a_w_out, b_w_qkv, b_rel_bias, b_w_out, ffn_w_gate, ffn_w_up, ffn_w_down, loss_target, m_norm_mix_g, m_norm_ffn_g, m_final_g, m_a_w_in, m_a_v_gain, m_a_w_s, m_a_b_s, m_a_w_out, m_b_w_qkv, m_b_rel_bias, m_b_w_out, m_ffn_w_gate, m_ffn_w_up, m_ffn_w_down, v_norm_mix_g, v_norm_ffn_g, v_final_g, v_a_w_in, v_a_v_gain, v_a_w_s, v_a_b_s, v_a_w_out, v_b_w_qkv, v_b_rel_bias, v_b_w_out, v_ffn_w_gate, v_ffn_w_up, v_ffn_w_down):
    T = x.shape[1]
    weights = dict(norm_mix_g=norm_mix_g, norm_ffn_g=norm_ffn_g, final_g=final_g, a_w_in=a_w_in, a_v_gain=a_v_gain,
                   a_w_s=a_w_s, a_b_s=a_b_s, a_w_out=a_w_out, b_w_qkv=b_w_qkv, b_rel_bias=b_rel_bias,
                   b_w_out=b_w_out, ffn_w_gate=ffn_w_gate, ffn_w_up=ffn_w_up, ffn_w_down=ffn_w_down)
    mom_m = dict(norm_mix_g=m_norm_mix_g, norm_ffn_g=m_norm_ffn_g, final_g=m_final_g, a_w_in=m_a_w_in,
                 a_v_gain=m_a_v_gain, a_w_s=m_a_w_s, a_b_s=m_a_b_s, a_w_out=m_a_w_out, b_w_qkv=m_b_w_qkv,
                 b_rel_bias=m_b_rel_bias, b_w_out=m_b_w_out, ffn_w_gate=m_ffn_w_gate, ffn_w_up=m_ffn_w_up,
                 ffn_w_down=m_ffn_w_down)
    mom_v = dict(norm_mix_g=v_norm_mix_g, norm_ffn_g=v_norm_ffn_g, final_g=v_final_g, a_w_in=v_a_w_in,
                 a_v_gain=v_a_v_gain, a_w_s=v_a_w_s, a_b_s=v_a_b_s, a_w_out=v_a_w_out, b_w_qkv=v_b_w_qkv,
                 b_rel_bias=v_b_rel_bias, b_w_out=v_b_w_out, ffn_w_gate=v_ffn_w_gate, ffn_w_up=v_ffn_w_up,
                 ffn_w_down=v_ffn_w_down)
    order = list(weights)
    transposed = ("ffn_w_gate", "ffn_w_up")
    for k in transposed:
        weights[k], mom_m[k], mom_v[k] = (jnp.swapaxes(a, 1, 2) for a in (weights[k], mom_m[k], mom_v[k]))

    xi, yi, ci = _coords()
    me = jnp.stack([ci, 2 * xi + yi]).astype(jnp.int32)

    shard16 = {k: cast_bf16(weights[k], "cast_" + k) for k in BIG}

    def layer_tensors(i):
        mix = ("a_w_in", "a_w_out") if i % 2 == 0 else ("b_w_qkv", "b_w_out")
        return [(k, i // 2) for k in mix] + [(k, i) for k in ("ffn_w_gate", "ffn_w_up", "ffn_w_down")]

    def gather_items(keys):
        return [(shard16[k], l, GATHER_KIND[k]) for k, l in keys]

    def grad_metas(keys):
        return [(GATHER_KIND[k],) + tuple(weights[k].shape[1:]) for k, _ in keys]

    FFN = ("ffn_w_gate", "ffn_w_up", "ffn_w_down")
    k0a = [("a_w_out", 0), ("ffn_w_gate", 0)]
    k0b = [("ffn_w_up", 0), ("ffn_w_down", 0)]
    k1a = [("b_w_qkv", 0), ("b_w_out", 0), ("ffn_w_gate", 1)]
    k1b = [("ffn_w_up", 1), ("ffn_w_down", 1)]
    k3a = [("b_w_qkv", 1), ("b_w_out", 1), ("ffn_w_gate", 3)]
    k3b = [("ffn_w_up", 3), ("ffn_w_down", 3)]
    plans = {
        "a_in_l0": [("g1", k0a)], "sgu_fwd_l0": [("g2", k0a), ("g1", k0b)], "a_out_l0": [("g2", k0b)],
        "rms_mix_l0": [("g1", [("a_w_in", 0)])],
        "ffn_up_l0": [("g1", k1a)], "ffn_down_l0": [("g2", k1a), ("g1", k1b[:1])],
        "b_qkv_l1": [("g2", k1b[:1]), ("g1", k1b[1:])],
        "attn_fwd_l1": [("g2", k1b[1:]), ("g1", layer_tensors(2))], "b_out_l1": [("g2", layer_tensors(2))],
        "ffn_up_l1": [("g1", k3a)], "ffn_down_l1": [("g2", k3a)],
        "a_in_l2": [("g1", k3b)], "sgu_fwd_l2": [("g2", k3b)],
        "ffn_bwd_dhn_l2": [("ex", layer_tensors(3))], "sgu_bwd_l2": [("sc", layer_tensors(3))],
        "ffn_bwd_dhn_l1": [("ex", layer_tensors(2))], "attn_bwd_l1": [("sc", layer_tensors(2))],
        "ffn_bwd_dh_l0": [("ex", layer_tensors(1))], "ffn_bwd_dhn_l0": [("sc", k1a)],
        "dffn_w_gate_l0": [("sc", [("ffn_w_up", 1)])], "dffn_w_up_l0": [("sc", [("ffn_w_down", 1)])],
        "a_out_bwd_l0": [("ex", [(k, 0) for k in FFN])],
        "sgu_bwd_l0": [("sc", [("ffn_w_gate", 0), ("ffn_w_up", 0)]), ("ex", [("a_w_out", 0)])],
        "dw_in_l0": [("sc", [("ffn_w_down", 0), ("a_w_out", 0)])],
    }
    part16, full16 = {}, {}
    sib, p16, own_parts, recv_parts = {}, {}, {}, {}

    def make_comm(kind, keys):
        if kind == "g1":
            return gather_stage1(gather_items(keys)), lambda outs: part16.update(zip(keys, outs))
        if kind == "g2":
            return (gather_stage2(gather_items(keys), [part16[kl] for kl in keys]),
                    lambda outs: full16.update(zip(keys, outs)))
        if kind == "ex":
            return (exchange_halves([big_grads[k][l] for k, l in keys], grad_metas(keys)),
                    lambda outs: sib.update(zip(keys, outs)))
        for kl, m_ in zip(keys, grad_metas(keys)):
            p16[kl], own_parts[kl] = pair_sum(me, big_grads[kl[0]][kl[1]], sib[kl], m_, "pair_sum_%s_l%d" % kl)
        return (scatter_partials([p16[kl] for kl in keys], grad_metas(keys)),
                lambda outs: recv_parts.update(zip(keys, outs)))

    def run(name, make):
        steps = plans.get(name)
        if not steps:
            return make(None)
        made = [make_comm(kind, keys) for kind, keys in steps]
        main, outs = make(combine([c for c, _ in made]))
        for c, done in made:
            done(outs[:len(c.out_shapes)])
            outs = outs[len(c.out_shapes):]
        return main

    def weight(k, l):
        w = full16[(k, l)]
        if k == "a_w_out":
            return w.reshape(1, GH, D)
        return w.reshape(1, D, D) if k == "b_w_out" else w


    xcur = x.reshape(T, D)
    hn = run("rms_mix_l0", lambda comm: rms_fwd(xcur, norm_mix_g[0][None], "rms_mix_l0", comm=comm))
    comm, done = make_comm("g2", [("a_w_in", 0)])
    done(run_comm(comm, "gather_first_d2d"))
    saved = []
    for i in range(DEPTH):
        j = i // 2
        tag = "_l%d" % i
        st = {"x_in": xcur, "hn": hn}
        if i % 2 == 0:
            pre = run("a_in" + tag, lambda comm: matmul(
                "a_in" + tag, NN, hn, pl.BlockSpec((1024, D), lambda i_, j_: (i_, 0)),
                weight("a_w_in", j), pl.BlockSpec((None, D, 1024), lambda i_, j_: (0, 0, j_)),
                _sds((T, 2 * GH), BF16), pl.BlockSpec((1024, 1024), lambda i_, j_: (i_, j_)),
                (T // 1024, 4), comm=comm))
            y = run("sgu_fwd" + tag, lambda comm: sgu_fwd(
                pre, a_v_gain[j][None], a_w_s[j], a_b_s[j][:, :, None], "sgu_fwd" + tag, comm=comm))
            xmid, hn2 = run("a_out" + tag, lambda comm: residual_proj(
                "a_out" + tag, *out_proj(y, weight("a_w_out", j)), xcur, norm_ffn_g[i][None], comm=comm))
            st.update(pre=pre, y=y)
        else:
            qkvp = run("b_qkv" + tag, lambda comm: proj_qkv(hn, weight("b_w_qkv", j), 0, "b_qkv" + tag, comm=comm))
            wb = jnp.transpose(bias_build(b_rel_bias[j], "bias_build" + tag), (1, 0, 2))
            o = run("attn_fwd" + tag, lambda comm: attn_fwd(qkvp, wb, "attn_fwd" + tag, comm=comm))
            xmid, hn2 = run("b_out" + tag, lambda comm: residual_proj(
                "b_out" + tag, *out_proj(o, weight("b_w_out", j)), xcur, norm_ffn_g[i][None], comm=comm))
            st.update(qkvp=qkvp, wb=wb, o=o)
        g, u, h = run("ffn_up" + tag, lambda comm: ffn_up(
            hn2, weight("ffn_w_gate", i), weight("ffn_w_up", i), 0, "ffn_up" + tag, comm=comm))
        next_g = norm_mix_g[i + 1][None] if i + 1 < DEPTH else None
        down = run("ffn_down" + tag, lambda comm: residual_proj(
            "ffn_down" + tag, *ffn_down(h, weight("ffn_w_down", i)), xmid, next_g, comm=comm))
        xcur, hn = down if next_g is not None else (down, None)
        st.update(x_mid=xmid, hn2=hn2, g=g, u=u, h=h)
        saved.append(st)

    loss_part, dx, dxb, d_final = final_loss(xcur, final_g[None], loss_target.reshape(T, D), "final_loss")

    tk = min(2048, T)
    big_grads = {k: [None] * weights[k].shape[0] for k in BIG}
    small = {"norm_mix_g": [None] * DEPTH, "norm_ffn_g": [None] * DEPTH, "a_v_gain": [None] * 2,
             "a_w_s": [None] * 2, "a_b_s": [None] * 2, "b_rel_bias": [None] * 2}
    tok = lambda width: pl.BlockSpec((tk, width), lambda j_, k_: (k_, 0))
    part = lambda: pl.BlockSpec((None, tk, FS), lambda j_, k_: (j_, k_, 0))
    for i in reversed(range(DEPTH)):
        j = i // 2
        tag = "_l%d" % i
        st = saved[i]
        dg, du = run("ffn_bwd_dh" + tag, lambda comm: ffn_bwd_dh(
            dxb, weight("ffn_w_down", i), st["g"], st["u"], 0, "ffn_bwd_dh" + tag, comm=comm))
        big_grads["ffn_w_down"][i] = wgrad(
            "dw_down" + tag, st["h"], part(), dxb, tok(D), _sds((N_CHIPS, FS, D), F32),
            pl.BlockSpec((None, FS, D), lambda j_, k_: (j_, 0, 0)), N_CHIPS, T, tk)
        dx_mid, dxb_mid, dgn = run("ffn_bwd_dhn" + tag, lambda comm: dgrad_rms(
            "ffn_bwd_dhn" + tag, *ffn_dgrad(dg, du, weight("ffn_w_gate", i), weight("ffn_w_up", i)),
            st["x_mid"], norm_ffn_g[i][None], dx, comm=comm))
        for nm, dz in (("ffn_w_gate", dg), ("ffn_w_up", du)):
            big_grads[nm][i] = run("d" + nm + tag, lambda comm: wgrad(
                "d" + nm + tag, dz, part(), st["hn2"], tok(D), _sds((N_CHIPS, FS, D), F32),
                pl.BlockSpec((None, FS, D), lambda j_, k_: (j_, 0, 0)), N_CHIPS, T, tk, comm=comm))
        dx, dxb = dx_mid, dxb_mid
        small["norm_ffn_g"][i] = dgn
        if i % 2 == 0:
            dy = run("a_out_bwd" + tag, lambda comm: matmul(
                "a_out_bwd" + tag, NT, dxb, pl.BlockSpec((1024, D), lambda i_, j_: (i_, 0)),
                weight("a_w_out", j), pl.BlockSpec((None, 1024, D), lambda i_, j_: (0, j_, 0)),
                _sds((T, GH), BF16), pl.BlockSpec((1024, 1024), lambda i_, j_: (i_, j_)), (T // 1024, 2), comm=comm))
            big_grads["a_w_out"][j] = wgrad(
                "dw_aout" + tag, st["y"], pl.BlockSpec((tk, 1024), lambda j_, k_: (k_, j_)), dxb, tok(D),
                _sds((GH, D), F32), pl.BlockSpec((1024, D), lambda j_, k_: (j_, 0)), 2, T, tk
            ).reshape(N_CHIPS, GH // N_CHIPS, D)
            dpre, d_ws, d_bs, d_gain = run("sgu_bwd" + tag, lambda comm: sgu_bwd(
                st["pre"], dy, a_v_gain[j][None], a_w_s[j], a_b_s[j][:, :, None], "sgu_bwd" + tag,
                tm=2 * SGU_BLOCK, comm=comm))
            small["a_w_s"][j], small["a_b_s"][j], small["a_v_gain"][j] = d_ws, d_bs, d_gain
            dx_in, dxb_in, dgn = run("a_in_bwd" + tag, lambda comm: dgrad_rms(
                "a_in_bwd" + tag, *in_dgrad(dpre, weight("a_w_in", j)),
                st["x_in"], norm_mix_g[i][None], dx, comm=comm))
            big_grads["a_w_in"][j] = run("dw_in" + tag, lambda comm: wgrad(
                "dw_in" + tag, st["hn"], tok(D), dpre, pl.BlockSpec((tk, 1024), lambda j_, k_: (k_, j_)),
                _sds((D, 2 * GH), F32), pl.BlockSpec((D, 1024), lambda j_, k_: (0, j_)), 4, T, tk, comm=comm))
        else:
            do = matmul("b_out_bwd" + tag, NT, dxb, pl.BlockSpec((1024, D), lambda i_, j_: (i_, 0)),
                        weight("b_w_out", j), pl.BlockSpec((None, D, D), lambda i_, j_: (0, 0, 0)),
                        _sds((T, D), BF16), pl.BlockSpec((1024, D), lambda i_, j_: (i_, 0)), (T // 1024, 1))
            big_grads["b_w_out"][j] = wgrad(
                "dw_bout" + tag, st["o"], tok(D), dxb, tok(D),
                _sds((D, D), F32), pl.BlockSpec((D, D), lambda j_, k_: (0, 0)), 1, T, tk
            ).reshape(N_CHIPS, D // N_CHIPS, D)
            dqkvp, dwb = run("attn_bwd" + tag, lambda comm: attn_bwd(
                st["qkvp"], st["o"], do, st["wb"], "attn_bwd" + tag, comm=comm))
            small["b_rel_bias"][j] = bias_grad(
                jnp.pad(jnp.transpose(dwb, (1, 0, 2)), ((0, 0), (0, 0), (0, DIAG - KW))), "bias_grad" + tag)
            dx_in, dxb_in, dgn = dgrad_rms(
                "b_qkv_bwd" + tag, *qkv_dgrad(dqkvp, weight("b_w_qkv", j)),
                st["x_in"], norm_mix_g[i][None], dx)
            big_grads["b_w_qkv"][j] = wgrad(
                "dw_qkv" + tag, st["hn"], tok(D), dqkvp,
                pl.BlockSpec((None, tk, D), lambda j_, k_: (j_, k_ + FRONT // tk, 0)),
                _sds((D, 3 * D), F32), pl.BlockSpec((D, D), lambda j_, k_: (0, j_)), 3, T, tk)
        dx, dxb = dx_in, dxb_in
        small["norm_mix_g"][i] = dgn

    small_grads = {
        "norm_mix_g": jnp.concatenate(small["norm_mix_g"], axis=0),
        "norm_ffn_g": jnp.concatenate(small["norm_ffn_g"], axis=0),
        "final_g": d_final.reshape(D),
        "a_v_gain": jnp.concatenate(small["a_v_gain"], axis=0),
        "a_w_s": jnp.stack(small["a_w_s"]),
        "a_b_s": jnp.stack(small["a_b_s"]).reshape(2, SGU_G, SGU_BLOCK),
        "b_rel_bias": jnp.stack(small["b_rel_bias"]),
    }
    small_names = list(small_grads)
    packed = [_rows128(small_grads[k]) for k in small_names] + [_rows128(loss_part[:, :1])]
    offs = [0]
    for p in packed:
        offs.append(offs[-1] + p.shape[0])
    reduced = allreduce_small(jnp.concatenate(packed, axis=0))
    grads = {}
    for t, k in enumerate(small_names):
        nelem = small_grads[k].size
        grads[k] = reduced[offs[t]:offs[t + 1]].reshape(-1)[:nelem].reshape(weights[k].shape)
    loss = reduced[offs[len(small_names)], 0]

    last = [("a_w_in", 0)]
    for kind, name in (("ex", "exchange_last"), ("sc", "scatter_last")):
        comm, done = make_comm(kind, last)
        done(run_comm(comm, name))
    bufs = {k: None for k in BIG}
    for i in range(DEPTH):
        for kl, m_ in zip(layer_tensors(i), grad_metas(layer_tensors(i))):
            bufs[kl[0]] = final_sum(me, own_parts[kl], recv_parts[kl], bufs[kl[0]], weights[kl[0]].shape, kl[1], m_,
                                    "final_sum_%s_l%d" % kl)
    shared = share_final([bufs[k] for k in BIG])
    for k, gfull in zip(BIG, shared):
        grads[k] = gfull

    delta, new_m, new_v = {}, {}, {}
    for k in order:
        shp = weights[k].shape
        if k in BIG:
            view = shp
        elif k == "a_w_s":
            view = (2, SGU_G * SGU_BLOCK, SGU_BLOCK)
        elif len(shp) == 1:
            view = (1, 1, shp[0])
        elif len(shp) == 2:
            view = (1,) + shp
        else:
            view = shp
        g_, d_, m_, v_ = adamw(weights[k].reshape(view), grads[k].reshape(view), mom_m[k].reshape(view),
                               mom_v[k].reshape(view), "adamw_" + k)
        grads[k], delta[k], new_m[k], new_v[k] = g_.reshape(shp), d_.reshape(shp), m_.reshape(shp), v_.reshape(shp)
    for k in transposed:
        for tree in (grads, delta, new_m, new_v):
            tree[k] = jnp.swapaxes(tree[k], 1, 2)

    return (loss, dx.reshape(1, T, D), *[grads[k] for k in order], *[delta[k] for k in order],
            *[new_m[k] for k in order], *[new_v[k] for k in order])
```

```python
import functools

import jax
import jax.numpy as jnp
from jax import lax
from jax.experimental import pallas as pl
from jax.experimental.pallas import tpu as pltpu

F32 = jnp.float32
BF16 = jnp.bfloat16
MESH = pl.DeviceIdType.MESH

D = 1024
DEPTH = 4
EPS = 1e-6
SGU_BLOCK = 128
GH = 2048
SGU_G = 8
SGU_GD = GH // SGU_G
N_HEADS = 16
HEAD_DIM = 64
CHUNK = 64
PAD = 8 * CHUNK
FRONT = 2048
QB = 128
KW = PAD + QB
N_REL = 192
REL_MIN = -(CHUNK - 1)
REL_MAX = 128
D_FF = 2816
FS = D_FF // 4
NEG = -1e30
SCALE = HEAD_DIM ** -0.5
N_CHIPS = 4

ADAM_LR = 0.001
ADAM_B1 = 0.9
ADAM_B2 = 0.999
ADAM_EPS = 1e-08
ADAM_WD = 0.01
ADAM_STEP = 10

VMEM_BIG = 56 * 1024 * 1024

NN = ((1,), (0,))
NT = ((1,), (1,))
TN = ((0,), (0,))


def _dot(a, b, dims):
    return lax.dot_general(a, b, (dims, ((), ())), preferred_element_type=F32)


class Comm:
    def __init__(self, ins, out_shapes, sems, start, wait, aliases=None):
        self.ins, self.out_shapes, self.sems = list(ins), list(out_shapes), list(sems)
        self.start, self.wait, self.aliases = start, wait, dict(aliases or {})


def _host(body, comm, kw):
    grid = tuple(kw["grid"])
    in_specs = list(kw["in_specs"])
    single = not isinstance(kw["out_specs"], (list, tuple))
    out_specs = [kw["out_specs"]] if single else list(kw["out_specs"])
    out_shape = [kw["out_shape"]] if single else list(kw["out_shape"])
    scratch = list(kw.get("scratch_shapes", ()))
    counts = (len(in_specs), len(comm.ins), len(out_specs), len(comm.out_shapes), len(scratch))

    def hosted(*refs):
        parts, p = [], 0
        for cnt in counts:
            parts.append(refs[p:p + cnt])
            p += cnt
        main_in, c_in, main_out, c_out, main_scr = parts
        sems = refs[p:]
        ids = [pl.program_id(a) for a in range(len(grid))]
        first = functools.reduce(jnp.logical_and, [i == 0 for i in ids])
        last = functools.reduce(jnp.logical_and, [i == n - 1 for i, n in zip(ids, grid)])
        pl.when(first)(lambda: comm.start(c_in, c_out, sems))
        body(*main_in, *main_out, *main_scr)
        pl.when(last)(lambda: comm.wait(c_in, c_out, sems))

    old = kw["compiler_params"]
    kw = dict(kw, in_specs=in_specs + [ANY] * len(comm.ins), out_specs=out_specs + [ANY] * len(comm.out_shapes),
              out_shape=out_shape + comm.out_shapes, scratch_shapes=scratch + comm.sems,
              compiler_params=pltpu.CompilerParams(dimension_semantics=("arbitrary",) * len(grid),
                                                   vmem_limit_bytes=old.vmem_limit_bytes, has_side_effects=True))
    if comm.aliases:
        kw["input_output_aliases"] = {counts[0] + i: counts[2] + o for i, o in comm.aliases.items()}
    return hosted, kw


def _pallas(body, comm=None, **kw):
    if comm is not None:
        body, kw = _host(body, comm, kw)
    return pl.pallas_call(body, **kw)


def _split_outs(outs, comm, n_main):
    outs = list(outs) if isinstance(outs, (list, tuple)) else [outs]
    main = outs[:n_main]
    return (main[0] if n_main == 1 else main), outs[n_main:]


def run_comm(comm, name):
    nci, nco = len(comm.ins), len(comm.out_shapes)

    def body(*refs):
        c_in, c_out, sems = refs[:nci], refs[nci:nci + nco], refs[nci + nco:]
        comm.start(c_in, c_out, sems)
        comm.wait(c_in, c_out, sems)

    kw = {}
    if comm.aliases:
        kw["input_output_aliases"] = dict(comm.aliases)
    return _pallas(body, name=name, in_specs=[ANY] * nci, out_specs=[ANY] * nco, out_shape=comm.out_shapes,
                   scratch_shapes=comm.sems, compiler_params=pltpu.CompilerParams(has_side_effects=True),
                   **kw)(*comm.ins)


def combine(comms):
    if len(comms) == 1:
        return comms[0]
    spans, ni, no, ns = [], 0, 0, 0
    for c in comms:
        spans.append((slice(ni, ni + len(c.ins)), slice(no, no + len(c.out_shapes)), slice(ns, ns + len(c.sems))))
        ni, no, ns = ni + len(c.ins), no + len(c.out_shapes), ns + len(c.sems)

    def start(ins, outs, sems):
        for c, (si, so, ss) in zip(comms, spans):
            c.start(ins[si], outs[so], sems[ss])

    def wait(ins, outs, sems):
        for c, (si, so, ss) in zip(comms, spans):
            c.wait(ins[si], outs[so], sems[ss])

    aliases = {}
    for c, (si, so, _) in zip(comms, spans):
        aliases.update({si.start + i: so.start + o for i, o in c.aliases.items()})
    return Comm([a for c in comms for a in c.ins], [o for c in comms for o in c.out_shapes],
                [s for c in comms for s in c.sems], start, wait, aliases)


def _call(body, comm, n_main, args, **kw):
    if comm is None:
        return _pallas(body, **kw)(*args)
    return _split_outs(_pallas(body, comm=comm, **kw)(*args, *comm.ins), comm, n_main)


def _params(sem=None, vmem=None):
    return pltpu.CompilerParams(dimension_semantics=sem, vmem_limit_bytes=vmem)


def _sds(shape, dtype):
    return jax.ShapeDtypeStruct(tuple(shape), dtype)


_GELU_C = 0.7978845608028654


_GELU_A = _GELU_C * 0.044715


def _gelu(x):
    t = jnp.tanh(x * (_GELU_C + _GELU_A * (x * x)))
    h = 0.5 * x
    return h + h * t


def _gelu_and_grad(x):
    x2 = x * x
    t = jnp.tanh(x * (_GELU_C + _GELU_A * x2))
    h = 0.5 * x
    val = h + h * t
    grad = (0.5 + 0.5 * t) + (h * (1.0 - t * t)) * (_GELU_C + (3.0 * _GELU_A) * x2)
    return val, grad


def _sigmoid(x):
    return 0.5 * (jnp.tanh(0.5 * x) + 1.0)


def cast_bf16(w, name):
    L, R, C = w.shape

    def body(w_ref, o_ref):
        o_ref[...] = w_ref[...].astype(BF16)

    spec = pl.BlockSpec((None, R, C), lambda l: (l, 0, 0))
    return _pallas(body, name=name, grid=(L,), in_specs=[spec], out_specs=spec,
                   out_shape=_sds((L, R, C), BF16), compiler_params=_params(("parallel",)))(w)


def rms_fwd(x, g, name, tm=512, comm=None):
    T = x.shape[0]

    def body(x_ref, g_ref, o_ref):
        o_ref[...] = _rms_rows(x_ref[...], g_ref[...])

    row = pl.BlockSpec((tm, D), lambda i: (i, 0))
    return _call(body, comm, 1, (x, g), name=name, grid=(T // tm,),
                 in_specs=[row, pl.BlockSpec((1, D), lambda i: (0, 0))], out_specs=row,
                 out_shape=_sds((T, D), BF16), compiler_params=_params(("parallel",)))


def dgrad_rms(name, compute, args, specs, x, g, dres, tm=512, comm=None):
    T = x.shape[0]
    n = T // tm
    k = len(args)

    def body(*refs):
        x_ref, g_ref, dres_ref, dx_ref, dxb_ref, dg_ref, acc_ref = refs[k:]
        i = pl.program_id(0)
        xf = x_ref[...]
        r = lax.rsqrt(jnp.mean(xf * xf, axis=-1, keepdims=True) + EPS)
        xhat = xf * r
        dhf = compute(*refs[:k])
        part = (dhf * xhat).reshape(tm // 8, 8, D).sum(axis=0)

        @pl.when(i == 0)
        def _():
            acc_ref[...] = part

        @pl.when(i > 0)
        def _():
            acc_ref[...] += part

        dxhat = dhf * g_ref[...]
        dx = dres_ref[...] + r * (dxhat - xhat * jnp.mean(dxhat * xhat, axis=-1, keepdims=True))
        dx_ref[...] = dx
        dxb_ref[...] = dx.astype(BF16)

        @pl.when(i == n - 1)
        def _():
            dg_ref[...] = jnp.sum(acc_ref[...], axis=0, keepdims=True)

    row = pl.BlockSpec((tm, D), lambda i: (i, 0))
    vec = pl.BlockSpec((1, D), lambda i: (0, 0))
    return _call(body, comm, 3, (*args, x, g, dres), name=name, grid=(n,),
                 in_specs=list(specs) + [row, vec, row], out_specs=[row, row, vec],
                 out_shape=[_sds((T, D), F32), _sds((T, D), BF16), _sds((1, D), F32)],
                 scratch_shapes=[pltpu.VMEM((8, D), F32)],
                 compiler_params=_params(("arbitrary",), VMEM_BIG))


def final_loss(x, g, tgt, name, tm=256):
    T = x.shape[0]
    n = T // tm

    def body(x_ref, g_ref, t_ref, loss_ref, dx_ref, dxb_ref, dg_ref, acc_ref, lacc_ref):
        i = pl.program_id(0)
        xf = x_ref[...]
        r = lax.rsqrt(jnp.mean(xf * xf, axis=-1, keepdims=True) + EPS)
        xhat = xf * r
        gg = g_ref[...]
        e = xhat * gg - t_ref[...]
        dy = e * (1.0 / D)
        part = (dy * xhat).reshape(tm // 8, 8, D).sum(axis=0)
        lpart = (e * e).reshape(tm // 8, 8, D).sum(axis=0)

        @pl.when(i == 0)
        def _():
            acc_ref[...] = part
            lacc_ref[...] = lpart

        @pl.when(i > 0)
        def _():
            acc_ref[...] += part
            lacc_ref[...] += lpart

        dxhat = dy * gg
        dx = r * (dxhat - xhat * jnp.mean(dxhat * xhat, axis=-1, keepdims=True))
        dx_ref[...] = dx
        dxb_ref[...] = dx.astype(BF16)

        @pl.when(i == n - 1)
        def _():
            dg_ref[...] = jnp.sum(acc_ref[...], axis=0, keepdims=True)
            total = jnp.sum(jnp.sum(lacc_ref[...], axis=0, keepdims=True), axis=1, keepdims=True)
            loss_ref[...] = jnp.broadcast_to(total * (0.5 / D), (1, 128))

    row = pl.BlockSpec((tm, D), lambda i: (i, 0))
    vec = pl.BlockSpec((1, D), lambda i: (0, 0))
    return _pallas(body, name=name, grid=(n,), in_specs=[row, vec, row],
                   out_specs=[pl.BlockSpec((1, 128), lambda i: (0, 0)), row, row, vec],
                   out_shape=[_sds((1, 128), F32), _sds((T, D), F32), _sds((T, D), BF16), _sds((1, D), F32)],
                   scratch_shapes=[pltpu.VMEM((8, D), F32), pltpu.VMEM((8, D), F32)],
                   compiler_params=_params(("arbitrary",)))(x, g, tgt)


def matmul(name, dims, a, a_spec, b, b_spec, out_shape, out_spec, grid, *, acc=False, res=None, res_spec=None,
           comm=None):
    has_res = res is not None

    def body(*refs):
        a_ref, b_ref = refs[0], refs[1]
        r_ref = refs[2] if has_res else None
        o_ref = refs[-1]
        d = _dot(a_ref[...], b_ref[...], dims)
        if not acc:
            if has_res:
                d = d + r_ref[...]
            o_ref[...] = d.astype(o_ref.dtype)
        else:
            k = pl.program_id(len(grid) - 1)

            @pl.when(k == 0)
            def _():
                o_ref[...] = (d + r_ref[...]) if has_res else d

            @pl.when(k > 0)
            def _():
                o_ref[...] += d

    sem = ("parallel",) * (len(grid) - 1) + (("arbitrary",) if acc else ("parallel",))
    ins = [a, b] + ([res] if has_res else [])
    specs = [a_spec, b_spec] + ([res_spec] if has_res else [])
    return _call(body, comm, 1, ins, name=name, grid=grid, in_specs=specs, out_specs=out_spec, out_shape=out_shape,
                 compiler_params=_params(sem, VMEM_BIG))


def wgrad(name, a, a_spec, b, b_spec, out_shape, out_spec, J, T, tk, comm=None):
    return matmul(name, TN, a, a_spec, b, b_spec, out_shape, out_spec, (J, T // tk), acc=True, comm=comm)


def _sgu_mask():
    p = lax.broadcasted_iota(jnp.int32, (SGU_BLOCK, SGU_BLOCK), 0)
    q = lax.broadcasted_iota(jnp.int32, (SGU_BLOCK, SGU_BLOCK), 1)
    return lax.shift_right_logical(q, 6) <= lax.shift_right_logical(p, 6)


def sgu_fwd(pre, gain, w_s, b_s, name, comm=None):
    T = pre.shape[0]

    def body(pre_ref, gain_ref, ws_ref, bs_ref, y_ref):
        mask = _sgu_mask()
        u = _gelu(pre_ref[:, :GH].astype(F32))
        va = _gelu(pre_ref[:, GH:].astype(F32))
        r = lax.rsqrt(jnp.mean(va * va, axis=-1, keepdims=True) + EPS)
        vn = ((va * r) * gain_ref[...]).astype(BF16)
        for g in range(SGU_G):
            sl = slice(g * SGU_GD, (g + 1) * SGU_GD)
            wm = jnp.where(mask, ws_ref[g], 0.0).astype(BF16)
            vm = _dot(wm, vn[:, sl], NN) + bs_ref[g]
            y_ref[:, sl] = (u[:, sl] * vm).astype(BF16)

    return _call(
        body, comm, 1, (pre, gain, w_s, b_s), name=name, grid=(T // SGU_BLOCK,),
        in_specs=[pl.BlockSpec((SGU_BLOCK, 2 * GH), lambda i: (i, 0)),
                  pl.BlockSpec((1, GH), lambda i: (0, 0)),
                  pl.BlockSpec((SGU_G, SGU_BLOCK, SGU_BLOCK), lambda i: (0, 0, 0)),
                  pl.BlockSpec((SGU_G, SGU_BLOCK, 1), lambda i: (0, 0, 0))],
        out_specs=pl.BlockSpec((SGU_BLOCK, GH), lambda i: (i, 0)),
        out_shape=_sds((T, GH), BF16), compiler_params=_params(("parallel",)))


def sgu_bwd(pre, dy, gain, w_s, b_s, name, tm=SGU_BLOCK, comm=None):
    T = pre.shape[0]
    n = T // tm

    def body(pre_ref, dy_ref, gain_ref, ws_ref, bs_ref, dpre_ref, dws_ref, dbs_ref, dgain_ref, gacc_ref):
        i = pl.program_id(0)

        @pl.when(i == 0)
        def _():
            dws_ref[...] = jnp.zeros_like(dws_ref)
            dbs_ref[...] = jnp.zeros_like(dbs_ref)
            gacc_ref[...] = jnp.zeros_like(gacc_ref)

        mask = _sgu_mask()
        gain_v = gain_ref[...]
        for sb in range(tm // SGU_BLOCK):
            rows = slice(sb * SGU_BLOCK, (sb + 1) * SGU_BLOCK)
            u, du_dpre = _gelu_and_grad(pre_ref[rows, :GH].astype(F32))
            va, dva_dpre = _gelu_and_grad(pre_ref[rows, GH:].astype(F32))
            r = lax.rsqrt(jnp.mean(va * va, axis=-1, keepdims=True) + EPS)
            vhat = va * r
            vn = (vhat * gain_v).astype(BF16)
            dyf = dy_ref[rows, :].astype(F32)
            dvn_parts = []
            for grp in range(SGU_G):
                sl = slice(grp * SGU_GD, (grp + 1) * SGU_GD)
                wm = jnp.where(mask, ws_ref[grp], 0.0).astype(BF16)
                vm = _dot(wm, vn[:, sl], NN) + bs_ref[grp]
                dpre_ref[rows, sl] = ((dyf[:, sl] * vm) * du_dpre[:, sl]).astype(BF16)
                dvm = dyf[:, sl] * u[:, sl]
                dbs_ref[grp] += jnp.sum(dvm, axis=-1, keepdims=True)
                dvm16 = dvm.astype(BF16)
                dws_ref[grp] += jnp.where(mask, _dot(dvm16, vn[:, sl], NT), 0.0)
                dvn_parts.append(_dot(wm, dvm16, TN))
            dvn = jnp.concatenate(dvn_parts, axis=-1)
            gacc_ref[...] += (dvn * vhat).reshape(SGU_BLOCK // 8, 8, GH).sum(axis=0)
            dvhat = dvn * gain_v
            dva = r * (dvhat - vhat * jnp.mean(dvhat * vhat, axis=-1, keepdims=True))
            dpre_ref[rows, GH:] = (dva * dva_dpre).astype(BF16)

        @pl.when(i == n - 1)
        def _():
            dgain_ref[...] = jnp.sum(gacc_ref[...], axis=0, keepdims=True)

    const3 = lambda i: (0, 0, 0)
    return _call(
        body, comm, 4, (pre, dy, gain, w_s, b_s), name=name, grid=(n,),
        in_specs=[pl.BlockSpec((tm, 2 * GH), lambda i: (i, 0)),
                  pl.BlockSpec((tm, GH), lambda i: (i, 0)),
                  pl.BlockSpec((1, GH), lambda i: (0, 0)),
                  pl.BlockSpec((SGU_G, SGU_BLOCK, SGU_BLOCK), const3),
                  pl.BlockSpec((SGU_G, SGU_BLOCK, 1), const3)],
        out_specs=[pl.BlockSpec((tm, 2 * GH), lambda i: (i, 0)),
                   pl.BlockSpec((SGU_G, SGU_BLOCK, SGU_BLOCK), const3),
                   pl.BlockSpec((SGU_G, SGU_BLOCK, 1), const3),
                   pl.BlockSpec((1, GH), lambda i: (0, 0))],
        out_shape=[_sds((T, 2 * GH), BF16), _sds((SGU_G, SGU_BLOCK, SGU_BLOCK), F32),
                   _sds((SGU_G, SGU_BLOCK, 1), F32), _sds((1, GH), F32)],
        scratch_shapes=[pltpu.VMEM((8, GH), F32)],
        compiler_params=_params(("arbitrary",)))


DIAG = 768


def _diag_onehot():
    n = lax.broadcasted_iota(jnp.int32, (N_REL, DIAG), 1)
    r = lax.broadcasted_iota(jnp.int32, (N_REL, DIAG), 0)
    idx = jnp.clip(KW - 1 - n, REL_MIN, REL_MAX) - REL_MIN
    return (idx == r).astype(BF16)


def _split3(v):
    hi = v.astype(BF16)
    r1 = v - hi.astype(F32)
    mid = r1.astype(BF16)
    lo = (r1 - mid.astype(F32)).astype(BF16)
    return hi, mid, lo


def bias_build(rel_bias, name):
    def body(rb_ref, o_ref):
        oh = _diag_onehot()
        hi, mid, lo = _split3(rb_ref[...])
        u = (_dot(hi, oh, NN) + _dot(mid, oh, NN) + _dot(lo, oh, NN)) * LOG2E
        j = lax.broadcasted_iota(jnp.int32, (1, KW), 1)

        def row(i, carry):
            val = pltpu.roll(u, (i + (DIAG - QB + 1)) % DIAG, 1)[:, :KW]
            rel = lax.shift_right_logical(i, 6) - lax.shift_right_logical(j, 6) + 8
            ok = (rel >= 0) & (rel <= 8)
            o_ref[i] = jnp.where(ok, val, NEG)
            return carry

        lax.fori_loop(0, QB, row, 0)

    return _pallas(body, name=name, out_shape=_sds((QB, N_HEADS, KW), F32),
                   in_specs=[pl.BlockSpec(memory_space=pltpu.VMEM)],
                   out_specs=pl.BlockSpec(memory_space=pltpu.VMEM))(rel_bias)


def bias_grad(dwb, name):
    def body(d_ref, o_ref):
        def row(i, acc):
            return acc + pltpu.roll(d_ref[i], QB - 1 - i, 1)

        du = lax.fori_loop(0, QB, row, jnp.zeros((N_HEADS, DIAG), F32))
        oh = _diag_onehot()
        hi, mid, lo = _split3(du)
        o_ref[...] = _dot(hi, oh, NT) + _dot(mid, oh, NT) + _dot(lo, oh, NT)

    return _pallas(body, name=name, out_shape=_sds((N_HEADS, N_REL), F32),
                   in_specs=[pl.BlockSpec(memory_space=pltpu.VMEM)],
                   out_specs=pl.BlockSpec(memory_space=pltpu.VMEM))(dwb)


LOG2E = 1.4426950408889634
Q_SCALE = SCALE * LOG2E


def _attn_block(qkv_ref, blk, masked):
    r0 = pl.multiple_of(blk * QB, QB)
    qs = qkv_ref[0, pl.ds(r0 + FRONT, QB), :]
    kvalid = (lax.broadcasted_iota(jnp.int32, (1, KW), 1) >= PAD - blk * QB) if masked else None
    return r0, qs, kvalid


def _step_windows(qkv_ref, b, step):
    r0 = pl.multiple_of(b * step, QB) + (FRONT - PAD)
    out = []
    for part in (1, 2):
        a = qkv_ref[part, pl.ds(r0, PAD + step), :]
        zero = jnp.zeros_like(a)
        out.append([jnp.where(_head_mask(h), a, zero) for h in range(2)])
    return out


def _window(stacks, t):
    return jnp.concatenate([s[t * QB:t * QB + KW] for s in stacks], axis=0)


def _head_mask(h):
    lane = lax.broadcasted_iota(jnp.int32, (1, 2 * HEAD_DIM), 1)
    return (lane < HEAD_DIM) if h == 0 else (lane >= HEAD_DIM)


def _stack_heads(a):
    zero = jnp.zeros_like(a)
    return jnp.concatenate([jnp.where(_head_mask(0), a, zero), jnp.where(_head_mask(1), a, zero)], axis=0)


def _rows_by_head(a):
    return jnp.concatenate([a[:, :KW], a[:, KW:]], axis=0)


def _per_head(lo, hi):
    return jnp.where(_head_mask(0), lo, hi)


def _attn_exp(qs, kst, w_ref, kvalid):
    s = _dot(qs, kst, NT) + jnp.concatenate([w_ref[0], w_ref[1]], axis=1)
    if kvalid is not None:
        s = jnp.where(jnp.concatenate([kvalid, kvalid], axis=1), s, NEG)
    es, invs = [], []
    for h in range(2):
        sh = s[:, h * KW:(h + 1) * KW]
        eh = jnp.exp2(sh - jnp.max(sh, axis=-1, keepdims=True))
        es.append(eh)
        invs.append(1.0 / jnp.sum(eh, axis=-1, keepdims=True))
    return jnp.concatenate(es, axis=1), invs


ATTN_G = 8


def _blocks_per_step(T):
    return min(ATTN_G, T // QB)


def _masked_and_not(b, fn, step):
    n_masked = -(-PAD // step)
    pl.when(b < n_masked)(functools.partial(fn, True))
    pl.when(b >= n_masked)(functools.partial(fn, False))


def attn_fwd(qkvp, wb, name, comm=None):
    T = qkvp.shape[1] - FRONT
    G = _blocks_per_step(T)

    def body(qkv_ref, w_ref, o_ref):
        b = pl.program_id(1)

        def blocks(masked):
            keys, values = _step_windows(qkv_ref, b, G * QB)
            for t in range(G):
                _, qs, kvalid = _attn_block(qkv_ref, b * G + t, masked)
                e, inv = _attn_exp(qs, _window(keys, t), w_ref, kvalid)
                o = _dot(e.astype(BF16), _window(values, t), NN) * _per_head(*inv)
                o_ref[t * QB:(t + 1) * QB, :] = o.astype(BF16)

        _masked_and_not(b, blocks, G * QB)

    return _call(
        body, comm, 1, (qkvp, wb), name=name, grid=(N_HEADS // 2, T // (QB * G)),
        in_specs=[pl.BlockSpec((3, FRONT + T, 2 * HEAD_DIM), lambda hp, b: (0, 0, hp)),
                  pl.BlockSpec((2, QB, KW), lambda hp, b: (hp, 0, 0))],
        out_specs=pl.BlockSpec((QB * G, 2 * HEAD_DIM), lambda hp, b: (b, hp)),
        out_shape=_sds((T, D), BF16),
        compiler_params=_params(("parallel", "arbitrary"), VMEM_BIG))


def attn_bwd(qkvp, o, do, wb, name, comm=None):
    T = qkvp.shape[1] - FRONT
    G = _blocks_per_step(T)
    nb = T // (QB * G)

    def body(qkv_ref, o_ref, do_ref, w_ref, dqkv_ref, dw_ref, dk_acc, dv_acc):
        b = pl.program_id(1)

        @pl.when(b == 0)
        def _():
            dk_acc[...] = jnp.zeros_like(dk_acc)
            dv_acc[...] = jnp.zeros_like(dv_acc)
            dw_ref[...] = jnp.zeros_like(dw_ref)
            dqkv_ref[0, 0:FRONT, :] = jnp.zeros((FRONT, 2 * HEAD_DIM), BF16)

        def blocks(masked):
            dws = None
            keys, values = _step_windows(qkv_ref, b, G * QB)
            for t in range(G):
                r0, qs, kvalid = _attn_block(qkv_ref, b * G + t, masked)
                kst = _window(keys, t)
                e, inv = _attn_exp(qs, kst, w_ref, kvalid)
                do2 = do_ref[t * QB:(t + 1) * QB, :]
                dof = do2.astype(F32)
                prod = dof * o_ref[t * QB:(t + 1) * QB, :].astype(F32)
                dp = _dot(do2, _window(values, t), NT)
                parts = []
                for h in range(2):
                    delta = jnp.sum(jnp.where(_head_mask(h), prod, 0.0), axis=-1, keepdims=True)
                    half = slice(h * KW, (h + 1) * KW)
                    parts.append(e[:, half] * ((dp[:, half] - delta) * inv[h]))
                ds = jnp.concatenate(parts, axis=1)
                dws = ds if dws is None else dws + ds
                ds16 = ds.astype(BF16)
                dqkv_ref[0, pl.ds(r0 + FRONT, QB), :] = (_dot(ds16, kst, NN) * SCALE).astype(BF16)
                dk_acc[pl.ds(r0 + (FRONT - PAD), KW), :] += _dot(_rows_by_head(ds16), _stack_heads(qs), TN)
                dv_acc[pl.ds(r0 + (FRONT - PAD), KW), :] += _dot(
                    _rows_by_head(e.astype(BF16)), _stack_heads((dof * _per_head(*inv)).astype(BF16)), TN)
            dw_ref[0] += dws[:, :KW]
            dw_ref[1] += dws[:, KW:]

        _masked_and_not(b, blocks, G * QB)

        @pl.when(b == nb - 1)
        def _():
            dqkv_ref[1] = (dk_acc[...] * (1.0 / LOG2E)).astype(BF16)
            dqkv_ref[2] = dv_acc[...].astype(BF16)

    slab = pl.BlockSpec((3, FRONT + T, 2 * HEAD_DIM), lambda hp, b: (0, 0, hp))
    wspec = pl.BlockSpec((2, QB, KW), lambda hp, b: (hp, 0, 0))
    rows = pl.BlockSpec((QB * G, 2 * HEAD_DIM), lambda hp, b: (b, hp))
    return _call(
        body, comm, 2, (qkvp, o, do, wb), name=name, grid=(N_HEADS // 2, nb),
        in_specs=[slab, rows, rows, wspec],
        out_specs=[slab, wspec],
        out_shape=[_sds((3, FRONT + T, D), BF16), _sds((N_HEADS, QB, KW), F32)],
        scratch_shapes=[pltpu.VMEM((FRONT + T, 2 * HEAD_DIM), F32), pltpu.VMEM((FRONT + T, 2 * HEAD_DIM), F32)],
        compiler_params=_params(("parallel", "arbitrary"), VMEM_BIG))


def proj_qkv(hn, w, l, name, tm=512, comm=None):
    T = hn.shape[0]
    pb = FRONT // tm

    def body(a_ref, b_ref, o_ref):
        i = pl.program_id(1)

        @pl.when(i < pb)
        def _():
            o_ref[...] = jnp.zeros_like(o_ref)

        @pl.when(i >= pb)
        def _():
            scale = jnp.where(pl.program_id(0) == 0, Q_SCALE, 1.0).astype(F32)
            o_ref[...] = (_dot(a_ref[...], b_ref[...], NN) * scale).astype(BF16)

    return _call(
        body, comm, 1, (hn, w), name=name, grid=(3, pb + T // tm),
        in_specs=[pl.BlockSpec((tm, D), lambda p, i: (jnp.maximum(i - pb, 0), 0)),
                  pl.BlockSpec((None, D, D), lambda p, i: (l, 0, p))],
        out_specs=pl.BlockSpec((None, tm, D), lambda p, i: (p, i, 0)),
        out_shape=_sds((3, FRONT + T, D), BF16),
        compiler_params=_params(("parallel", "parallel"), VMEM_BIG))


def ffn_up(hn, wg, wu, l, name, tm=1024, comm=None):
    T = hn.shape[0]

    def body(a_ref, wg_ref, wu_ref, g_ref, u_ref, h_ref):
        a = a_ref[...]
        g = _dot(a, wg_ref[...], NT)
        u = _dot(a, wu_ref[...], NT)
        s = _sigmoid(g)
        silu = g * s
        g_ref[...] = (u * (s * (1.0 + g * (1.0 - s)))).astype(BF16)
        u_ref[...] = silu.astype(BF16)
        h_ref[...] = (silu * u).astype(BF16)

    wspec = pl.BlockSpec((None, None, FS, D), lambda s, i: (l, s, 0, 0))
    ospec = pl.BlockSpec((None, tm, FS), lambda s, i: (s, i, 0))
    return _call(
        body, comm, 3, (hn, wg, wu), name=name, grid=(N_CHIPS, T // tm),
        in_specs=[pl.BlockSpec((tm, D), lambda s, i: (i, 0)), wspec, wspec],
        out_specs=[ospec, ospec, ospec],
        out_shape=[_sds((N_CHIPS, T, FS), BF16)] * 3,
        compiler_params=_params(("parallel", "parallel"), VMEM_BIG))


def ffn_bwd_dh(dxb, wd, g, u, l, name, tm=2048, comm=None):
    T = dxb.shape[0]
    tm = min(tm, T)

    def body(a_ref, wd_ref, g_ref, u_ref, dg_ref, du_ref):
        dh = _dot(a_ref[...], wd_ref[...], NT)
        dg_ref[...] = (dh * g_ref[...].astype(F32)).astype(BF16)
        du_ref[...] = (dh * u_ref[...].astype(F32)).astype(BF16)

    aspec = pl.BlockSpec((None, tm, FS), lambda i, s: (s, i, 0))
    return _call(
        body, comm, 2, (dxb, wd, g, u), name=name, grid=(T // tm, N_CHIPS),
        in_specs=[pl.BlockSpec((tm, D), lambda i, s: (i, 0)),
                  pl.BlockSpec((None, None, FS, D), lambda i, s: (l, s, 0, 0)), aspec, aspec],
        out_specs=[aspec, aspec],
        out_shape=[_sds((N_CHIPS, T, FS), BF16)] * 2,
        compiler_params=_params(("parallel", "parallel"), VMEM_BIG))


def ffn_dgrad(dg, du, wg, wu, tm=512):
    def compute(dg_ref, du_ref, wg_ref, wu_ref):
        d = None
        for s in range(N_CHIPS):
            t = _dot(dg_ref[s], wg_ref[s], NN) + _dot(du_ref[s], wu_ref[s], NN)
            d = t if d is None else d + t
        return d

    aspec = pl.BlockSpec((N_CHIPS, tm, FS), lambda i: (0, i, 0))
    wspec = pl.BlockSpec((None, N_CHIPS, FS, D), lambda i: (0, 0, 0, 0), pipeline_mode=pl.Buffered(1))
    return compute, (dg, du, wg, wu), [aspec, aspec, wspec, wspec]


def qkv_dgrad(dqkvp, w, tm=512):
    def compute(a_ref, w_ref):
        d = None
        for p in range(3):
            t = _dot(a_ref[p], w_ref[:, p * D:(p + 1) * D], NT)
            d = t if d is None else d + t
        return d

    return compute, (dqkvp, w), [pl.BlockSpec((3, tm, D), lambda i: (0, i + FRONT // tm, 0)),
                                 pl.BlockSpec((None, D, 3 * D), lambda i: (0, 0, 0))]


def in_dgrad(dpre, w, tm=512):
    def compute(a_ref, w_ref):
        return _dot(a_ref[...], w_ref[...], NT)

    return compute, (dpre, w), [pl.BlockSpec((tm, 2 * GH), lambda i: (i, 0)),
                                pl.BlockSpec((None, D, 2 * GH), lambda i: (0, 0, 0))]


def _rms_rows(x, g):
    r = lax.rsqrt(jnp.mean(x * x, axis=-1, keepdims=True) + EPS)
    return ((x * r) * g).astype(BF16)


def residual_proj(name, compute, args, specs, res, norm_g, tm=512, comm=None):
    T = res.shape[0]
    k = len(args)
    with_norm = norm_g is not None

    def body(*refs):
        d = refs[k][...] + compute(*refs[:k])
        if with_norm:
            refs[k + 2][...] = d
            refs[k + 3][...] = _rms_rows(d, refs[k + 1][...])
        else:
            refs[k + 1][...] = d

    row = pl.BlockSpec((tm, D), lambda i: (i, 0))
    vec = pl.BlockSpec((1, D), lambda i: (0, 0))
    if with_norm:
        return _call(body, comm, 2, (*args, res, norm_g), name=name, grid=(T // tm,),
                     in_specs=list(specs) + [row, vec], out_specs=[row, row],
                     out_shape=[_sds((T, D), F32), _sds((T, D), BF16)],
                     compiler_params=_params(("parallel",), VMEM_BIG))
    return _call(body, comm, 1, (*args, res), name=name, grid=(T // tm,), in_specs=list(specs) + [row],
                 out_specs=row, out_shape=_sds((T, D), F32), compiler_params=_params(("parallel",), VMEM_BIG))


def ffn_down(h, wd, tm=512):
    def compute(h_ref, wd_ref):
        d = None
        for s in range(N_CHIPS):
            t = _dot(h_ref[s], wd_ref[s], NN)
            d = t if d is None else d + t
        return d

    return compute, (h, wd), [pl.BlockSpec((N_CHIPS, tm, FS), lambda i: (0, i, 0)),
                              pl.BlockSpec((None, N_CHIPS, FS, D), lambda i: (0, 0, 0, 0))]


def out_proj(a, w, tm=512):
    K = a.shape[1]

    def compute(a_ref, w_ref):
        return _dot(a_ref[...], w_ref[...], NN)

    return compute, (a, w), [pl.BlockSpec((tm, K), lambda i: (i, 0)), pl.BlockSpec((None, K, D), lambda i: (0, 0, 0))]


def adamw(w, g, m, v, name):
    L, R, C = w.shape

    def body(w_ref, g_ref, m_ref, v_ref, go_ref, d_ref, nm_ref, nv_ref):
        gf = g_ref[...]
        go_ref[...] = gf
        nm = ADAM_B1 * m_ref[...] + (1.0 - ADAM_B1) * gf
        nv = ADAM_B2 * v_ref[...] + (1.0 - ADAM_B2) * (gf * gf)
        m_hat = nm / (1.0 - ADAM_B1 ** ADAM_STEP)
        v_hat = nv / (1.0 - ADAM_B2 ** ADAM_STEP)
        d_ref[...] = -ADAM_LR * (m_hat / (jnp.sqrt(v_hat) + ADAM_EPS) + ADAM_WD * w_ref[...])
        nm_ref[...] = nm
        nv_ref[...] = nv

    tr = R // 4 if R % 32 == 0 else R
    spec = pl.BlockSpec((None, tr, C), lambda l, r: (l, r, 0))
    return _pallas(body, name=name, grid=(L, R // tr), in_specs=[spec] * 4, out_specs=[spec] * 4,
                   out_shape=[_sds((L, R, C), F32)] * 4,
                   compiler_params=_params(("parallel", "parallel")))(w, g, m, v)


def _coords():
    return lax.axis_index("x"), lax.axis_index("y"), lax.axis_index("c")


def _other_chips(x, y):
    out = []
    for fx, fy in ((1, 0), (0, 1), (1, 1)):
        px = (1 - x) if fx else x
        py = (1 - y) if fy else y
        out.append((px, py))
    return out


def _flip_index(s, j):
    sx, sy = s // 2, s % 2
    fx, fy = ((1, 0), (0, 1), (1, 1))[j]
    return 2 * (sx ^ fx) + (sy ^ fy)


def _for_my_chip(sme, fn):
    for s in range(N_CHIPS):
        pl.when(sme == s)(functools.partial(fn, s))


ANY = pl.BlockSpec(memory_space=pl.ANY)

GATHER_KIND = {"a_w_in": "col", "b_w_qkv": "col", "a_w_out": "row", "b_w_out": "row",
               "ffn_w_gate": "row", "ffn_w_up": "row", "ffn_w_down": "row"}
BIG = tuple(GATHER_KIND)


def _gathered_shape(kind, shape):
    L, R, C = shape
    return (L, R, N_CHIPS * C) if kind == "col" else (L, N_CHIPS, R, C)


def _shard_rows(ref, kind, s, r0, rn, C):
    if kind == "col":
        return ref.at[:, pl.ds(r0, rn), s * C:(s + 1) * C]
    return ref.at[:, s, pl.ds(r0, rn), :]


def gather_stage1(items):
    n = len(items)
    dims = [it[0].shape[1:] for it in items]

    def copies(ins, outs, sems, s, with_landed=True):
        lsem, ssem, rsem = sems
        x, y, c = _coords()
        chips = _other_chips(x, y)
        local, send, landed = [], [], []
        for t, (_, li, kind) in enumerate(items):
            R, C = dims[t]
            r0 = pl.multiple_of(c * (R // 2), 8)
            local.append(pltpu.make_async_copy(ins[t].at[pl.ds(li, 1)], _shard_rows(outs[t], kind, s, 0, R, C),
                                               lsem.at[t]))
            for j in range(3):
                pair = dict(send_sem=ssem.at[3 * t + j], recv_sem=rsem.at[3 * t + j],
                            device_id=(chips[j][0], chips[j][1], c), device_id_type=MESH)
                send.append(pltpu.make_async_remote_copy(
                    src_ref=ins[t].at[pl.ds(li, 1), pl.ds(r0, R // 2), :],
                    dst_ref=_shard_rows(outs[t], kind, s, r0, R // 2, C), **pair))
                if with_landed:
                    got = _shard_rows(outs[t], kind, _flip_index(s, j), r0, R // 2, C)
                    landed.append(pltpu.make_async_remote_copy(src_ref=got, dst_ref=got, **pair))
        return local, send, landed

    def start(ins, outs, sems):
        def run(s):
            local, send, _ = copies(ins, outs, sems, s, with_landed=False)
            for cp in local + send:
                cp.start()
        x, y, _ = _coords()
        _for_my_chip(2 * x + y, run)

    def wait(ins, outs, sems):
        def run(s):
            local, send, landed = copies(ins, outs, sems, s)
            for cp in landed:
                cp.wait_recv()
            for cp in send:
                cp.wait_send()
            for cp in local:
                cp.wait()
        x, y, _ = _coords()
        _for_my_chip(2 * x + y, run)

    out_shapes = [_sds(_gathered_shape(kind, (1,) + tuple(dims[t])), BF16) for t, (_, _, kind) in enumerate(items)]
    sems = [pltpu.SemaphoreType.DMA((n,)), pltpu.SemaphoreType.DMA((3 * n,)), pltpu.SemaphoreType.DMA((3 * n,))]
    return Comm([it[0] for it in items], out_shapes, sems, start, wait)


def gather_stage2(items, gathered):
    n = len(items)
    dims = [it[0].shape[1:] for it in items]

    def copies(outs, sems, s, with_landed=True):
        ssem, rsem = sems
        x, y, c = _coords()
        send, landed = [], []
        for t, (_, _, kind) in enumerate(items):
            R, C = dims[t]
            for j in range(3):
                pair = dict(send_sem=ssem.at[3 * t + j], recv_sem=rsem.at[3 * t + j],
                            device_id=(x, y, 1 - c), device_id_type=MESH)
                mine = _shard_rows(outs[t], kind, _flip_index(s, j), pl.multiple_of(c * (R // 2), 8), R // 2, C)
                send.append(pltpu.make_async_remote_copy(src_ref=mine, dst_ref=mine, **pair))
                if with_landed:
                    other = _shard_rows(outs[t], kind, _flip_index(s, j), pl.multiple_of((1 - c) * (R // 2), 8),
                                        R // 2, C)
                    landed.append(pltpu.make_async_remote_copy(src_ref=other, dst_ref=other, **pair))
        return send, landed

    def start(ins, outs, sems):
        def run(s):
            for cp in copies(outs, sems, s, with_landed=False)[0]:
                cp.start()
        x, y, _ = _coords()
        _for_my_chip(2 * x + y, run)

    def wait(ins, outs, sems):
        def run(s):
            send, landed = copies(outs, sems, s)
            for cp in landed:
                cp.wait_recv()
            for cp in send:
                cp.wait_send()
        x, y, _ = _coords()
        _for_my_chip(2 * x + y, run)

    out_shapes = [_sds(g.shape, BF16) for g in gathered]
    sems = [pltpu.SemaphoreType.DMA((3 * n,)), pltpu.SemaphoreType.DMA((3 * n,))]
    return Comm(gathered, out_shapes, sems, start, wait, aliases={t: t for t in range(n)})


def _half_shape(kind, R, C):
    return (R // 2, N_CHIPS * C) if kind == "col" else (N_CHIPS, R // 2, C)


def exchange_halves(grads, metas):
    n = len(grads)

    def copies(ins, outs, sems):
        ssem, rsem = sems
        x, y, c = _coords()
        out = []
        for t, (kind, R, C) in enumerate(metas):
            r0 = pl.multiple_of((1 - c) * (R // 2), 8)
            src = ins[t].at[pl.ds(r0, R // 2), :] if kind == "col" else ins[t].at[:, pl.ds(r0, R // 2), :]
            out.append(pltpu.make_async_remote_copy(
                src_ref=src, dst_ref=outs[t], send_sem=ssem.at[t], recv_sem=rsem.at[t],
                device_id=(x, y, 1 - c), device_id_type=MESH))
        return out

    def start(ins, outs, sems):
        for cp in copies(ins, outs, sems):
            cp.start()

    def wait(ins, outs, sems):
        for cp in copies(ins, outs, sems):
            cp.wait()

    return Comm(grads, [_sds(_half_shape(*m), F32) for m in metas], [pltpu.SemaphoreType.DMA((n,))] * 2, start, wait)


def pair_sum(me, g, sib, meta, name):
    kind, R, C = meta
    h = R // 2

    def body(me_ref, g_ref, sib_ref, p16_ref, own_ref):
        s = pl.program_id(0)
        v = g_ref[...] + sib_ref[...]
        p16_ref[...] = v.astype(BF16)

        @pl.when(s == me_ref[1])
        def _():
            own_ref[...] = v

    if kind == "col":
        gspec = pl.BlockSpec((h, C), lambda s, me_ref: (me_ref[0], s))
        sspec = pl.BlockSpec((h, C), lambda s, me_ref: (0, s))
    else:
        gspec = pl.BlockSpec((None, h, C), lambda s, me_ref: (s, me_ref[0], 0))
        sspec = pl.BlockSpec((None, h, C), lambda s, me_ref: (s, 0, 0))
    grid_spec = pltpu.PrefetchScalarGridSpec(
        num_scalar_prefetch=1, grid=(N_CHIPS,), in_specs=[gspec, sspec],
        out_specs=[sspec, pl.BlockSpec((h, C), lambda s, me_ref: (0, 0))])
    return _pallas(body, name=name, grid_spec=grid_spec,
                   out_shape=[_sds(_half_shape(*meta), BF16), _sds((h, C), F32)],
                   compiler_params=_params(("arbitrary",), VMEM_BIG))(me, g, sib)


def scatter_partials(p16s, metas):
    n = len(p16s)

    def copies(ins, outs, sems, s):
        ssem, rsem = sems
        x, y, c = _coords()
        chips = _other_chips(x, y)
        out = []
        for t, (kind, R, C) in enumerate(metas):
            for j in range(3):
                sj = _flip_index(s, j)
                src = ins[t].at[:, sj * C:(sj + 1) * C] if kind == "col" else ins[t].at[sj]
                out.append(pltpu.make_async_remote_copy(
                    src_ref=src, dst_ref=outs[t].at[j], send_sem=ssem.at[3 * t + j], recv_sem=rsem.at[3 * t + j],
                    device_id=(chips[j][0], chips[j][1], c), device_id_type=MESH))
        return out

    def start(ins, outs, sems):
        def run(s):
            for cp in copies(ins, outs, sems, s):
                cp.start()
        x, y, _ = _coords()
        _for_my_chip(2 * x + y, run)

    def wait(ins, outs, sems):
        def run(s):
            for cp in copies(ins, outs, sems, s):
                cp.wait()
        x, y, _ = _coords()
        _for_my_chip(2 * x + y, run)

    return Comm(p16s, [_sds((3, R // 2, C), BF16) for (_, R, C) in metas],
                [pltpu.SemaphoreType.DMA((3 * n,))] * 2, start, wait)


def final_sum(me, own, q, buf, shape, l, meta, name):
    _, R, C = meta
    h = R // 2

    def body(me_ref, own_ref, q_ref, *rest):
        rest[-1][...] = ((own_ref[...] + q_ref[0].astype(F32)) + q_ref[1].astype(F32)) + q_ref[2].astype(F32)

    grid_spec = pltpu.PrefetchScalarGridSpec(
        num_scalar_prefetch=1, grid=(1,),
        in_specs=[pl.BlockSpec((h, C), lambda i, me_ref: (0, 0)),
                  pl.BlockSpec((3, h, C), lambda i, me_ref: (0, 0, 0))] + ([] if buf is None else [ANY]),
        out_specs=pl.BlockSpec((None, h, C), lambda i, me_ref: (l, me_ref[0], 0)))
    alias = {} if buf is None else {"input_output_aliases": {3: 0}}
    args = (me, own, q) if buf is None else (me, own, q, buf)
    return _pallas(body, name=name, grid_spec=grid_spec, out_shape=_sds(shape, F32),
                   compiler_params=_params(("arbitrary",), VMEM_BIG), **alias)(*args)


def share_final(bufs):
    n = len(bufs)

    def body(*refs):
        ins, outs = refs[:n], refs[n:2 * n]
        ssem, rsem = refs[2 * n:]
        del ins
        x, y, c = _coords()
        copies = []
        for t in range(n):
            R = bufs[t].shape[1]
            r0 = pl.multiple_of(c * (R // 2), 8)
            blk = outs[t].at[:, pl.ds(r0, R // 2), :]
            copies.append(pltpu.make_async_remote_copy(
                src_ref=blk, dst_ref=blk, send_sem=ssem.at[t], recv_sem=rsem.at[t],
                device_id=(x, y, 1 - c), device_id_type=MESH))
        for cp in copies:
            cp.start()
        for t in range(n):
            R = bufs[t].shape[1]
            r1 = pl.multiple_of((1 - c) * (R // 2), 8)
            other = outs[t].at[:, pl.ds(r1, R // 2), :]
            pltpu.make_async_remote_copy(
                src_ref=other, dst_ref=other, send_sem=ssem.at[t], recv_sem=rsem.at[t],
                device_id=(x, y, 1 - c), device_id_type=MESH).wait_recv()
        for cp in copies:
            cp.wait_send()

    out_shape = [_sds(b.shape, F32) for b in bufs]
    return _pallas(body, name="share_final", in_specs=[ANY] * n, out_specs=[ANY] * n, out_shape=out_shape,
                   input_output_aliases={t: t for t in range(n)},
                   scratch_shapes=[pltpu.SemaphoreType.DMA((n,))] * 2,
                   compiler_params=pltpu.CompilerParams(has_side_effects=True))(*bufs)


def allreduce_small(part):
    rows = part.shape[0]
    h = rows // 2

    def body(p_ref, o_ref, sib_buf, pair_buf, chip_buf, ssem, rsem):
        x, y, c = _coords()
        sibling = dict(device_id=(x, y, 1 - c), device_id_type=MESH)
        mine = pl.ds(pl.multiple_of(c * h, 8), h)
        theirs = pl.ds(pl.multiple_of((1 - c) * h, 8), h)

        swap = pltpu.make_async_remote_copy(src_ref=p_ref.at[theirs], dst_ref=sib_buf, send_sem=ssem.at[0],
                                            recv_sem=rsem.at[0], **sibling)
        swap.start()
        swap.wait()
        pair_buf[...] = p_ref[mine, :] + sib_buf[...]

        chips = _other_chips(x, y)
        sends = [pltpu.make_async_remote_copy(src_ref=pair_buf, dst_ref=chip_buf.at[j], send_sem=ssem.at[1 + j],
                                              recv_sem=rsem.at[1 + j], device_id=(chips[j][0], chips[j][1], c),
                                              device_id_type=MESH) for j in range(3)]
        for cp in sends:
            cp.start()
        for cp in sends:
            cp.wait()

        def total(s):
            terms = {s: pair_buf[...]}
            for j in range(3):
                terms[_flip_index(s, j)] = chip_buf[j]
            o_ref[mine, :] = ((terms[0] + terms[1]) + terms[2]) + terms[3]

        _for_my_chip(2 * x + y, total)

        back = pltpu.make_async_remote_copy(src_ref=o_ref.at[mine], dst_ref=o_ref.at[mine], send_sem=ssem.at[4],
                                            recv_sem=rsem.at[4], **sibling)
        back.start()
        pltpu.make_async_remote_copy(src_ref=o_ref.at[theirs], dst_ref=o_ref.at[theirs], send_sem=ssem.at[4],
                                     recv_sem=rsem.at[4], **sibling).wait_recv()
        back.wait_send()

    return _pallas(body, name="allreduce_small",
                   in_specs=[pl.BlockSpec(memory_space=pltpu.VMEM)], out_specs=pl.BlockSpec(memory_space=pltpu.VMEM),
                   out_shape=_sds((rows, 128), F32),
                   scratch_shapes=[pltpu.VMEM((h, 128), F32), pltpu.VMEM((h, 128), F32), pltpu.VMEM((3, h, 128), F32),
                                   pltpu.SemaphoreType.DMA((5,)), pltpu.SemaphoreType.DMA((5,))],
                   compiler_params=pltpu.CompilerParams(has_side_effects=True))(part)


def _rows128(a):
    flat = a.reshape(-1)
    rows = -(-flat.shape[0] // 128)
    rows8 = -(-rows // 8) * 8
    flat = jnp.pad(flat, (0, rows8 * 128 - flat.shape[0]))
    return flat.reshape(rows8, 128)


def kernel(x, norm_mix_g, norm_ffn_g, final_g, a_w_in, a_v_gain, a_w_s, a_b_s, a_w_out, b_w_qkv, b_rel_bias, b_w_out, ffn_w_gate, ffn_w_up, ffn_w_down, loss_target, m_norm_mix_g, m_norm_ffn_g, m_final_g, m_a_w_in, m_a_v_gain, m_a_w_s, m_a_b_s, m_a_w_out, m_b_w_qkv, m_b_rel_bias, m_b_w_out, m_ffn_w_gate, m_ffn_w_up, m_ffn_w_down, v_norm_mix_g, v_norm_ffn_g, v_final_g, v_a_w_in, v_a_v_gain, v_a_w_s, v_a_b_s, v_a_w_out, v_b_w_qkv, v_b_rel_bias, v_b_w_out, v_ffn_w_gate, v_ffn_w_up, v_ffn_w_down):
    T = x.shape[1]
    weights = dict(norm_mix_g=norm_mix_g, norm_ffn_g=norm_ffn_g, final_g=final_g, a_w_in=a_w_in, a_v_gain=a_v_gain,
                   a_w_s=a_w_s, a_b_s=a_b_s, a_w_out=a_w_out, b_w_qkv=b_w_qkv, b_rel_bias=b_rel_bias,
                   b_w_out=b_w_out, ffn_w_gate=ffn_w_gate, ffn_w_up=ffn_w_up, ffn_w_down=ffn_w_down)
    mom_m = dict(norm_mix_g=m_norm_mix_g, norm_ffn_g=m_norm_ffn_g, final_g=m_final_g, a_w_in=m_a_w_in,
                 a_v_gain=m_a_v_gain, a_w_s=m_a_w_s, a_b_s=m_a_b_s, a_w_out=m_a_w_out, b_w_qkv=m_b_w_qkv,
                 b_rel_bias=m_b_rel_bias, b_w_out=m_b_w_out, ffn_w_gate=m_ffn_w_gate, ffn_w_up=m_ffn_w_up,
                 ffn_w_down=m_ffn_w_down)
    mom_v = dict(norm_mix_g=v_norm_mix_g, norm_ffn_g=v_norm_ffn_g, final_g=v_final_g, a_w_in=v_a_w_in,
                 a_v_gain=v_a_v_gain, a_w_s=v_a_w_s, a_b_s=v_a_b_s, a_w_out=v_a_w_out, b_w_qkv=v_b_w_qkv,
                 b_rel_bias=v_b_rel_bias, b_w_out=v_b_w_out, ffn_w_gate=v_ffn_w_gate, ffn_w_up=v_ffn_w_up,
                 ffn_w_down=v_ffn_w_down)
    order = list(weights)
    transposed = ("ffn_w_gate", "ffn_w_up")
    for k in transposed:
        weights[k], mom_m[k], mom_v[k] = (jnp.swapaxes(a, 1, 2) for a in (weights[k], mom_m[k], mom_v[k]))

    xi, yi, ci = _coords()
    me = jnp.stack([ci, 2 * xi + yi]).astype(jnp.int32)

    shard16 = {k: cast_bf16(weights[k], "cast_" + k) for k in BIG}

    def layer_tensors(i):
        mix = ("a_w_in", "a_w_out") if i % 2 == 0 else ("b_w_qkv", "b_w_out")
        return [(k, i // 2) for k in mix] + [(k, i) for k in ("ffn_w_gate", "ffn_w_up", "ffn_w_down")]

    def gather_items(keys):
        return [(shard16[k], l, GATHER_KIND[k]) for k, l in keys]

    def grad_metas(keys):
        return [(GATHER_KIND[k],) + tuple(weights[k].shape[1:]) for k, _ in keys]

    FFN = ("ffn_w_gate", "ffn_w_up", "ffn_w_down")
    k0a = [("a_w_out", 0), ("ffn_w_gate", 0)]
    k0b = [("ffn_w_up", 0), ("ffn_w_down", 0)]
    k1a = [("b_w_qkv", 0), ("b_w_out", 0), ("ffn_w_gate", 1)]
    k1b = [("ffn_w_up", 1), ("ffn_w_down", 1)]
    k3a = [("b_w_qkv", 1), ("b_w_out", 1), ("ffn_w_gate", 3)]
    k3b = [("ffn_w_up", 3), ("ffn_w_down", 3)]
    plans = {
        "a_in_l0": [("g1", k0a)], "sgu_fwd_l0": [("g2", k0a), ("g1", k0b)], "a_out_l0": [("g2", k0b)],
        "rms_mix_l0": [("g1", [("a_w_in", 0)])],
        "ffn_up_l0": [("g1", k1a)], "ffn_down_l0": [("g2", k1a), ("g1", k1b[:1])],
        "b_qkv_l1": [("g2", k1b[:1]), ("g1", k1b[1:])],
        "attn_fwd_l1": [("g2", k1b[1:]), ("g1", layer_tensors(2))], "b_out_l1": [("g2", layer_tensors(2))],
        "ffn_up_l1": [("g1", k3a)], "ffn_down_l1": [("g2", k3a)],
        "a_in_l2": [("g1", k3b)], "sgu_fwd_l2": [("g2", k3b)],
        "ffn_bwd_dhn_l2": [("ex", layer_tensors(3))], "sgu_bwd_l2": [("sc", layer_tensors(3))],
        "ffn_bwd_dhn_l1": [("ex", layer_tensors(2))], "attn_bwd_l1": [("sc", layer_tensors(2))],
        "dw_down_l0": [("ex", layer_tensors(1))], "ffn_bwd_dhn_l0": [("sc", k1a)],
        "dffn_w_gate_l0": [("sc", [("ffn_w_up", 1)])], "dffn_w_up_l0": [("sc", [("ffn_w_down", 1)])],
        "a_out_bwd_l0": [("ex", [(k, 0) for k in FFN])],
        "sgu_bwd_l0": [("sc", [("ffn_w_gate", 0), ("ffn_w_up", 0)]), ("ex", [("a_w_out", 0)])],
        "dw_in_l0": [("sc", [("ffn_w_down", 0), ("a_w_out", 0)])],
    }
    part16, full16 = {}, {}
    sib, p16, own_parts, recv_parts = {}, {}, {}, {}

    def make_comm(kind, keys):
        if kind == "g1":
            return gather_stage1(gather_items(keys)), lambda outs: part16.update(zip(keys, outs))
        if kind == "g2":
            return (gather_stage2(gather_items(keys), [part16[kl] for kl in keys]),
                    lambda outs: full16.update(zip(keys, outs)))
        if kind == "ex":
            return (exchange_halves([big_grads[k][l] for k, l in keys], grad_metas(keys)),
                    lambda outs: sib.update(zip(keys, outs)))
        for kl, m_ in zip(keys, grad_metas(keys)):
            p16[kl], own_parts[kl] = pair_sum(me, big_grads[kl[0]][kl[1]], sib[kl], m_, "pair_sum_%s_l%d" % kl)
        return (scatter_partials([p16[kl] for kl in keys], grad_metas(keys)),
                lambda outs: recv_parts.update(zip(keys, outs)))

    def run(name, make):
        steps = plans.get(name)
        if not steps:
            return make(None)
        made = [make_comm(kind, keys) for kind, keys in steps]
        main, outs = make(combine([c for c, _ in made]))
        for c, done in made:
            done(outs[:len(c.out_shapes)])
            outs = outs[len(c.out_shapes):]
        return main

    def weight(k, l):
        w = full16[(k, l)]
        if k == "a_w_out":
            return w.reshape(1, GH, D)
        return w.reshape(1, D, D) if k == "b_w_out" else w


    xcur = x.reshape(T, D)
    hn = run("rms_mix_l0", lambda comm: rms_fwd(xcur, norm_mix_g[0][None], "rms_mix_l0", comm=comm))
    comm, done = make_comm("g2", [("a_w_in", 0)])
    done(run_comm(comm, "gather_first_d2d"))
    saved = []
    for i in range(DEPTH):
        j = i // 2
        tag = "_l%d" % i
        st = {"x_in": xcur, "hn": hn}
        if i % 2 == 0:
            pre = run("a_in" + tag, lambda comm: matmul(
                "a_in" + tag, NN, hn, pl.BlockSpec((1024, D), lambda i_, j_: (i_, 0)),
                weight("a_w_in", j), pl.BlockSpec((None, D, 1024), lambda i_, j_: (0, 0, j_)),
                _sds((T, 2 * GH), BF16), pl.BlockSpec((1024, 1024), lambda i_, j_: (i_, j_)),
                (T // 1024, 4), comm=comm))
            y = run("sgu_fwd" + tag, lambda comm: sgu_fwd(
                pre, a_v_gain[j][None], a_w_s[j], a_b_s[j][:, :, None], "sgu_fwd" + tag, comm=comm))
            xmid, hn2 = run("a_out" + tag, lambda comm: residual_proj(
                "a_out" + tag, *out_proj(y, weight("a_w_out", j)), xcur, norm_ffn_g[i][None], comm=comm))
            st.update(pre=pre, y=y)
        else:
            qkvp = run("b_qkv" + tag, lambda comm: proj_qkv(hn, weight("b_w_qkv", j), 0, "b_qkv" + tag, comm=comm))
            wb = jnp.transpose(bias_build(b_rel_bias[j], "bias_build" + tag), (1, 0, 2))
            o = run("attn_fwd" + tag, lambda comm: attn_fwd(qkvp, wb, "attn_fwd" + tag, comm=comm))
            xmid, hn2 = run("b_out" + tag, lambda comm: residual_proj(
                "b_out" + tag, *out_proj(o, weight("b_w_out", j)), xcur, norm_ffn_g[i][None], comm=comm))
            st.update(qkvp=qkvp, wb=wb, o=o)
        g, u, h = run("ffn_up" + tag, lambda comm: ffn_up(
            hn2, weight("ffn_w_gate", i), weight("ffn_w_up", i), 0, "ffn_up" + tag, comm=comm))
        next_g = norm_mix_g[i + 1][None] if i + 1 < DEPTH else None
        down = run("ffn_down" + tag, lambda comm: residual_proj(
            "ffn_down" + tag, *ffn_down(h, weight("ffn_w_down", i)), xmid, next_g, comm=comm))
        xcur, hn = down if next_g is not None else (down, None)
        st.update(x_mid=xmid, hn2=hn2, g=g, u=u, h=h)
        saved.append(st)

    loss_part, dx, dxb, d_final = final_loss(xcur, final_g[None], loss_target.reshape(T, D), "final_loss")

    tk = min(2048, T)
    big_grads = {k: [None] * weights[k].shape[0] for k in BIG}
    small = {"norm_mix_g": [None] * DEPTH, "norm_ffn_g": [None] * DEPTH, "a_v_gain": [None] * 2,
             "a_w_s": [None] * 2, "a_b_s": [None] * 2, "b_rel_bias": [None] * 2}
    tok = lambda width: pl.BlockSpec((tk, width), lambda j_, k_: (k_, 0))
    part = lambda: pl.BlockSpec((None, tk, FS), lambda j_, k_: (j_, k_, 0))
    for i in reversed(range(DEPTH)):
        j = i // 2
        tag = "_l%d" % i
        st = saved[i]
        dg, du = run("ffn_bwd_dh" + tag, lambda comm: ffn_bwd_dh(
            dxb, weight("ffn_w_down", i), st["g"], st["u"], 0, "ffn_bwd_dh" + tag, comm=comm))
        big_grads["ffn_w_down"][i] = run("dw_down" + tag, lambda comm: wgrad(
            "dw_down" + tag, st["h"], part(), dxb, tok(D), _sds((N_CHIPS, FS, D), F32),
            pl.BlockSpec((None, FS, D), lambda j_, k_: (j_, 0, 0)), N_CHIPS, T, tk, comm=comm))
        dx_mid, dxb_mid, dgn = run("ffn_bwd_dhn" + tag, lambda comm: dgrad_rms(
            "ffn_bwd_dhn" + tag, *ffn_dgrad(dg, du, weight("ffn_w_gate", i), weight("ffn_w_up", i)),
            st["x_mid"], norm_ffn_g[i][None], dx, comm=comm))
        for nm, dz in (("ffn_w_gate", dg), ("ffn_w_up", du)):
            big_grads[nm][i] = run("d" + nm + tag, lambda comm: wgrad(
                "d" + nm + tag, dz, part(), st["hn2"], tok(D), _sds((N_CHIPS, FS, D), F32),
                pl.BlockSpec((None, FS, D), lambda j_, k_: (j_, 0, 0)), N_CHIPS, T, tk, comm=comm))
        dx, dxb = dx_mid, dxb_mid
        small["norm_ffn_g"][i] = dgn
        if i % 2 == 0:
            dy = run("a_out_bwd" + tag, lambda comm: matmul(
                "a_out_bwd" + tag, NT, dxb, pl.BlockSpec((1024, D), lambda i_, j_: (i_, 0)),
                weight("a_w_out", j), pl.BlockSpec((None, 1024, D), lambda i_, j_: (0, j_, 0)),
                _sds((T, GH), BF16), pl.BlockSpec((1024, 1024), lambda i_, j_: (i_, j_)), (T // 1024, 2), comm=comm))
            big_grads["a_w_out"][j] = wgrad(
                "dw_aout" + tag, st["y"], pl.BlockSpec((tk, 1024), lambda j_, k_: (k_, j_)), dxb, tok(D),
                _sds((GH, D), F32), pl.BlockSpec((1024, D), lambda j_, k_: (j_, 0)), 2, T, tk
            ).reshape(N_CHIPS, GH // N_CHIPS, D)
            dpre, d_ws, d_bs, d_gain = run("sgu_bwd" + tag, lambda comm: sgu_bwd(
                st["pre"], dy, a_v_gain[j][None], a_w_s[j], a_b_s[j][:, :, None], "sgu_bwd" + tag,
                tm=2 * SGU_BLOCK, comm=comm))
            small["a_w_s"][j], small["a_b_s"][j], small["a_v_gain"][j] = d_ws, d_bs, d_gain
            dx_in, dxb_in, dgn = run("a_in_bwd" + tag, lambda comm: dgrad_rms(
                "a_in_bwd" + tag, *in_dgrad(dpre, weight("a_w_in", j)),
                st["x_in"], norm_mix_g[i][None], dx, comm=comm))
            big_grads["a_w_in"][j] = run("dw_in" + tag, lambda comm: wgrad(
                "dw_in" + tag, st["hn"], tok(D), dpre, pl.BlockSpec((tk, 1024), lambda j_, k_: (k_, j_)),
                _sds((D, 2 * GH), F32), pl.BlockSpec((D, 1024), lambda j_, k_: (0, j_)), 4, T, tk, comm=comm))
        else:
            do = matmul("b_out_bwd" + tag, NT, dxb, pl.BlockSpec((1024, D), lambda i_, j_: (i_, 0)),
                        weight("b_w_out", j), pl.BlockSpec((None, D, D), lambda i_, j_: (0, 0, 0)),
                        _sds((T, D), BF16), pl.BlockSpec((1024, D), lambda i_, j_: (i_, 0)), (T // 1024, 1))
            big_grads["b_w_out"][j] = wgrad(
                "dw_bout" + tag, st["o"], tok(D), dxb, tok(D),
                _sds((D, D), F32), pl.BlockSpec((D, D), lambda j_, k_: (0, 0)), 1, T, tk
            ).reshape(N_CHIPS, D // N_CHIPS, D)
            dqkvp, dwb = run("attn_bwd" + tag, lambda comm: attn_bwd(
                st["qkvp"], st["o"], do, st["wb"], "attn_bwd" + tag, comm=comm))
            small["b_rel_bias"][j] = bias_grad(
                jnp.pad(jnp.transpose(dwb, (1, 0, 2)), ((0, 0), (0, 0), (0, DIAG - KW))), "bias_grad" + tag)
            dx_in, dxb_in, dgn = dgrad_rms(
                "b_qkv_bwd" + tag, *qkv_dgrad(dqkvp, weight("b_w_qkv", j)),
                st["x_in"], norm_mix_g[i][None], dx)
            big_grads["b_w_qkv"][j] = wgrad(
                "dw_qkv" + tag, st["hn"], tok(D), dqkvp,
                pl.BlockSpec((None, tk, D), lambda j_, k_: (j_, k_ + FRONT // tk, 0)),
                _sds((D, 3 * D), F32), pl.BlockSpec((D, D), lambda j_, k_: (0, j_)), 3, T, tk)
        dx, dxb = dx_in, dxb_in
        small["norm_mix_g"][i] = dgn

    small_grads = {
        "norm_mix_g": jnp.concatenate(small["norm_mix_g"], axis=0),
        "norm_ffn_g": jnp.concatenate(small["norm_ffn_g"], axis=0),
        "final_g": d_final.reshape(D),
        "a_v_gain": jnp.concatenate(small["a_v_gain"], axis=0),
        "a_w_s": jnp.stack(small["a_w_s"]),
        "a_b_s": jnp.stack(small["a_b_s"]).reshape(2, SGU_G, SGU_BLOCK),
        "b_rel_bias": jnp.stack(small["b_rel_bias"]),
    }
    small_names = list(small_grads)
    packed = [_rows128(small_grads[k]) for k in small_names] + [_rows128(loss_part[:, :1])]
    offs = [0]
    for p in packed:
        offs.append(offs[-1] + p.shape[0])
    reduced = allreduce_small(jnp.concatenate(packed, axis=0))
    grads = {}
    for t, k in enumerate(small_names):
        nelem = small_grads[k].size
        grads[k] = reduced[offs[t]:offs[t + 1]].reshape(-1)[:nelem].reshape(weights[k].shape)
    loss = reduced[offs[len(small_names)], 0]

    last = [("a_w_in", 0)]
    for kind, name in (("ex", "exchange_last"), ("sc", "scatter_last")):
        comm, done = make_comm(kind, last)
        done(run_comm(comm, name))
    bufs = {k: None for k in BIG}
    for i in range(DEPTH):
        for kl, m_ in zip(layer_tensors(i), grad_metas(layer_tensors(i))):
            bufs[kl[0]] = final_sum(me, own_parts[kl], recv_parts[kl], bufs[kl[0]], weights[kl[0]].shape, kl[1], m_,
                                    "final_sum_%s_l%d" % kl)
    shared = share_final([bufs[k] for k in BIG])
    for k, gfull in zip(BIG, shared):
        grads[k] = gfull

    delta, new_m, new_v = {}, {}, {}
    for k in order:
        shp = weights[k].shape
        if k in BIG:
            view = shp
        elif k == "a_w_s":
            view = (2, SGU_G * SGU_BLOCK, SGU_BLOCK)
        elif len(shp) == 1:
            view = (1, 1, shp[0])
        elif len(shp) == 2:
            view = (1,) + shp
        else:
            view = shp
        g_, d_, m_, v_ = adamw(weights[k].reshape(view), grads[k].reshape(view), mom_m[k].reshape(view),
                               mom_v[k].reshape(view), "adamw_" + k)
        grads[k], delta[k], new_m[k], new_v[k] = g_.reshape(shp), d_.reshape(shp), m_.reshape(shp), v_.reshape(shp)
    for k in transposed:
        for tree in (grads, delta, new_m, new_v):
            tree[k] = jnp.swapaxes(tree[k], 1, 2)

    return (loss, dx.reshape(1, T, D), *[grads[k] for k in order], *[delta[k] for k in order],
            *[new_m[k] for k in order], *[new_v[k] for k in order])
```

```python
import functools

import jax
import jax.numpy as jnp
from jax import lax
from jax.experimental import pallas as pl
from jax.experimental.pallas import tpu as pltpu

F32 = jnp.float32
BF16 = jnp.bfloat16
MESH = pl.DeviceIdType.MESH

D = 1024
DEPTH = 4
EPS = 1e-6
SGU_BLOCK = 128
GH = 2048
SGU_G = 8
SGU_GD = GH // SGU_G
N_HEADS = 16
HEAD_DIM = 64
CHUNK = 64
PAD = 8 * CHUNK
FRONT = 2048
QB = 128
KW = PAD + QB
N_REL = 192
REL_MIN = -(CHUNK - 1)
REL_MAX = 128
D_FF = 2816
FS = D_FF // 4
NEG = -1e30
SCALE = HEAD_DIM ** -0.5
N_CHIPS = 4

ADAM_LR = 0.001
ADAM_B1 = 0.9
ADAM_B2 = 0.999
ADAM_EPS = 1e-08
ADAM_WD = 0.01
ADAM_STEP = 10

VMEM_BIG = 56 * 1024 * 1024

NN = ((1,), (0,))
NT = ((1,), (1,))
TN = ((0,), (0,))


def _dot(a, b, dims):
    return lax.dot_general(a, b, (dims, ((), ())), preferred_element_type=F32)


class Comm:
    def __init__(self, ins, out_shapes, sems, start, wait, aliases=None):
        self.ins, self.out_shapes, self.sems = list(ins), list(out_shapes), list(sems)
        self.start, self.wait, self.aliases = start, wait, dict(aliases or {})


def _host(body, comm, kw):
    grid = tuple(kw["grid"])
    in_specs = list(kw["in_specs"])
    single = not isinstance(kw["out_specs"], (list, tuple))
    out_specs = [kw["out_specs"]] if single else list(kw["out_specs"])
    out_shape = [kw["out_shape"]] if single else list(kw["out_shape"])
    scratch = list(kw.get("scratch_shapes", ()))
    counts = (len(in_specs), len(comm.ins), len(out_specs), len(comm.out_shapes), len(scratch))

    def hosted(*refs):
        parts, p = [], 0
        for cnt in counts:
            parts.append(refs[p:p + cnt])
            p += cnt
        main_in, c_in, main_out, c_out, main_scr = parts
        sems = refs[p:]
        ids = [pl.program_id(a) for a in range(len(grid))]
        first = functools.reduce(jnp.logical_and, [i == 0 for i in ids])
        last = functools.reduce(jnp.logical_and, [i == n - 1 for i, n in zip(ids, grid)])
        pl.when(first)(lambda: comm.start(c_in, c_out, sems))
        body(*main_in, *main_out, *main_scr)
        pl.when(last)(lambda: comm.wait(c_in, c_out, sems))

    old = kw["compiler_params"]
    kw = dict(kw, in_specs=in_specs + [ANY] * len(comm.ins), out_specs=out_specs + [ANY] * len(comm.out_shapes),
              out_shape=out_shape + comm.out_shapes, scratch_shapes=scratch + comm.sems,
              compiler_params=pltpu.CompilerParams(dimension_semantics=("arbitrary",) * len(grid),
                                                   vmem_limit_bytes=old.vmem_limit_bytes, has_side_effects=True))
    if comm.aliases:
        kw["input_output_aliases"] = {counts[0] + i: counts[2] + o for i, o in comm.aliases.items()}
    return hosted, kw


def _pallas(body, comm=None, **kw):
    if comm is not None:
        body, kw = _host(body, comm, kw)
    return pl.pallas_call(body, **kw)


def _split_outs(outs, comm, n_main):
    outs = list(outs) if isinstance(outs, (list, tuple)) else [outs]
    main = outs[:n_main]
    return (main[0] if n_main == 1 else main), outs[n_main:]


def run_comm(comm, name):
    nci, nco = len(comm.ins), len(comm.out_shapes)

    def body(*refs):
        c_in, c_out, sems = refs[:nci], refs[nci:nci + nco], refs[nci + nco:]
        comm.start(c_in, c_out, sems)
        comm.wait(c_in, c_out, sems)

    kw = {}
    if comm.aliases:
        kw["input_output_aliases"] = dict(comm.aliases)
    return _pallas(body, name=name, in_specs=[ANY] * nci, out_specs=[ANY] * nco, out_shape=comm.out_shapes,
                   scratch_shapes=comm.sems, compiler_params=pltpu.CompilerParams(has_side_effects=True),
                   **kw)(*comm.ins)


def combine(comms):
    if len(comms) == 1:
        return comms[0]
    spans, ni, no, ns = [], 0, 0, 0
    for c in comms:
        spans.append((slice(ni, ni + len(c.ins)), slice(no, no + len(c.out_shapes)), slice(ns, ns + len(c.sems))))
        ni, no, ns = ni + len(c.ins), no + len(c.out_shapes), ns + len(c.sems)

    def start(ins, outs, sems):
        for c, (si, so, ss) in zip(comms, spans):
            c.start(ins[si], outs[so], sems[ss])

    def wait(ins, outs, sems):
        for c, (si, so, ss) in zip(comms, spans):
            c.wait(ins[si], outs[so], sems[ss])

    aliases = {}
    for c, (si, so, _) in zip(comms, spans):
        aliases.update({si.start + i: so.start + o for i, o in c.aliases.items()})
    return Comm([a for c in comms for a in c.ins], [o for c in comms for o in c.out_shapes],
                [s for c in comms for s in c.sems], start, wait, aliases)


def _call(body, comm, n_main, args, **kw):
    if comm is None:
        return _pallas(body, **kw)(*args)
    return _split_outs(_pallas(body, comm=comm, **kw)(*args, *comm.ins), comm, n_main)


def _params(sem=None, vmem=None):
    return pltpu.CompilerParams(dimension_semantics=sem, vmem_limit_bytes=vmem)


def _sds(shape, dtype):
    return jax.ShapeDtypeStruct(tuple(shape), dtype)


_GELU_C = 0.7978845608028654


_GELU_A = _GELU_C * 0.044715


def _gelu(x):
    t = jnp.tanh(x * (_GELU_C + _GELU_A * (x * x)))
    h = 0.5 * x
    return h + h * t


def _gelu_and_grad(x):
    x2 = x * x
    t = jnp.tanh(x * (_GELU_C + _GELU_A * x2))
    h = 0.5 * x
    val = h + h * t
    grad = (0.5 + 0.5 * t) + (h * (1.0 - t * t)) * (_GELU_C + (3.0 * _GELU_A) * x2)
    return val, grad


def _sigmoid(x):
    return 0.5 * (jnp.tanh(0.5 * x) + 1.0)


def cast_bf16(w, name):
    L, R, C = w.shape

    def body(w_ref, o_ref):
        o_ref[...] = w_ref[...].astype(BF16)

    spec = pl.BlockSpec((None, R, C), lambda l: (l, 0, 0))
    return _pallas(body, name=name, grid=(L,), in_specs=[spec], out_specs=spec,
                   out_shape=_sds((L, R, C), BF16), compiler_params=_params(("parallel",)))(w)


def rms_fwd(x, g, name, tm=512, comm=None):
    T = x.shape[0]

    def body(x_ref, g_ref, o_ref):
        o_ref[...] = _rms_rows(x_ref[...], g_ref[...])

    row = pl.BlockSpec((tm, D), lambda i: (i, 0))
    return _call(body, comm, 1, (x, g), name=name, grid=(T // tm,),
                 in_specs=[row, pl.BlockSpec((1, D), lambda i: (0, 0))], out_specs=row,
                 out_shape=_sds((T, D), BF16), compiler_params=_params(("parallel",)))


def dgrad_rms(name, compute, args, specs, x, g, dres, tm=512, comm=None):
    T = x.shape[0]
    n = T // tm
    k = len(args)

    def body(*refs):
        x_ref, g_ref, dres_ref, dx_ref, dxb_ref, dg_ref, acc_ref = refs[k:]
        i = pl.program_id(0)
        xf = x_ref[...]
        r = lax.rsqrt(jnp.mean(xf * xf, axis=-1, keepdims=True) + EPS)
        xhat = xf * r
        dhf = compute(*refs[:k])
        part = (dhf * xhat).reshape(tm // 8, 8, D).sum(axis=0)

        @pl.when(i == 0)
        def _():
            acc_ref[...] = part

        @pl.when(i > 0)
        def _():
            acc_ref[...] += part

        dxhat = dhf * g_ref[...]
        dx = dres_ref[...] + r * (dxhat - xhat * jnp.mean(dxhat * xhat, axis=-1, keepdims=True))
        dx_ref[...] = dx
        dxb_ref[...] = dx.astype(BF16)

        @pl.when(i == n - 1)
        def _():
            dg_ref[...] = jnp.sum(acc_ref[...], axis=0, keepdims=True)

    row = pl.BlockSpec((tm, D), lambda i: (i, 0))
    vec = pl.BlockSpec((1, D), lambda i: (0, 0))
    return _call(body, comm, 3, (*args, x, g, dres), name=name, grid=(n,),
                 in_specs=list(specs) + [row, vec, row], out_specs=[row, row, vec],
                 out_shape=[_sds((T, D), F32), _sds((T, D), BF16), _sds((1, D), F32)],
                 scratch_shapes=[pltpu.VMEM((8, D), F32)],
                 compiler_params=_params(("arbitrary",), VMEM_BIG))


def final_loss(x, g, tgt, name, tm=256):
    T = x.shape[0]
    n = T // tm

    def body(x_ref, g_ref, t_ref, loss_ref, dx_ref, dxb_ref, dg_ref, acc_ref, lacc_ref):
        i = pl.program_id(0)
        xf = x_ref[...]
        r = lax.rsqrt(jnp.mean(xf * xf, axis=-1, keepdims=True) + EPS)
        xhat = xf * r
        gg = g_ref[...]
        e = xhat * gg - t_ref[...]
        dy = e * (1.0 / D)
        part = (dy * xhat).reshape(tm // 8, 8, D).sum(axis=0)
        lpart = (e * e).reshape(tm // 8, 8, D).sum(axis=0)

        @pl.when(i == 0)
        def _():
            acc_ref[...] = part
            lacc_ref[...] = lpart

        @pl.when(i > 0)
        def _():
            acc_ref[...] += part
            lacc_ref[...] += lpart

        dxhat = dy * gg
        dx = r * (dxhat - xhat * jnp.mean(dxhat * xhat, axis=-1, keepdims=True))
        dx_ref[...] = dx
        dxb_ref[...] = dx.astype(BF16)

        @pl.when(i == n - 1)
        def _():
            dg_ref[...] = jnp.sum(acc_ref[...], axis=0, keepdims=True)
            total = jnp.sum(jnp.sum(lacc_ref[...], axis=0, keepdims=True), axis=1, keepdims=True)
            loss_ref[...] = jnp.broadcast_to(total * (0.5 / D), (1, 128))

    row = pl.BlockSpec((tm, D), lambda i: (i, 0))
    vec = pl.BlockSpec((1, D), lambda i: (0, 0))
    return _pallas(body, name=name, grid=(n,), in_specs=[row, vec, row],
                   out_specs=[pl.BlockSpec((1, 128), lambda i: (0, 0)), row, row, vec],
                   out_shape=[_sds((1, 128), F32), _sds((T, D), F32), _sds((T, D), BF16), _sds((1, D), F32)],
                   scratch_shapes=[pltpu.VMEM((8, D), F32), pltpu.VMEM((8, D), F32)],
                   compiler_params=_params(("arbitrary",)))(x, g, tgt)


def matmul(name, dims, a, a_spec, b, b_spec, out_shape, out_spec, grid, *, acc=False, res=None, res_spec=None,
           comm=None):
    has_res = res is not None

    def body(*refs):
        a_ref, b_ref = refs[0], refs[1]
        r_ref = refs[2] if has_res else None
        o_ref = refs[-1]
        d = _dot(a_ref[...], b_ref[...], dims)
        if not acc:
            if has_res:
                d = d + r_ref[...]
            o_ref[...] = d.astype(o_ref.dtype)
        else:
            k = pl.program_id(len(grid) - 1)

            @pl.when(k == 0)
            def _():
                o_ref[...] = (d + r_ref[...]) if has_res else d

            @pl.when(k > 0)
            def _():
                o_ref[...] += d

    sem = ("parallel",) * (len(grid) - 1) + (("arbitrary",) if acc else ("parallel",))
    ins = [a, b] + ([res] if has_res else [])
    specs = [a_spec, b_spec] + ([res_spec] if has_res else [])
    return _call(body, comm, 1, ins, name=name, grid=grid, in_specs=specs, out_specs=out_spec, out_shape=out_shape,
                 compiler_params=_params(sem, VMEM_BIG))


def wgrad(name, a, a_spec, b, b_spec, out_shape, out_spec, J, T, tk, comm=None):
    return matmul(name, TN, a, a_spec, b, b_spec, out_shape, out_spec, (J, T // tk), acc=True, comm=comm)


def _sgu_mask():
    p = lax.broadcasted_iota(jnp.int32, (SGU_BLOCK, SGU_BLOCK), 0)
    q = lax.broadcasted_iota(jnp.int32, (SGU_BLOCK, SGU_BLOCK), 1)
    return lax.shift_right_logical(q, 6) <= lax.shift_right_logical(p, 6)


def sgu_fwd(pre, gain, w_s, b_s, name, comm=None):
    T = pre.shape[0]

    def body(pre_ref, gain_ref, ws_ref, bs_ref, y_ref):
        mask = _sgu_mask()
        u = _gelu(pre_ref[:, :GH].astype(F32))
        va = _gelu(pre_ref[:, GH:].astype(F32))
        r = lax.rsqrt(jnp.mean(va * va, axis=-1, keepdims=True) + EPS)
        vn = ((va * r) * gain_ref[...]).astype(BF16)
        for g in range(SGU_G):
            sl = slice(g * SGU_GD, (g + 1) * SGU_GD)
            wm = jnp.where(mask, ws_ref[g], 0.0).astype(BF16)
            vm = _dot(wm, vn[:, sl], NN) + bs_ref[g]
            y_ref[:, sl] = (u[:, sl] * vm).astype(BF16)

    return _call(
        body, comm, 1, (pre, gain, w_s, b_s), name=name, grid=(T // SGU_BLOCK,),
        in_specs=[pl.BlockSpec((SGU_BLOCK, 2 * GH), lambda i: (i, 0)),
                  pl.BlockSpec((1, GH), lambda i: (0, 0)),
                  pl.BlockSpec((SGU_G, SGU_BLOCK, SGU_BLOCK), lambda i: (0, 0, 0)),
                  pl.BlockSpec((SGU_G, SGU_BLOCK, 1), lambda i: (0, 0, 0))],
        out_specs=pl.BlockSpec((SGU_BLOCK, GH), lambda i: (i, 0)),
        out_shape=_sds((T, GH), BF16), compiler_params=_params(("parallel",)))


def sgu_bwd(pre, dy, gain, w_s, b_s, name, tm=SGU_BLOCK, comm=None):
    T = pre.shape[0]
    n = T // tm

    def body(pre_ref, dy_ref, gain_ref, ws_ref, bs_ref, dpre_ref, dws_ref, dbs_ref, dgain_ref, gacc_ref):
        i = pl.program_id(0)

        @pl.when(i == 0)
        def _():
            dws_ref[...] = jnp.zeros_like(dws_ref)
            dbs_ref[...] = jnp.zeros_like(dbs_ref)
            gacc_ref[...] = jnp.zeros_like(gacc_ref)

        mask = _sgu_mask()
        gain_v = gain_ref[...]
        for sb in range(tm // SGU_BLOCK):
            rows = slice(sb * SGU_BLOCK, (sb + 1) * SGU_BLOCK)
            u, du_dpre = _gelu_and_grad(pre_ref[rows, :GH].astype(F32))
            va, dva_dpre = _gelu_and_grad(pre_ref[rows, GH:].astype(F32))
            r = lax.rsqrt(jnp.mean(va * va, axis=-1, keepdims=True) + EPS)
            vhat = va * r
            vn = (vhat * gain_v).astype(BF16)
            dyf = dy_ref[rows, :].astype(F32)
            dvn_parts = []
            for grp in range(SGU_G):
                sl = slice(grp * SGU_GD, (grp + 1) * SGU_GD)
                wm = jnp.where(mask, ws_ref[grp], 0.0).astype(BF16)
                vm = _dot(wm, vn[:, sl], NN) + bs_ref[grp]
                dpre_ref[rows, sl] = ((dyf[:, sl] * vm) * du_dpre[:, sl]).astype(BF16)
                dvm = dyf[:, sl] * u[:, sl]
                dbs_ref[grp] += jnp.sum(dvm, axis=-1, keepdims=True)
                dvm16 = dvm.astype(BF16)
                dws_ref[grp] += jnp.where(mask, _dot(dvm16, vn[:, sl], NT), 0.0)
                dvn_parts.append(_dot(wm, dvm16, TN))
            dvn = jnp.concatenate(dvn_parts, axis=-1)
            gacc_ref[...] += (dvn * vhat).reshape(SGU_BLOCK // 8, 8, GH).sum(axis=0)
            dvhat = dvn * gain_v
            dva = r * (dvhat - vhat * jnp.mean(dvhat * vhat, axis=-1, keepdims=True))
            dpre_ref[rows, GH:] = (dva * dva_dpre).astype(BF16)

        @pl.when(i == n - 1)
        def _():
            dgain_ref[...] = jnp.sum(gacc_ref[...], axis=0, keepdims=True)

    const3 = lambda i: (0, 0, 0)
    return _call(
        body, comm, 4, (pre, dy, gain, w_s, b_s), name=name, grid=(n,),
        in_specs=[pl.BlockSpec((tm, 2 * GH), lambda i: (i, 0)),
                  pl.BlockSpec((tm, GH), lambda i: (i, 0)),
                  pl.BlockSpec((1, GH), lambda i: (0, 0)),
                  pl.BlockSpec((SGU_G, SGU_BLOCK, SGU_BLOCK), const3),
                  pl.BlockSpec((SGU_G, SGU_BLOCK, 1), const3)],
        out_specs=[pl.BlockSpec((tm, 2 * GH), lambda i: (i, 0)),
                   pl.BlockSpec((SGU_G, SGU_BLOCK, SGU_BLOCK), const3),
                   pl.BlockSpec((SGU_G, SGU_BLOCK, 1), const3),
                   pl.BlockSpec((1, GH), lambda i: (0, 0))],
        out_shape=[_sds((T, 2 * GH), BF16), _sds((SGU_G, SGU_BLOCK, SGU_BLOCK), F32),
                   _sds((SGU_G, SGU_BLOCK, 1), F32), _sds((1, GH), F32)],
        scratch_shapes=[pltpu.VMEM((8, GH), F32)],
        compiler_params=_params(("arbitrary",)))


DIAG = 768


def _diag_onehot():
    n = lax.broadcasted_iota(jnp.int32, (N_REL, DIAG), 1)
    r = lax.broadcasted_iota(jnp.int32, (N_REL, DIAG), 0)
    idx = jnp.clip(KW - 1 - n, REL_MIN, REL_MAX) - REL_MIN
    return (idx == r).astype(BF16)


def _split3(v):
    hi = v.astype(BF16)
    r1 = v - hi.astype(F32)
    mid = r1.astype(BF16)
    lo = (r1 - mid.astype(F32)).astype(BF16)
    return hi, mid, lo


def bias_build(rel_bias, name):
    def body(rb_ref, o_ref):
        oh = _diag_onehot()
        hi, mid, lo = _split3(rb_ref[...])
        u = (_dot(hi, oh, NN) + _dot(mid, oh, NN) + _dot(lo, oh, NN)) * LOG2E
        j = lax.broadcasted_iota(jnp.int32, (1, KW), 1)

        def row(i, carry):
            val = pltpu.roll(u, (i + (DIAG - QB + 1)) % DIAG, 1)[:, :KW]
            rel = lax.shift_right_logical(i, 6) - lax.shift_right_logical(j, 6) + 8
            ok = (rel >= 0) & (rel <= 8)
            o_ref[i] = jnp.where(ok, val, NEG)
            return carry

        lax.fori_loop(0, QB, row, 0)

    return _pallas(body, name=name, out_shape=_sds((QB, N_HEADS, KW), F32),
                   in_specs=[pl.BlockSpec(memory_space=pltpu.VMEM)],
                   out_specs=pl.BlockSpec(memory_space=pltpu.VMEM))(rel_bias)


def bias_grad(dwb, name):
    def body(d_ref, o_ref):
        def row(i, acc):
            return acc + pltpu.roll(d_ref[i], QB - 1 - i, 1)

        du = lax.fori_loop(0, QB, row, jnp.zeros((N_HEADS, DIAG), F32))
        oh = _diag_onehot()
        hi, mid, lo = _split3(du)
        o_ref[...] = _dot(hi, oh, NT) + _dot(mid, oh, NT) + _dot(lo, oh, NT)

    return _pallas(body, name=name, out_shape=_sds((N_HEADS, N_REL), F32),
                   in_specs=[pl.BlockSpec(memory_space=pltpu.VMEM)],
                   out_specs=pl.BlockSpec(memory_space=pltpu.VMEM))(dwb)


LOG2E = 1.4426950408889634
Q_SCALE = SCALE * LOG2E


def _attn_block(qkv_ref, blk, masked):
    r0 = pl.multiple_of(blk * QB, QB)
    qs = qkv_ref[0, pl.ds(r0 + FRONT, QB), :]
    kvalid = (lax.broadcasted_iota(jnp.int32, (1, KW), 1) >= PAD - blk * QB) if masked else None
    return r0, qs, kvalid


def _step_windows(qkv_ref, b, step):
    r0 = pl.multiple_of(b * step, QB) + (FRONT - PAD)
    out = []
    for part in (1, 2):
        a = qkv_ref[part, pl.ds(r0, PAD + step), :]
        zero = jnp.zeros_like(a)
        out.append([jnp.where(_head_mask(h), a, zero) for h in range(2)])
    return out


def _window(stacks, t):
    return jnp.concatenate([s[t * QB:t * QB + KW] for s in stacks], axis=0)


def _head_mask(h):
    lane = lax.broadcasted_iota(jnp.int32, (1, 2 * HEAD_DIM), 1)
    return (lane < HEAD_DIM) if h == 0 else (lane >= HEAD_DIM)


def _stack_heads(a):
    zero = jnp.zeros_like(a)
    return jnp.concatenate([jnp.where(_head_mask(0), a, zero), jnp.where(_head_mask(1), a, zero)], axis=0)


def _rows_by_head(a):
    return jnp.concatenate([a[:, :KW], a[:, KW:]], axis=0)


def _per_head(lo, hi):
    return jnp.where(_head_mask(0), lo, hi)


def _attn_exp(qs, kst, w_ref, kvalid):
    s = _dot(qs, kst, NT) + jnp.concatenate([w_ref[0], w_ref[1]], axis=1)
    if kvalid is not None:
        s = jnp.where(jnp.concatenate([kvalid, kvalid], axis=1), s, NEG)
    es, invs = [], []
    for h in range(2):
        sh = s[:, h * KW:(h + 1) * KW]
        eh = jnp.exp2(sh - jnp.max(sh, axis=-1, keepdims=True))
        es.append(eh)
        invs.append(1.0 / jnp.sum(eh, axis=-1, keepdims=True))
    return jnp.concatenate(es, axis=1), invs


ATTN_G = 16


def _blocks_per_step(T):
    return min(ATTN_G, T // QB)


def _masked_and_not(b, fn, step):
    n_masked = -(-PAD // step)
    pl.when(b < n_masked)(functools.partial(fn, True))
    pl.when(b >= n_masked)(functools.partial(fn, False))


def attn_fwd(qkvp, wb, name, comm=None):
    T = qkvp.shape[1] - FRONT
    G = _blocks_per_step(T)

    def body(qkv_ref, w_ref, o_ref):
        b = pl.program_id(1)

        def blocks(masked):
            keys, values = _step_windows(qkv_ref, b, G * QB)
            for t in range(G):
                _, qs, kvalid = _attn_block(qkv_ref, b * G + t, masked)
                e, inv = _attn_exp(qs, _window(keys, t), w_ref, kvalid)
                o = _dot(e.astype(BF16), _window(values, t), NN) * _per_head(*inv)
                o_ref[t * QB:(t + 1) * QB, :] = o.astype(BF16)

        _masked_and_not(b, blocks, G * QB)

    return _call(
        body, comm, 1, (qkvp, wb), name=name, grid=(N_HEADS // 2, T // (QB * G)),
        in_specs=[pl.BlockSpec((3, FRONT + T, 2 * HEAD_DIM), lambda hp, b: (0, 0, hp)),
                  pl.BlockSpec((2, QB, KW), lambda hp, b: (hp, 0, 0))],
        out_specs=pl.BlockSpec((QB * G, 2 * HEAD_DIM), lambda hp, b: (b, hp)),
        out_shape=_sds((T, D), BF16),
        compiler_params=_params(("parallel", "arbitrary"), VMEM_BIG))


def attn_bwd(qkvp, o, do, wb, name, comm=None):
    T = qkvp.shape[1] - FRONT
    G = _blocks_per_step(T)
    nb = T // (QB * G)

    def body(qkv_ref, o_ref, do_ref, w_ref, dqkv_ref, dw_ref, dk_acc, dv_acc):
        b = pl.program_id(1)

        @pl.when(b == 0)
        def _():
            dk_acc[...] = jnp.zeros_like(dk_acc)
            dv_acc[...] = jnp.zeros_like(dv_acc)
            dw_ref[...] = jnp.zeros_like(dw_ref)
            dqkv_ref[0, 0:FRONT, :] = jnp.zeros((FRONT, 2 * HEAD_DIM), BF16)

        def blocks(masked):
            dws = None
            keys, values = _step_windows(qkv_ref, b, G * QB)
            for t in range(G):
                r0, qs, kvalid = _attn_block(qkv_ref, b * G + t, masked)
                kst = _window(keys, t)
                e, inv = _attn_exp(qs, kst, w_ref, kvalid)
                do2 = do_ref[t * QB:(t + 1) * QB, :]
                dof = do2.astype(F32)
                prod = dof * o_ref[t * QB:(t + 1) * QB, :].astype(F32)
                dp = _dot(do2, _window(values, t), NT)
                parts = []
                for h in range(2):
                    delta = jnp.sum(jnp.where(_head_mask(h), prod, 0.0), axis=-1, keepdims=True)
                    half = slice(h * KW, (h + 1) * KW)
                    parts.append(e[:, half] * ((dp[:, half] - delta) * inv[h]))
                ds = jnp.concatenate(parts, axis=1)
                dws = ds if dws is None else dws + ds
                ds16 = ds.astype(BF16)
                dqkv_ref[0, pl.ds(r0 + FRONT, QB), :] = (_dot(ds16, kst, NN) * SCALE).astype(BF16)
                dk_acc[pl.ds(r0 + (FRONT - PAD), KW), :] += _dot(_rows_by_head(ds16), _stack_heads(qs), TN)
                dv_acc[pl.ds(r0 + (FRONT - PAD), KW), :] += _dot(
                    _rows_by_head(e.astype(BF16)), _stack_heads((dof * _per_head(*inv)).astype(BF16)), TN)
            dw_ref[0] += dws[:, :KW]
            dw_ref[1] += dws[:, KW:]

        _masked_and_not(b, blocks, G * QB)

        @pl.when(b == nb - 1)
        def _():
            dqkv_ref[1] = (dk_acc[...] * (1.0 / LOG2E)).astype(BF16)
            dqkv_ref[2] = dv_acc[...].astype(BF16)

    slab = pl.BlockSpec((3, FRONT + T, 2 * HEAD_DIM), lambda hp, b: (0, 0, hp))
    wspec = pl.BlockSpec((2, QB, KW), lambda hp, b: (hp, 0, 0))
    rows = pl.BlockSpec((QB * G, 2 * HEAD_DIM), lambda hp, b: (b, hp))
    return _call(
        body, comm, 2, (qkvp, o, do, wb), name=name, grid=(N_HEADS // 2, nb),
        in_specs=[slab, rows, rows, wspec],
        out_specs=[slab, wspec],
        out_shape=[_sds((3, FRONT + T, D), BF16), _sds((N_HEADS, QB, KW), F32)],
        scratch_shapes=[pltpu.VMEM((FRONT + T, 2 * HEAD_DIM), F32), pltpu.VMEM((FRONT + T, 2 * HEAD_DIM), F32)],
        compiler_params=_params(("parallel", "arbitrary"), VMEM_BIG))


def proj_qkv(hn, w, l, name, tm=512, comm=None):
    T = hn.shape[0]
    pb = FRONT // tm

    def body(a_ref, b_ref, o_ref):
        i = pl.program_id(1)

        @pl.when(i < pb)
        def _():
            o_ref[...] = jnp.zeros_like(o_ref)

        @pl.when(i >= pb)
        def _():
            scale = jnp.where(pl.program_id(0) == 0, Q_SCALE, 1.0).astype(F32)
            o_ref[...] = (_dot(a_ref[...], b_ref[...], NN) * scale).astype(BF16)

    return _call(
        body, comm, 1, (hn, w), name=name, grid=(3, pb + T // tm),
        in_specs=[pl.BlockSpec((tm, D), lambda p, i: (jnp.maximum(i - pb, 0), 0)),
                  pl.BlockSpec((None, D, D), lambda p, i: (l, 0, p))],
        out_specs=pl.BlockSpec((None, tm, D), lambda p, i: (p, i, 0)),
        out_shape=_sds((3, FRONT + T, D), BF16),
        compiler_params=_params(("parallel", "parallel"), VMEM_BIG))


def ffn_up(hn, wg, wu, l, name, tm=1024, comm=None):
    T = hn.shape[0]

    def body(a_ref, wg_ref, wu_ref, g_ref, u_ref, h_ref):
        a = a_ref[...]
        g = _dot(a, wg_ref[...], NT)
        u = _dot(a, wu_ref[...], NT)
        s = _sigmoid(g)
        silu = g * s
        g_ref[...] = (u * (s * (1.0 + g * (1.0 - s)))).astype(BF16)
        u_ref[...] = silu.astype(BF16)
        h_ref[...] = (silu * u).astype(BF16)

    wspec = pl.BlockSpec((None, None, FS, D), lambda s, i: (l, s, 0, 0))
    ospec = pl.BlockSpec((None, tm, FS), lambda s, i: (s, i, 0))
    return _call(
        body, comm, 3, (hn, wg, wu), name=name, grid=(N_CHIPS, T // tm),
        in_specs=[pl.BlockSpec((tm, D), lambda s, i: (i, 0)), wspec, wspec],
        out_specs=[ospec, ospec, ospec],
        out_shape=[_sds((N_CHIPS, T, FS), BF16)] * 3,
        compiler_params=_params(("parallel", "parallel"), VMEM_BIG))


def ffn_bwd_dh(dxb, wd, g, u, l, name, tm=2048, comm=None):
    T = dxb.shape[0]
    tm = min(tm, T)

    def body(a_ref, wd_ref, g_ref, u_ref, dg_ref, du_ref):
        dh = _dot(a_ref[...], wd_ref[...], NT)
        dg_ref[...] = (dh * g_ref[...].astype(F32)).astype(BF16)
        du_ref[...] = (dh * u_ref[...].astype(F32)).astype(BF16)

    aspec = pl.BlockSpec((None, tm, FS), lambda i, s: (s, i, 0))
    return _call(
        body, comm, 2, (dxb, wd, g, u), name=name, grid=(T // tm, N_CHIPS),
        in_specs=[pl.BlockSpec((tm, D), lambda i, s: (i, 0)),
                  pl.BlockSpec((None, None, FS, D), lambda i, s: (l, s, 0, 0)), aspec, aspec],
        out_specs=[aspec, aspec],
        out_shape=[_sds((N_CHIPS, T, FS), BF16)] * 2,
        compiler_params=_params(("parallel", "parallel"), VMEM_BIG))


def ffn_dgrad(dg, du, wg, wu, tm=512):
    def compute(dg_ref, du_ref, wg_ref, wu_ref):
        d = None
        for s in range(N_CHIPS):
            t = _dot(dg_ref[s], wg_ref[s], NN) + _dot(du_ref[s], wu_ref[s], NN)
            d = t if d is None else d + t
        return d

    aspec = pl.BlockSpec((N_CHIPS, tm, FS), lambda i: (0, i, 0))
    wspec = pl.BlockSpec((None, N_CHIPS, FS, D), lambda i: (0, 0, 0, 0), pipeline_mode=pl.Buffered(1))
    return compute, (dg, du, wg, wu), [aspec, aspec, wspec, wspec]


def qkv_dgrad(dqkvp, w, tm=512):
    def compute(a_ref, w_ref):
        d = None
        for p in range(3):
            t = _dot(a_ref[p], w_ref[:, p * D:(p + 1) * D], NT)
            d = t if d is None else d + t
        return d

    return compute, (dqkvp, w), [pl.BlockSpec((3, tm, D), lambda i: (0, i + FRONT // tm, 0)),
                                 pl.BlockSpec((None, D, 3 * D), lambda i: (0, 0, 0))]


def in_dgrad(dpre, w, tm=512):
    def compute(a_ref, w_ref):
        return _dot(a_ref[...], w_ref[...], NT)

    return compute, (dpre, w), [pl.BlockSpec((tm, 2 * GH), lambda i: (i, 0)),
                                pl.BlockSpec((None, D, 2 * GH), lambda i: (0, 0, 0))]


def _rms_rows(x, g):
    r = lax.rsqrt(jnp.mean(x * x, axis=-1, keepdims=True) + EPS)
    return ((x * r) * g).astype(BF16)


def residual_proj(name, compute, args, specs, res, norm_g, tm=512, comm=None):
    T = res.shape[0]
    k = len(args)
    with_norm = norm_g is not None

    def body(*refs):
        d = refs[k][...] + compute(*refs[:k])
        if with_norm:
            refs[k + 2][...] = d
            refs[k + 3][...] = _rms_rows(d, refs[k + 1][...])
        else:
            refs[k + 1][...] = d

    row = pl.BlockSpec((tm, D), lambda i: (i, 0))
    vec = pl.BlockSpec((1, D), lambda i: (0, 0))
    if with_norm:
        return _call(body, comm, 2, (*args, res, norm_g), name=name, grid=(T // tm,),
                     in_specs=list(specs) + [row, vec], out_specs=[row, row],
                     out_shape=[_sds((T, D), F32), _sds((T, D), BF16)],
                     compiler_params=_params(("parallel",), VMEM_BIG))
    return _call(body, comm, 1, (*args, res), name=name, grid=(T // tm,), in_specs=list(specs) + [row],
                 out_specs=row, out_shape=_sds((T, D), F32), compiler_params=_params(("parallel",), VMEM_BIG))


def ffn_down(h, wd, tm=512):
    def compute(h_ref, wd_ref):
        d = None
        for s in range(N_CHIPS):
            t = _dot(h_ref[s], wd_ref[s], NN)
            d = t if d is None else d + t
        return d

    return compute, (h, wd), [pl.BlockSpec((N_CHIPS, tm, FS), lambda i: (0, i, 0)),
                              pl.BlockSpec((None, N_CHIPS, FS, D), lambda i: (0, 0, 0, 0))]


def out_proj(a, w, tm=512):
    K = a.shape[1]

    def compute(a_ref, w_ref):
        return _dot(a_ref[...], w_ref[...], NN)

    return compute, (a, w), [pl.BlockSpec((tm, K), lambda i: (i, 0)), pl.BlockSpec((None, K, D), lambda i: (0, 0, 0))]


def adamw(w, g, m, v, name):
    L, R, C = w.shape

    def body(w_ref, g_ref, m_ref, v_ref, go_ref, d_ref, nm_ref, nv_ref):
        gf = g_ref[...]
        go_ref[...] = gf
        nm = ADAM_B1 * m_ref[...] + (1.0 - ADAM_B1) * gf
        nv = ADAM_B2 * v_ref[...] + (1.0 - ADAM_B2) * (gf * gf)
        m_hat = nm / (1.0 - ADAM_B1 ** ADAM_STEP)
        v_hat = nv / (1.0 - ADAM_B2 ** ADAM_STEP)
        d_ref[...] = -ADAM_LR * (m_hat / (jnp.sqrt(v_hat) + ADAM_EPS) + ADAM_WD * w_ref[...])
        nm_ref[...] = nm
        nv_ref[...] = nv

    tr = R // 4 if R % 32 == 0 else R
    spec = pl.BlockSpec((None, tr, C), lambda l, r: (l, r, 0))
    return _pallas(body, name=name, grid=(L, R // tr), in_specs=[spec] * 4, out_specs=[spec] * 4,
                   out_shape=[_sds((L, R, C), F32)] * 4,
                   compiler_params=_params(("parallel", "parallel")))(w, g, m, v)


def _coords():
    return lax.axis_index("x"), lax.axis_index("y"), lax.axis_index("c")


def _other_chips(x, y):
    out = []
    for fx, fy in ((1, 0), (0, 1), (1, 1)):
        px = (1 - x) if fx else x
        py = (1 - y) if fy else y
        out.append((px, py))
    return out


def _flip_index(s, j):
    sx, sy = s // 2, s % 2
    fx, fy = ((1, 0), (0, 1), (1, 1))[j]
    return 2 * (sx ^ fx) + (sy ^ fy)


def _for_my_chip(sme, fn):
    for s in range(N_CHIPS):
        pl.when(sme == s)(functools.partial(fn, s))


ANY = pl.BlockSpec(memory_space=pl.ANY)

GATHER_KIND = {"a_w_in": "col", "b_w_qkv": "col", "a_w_out": "row", "b_w_out": "row",
               "ffn_w_gate": "row", "ffn_w_up": "row", "ffn_w_down": "row"}
BIG = tuple(GATHER_KIND)


def _gathered_shape(kind, shape):
    L, R, C = shape
    return (L, R, N_CHIPS * C) if kind == "col" else (L, N_CHIPS, R, C)


def _shard_rows(ref, kind, s, r0, rn, C):
    if kind == "col":
        return ref.at[:, pl.ds(r0, rn), s * C:(s + 1) * C]
    return ref.at[:, s, pl.ds(r0, rn), :]


def gather_stage1(items):
    n = len(items)
    dims = [it[0].shape[1:] for it in items]

    def copies(ins, outs, sems, s, with_landed=True):
        lsem, ssem, rsem = sems
        x, y, c = _coords()
        chips = _other_chips(x, y)
        local, send, landed = [], [], []
        for t, (_, li, kind) in enumerate(items):
            R, C = dims[t]
            r0 = pl.multiple_of(c * (R // 2), 8)
            local.append(pltpu.make_async_copy(ins[t].at[pl.ds(li, 1)], _shard_rows(outs[t], kind, s, 0, R, C),
                                               lsem.at[t]))
            for j in range(3):
                pair = dict(send_sem=ssem.at[3 * t + j], recv_sem=rsem.at[3 * t + j],
                            device_id=(chips[j][0], chips[j][1], c), device_id_type=MESH)
                send.append(pltpu.make_async_remote_copy(
                    src_ref=ins[t].at[pl.ds(li, 1), pl.ds(r0, R // 2), :],
                    dst_ref=_shard_rows(outs[t], kind, s, r0, R // 2, C), **pair))
                if with_landed:
                    got = _shard_rows(outs[t], kind, _flip_index(s, j), r0, R // 2, C)
                    landed.append(pltpu.make_async_remote_copy(src_ref=got, dst_ref=got, **pair))
        return local, send, landed

    def start(ins, outs, sems):
        def run(s):
            local, send, _ = copies(ins, outs, sems, s, with_landed=False)
            for cp in local + send:
                cp.start()
        x, y, _ = _coords()
        _for_my_chip(2 * x + y, run)

    def wait(ins, outs, sems):
        def run(s):
            local, send, landed = copies(ins, outs, sems, s)
            for cp in landed:
                cp.wait_recv()
            for cp in send:
                cp.wait_send()
            for cp in local:
                cp.wait()
        x, y, _ = _coords()
        _for_my_chip(2 * x + y, run)

    out_shapes = [_sds(_gathered_shape(kind, (1,) + tuple(dims[t])), BF16) for t, (_, _, kind) in enumerate(items)]
    sems = [pltpu.SemaphoreType.DMA((n,)), pltpu.SemaphoreType.DMA((3 * n,)), pltpu.SemaphoreType.DMA((3 * n,))]
    return Comm([it[0] for it in items], out_shapes, sems, start, wait)


def gather_stage2(items, gathered):
    n = len(items)
    dims = [it[0].shape[1:] for it in items]

    def copies(outs, sems, s, with_landed=True):
        ssem, rsem = sems
        x, y, c = _coords()
        send, landed = [], []
        for t, (_, _, kind) in enumerate(items):
            R, C = dims[t]
            for j in range(3):
                pair = dict(send_sem=ssem.at[3 * t + j], recv_sem=rsem.at[3 * t + j],
                            device_id=(x, y, 1 - c), device_id_type=MESH)
                mine = _shard_rows(outs[t], kind, _flip_index(s, j), pl.multiple_of(c * (R // 2), 8), R // 2, C)
                send.append(pltpu.make_async_remote_copy(src_ref=mine, dst_ref=mine, **pair))
                if with_landed:
                    other = _shard_rows(outs[t], kind, _flip_index(s, j), pl.multiple_of((1 - c) * (R // 2), 8),
                                        R // 2, C)
                    landed.append(pltpu.make_async_remote_copy(src_ref=other, dst_ref=other, **pair))
        return send, landed

    def start(ins, outs, sems):
        def run(s):
            for cp in copies(outs, sems, s, with_landed=False)[0]:
                cp.start()
        x, y, _ = _coords()
        _for_my_chip(2 * x + y, run)

    def wait(ins, outs, sems):
        def run(s):
            send, landed = copies(outs, sems, s)
            for cp in landed:
                cp.wait_recv()
            for cp in send:
                cp.wait_send()
        x, y, _ = _coords()
        _for_my_chip(2 * x + y, run)

    out_shapes = [_sds(g.shape, BF16) for g in gathered]
    sems = [pltpu.SemaphoreType.DMA((3 * n,)), pltpu.SemaphoreType.DMA((3 * n,))]
    return Comm(gathered, out_shapes, sems, start, wait, aliases={t: t for t in range(n)})


def _half_shape(kind, R, C):
    return (R // 2, N_CHIPS * C) if kind == "col" else (N_CHIPS, R // 2, C)


def exchange_halves(grads, metas):
    n = len(grads)

    def copies(ins, outs, sems):
        ssem, rsem = sems
        x, y, c = _coords()
        out = []
        for t, (kind, R, C) in enumerate(metas):
            r0 = pl.multiple_of((1 - c) * (R // 2), 8)
            src = ins[t].at[pl.ds(r0, R // 2), :] if kind == "col" else ins[t].at[:, pl.ds(r0, R // 2), :]
            out.append(pltpu.make_async_remote_copy(
                src_ref=src, dst_ref=outs[t], send_sem=ssem.at[t], recv_sem=rsem.at[t],
                device_id=(x, y, 1 - c), device_id_type=MESH))
        return out

    def start(ins, outs, sems):
        for cp in copies(ins, outs, sems):
            cp.start()

    def wait(ins, outs, sems):
        for cp in copies(ins, outs, sems):
            cp.wait()

    return Comm(grads, [_sds(_half_shape(*m), F32) for m in metas], [pltpu.SemaphoreType.DMA((n,))] * 2, start, wait)


def pair_sum(me, g, sib, meta, name):
    kind, R, C = meta
    h = R // 2

    def body(me_ref, g_ref, sib_ref, p16_ref, own_ref):
        s = pl.program_id(0)
        v = g_ref[...] + sib_ref[...]
        p16_ref[...] = v.astype(BF16)

        @pl.when(s == me_ref[1])
        def _():
            own_ref[...] = v

    if kind == "col":
        gspec = pl.BlockSpec((h, C), lambda s, me_ref: (me_ref[0], s))
        sspec = pl.BlockSpec((h, C), lambda s, me_ref: (0, s))
    else:
        gspec = pl.BlockSpec((None, h, C), lambda s, me_ref: (s, me_ref[0], 0))
        sspec = pl.BlockSpec((None, h, C), lambda s, me_ref: (s, 0, 0))
    grid_spec = pltpu.PrefetchScalarGridSpec(
        num_scalar_prefetch=1, grid=(N_CHIPS,), in_specs=[gspec, sspec],
        out_specs=[sspec, pl.BlockSpec((h, C), lambda s, me_ref: (0, 0))])
    return _pallas(body, name=name, grid_spec=grid_spec,
                   out_shape=[_sds(_half_shape(*meta), BF16), _sds((h, C), F32)],
                   compiler_params=_params(("arbitrary",), VMEM_BIG))(me, g, sib)


def scatter_partials(p16s, metas):
    n = len(p16s)

    def copies(ins, outs, sems, s):
        ssem, rsem = sems
        x, y, c = _coords()
        chips = _other_chips(x, y)
        out = []
        for t, (kind, R, C) in enumerate(metas):
            for j in range(3):
                sj = _flip_index(s, j)
                src = ins[t].at[:, sj * C:(sj + 1) * C] if kind == "col" else ins[t].at[sj]
                out.append(pltpu.make_async_remote_copy(
                    src_ref=src, dst_ref=outs[t].at[j], send_sem=ssem.at[3 * t + j], recv_sem=rsem.at[3 * t + j],
                    device_id=(chips[j][0], chips[j][1], c), device_id_type=MESH))
        return out

    def start(ins, outs, sems):
        def run(s):
            for cp in copies(ins, outs, sems, s):
                cp.start()
        x, y, _ = _coords()
        _for_my_chip(2 * x + y, run)

    def wait(ins, outs, sems):
        def run(s):
            for cp in copies(ins, outs, sems, s):
                cp.wait()
        x, y, _ = _coords()
        _for_my_chip(2 * x + y, run)

    return Comm(p16s, [_sds((3, R // 2, C), BF16) for (_, R, C) in metas],
                [pltpu.SemaphoreType.DMA((3 * n,))] * 2, start, wait)


def final_sum(me, own, q, buf, shape, l, meta, name):
    _, R, C = meta
    h = R // 2

    def body(me_ref, own_ref, q_ref, *rest):
        rest[-1][...] = ((own_ref[...] + q_ref[0].astype(F32)) + q_ref[1].astype(F32)) + q_ref[2].astype(F32)

    grid_spec = pltpu.PrefetchScalarGridSpec(
        num_scalar_prefetch=1, grid=(1,),
        in_specs=[pl.BlockSpec((h, C), lambda i, me_ref: (0, 0)),
                  pl.BlockSpec((3, h, C), lambda i, me_ref: (0, 0, 0))] + ([] if buf is None else [ANY]),
        out_specs=pl.BlockSpec((None, h, C), lambda i, me_ref: (l, me_ref[0], 0)))
    alias = {} if buf is None else {"input_output_aliases": {3: 0}}
    args = (me, own, q) if buf is None else (me, own, q, buf)
    return _pallas(body, name=name, grid_spec=grid_spec, out_shape=_sds(shape, F32),
                   compiler_params=_params(("arbitrary",), VMEM_BIG), **alias)(*args)


def share_final(bufs):
    n = len(bufs)

    def body(*refs):
        ins, outs = refs[:n], refs[n:2 * n]
        ssem, rsem = refs[2 * n:]
        del ins
        x, y, c = _coords()
        copies = []
        for t in range(n):
            R = bufs[t].shape[1]
            r0 = pl.multiple_of(c * (R // 2), 8)
            blk = outs[t].at[:, pl.ds(r0, R // 2), :]
            copies.append(pltpu.make_async_remote_copy(
                src_ref=blk, dst_ref=blk, send_sem=ssem.at[t], recv_sem=rsem.at[t],
                device_id=(x, y, 1 - c), device_id_type=MESH))
        for cp in copies:
            cp.start()
        for t in range(n):
            R = bufs[t].shape[1]
            r1 = pl.multiple_of((1 - c) * (R // 2), 8)
            other = outs[t].at[:, pl.ds(r1, R // 2), :]
            pltpu.make_async_remote_copy(
                src_ref=other, dst_ref=other, send_sem=ssem.at[t], recv_sem=rsem.at[t],
                device_id=(x, y, 1 - c), device_id_type=MESH).wait_recv()
        for cp in copies:
            cp.wait_send()

    out_shape = [_sds(b.shape, F32) for b in bufs]
    return _pallas(body, name="share_final", in_specs=[ANY] * n, out_specs=[ANY] * n, out_shape=out_shape,
                   input_output_aliases={t: t for t in range(n)},
                   scratch_shapes=[pltpu.SemaphoreType.DMA((n,))] * 2,
                   compiler_params=pltpu.CompilerParams(has_side_effects=True))(*bufs)


def allreduce_small(part):
    rows = part.shape[0]
    h = rows // 2

    def body(p_ref, o_ref, sib_buf, pair_buf, chip_buf, ssem, rsem):
        x, y, c = _coords()
        sibling = dict(device_id=(x, y, 1 - c), device_id_type=MESH)
        mine = pl.ds(pl.multiple_of(c * h, 8), h)
        theirs = pl.ds(pl.multiple_of((1 - c) * h, 8), h)

        swap = pltpu.make_async_remote_copy(src_ref=p_ref.at[theirs], dst_ref=sib_buf, send_sem=ssem.at[0],
                                            recv_sem=rsem.at[0], **sibling)
        swap.start()
        swap.wait()
        pair_buf[...] = p_ref[mine, :] + sib_buf[...]

        chips = _other_chips(x, y)
        sends = [pltpu.make_async_remote_copy(src_ref=pair_buf, dst_ref=chip_buf.at[j], send_sem=ssem.at[1 + j],
                                              recv_sem=rsem.at[1 + j], device_id=(chips[j][0], chips[j][1], c),
                                              device_id_type=MESH) for j in range(3)]
        for cp in sends:
            cp.start()
        for cp in sends:
            cp.wait()

        def total(s):
            terms = {s: pair_buf[...]}
            for j in range(3):
                terms[_flip_index(s, j)] = chip_buf[j]
            o_ref[mine, :] = ((terms[0] + terms[1]) + terms[2]) + terms[3]

        _for_my_chip(2 * x + y, total)

        back = pltpu.make_async_remote_copy(src_ref=o_ref.at[mine], dst_ref=o_ref.at[mine], send_sem=ssem.at[4],
                                            recv_sem=rsem.at[4], **sibling)
        back.start()
        pltpu.make_async_remote_copy(src_ref=o_ref.at[theirs], dst_ref=o_ref.at[theirs], send_sem=ssem.at[4],
                                     recv_sem=rsem.at[4], **sibling).wait_recv()
        back.wait_send()

    return _pallas(body, name="allreduce_small",
                   in_specs=[pl.BlockSpec(memory_space=pltpu.VMEM)], out_specs=pl.BlockSpec(memory_space=pltpu.VMEM),
                   out_shape=_sds((rows, 128), F32),
                   scratch_shapes=[pltpu.VMEM((h, 128), F32), pltpu.VMEM((h, 128), F32), pltpu.VMEM((3, h, 128), F32),
                                   pltpu.SemaphoreType.DMA((5,)), pltpu.SemaphoreType.DMA((5,))],
                   compiler_params=pltpu.CompilerParams(has_side_effects=True))(part)


def _rows128(a):
    flat = a.reshape(-1)
    rows = -(-flat.shape[0] // 128)
    rows8 = -(-rows // 8) * 8
    flat = jnp.pad(flat, (0, rows8 * 128 - flat.shape[0]))
    return flat.reshape(rows8, 128)


def kernel(x, norm_mix_g, norm_ffn_g, final_g, a_w_in, a_v_gain, a_w_s, a_b_s, a_w_out, b_w_qkv, b_rel_bias, b_w_out, ffn_w_gate, ffn_w_up, ffn_w_down, loss_target, m_norm_mix_g, m_norm_ffn_g, m_final_g, m_a_w_in, m_a_v_gain, m_a_w_s, m_a_b_s, m_a_w_out, m_b_w_qkv, m_b_rel_bias, m_b_w_out, m_ffn_w_gate, m_ffn_w_up, m_ffn_w_down, v_norm_mix_g, v_norm_ffn_g, v_final_g, v_a_w_in, v_a_v_gain, v_a_w_s, v_a_b_s, v_a_w_out, v_b_w_qkv, v_b_rel_bias, v_b_w_out, v_ffn_w_gate, v_ffn_w_up, v_ffn_w_down):
    T = x.shape[1]
    weights = dict(norm_mix_g=norm_mix_g, norm_ffn_g=norm_ffn_g, final_g=final_g, a_w_in=a_w_in, a_v_gain=a_v_gain,
                   a_w_s=a_w_s, a_b_s=a_b_s, a_w_out=a_w_out, b_w_qkv=b_w_qkv, b_rel_bias=b_rel_bias,
                   b_w_out=b_w_out, ffn_w_gate=ffn_w_gate, ffn_w_up=ffn_w_up, ffn_w_down=ffn_w_down)
    mom_m = dict(norm_mix_g=m_norm_mix_g, norm_ffn_g=m_norm_ffn_g, final_g=m_final_g, a_w_in=m_a_w_in,
                 a_v_gain=m_a_v_gain, a_w_s=m_a_w_s, a_b_s=m_a_b_s, a_w_out=m_a_w_out, b_w_qkv=m_b_w_qkv,
                 b_rel_bias=m_b_rel_bias, b_w_out=m_b_w_out, ffn_w_gate=m_ffn_w_gate, ffn_w_up=m_ffn_w_up,
                 ffn_w_down=m_ffn_w_down)
    mom_v = dict(norm_mix_g=v_norm_mix_g, norm_ffn_g=v_norm_ffn_g, final_g=v_final_g, a_w_in=v_a_w_in,
                 a_v_gain=v_a_v_gain, a_w_s=v_a_w_s, a_b_s=v_a_b_s, a_w_out=v_a_w_out, b_w_qkv=v_b_w_qkv,
                 b_rel_bias=v_b_rel_bias, b_w_out=v_b_w_out, ffn_w_gate=v_ffn_w_gate, ffn_w_up=v_ffn_w_up,
                 ffn_w_down=v_ffn_w_down)
    order = list(weights)
    transposed = ("ffn_w_gate", "ffn_w_up")
    for k in transposed:
        weights[k], mom_m[k], mom_v[k] = (jnp.swapaxes(a, 1, 2) for a in (weights[k], mom_m[k], mom_v[k]))

    xi, yi, ci = _coords()
    me = jnp.stack([ci, 2 * xi + yi]).astype(jnp.int32)

    shard16 = {k: cast_bf16(weights[k], "cast_" + k) for k in BIG}

    def layer_tensors(i):
        mix = ("a_w_in", "a_w_out") if i % 2 == 0 else ("b_w_qkv", "b_w_out")
        return [(k, i // 2) for k in mix] + [(k, i) for k in ("ffn_w_gate", "ffn_w_up", "ffn_w_down")]

    def gather_items(keys):
        return [(shard16[k], l, GATHER_KIND[k]) for k, l in keys]

    def grad_metas(keys):
        return [(GATHER_KIND[k],) + tuple(weights[k].shape[1:]) for k, _ in keys]

    FFN = ("ffn_w_gate", "ffn_w_up", "ffn_w_down")
    k0a = [("a_w_out", 0), ("ffn_w_gate", 0)]
    k0b = [("ffn_w_up", 0), ("ffn_w_down", 0)]
    k1a = [("b_w_qkv", 0), ("b_w_out", 0), ("ffn_w_gate", 1)]
    k1b = [("ffn_w_up", 1), ("ffn_w_down", 1)]
    k3a = [("b_w_qkv", 1), ("b_w_out", 1), ("ffn_w_gate", 3)]
    k3b = [("ffn_w_up", 3), ("ffn_w_down", 3)]
    plans = {
        "a_in_l0": [("g1", k0a)], "sgu_fwd_l0": [("g2", k0a), ("g1", k0b)], "a_out_l0": [("g2", k0b)],
        "rms_mix_l0": [("g1", [("a_w_in", 0)])],
        "ffn_up_l0": [("g1", k1a)], "ffn_down_l0": [("g2", k1a), ("g1", k1b[:1])],
        "b_qkv_l1": [("g2", k1b[:1]), ("g1", k1b[1:])],
        "attn_fwd_l1": [("g2", k1b[1:]), ("g1", layer_tensors(2))], "b_out_l1": [("g2", layer_tensors(2))],
        "ffn_up_l1": [("g1", k3a)], "ffn_down_l1": [("g2", k3a)],
        "a_in_l2": [("g1", k3b)], "sgu_fwd_l2": [("g2", k3b)],
        "ffn_bwd_dhn_l2": [("ex", layer_tensors(3))], "sgu_bwd_l2": [("sc", layer_tensors(3))],
        "ffn_bwd_dhn_l1": [("ex", layer_tensors(2))], "attn_bwd_l1": [("sc", layer_tensors(2))],
        "dw_down_l0": [("ex", layer_tensors(1))], "ffn_bwd_dhn_l0": [("sc", k1a)],
        "dffn_w_gate_l0": [("sc", [("ffn_w_up", 1)])], "dffn_w_up_l0": [("sc", [("ffn_w_down", 1)])],
        "a_out_bwd_l0": [("ex", [(k, 0) for k in FFN])],
        "sgu_bwd_l0": [("sc", [("ffn_w_gate", 0), ("ffn_w_up", 0)]), ("ex", [("a_w_out", 0)])],
        "dw_in_l0": [("sc", [("ffn_w_down", 0), ("a_w_out", 0)])],
    }
    part16, full16 = {}, {}
    sib, p16, own_parts, recv_parts = {}, {}, {}, {}

    def make_comm(kind, keys):
        if kind == "g1":
            return gather_stage1(gather_items(keys)), lambda outs: part16.update(zip(keys, outs))
        if kind == "g2":
            return (gather_stage2(gather_items(keys), [part16[kl] for kl in keys]),
                    lambda outs: full16.update(zip(keys, outs)))
        if kind == "ex":
            return (exchange_halves([big_grads[k][l] for k, l in keys], grad_metas(keys)),
                    lambda outs: sib.update(zip(keys, outs)))
        for kl, m_ in zip(keys, grad_metas(keys)):
            p16[kl], own_parts[kl] = pair_sum(me, big_grads[kl[0]][kl[1]], sib[kl], m_, "pair_sum_%s_l%d" % kl)
        return (scatter_partials([p16[kl] for kl in keys], grad_metas(keys)),
                lambda outs: recv_parts.update(zip(keys, outs)))

    def run(name, make):
        steps = plans.get(name)
        if not steps:
            return make(None)
        made = [make_comm(kind, keys) for kind, keys in steps]
        main, outs = make(combine([c for c, _ in made]))
        for c, done in made:
            done(outs[:len(c.out_shapes)])
            outs = outs[len(c.out_shapes):]
        return main

    def weight(k, l):
        w = full16[(k, l)]
        if k == "a_w_out":
            return w.reshape(1, GH, D)
        return w.reshape(1, D, D) if k == "b_w_out" else w


    xcur = x.reshape(T, D)
    hn = run("rms_mix_l0", lambda comm: rms_fwd(xcur, norm_mix_g[0][None], "rms_mix_l0", comm=comm))
    comm, done = make_comm("g2", [("a_w_in", 0)])
    done(run_comm(comm, "gather_first_d2d"))
    saved = []
    for i in range(DEPTH):
        j = i // 2
        tag = "_l%d" % i
        st = {"x_in": xcur, "hn": hn}
        if i % 2 == 0:
            pre = run("a_in" + tag, lambda comm: matmul(
                "a_in" + tag, NN, hn, pl.BlockSpec((1024, D), lambda i_, j_: (i_, 0)),
                weight("a_w_in", j), pl.BlockSpec((None, D, 1024), lambda i_, j_: (0, 0, j_)),
                _sds((T, 2 * GH), BF16), pl.BlockSpec((1024, 1024), lambda i_, j_: (i_, j_)),
                (T // 1024, 4), comm=comm))
            y = run("sgu_fwd" + tag, lambda comm: sgu_fwd(
                pre, a_v_gain[j][None], a_w_s[j], a_b_s[j][:, :, None], "sgu_fwd" + tag, comm=comm))
            xmid, hn2 = run("a_out" + tag, lambda comm: residual_proj(
                "a_out" + tag, *out_proj(y, weight("a_w_out", j)), xcur, norm_ffn_g[i][None], comm=comm))
            st.update(pre=pre, y=y)
        else:
            qkvp = run("b_qkv" + tag, lambda comm: proj_qkv(hn, weight("b_w_qkv", j), 0, "b_qkv" + tag, comm=comm))
            wb = jnp.transpose(bias_build(b_rel_bias[j], "bias_build" + tag), (1, 0, 2))
            o = run("attn_fwd" + tag, lambda comm: attn_fwd(qkvp, wb, "attn_fwd" + tag, comm=comm))
            xmid, hn2 = run("b_out" + tag, lambda comm: residual_proj(
                "b_out" + tag, *out_proj(o, weight("b_w_out", j)), xcur, norm_ffn_g[i][None], comm=comm))
            st.update(qkvp=qkvp, wb=wb, o=o)
        g, u, h = run("ffn_up" + tag, lambda comm: ffn_up(
            hn2, weight("ffn_w_gate", i), weight("ffn_w_up", i), 0, "ffn_up" + tag, comm=comm))
        next_g = norm_mix_g[i + 1][None] if i + 1 < DEPTH else None
        down = run("ffn_down" + tag, lambda comm: residual_proj(
            "ffn_down" + tag, *ffn_down(h, weight("ffn_w_down", i)), xmid, next_g, comm=comm))
        xcur, hn = down if next_g is not None else (down, None)
        st.update(x_mid=xmid, hn2=hn2, g=g, u=u, h=h)
        saved.append(st)

    loss_part, dx, dxb, d_final = final_loss(xcur, final_g[None], loss_target.reshape(T, D), "final_loss")

    tk = min(2048, T)
    big_grads = {k: [None] * weights[k].shape[0] for k in BIG}
    small = {"norm_mix_g": [None] * DEPTH, "norm_ffn_g": [None] * DEPTH, "a_v_gain": [None] * 2,
             "a_w_s": [None] * 2, "a_b_s": [None] * 2, "b_rel_bias": [None] * 2}
    tok = lambda width: pl.BlockSpec((tk, width), lambda j_, k_: (k_, 0))
    part = lambda: pl.BlockSpec((None, tk, FS), lambda j_, k_: (j_, k_, 0))
    for i in reversed(range(DEPTH)):
        j = i // 2
        tag = "_l%d" % i
        st = saved[i]
        dg, du = run("ffn_bwd_dh" + tag, lambda comm: ffn_bwd_dh(
            dxb, weight("ffn_w_down", i), st["g"], st["u"], 0, "ffn_bwd_dh" + tag, comm=comm))
        big_grads["ffn_w_down"][i] = run("dw_down" + tag, lambda comm: wgrad(
            "dw_down" + tag, st["h"], part(), dxb, tok(D), _sds((N_CHIPS, FS, D), F32),
            pl.BlockSpec((None, FS, D), lambda j_, k_: (j_, 0, 0)), N_CHIPS, T, tk, comm=comm))
        dx_mid, dxb_mid, dgn = run("ffn_bwd_dhn" + tag, lambda comm: dgrad_rms(
            "ffn_bwd_dhn" + tag, *ffn_dgrad(dg, du, weight("ffn_w_gate", i), weight("ffn_w_up", i)),
            st["x_mid"], norm_ffn_g[i][None], dx, comm=comm))
        for nm, dz in (("ffn_w_gate", dg), ("ffn_w_up", du)):
            big_grads[nm][i] = run("d" + nm + tag, lambda comm: wgrad(
                "d" + nm + tag, dz, part(), st["hn2"], tok(D), _sds((N_CHIPS, FS, D), F32),
                pl.BlockSpec((None, FS, D), lambda j_, k_: (j_, 0, 0)), N_CHIPS, T, tk, comm=comm))
        dx, dxb = dx_mid, dxb_mid
        small["norm_ffn_g"][i] = dgn
        if i % 2 == 0:
            dy = run("a_out_bwd" + tag, lambda comm: matmul(
                "a_out_bwd" + tag, NT, dxb, pl.BlockSpec((1024, D), lambda i_, j_: (i_, 0)),
                weight("a_w_out", j), pl.BlockSpec((None, 1024, D), lambda i_, j_: (0, j_, 0)),
                _sds((T, GH), BF16), pl.BlockSpec((1024, 1024), lambda i_, j_: (i_, j_)), (T // 1024, 2), comm=comm))
            big_grads["a_w_out"][j] = wgrad(
                "dw_aout" + tag, st["y"], pl.BlockSpec((tk, 1024), lambda j_, k_: (k_, j_)), dxb, tok(D),
                _sds((GH, D), F32), pl.BlockSpec((1024, D), lambda j_, k_: (j_, 0)), 2, T, tk
            ).reshape(N_CHIPS, GH // N_CHIPS, D)
            dpre, d_ws, d_bs, d_gain = run("sgu_bwd" + tag, lambda comm: sgu_bwd(
                st["pre"], dy, a_v_gain[j][None], a_w_s[j], a_b_s[j][:, :, None], "sgu_bwd" + tag,
                tm=2 * SGU_BLOCK, comm=comm))
            small["a_w_s"][j], small["a_b_s"][j], small["a_v_gain"][j] = d_ws, d_bs, d_gain
            dx_in, dxb_in, dgn = run("a_in_bwd" + tag, lambda comm: dgrad_rms(
                "a_in_bwd" + tag, *in_dgrad(dpre, weight("a_w_in", j)),
                st["x_in"], norm_mix_g[i][None], dx, comm=comm))
            big_grads["a_w_in"][j] = run("dw_in" + tag, lambda comm: wgrad(
                "dw_in" + tag, st["hn"], tok(D), dpre, pl.BlockSpec((tk, 1024), lambda j_, k_: (k_, j_)),
                _sds((D, 2 * GH), F32), pl.BlockSpec((D, 1024), lambda j_, k_: (0, j_)), 4, T, tk, comm=comm))
        else:
            do = matmul("b_out_bwd" + tag, NT, dxb, pl.BlockSpec((1024, D), lambda i_, j_: (i_, 0)),
                        weight("b_w_out", j), pl.BlockSpec((None, D, D), lambda i_, j_: (0, 0, 0)),
                        _sds((T, D), BF16), pl.BlockSpec((1024, D), lambda i_, j_: (i_, 0)), (T // 1024, 1))
            big_grads["b_w_out"][j] = wgrad(
                "dw_bout" + tag, st["o"], tok(D), dxb, tok(D),
                _sds((D, D), F32), pl.BlockSpec((D, D), lambda j_, k_: (0, 0)), 1, T, tk
            ).reshape(N_CHIPS, D // N_CHIPS, D)
            dqkvp, dwb = run("attn_bwd" + tag, lambda comm: attn_bwd(
                st["qkvp"], st["o"], do, st["wb"], "attn_bwd" + tag, comm=comm))
            small["b_rel_bias"][j] = bias_grad(
                jnp.pad(jnp.transpose(dwb, (1, 0, 2)), ((0, 0), (0, 0), (0, DIAG - KW))), "bias_grad" + tag)
            dx_in, dxb_in, dgn = dgrad_rms(
                "b_qkv_bwd" + tag, *qkv_dgrad(dqkvp, weight("b_w_qkv", j)),
                st["x_in"], norm_mix_g[i][None], dx)
            big_grads["b_w_qkv"][j] = wgrad(
                "dw_qkv" + tag, st["hn"], tok(D), dqkvp,
                pl.BlockSpec((None, tk, D), lambda j_, k_: (j_, k_ + FRONT // tk, 0)),
                _sds((D, 3 * D), F32), pl.BlockSpec((D, D), lambda j_, k_: (0, j_)), 3, T, tk)
        dx, dxb = dx_in, dxb_in
        small["norm_mix_g"][i] = dgn

    small_grads = {
        "norm_mix_g": jnp.concatenate(small["norm_mix_g"], axis=0),
        "norm_ffn_g": jnp.concatenate(small["norm_ffn_g"], axis=0),
        "final_g": d_final.reshape(D),
        "a_v_gain": jnp.concatenate(small["a_v_gain"], axis=0),
        "a_w_s": jnp.stack(small["a_w_s"]),
        "a_b_s": jnp.stack(small["a_b_s"]).reshape(2, SGU_G, SGU_BLOCK),
        "b_rel_bias": jnp.stack(small["b_rel_bias"]),
    }
    small_names = list(small_grads)
    packed = [_rows128(small_grads[k]) for k in small_names] + [_rows128(loss_part[:, :1])]
    offs = [0]
    for p in packed:
        offs.append(offs[-1] + p.shape[0])
    reduced = allreduce_small(jnp.concatenate(packed, axis=0))
    grads = {}
    for t, k in enumerate(small_names):
        nelem = small_grads[k].size
        grads[k] = reduced[offs[t]:offs[t + 1]].reshape(-1)[:nelem].reshape(weights[k].shape)
    loss = reduced[offs[len(small_names)], 0]

    last = [("a_w_in", 0)]
    for kind, name in (("ex", "exchange_last"), ("sc", "scatter_last")):
        comm, done = make_comm(kind, last)
        done(run_comm(comm, name))
    bufs = {k: None for k in BIG}
    for i in range(DEPTH):
        for kl, m_ in zip(layer_tensors(i), grad_metas(layer_tensors(i))):
            bufs[kl[0]] = final_sum(me, own_parts[kl], recv_parts[kl], bufs[kl[0]], weights[kl[0]].shape, kl[1], m_,
                                    "final_sum_%s_l%d" % kl)
    shared = share_final([bufs[k] for k in BIG])
    for k, gfull in zip(BIG, shared):
        grads[k] = gfull

    delta, new_m, new_v = {}, {}, {}
    for k in order:
        shp = weights[k].shape
        if k in BIG:
            view = shp
        elif k == "a_w_s":
            view = (2, SGU_G * SGU_BLOCK, SGU_BLOCK)
        elif len(shp) == 1:
            view = (1, 1, shp[0])
        elif len(shp) == 2:
            view = (1,) + shp
        else:
            view = shp
        g_, d_, m_, v_ = adamw(weights[k].reshape(view), grads[k].reshape(view), mom_m[k].reshape(view),
                               mom_v[k].reshape(view), "adamw_" + k)
        grads[k], delta[k], new_m[k], new_v[k] = g_.reshape(shp), d_.reshape(shp), m_.reshape(shp), v_.reshape(shp)
    for k in transposed:
        for tree in (grads, delta, new_m, new_v):
            tree[k] = jnp.swapaxes(tree[k], 1, 2)

    return (loss, dx.reshape(1, T, D), *[grads[k] for k in order], *[delta[k] for k in order],
            *[new_m[k] for k in order], *[new_v[k] for k in order])
```

```python
import functools

import jax
import jax.numpy as jnp
from jax import lax
from jax.experimental import pallas as pl
from jax.experimental.pallas import tpu as pltpu

F32 = jnp.float32
BF16 = jnp.bfloat16
MESH = pl.DeviceIdType.MESH

D = 1024
DEPTH = 4
EPS = 1e-6
SGU_BLOCK = 128
GH = 2048
SGU_G = 8
SGU_GD = GH // SGU_G
N_HEADS = 16
HEAD_DIM = 64
CHUNK = 64
PAD = 8 * CHUNK
FRONT = 2048
QB = 128
KW = PAD + QB
N_REL = 192
REL_MIN = -(CHUNK - 1)
REL_MAX = 128
D_FF = 2816
FS = D_FF // 4
NEG = -1e30
SCALE = HEAD_DIM ** -0.5
N_CHIPS = 4

ADAM_LR = 0.001
ADAM_B1 = 0.9
ADAM_B2 = 0.999
ADAM_EPS = 1e-08
ADAM_WD = 0.01
ADAM_STEP = 10

VMEM_BIG = 56 * 1024 * 1024

NN = ((1,), (0,))
NT = ((1,), (1,))
TN = ((0,), (0,))


def _dot(a, b, dims):
    return lax.dot_general(a, b, (dims, ((), ())), preferred_element_type=F32)


class Comm:
    def __init__(self, ins, out_shapes, sems, start, wait, aliases=None):
        self.ins, self.out_shapes, self.sems = list(ins), list(out_shapes), list(sems)
        self.start, self.wait, self.aliases = start, wait, dict(aliases or {})


def _host(body, comm, kw):
    grid = tuple(kw["grid"])
    in_specs = list(kw["in_specs"])
    single = not isinstance(kw["out_specs"], (list, tuple))
    out_specs = [kw["out_specs"]] if single else list(kw["out_specs"])
    out_shape = [kw["out_shape"]] if single else list(kw["out_shape"])
    scratch = list(kw.get("scratch_shapes", ()))
    counts = (len(in_specs), len(comm.ins), len(out_specs), len(comm.out_shapes), len(scratch))

    def hosted(*refs):
        parts, p = [], 0
        for cnt in counts:
            parts.append(refs[p:p + cnt])
            p += cnt
        main_in, c_in, main_out, c_out, main_scr = parts
        sems = refs[p:]
        ids = [pl.program_id(a) for a in range(len(grid))]
        first = functools.reduce(jnp.logical_and, [i == 0 for i in ids])
        last = functools.reduce(jnp.logical_and, [i == n - 1 for i, n in zip(ids, grid)])
        pl.when(first)(lambda: comm.start(c_in, c_out, sems))
        body(*main_in, *main_out, *main_scr)
        pl.when(last)(lambda: comm.wait(c_in, c_out, sems))

    old = kw["compiler_params"]
    kw = dict(kw, in_specs=in_specs + [ANY] * len(comm.ins), out_specs=out_specs + [ANY] * len(comm.out_shapes),
              out_shape=out_shape + comm.out_shapes, scratch_shapes=scratch + comm.sems,
              compiler_params=pltpu.CompilerParams(dimension_semantics=("arbitrary",) * len(grid),
                                                   vmem_limit_bytes=old.vmem_limit_bytes, has_side_effects=True))
    if comm.aliases:
        kw["input_output_aliases"] = {counts[0] + i: counts[2] + o for i, o in comm.aliases.items()}
    return hosted, kw


def _pallas(body, comm=None, **kw):
    if comm is not None:
        body, kw = _host(body, comm, kw)
    return pl.pallas_call(body, **kw)


def _split_outs(outs, comm, n_main):
    outs = list(outs) if isinstance(outs, (list, tuple)) else [outs]
    main = outs[:n_main]
    return (main[0] if n_main == 1 else main), outs[n_main:]


def run_comm(comm, name):
    nci, nco = len(comm.ins), len(comm.out_shapes)

    def body(*refs):
        c_in, c_out, sems = refs[:nci], refs[nci:nci + nco], refs[nci + nco:]
        comm.start(c_in, c_out, sems)
        comm.wait(c_in, c_out, sems)

    kw = {}
    if comm.aliases:
        kw["input_output_aliases"] = dict(comm.aliases)
    return _pallas(body, name=name, in_specs=[ANY] * nci, out_specs=[ANY] * nco, out_shape=comm.out_shapes,
                   scratch_shapes=comm.sems, compiler_params=pltpu.CompilerParams(has_side_effects=True),
                   **kw)(*comm.ins)


def combine(comms):
    if len(comms) == 1:
        return comms[0]
    spans, ni, no, ns = [], 0, 0, 0
    for c in comms:
        spans.append((slice(ni, ni + len(c.ins)), slice(no, no + len(c.out_shapes)), slice(ns, ns + len(c.sems))))
        ni, no, ns = ni + len(c.ins), no + len(c.out_shapes), ns + len(c.sems)

    def start(ins, outs, sems):
        for c, (si, so, ss) in zip(comms, spans):
            c.start(ins[si], outs[so], sems[ss])

    def wait(ins, outs, sems):
        for c, (si, so, ss) in zip(comms, spans):
            c.wait(ins[si], outs[so], sems[ss])

    aliases = {}
    for c, (si, so, _) in zip(comms, spans):
        aliases.update({si.start + i: so.start + o for i, o in c.aliases.items()})
    return Comm([a for c in comms for a in c.ins], [o for c in comms for o in c.out_shapes],
                [s for c in comms for s in c.sems], start, wait, aliases)


def _call(body, comm, n_main, args, **kw):
    if comm is None:
        return _pallas(body, **kw)(*args)
    return _split_outs(_pallas(body, comm=comm, **kw)(*args, *comm.ins), comm, n_main)


def _params(sem=None, vmem=None):
    return pltpu.CompilerParams(dimension_semantics=sem, vmem_limit_bytes=vmem)


def _sds(shape, dtype):
    return jax.ShapeDtypeStruct(tuple(shape), dtype)


_GELU_C = 0.7978845608028654


_GELU_A = _GELU_C * 0.044715


def _gelu(x):
    t = jnp.tanh(x * (_GELU_C + _GELU_A * (x * x)))
    h = 0.5 * x
    return h + h * t


def _gelu_and_grad(x):
    x2 = x * x
    t = jnp.tanh(x * (_GELU_C + _GELU_A * x2))
    h = 0.5 * x
    val = h + h * t
    grad = (0.5 + 0.5 * t) + (h * (1.0 - t * t)) * (_GELU_C + (3.0 * _GELU_A) * x2)
    return val, grad


def _sigmoid(x):
    return 0.5 * (jnp.tanh(0.5 * x) + 1.0)


def cast_bf16(w, name):
    L, R, C = w.shape

    def body(w_ref, o_ref):
        o_ref[...] = w_ref[...].astype(BF16)

    spec = pl.BlockSpec((None, R, C), lambda l: (l, 0, 0))
    return _pallas(body, name=name, grid=(L,), in_specs=[spec], out_specs=spec,
                   out_shape=_sds((L, R, C), BF16), compiler_params=_params(("parallel",)))(w)


def rms_fwd(x, g, name, tm=512, comm=None):
    T = x.shape[0]

    def body(x_ref, g_ref, o_ref):
        o_ref[...] = _rms_rows(x_ref[...], g_ref[...])

    row = pl.BlockSpec((tm, D), lambda i: (i, 0))
    return _call(body, comm, 1, (x, g), name=name, grid=(T // tm,),
                 in_specs=[row, pl.BlockSpec((1, D), lambda i: (0, 0))], out_specs=row,
                 out_shape=_sds((T, D), BF16), compiler_params=_params(("parallel",)))


def dgrad_rms(name, compute, args, specs, x, g, dres, tm=512, comm=None):
    T = x.shape[0]
    n = T // tm
    k = len(args)

    def body(*refs):
        x_ref, g_ref, dres_ref, dx_ref, dxb_ref, dg_ref, acc_ref = refs[k:]
        i = pl.program_id(0)
        xf = x_ref[...]
        r = lax.rsqrt(jnp.mean(xf * xf, axis=-1, keepdims=True) + EPS)
        xhat = xf * r
        dhf = compute(*refs[:k])
        part = (dhf * xhat).reshape(tm // 8, 8, D).sum(axis=0)

        @pl.when(i == 0)
        def _():
            acc_ref[...] = part

        @pl.when(i > 0)
        def _():
            acc_ref[...] += part

        dxhat = dhf * g_ref[...]
        dx = dres_ref[...] + r * (dxhat - xhat * jnp.mean(dxhat * xhat, axis=-1, keepdims=True))
        dx_ref[...] = dx
        dxb_ref[...] = dx.astype(BF16)

        @pl.when(i == n - 1)
        def _():
            dg_ref[...] = jnp.sum(acc_ref[...], axis=0, keepdims=True)

    row = pl.BlockSpec((tm, D), lambda i: (i, 0))
    vec = pl.BlockSpec((1, D), lambda i: (0, 0))
    return _call(body, comm, 3, (*args, x, g, dres), name=name, grid=(n,),
                 in_specs=list(specs) + [row, vec, row], out_specs=[row, row, vec],
                 out_shape=[_sds((T, D), F32), _sds((T, D), BF16), _sds((1, D), F32)],
                 scratch_shapes=[pltpu.VMEM((8, D), F32)],
                 compiler_params=_params(("arbitrary",), VMEM_BIG))


def final_loss(x, g, tgt, name, tm=256):
    T = x.shape[0]
    n = T // tm

    def body(x_ref, g_ref, t_ref, loss_ref, dx_ref, dxb_ref, dg_ref, acc_ref, lacc_ref):
        i = pl.program_id(0)
        xf = x_ref[...]
        r = lax.rsqrt(jnp.mean(xf * xf, axis=-1, keepdims=True) + EPS)
        xhat = xf * r
        gg = g_ref[...]
        e = xhat * gg - t_ref[...]
        dy = e * (1.0 / D)
        part = (dy * xhat).reshape(tm // 8, 8, D).sum(axis=0)
        lpart = (e * e).reshape(tm // 8, 8, D).sum(axis=0)

        @pl.when(i == 0)
        def _():
            acc_ref[...] = part
            lacc_ref[...] = lpart

        @pl.when(i > 0)
        def _():
            acc_ref[...] += part
            lacc_ref[...] += lpart

        dxhat = dy * gg
        dx = r * (dxhat - xhat * jnp.mean(dxhat * xhat, axis=-1, keepdims=True))
        dx_ref[...] = dx
        dxb_ref[...] = dx.astype(BF16)

        @pl.when(i == n - 1)
        def _():
            dg_ref[...] = jnp.sum(acc_ref[...], axis=0, keepdims=True)
            total = jnp.sum(jnp.sum(lacc_ref[...], axis=0, keepdims=True), axis=1, keepdims=True)
            loss_ref[...] = jnp.broadcast_to(total * (0.5 / D), (1, 128))

    row = pl.BlockSpec((tm, D), lambda i: (i, 0))
    vec = pl.BlockSpec((1, D), lambda i: (0, 0))
    return _pallas(body, name=name, grid=(n,), in_specs=[row, vec, row],
                   out_specs=[pl.BlockSpec((1, 128), lambda i: (0, 0)), row, row, vec],
                   out_shape=[_sds((1, 128), F32), _sds((T, D), F32), _sds((T, D), BF16), _sds((1, D), F32)],
                   scratch_shapes=[pltpu.VMEM((8, D), F32), pltpu.VMEM((8, D), F32)],
                   compiler_params=_params(("arbitrary",)))(x, g, tgt)


def matmul(name, dims, a, a_spec, b, b_spec, out_shape, out_spec, grid, *, acc=False, res=None, res_spec=None,
           comm=None):
    has_res = res is not None

    def body(*refs):
        a_ref, b_ref = refs[0], refs[1]
        r_ref = refs[2] if has_res else None
        o_ref = refs[-1]
        d = _dot(a_ref[...], b_ref[...], dims)
        if not acc:
            if has_res:
                d = d + r_ref[...]
            o_ref[...] = d.astype(o_ref.dtype)
        else:
            k = pl.program_id(len(grid) - 1)

            @pl.when(k == 0)
            def _():
                o_ref[...] = (d + r_ref[...]) if has_res else d

            @pl.when(k > 0)
            def _():
                o_ref[...] += d

    sem = ("parallel",) * (len(grid) - 1) + (("arbitrary",) if acc else ("parallel",))
    ins = [a, b] + ([res] if has_res else [])
    specs = [a_spec, b_spec] + ([res_spec] if has_res else [])
    return _call(body, comm, 1, ins, name=name, grid=grid, in_specs=specs, out_specs=out_spec, out_shape=out_shape,
                 compiler_params=_params(sem, VMEM_BIG))


def wgrad(name, a, a_spec, b, b_spec, out_shape, out_spec, J, T, tk, comm=None):
    return matmul(name, TN, a, a_spec, b, b_spec, out_shape, out_spec, (J, T // tk), acc=True, comm=comm)


def _sgu_mask():
    p = lax.broadcasted_iota(jnp.int32, (SGU_BLOCK, SGU_BLOCK), 0)
    q = lax.broadcasted_iota(jnp.int32, (SGU_BLOCK, SGU_BLOCK), 1)
    return lax.shift_right_logical(q, 6) <= lax.shift_right_logical(p, 6)


def sgu_fwd(pre, gain, w_s, b_s, name, tm=4 * SGU_BLOCK, comm=None):
    T = pre.shape[0]
    tm = min(tm, T)

    def body(pre_ref, gain_ref, ws_ref, bs_ref, y_ref):
        mask = _sgu_mask()
        gain_v = gain_ref[...]
        for sb in range(tm // SGU_BLOCK):
            rows = slice(sb * SGU_BLOCK, (sb + 1) * SGU_BLOCK)
            u = _gelu(pre_ref[rows, :GH].astype(F32))
            va = _gelu(pre_ref[rows, GH:].astype(F32))
            r = lax.rsqrt(jnp.mean(va * va, axis=-1, keepdims=True) + EPS)
            vn = ((va * r) * gain_v).astype(BF16)
            for g in range(SGU_G):
                sl = slice(g * SGU_GD, (g + 1) * SGU_GD)
                wm = jnp.where(mask, ws_ref[g], 0.0).astype(BF16)
                vm = _dot(wm, vn[:, sl], NN) + bs_ref[g]
                y_ref[rows, sl] = (u[:, sl] * vm).astype(BF16)

    return _call(
        body, comm, 1, (pre, gain, w_s, b_s), name=name, grid=(T // tm,),
        in_specs=[pl.BlockSpec((tm, 2 * GH), lambda i: (i, 0)),
                  pl.BlockSpec((1, GH), lambda i: (0, 0)),
                  pl.BlockSpec((SGU_G, SGU_BLOCK, SGU_BLOCK), lambda i: (0, 0, 0)),
                  pl.BlockSpec((SGU_G, SGU_BLOCK, 1), lambda i: (0, 0, 0))],
        out_specs=pl.BlockSpec((tm, GH), lambda i: (i, 0)),
        out_shape=_sds((T, GH), BF16), compiler_params=_params(("parallel",)))


def sgu_bwd(pre, dy, gain, w_s, b_s, name, tm=SGU_BLOCK, comm=None):
    T = pre.shape[0]
    n = T // tm

    def body(pre_ref, dy_ref, gain_ref, ws_ref, bs_ref, dpre_ref, dws_ref, dbs_ref, dgain_ref, gacc_ref):
        i = pl.program_id(0)

        @pl.when(i == 0)
        def _():
            dws_ref[...] = jnp.zeros_like(dws_ref)
            dbs_ref[...] = jnp.zeros_like(dbs_ref)
            gacc_ref[...] = jnp.zeros_like(gacc_ref)

        mask = _sgu_mask()
        gain_v = gain_ref[...]
        for sb in range(tm // SGU_BLOCK):
            rows = slice(sb * SGU_BLOCK, (sb + 1) * SGU_BLOCK)
            u, du_dpre = _gelu_and_grad(pre_ref[rows, :GH].astype(F32))
            va, dva_dpre = _gelu_and_grad(pre_ref[rows, GH:].astype(F32))
            r = lax.rsqrt(jnp.mean(va * va, axis=-1, keepdims=True) + EPS)
            vhat = va * r
            vn = (vhat * gain_v).astype(BF16)
            dyf = dy_ref[rows, :].astype(F32)
            dvn_parts = []
            for grp in range(SGU_G):
                sl = slice(grp * SGU_GD, (grp + 1) * SGU_GD)
                wm = jnp.where(mask, ws_ref[grp], 0.0).astype(BF16)
                vm = _dot(wm, vn[:, sl], NN) + bs_ref[grp]
                dpre_ref[rows, sl] = ((dyf[:, sl] * vm) * du_dpre[:, sl]).astype(BF16)
                dvm = dyf[:, sl] * u[:, sl]
                dbs_ref[grp] += jnp.sum(dvm, axis=-1, keepdims=True)
                dvm16 = dvm.astype(BF16)
                dws_ref[grp] += jnp.where(mask, _dot(dvm16, vn[:, sl], NT), 0.0)
                dvn_parts.append(_dot(wm, dvm16, TN))
            dvn = jnp.concatenate(dvn_parts, axis=-1)
            gacc_ref[...] += (dvn * vhat).reshape(SGU_BLOCK // 8, 8, GH).sum(axis=0)
            dvhat = dvn * gain_v
            dva = r * (dvhat - vhat * jnp.mean(dvhat * vhat, axis=-1, keepdims=True))
            dpre_ref[rows, GH:] = (dva * dva_dpre).astype(BF16)

        @pl.when(i == n - 1)
        def _():
            dgain_ref[...] = jnp.sum(gacc_ref[...], axis=0, keepdims=True)

    const3 = lambda i: (0, 0, 0)
    return _call(
        body, comm, 4, (pre, dy, gain, w_s, b_s), name=name, grid=(n,),
        in_specs=[pl.BlockSpec((tm, 2 * GH), lambda i: (i, 0)),
                  pl.BlockSpec((tm, GH), lambda i: (i, 0)),
                  pl.BlockSpec((1, GH), lambda i: (0, 0)),
                  pl.BlockSpec((SGU_G, SGU_BLOCK, SGU_BLOCK), const3),
                  pl.BlockSpec((SGU_G, SGU_BLOCK, 1), const3)],
        out_specs=[pl.BlockSpec((tm, 2 * GH), lambda i: (i, 0)),
                   pl.BlockSpec((SGU_G, SGU_BLOCK, SGU_BLOCK), const3),
                   pl.BlockSpec((SGU_G, SGU_BLOCK, 1), const3),
                   pl.BlockSpec((1, GH), lambda i: (0, 0))],
        out_shape=[_sds((T, 2 * GH), BF16), _sds((SGU_G, SGU_BLOCK, SGU_BLOCK), F32),
                   _sds((SGU_G, SGU_BLOCK, 1), F32), _sds((1, GH), F32)],
        scratch_shapes=[pltpu.VMEM((8, GH), F32)],
        compiler_params=_params(("arbitrary",)))


DIAG = 768


def _diag_onehot():
    n = lax.broadcasted_iota(jnp.int32, (N_REL, DIAG), 1)
    r = lax.broadcasted_iota(jnp.int32, (N_REL, DIAG), 0)
    idx = jnp.clip(KW - 1 - n, REL_MIN, REL_MAX) - REL_MIN
    return (idx == r).astype(BF16)


def _split3(v):
    hi = v.astype(BF16)
    r1 = v - hi.astype(F32)
    mid = r1.astype(BF16)
    lo = (r1 - mid.astype(F32)).astype(BF16)
    return hi, mid, lo


def bias_build(rel_bias, name):
    def body(rb_ref, o_ref):
        oh = _diag_onehot()
        hi, mid, lo = _split3(rb_ref[...])
        u = (_dot(hi, oh, NN) + _dot(mid, oh, NN) + _dot(lo, oh, NN)) * LOG2E
        j = lax.broadcasted_iota(jnp.int32, (1, KW), 1)

        def row(i, carry):
            val = pltpu.roll(u, (i + (DIAG - QB + 1)) % DIAG, 1)[:, :KW]
            rel = lax.shift_right_logical(i, 6) - lax.shift_right_logical(j, 6) + 8
            ok = (rel >= 0) & (rel <= 8)
            o_ref[i] = jnp.where(ok, val, NEG)
            return carry

        lax.fori_loop(0, QB, row, 0)

    return _pallas(body, name=name, out_shape=_sds((QB, N_HEADS, KW), F32),
                   in_specs=[pl.BlockSpec(memory_space=pltpu.VMEM)],
                   out_specs=pl.BlockSpec(memory_space=pltpu.VMEM))(rel_bias)


def bias_grad(dwb, name):
    def body(d_ref, o_ref):
        def row(i, acc):
            return acc + pltpu.roll(d_ref[i], QB - 1 - i, 1)

        du = lax.fori_loop(0, QB, row, jnp.zeros((N_HEADS, DIAG), F32))
        oh = _diag_onehot()
        hi, mid, lo = _split3(du)
        o_ref[...] = _dot(hi, oh, NT) + _dot(mid, oh, NT) + _dot(lo, oh, NT)

    return _pallas(body, name=name, out_shape=_sds((N_HEADS, N_REL), F32),
                   in_specs=[pl.BlockSpec(memory_space=pltpu.VMEM)],
                   out_specs=pl.BlockSpec(memory_space=pltpu.VMEM))(dwb)


LOG2E = 1.4426950408889634
Q_SCALE = SCALE * LOG2E


def _attn_block(qkv_ref, blk, masked):
    r0 = pl.multiple_of(blk * QB, QB)
    qs = qkv_ref[0, pl.ds(r0 + FRONT, QB), :]
    kvalid = (lax.broadcasted_iota(jnp.int32, (1, KW), 1) >= PAD - blk * QB) if masked else None
    return r0, qs, kvalid


def _step_windows(qkv_ref, b, step):
    r0 = pl.multiple_of(b * step, QB) + (FRONT - PAD)
    out = []
    for part in (1, 2):
        a = qkv_ref[part, pl.ds(r0, PAD + step), :]
        zero = jnp.zeros_like(a)
        out.append([jnp.where(_head_mask(h), a, zero) for h in range(2)])
    return out


def _window(stacks, t):
    return jnp.concatenate([s[t * QB:t * QB + KW] for s in stacks], axis=0)


def _head_mask(h):
    lane = lax.broadcasted_iota(jnp.int32, (1, 2 * HEAD_DIM), 1)
    return (lane < HEAD_DIM) if h == 0 else (lane >= HEAD_DIM)


def _stack_heads(a):
    zero = jnp.zeros_like(a)
    return jnp.concatenate([jnp.where(_head_mask(0), a, zero), jnp.where(_head_mask(1), a, zero)], axis=0)


def _rows_by_head(a):
    return jnp.concatenate([a[:, :KW], a[:, KW:]], axis=0)


def _per_head(lo, hi):
    return jnp.where(_head_mask(0), lo, hi)


def _attn_exp(qs, kst, w_ref, kvalid):
    s = _dot(qs, kst, NT) + jnp.concatenate([w_ref[0], w_ref[1]], axis=1)
    if kvalid is not None:
        s = jnp.where(jnp.concatenate([kvalid, kvalid], axis=1), s, NEG)
    es, invs = [], []
    for h in range(2):
        sh = s[:, h * KW:(h + 1) * KW]
        eh = jnp.exp2(sh - jnp.max(sh, axis=-1, keepdims=True))
        es.append(eh)
        invs.append(1.0 / jnp.sum(eh, axis=-1, keepdims=True))
    return jnp.concatenate(es, axis=1), invs


ATTN_G = 16


def _blocks_per_step(T):
    return min(ATTN_G, T // QB)


def _masked_and_not(b, fn, step):
    n_masked = -(-PAD // step)
    pl.when(b < n_masked)(functools.partial(fn, True))
    pl.when(b >= n_masked)(functools.partial(fn, False))


def attn_fwd(qkvp, wb, name, comm=None):
    T = qkvp.shape[1] - FRONT
    G = _blocks_per_step(T)

    def body(qkv_ref, w_ref, o_ref):
        b = pl.program_id(1)

        def blocks(masked):
            keys, values = _step_windows(qkv_ref, b, G * QB)
            for t in range(G):
                _, qs, kvalid = _attn_block(qkv_ref, b * G + t, masked)
                e, inv = _attn_exp(qs, _window(keys, t), w_ref, kvalid)
                o = _dot(e.astype(BF16), _window(values, t), NN) * _per_head(*inv)
                o_ref[t * QB:(t + 1) * QB, :] = o.astype(BF16)

        _masked_and_not(b, blocks, G * QB)

    return _call(
        body, comm, 1, (qkvp, wb), name=name, grid=(N_HEADS // 2, T // (QB * G)),
        in_specs=[pl.BlockSpec((3, FRONT + T, 2 * HEAD_DIM), lambda hp, b: (0, 0, hp)),
                  pl.BlockSpec((2, QB, KW), lambda hp, b: (hp, 0, 0))],
        out_specs=pl.BlockSpec((QB * G, 2 * HEAD_DIM), lambda hp, b: (b, hp)),
        out_shape=_sds((T, D), BF16),
        compiler_params=_params(("parallel", "arbitrary"), VMEM_BIG))


def attn_bwd(qkvp, o, do, wb, name, comm=None):
    T = qkvp.shape[1] - FRONT
    G = _blocks_per_step(T)
    nb = T // (QB * G)

    def body(qkv_ref, o_ref, do_ref, w_ref, dqkv_ref, dw_ref, dk_acc, dv_acc):
        b = pl.program_id(1)

        @pl.when(b == 0)
        def _():
            dk_acc[...] = jnp.zeros_like(dk_acc)
            dv_acc[...] = jnp.zeros_like(dv_acc)
            dw_ref[...] = jnp.zeros_like(dw_ref)
            dqkv_ref[0, 0:FRONT, :] = jnp.zeros((FRONT, 2 * HEAD_DIM), BF16)

        def blocks(masked):
            dws = None
            keys, values = _step_windows(qkv_ref, b, G * QB)
            for t in range(G):
                r0, qs, kvalid = _attn_block(qkv_ref, b * G + t, masked)
                kst = _window(keys, t)
                e, inv = _attn_exp(qs, kst, w_ref, kvalid)
                do2 = do_ref[t * QB:(t + 1) * QB, :]
                dof = do2.astype(F32)
                prod = dof * o_ref[t * QB:(t + 1) * QB, :].astype(F32)
                dp = _dot(do2, _window(values, t), NT)
                parts = []
                for h in range(2):
                    delta = jnp.sum(jnp.where(_head_mask(h), prod, 0.0), axis=-1, keepdims=True)
                    half = slice(h * KW, (h + 1) * KW)
                    parts.append(e[:, half] * ((dp[:, half] - delta) * inv[h]))
                ds = jnp.concatenate(parts, axis=1)
                dws = ds if dws is None else dws + ds
                ds16 = ds.astype(BF16)
                dqkv_ref[0, pl.ds(r0 + FRONT, QB), :] = (_dot(ds16, kst, NN) * SCALE).astype(BF16)
                dk_acc[pl.ds(r0 + (FRONT - PAD), KW), :] += _dot(_rows_by_head(ds16), _stack_heads(qs), TN)
                dv_acc[pl.ds(r0 + (FRONT - PAD), KW), :] += _dot(
                    _rows_by_head(e.astype(BF16)), _stack_heads((dof * _per_head(*inv)).astype(BF16)), TN)
            dw_ref[0] += dws[:, :KW]
            dw_ref[1] += dws[:, KW:]

        _masked_and_not(b, blocks, G * QB)

        @pl.when(b == nb - 1)
        def _():
            dqkv_ref[1] = (dk_acc[...] * (1.0 / LOG2E)).astype(BF16)
            dqkv_ref[2] = dv_acc[...].astype(BF16)

    slab = pl.BlockSpec((3, FRONT + T, 2 * HEAD_DIM), lambda hp, b: (0, 0, hp))
    wspec = pl.BlockSpec((2, QB, KW), lambda hp, b: (hp, 0, 0))
    rows = pl.BlockSpec((QB * G, 2 * HEAD_DIM), lambda hp, b: (b, hp))
    return _call(
        body, comm, 2, (qkvp, o, do, wb), name=name, grid=(N_HEADS // 2, nb),
        in_specs=[slab, rows, rows, wspec],
        out_specs=[slab, wspec],
        out_shape=[_sds((3, FRONT + T, D), BF16), _sds((N_HEADS, QB, KW), F32)],
        scratch_shapes=[pltpu.VMEM((FRONT + T, 2 * HEAD_DIM), F32), pltpu.VMEM((FRONT + T, 2 * HEAD_DIM), F32)],
        compiler_params=_params(("parallel", "arbitrary"), VMEM_BIG))


def proj_qkv(hn, w, l, name, tm=512, comm=None):
    T = hn.shape[0]
    pb = FRONT // tm

    def body(a_ref, b_ref, o_ref):
        i = pl.program_id(1)

        @pl.when(i < pb)
        def _():
            o_ref[...] = jnp.zeros_like(o_ref)

        @pl.when(i >= pb)
        def _():
            scale = jnp.where(pl.program_id(0) == 0, Q_SCALE, 1.0).astype(F32)
            o_ref[...] = (_dot(a_ref[...], b_ref[...], NN) * scale).astype(BF16)

    return _call(
        body, comm, 1, (hn, w), name=name, grid=(3, pb + T // tm),
        in_specs=[pl.BlockSpec((tm, D), lambda p, i: (jnp.maximum(i - pb, 0), 0)),
                  pl.BlockSpec((None, D, D), lambda p, i: (l, 0, p))],
        out_specs=pl.BlockSpec((None, tm, D), lambda p, i: (p, i, 0)),
        out_shape=_sds((3, FRONT + T, D), BF16),
        compiler_params=_params(("parallel", "parallel"), VMEM_BIG))


def ffn_up(hn, wg, wu, l, name, tm=1024, comm=None):
    T = hn.shape[0]

    def body(a_ref, wg_ref, wu_ref, g_ref, u_ref, h_ref):
        a = a_ref[...]
        g = _dot(a, wg_ref[...], NT)
        u = _dot(a, wu_ref[...], NT)
        s = _sigmoid(g)
        silu = g * s
        g_ref[...] = (u * (s * (1.0 + g * (1.0 - s)))).astype(BF16)
        u_ref[...] = silu.astype(BF16)
        h_ref[...] = (silu * u).astype(BF16)

    wspec = pl.BlockSpec((None, None, FS, D), lambda s, i: (l, s, 0, 0))
    ospec = pl.BlockSpec((None, tm, FS), lambda s, i: (s, i, 0))
    return _call(
        body, comm, 3, (hn, wg, wu), name=name, grid=(N_CHIPS, T // tm),
        in_specs=[pl.BlockSpec((tm, D), lambda s, i: (i, 0)), wspec, wspec],
        out_specs=[ospec, ospec, ospec],
        out_shape=[_sds((N_CHIPS, T, FS), BF16)] * 3,
        compiler_params=_params(("parallel", "parallel"), VMEM_BIG))


def ffn_bwd_dh(dxb, wd, g, u, l, name, tm=2048, comm=None):
    T = dxb.shape[0]
    tm = min(tm, T)

    def body(a_ref, wd_ref, g_ref, u_ref, dg_ref, du_ref):
        dh = _dot(a_ref[...], wd_ref[...], NT)
        dg_ref[...] = (dh * g_ref[...].astype(F32)).astype(BF16)
        du_ref[...] = (dh * u_ref[...].astype(F32)).astype(BF16)

    aspec = pl.BlockSpec((None, tm, FS), lambda i, s: (s, i, 0))
    return _call(
        body, comm, 2, (dxb, wd, g, u), name=name, grid=(T // tm, N_CHIPS),
        in_specs=[pl.BlockSpec((tm, D), lambda i, s: (i, 0)),
                  pl.BlockSpec((None, None, FS, D), lambda i, s: (l, s, 0, 0)), aspec, aspec],
        out_specs=[aspec, aspec],
        out_shape=[_sds((N_CHIPS, T, FS), BF16)] * 2,
        compiler_params=_params(("parallel", "parallel"), VMEM_BIG))


def ffn_dgrad(dg, du, wg, wu, tm=512):
    def compute(dg_ref, du_ref, wg_ref, wu_ref):
        d = None
        for s in range(N_CHIPS):
            t = _dot(dg_ref[s], wg_ref[s], NN) + _dot(du_ref[s], wu_ref[s], NN)
            d = t if d is None else d + t
        return d

    aspec = pl.BlockSpec((N_CHIPS, tm, FS), lambda i: (0, i, 0))
    wspec = pl.BlockSpec((None, N_CHIPS, FS, D), lambda i: (0, 0, 0, 0), pipeline_mode=pl.Buffered(1))
    return compute, (dg, du, wg, wu), [aspec, aspec, wspec, wspec]


def qkv_dgrad(dqkvp, w, tm=512):
    def compute(a_ref, w_ref):
        d = None
        for p in range(3):
            t = _dot(a_ref[p], w_ref[:, p * D:(p + 1) * D], NT)
            d = t if d is None else d + t
        return d

    return compute, (dqkvp, w), [pl.BlockSpec((3, tm, D), lambda i: (0, i + FRONT // tm, 0)),
                                 pl.BlockSpec((None, D, 3 * D), lambda i: (0, 0, 0))]


def in_dgrad(dpre, w, tm=512):
    def compute(a_ref, w_ref):
        return _dot(a_ref[...], w_ref[...], NT)

    return compute, (dpre, w), [pl.BlockSpec((tm, 2 * GH), lambda i: (i, 0)),
                                pl.BlockSpec((None, D, 2 * GH), lambda i: (0, 0, 0))]


def _rms_rows(x, g):
    r = lax.rsqrt(jnp.mean(x * x, axis=-1, keepdims=True) + EPS)
    return ((x * r) * g).astype(BF16)


def residual_proj(name, compute, args, specs, res, norm_g, tm=512, comm=None):
    T = res.shape[0]
    k = len(args)
    with_norm = norm_g is not None

    def body(*refs):
        d = refs[k][...] + compute(*refs[:k])
        if with_norm:
            refs[k + 2][...] = d
            refs[k + 3][...] = _rms_rows(d, refs[k + 1][...])
        else:
            refs[k + 1][...] = d

    row = pl.BlockSpec((tm, D), lambda i: (i, 0))
    vec = pl.BlockSpec((1, D), lambda i: (0, 0))
    if with_norm:
        return _call(body, comm, 2, (*args, res, norm_g), name=name, grid=(T // tm,),
                     in_specs=list(specs) + [row, vec], out_specs=[row, row],
                     out_shape=[_sds((T, D), F32), _sds((T, D), BF16)],
                     compiler_params=_params(("parallel",), VMEM_BIG))
    return _call(body, comm, 1, (*args, res), name=name, grid=(T // tm,), in_specs=list(specs) + [row],
                 out_specs=row, out_shape=_sds((T, D), F32), compiler_params=_params(("parallel",), VMEM_BIG))


def ffn_down(h, wd, tm=512):
    def compute(h_ref, wd_ref):
        d = None
        for s in range(N_CHIPS):
            t = _dot(h_ref[s], wd_ref[s], NN)
            d = t if d is None else d + t
        return d

    return compute, (h, wd), [pl.BlockSpec((N_CHIPS, tm, FS), lambda i: (0, i, 0)),
                              pl.BlockSpec((None, N_CHIPS, FS, D), lambda i: (0, 0, 0, 0))]


def out_proj(a, w, tm=512):
    K = a.shape[1]

    def compute(a_ref, w_ref):
        return _dot(a_ref[...], w_ref[...], NN)

    return compute, (a, w), [pl.BlockSpec((tm, K), lambda i: (i, 0)), pl.BlockSpec((None, K, D), lambda i: (0, 0, 0))]


def adamw(w, g, m, v, name):
    L, R, C = w.shape

    def body(w_ref, g_ref, m_ref, v_ref, go_ref, d_ref, nm_ref, nv_ref):
        gf = g_ref[...]
        go_ref[...] = gf
        nm = ADAM_B1 * m_ref[...] + (1.0 - ADAM_B1) * gf
        nv = ADAM_B2 * v_ref[...] + (1.0 - ADAM_B2) * (gf * gf)
        m_hat = nm / (1.0 - ADAM_B1 ** ADAM_STEP)
        v_hat = nv / (1.0 - ADAM_B2 ** ADAM_STEP)
        d_ref[...] = -ADAM_LR * (m_hat / (jnp.sqrt(v_hat) + ADAM_EPS) + ADAM_WD * w_ref[...])
        nm_ref[...] = nm
        nv_ref[...] = nv

    tr = R // 4 if R % 32 == 0 else R
    spec = pl.BlockSpec((None, tr, C), lambda l, r: (l, r, 0))
    return _pallas(body, name=name, grid=(L, R // tr), in_specs=[spec] * 4, out_specs=[spec] * 4,
                   out_shape=[_sds((L, R, C), F32)] * 4,
                   compiler_params=_params(("parallel", "parallel")))(w, g, m, v)


def _coords():
    return lax.axis_index("x"), lax.axis_index("y"), lax.axis_index("c")


def _other_chips(x, y):
    out = []
    for fx, fy in ((1, 0), (0, 1), (1, 1)):
        px = (1 - x) if fx else x
        py = (1 - y) if fy else y
        out.append((px, py))
    return out


def _flip_index(s, j):
    sx, sy = s // 2, s % 2
    fx, fy = ((1, 0), (0, 1), (1, 1))[j]
    return 2 * (sx ^ fx) + (sy ^ fy)


def _for_my_chip(sme, fn):
    for s in range(N_CHIPS):
        pl.when(sme == s)(functools.partial(fn, s))


ANY = pl.BlockSpec(memory_space=pl.ANY)

GATHER_KIND = {"a_w_in": "col", "b_w_qkv": "col", "a_w_out": "row", "b_w_out": "row",
               "ffn_w_gate": "row", "ffn_w_up": "row", "ffn_w_down": "row"}
BIG = tuple(GATHER_KIND)


def _gathered_shape(kind, shape):
    L, R, C = shape
    return (L, R, N_CHIPS * C) if kind == "col" else (L, N_CHIPS, R, C)


def _shard_rows(ref, kind, s, r0, rn, C):
    if kind == "col":
        return ref.at[:, pl.ds(r0, rn), s * C:(s + 1) * C]
    return ref.at[:, s, pl.ds(r0, rn), :]


def gather_stage1(items):
    n = len(items)
    dims = [it[0].shape[1:] for it in items]

    def copies(ins, outs, sems, s, with_landed=True):
        lsem, ssem, rsem = sems
        x, y, c = _coords()
        chips = _other_chips(x, y)
        local, send, landed = [], [], []
        for t, (_, li, kind) in enumerate(items):
            R, C = dims[t]
            r0 = pl.multiple_of(c * (R // 2), 8)
            local.append(pltpu.make_async_copy(ins[t].at[pl.ds(li, 1)], _shard_rows(outs[t], kind, s, 0, R, C),
                                               lsem.at[t]))
            for j in range(3):
                pair = dict(send_sem=ssem.at[3 * t + j], recv_sem=rsem.at[3 * t + j],
                            device_id=(chips[j][0], chips[j][1], c), device_id_type=MESH)
                send.append(pltpu.make_async_remote_copy(
                    src_ref=ins[t].at[pl.ds(li, 1), pl.ds(r0, R // 2), :],
                    dst_ref=_shard_rows(outs[t], kind, s, r0, R // 2, C), **pair))
                if with_landed:
                    got = _shard_rows(outs[t], kind, _flip_index(s, j), r0, R // 2, C)
                    landed.append(pltpu.make_async_remote_copy(src_ref=got, dst_ref=got, **pair))
        return local, send, landed

    def start(ins, outs, sems):
        def run(s):
            local, send, _ = copies(ins, outs, sems, s, with_landed=False)
            for cp in local + send:
                cp.start()
        x, y, _ = _coords()
        _for_my_chip(2 * x + y, run)

    def wait(ins, outs, sems):
        def run(s):
            local, send, landed = copies(ins, outs, sems, s)
            for cp in landed:
                cp.wait_recv()
            for cp in send:
                cp.wait_send()
            for cp in local:
                cp.wait()
        x, y, _ = _coords()
        _for_my_chip(2 * x + y, run)

    out_shapes = [_sds(_gathered_shape(kind, (1,) + tuple(dims[t])), BF16) for t, (_, _, kind) in enumerate(items)]
    sems = [pltpu.SemaphoreType.DMA((n,)), pltpu.SemaphoreType.DMA((3 * n,)), pltpu.SemaphoreType.DMA((3 * n,))]
    return Comm([it[0] for it in items], out_shapes, sems, start, wait)


def gather_stage2(items, gathered):
    n = len(items)
    dims = [it[0].shape[1:] for it in items]

    def copies(outs, sems, s, with_landed=True):
        ssem, rsem = sems
        x, y, c = _coords()
        send, landed = [], []
        for t, (_, _, kind) in enumerate(items):
            R, C = dims[t]
            for j in range(3):
                pair = dict(send_sem=ssem.at[3 * t + j], recv_sem=rsem.at[3 * t + j],
                            device_id=(x, y, 1 - c), device_id_type=MESH)
                mine = _shard_rows(outs[t], kind, _flip_index(s, j), pl.multiple_of(c * (R // 2), 8), R // 2, C)
                send.append(pltpu.make_async_remote_copy(src_ref=mine, dst_ref=mine, **pair))
                if with_landed:
                    other = _shard_rows(outs[t], kind, _flip_index(s, j), pl.multiple_of((1 - c) * (R // 2), 8),
                                        R // 2, C)
                    landed.append(pltpu.make_async_remote_copy(src_ref=other, dst_ref=other, **pair))
        return send, landed

    def start(ins, outs, sems):
        def run(s):
            for cp in copies(outs, sems, s, with_landed=False)[0]:
                cp.start()
        x, y, _ = _coords()
        _for_my_chip(2 * x + y, run)

    def wait(ins, outs, sems):
        def run(s):
            send, landed = copies(outs, sems, s)
            for cp in landed:
                cp.wait_recv()
            for cp in send:
                cp.wait_send()
        x, y, _ = _coords()
        _for_my_chip(2 * x + y, run)

    out_shapes = [_sds(g.shape, BF16) for g in gathered]
    sems = [pltpu.SemaphoreType.DMA((3 * n,)), pltpu.SemaphoreType.DMA((3 * n,))]
    return Comm(gathered, out_shapes, sems, start, wait, aliases={t: t for t in range(n)})


def _half_shape(kind, R, C):
    return (R // 2, N_CHIPS * C) if kind == "col" else (N_CHIPS, R // 2, C)


def exchange_halves(grads, metas):
    n = len(grads)

    def copies(ins, outs, sems):
        ssem, rsem = sems
        x, y, c = _coords()
        out = []
        for t, (kind, R, C) in enumerate(metas):
            r0 = pl.multiple_of((1 - c) * (R // 2), 8)
            src = ins[t].at[pl.ds(r0, R // 2), :] if kind == "col" else ins[t].at[:, pl.ds(r0, R // 2), :]
            out.append(pltpu.make_async_remote_copy(
                src_ref=src, dst_ref=outs[t], send_sem=ssem.at[t], recv_sem=rsem.at[t],
                device_id=(x, y, 1 - c), device_id_type=MESH))
        return out

    def start(ins, outs, sems):
        for cp in copies(ins, outs, sems):
            cp.start()

    def wait(ins, outs, sems):
        for cp in copies(ins, outs, sems):
            cp.wait()

    return Comm(grads, [_sds(_half_shape(*m), F32) for m in metas], [pltpu.SemaphoreType.DMA((n,))] * 2, start, wait)


def pair_sum(me, g, sib, meta, name):
    kind, R, C = meta
    h = R // 2

    def body(me_ref, g_ref, sib_ref, p16_ref, own_ref):
        s = pl.program_id(0)
        v = g_ref[...] + sib_ref[...]
        p16_ref[...] = v.astype(BF16)

        @pl.when(s == me_ref[1])
        def _():
            own_ref[...] = v

    if kind == "col":
        gspec = pl.BlockSpec((h, C), lambda s, me_ref: (me_ref[0], s))
        sspec = pl.BlockSpec((h, C), lambda s, me_ref: (0, s))
    else:
        gspec = pl.BlockSpec((None, h, C), lambda s, me_ref: (s, me_ref[0], 0))
        sspec = pl.BlockSpec((None, h, C), lambda s, me_ref: (s, 0, 0))
    grid_spec = pltpu.PrefetchScalarGridSpec(
        num_scalar_prefetch=1, grid=(N_CHIPS,), in_specs=[gspec, sspec],
        out_specs=[sspec, pl.BlockSpec((h, C), lambda s, me_ref: (0, 0))])
    return _pallas(body, name=name, grid_spec=grid_spec,
                   out_shape=[_sds(_half_shape(*meta), BF16), _sds((h, C), F32)],
                   compiler_params=_params(("arbitrary",), VMEM_BIG))(me, g, sib)


def scatter_partials(p16s, metas):
    n = len(p16s)

    def copies(ins, outs, sems, s):
        ssem, rsem = sems
        x, y, c = _coords()
        chips = _other_chips(x, y)
        out = []
        for t, (kind, R, C) in enumerate(metas):
            for j in range(3):
                sj = _flip_index(s, j)
                src = ins[t].at[:, sj * C:(sj + 1) * C] if kind == "col" else ins[t].at[sj]
                out.append(pltpu.make_async_remote_copy(
                    src_ref=src, dst_ref=outs[t].at[j], send_sem=ssem.at[3 * t + j], recv_sem=rsem.at[3 * t + j],
                    device_id=(chips[j][0], chips[j][1], c), device_id_type=MESH))
        return out

    def start(ins, outs, sems):
        def run(s):
            for cp in copies(ins, outs, sems, s):
                cp.start()
        x, y, _ = _coords()
        _for_my_chip(2 * x + y, run)

    def wait(ins, outs, sems):
        def run(s):
            for cp in copies(ins, outs, sems, s):
                cp.wait()
        x, y, _ = _coords()
        _for_my_chip(2 * x + y, run)

    return Comm(p16s, [_sds((3, R // 2, C), BF16) for (_, R, C) in metas],
                [pltpu.SemaphoreType.DMA((3 * n,))] * 2, start, wait)


def final_sum(me, own, q, buf, shape, l, meta, name):
    _, R, C = meta
    h = R // 2

    def body(me_ref, own_ref, q_ref, *rest):
        rest[-1][...] = ((own_ref[...] + q_ref[0].astype(F32)) + q_ref[1].astype(F32)) + q_ref[2].astype(F32)

    grid_spec = pltpu.PrefetchScalarGridSpec(
        num_scalar_prefetch=1, grid=(1,),
        in_specs=[pl.BlockSpec((h, C), lambda i, me_ref: (0, 0)),
                  pl.BlockSpec((3, h, C), lambda i, me_ref: (0, 0, 0))] + ([] if buf is None else [ANY]),
        out_specs=pl.BlockSpec((None, h, C), lambda i, me_ref: (l, me_ref[0], 0)))
    alias = {} if buf is None else {"input_output_aliases": {3: 0}}
    args = (me, own, q) if buf is None else (me, own, q, buf)
    return _pallas(body, name=name, grid_spec=grid_spec, out_shape=_sds(shape, F32),
                   compiler_params=_params(("arbitrary",), VMEM_BIG), **alias)(*args)


def share_final(bufs):
    n = len(bufs)

    def body(*refs):
        ins, outs = refs[:n], refs[n:2 * n]
        ssem, rsem = refs[2 * n:]
        del ins
        x, y, c = _coords()
        copies = []
        for t in range(n):
            R = bufs[t].shape[1]
            r0 = pl.multiple_of(c * (R // 2), 8)
            blk = outs[t].at[:, pl.ds(r0, R // 2), :]
            copies.append(pltpu.make_async_remote_copy(
                src_ref=blk, dst_ref=blk, send_sem=ssem.at[t], recv_sem=rsem.at[t],
                device_id=(x, y, 1 - c), device_id_type=MESH))
        for cp in copies:
            cp.start()
        for t in range(n):
            R = bufs[t].shape[1]
            r1 = pl.multiple_of((1 - c) * (R // 2), 8)
            other = outs[t].at[:, pl.ds(r1, R // 2), :]
            pltpu.make_async_remote_copy(
                src_ref=other, dst_ref=other, send_sem=ssem.at[t], recv_sem=rsem.at[t],
                device_id=(x, y, 1 - c), device_id_type=MESH).wait_recv()
        for cp in copies:
            cp.wait_send()

    out_shape = [_sds(b.shape, F32) for b in bufs]
    return _pallas(body, name="share_final", in_specs=[ANY] * n, out_specs=[ANY] * n, out_shape=out_shape,
                   input_output_aliases={t: t for t in range(n)},
                   scratch_shapes=[pltpu.SemaphoreType.DMA((n,))] * 2,
                   compiler_params=pltpu.CompilerParams(has_side_effects=True))(*bufs)


def allreduce_small(part):
    rows = part.shape[0]
    h = rows // 2

    def body(p_ref, o_ref, sib_buf, pair_buf, chip_buf, ssem, rsem):
        x, y, c = _coords()
        sibling = dict(device_id=(x, y, 1 - c), device_id_type=MESH)
        mine = pl.ds(pl.multiple_of(c * h, 8), h)
        theirs = pl.ds(pl.multiple_of((1 - c) * h, 8), h)

        swap = pltpu.make_async_remote_copy(src_ref=p_ref.at[theirs], dst_ref=sib_buf, send_sem=ssem.at[0],
                                            recv_sem=rsem.at[0], **sibling)
        swap.start()
        swap.wait()
        pair_buf[...] = p_ref[mine, :] + sib_buf[...]

        chips = _other_chips(x, y)
        sends = [pltpu.make_async_remote_copy(src_ref=pair_buf, dst_ref=chip_buf.at[j], send_sem=ssem.at[1 + j],
                                              recv_sem=rsem.at[1 + j], device_id=(chips[j][0], chips[j][1], c),
                                              device_id_type=MESH) for j in range(3)]
        for cp in sends:
            cp.start()
        for cp in sends:
            cp.wait()

        def total(s):
            terms = {s: pair_buf[...]}
            for j in range(3):
                terms[_flip_index(s, j)] = chip_buf[j]
            o_ref[mine, :] = ((terms[0] + terms[1]) + terms[2]) + terms[3]

        _for_my_chip(2 * x + y, total)

        back = pltpu.make_async_remote_copy(src_ref=o_ref.at[mine], dst_ref=o_ref.at[mine], send_sem=ssem.at[4],
                                            recv_sem=rsem.at[4], **sibling)
        back.start()
        pltpu.make_async_remote_copy(src_ref=o_ref.at[theirs], dst_ref=o_ref.at[theirs], send_sem=ssem.at[4],
                                     recv_sem=rsem.at[4], **sibling).wait_recv()
        back.wait_send()

    return _pallas(body, name="allreduce_small",
                   in_specs=[pl.BlockSpec(memory_space=pltpu.VMEM)], out_specs=pl.BlockSpec(memory_space=pltpu.VMEM),
                   out_shape=_sds((rows, 128), F32),
                   scratch_shapes=[pltpu.VMEM((h, 128), F32), pltpu.VMEM((h, 128), F32), pltpu.VMEM((3, h, 128), F32),
                                   pltpu.SemaphoreType.DMA((5,)), pltpu.SemaphoreType.DMA((5,))],
                   compiler_params=pltpu.CompilerParams(has_side_effects=True))(part)


def _rows128(a):
    flat = a.reshape(-1)
    rows = -(-flat.shape[0] // 128)
    rows8 = -(-rows // 8) * 8
    flat = jnp.pad(flat, (0, rows8 * 128 - flat.shape[0]))
    return flat.reshape(rows8, 128)


def kernel(x, norm_mix_g, norm_ffn_g, final_g, a_w_in, a_v_gain, a_w_s, a_b_s, a_w_out, b_w_qkv, b_rel_bias, b_w_out, ffn_w_gate, ffn_w_up, ffn_w_down, loss_target, m_norm_mix_g, m_norm_ffn_g, m_final_g, m_a_w_in, m_a_v_gain, m_a_w_s, m_a_b_s, m_a_w_out, m_b_w_qkv, m_b_rel_bias, m_b_w_out, m_ffn_w_gate, m_ffn_w_up, m_ffn_w_down, v_norm_mix_g, v_norm_ffn_g, v_final_g, v_a_w_in, v_a_v_gain, v_a_w_s, v_a_b_s, v_a_w_out, v_b_w_qkv, v_b_rel_bias, v_b_w_out, v_ffn_w_gate, v_ffn_w_up, v_ffn_w_down):
    T = x.shape[1]
    weights = dict(norm_mix_g=norm_mix_g, norm_ffn_g=norm_ffn_g, final_g=final_g, a_w_in=a_w_in, a_v_gain=a_v_gain,
                   a_w_s=a_w_s, a_b_s=a_b_s, a_w_out=a_w_out, b_w_qkv=b_w_qkv, b_rel_bias=b_rel_bias,
                   b_w_out=b_w_out, ffn_w_gate=ffn_w_gate, ffn_w_up=ffn_w_up, ffn_w_down=ffn_w_down)
    mom_m = dict(norm_mix_g=m_norm_mix_g, norm_ffn_g=m_norm_ffn_g, final_g=m_final_g, a_w_in=m_a_w_in,
                 a_v_gain=m_a_v_gain, a_w_s=m_a_w_s, a_b_s=m_a_b_s, a_w_out=m_a_w_out, b_w_qkv=m_b_w_qkv,
                 b_rel_bias=m_b_rel_bias, b_w_out=m_b_w_out, ffn_w_gate=m_ffn_w_gate, ffn_w_up=m_ffn_w_up,
                 ffn_w_down=m_ffn_w_down)
    mom_v = dict(norm_mix_g=v_norm_mix_g, norm_ffn_g=v_norm_ffn_g, final_g=v_final_g, a_w_in=v_a_w_in,
                 a_v_gain=v_a_v_gain, a_w_s=v_a_w_s, a_b_s=v_a_b_s, a_w_out=v_a_w_out, b_w_qkv=v_b_w_qkv,
                 b_rel_bias=v_b_rel_bias, b_w_out=v_b_w_out, ffn_w_gate=v_ffn_w_gate, ffn_w_up=v_ffn_w_up,
                 ffn_w_down=v_ffn_w_down)
    order = list(weights)
    transposed = ("ffn_w_gate", "ffn_w_up")
    for k in transposed:
        weights[k], mom_m[k], mom_v[k] = (jnp.swapaxes(a, 1, 2) for a in (weights[k], mom_m[k], mom_v[k]))

    xi, yi, ci = _coords()
    me = jnp.stack([ci, 2 * xi + yi]).astype(jnp.int32)

    shard16 = {k: cast_bf16(weights[k], "cast_" + k) for k in BIG}

    def layer_tensors(i):
        mix = ("a_w_in", "a_w_out") if i % 2 == 0 else ("b_w_qkv", "b_w_out")
        return [(k, i // 2) for k in mix] + [(k, i) for k in ("ffn_w_gate", "ffn_w_up", "ffn_w_down")]

    def gather_items(keys):
        return [(shard16[k], l, GATHER_KIND[k]) for k, l in keys]

    def grad_metas(keys):
        return [(GATHER_KIND[k],) + tuple(weights[k].shape[1:]) for k, _ in keys]

    FFN = ("ffn_w_gate", "ffn_w_up", "ffn_w_down")
    k0a = [("a_w_out", 0), ("ffn_w_gate", 0)]
    k0b = [("ffn_w_up", 0), ("ffn_w_down", 0)]
    k1a = [("b_w_qkv", 0), ("b_w_out", 0), ("ffn_w_gate", 1)]
    k1b = [("ffn_w_up", 1), ("ffn_w_down", 1)]
    k3a = [("b_w_qkv", 1), ("b_w_out", 1), ("ffn_w_gate", 3)]
    k3b = [("ffn_w_up", 3), ("ffn_w_down", 3)]
    plans = {
        "a_in_l0": [("g1", k0a)], "sgu_fwd_l0": [("g2", k0a), ("g1", k0b)], "a_out_l0": [("g2", k0b)],
        "rms_mix_l0": [("g1", [("a_w_in", 0)])],
        "ffn_up_l0": [("g1", k1a)], "ffn_down_l0": [("g2", k1a), ("g1", k1b[:1])],
        "b_qkv_l1": [("g2", k1b[:1]), ("g1", k1b[1:])],
        "attn_fwd_l1": [("g2", k1b[1:]), ("g1", layer_tensors(2))], "b_out_l1": [("g2", layer_tensors(2))],
        "ffn_up_l1": [("g1", k3a)], "ffn_down_l1": [("g2", k3a)],
        "a_in_l2": [("g1", k3b)], "sgu_fwd_l2": [("g2", k3b)],
        "ffn_bwd_dhn_l2": [("ex", layer_tensors(3))], "sgu_bwd_l2": [("sc", layer_tensors(3))],
        "ffn_bwd_dhn_l1": [("ex", layer_tensors(2))], "attn_bwd_l1": [("sc", layer_tensors(2))],
        "dw_down_l0": [("ex", layer_tensors(1))], "ffn_bwd_dhn_l0": [("sc", k1a)],
        "dffn_w_gate_l0": [("sc", [("ffn_w_up", 1)])], "dffn_w_up_l0": [("sc", [("ffn_w_down", 1)])],
        "a_out_bwd_l0": [("ex", [(k, 0) for k in FFN])],
        "sgu_bwd_l0": [("sc", [("ffn_w_gate", 0), ("ffn_w_up", 0)]), ("ex", [("a_w_out", 0)])],
        "dw_in_l0": [("sc", [("ffn_w_down", 0), ("a_w_out", 0)])],
    }
    part16, full16 = {}, {}
    sib, p16, own_parts, recv_parts = {}, {}, {}, {}

    def make_comm(kind, keys):
        if kind == "g1":
            return gather_stage1(gather_items(keys)), lambda outs: part16.update(zip(keys, outs))
        if kind == "g2":
            return (gather_stage2(gather_items(keys), [part16[kl] for kl in keys]),
                    lambda outs: full16.update(zip(keys, outs)))
        if kind == "ex":
            return (exchange_halves([big_grads[k][l] for k, l in keys], grad_metas(keys)),
                    lambda outs: sib.update(zip(keys, outs)))
        for kl, m_ in zip(keys, grad_metas(keys)):
            p16[kl], own_parts[kl] = pair_sum(me, big_grads[kl[0]][kl[1]], sib[kl], m_, "pair_sum_%s_l%d" % kl)
        return (scatter_partials([p16[kl] for kl in keys], grad_metas(keys)),
                lambda outs: recv_parts.update(zip(keys, outs)))

    def run(name, make):
        steps = plans.get(name)
        if not steps:
            return make(None)
        made = [make_comm(kind, keys) for kind, keys in steps]
        main, outs = make(combine([c for c, _ in made]))
        for c, done in made:
            done(outs[:len(c.out_shapes)])
            outs = outs[len(c.out_shapes):]
        return main

    def weight(k, l):
        w = full16[(k, l)]
        if k == "a_w_out":
            return w.reshape(1, GH, D)
        return w.reshape(1, D, D) if k == "b_w_out" else w


    xcur = x.reshape(T, D)
    hn = run("rms_mix_l0", lambda comm: rms_fwd(xcur, norm_mix_g[0][None], "rms_mix_l0", comm=comm))
    comm, done = make_comm("g2", [("a_w_in", 0)])
    done(run_comm(comm, "gather_first_d2d"))
    saved = []
    for i in range(DEPTH):
        j = i // 2
        tag = "_l%d" % i
        st = {"x_in": xcur, "hn": hn}
        if i % 2 == 0:
            pre = run("a_in" + tag, lambda comm: matmul(
                "a_in" + tag, NN, hn, pl.BlockSpec((1024, D), lambda i_, j_: (i_, 0)),
                weight("a_w_in", j), pl.BlockSpec((None, D, 1024), lambda i_, j_: (0, 0, j_)),
                _sds((T, 2 * GH), BF16), pl.BlockSpec((1024, 1024), lambda i_, j_: (i_, j_)),
                (T // 1024, 4), comm=comm))
            y = run("sgu_fwd" + tag, lambda comm: sgu_fwd(
                pre, a_v_gain[j][None], a_w_s[j], a_b_s[j][:, :, None], "sgu_fwd" + tag, comm=comm))
            xmid, hn2 = run("a_out" + tag, lambda comm: residual_proj(
                "a_out" + tag, *out_proj(y, weight("a_w_out", j)), xcur, norm_ffn_g[i][None], comm=comm))
            st.update(pre=pre, y=y)
        else:
            qkvp = run("b_qkv" + tag, lambda comm: proj_qkv(hn, weight("b_w_qkv", j), 0, "b_qkv" + tag, comm=comm))
            wb = jnp.transpose(bias_build(b_rel_bias[j], "bias_build" + tag), (1, 0, 2))
            o = run("attn_fwd" + tag, lambda comm: attn_fwd(qkvp, wb, "attn_fwd" + tag, comm=comm))
            xmid, hn2 = run("b_out" + tag, lambda comm: residual_proj(
                "b_out" + tag, *out_proj(o, weight("b_w_out", j)), xcur, norm_ffn_g[i][None], comm=comm))
            st.update(qkvp=qkvp, wb=wb, o=o)
        g, u, h = run("ffn_up" + tag, lambda comm: ffn_up(
            hn2, weight("ffn_w_gate", i), weight("ffn_w_up", i), 0, "ffn_up" + tag, comm=comm))
        next_g = norm_mix_g[i + 1][None] if i + 1 < DEPTH else None
        down = run("ffn_down" + tag, lambda comm: residual_proj(
            "ffn_down" + tag, *ffn_down(h, weight("ffn_w_down", i)), xmid, next_g, comm=comm))
        xcur, hn = down if next_g is not None else (down, None)
        st.update(x_mid=xmid, hn2=hn2, g=g, u=u, h=h)
        saved.append(st)

    loss_part, dx, dxb, d_final = final_loss(xcur, final_g[None], loss_target.reshape(T, D), "final_loss")

    tk = min(2048, T)
    big_grads = {k: [None] * weights[k].shape[0] for k in BIG}
    small = {"norm_mix_g": [None] * DEPTH, "norm_ffn_g": [None] * DEPTH, "a_v_gain": [None] * 2,
             "a_w_s": [None] * 2, "a_b_s": [None] * 2, "b_rel_bias": [None] * 2}
    tok = lambda width: pl.BlockSpec((tk, width), lambda j_, k_: (k_, 0))
    part = lambda: pl.BlockSpec((None, tk, FS), lambda j_, k_: (j_, k_, 0))
    for i in reversed(range(DEPTH)):
        j = i // 2
        tag = "_l%d" % i
        st = saved[i]
        dg, du = run("ffn_bwd_dh" + tag, lambda comm: ffn_bwd_dh(
            dxb, weight("ffn_w_down", i), st["g"], st["u"], 0, "ffn_bwd_dh" + tag, comm=comm))
        big_grads["ffn_w_down"][i] = run("dw_down" + tag, lambda comm: wgrad(
            "dw_down" + tag, st["h"], part(), dxb, tok(D), _sds((N_CHIPS, FS, D), F32),
            pl.BlockSpec((None, FS, D), lambda j_, k_: (j_, 0, 0)), N_CHIPS, T, tk, comm=comm))
        dx_mid, dxb_mid, dgn = run("ffn_bwd_dhn" + tag, lambda comm: dgrad_rms(
            "ffn_bwd_dhn" + tag, *ffn_dgrad(dg, du, weight("ffn_w_gate", i), weight("ffn_w_up", i)),
            st["x_mid"], norm_ffn_g[i][None], dx, comm=comm))
        for nm, dz in (("ffn_w_gate", dg), ("ffn_w_up", du)):
            big_grads[nm][i] = run("d" + nm + tag, lambda comm: wgrad(
                "d" + nm + tag, dz, part(), st["hn2"], tok(D), _sds((N_CHIPS, FS, D), F32),
                pl.BlockSpec((None, FS, D), lambda j_, k_: (j_, 0, 0)), N_CHIPS, T, tk, comm=comm))
        dx, dxb = dx_mid, dxb_mid
        small["norm_ffn_g"][i] = dgn
        if i % 2 == 0:
            dy = run("a_out_bwd" + tag, lambda comm: matmul(
                "a_out_bwd" + tag, NT, dxb, pl.BlockSpec((1024, D), lambda i_, j_: (i_, 0)),
                weight("a_w_out", j), pl.BlockSpec((None, 1024, D), lambda i_, j_: (0, j_, 0)),
                _sds((T, GH), BF16), pl.BlockSpec((1024, 1024), lambda i_, j_: (i_, j_)), (T // 1024, 2), comm=comm))
            big_grads["a_w_out"][j] = wgrad(
                "dw_aout" + tag, st["y"], pl.BlockSpec((tk, 1024), lambda j_, k_: (k_, j_)), dxb, tok(D),
                _sds((GH, D), F32), pl.BlockSpec((1024, D), lambda j_, k_: (j_, 0)), 2, T, tk
            ).reshape(N_CHIPS, GH // N_CHIPS, D)
            dpre, d_ws, d_bs, d_gain = run("sgu_bwd" + tag, lambda comm: sgu_bwd(
                st["pre"], dy, a_v_gain[j][None], a_w_s[j], a_b_s[j][:, :, None], "sgu_bwd" + tag,
                tm=min(4 * SGU_BLOCK, T), comm=comm))
            small["a_w_s"][j], small["a_b_s"][j], small["a_v_gain"][j] = d_ws, d_bs, d_gain
            dx_in, dxb_in, dgn = run("a_in_bwd" + tag, lambda comm: dgrad_rms(
                "a_in_bwd" + tag, *in_dgrad(dpre, weight("a_w_in", j)),
                st["x_in"], norm_mix_g[i][None], dx, comm=comm))
            big_grads["a_w_in"][j] = run("dw_in" + tag, lambda comm: wgrad(
                "dw_in" + tag, st["hn"], tok(D), dpre, pl.BlockSpec((tk, 1024), lambda j_, k_: (k_, j_)),
                _sds((D, 2 * GH), F32), pl.BlockSpec((D, 1024), lambda j_, k_: (0, j_)), 4, T, tk, comm=comm))
        else:
            do = matmul("b_out_bwd" + tag, NT, dxb, pl.BlockSpec((1024, D), lambda i_, j_: (i_, 0)),
                        weight("b_w_out", j), pl.BlockSpec((None, D, D), lambda i_, j_: (0, 0, 0)),
                        _sds((T, D), BF16), pl.BlockSpec((1024, D), lambda i_, j_: (i_, 0)), (T // 1024, 1))
            big_grads["b_w_out"][j] = wgrad(
                "dw_bout" + tag, st["o"], tok(D), dxb, tok(D),
                _sds((D, D), F32), pl.BlockSpec((D, D), lambda j_, k_: (0, 0)), 1, T, tk
            ).reshape(N_CHIPS, D // N_CHIPS, D)
            dqkvp, dwb = run("attn_bwd" + tag, lambda comm: attn_bwd(
                st["qkvp"], st["o"], do, st["wb"], "attn_bwd" + tag, comm=comm))
            small["b_rel_bias"][j] = bias_grad(
                jnp.pad(jnp.transpose(dwb, (1, 0, 2)), ((0, 0), (0, 0), (0, DIAG - KW))), "bias_grad" + tag)
            dx_in, dxb_in, dgn = dgrad_rms(
                "b_qkv_bwd" + tag, *qkv_dgrad(dqkvp, weight("b_w_qkv", j)),
                st["x_in"], norm_mix_g[i][None], dx)
            big_grads["b_w_qkv"][j] = wgrad(
                "dw_qkv" + tag, st["hn"], tok(D), dqkvp,
                pl.BlockSpec((None, tk, D), lambda j_, k_: (j_, k_ + FRONT // tk, 0)),
                _sds((D, 3 * D), F32), pl.BlockSpec((D, D), lambda j_, k_: (0, j_)), 3, T, tk)
        dx, dxb = dx_in, dxb_in
        small["norm_mix_g"][i] = dgn

    small_grads = {
        "norm_mix_g": jnp.concatenate(small["norm_mix_g"], axis=0),
        "norm_ffn_g": jnp.concatenate(small["norm_ffn_g"], axis=0),
        "final_g": d_final.reshape(D),
        "a_v_gain": jnp.concatenate(small["a_v_gain"], axis=0),
        "a_w_s": jnp.stack(small["a_w_s"]),
        "a_b_s": jnp.stack(small["a_b_s"]).reshape(2, SGU_G, SGU_BLOCK),
        "b_rel_bias": jnp.stack(small["b_rel_bias"]),
    }
    small_names = list(small_grads)
    packed = [_rows128(small_grads[k]) for k in small_names] + [_rows128(loss_part[:, :1])]
    offs = [0]
    for p in packed:
        offs.append(offs[-1] + p.shape[0])
    reduced = allreduce_small(jnp.concatenate(packed, axis=0))
    grads = {}
    for t, k in enumerate(small_names):
        nelem = small_grads[k].size
        grads[k] = reduced[offs[t]:offs[t + 1]].reshape(-1)[:nelem].reshape(weights[k].shape)
    loss = reduced[offs[len(small_names)], 0]

    last = [("a_w_in", 0)]
    for kind, name in (("ex", "exchange_last"), ("sc", "scatter_last")):
        comm, done = make_comm(kind, last)
        done(run_comm(comm, name))
    bufs = {k: None for k in BIG}
    for i in range(DEPTH):
        for kl, m_ in zip(layer_tensors(i), grad_metas(layer_tensors(i))):
            bufs[kl[0]] = final_sum(me, own_parts[kl], recv_parts[kl], bufs[kl[0]], weights[kl[0]].shape, kl[1], m_,
                                    "final_sum_%s_l%d" % kl)
    shared = share_final([bufs[k] for k in BIG])
    for k, gfull in zip(BIG, shared):
        grads[k] = gfull

    delta, new_m, new_v = {}, {}, {}
    for k in order:
        shp = weights[k].shape
        if k in BIG:
            view = shp
        elif k == "a_w_s":
            view = (2, SGU_G * SGU_BLOCK, SGU_BLOCK)
        elif len(shp) == 1:
            view = (1, 1, shp[0])
        elif len(shp) == 2:
            view = (1,) + shp
        else:
            view = shp
        g_, d_, m_, v_ = adamw(weights[k].reshape(view), grads[k].reshape(view), mom_m[k].reshape(view),
                               mom_v[k].reshape(view), "adamw_" + k)
        grads[k], delta[k], new_m[k], new_v[k] = g_.reshape(shp), d_.reshape(shp), m_.reshape(shp), v_.reshape(shp)
    for k in transposed:
        for tree in (grads, delta, new_m, new_v):
            tree[k] = jnp.swapaxes(tree[k], 1, 2)

    return (loss, dx.reshape(1, T, D), *[grads[k] for k in order], *[delta[k] for k in order],
            *[new_m[k] for k in order], *[new_v[k] for k in order])
```

```python
import functools

import jax
import jax.numpy as jnp
from jax import lax
from jax.experimental import pallas as pl
from jax.experimental.pallas import tpu as pltpu

F32 = jnp.float32
BF16 = jnp.bfloat16
MESH = pl.DeviceIdType.MESH

D = 1024
DEPTH = 4
EPS = 1e-6
SGU_BLOCK = 128
GH = 2048
SGU_G = 8
SGU_GD = GH // SGU_G
N_HEADS = 16
HEAD_DIM = 64
CHUNK = 64
PAD = 8 * CHUNK
FRONT = 2048
QB = 128
KW = PAD + QB
N_REL = 192
REL_MIN = -(CHUNK - 1)
REL_MAX = 128
D_FF = 2816
FS = D_FF // 4
NEG = -1e30
SCALE = HEAD_DIM ** -0.5
N_CHIPS = 4

ADAM_LR = 0.001
ADAM_B1 = 0.9
ADAM_B2 = 0.999
ADAM_EPS = 1e-08
ADAM_WD = 0.01
ADAM_STEP = 10

VMEM_BIG = 56 * 1024 * 1024

NN = ((1,), (0,))
NT = ((1,), (1,))
TN = ((0,), (0,))


def _dot(a, b, dims):
    return lax.dot_general(a, b, (dims, ((), ())), preferred_element_type=F32)


class Comm:
    def __init__(self, ins, out_shapes, sems, start, wait, aliases=None):
        self.ins, self.out_shapes, self.sems = list(ins), list(out_shapes), list(sems)
        self.start, self.wait, self.aliases = start, wait, dict(aliases or {})


def _host(body, comm, kw):
    grid = tuple(kw["grid"])
    in_specs = list(kw["in_specs"])
    single = not isinstance(kw["out_specs"], (list, tuple))
    out_specs = [kw["out_specs"]] if single else list(kw["out_specs"])
    out_shape = [kw["out_shape"]] if single else list(kw["out_shape"])
    scratch = list(kw.get("scratch_shapes", ()))
    counts = (len(in_specs), len(comm.ins), len(out_specs), len(comm.out_shapes), len(scratch))

    def hosted(*refs):
        parts, p = [], 0
        for cnt in counts:
            parts.append(refs[p:p + cnt])
            p += cnt
        main_in, c_in, main_out, c_out, main_scr = parts
        sems = refs[p:]
        ids = [pl.program_id(a) for a in range(len(grid))]
        first = functools.reduce(jnp.logical_and, [i == 0 for i in ids])
        last = functools.reduce(jnp.logical_and, [i == n - 1 for i, n in zip(ids, grid)])
        pl.when(first)(lambda: comm.start(c_in, c_out, sems))
        body(*main_in, *main_out, *main_scr)
        pl.when(last)(lambda: comm.wait(c_in, c_out, sems))

    old = kw["compiler_params"]
    kw = dict(kw, in_specs=in_specs + [ANY] * len(comm.ins), out_specs=out_specs + [ANY] * len(comm.out_shapes),
              out_shape=out_shape + comm.out_shapes, scratch_shapes=scratch + comm.sems,
              compiler_params=pltpu.CompilerParams(dimension_semantics=("arbitrary",) * len(grid),
                                                   vmem_limit_bytes=old.vmem_limit_bytes, has_side_effects=True))
    if comm.aliases:
        kw["input_output_aliases"] = {counts[0] + i: counts[2] + o for i, o in comm.aliases.items()}
    return hosted, kw


def _pallas(body, comm=None, **kw):
    if comm is not None:
        body, kw = _host(body, comm, kw)
    return pl.pallas_call(body, **kw)


def _split_outs(outs, comm, n_main):
    outs = list(outs) if isinstance(outs, (list, tuple)) else [outs]
    main = outs[:n_main]
    return (main[0] if n_main == 1 else main), outs[n_main:]


def run_comm(comm, name):
    nci, nco = len(comm.ins), len(comm.out_shapes)

    def body(*refs):
        c_in, c_out, sems = refs[:nci], refs[nci:nci + nco], refs[nci + nco:]
        comm.start(c_in, c_out, sems)
        comm.wait(c_in, c_out, sems)

    kw = {}
    if comm.aliases:
        kw["input_output_aliases"] = dict(comm.aliases)
    return _pallas(body, name=name, in_specs=[ANY] * nci, out_specs=[ANY] * nco, out_shape=comm.out_shapes,
                   scratch_shapes=comm.sems, compiler_params=pltpu.CompilerParams(has_side_effects=True),
                   **kw)(*comm.ins)


def combine(comms):
    if len(comms) == 1:
        return comms[0]
    spans, ni, no, ns = [], 0, 0, 0
    for c in comms:
        spans.append((slice(ni, ni + len(c.ins)), slice(no, no + len(c.out_shapes)), slice(ns, ns + len(c.sems))))
        ni, no, ns = ni + len(c.ins), no + len(c.out_shapes), ns + len(c.sems)

    def start(ins, outs, sems):
        for c, (si, so, ss) in zip(comms, spans):
            c.start(ins[si], outs[so], sems[ss])

    def wait(ins, outs, sems):
        for c, (si, so, ss) in zip(comms, spans):
            c.wait(ins[si], outs[so], sems[ss])

    aliases = {}
    for c, (si, so, _) in zip(comms, spans):
        aliases.update({si.start + i: so.start + o for i, o in c.aliases.items()})
    return Comm([a for c in comms for a in c.ins], [o for c in comms for o in c.out_shapes],
                [s for c in comms for s in c.sems], start, wait, aliases)


def _call(body, comm, n_main, args, **kw):
    if comm is None:
        return _pallas(body, **kw)(*args)
    return _split_outs(_pallas(body, comm=comm, **kw)(*args, *comm.ins), comm, n_main)


def _params(sem=None, vmem=None):
    return pltpu.CompilerParams(dimension_semantics=sem, vmem_limit_bytes=vmem)


def _sds(shape, dtype):
    return jax.ShapeDtypeStruct(tuple(shape), dtype)


_GELU_C = 0.7978845608028654


_GELU_A = _GELU_C * 0.044715


def _gelu(x):
    t = jnp.tanh(x * (_GELU_C + _GELU_A * (x * x)))
    h = 0.5 * x
    return h + h * t


def _gelu_and_grad(x):
    x2 = x * x
    t = jnp.tanh(x * (_GELU_C + _GELU_A * x2))
    h = 0.5 * x
    val = h + h * t
    grad = (0.5 + 0.5 * t) + (h * (1.0 - t * t)) * (_GELU_C + (3.0 * _GELU_A) * x2)
    return val, grad


def _sigmoid(x):
    return 0.5 * (jnp.tanh(0.5 * x) + 1.0)


def cast_bf16(w, name):
    L, R, C = w.shape

    def body(w_ref, o_ref):
        o_ref[...] = w_ref[...].astype(BF16)

    spec = pl.BlockSpec((None, R, C), lambda l: (l, 0, 0))
    return _pallas(body, name=name, grid=(L,), in_specs=[spec], out_specs=spec,
                   out_shape=_sds((L, R, C), BF16), compiler_params=_params(("parallel",)))(w)


def rms_fwd(x, g, name, tm=512, comm=None):
    T = x.shape[0]

    def body(x_ref, g_ref, o_ref):
        o_ref[...] = _rms_rows(x_ref[...], g_ref[...])

    row = pl.BlockSpec((tm, D), lambda i: (i, 0))
    return _call(body, comm, 1, (x, g), name=name, grid=(T // tm,),
                 in_specs=[row, pl.BlockSpec((1, D), lambda i: (0, 0))], out_specs=row,
                 out_shape=_sds((T, D), BF16), compiler_params=_params(("parallel",)))


def dgrad_rms(name, compute, args, specs, x, g, dres, tm=512, comm=None):
    T = x.shape[0]
    n = T // tm
    k = len(args)

    def body(*refs):
        x_ref, g_ref, dres_ref, dx_ref, dxb_ref, dg_ref, acc_ref = refs[k:]
        i = pl.program_id(0)
        xf = x_ref[...]
        r = lax.rsqrt(jnp.mean(xf * xf, axis=-1, keepdims=True) + EPS)
        xhat = xf * r
        dhf = compute(*refs[:k])
        part = (dhf * xhat).reshape(tm // 8, 8, D).sum(axis=0)

        @pl.when(i == 0)
        def _():
            acc_ref[...] = part

        @pl.when(i > 0)
        def _():
            acc_ref[...] += part

        dxhat = dhf * g_ref[...]
        dx = dres_ref[...] + r * (dxhat - xhat * jnp.mean(dxhat * xhat, axis=-1, keepdims=True))
        dx_ref[...] = dx
        dxb_ref[...] = dx.astype(BF16)

        @pl.when(i == n - 1)
        def _():
            dg_ref[...] = jnp.sum(acc_ref[...], axis=0, keepdims=True)

    row = pl.BlockSpec((tm, D), lambda i: (i, 0))
    vec = pl.BlockSpec((1, D), lambda i: (0, 0))
    return _call(body, comm, 3, (*args, x, g, dres), name=name, grid=(n,),
                 in_specs=list(specs) + [row, vec, row], out_specs=[row, row, vec],
                 out_shape=[_sds((T, D), F32), _sds((T, D), BF16), _sds((1, D), F32)],
                 scratch_shapes=[pltpu.VMEM((8, D), F32)],
                 compiler_params=_params(("arbitrary",), VMEM_BIG))


def final_loss(x, g, tgt, name, tm=256):
    T = x.shape[0]
    n = T // tm

    def body(x_ref, g_ref, t_ref, loss_ref, dx_ref, dxb_ref, dg_ref, acc_ref, lacc_ref):
        i = pl.program_id(0)
        xf = x_ref[...]
        r = lax.rsqrt(jnp.mean(xf * xf, axis=-1, keepdims=True) + EPS)
        xhat = xf * r
        gg = g_ref[...]
        e = xhat * gg - t_ref[...]
        dy = e * (1.0 / D)
        part = (dy * xhat).reshape(tm // 8, 8, D).sum(axis=0)
        lpart = (e * e).reshape(tm // 8, 8, D).sum(axis=0)

        @pl.when(i == 0)
        def _():
            acc_ref[...] = part
            lacc_ref[...] = lpart

        @pl.when(i > 0)
        def _():
            acc_ref[...] += part
            lacc_ref[...] += lpart

        dxhat = dy * gg
        dx = r * (dxhat - xhat * jnp.mean(dxhat * xhat, axis=-1, keepdims=True))
        dx_ref[...] = dx
        dxb_ref[...] = dx.astype(BF16)

        @pl.when(i == n - 1)
        def _():
            dg_ref[...] = jnp.sum(acc_ref[...], axis=0, keepdims=True)
            total = jnp.sum(jnp.sum(lacc_ref[...], axis=0, keepdims=True), axis=1, keepdims=True)
            loss_ref[...] = jnp.broadcast_to(total * (0.5 / D), (1, 128))

    row = pl.BlockSpec((tm, D), lambda i: (i, 0))
    vec = pl.BlockSpec((1, D), lambda i: (0, 0))
    return _pallas(body, name=name, grid=(n,), in_specs=[row, vec, row],
                   out_specs=[pl.BlockSpec((1, 128), lambda i: (0, 0)), row, row, vec],
                   out_shape=[_sds((1, 128), F32), _sds((T, D), F32), _sds((T, D), BF16), _sds((1, D), F32)],
                   scratch_shapes=[pltpu.VMEM((8, D), F32), pltpu.VMEM((8, D), F32)],
                   compiler_params=_params(("arbitrary",)))(x, g, tgt)


def matmul(name, dims, a, a_spec, b, b_spec, out_shape, out_spec, grid, *, acc=False, res=None, res_spec=None,
           comm=None):
    has_res = res is not None

    def body(*refs):
        a_ref, b_ref = refs[0], refs[1]
        r_ref = refs[2] if has_res else None
        o_ref = refs[-1]
        d = _dot(a_ref[...], b_ref[...], dims)
        if not acc:
            if has_res:
                d = d + r_ref[...]
            o_ref[...] = d.astype(o_ref.dtype)
        else:
            k = pl.program_id(len(grid) - 1)

            @pl.when(k == 0)
            def _():
                o_ref[...] = (d + r_ref[...]) if has_res else d

            @pl.when(k > 0)
            def _():
                o_ref[...] += d

    sem = ("parallel",) * (len(grid) - 1) + (("arbitrary",) if acc else ("parallel",))
    ins = [a, b] + ([res] if has_res else [])
    specs = [a_spec, b_spec] + ([res_spec] if has_res else [])
    return _call(body, comm, 1, ins, name=name, grid=grid, in_specs=specs, out_specs=out_spec, out_shape=out_shape,
                 compiler_params=_params(sem, VMEM_BIG))


def wgrad(name, a, a_spec, b, b_spec, out_shape, out_spec, J, T, tk, comm=None):
    return matmul(name, TN, a, a_spec, b, b_spec, out_shape, out_spec, (J, T // tk), acc=True, comm=comm)


def _sgu_mask():
    p = lax.broadcasted_iota(jnp.int32, (SGU_BLOCK, SGU_BLOCK), 0)
    q = lax.broadcasted_iota(jnp.int32, (SGU_BLOCK, SGU_BLOCK), 1)
    return lax.shift_right_logical(q, 6) <= lax.shift_right_logical(p, 6)


def sgu_fwd(pre, gain, w_s, b_s, name, tm=4 * SGU_BLOCK, comm=None):
    T = pre.shape[0]
    tm = min(tm, T)

    def body(pre_ref, gain_ref, ws_ref, bs_ref, y_ref):
        mask = _sgu_mask()
        gain_v = gain_ref[...]
        for sb in range(tm // SGU_BLOCK):
            rows = slice(sb * SGU_BLOCK, (sb + 1) * SGU_BLOCK)
            u = _gelu(pre_ref[rows, :GH].astype(F32))
            va = _gelu(pre_ref[rows, GH:].astype(F32))
            r = lax.rsqrt(jnp.mean(va * va, axis=-1, keepdims=True) + EPS)
            vn = ((va * r) * gain_v).astype(BF16)
            for g in range(SGU_G):
                sl = slice(g * SGU_GD, (g + 1) * SGU_GD)
                wm = jnp.where(mask, ws_ref[g], 0.0).astype(BF16)
                vm = _dot(wm, vn[:, sl], NN) + bs_ref[g]
                y_ref[rows, sl] = (u[:, sl] * vm).astype(BF16)

    return _call(
        body, comm, 1, (pre, gain, w_s, b_s), name=name, grid=(T // tm,),
        in_specs=[pl.BlockSpec((tm, 2 * GH), lambda i: (i, 0)),
                  pl.BlockSpec((1, GH), lambda i: (0, 0)),
                  pl.BlockSpec((SGU_G, SGU_BLOCK, SGU_BLOCK), lambda i: (0, 0, 0)),
                  pl.BlockSpec((SGU_G, SGU_BLOCK, 1), lambda i: (0, 0, 0))],
        out_specs=pl.BlockSpec((tm, GH), lambda i: (i, 0)),
        out_shape=_sds((T, GH), BF16), compiler_params=_params(("parallel",)))


def sgu_bwd(pre, dy, gain, w_s, b_s, name, tm=SGU_BLOCK, comm=None):
    T = pre.shape[0]
    n = T // tm

    def body(pre_ref, dy_ref, gain_ref, ws_ref, bs_ref, dpre_ref, dws_ref, dbs_ref, dgain_ref, gacc_ref):
        i = pl.program_id(0)

        @pl.when(i == 0)
        def _():
            dws_ref[...] = jnp.zeros_like(dws_ref)
            dbs_ref[...] = jnp.zeros_like(dbs_ref)
            gacc_ref[...] = jnp.zeros_like(gacc_ref)

        mask = _sgu_mask()
        gain_v = gain_ref[...]
        for sb in range(tm // SGU_BLOCK):
            rows = slice(sb * SGU_BLOCK, (sb + 1) * SGU_BLOCK)
            u, du_dpre = _gelu_and_grad(pre_ref[rows, :GH].astype(F32))
            va, dva_dpre = _gelu_and_grad(pre_ref[rows, GH:].astype(F32))
            r = lax.rsqrt(jnp.mean(va * va, axis=-1, keepdims=True) + EPS)
            vhat = va * r
            vn = (vhat * gain_v).astype(BF16)
            dyf = dy_ref[rows, :].astype(F32)
            dvn_parts = []
            for grp in range(SGU_G):
                sl = slice(grp * SGU_GD, (grp + 1) * SGU_GD)
                wm = jnp.where(mask, ws_ref[grp], 0.0).astype(BF16)
                vm = _dot(wm, vn[:, sl], NN) + bs_ref[grp]
                dpre_ref[rows, sl] = ((dyf[:, sl] * vm) * du_dpre[:, sl]).astype(BF16)
                dvm = dyf[:, sl] * u[:, sl]
                dbs_ref[grp] += jnp.sum(dvm, axis=-1, keepdims=True)
                dvm16 = dvm.astype(BF16)
                dws_ref[grp] += jnp.where(mask, _dot(dvm16, vn[:, sl], NT), 0.0)
                dvn_parts.append(_dot(wm, dvm16, TN))
            dvn = jnp.concatenate(dvn_parts, axis=-1)
            gacc_ref[...] += (dvn * vhat).reshape(SGU_BLOCK // 8, 8, GH).sum(axis=0)
            dvhat = dvn * gain_v
            dva = r * (dvhat - vhat * jnp.mean(dvhat * vhat, axis=-1, keepdims=True))
            dpre_ref[rows, GH:] = (dva * dva_dpre).astype(BF16)

        @pl.when(i == n - 1)
        def _():
            dgain_ref[...] = jnp.sum(gacc_ref[...], axis=0, keepdims=True)

    const3 = lambda i: (0, 0, 0)
    return _call(
        body, comm, 4, (pre, dy, gain, w_s, b_s), name=name, grid=(n,),
        in_specs=[pl.BlockSpec((tm, 2 * GH), lambda i: (i, 0)),
                  pl.BlockSpec((tm, GH), lambda i: (i, 0)),
                  pl.BlockSpec((1, GH), lambda i: (0, 0)),
                  pl.BlockSpec((SGU_G, SGU_BLOCK, SGU_BLOCK), const3),
                  pl.BlockSpec((SGU_G, SGU_BLOCK, 1), const3)],
        out_specs=[pl.BlockSpec((tm, 2 * GH), lambda i: (i, 0)),
                   pl.BlockSpec((SGU_G, SGU_BLOCK, SGU_BLOCK), const3),
                   pl.BlockSpec((SGU_G, SGU_BLOCK, 1), const3),
                   pl.BlockSpec((1, GH), lambda i: (0, 0))],
        out_shape=[_sds((T, 2 * GH), BF16), _sds((SGU_G, SGU_BLOCK, SGU_BLOCK), F32),
                   _sds((SGU_G, SGU_BLOCK, 1), F32), _sds((1, GH), F32)],
        scratch_shapes=[pltpu.VMEM((8, GH), F32)],
        compiler_params=_params(("arbitrary",)))


DIAG = 768


def _diag_onehot():
    n = lax.broadcasted_iota(jnp.int32, (N_REL, DIAG), 1)
    r = lax.broadcasted_iota(jnp.int32, (N_REL, DIAG), 0)
    idx = jnp.clip(KW - 1 - n, REL_MIN, REL_MAX) - REL_MIN
    return (idx == r).astype(BF16)


def _split3(v):
    hi = v.astype(BF16)
    r1 = v - hi.astype(F32)
    mid = r1.astype(BF16)
    lo = (r1 - mid.astype(F32)).astype(BF16)
    return hi, mid, lo


def bias_build(rel_bias, name):
    def body(rb_ref, o_ref):
        oh = _diag_onehot()
        hi, mid, lo = _split3(rb_ref[...])
        u = (_dot(hi, oh, NN) + _dot(mid, oh, NN) + _dot(lo, oh, NN)) * LOG2E
        j = lax.broadcasted_iota(jnp.int32, (1, KW), 1)

        def row(i, carry):
            val = pltpu.roll(u, (i + (DIAG - QB + 1)) % DIAG, 1)[:, :KW]
            rel = lax.shift_right_logical(i, 6) - lax.shift_right_logical(j, 6) + 8
            ok = (rel >= 0) & (rel <= 8)
            o_ref[i] = jnp.where(ok, val, NEG)
            return carry

        lax.fori_loop(0, QB, row, 0)

    return _pallas(body, name=name, out_shape=_sds((QB, N_HEADS, KW), F32),
                   in_specs=[pl.BlockSpec(memory_space=pltpu.VMEM)],
                   out_specs=pl.BlockSpec(memory_space=pltpu.VMEM))(rel_bias)


def bias_grad(dwb, name):
    def body(d_ref, o_ref):
        def row(i, acc):
            return acc + pltpu.roll(d_ref[i], QB - 1 - i, 1)

        du = lax.fori_loop(0, QB, row, jnp.zeros((N_HEADS, DIAG), F32))
        oh = _diag_onehot()
        hi, mid, lo = _split3(du)
        o_ref[...] = _dot(hi, oh, NT) + _dot(mid, oh, NT) + _dot(lo, oh, NT)

    return _pallas(body, name=name, out_shape=_sds((N_HEADS, N_REL), F32),
                   in_specs=[pl.BlockSpec(memory_space=pltpu.VMEM)],
                   out_specs=pl.BlockSpec(memory_space=pltpu.VMEM))(dwb)


LOG2E = 1.4426950408889634
Q_SCALE = SCALE * LOG2E


def _attn_block(qkv_ref, blk, masked):
    r0 = pl.multiple_of(blk * QB, QB)
    qs = qkv_ref[0, pl.ds(r0 + FRONT, QB), :]
    kvalid = (lax.broadcasted_iota(jnp.int32, (1, KW), 1) >= PAD - blk * QB) if masked else None
    return r0, qs, kvalid


def _step_windows(qkv_ref, b, step):
    r0 = pl.multiple_of(b * step, QB) + (FRONT - PAD)
    out = []
    for part in (1, 2):
        a = qkv_ref[part, pl.ds(r0, PAD + step), :]
        zero = jnp.zeros_like(a)
        out.append([jnp.where(_head_mask(h), a, zero) for h in range(2)])
    return out


def _window(stacks, t):
    return jnp.concatenate([s[t * QB:t * QB + KW] for s in stacks], axis=0)


def _head_mask(h):
    lane = lax.broadcasted_iota(jnp.int32, (1, 2 * HEAD_DIM), 1)
    return (lane < HEAD_DIM) if h == 0 else (lane >= HEAD_DIM)


def _stack_heads(a):
    zero = jnp.zeros_like(a)
    return jnp.concatenate([jnp.where(_head_mask(0), a, zero), jnp.where(_head_mask(1), a, zero)], axis=0)


def _rows_by_head(a):
    return jnp.concatenate([a[:, :KW], a[:, KW:]], axis=0)


def _per_head(lo, hi):
    return jnp.where(_head_mask(0), lo, hi)


def _attn_exp(qs, kst, w_ref, kvalid):
    s = _dot(qs, kst, NT) + jnp.concatenate([w_ref[0], w_ref[1]], axis=1)
    if kvalid is not None:
        s = jnp.where(jnp.concatenate([kvalid, kvalid], axis=1), s, NEG)
    es, invs = [], []
    for h in range(2):
        sh = s[:, h * KW:(h + 1) * KW]
        eh = jnp.exp2(sh - jnp.max(sh, axis=-1, keepdims=True))
        es.append(eh)
        invs.append(1.0 / jnp.sum(eh, axis=-1, keepdims=True))
    return jnp.concatenate(es, axis=1), invs


ATTN_G = 16


def _blocks_per_step(T):
    return min(ATTN_G, T // QB)


def _masked_and_not(b, fn, step):
    n_masked = -(-PAD // step)
    pl.when(b < n_masked)(functools.partial(fn, True))
    pl.when(b >= n_masked)(functools.partial(fn, False))


def attn_fwd(qkvp, wb, name, comm=None):
    T = qkvp.shape[1] - FRONT
    G = _blocks_per_step(T)

    def body(qkv_ref, w_ref, o_ref):
        b = pl.program_id(1)

        def blocks(masked):
            keys, values = _step_windows(qkv_ref, b, G * QB)
            for t in range(G):
                _, qs, kvalid = _attn_block(qkv_ref, b * G + t, masked)
                e, inv = _attn_exp(qs, _window(keys, t), w_ref, kvalid)
                o = _dot(e.astype(BF16), _window(values, t), NN) * _per_head(*inv)
                o_ref[t * QB:(t + 1) * QB, :] = o.astype(BF16)

        _masked_and_not(b, blocks, G * QB)

    return _call(
        body, comm, 1, (qkvp, wb), name=name, grid=(N_HEADS // 2, T // (QB * G)),
        in_specs=[pl.BlockSpec((3, FRONT + T, 2 * HEAD_DIM), lambda hp, b: (0, 0, hp)),
                  pl.BlockSpec((2, QB, KW), lambda hp, b: (hp, 0, 0))],
        out_specs=pl.BlockSpec((QB * G, 2 * HEAD_DIM), lambda hp, b: (b, hp)),
        out_shape=_sds((T, D), BF16),
        compiler_params=_params(("parallel", "arbitrary"), VMEM_BIG))


def attn_bwd(qkvp, o, do, wb, name, comm=None):
    T = qkvp.shape[1] - FRONT
    G = _blocks_per_step(T)
    nb = T // (QB * G)

    def body(qkv_ref, o_ref, do_ref, w_ref, dqkv_ref, dw_ref, dk_acc, dv_acc):
        b = pl.program_id(1)

        @pl.when(b == 0)
        def _():
            dk_acc[...] = jnp.zeros_like(dk_acc)
            dv_acc[...] = jnp.zeros_like(dv_acc)
            dw_ref[...] = jnp.zeros_like(dw_ref)
            dqkv_ref[0, 0:FRONT, :] = jnp.zeros((FRONT, 2 * HEAD_DIM), BF16)

        def blocks(masked):
            dws = None
            keys, values = _step_windows(qkv_ref, b, G * QB)
            for t in range(G):
                r0, qs, kvalid = _attn_block(qkv_ref, b * G + t, masked)
                kst = _window(keys, t)
                e, inv = _attn_exp(qs, kst, w_ref, kvalid)
                do2 = do_ref[t * QB:(t + 1) * QB, :]
                dof = do2.astype(F32)
                prod = dof * o_ref[t * QB:(t + 1) * QB, :].astype(F32)
                dp = _dot(do2, _window(values, t), NT)
                parts = []
                for h in range(2):
                    delta = jnp.sum(jnp.where(_head_mask(h), prod, 0.0), axis=-1, keepdims=True)
                    half = slice(h * KW, (h + 1) * KW)
                    parts.append(e[:, half] * ((dp[:, half] - delta) * inv[h]))
                ds = jnp.concatenate(parts, axis=1)
                dws = ds if dws is None else dws + ds
                ds16 = ds.astype(BF16)
                dqkv_ref[0, pl.ds(r0 + FRONT, QB), :] = (_dot(ds16, kst, NN) * SCALE).astype(BF16)
                dk_acc[pl.ds(r0 + (FRONT - PAD), KW), :] += _dot(_rows_by_head(ds16), _stack_heads(qs), TN)
                dv_acc[pl.ds(r0 + (FRONT - PAD), KW), :] += _dot(
                    _rows_by_head(e.astype(BF16)), _stack_heads((dof * _per_head(*inv)).astype(BF16)), TN)
            dw_ref[0] += dws[:, :KW]
            dw_ref[1] += dws[:, KW:]

        _masked_and_not(b, blocks, G * QB)

        @pl.when(b == nb - 1)
        def _():
            dqkv_ref[1] = (dk_acc[...] * (1.0 / LOG2E)).astype(BF16)
            dqkv_ref[2] = dv_acc[...].astype(BF16)

    slab = pl.BlockSpec((3, FRONT + T, 2 * HEAD_DIM), lambda hp, b: (0, 0, hp))
    wspec = pl.BlockSpec((2, QB, KW), lambda hp, b: (hp, 0, 0))
    rows = pl.BlockSpec((QB * G, 2 * HEAD_DIM), lambda hp, b: (b, hp))
    return _call(
        body, comm, 2, (qkvp, o, do, wb), name=name, grid=(N_HEADS // 2, nb),
        in_specs=[slab, rows, rows, wspec],
        out_specs=[slab, wspec],
        out_shape=[_sds((3, FRONT + T, D), BF16), _sds((N_HEADS, QB, KW), F32)],
        scratch_shapes=[pltpu.VMEM((FRONT + T, 2 * HEAD_DIM), F32), pltpu.VMEM((FRONT + T, 2 * HEAD_DIM), F32)],
        compiler_params=_params(("parallel", "arbitrary"), VMEM_BIG))


def proj_qkv(hn, w, l, name, tm=512, comm=None):
    T = hn.shape[0]
    pb = FRONT // tm

    def body(a_ref, b_ref, o_ref):
        i = pl.program_id(1)

        @pl.when(i < pb)
        def _():
            o_ref[...] = jnp.zeros_like(o_ref)

        @pl.when(i >= pb)
        def _():
            scale = jnp.where(pl.program_id(0) == 0, Q_SCALE, 1.0).astype(F32)
            o_ref[...] = (_dot(a_ref[...], b_ref[...], NN) * scale).astype(BF16)

    return _call(
        body, comm, 1, (hn, w), name=name, grid=(3, pb + T // tm),
        in_specs=[pl.BlockSpec((tm, D), lambda p, i: (jnp.maximum(i - pb, 0), 0)),
                  pl.BlockSpec((None, D, D), lambda p, i: (l, 0, p))],
        out_specs=pl.BlockSpec((None, tm, D), lambda p, i: (p, i, 0)),
        out_shape=_sds((3, FRONT + T, D), BF16),
        compiler_params=_params(("parallel", "parallel"), VMEM_BIG))


def ffn_up(hn, wg, wu, l, name, tm=1024, comm=None):
    T = hn.shape[0]

    def body(a_ref, wg_ref, wu_ref, g_ref, u_ref, h_ref):
        a = a_ref[...]
        g = _dot(a, wg_ref[...], NT)
        u = _dot(a, wu_ref[...], NT)
        s = _sigmoid(g)
        silu = g * s
        g_ref[...] = (u * (s * (1.0 + g * (1.0 - s)))).astype(BF16)
        u_ref[...] = silu.astype(BF16)
        h_ref[...] = (silu * u).astype(BF16)

    wspec = pl.BlockSpec((None, None, FS, D), lambda s, i: (l, s, 0, 0))
    ospec = pl.BlockSpec((None, tm, FS), lambda s, i: (s, i, 0))
    return _call(
        body, comm, 3, (hn, wg, wu), name=name, grid=(N_CHIPS, T // tm),
        in_specs=[pl.BlockSpec((tm, D), lambda s, i: (i, 0)), wspec, wspec],
        out_specs=[ospec, ospec, ospec],
        out_shape=[_sds((N_CHIPS, T, FS), BF16)] * 3,
        compiler_params=_params(("parallel", "parallel"), VMEM_BIG))


def ffn_bwd_dh(dxb, wd, g, u, l, name, tm=2048, comm=None):
    T = dxb.shape[0]
    tm = min(tm, T)

    n_steps = (T // tm) * N_CHIPS
    RING = 3

    def body(a_ref, wd_ref, g_hbm, u_hbm, dg_ref, du_ref, gbuf, ubuf, sems):
        st = pl.program_id(0) * N_CHIPS + pl.program_id(1)

        def reads(step):
            rows = pl.ds(pl.multiple_of((step // N_CHIPS) * tm, tm), tm)
            slot = lax.rem(step, RING)
            return [pltpu.make_async_copy(src.at[lax.rem(step, N_CHIPS), rows, :], buf.at[slot], sems.at[k, slot])
                    for k, (src, buf) in enumerate(((g_hbm, gbuf), (u_hbm, ubuf)))]

        @pl.when(st == 0)
        def _():
            for first in range(min(RING - 1, n_steps)):
                for cp in reads(first):
                    cp.start()

        @pl.when(st + (RING - 1) < n_steps)
        def _():
            for cp in reads(st + (RING - 1)):
                cp.start()

        for cp in reads(st):
            cp.wait()
        slot = lax.rem(st, RING)
        dh = _dot(a_ref[...], wd_ref[...], NT)
        dg_ref[...] = (dh * gbuf[slot].astype(F32)).astype(BF16)
        du_ref[...] = (dh * ubuf[slot].astype(F32)).astype(BF16)

    aspec = pl.BlockSpec((None, tm, FS), lambda i, s: (s, i, 0))
    return _call(
        body, comm, 2, (dxb, wd, g, u), name=name, grid=(T // tm, N_CHIPS),
        in_specs=[pl.BlockSpec((tm, D), lambda i, s: (i, 0)),
                  pl.BlockSpec((None, None, FS, D), lambda i, s: (l, s, 0, 0)), ANY, ANY],
        out_specs=[aspec, aspec],
        out_shape=[_sds((N_CHIPS, T, FS), BF16)] * 2,
        scratch_shapes=[pltpu.VMEM((RING, tm, FS), BF16), pltpu.VMEM((RING, tm, FS), BF16),
                        pltpu.SemaphoreType.DMA((2, RING))],
        compiler_params=_params(("arbitrary", "arbitrary"), VMEM_BIG))


def ffn_dgrad(dg, du, wg, wu, tm=512):
    def compute(dg_ref, du_ref, wg_ref, wu_ref):
        d = None
        for s in range(N_CHIPS):
            t = _dot(dg_ref[s], wg_ref[s], NN) + _dot(du_ref[s], wu_ref[s], NN)
            d = t if d is None else d + t
        return d

    aspec = pl.BlockSpec((N_CHIPS, tm, FS), lambda i: (0, i, 0))
    wspec = pl.BlockSpec((None, N_CHIPS, FS, D), lambda i: (0, 0, 0, 0), pipeline_mode=pl.Buffered(1))
    return compute, (dg, du, wg, wu), [aspec, aspec, wspec, wspec]


def qkv_dgrad(dqkvp, w, tm=512):
    def compute(a_ref, w_ref):
        d = None
        for p in range(3):
            t = _dot(a_ref[p], w_ref[:, p * D:(p + 1) * D], NT)
            d = t if d is None else d + t
        return d

    return compute, (dqkvp, w), [pl.BlockSpec((3, tm, D), lambda i: (0, i + FRONT // tm, 0)),
                                 pl.BlockSpec((None, D, 3 * D), lambda i: (0, 0, 0))]


def in_dgrad(dpre, w, tm=512):
    def compute(a_ref, w_ref):
        return _dot(a_ref[...], w_ref[...], NT)

    return compute, (dpre, w), [pl.BlockSpec((tm, 2 * GH), lambda i: (i, 0)),
                                pl.BlockSpec((None, D, 2 * GH), lambda i: (0, 0, 0))]


def _rms_rows(x, g):
    r = lax.rsqrt(jnp.mean(x * x, axis=-1, keepdims=True) + EPS)
    return ((x * r) * g).astype(BF16)


def residual_proj(name, compute, args, specs, res, norm_g, tm=512, comm=None):
    T = res.shape[0]
    k = len(args)
    with_norm = norm_g is not None

    def body(*refs):
        d = refs[k][...] + compute(*refs[:k])
        if with_norm:
            refs[k + 2][...] = d
            refs[k + 3][...] = _rms_rows(d, refs[k + 1][...])
        else:
            refs[k + 1][...] = d

    row = pl.BlockSpec((tm, D), lambda i: (i, 0))
    vec = pl.BlockSpec((1, D), lambda i: (0, 0))
    if with_norm:
        return _call(body, comm, 2, (*args, res, norm_g), name=name, grid=(T // tm,),
                     in_specs=list(specs) + [row, vec], out_specs=[row, row],
                     out_shape=[_sds((T, D), F32), _sds((T, D), BF16)],
                     compiler_params=_params(("parallel",), VMEM_BIG))
    return _call(body, comm, 1, (*args, res), name=name, grid=(T // tm,), in_specs=list(specs) + [row],
                 out_specs=row, out_shape=_sds((T, D), F32), compiler_params=_params(("parallel",), VMEM_BIG))


def ffn_down(h, wd, tm=512):
    def compute(h_ref, wd_ref):
        d = None
        for s in range(N_CHIPS):
            t = _dot(h_ref[s], wd_ref[s], NN)
            d = t if d is None else d + t
        return d

    return compute, (h, wd), [pl.BlockSpec((N_CHIPS, tm, FS), lambda i: (0, i, 0)),
                              pl.BlockSpec((None, N_CHIPS, FS, D), lambda i: (0, 0, 0, 0))]


def out_proj(a, w, tm=512):
    K = a.shape[1]

    def compute(a_ref, w_ref):
        return _dot(a_ref[...], w_ref[...], NN)

    return compute, (a, w), [pl.BlockSpec((tm, K), lambda i: (i, 0)), pl.BlockSpec((None, K, D), lambda i: (0, 0, 0))]


def adamw(w, g, m, v, name):
    L, R, C = w.shape

    def body(w_ref, g_ref, m_ref, v_ref, go_ref, d_ref, nm_ref, nv_ref):
        gf = g_ref[...]
        go_ref[...] = gf
        nm = ADAM_B1 * m_ref[...] + (1.0 - ADAM_B1) * gf
        nv = ADAM_B2 * v_ref[...] + (1.0 - ADAM_B2) * (gf * gf)
        m_hat = nm / (1.0 - ADAM_B1 ** ADAM_STEP)
        v_hat = nv / (1.0 - ADAM_B2 ** ADAM_STEP)
        d_ref[...] = -ADAM_LR * (m_hat / (jnp.sqrt(v_hat) + ADAM_EPS) + ADAM_WD * w_ref[...])
        nm_ref[...] = nm
        nv_ref[...] = nv

    tr = R // 4 if R % 32 == 0 else R
    spec = pl.BlockSpec((None, tr, C), lambda l, r: (l, r, 0))
    return _pallas(body, name=name, grid=(L, R // tr), in_specs=[spec] * 4, out_specs=[spec] * 4,
                   out_shape=[_sds((L, R, C), F32)] * 4,
                   compiler_params=_params(("parallel", "parallel")))(w, g, m, v)


def _coords():
    return lax.axis_index("x"), lax.axis_index("y"), lax.axis_index("c")


def _other_chips(x, y):
    out = []
    for fx, fy in ((1, 0), (0, 1), (1, 1)):
        px = (1 - x) if fx else x
        py = (1 - y) if fy else y
        out.append((px, py))
    return out


def _flip_index(s, j):
    sx, sy = s // 2, s % 2
    fx, fy = ((1, 0), (0, 1), (1, 1))[j]
    return 2 * (sx ^ fx) + (sy ^ fy)


def _for_my_chip(sme, fn):
    for s in range(N_CHIPS):
        pl.when(sme == s)(functools.partial(fn, s))


ANY = pl.BlockSpec(memory_space=pl.ANY)

GATHER_KIND = {"a_w_in": "col", "b_w_qkv": "col", "a_w_out": "row", "b_w_out": "row",
               "ffn_w_gate": "row", "ffn_w_up": "row", "ffn_w_down": "row"}
BIG = tuple(GATHER_KIND)


def _gathered_shape(kind, shape):
    L, R, C = shape
    return (L, R, N_CHIPS * C) if kind == "col" else (L, N_CHIPS, R, C)


def _shard_rows(ref, kind, s, r0, rn, C):
    if kind == "col":
        return ref.at[:, pl.ds(r0, rn), s * C:(s + 1) * C]
    return ref.at[:, s, pl.ds(r0, rn), :]


def gather_stage1(items):
    n = len(items)
    dims = [it[0].shape[1:] for it in items]

    def copies(ins, outs, sems, s, with_landed=True):
        lsem, ssem, rsem = sems
        x, y, c = _coords()
        chips = _other_chips(x, y)
        local, send, landed = [], [], []
        for t, (_, li, kind) in enumerate(items):
            R, C = dims[t]
            r0 = pl.multiple_of(c * (R // 2), 8)
            local.append(pltpu.make_async_copy(ins[t].at[pl.ds(li, 1)], _shard_rows(outs[t], kind, s, 0, R, C),
                                               lsem.at[t]))
            for j in range(3):
                pair = dict(send_sem=ssem.at[3 * t + j], recv_sem=rsem.at[3 * t + j],
                            device_id=(chips[j][0], chips[j][1], c), device_id_type=MESH)
                send.append(pltpu.make_async_remote_copy(
                    src_ref=ins[t].at[pl.ds(li, 1), pl.ds(r0, R // 2), :],
                    dst_ref=_shard_rows(outs[t], kind, s, r0, R // 2, C), **pair))
                if with_landed:
                    got = _shard_rows(outs[t], kind, _flip_index(s, j), r0, R // 2, C)
                    landed.append(pltpu.make_async_remote_copy(src_ref=got, dst_ref=got, **pair))
        return local, send, landed

    def start(ins, outs, sems):
        def run(s):
            local, send, _ = copies(ins, outs, sems, s, with_landed=False)
            for cp in local + send:
                cp.start()
        x, y, _ = _coords()
        _for_my_chip(2 * x + y, run)

    def wait(ins, outs, sems):
        def run(s):
            local, send, landed = copies(ins, outs, sems, s)
            for cp in landed:
                cp.wait_recv()
            for cp in send:
                cp.wait_send()
            for cp in local:
                cp.wait()
        x, y, _ = _coords()
        _for_my_chip(2 * x + y, run)

    out_shapes = [_sds(_gathered_shape(kind, (1,) + tuple(dims[t])), BF16) for t, (_, _, kind) in enumerate(items)]
    sems = [pltpu.SemaphoreType.DMA((n,)), pltpu.SemaphoreType.DMA((3 * n,)), pltpu.SemaphoreType.DMA((3 * n,))]
    return Comm([it[0] for it in items], out_shapes, sems, start, wait)


def gather_stage2(items, gathered):
    n = len(items)
    dims = [it[0].shape[1:] for it in items]

    def copies(outs, sems, s, with_landed=True):
        ssem, rsem = sems
        x, y, c = _coords()
        send, landed = [], []
        for t, (_, _, kind) in enumerate(items):
            R, C = dims[t]
            for j in range(3):
                pair = dict(send_sem=ssem.at[3 * t + j], recv_sem=rsem.at[3 * t + j],
                            device_id=(x, y, 1 - c), device_id_type=MESH)
                mine = _shard_rows(outs[t], kind, _flip_index(s, j), pl.multiple_of(c * (R // 2), 8), R // 2, C)
                send.append(pltpu.make_async_remote_copy(src_ref=mine, dst_ref=mine, **pair))
                if with_landed:
                    other = _shard_rows(outs[t], kind, _flip_index(s, j), pl.multiple_of((1 - c) * (R // 2), 8),
                                        R // 2, C)
                    landed.append(pltpu.make_async_remote_copy(src_ref=other, dst_ref=other, **pair))
        return send, landed

    def start(ins, outs, sems):
        def run(s):
            for cp in copies(outs, sems, s, with_landed=False)[0]:
                cp.start()
        x, y, _ = _coords()
        _for_my_chip(2 * x + y, run)

    def wait(ins, outs, sems):
        def run(s):
            send, landed = copies(outs, sems, s)
            for cp in landed:
                cp.wait_recv()
            for cp in send:
                cp.wait_send()
        x, y, _ = _coords()
        _for_my_chip(2 * x + y, run)

    out_shapes = [_sds(g.shape, BF16) for g in gathered]
    sems = [pltpu.SemaphoreType.DMA((3 * n,)), pltpu.SemaphoreType.DMA((3 * n,))]
    return Comm(gathered, out_shapes, sems, start, wait, aliases={t: t for t in range(n)})


def _half_shape(kind, R, C):
    return (R // 2, N_CHIPS * C) if kind == "col" else (N_CHIPS, R // 2, C)


def exchange_halves(grads, metas):
    n = len(grads)

    def copies(ins, outs, sems):
        ssem, rsem = sems
        x, y, c = _coords()
        out = []
        for t, (kind, R, C) in enumerate(metas):
            r0 = pl.multiple_of((1 - c) * (R // 2), 8)
            src = ins[t].at[pl.ds(r0, R // 2), :] if kind == "col" else ins[t].at[:, pl.ds(r0, R // 2), :]
            out.append(pltpu.make_async_remote_copy(
                src_ref=src, dst_ref=outs[t], send_sem=ssem.at[t], recv_sem=rsem.at[t],
                device_id=(x, y, 1 - c), device_id_type=MESH))
        return out

    def start(ins, outs, sems):
        for cp in copies(ins, outs, sems):
            cp.start()

    def wait(ins, outs, sems):
        for cp in copies(ins, outs, sems):
            cp.wait()

    return Comm(grads, [_sds(_half_shape(*m), F32) for m in metas], [pltpu.SemaphoreType.DMA((n,))] * 2, start, wait)


def pair_sum(me, g, sib, meta, name):
    kind, R, C = meta
    h = R // 2

    def body(me_ref, g_ref, sib_ref, p16_ref, own_ref):
        s = pl.program_id(0)
        v = g_ref[...] + sib_ref[...]
        p16_ref[...] = v.astype(BF16)

        @pl.when(s == me_ref[1])
        def _():
            own_ref[...] = v

    if kind == "col":
        gspec = pl.BlockSpec((h, C), lambda s, me_ref: (me_ref[0], s))
        sspec = pl.BlockSpec((h, C), lambda s, me_ref: (0, s))
    else:
        gspec = pl.BlockSpec((None, h, C), lambda s, me_ref: (s, me_ref[0], 0))
        sspec = pl.BlockSpec((None, h, C), lambda s, me_ref: (s, 0, 0))
    grid_spec = pltpu.PrefetchScalarGridSpec(
        num_scalar_prefetch=1, grid=(N_CHIPS,), in_specs=[gspec, sspec],
        out_specs=[sspec, pl.BlockSpec((h, C), lambda s, me_ref: (0, 0))])
    return _pallas(body, name=name, grid_spec=grid_spec,
                   out_shape=[_sds(_half_shape(*meta), BF16), _sds((h, C), F32)],
                   compiler_params=_params(("arbitrary",), VMEM_BIG))(me, g, sib)


def scatter_partials(p16s, metas):
    n = len(p16s)

    def copies(ins, outs, sems, s):
        ssem, rsem = sems
        x, y, c = _coords()
        chips = _other_chips(x, y)
        out = []
        for t, (kind, R, C) in enumerate(metas):
            for j in range(3):
                sj = _flip_index(s, j)
                src = ins[t].at[:, sj * C:(sj + 1) * C] if kind == "col" else ins[t].at[sj]
                out.append(pltpu.make_async_remote_copy(
                    src_ref=src, dst_ref=outs[t].at[j], send_sem=ssem.at[3 * t + j], recv_sem=rsem.at[3 * t + j],
                    device_id=(chips[j][0], chips[j][1], c), device_id_type=MESH))
        return out

    def start(ins, outs, sems):
        def run(s):
            for cp in copies(ins, outs, sems, s):
                cp.start()
        x, y, _ = _coords()
        _for_my_chip(2 * x + y, run)

    def wait(ins, outs, sems):
        def run(s):
            for cp in copies(ins, outs, sems, s):
                cp.wait()
        x, y, _ = _coords()
        _for_my_chip(2 * x + y, run)

    return Comm(p16s, [_sds((3, R // 2, C), BF16) for (_, R, C) in metas],
                [pltpu.SemaphoreType.DMA((3 * n,))] * 2, start, wait)


def final_sum(me, own, q, buf, shape, l, meta, name):
    _, R, C = meta
    h = R // 2

    def body(me_ref, own_ref, q_ref, *rest):
        rest[-1][...] = ((own_ref[...] + q_ref[0].astype(F32)) + q_ref[1].astype(F32)) + q_ref[2].astype(F32)

    grid_spec = pltpu.PrefetchScalarGridSpec(
        num_scalar_prefetch=1, grid=(1,),
        in_specs=[pl.BlockSpec((h, C), lambda i, me_ref: (0, 0)),
                  pl.BlockSpec((3, h, C), lambda i, me_ref: (0, 0, 0))] + ([] if buf is None else [ANY]),
        out_specs=pl.BlockSpec((None, h, C), lambda i, me_ref: (l, me_ref[0], 0)))
    alias = {} if buf is None else {"input_output_aliases": {3: 0}}
    args = (me, own, q) if buf is None else (me, own, q, buf)
    return _pallas(body, name=name, grid_spec=grid_spec, out_shape=_sds(shape, F32),
                   compiler_params=_params(("arbitrary",), VMEM_BIG), **alias)(*args)


def share_final(bufs):
    n = len(bufs)

    def body(*refs):
        ins, outs = refs[:n], refs[n:2 * n]
        ssem, rsem = refs[2 * n:]
        del ins
        x, y, c = _coords()
        copies = []
        for t in range(n):
            R = bufs[t].shape[1]
            r0 = pl.multiple_of(c * (R // 2), 8)
            blk = outs[t].at[:, pl.ds(r0, R // 2), :]
            copies.append(pltpu.make_async_remote_copy(
                src_ref=blk, dst_ref=blk, send_sem=ssem.at[t], recv_sem=rsem.at[t],
                device_id=(x, y, 1 - c), device_id_type=MESH))
        for cp in copies:
            cp.start()
        for t in range(n):
            R = bufs[t].shape[1]
            r1 = pl.multiple_of((1 - c) * (R // 2), 8)
            other = outs[t].at[:, pl.ds(r1, R // 2), :]
            pltpu.make_async_remote_copy(
                src_ref=other, dst_ref=other, send_sem=ssem.at[t], recv_sem=rsem.at[t],
                device_id=(x, y, 1 - c), device_id_type=MESH).wait_recv()
        for cp in copies:
            cp.wait_send()

    out_shape = [_sds(b.shape, F32) for b in bufs]
    return _pallas(body, name="share_final", in_specs=[ANY] * n, out_specs=[ANY] * n, out_shape=out_shape,
                   input_output_aliases={t: t for t in range(n)},
                   scratch_shapes=[pltpu.SemaphoreType.DMA((n,))] * 2,
                   compiler_params=pltpu.CompilerParams(has_side_effects=True))(*bufs)


def allreduce_small(part):
    rows = part.shape[0]
    h = rows // 2

    def body(p_ref, o_ref, sib_buf, pair_buf, chip_buf, ssem, rsem):
        x, y, c = _coords()
        sibling = dict(device_id=(x, y, 1 - c), device_id_type=MESH)
        mine = pl.ds(pl.multiple_of(c * h, 8), h)
        theirs = pl.ds(pl.multiple_of((1 - c) * h, 8), h)

        swap = pltpu.make_async_remote_copy(src_ref=p_ref.at[theirs], dst_ref=sib_buf, send_sem=ssem.at[0],
                                            recv_sem=rsem.at[0], **sibling)
        swap.start()
        swap.wait()
        pair_buf[...] = p_ref[mine, :] + sib_buf[...]

        chips = _other_chips(x, y)
        sends = [pltpu.make_async_remote_copy(src_ref=pair_buf, dst_ref=chip_buf.at[j], send_sem=ssem.at[1 + j],
                                              recv_sem=rsem.at[1 + j], device_id=(chips[j][0], chips[j][1], c),
                                              device_id_type=MESH) for j in range(3)]
        for cp in sends:
            cp.start()
        for cp in sends:
            cp.wait()

        def total(s):
            terms = {s: pair_buf[...]}
            for j in range(3):
                terms[_flip_index(s, j)] = chip_buf[j]
            o_ref[mine, :] = ((terms[0] + terms[1]) + terms[2]) + terms[3]

        _for_my_chip(2 * x + y, total)

        back = pltpu.make_async_remote_copy(src_ref=o_ref.at[mine], dst_ref=o_ref.at[mine], send_sem=ssem.at[4],
                                            recv_sem=rsem.at[4], **sibling)
        back.start()
        pltpu.make_async_remote_copy(src_ref=o_ref.at[theirs], dst_ref=o_ref.at[theirs], send_sem=ssem.at[4],
                                     recv_sem=rsem.at[4], **sibling).wait_recv()
        back.wait_send()

    return _pallas(body, name="allreduce_small",
                   in_specs=[pl.BlockSpec(memory_space=pltpu.VMEM)], out_specs=pl.BlockSpec(memory_space=pltpu.VMEM),
                   out_shape=_sds((rows, 128), F32),
                   scratch_shapes=[pltpu.VMEM((h, 128), F32), pltpu.VMEM((h, 128), F32), pltpu.VMEM((3, h, 128), F32),
                                   pltpu.SemaphoreType.DMA((5,)), pltpu.SemaphoreType.DMA((5,))],
                   compiler_params=pltpu.CompilerParams(has_side_effects=True))(part)


def _rows128(a):
    flat = a.reshape(-1)
    rows = -(-flat.shape[0] // 128)
    rows8 = -(-rows // 8) * 8
    flat = jnp.pad(flat, (0, rows8 * 128 - flat.shape[0]))
    return flat.reshape(rows8, 128)


def kernel(x, norm_mix_g, norm_ffn_g, final_g, a_w_in, a_v_gain, a_w_s, a_b_s, a_w_out, b_w_qkv, b_rel_bias, b_w_out, ffn_w_gate, ffn_w_up, ffn_w_down, loss_target, m_norm_mix_g, m_norm_ffn_g, m_final_g, m_a_w_in, m_a_v_gain, m_a_w_s, m_a_b_s, m_a_w_out, m_b_w_qkv, m_b_rel_bias, m_b_w_out, m_ffn_w_gate, m_ffn_w_up, m_ffn_w_down, v_norm_mix_g, v_norm_ffn_g, v_final_g, v_a_w_in, v_a_v_gain, v_a_w_s, v_a_b_s, v_a_w_out, v_b_w_qkv, v_b_rel_bias, v_b_w_out, v_ffn_w_gate, v_ffn_w_up, v_ffn_w_down):
    T = x.shape[1]
    weights = dict(norm_mix_g=norm_mix_g, norm_ffn_g=norm_ffn_g, final_g=final_g, a_w_in=a_w_in, a_v_gain=a_v_gain,
                   a_w_s=a_w_s, a_b_s=a_b_s, a_w_out=a_w_out, b_w_qkv=b_w_qkv, b_rel_bias=b_rel_bias,
                   b_w_out=b_w_out, ffn_w_gate=ffn_w_gate, ffn_w_up=ffn_w_up, ffn_w_down=ffn_w_down)
    mom_m = dict(norm_mix_g=m_norm_mix_g, norm_ffn_g=m_norm_ffn_g, final_g=m_final_g, a_w_in=m_a_w_in,
                 a_v_gain=m_a_v_gain, a_w_s=m_a_w_s, a_b_s=m_a_b_s, a_w_out=m_a_w_out, b_w_qkv=m_b_w_qkv,
                 b_rel_bias=m_b_rel_bias, b_w_out=m_b_w_out, ffn_w_gate=m_ffn_w_gate, ffn_w_up=m_ffn_w_up,
                 ffn_w_down=m_ffn_w_down)
    mom_v = dict(norm_mix_g=v_norm_mix_g, norm_ffn_g=v_norm_ffn_g, final_g=v_final_g, a_w_in=v_a_w_in,
                 a_v_gain=v_a_v_gain, a_w_s=v_a_w_s, a_b_s=v_a_b_s, a_w_out=v_a_w_out, b_w_qkv=v_b_w_qkv,
                 b_rel_bias=v_b_rel_bias, b_w_out=v_b_w_out, ffn_w_gate=v_ffn_w_gate, ffn_w_up=v_ffn_w_up,
                 ffn_w_down=v_ffn_w_down)
    order = list(weights)
    transposed = ("ffn_w_gate", "ffn_w_up")
    for k in transposed:
        weights[k], mom_m[k], mom_v[k] = (jnp.swapaxes(a, 1, 2) for a in (weights[k], mom_m[k], mom_v[k]))

    xi, yi, ci = _coords()
    me = jnp.stack([ci, 2 * xi + yi]).astype(jnp.int32)

    shard16 = {k: cast_bf16(weights[k], "cast_" + k) for k in BIG}

    def layer_tensors(i):
        mix = ("a_w_in", "a_w_out") if i % 2 == 0 else ("b_w_qkv", "b_w_out")
        return [(k, i // 2) for k in mix] + [(k, i) for k in ("ffn_w_gate", "ffn_w_up", "ffn_w_down")]

    def gather_items(keys):
        return [(shard16[k], l, GATHER_KIND[k]) for k, l in keys]

    def grad_metas(keys):
        return [(GATHER_KIND[k],) + tuple(weights[k].shape[1:]) for k, _ in keys]

    FFN = ("ffn_w_gate", "ffn_w_up", "ffn_w_down")
    k0a = [("a_w_out", 0), ("ffn_w_gate", 0)]
    k0b = [("ffn_w_up", 0), ("ffn_w_down", 0)]
    k1a = [("b_w_qkv", 0), ("b_w_out", 0), ("ffn_w_gate", 1)]
    k1b = [("ffn_w_up", 1), ("ffn_w_down", 1)]
    k3a = [("b_w_qkv", 1), ("b_w_out", 1), ("ffn_w_gate", 3)]
    k3b = [("ffn_w_up", 3), ("ffn_w_down", 3)]
    plans = {
        "a_in_l0": [("g1", k0a)], "sgu_fwd_l0": [("g2", k0a), ("g1", k0b)], "a_out_l0": [("g2", k0b)],
        "rms_mix_l0": [("g1", [("a_w_in", 0)])],
        "ffn_up_l0": [("g1", k1a)], "ffn_down_l0": [("g2", k1a), ("g1", k1b[:1])],
        "b_qkv_l1": [("g2", k1b[:1]), ("g1", k1b[1:])],
        "attn_fwd_l1": [("g2", k1b[1:]), ("g1", layer_tensors(2))], "b_out_l1": [("g2", layer_tensors(2))],
        "ffn_up_l1": [("g1", k3a)], "ffn_down_l1": [("g2", k3a)],
        "a_in_l2": [("g1", k3b)], "sgu_fwd_l2": [("g2", k3b)],
        "ffn_bwd_dhn_l2": [("ex", layer_tensors(3))], "sgu_bwd_l2": [("sc", layer_tensors(3))],
        "ffn_bwd_dhn_l1": [("ex", layer_tensors(2))], "attn_bwd_l1": [("sc", layer_tensors(2))],
        "dw_down_l0": [("ex", layer_tensors(1))], "ffn_bwd_dhn_l0": [("sc", k1a)],
        "dffn_w_gate_l0": [("sc", [("ffn_w_up", 1)])], "dffn_w_up_l0": [("sc", [("ffn_w_down", 1)])],
        "a_out_bwd_l0": [("ex", [(k, 0) for k in FFN])],
        "sgu_bwd_l0": [("sc", [("ffn_w_gate", 0), ("ffn_w_up", 0)]), ("ex", [("a_w_out", 0)])],
        "dw_in_l0": [("sc", [("ffn_w_down", 0), ("a_w_out", 0)])],
    }
    part16, full16 = {}, {}
    sib, p16, own_parts, recv_parts = {}, {}, {}, {}

    def make_comm(kind, keys):
        if kind == "g1":
            return gather_stage1(gather_items(keys)), lambda outs: part16.update(zip(keys, outs))
        if kind == "g2":
            return (gather_stage2(gather_items(keys), [part16[kl] for kl in keys]),
                    lambda outs: full16.update(zip(keys, outs)))
        if kind == "ex":
            return (exchange_halves([big_grads[k][l] for k, l in keys], grad_metas(keys)),
                    lambda outs: sib.update(zip(keys, outs)))
        for kl, m_ in zip(keys, grad_metas(keys)):
            p16[kl], own_parts[kl] = pair_sum(me, big_grads[kl[0]][kl[1]], sib[kl], m_, "pair_sum_%s_l%d" % kl)
        return (scatter_partials([p16[kl] for kl in keys], grad_metas(keys)),
                lambda outs: recv_parts.update(zip(keys, outs)))

    def run(name, make):
        steps = plans.get(name)
        if not steps:
            return make(None)
        made = [make_comm(kind, keys) for kind, keys in steps]
        main, outs = make(combine([c for c, _ in made]))
        for c, done in made:
            done(outs[:len(c.out_shapes)])
            outs = outs[len(c.out_shapes):]
        return main

    def weight(k, l):
        w = full16[(k, l)]
        if k == "a_w_out":
            return w.reshape(1, GH, D)
        return w.reshape(1, D, D) if k == "b_w_out" else w


    xcur = x.reshape(T, D)
    hn = run("rms_mix_l0", lambda comm: rms_fwd(xcur, norm_mix_g[0][None], "rms_mix_l0", comm=comm))
    comm, done = make_comm("g2", [("a_w_in", 0)])
    done(run_comm(comm, "gather_first_d2d"))
    saved = []
    for i in range(DEPTH):
        j = i // 2
        tag = "_l%d" % i
        st = {"x_in": xcur, "hn": hn}
        if i % 2 == 0:
            pre = run("a_in" + tag, lambda comm: matmul(
                "a_in" + tag, NN, hn, pl.BlockSpec((1024, D), lambda i_, j_: (i_, 0)),
                weight("a_w_in", j), pl.BlockSpec((None, D, 1024), lambda i_, j_: (0, 0, j_)),
                _sds((T, 2 * GH), BF16), pl.BlockSpec((1024, 1024), lambda i_, j_: (i_, j_)),
                (T // 1024, 4), comm=comm))
            y = run("sgu_fwd" + tag, lambda comm: sgu_fwd(
                pre, a_v_gain[j][None], a_w_s[j], a_b_s[j][:, :, None], "sgu_fwd" + tag, comm=comm))
            xmid, hn2 = run("a_out" + tag, lambda comm: residual_proj(
                "a_out" + tag, *out_proj(y, weight("a_w_out", j)), xcur, norm_ffn_g[i][None], comm=comm))
            st.update(pre=pre, y=y)
        else:
            qkvp = run("b_qkv" + tag, lambda comm: proj_qkv(hn, weight("b_w_qkv", j), 0, "b_qkv" + tag, comm=comm))
            wb = jnp.transpose(bias_build(b_rel_bias[j], "bias_build" + tag), (1, 0, 2))
            o = run("attn_fwd" + tag, lambda comm: attn_fwd(qkvp, wb, "attn_fwd" + tag, comm=comm))
            xmid, hn2 = run("b_out" + tag, lambda comm: residual_proj(
                "b_out" + tag, *out_proj(o, weight("b_w_out", j)), xcur, norm_ffn_g[i][None], comm=comm))
            st.update(qkvp=qkvp, wb=wb, o=o)
        g, u, h = run("ffn_up" + tag, lambda comm: ffn_up(
            hn2, weight("ffn_w_gate", i), weight("ffn_w_up", i), 0, "ffn_up" + tag, comm=comm))
        next_g = norm_mix_g[i + 1][None] if i + 1 < DEPTH else None
        down = run("ffn_down" + tag, lambda comm: residual_proj(
            "ffn_down" + tag, *ffn_down(h, weight("ffn_w_down", i)), xmid, next_g, comm=comm))
        xcur, hn = down if next_g is not None else (down, None)
        st.update(x_mid=xmid, hn2=hn2, g=g, u=u, h=h)
        saved.append(st)

    loss_part, dx, dxb, d_final = final_loss(xcur, final_g[None], loss_target.reshape(T, D), "final_loss")

    tk = min(2048, T)
    big_grads = {k: [None] * weights[k].shape[0] for k in BIG}
    small = {"norm_mix_g": [None] * DEPTH, "norm_ffn_g": [None] * DEPTH, "a_v_gain": [None] * 2,
             "a_w_s": [None] * 2, "a_b_s": [None] * 2, "b_rel_bias": [None] * 2}
    tok = lambda width: pl.BlockSpec((tk, width), lambda j_, k_: (k_, 0))
    part = lambda: pl.BlockSpec((None, tk, FS), lambda j_, k_: (j_, k_, 0))
    for i in reversed(range(DEPTH)):
        j = i // 2
        tag = "_l%d" % i
        st = saved[i]
        dg, du = run("ffn_bwd_dh" + tag, lambda comm: ffn_bwd_dh(
            dxb, weight("ffn_w_down", i), st["g"], st["u"], 0, "ffn_bwd_dh" + tag, comm=comm))
        big_grads["ffn_w_down"][i] = run("dw_down" + tag, lambda comm: wgrad(
            "dw_down" + tag, st["h"], part(), dxb, tok(D), _sds((N_CHIPS, FS, D), F32),
            pl.BlockSpec((None, FS, D), lambda j_, k_: (j_, 0, 0)), N_CHIPS, T, tk, comm=comm))
        dx_mid, dxb_mid, dgn = run("ffn_bwd_dhn" + tag, lambda comm: dgrad_rms(
            "ffn_bwd_dhn" + tag, *ffn_dgrad(dg, du, weight("ffn_w_gate", i), weight("ffn_w_up", i)),
            st["x_mid"], norm_ffn_g[i][None], dx, comm=comm))
        for nm, dz in (("ffn_w_gate", dg), ("ffn_w_up", du)):
            big_grads[nm][i] = run("d" + nm + tag, lambda comm: wgrad(
                "d" + nm + tag, dz, part(), st["hn2"], tok(D), _sds((N_CHIPS, FS, D), F32),
                pl.BlockSpec((None, FS, D), lambda j_, k_: (j_, 0, 0)), N_CHIPS, T, tk, comm=comm))
        dx, dxb = dx_mid, dxb_mid
        small["norm_ffn_g"][i] = dgn
        if i % 2 == 0:
            dy = run("a_out_bwd" + tag, lambda comm: matmul(
                "a_out_bwd" + tag, NT, dxb, pl.BlockSpec((1024, D), lambda i_, j_: (i_, 0)),
                weight("a_w_out", j), pl.BlockSpec((None, 1024, D), lambda i_, j_: (0, j_, 0)),
                _sds((T, GH), BF16), pl.BlockSpec((1024, 1024), lambda i_, j_: (i_, j_)), (T // 1024, 2), comm=comm))
            big_grads["a_w_out"][j] = wgrad(
                "dw_aout" + tag, st["y"], pl.BlockSpec((tk, 1024), lambda j_, k_: (k_, j_)), dxb, tok(D),
                _sds((GH, D), F32), pl.BlockSpec((1024, D), lambda j_, k_: (j_, 0)), 2, T, tk
            ).reshape(N_CHIPS, GH // N_CHIPS, D)
            dpre, d_ws, d_bs, d_gain = run("sgu_bwd" + tag, lambda comm: sgu_bwd(
                st["pre"], dy, a_v_gain[j][None], a_w_s[j], a_b_s[j][:, :, None], "sgu_bwd" + tag,
                tm=min(4 * SGU_BLOCK, T), comm=comm))
            small["a_w_s"][j], small["a_b_s"][j], small["a_v_gain"][j] = d_ws, d_bs, d_gain
            dx_in, dxb_in, dgn = run("a_in_bwd" + tag, lambda comm: dgrad_rms(
                "a_in_bwd" + tag, *in_dgrad(dpre, weight("a_w_in", j)),
                st["x_in"], norm_mix_g[i][None], dx, comm=comm))
            big_grads["a_w_in"][j] = run("dw_in" + tag, lambda comm: wgrad(
                "dw_in" + tag, st["hn"], tok(D), dpre, pl.BlockSpec((tk, 1024), lambda j_, k_: (k_, j_)),
                _sds((D, 2 * GH), F32), pl.BlockSpec((D, 1024), lambda j_, k_: (0, j_)), 4, T, tk, comm=comm))
        else:
            do = matmul("b_out_bwd" + tag, NT, dxb, pl.BlockSpec((1024, D), lambda i_, j_: (i_, 0)),
                        weight("b_w_out", j), pl.BlockSpec((None, D, D), lambda i_, j_: (0, 0, 0)),
                        _sds((T, D), BF16), pl.BlockSpec((1024, D), lambda i_, j_: (i_, 0)), (T // 1024, 1))
            big_grads["b_w_out"][j] = wgrad(
                "dw_bout" + tag, st["o"], tok(D), dxb, tok(D),
                _sds((D, D), F32), pl.BlockSpec((D, D), lambda j_, k_: (0, 0)), 1, T, tk
            ).reshape(N_CHIPS, D // N_CHIPS, D)
            dqkvp, dwb = run("attn_bwd" + tag, lambda comm: attn_bwd(
                st["qkvp"], st["o"], do, st["wb"], "attn_bwd" + tag, comm=comm))
            small["b_rel_bias"][j] = bias_grad(
                jnp.pad(jnp.transpose(dwb, (1, 0, 2)), ((0, 0), (0, 0), (0, DIAG - KW))), "bias_grad" + tag)
            dx_in, dxb_in, dgn = dgrad_rms(
                "b_qkv_bwd" + tag, *qkv_dgrad(dqkvp, weight("b_w_qkv", j)),
                st["x_in"], norm_mix_g[i][None], dx)
            big_grads["b_w_qkv"][j] = wgrad(
                "dw_qkv" + tag, st["hn"], tok(D), dqkvp,
                pl.BlockSpec((None, tk, D), lambda j_, k_: (j_, k_ + FRONT // tk, 0)),
                _sds((D, 3 * D), F32), pl.BlockSpec((D, D), lambda j_, k_: (0, j_)), 3, T, tk)
        dx, dxb = dx_in, dxb_in
        small["norm_mix_g"][i] = dgn

    small_grads = {
        "norm_mix_g": jnp.concatenate(small["norm_mix_g"], axis=0),
        "norm_ffn_g": jnp.concatenate(small["norm_ffn_g"], axis=0),
        "final_g": d_final.reshape(D),
        "a_v_gain": jnp.concatenate(small["a_v_gain"], axis=0),
        "a_w_s": jnp.stack(small["a_w_s"]),
        "a_b_s": jnp.stack(small["a_b_s"]).reshape(2, SGU_G, SGU_BLOCK),
        "b_rel_bias": jnp.stack(small["b_rel_bias"]),
    }
    small_names = list(small_grads)
    packed = [_rows128(small_grads[k]) for k in small_names] + [_rows128(loss_part[:, :1])]
    offs = [0]
    for p in packed:
        offs.append(offs[-1] + p.shape[0])
    reduced = allreduce_small(jnp.concatenate(packed, axis=0))
    grads = {}
    for t, k in enumerate(small_names):
        nelem = small_grads[k].size
        grads[k] = reduced[offs[t]:offs[t + 1]].reshape(-1)[:nelem].reshape(weights[k].shape)
    loss = reduced[offs[len(small_names)], 0]

    last = [("a_w_in", 0)]
    for kind, name in (("ex", "exchange_last"), ("sc", "scatter_last")):
        comm, done = make_comm(kind, last)
        done(run_comm(comm, name))
    bufs = {k: None for k in BIG}
    for i in range(DEPTH):
        for kl, m_ in zip(layer_tensors(i), grad_metas(layer_tensors(i))):
            bufs[kl[0]] = final_sum(me, own_parts[kl], recv_parts[kl], bufs[kl[0]], weights[kl[0]].shape, kl[1], m_,
                                    "final_sum_%s_l%d" % kl)
    shared = share_final([bufs[k] for k in BIG])
    for k, gfull in zip(BIG, shared):
        grads[k] = gfull

    delta, new_m, new_v = {}, {}, {}
    for k in order:
        shp = weights[k].shape
        if k in BIG:
            view = shp
        elif k == "a_w_s":
            view = (2, SGU_G * SGU_BLOCK, SGU_BLOCK)
        elif len(shp) == 1:
            view = (1, 1, shp[0])
        elif len(shp) == 2:
            view = (1,) + shp
        else:
            view = shp
        g_, d_, m_, v_ = adamw(weights[k].reshape(view), grads[k].reshape(view), mom_m[k].reshape(view),
                               mom_v[k].reshape(view), "adamw_" + k)
        grads[k], delta[k], new_m[k], new_v[k] = g_.reshape(shp), d_.reshape(shp), m_.reshape(shp), v_.reshape(shp)
    for k in transposed:
        for tree in (grads, delta, new_m, new_v):
            tree[k] = jnp.swapaxes(tree[k], 1, 2)

    return (loss, dx.reshape(1, T, D), *[grads[k] for k in order], *[delta[k] for k in order],
            *[new_m[k] for k in order], *[new_v[k] for k in order])
```
